```python
import math
import jax, jax.numpy as jnp
from jax import lax
import numpy as np

D_MODEL = 1024
BATCH = 8
SEQ = 8192
DEPTH = 2

N_A_LAYERS = DEPTH // 2
N_B_LAYERS = DEPTH - N_A_LAYERS
D_FF = 2816
FFN_HALF = 0.5
SSM_EXPAND = 2
SSM_D_INNER = SSM_EXPAND * D_MODEL
SSM_HEAD_DIM = 64
SSM_HEADS = SSM_D_INNER // SSM_HEAD_DIM
SSM_GROUPS = 4
SSM_STATE = 128
SSM_CONV = 4
SSM_CHUNK = 256
SSM_CONV_DIM = SSM_D_INNER + 2 * SSM_GROUPS * SSM_STATE
SSM_IN_DIM = SSM_D_INNER + SSM_CONV_DIM + SSM_HEADS
ATT_HEAD_DIM = 64
ATT_HEADS = D_MODEL // ATT_HEAD_DIM
ATT_KV_HEADS = ATT_HEADS // 8
ATT_GROUP = ATT_HEADS // ATT_KV_HEADS
ATT_WINDOW = 128
ATT_BLOCK = ATT_WINDOW
REL_BUCKETS = 32
REL_MAX_DIST = ATT_WINDOW
EPS = 1e-6

kernel_name = 'yoco_mamba2_swa_sink_macaron'


def rmsnorm(x, g):
    xf = x.astype(jnp.float32)
    y = xf * lax.rsqrt(jnp.mean(xf * xf, axis=-1, keepdims=True) + EPS)
    return (y * g.astype(jnp.float32)).astype(x.dtype)


def swiglu_half_step(h, g, w1, w3, w2):
    u = rmsnorm(h, g)
    return h + FFN_HALF * ((jax.nn.silu(u @ w1) * (u @ w3)) @ w2)


def causal_depthwise_conv(x, w, b):
    c = x.shape[-1]
    y = lax.conv_general_dilated(
        x, w.astype(x.dtype)[:, None, :], window_strides=(1,),
        padding=[(SSM_CONV - 1, 0)], dimension_numbers=('NWC', 'WIO', 'NWC'),
        feature_group_count=c)
    return y + b.astype(x.dtype)


def ssd_chunked_scan(x, dt, a, b_in, c_in):
    bsz, t, h, p = x.shape
    g, n = b_in.shape[2], b_in.shape[3]
    r = h // g
    L = SSM_CHUNK
    nc = -(-t // L)
    pad = nc * L - t
    f32 = jnp.float32

    def chunks(z):
        z = z.astype(f32)
        z = jnp.pad(z, [(0, 0), (0, pad)] + [(0, 0)] * (z.ndim - 2))
        z = z.reshape((bsz, nc, L) + z.shape[2:])
        return jnp.moveaxis(z, 1, 0)

    xc = chunks(x.reshape(bsz, t, g, r, p))
    dtc = chunks(dt.reshape(bsz, t, g, r))
    bc = chunks(b_in)
    cc = chunks(c_in)
    a_gr = a.astype(f32).reshape(g, r)
    causal = jnp.tril(jnp.ones((L, L), bool))[:, :, None, None]

    def step(state, inp):
        xk, dtk, bk, ck = inp
        acum = jnp.cumsum(dtk * a_gr, axis=1)
        seg = acum[:, :, None] - acum[:, None, :]
        decay = jnp.exp(jnp.where(causal, seg, -jnp.inf))
        cb = jnp.einsum('blgn,bsgn->blsg', ck, bk)
        scores = cb[..., None] * decay
        y_diag = jnp.einsum('blsgr,bsgr,bsgrp->blgrp', scores, dtk, xk)
        y_off = jnp.einsum('blgn,bgrpn,blgr->blgrp', ck, state, jnp.exp(acum))
        w_end = jnp.exp(acum[:, -1:] - acum) * dtk
        state = (state * jnp.exp(acum[:, -1])[..., None, None]
                 + jnp.einsum('bsgn,bsgr,bsgrp->bgrpn', bk, w_end, xk))
        return state, y_diag + y_off

    state0 = jnp.zeros((bsz, g, r, p, n), f32)
    _, y = lax.scan(step, state0, (xc, dtc, bc, cc))
    y = jnp.moveaxis(y, 0, 1).reshape(bsz, nc * L, h, p)[:, :t]
    return y.astype(x.dtype)


def mamba2_mixer(u, w_in, conv_w, conv_b, dt_bias, a_log, d_skip, gate_norm, w_out):
    bsz, t, _ = u.shape
    zxbcdt = u @ w_in
    z, xbc, dt = jnp.split(zxbcdt, [SSM_D_INNER, SSM_D_INNER + SSM_CONV_DIM], axis=-1)
    xbc = jax.nn.silu(causal_depthwise_conv(xbc, conv_w, conv_b))
    xs, b_in, c_in = jnp.split(xbc, [SSM_D_INNER, SSM_D_INNER + SSM_GROUPS * SSM_STATE], axis=-1)
    xs = xs.reshape(bsz, t, SSM_HEADS, SSM_HEAD_DIM)
    b_in = b_in.reshape(bsz, t, SSM_GROUPS, SSM_STATE)
    c_in = c_in.reshape(bsz, t, SSM_GROUPS, SSM_STATE)
    dt = jax.nn.softplus((dt + dt_bias).astype(jnp.float32))
    a = -jnp.exp(a_log.astype(jnp.float32))
    y = ssd_chunked_scan(xs, dt, a, b_in, c_in) + d_skip[:, None].astype(xs.dtype) * xs
    y = y.reshape(bsz, t, SSM_D_INNER) * jax.nn.silu(z)
    y = rmsnorm(y.reshape(bsz, t, SSM_GROUPS, SSM_D_INNER // SSM_GROUPS),
                gate_norm.reshape(SSM_GROUPS, SSM_D_INNER // SSM_GROUPS))
    return y.reshape(bsz, t, SSM_D_INNER) @ w_out


def shared_kv(h, kv_norm, w_kv, k_norm):
    bsz, t, _ = h.shape
    kv = rmsnorm(h, kv_norm) @ w_kv
    k, v = jnp.split(kv, 2, axis=-1)
    k = rmsnorm(k.reshape(bsz, t, ATT_KV_HEADS, ATT_HEAD_DIM), k_norm)
    v = v.reshape(bsz, t, ATT_KV_HEADS, ATT_HEAD_DIM)
    return k, v


def t5_bucket(dist):
    n = jnp.maximum(dist, 0)
    max_exact = REL_BUCKETS // 2
    nf = jnp.maximum(n, 1).astype(jnp.float32)
    large = max_exact + (jnp.log(nf / max_exact) / math.log(REL_MAX_DIST / max_exact)
                         * (REL_BUCKETS - max_exact)).astype(jnp.int32)
    large = jnp.minimum(large, REL_BUCKETS - 1)
    return jnp.where(n < max_exact, n, large)


def sliding_window_attention(u, k, v, w_q, q_norm, sinks, rel_bias, w_o):
    bsz, t, _ = u.shape
    blk = ATT_BLOCK
    nb = t // blk
    q = rmsnorm((u @ w_q).reshape(bsz, t, ATT_KV_HEADS, ATT_GROUP, ATT_HEAD_DIM), q_norm)
    qb = jnp.moveaxis(q.reshape(bsz, nb, blk, ATT_KV_HEADS, ATT_GROUP, ATT_HEAD_DIM), 1, 0)

    def band(z):
        prev = jnp.pad(z, [(0, 0), (blk, 0), (0, 0), (0, 0)])[:, :t]
        zz = jnp.concatenate([prev.reshape(bsz, nb, blk, ATT_KV_HEADS, ATT_HEAD_DIM),
                              z.reshape(bsz, nb, blk, ATT_KV_HEADS, ATT_HEAD_DIM)], axis=2)
        return jnp.moveaxis(zz, 1, 0)

    kb, vb = band(k), band(v)
    qi = jnp.arange(blk)[:, None] + blk
    kj = jnp.arange(2 * blk)[None, :]
    dist = qi - kj
    in_window = (dist >= 0) & (dist < ATT_WINDOW)
    bias = rel_bias[t5_bucket(dist)]
    bias = jnp.transpose(bias.reshape(blk, 2 * blk, ATT_KV_HEADS, ATT_GROUP),
                         (2, 3, 0, 1)).astype(jnp.float32)
    sink = sinks.reshape(ATT_KV_HEADS, ATT_GROUP)[None, :, :, None, None].astype(jnp.float32)
    scale = ATT_HEAD_DIM ** -0.5

    def block(args):
        qk, kk, vk, bi = args
        s = jnp.einsum('bqkrd,bskd->bkrqs', qk, kk).astype(jnp.float32) * scale + bias
        valid = in_window & ((bi > 0) | (kj >= blk))
        s = jnp.where(valid, s, -jnp.inf)
        m = jnp.maximum(jnp.max(s, axis=-1, keepdims=True), sink)
        p = jnp.exp(s - m)
        denom = jnp.sum(p, axis=-1, keepdims=True) + jnp.exp(sink - m)
        return jnp.einsum('bkrqs,bskd->bqkrd', (p / denom).astype(vk.dtype), vk)

    o = lax.map(block, (qb, kb, vb, jnp.arange(nb)))
    o = jnp.moveaxis(o, 0, 1).reshape(bsz, t, ATT_HEADS * ATT_HEAD_DIM)
    return o @ w_o


def _fwd_setup_inputs(seed: int = 0) -> dict:
    key = jax.random.key(seed)
    ks = jax.random.split(key, 24)
    nrm = jax.random.normal
    f32 = jnp.float32
    x = nrm(ks[0], (BATCH, SEQ, D_MODEL), f32)
    ffn_norm = 1.0 + 0.05 * nrm(ks[1], (DEPTH, 2, D_MODEL), f32)
    ffn_w1 = nrm(ks[2], (DEPTH, 2, D_MODEL, D_FF), f32) * D_MODEL ** -0.5
    ffn_w3 = nrm(ks[3], (DEPTH, 2, D_MODEL, D_FF), f32) * D_MODEL ** -0.5
    ffn_w2 = nrm(ks[4], (DEPTH, 2, D_FF, D_MODEL), f32) * D_FF ** -0.5
    ssm_norm = 1.0 + 0.05 * nrm(ks[5], (N_A_LAYERS, D_MODEL), f32)
    ssm_w_in = nrm(ks[6], (N_A_LAYERS, D_MODEL, SSM_IN_DIM), f32) * D_MODEL ** -0.5
    ssm_conv_w = nrm(ks[7], (N_A_LAYERS, SSM_CONV, SSM_CONV_DIM), f32) * SSM_CONV ** -0.5
    ssm_conv_b = 0.01 * nrm(ks[8], (N_A_LAYERS, SSM_CONV_DIM), f32)
    dt0 = jnp.exp(jax.random.uniform(ks[9], (N_A_LAYERS, SSM_HEADS), f32,
                                     math.log(1e-3), math.log(1e-1)))
    ssm_dt_bias = dt0 + jnp.log(-jnp.expm1(-dt0))
    ssm_a_log = jnp.log(jax.random.uniform(ks[10], (N_A_LAYERS, SSM_HEADS), f32, 1.0, 16.0))
    ssm_d = 1.0 + 0.1 * nrm(ks[11], (N_A_LAYERS, SSM_HEADS), f32)
    ssm_gate_norm = 1.0 + 0.05 * nrm(ks[12], (N_A_LAYERS, SSM_D_INNER), f32)
    ssm_w_out = nrm(ks[13], (N_A_LAYERS, SSM_D_INNER, D_MODEL), f32) * SSM_D_INNER ** -0.5
    kv_norm = 1.0 + 0.05 * nrm(ks[14], (D_MODEL,), f32)
    w_kv = nrm(ks[15], (D_MODEL, 2 * ATT_KV_HEADS * ATT_HEAD_DIM), f32) * D_MODEL ** -0.5
    k_norm = 1.0 + 0.05 * nrm(ks[16], (ATT_HEAD_DIM,), f32)
    attn_norm = 1.0 + 0.05 * nrm(ks[17], (N_B_LAYERS, D_MODEL), f32)
    w_q = nrm(ks[18], (N_B_LAYERS, D_MODEL, ATT_HEADS * ATT_HEAD_DIM), f32) * D_MODEL ** -0.5
    q_norm = 1.0 + 0.05 * nrm(ks[19], (N_B_LAYERS, ATT_HEAD_DIM), f32)
    sinks = 0.5 * nrm(ks[20], (N_B_LAYERS, ATT_HEADS), f32)
    w_o = nrm(ks[21], (N_B_LAYERS, ATT_HEADS * ATT_HEAD_DIM, D_MODEL), f32) * (ATT_HEADS * ATT_HEAD_DIM) ** -0.5
    rel_bias = 0.5 * nrm(ks[22], (REL_BUCKETS, ATT_HEADS), f32)
    return {'x': x, 'ffn_norm': ffn_norm, 'ffn_w1': ffn_w1, 'ffn_w3': ffn_w3, 'ffn_w2': ffn_w2,
            'ssm_norm': ssm_norm, 'ssm_w_in': ssm_w_in, 'ssm_conv_w': ssm_conv_w,
            'ssm_conv_b': ssm_conv_b, 'ssm_dt_bias': ssm_dt_bias, 'ssm_a_log': ssm_a_log,
            'ssm_d': ssm_d, 'ssm_gate_norm': ssm_gate_norm, 'ssm_w_out': ssm_w_out,
            'kv_norm': kv_norm, 'w_kv': w_kv, 'k_norm': k_norm,
            'attn_norm': attn_norm, 'w_q': w_q, 'q_norm': q_norm, 'sinks': sinks, 'w_o': w_o,
            'rel_bias': rel_bias}


def _fwd_reference(x, ffn_norm, ffn_w1, ffn_w3, ffn_w2,
              ssm_norm, ssm_w_in, ssm_conv_w, ssm_conv_b, ssm_dt_bias, ssm_a_log,
              ssm_d, ssm_gate_norm, ssm_w_out,
              kv_norm, w_kv, k_norm,
              attn_norm, w_q, q_norm, sinks, w_o,
              rel_bias):
    h = x
    k_shared, v_shared = None, None
    for layer in range(DEPTH):
        h = swiglu_half_step(h, ffn_norm[layer, 0], ffn_w1[layer, 0], ffn_w3[layer, 0], ffn_w2[layer, 0])
        if layer < N_A_LAYERS:
            i = layer
            h = h + mamba2_mixer(rmsnorm(h, ssm_norm[i]), ssm_w_in[i], ssm_conv_w[i], ssm_conv_b[i],
                                 ssm_dt_bias[i], ssm_a_log[i], ssm_d[i], ssm_gate_norm[i], ssm_w_out[i])
        else:
            j = layer - N_A_LAYERS
            h = h + sliding_window_attention(rmsnorm(h, attn_norm[j]), k_shared, v_shared,
                                             w_q[j], q_norm[j], sinks[j], rel_bias, w_o[j])
        h = swiglu_half_step(h, ffn_norm[layer, 1], ffn_w1[layer, 1], ffn_w3[layer, 1], ffn_w2[layer, 1])
        if layer == N_A_LAYERS - 1:
            k_shared, v_shared = shared_kv(h, kv_norm, w_kv, k_norm)
    return h


import jax as _jax
import jax.numpy as _jnp

TWIN_FORMAT = 'train_step'
FWD_PARAMS = ['x', 'ffn_norm', 'ffn_w1', 'ffn_w3', 'ffn_w2', 'ssm_norm', 'ssm_w_in', 'ssm_conv_w', 'ssm_conv_b', 'ssm_dt_bias', 'ssm_a_log', 'ssm_d', 'ssm_gate_norm', 'ssm_w_out', 'kv_norm', 'w_kv', 'k_norm', 'attn_norm', 'w_q', 'q_norm', 'sinks', 'w_o', 'rel_bias']
TWIN_WEIGHTS = ['ffn_norm', 'ffn_w1', 'ffn_w3', 'ffn_w2', 'ssm_norm', 'ssm_w_in', 'ssm_conv_w', 'ssm_conv_b', 'ssm_dt_bias', 'ssm_a_log', 'ssm_d', 'ssm_gate_norm', 'ssm_w_out', 'kv_norm', 'w_kv', 'k_norm', 'attn_norm', 'w_q', 'q_norm', 'sinks', 'w_o', 'rel_bias']
TWIN_DIFF_INPUT = 'x'
TWIN_INPUTS = ['x', 'ffn_norm', 'ffn_w1', 'ffn_w3', 'ffn_w2', 'ssm_norm', 'ssm_w_in', 'ssm_conv_w', 'ssm_conv_b', 'ssm_dt_bias', 'ssm_a_log', 'ssm_d', 'ssm_gate_norm', 'ssm_w_out', 'kv_norm', 'w_kv', 'k_norm', 'attn_norm', 'w_q', 'q_norm', 'sinks', 'w_o', 'rel_bias', 'loss_target', 'm_ffn_norm', 'm_ffn_w1', 'm_ffn_w3', 'm_ffn_w2', 'm_ssm_norm', 'm_ssm_w_in', 'm_ssm_conv_w', 'm_ssm_conv_b', 'm_ssm_dt_bias', 'm_ssm_a_log', 'm_ssm_d', 'm_ssm_gate_norm', 'm_ssm_w_out', 'm_kv_norm', 'm_w_kv', 'm_k_norm', 'm_attn_norm', 'm_w_q', 'm_q_norm', 'm_sinks', 'm_w_o', 'm_rel_bias', 'v_ffn_norm', 'v_ffn_w1', 'v_ffn_w3', 'v_ffn_w2', 'v_ssm_norm', 'v_ssm_w_in', 'v_ssm_conv_w', 'v_ssm_conv_b', 'v_ssm_dt_bias', 'v_ssm_a_log', 'v_ssm_d', 'v_ssm_gate_norm', 'v_ssm_w_out', 'v_kv_norm', 'v_w_kv', 'v_k_norm', 'v_attn_norm', 'v_w_q', 'v_q_norm', 'v_sinks', 'v_w_o', 'v_rel_bias']
TWIN_OUTPUTS = ['loss', 'grad_x', 'grad_ffn_norm', 'grad_ffn_w1', 'grad_ffn_w3', 'grad_ffn_w2', 'grad_ssm_norm', 'grad_ssm_w_in', 'grad_ssm_conv_w', 'grad_ssm_conv_b', 'grad_ssm_dt_bias', 'grad_ssm_a_log', 'grad_ssm_d', 'grad_ssm_gate_norm', 'grad_ssm_w_out', 'grad_kv_norm', 'grad_w_kv', 'grad_k_norm', 'grad_attn_norm', 'grad_w_q', 'grad_q_norm', 'grad_sinks', 'grad_w_o', 'grad_rel_bias', 'delta_ffn_norm', 'delta_ffn_w1', 'delta_ffn_w3', 'delta_ffn_w2', 'delta_ssm_norm', 'delta_ssm_w_in', 'delta_ssm_conv_w', 'delta_ssm_conv_b', 'delta_ssm_dt_bias', 'delta_ssm_a_log', 'delta_ssm_d', 'delta_ssm_gate_norm', 'delta_ssm_w_out', 'delta_kv_norm', 'delta_w_kv', 'delta_k_norm', 'delta_attn_norm', 'delta_w_q', 'delta_q_norm', 'delta_sinks', 'delta_w_o', 'delta_rel_bias', 'new_m_ffn_norm', 'new_m_ffn_w1', 'new_m_ffn_w3', 'new_m_ffn_w2', 'new_m_ssm_norm', 'new_m_ssm_w_in', 'new_m_ssm_conv_w', 'new_m_ssm_conv_b', 'new_m_ssm_dt_bias', 'new_m_ssm_a_log', 'new_m_ssm_d', 'new_m_ssm_gate_norm', 'new_m_ssm_w_out', 'new_m_kv_norm', 'new_m_w_kv', 'new_m_k_norm', 'new_m_attn_norm', 'new_m_w_q', 'new_m_q_norm', 'new_m_sinks', 'new_m_w_o', 'new_m_rel_bias', 'new_v_ffn_norm', 'new_v_ffn_w1', 'new_v_ffn_w3', 'new_v_ffn_w2', 'new_v_ssm_norm', 'new_v_ssm_w_in', 'new_v_ssm_conv_w', 'new_v_ssm_conv_b', 'new_v_ssm_dt_bias', 'new_v_ssm_a_log', 'new_v_ssm_d', 'new_v_ssm_gate_norm', 'new_v_ssm_w_out', 'new_v_kv_norm', 'new_v_w_kv', 'new_v_k_norm', 'new_v_attn_norm', 'new_v_w_q', 'new_v_q_norm', 'new_v_sinks', 'new_v_w_o', 'new_v_rel_bias']
TWIN_LEAF_KINDS = {'loss': 'loss', 'grad_x': 'grad_x', 'grad_ffn_norm': 'grad_w', 'grad_ffn_w1': 'grad_w', 'grad_ffn_w3': 'grad_w', 'grad_ffn_w2': 'grad_w', 'grad_ssm_norm': 'grad_w', 'grad_ssm_w_in': 'grad_w', 'grad_ssm_conv_w': 'grad_w', 'grad_ssm_conv_b': 'grad_w', 'grad_ssm_dt_bias': 'grad_w', 'grad_ssm_a_log': 'grad_w', 'grad_ssm_d': 'grad_w', 'grad_ssm_gate_norm': 'grad_w', 'grad_ssm_w_out': 'grad_w', 'grad_kv_norm': 'grad_w', 'grad_w_kv': 'grad_w', 'grad_k_norm': 'grad_w', 'grad_attn_norm': 'grad_w', 'grad_w_q': 'grad_w', 'grad_q_norm': 'grad_w', 'grad_sinks': 'grad_w', 'grad_w_o': 'grad_w', 'grad_rel_bias': 'grad_w', 'delta_ffn_norm': 'delta_w', 'delta_ffn_w1': 'delta_w', 'delta_ffn_w3': 'delta_w', 'delta_ffn_w2': 'delta_w', 'delta_ssm_norm': 'delta_w', 'delta_ssm_w_in': 'delta_w', 'delta_ssm_conv_w': 'delta_w', 'delta_ssm_conv_b': 'delta_w', 'delta_ssm_dt_bias': 'delta_w', 'delta_ssm_a_log': 'delta_w', 'delta_ssm_d': 'delta_w', 'delta_ssm_gate_norm': 'delta_w', 'delta_ssm_w_out': 'delta_w', 'delta_kv_norm': 'delta_w', 'delta_w_kv': 'delta_w', 'delta_k_norm': 'delta_w', 'delta_attn_norm': 'delta_w', 'delta_w_q': 'delta_w', 'delta_q_norm': 'delta_w', 'delta_sinks': 'delta_w', 'delta_w_o': 'delta_w', 'delta_rel_bias': 'delta_w', 'new_m_ffn_norm': 'new_m', 'new_m_ffn_w1': 'new_m', 'new_m_ffn_w3': 'new_m', 'new_m_ffn_w2': 'new_m', 'new_m_ssm_norm': 'new_m', 'new_m_ssm_w_in': 'new_m', 'new_m_ssm_conv_w': 'new_m', 'new_m_ssm_conv_b': 'new_m', 'new_m_ssm_dt_bias': 'new_m', 'new_m_ssm_a_log': 'new_m', 'new_m_ssm_d': 'new_m', 'new_m_ssm_gate_norm': 'new_m', 'new_m_ssm_w_out': 'new_m', 'new_m_kv_norm': 'new_m', 'new_m_w_kv': 'new_m', 'new_m_k_norm': 'new_m', 'new_m_attn_norm': 'new_m', 'new_m_w_q': 'new_m', 'new_m_q_norm': 'new_m', 'new_m_sinks': 'new_m', 'new_m_w_o': 'new_m', 'new_m_rel_bias': 'new_m', 'new_v_ffn_norm': 'new_v', 'new_v_ffn_w1': 'new_v', 'new_v_ffn_w3': 'new_v', 'new_v_ffn_w2': 'new_v', 'new_v_ssm_norm': 'new_v', 'new_v_ssm_w_in': 'new_v', 'new_v_ssm_conv_w': 'new_v', 'new_v_ssm_conv_b': 'new_v', 'new_v_ssm_dt_bias': 'new_v', 'new_v_ssm_a_log': 'new_v', 'new_v_ssm_d': 'new_v', 'new_v_ssm_gate_norm': 'new_v', 'new_v_ssm_w_out': 'new_v', 'new_v_kv_norm': 'new_v', 'new_v_w_kv': 'new_v', 'new_v_k_norm': 'new_v', 'new_v_attn_norm': 'new_v', 'new_v_w_q': 'new_v', 'new_v_q_norm': 'new_v', 'new_v_sinks': 'new_v', 'new_v_w_o': 'new_v', 'new_v_rel_bias': 'new_v'}


def _forward(args):
    return _fwd_reference(*[args[k] for k in FWD_PARAMS])


def _output_shape():
    def fwd():
        inp = _fwd_setup_inputs(0)
        return _fwd_reference(*[inp[k] for k in FWD_PARAMS])
    out = _jax.eval_shape(fwd)
    return out.shape, out.dtype

N_MICROBATCH = 1
ADAM_LR = 0.001
ADAM_B1 = 0.9
ADAM_B2 = 0.999
ADAM_EPS = 1e-08
ADAM_WD = 0.01
ADAM_STEP = 10
PER_EXAMPLE_BATCH_AXIS = {'x': 0, 'loss_target': 0}
SHARED_INPUTS = []
_WEIGHT_DTYPES = {'ffn_norm': _jnp.float32, 'ffn_w1': _jnp.float32, 'ffn_w3': _jnp.float32, 'ffn_w2': _jnp.float32, 'ssm_norm': _jnp.float32, 'ssm_w_in': _jnp.float32, 'ssm_conv_w': _jnp.float32, 'ssm_conv_b': _jnp.float32, 'ssm_dt_bias': _jnp.float32, 'ssm_a_log': _jnp.float32, 'ssm_d': _jnp.float32, 'ssm_gate_norm': _jnp.float32, 'ssm_w_out': _jnp.float32, 'kv_norm': _jnp.float32, 'w_kv': _jnp.float32, 'k_norm': _jnp.float32, 'attn_norm': _jnp.float32, 'w_q': _jnp.float32, 'q_norm': _jnp.float32, 'sinks': _jnp.float32, 'w_o': _jnp.float32, 'rel_bias': _jnp.float32}
MOMENT_SCALE = {'ffn_norm': 1.238448e+01, 'ffn_w1': 1.903698e-01, 'ffn_w3': 2.095339e-01, 'ffn_w2': 3.430001e-01, 'ssm_norm': 1.410017e+00, 'ssm_w_in': 4.703475e-01, 'ssm_conv_w': 2.285081e+00, 'ssm_conv_b': 6.773200e+00, 'ssm_dt_bias': 3.462561e+00, 'ssm_a_log': 1.171645e+01, 'ssm_d': 2.367439e+01, 'ssm_gate_norm': 4.384322e+01, 'ssm_w_out': 5.071312e+00, 'kv_norm': 3.583218e+00, 'w_kv': 5.991719e+00, 'k_norm': 1.460293e+01, 'attn_norm': 1.086145e-01, 'w_q': 1.077970e-01, 'q_norm': 1.459515e+01, 'sinks': 1.636113e+00, 'w_o': 1.668711e+00, 'rel_bias': 1.994142e+00}


def _to_microbatches(a, axis):
    t = _jnp.moveaxis(a, axis, 0)
    t = t.reshape((N_MICROBATCH, t.shape[0] // N_MICROBATCH) + t.shape[1:])
    return _jnp.moveaxis(t, 1, axis + 1)


def setup_inputs(seed: int = 0) -> dict:
    inp = _fwd_setup_inputs(seed)
    key = _jax.random.fold_in(_jax.random.key(seed), 7919)
    shape, _ = _output_shape()
    out = dict(inp)
    out["loss_target"] = _jax.random.normal(_jax.random.fold_in(key, 0), shape, _jnp.float32)
    for i, name in enumerate(TWIN_WEIGHTS):
        w = inp[name].astype(_jnp.float32)
        if MOMENT_SCALE is None:
            s = _jnp.sqrt(_jnp.mean(_jnp.square(w)) + 1e-30)
        else:
            s = MOMENT_SCALE[name]
        km, kv = _jax.random.split(_jax.random.fold_in(key, i + 1))
        out[name] = w
        out["m_" + name] = s * _jax.random.normal(km, w.shape, _jnp.float32)
        out["v_" + name] = (s * s) * _jax.random.uniform(kv, w.shape, _jnp.float32, 0.5, 1.5)
    if N_MICROBATCH > 1:
        for name, axis in PER_EXAMPLE_BATCH_AXIS.items():
            out[name] = _to_microbatches(out[name], axis)
    return {'x': out['x'], 'ffn_norm': out['ffn_norm'], 'ffn_w1': out['ffn_w1'], 'ffn_w3': out['ffn_w3'], 'ffn_w2': out['ffn_w2'], 'ssm_norm': out['ssm_norm'], 'ssm_w_in': out['ssm_w_in'], 'ssm_conv_w': out['ssm_conv_w'], 'ssm_conv_b': out['ssm_conv_b'], 'ssm_dt_bias': out['ssm_dt_bias'], 'ssm_a_log': out['ssm_a_log'], 'ssm_d': out['ssm_d'], 'ssm_gate_norm': out['ssm_gate_norm'], 'ssm_w_out': out['ssm_w_out'], 'kv_norm': out['kv_norm'], 'w_kv': out['w_kv'], 'k_norm': out['k_norm'], 'attn_norm': out['attn_norm'], 'w_q': out['w_q'], 'q_norm': out['q_norm'], 'sinks': out['sinks'], 'w_o': out['w_o'], 'rel_bias': out['rel_bias'], 'loss_target': out['loss_target'], 'm_ffn_norm': out['m_ffn_norm'], 'm_ffn_w1': out['m_ffn_w1'], 'm_ffn_w3': out['m_ffn_w3'], 'm_ffn_w2': out['m_ffn_w2'], 'm_ssm_norm': out['m_ssm_norm'], 'm_ssm_w_in': out['m_ssm_w_in'], 'm_ssm_conv_w': out['m_ssm_conv_w'], 'm_ssm_conv_b': out['m_ssm_conv_b'], 'm_ssm_dt_bias': out['m_ssm_dt_bias'], 'm_ssm_a_log': out['m_ssm_a_log'], 'm_ssm_d': out['m_ssm_d'], 'm_ssm_gate_norm': out['m_ssm_gate_norm'], 'm_ssm_w_out': out['m_ssm_w_out'], 'm_kv_norm': out['m_kv_norm'], 'm_w_kv': out['m_w_kv'], 'm_k_norm': out['m_k_norm'], 'm_attn_norm': out['m_attn_norm'], 'm_w_q': out['m_w_q'], 'm_q_norm': out['m_q_norm'], 'm_sinks': out['m_sinks'], 'm_w_o': out['m_w_o'], 'm_rel_bias': out['m_rel_bias'], 'v_ffn_norm': out['v_ffn_norm'], 'v_ffn_w1': out['v_ffn_w1'], 'v_ffn_w3': out['v_ffn_w3'], 'v_ffn_w2': out['v_ffn_w2'], 'v_ssm_norm': out['v_ssm_norm'], 'v_ssm_w_in': out['v_ssm_w_in'], 'v_ssm_conv_w': out['v_ssm_conv_w'], 'v_ssm_conv_b': out['v_ssm_conv_b'], 'v_ssm_dt_bias': out['v_ssm_dt_bias'], 'v_ssm_a_log': out['v_ssm_a_log'], 'v_ssm_d': out['v_ssm_d'], 'v_ssm_gate_norm': out['v_ssm_gate_norm'], 'v_ssm_w_out': out['v_ssm_w_out'], 'v_kv_norm': out['v_kv_norm'], 'v_w_kv': out['v_w_kv'], 'v_k_norm': out['v_k_norm'], 'v_attn_norm': out['v_attn_norm'], 'v_w_q': out['v_w_q'], 'v_q_norm': out['v_q_norm'], 'v_sinks': out['v_sinks'], 'v_w_o': out['v_w_o'], 'v_rel_bias': out['v_rel_bias']}


def _loss(weights, diff, rest, loss_target):
    with _jax.named_scope("forward"):
        args = {**rest, TWIN_DIFF_INPUT: diff, **{k: w.astype(_WEIGHT_DTYPES[k]) for k, w in weights.items()}}
        y = _forward(args)
    with _jax.named_scope("loss_head"):
        err = _jnp.square(y.astype(_jnp.float32) - loss_target)
        return 0.5 * _jnp.sum(_jnp.mean(err, axis=-1)) if err.ndim else 0.5 * err


def _adamw(w, g, m, v):
    m = ADAM_B1 * m + (1.0 - ADAM_B1) * g
    v = ADAM_B2 * v + (1.0 - ADAM_B2) * _jnp.square(g)
    m_hat = m / (1.0 - ADAM_B1 ** ADAM_STEP)
    v_hat = v / (1.0 - ADAM_B2 ** ADAM_STEP)
    delta = -ADAM_LR * (m_hat / (_jnp.sqrt(v_hat) + ADAM_EPS) + ADAM_WD * w)
    return delta, m, v


def reference(x, ffn_norm, ffn_w1, ffn_w3, ffn_w2, ssm_norm, ssm_w_in, ssm_conv_w, ssm_conv_b, ssm_dt_bias, ssm_a_log, ssm_d, ssm_gate_norm, ssm_w_out, kv_norm, w_kv, k_norm, attn_norm, w_q, q_norm, sinks, w_o, rel_bias, loss_target, m_ffn_norm, m_ffn_w1, m_ffn_w3, m_ffn_w2, m_ssm_norm, m_ssm_w_in, m_ssm_conv_w, m_ssm_conv_b, m_ssm_dt_bias, m_ssm_a_log, m_ssm_d, m_ssm_gate_norm, m_ssm_w_out, m_kv_norm, m_w_kv, m_k_norm, m_attn_norm, m_w_q, m_q_norm, m_sinks, m_w_o, m_rel_bias, v_ffn_norm, v_ffn_w1, v_ffn_w3, v_ffn_w2, v_ssm_norm, v_ssm_w_in, v_ssm_conv_w, v_ssm_conv_b, v_ssm_dt_bias, v_ssm_a_log, v_ssm_d, v_ssm_gate_norm, v_ssm_w_out, v_kv_norm, v_w_kv, v_k_norm, v_attn_norm, v_w_q, v_q_norm, v_sinks, v_w_o, v_rel_bias):
    given = dict(x=x, ffn_norm=ffn_norm, ffn_w1=ffn_w1, ffn_w3=ffn_w3, ffn_w2=ffn_w2, ssm_norm=ssm_norm, ssm_w_in=ssm_w_in, ssm_conv_w=ssm_conv_w, ssm_conv_b=ssm_conv_b, ssm_dt_bias=ssm_dt_bias, ssm_a_log=ssm_a_log, ssm_d=ssm_d, ssm_gate_norm=ssm_gate_norm, ssm_w_out=ssm_w_out, kv_norm=kv_norm, w_kv=w_kv, k_norm=k_norm, attn_norm=attn_norm, w_q=w_q, q_norm=q_norm, sinks=sinks, w_o=w_o, rel_bias=rel_bias, loss_target=loss_target, m_ffn_norm=m_ffn_norm, m_ffn_w1=m_ffn_w1, m_ffn_w3=m_ffn_w3, m_ffn_w2=m_ffn_w2, m_ssm_norm=m_ssm_norm, m_ssm_w_in=m_ssm_w_in, m_ssm_conv_w=m_ssm_conv_w, m_ssm_conv_b=m_ssm_conv_b, m_ssm_dt_bias=m_ssm_dt_bias, m_ssm_a_log=m_ssm_a_log, m_ssm_d=m_ssm_d, m_ssm_gate_norm=m_ssm_gate_norm, m_ssm_w_out=m_ssm_w_out, m_kv_norm=m_kv_norm, m_w_kv=m_w_kv, m_k_norm=m_k_norm, m_attn_norm=m_attn_norm, m_w_q=m_w_q, m_q_norm=m_q_norm, m_sinks=m_sinks, m_w_o=m_w_o, m_rel_bias=m_rel_bias, v_ffn_norm=v_ffn_norm, v_ffn_w1=v_ffn_w1, v_ffn_w3=v_ffn_w3, v_ffn_w2=v_ffn_w2, v_ssm_norm=v_ssm_norm, v_ssm_w_in=v_ssm_w_in, v_ssm_conv_w=v_ssm_conv_w, v_ssm_conv_b=v_ssm_conv_b, v_ssm_dt_bias=v_ssm_dt_bias, v_ssm_a_log=v_ssm_a_log, v_ssm_d=v_ssm_d, v_ssm_gate_norm=v_ssm_gate_norm, v_ssm_w_out=v_ssm_w_out, v_kv_norm=v_kv_norm, v_w_kv=v_w_kv, v_k_norm=v_k_norm, v_attn_norm=v_attn_norm, v_w_q=v_w_q, v_q_norm=v_q_norm, v_sinks=v_sinks, v_w_o=v_w_o, v_rel_bias=v_rel_bias)
    weights = {n: given[n] for n in TWIN_WEIGHTS}
    shared = {n: given[n] for n in SHARED_INPUTS}
    per_example = {n: given[n] for n in ['x']}
    grad_fn = _jax.value_and_grad(_loss, argnums=(0, 1))

    def one_microbatch(ex, loss_target):
        ex = dict(ex)
        diff = ex.pop(TWIN_DIFF_INPUT)
        return grad_fn(weights, diff, {**shared, **ex}, loss_target)

    if N_MICROBATCH == 1:
        loss, (grad_w, grad_x) = one_microbatch(per_example, given["loss_target"])
    else:
        def body(carry, xs):
            loss_sum, grad_sum = carry
            l_k, (gw_k, gx_k) = one_microbatch(xs[0], xs[1])
            with _jax.named_scope("update"):
                return (loss_sum + l_k, _jax.tree.map(_jnp.add, grad_sum, gw_k)), gx_k

        init = (_jnp.zeros((), _jnp.float32), _jax.tree.map(_jnp.zeros_like, weights))
        (loss, grad_w), grad_x = _jax.lax.scan(body, init, (per_example, given["loss_target"]))
    with _jax.named_scope("update"):
        delta_w, new_m, new_v = {}, {}, {}
        for n in TWIN_WEIGHTS:
            delta_w[n], new_m[n], new_v[n] = _adamw(weights[n], grad_w[n], given["m_" + n], given["v_" + n])
    return (loss, grad_x, *[grad_w[n] for n in TWIN_WEIGHTS], *[delta_w[n] for n in TWIN_WEIGHTS],
            *[new_m[n] for n in TWIN_WEIGHTS], *[new_v[n] for n in TWIN_WEIGHTS])
```

```python
import functools
import math

import jax
import jax.numpy as jnp
import numpy as np
from jax import lax
from jax.experimental import pallas as pl
from jax.experimental.pallas import tpu as pltpu

F32 = jnp.float32
BF16 = jnp.bfloat16
EPS = 1e-6
MESH = pl.DeviceIdType.MESH

SSM_HEAD_DIM = 64
SSM_GROUPS = 4
SSM_STATE = 128
SSM_CONV = 4
SSM_CHUNK = 256
ATT_HEAD_DIM = 64
ATT_WINDOW = 128
REL_BUCKETS = 32
N_CHIPS = 4

ADAM_LR = 0.001
ADAM_B1 = 0.9
ADAM_B2 = 0.999
ADAM_EPS = 1e-08
ADAM_WD = 0.01
ADAM_STEP = 10

VMEM_LIMIT_BYTES = 56 * 1024 * 1024
NEG = -1e30


def _params(*sem):
    return pltpu.CompilerParams(dimension_semantics=sem if sem else None, vmem_limit_bytes=VMEM_LIMIT_BYTES)


def _dot(a, b):
    return jnp.dot(a, b, preferred_element_type=F32)


def _dot_nt(a, b):
    return lax.dot_general(a, b, (((1,), (1,)), ((), ())), preferred_element_type=F32)


def _dot_tn(a, b):
    return lax.dot_general(a, b, (((0,), (0,)), ((), ())), preferred_element_type=F32)


def _b(x):
    return x.astype(BF16)


@jax.custom_vjp
def _bmm(a, b):
    return _dot(_b(a), _b(b))


def _bmm_fwd(a, b):
    return _bmm(a, b), (a, b)


def _bmm_bwd(res, g):
    a, b = res
    g16 = _b(g)
    return _dot_nt(g16, _b(b)).astype(a.dtype), _dot_tn(_b(a), g16).astype(b.dtype)


_bmm.defvjp(_bmm_fwd, _bmm_bwd)


@jax.custom_vjp
def _bmm_nt(a, b):
    return _dot_nt(_b(a), _b(b))


def _bmm_nt_fwd(a, b):
    return _bmm_nt(a, b), (a, b)


def _bmm_nt_bwd(res, g):
    a, b = res
    g16 = _b(g)
    return _dot(g16, _b(b)).astype(a.dtype), _dot_tn(g16, _b(a)).astype(b.dtype)


_bmm_nt.defvjp(_bmm_nt_fwd, _bmm_nt_bwd)


@jax.custom_vjp
def _bmm_tn(a, b):
    return _dot_tn(_b(a), _b(b))


def _bmm_tn_fwd(a, b):
    return _bmm_tn(a, b), (a, b)


def _bmm_tn_bwd(res, g):
    a, b = res
    g16 = _b(g)
    return _dot_nt(_b(b), g16).astype(a.dtype), _dot(_b(a), g16).astype(b.dtype)


_bmm_tn.defvjp(_bmm_tn_fwd, _bmm_tn_bwd)


def _split3(x):
    hi = _b(x)
    r = x - hi.astype(F32)
    mid = _b(r)
    lo = _b(r - mid.astype(F32))
    return hi, mid, lo


def _x_left_raw(m, x):
    hi, mid, lo = _split3(x)
    return _dot(m, hi) + _dot(m, mid) + _dot(m, lo)


def _x_left_t_raw(m, x):
    hi, mid, lo = _split3(x)
    return _dot_tn(m, hi) + _dot_tn(m, mid) + _dot_tn(m, lo)


def _x_right_raw(x, m):
    hi, mid, lo = _split3(x)
    return _dot(hi, m) + _dot(mid, m) + _dot(lo, m)


def _x_right_t_raw(x, m):
    hi, mid, lo = _split3(x)
    return _dot_nt(hi, m) + _dot_nt(mid, m) + _dot_nt(lo, m)


@jax.custom_vjp
def _xleft(m, x):
    return _x_left_raw(m, x)


_xleft.defvjp(lambda m, x: (_x_left_raw(m, x), m),
              lambda m, g: (jnp.zeros_like(m), _x_left_t_raw(m, g)))


@jax.custom_vjp
def _xright(x, m):
    return _x_right_raw(x, m)


_xright.defvjp(lambda x, m: (_x_right_raw(x, m), m),
               lambda m, g: (_x_right_t_raw(g, m), jnp.zeros_like(m)))


def _sigmoid(x):
    return 1.0 / (1.0 + jnp.exp(-x))


def _silu(x):
    return x * _sigmoid(x)


def _softplus(x):
    return jnp.maximum(x, 0.0) + jnp.log(1.0 + jnp.exp(-jnp.abs(x)))


def _rms(x):
    return x * lax.rsqrt(jnp.mean(x * x, axis=-1, keepdims=True) + EPS)


def _iota(shape, dim):
    return lax.broadcasted_iota(jnp.int32, shape, dim)


def _blockdiag64(n):
    return jnp.where(_iota((n, n), 0) // 64 == _iota((n, n), 1) // 64, 1.0, 0.0).astype(BF16)


def _group64_rms(x, seg_sum):
    ms = seg_sum(x * x) * (1.0 / 64.0)
    return x * lax.rsqrt(ms + EPS)


def _fold64(x):
    ax = x.ndim - 1
    w = x.shape[ax]
    lo = (_iota(x.shape, ax) % 128) < 64
    return x + jnp.where(lo, pltpu.roll(x, w - 64, ax), pltpu.roll(x, 64, ax))


def _tile(n, want):
    t = min(n, want)
    assert n % t == 0, (n, t)
    return t


def ffn_fwd(h, g, w1, w3, w2, layer, idx):
    t, d = h.shape
    nk, fs = w1.shape[0], w1.shape[-1]
    tm = _tile(t, 512)

    def body(h_ref, g_ref, w1_ref, w3_ref, w2_ref, o_ref, a_ref, b_ref, u_scr, acc):
        k = pl.program_id(1)

        @pl.when(k == 0)
        def _():
            u_scr[...] = _b(_rms(h_ref[...]) * g_ref[...])
            acc[...] = jnp.zeros_like(acc)

        u = u_scr[...]
        a = _dot(u, w1_ref[...])
        b = _dot(u, w3_ref[...])
        a_ref[...] = _b(a)
        b_ref[...] = _b(b)
        acc[...] += _dot(_b(_silu(a) * b), w2_ref[...])

        @pl.when(k == nk - 1)
        def _():
            o_ref[...] = h_ref[...] + 0.5 * acc[...]

    wspec = lambda r, c: pl.BlockSpec((None, None, None, r, c), lambda i, k: (k, layer, idx, 0, 0))
    return pl.pallas_call(
        body, name="ffn_fwd",
        grid=(t // tm, nk),
        in_specs=[pl.BlockSpec((tm, d), lambda i, k: (i, 0)), pl.BlockSpec((1, d), lambda i, k: (0, 0)),
                  wspec(d, fs), wspec(d, fs), wspec(fs, d)],
        out_specs=[pl.BlockSpec((tm, d), lambda i, k: (i, 0)),
                   pl.BlockSpec((None, tm, fs), lambda i, k: (k, i, 0)),
                   pl.BlockSpec((None, tm, fs), lambda i, k: (k, i, 0))],
        out_shape=[jax.ShapeDtypeStruct((t, d), F32), jax.ShapeDtypeStruct((nk, t, fs), BF16),
                   jax.ShapeDtypeStruct((nk, t, fs), BF16)],
        scratch_shapes=[pltpu.VMEM((tm, d), BF16), pltpu.VMEM((tm, d), F32)],
        compiler_params=_params("arbitrary", "arbitrary"),
    )(h, g, w1, w3, w2)


def ffn_bwd(h, dy, g, a_s, b_s, w1, w3, w2, layer, idx):
    t, d = h.shape
    nk, fs = w1.shape[0], w1.shape[-1]
    tm = _tile(t, 512)

    def body(h_ref, dy_ref, g_ref, a_ref, b_ref, w1_ref, w3_ref, w2_ref,
             dh_ref, u_ref, da_ref, db_ref, s_ref, dg_ref, dyh_scr, du_acc):
        i, k = pl.program_id(0), pl.program_id(1)

        @pl.when(k == 0)
        def _():
            dyh_scr[...] = _b(0.5 * dy_ref[...])
            du_acc[...] = jnp.zeros_like(du_acc)

        @pl.when((k == 0) & (i == 0))
        def _():
            dg_ref[...] = jnp.zeros_like(dg_ref)

        ds = _dot_nt(dyh_scr[...], w2_ref[...])
        a = a_ref[...].astype(F32)
        b = b_ref[...].astype(F32)
        sig = _sigmoid(a)
        sl = a * sig
        s_ref[...] = _b(sl * b)
        da = _b(ds * b * (sig * (1.0 + a * (1.0 - sig))))
        db = _b(ds * sl)
        da_ref[...] = da
        db_ref[...] = db
        du_acc[...] += _dot_nt(da, w1_ref[...]) + _dot_nt(db, w3_ref[...])

        @pl.when(k == nk - 1)
        def _():
            hh = h_ref[...]
            rstd = lax.rsqrt(jnp.mean(hh * hh, axis=-1, keepdims=True) + EPS)
            xh = hh * rstd
            gg = g_ref[...]
            u_ref[...] = _b(xh * gg)
            du = du_acc[...]
            dg_ref[...] += jnp.sum(du * xh, axis=0, keepdims=True)
            dxh = du * gg
            dh_ref[...] = dy_ref[...] + rstd * (dxh - xh * jnp.mean(dxh * xh, axis=-1, keepdims=True))

    wspec = lambda r, c: pl.BlockSpec((None, None, None, r, c), lambda i, k: (k, layer, idx, 0, 0))
    tok = pl.BlockSpec((tm, d), lambda i, k: (i, 0))
    hid = pl.BlockSpec((None, tm, fs), lambda i, k: (k, i, 0))
    return pl.pallas_call(
        body, name="ffn_bwd",
        grid=(t // tm, nk),
        in_specs=[tok, tok, pl.BlockSpec((1, d), lambda i, k: (0, 0)), hid, hid, wspec(d, fs), wspec(d, fs), wspec(fs, d)],
        out_specs=[tok, tok, hid, hid, hid, pl.BlockSpec((1, d), lambda i, k: (0, 0))],
        out_shape=[jax.ShapeDtypeStruct((t, d), F32), jax.ShapeDtypeStruct((t, d), BF16),
                   jax.ShapeDtypeStruct((nk, t, fs), BF16), jax.ShapeDtypeStruct((nk, t, fs), BF16),
                   jax.ShapeDtypeStruct((nk, t, fs), BF16), jax.ShapeDtypeStruct((1, d), F32)],
        scratch_shapes=[pltpu.VMEM((tm, d), BF16), pltpu.VMEM((tm, d), F32)],
        compiler_params=_params("arbitrary", "arbitrary"),
    )(h, dy, g, a_s, b_s, w1, w3, w2)


def wgrad_grouped_b(a, bs, scale=1.0):
    t, m = a.shape
    ng, _, n = bs.shape
    tk = _tile(t, 512)

    def body(a_ref, b_ref, o_ref):
        j = pl.program_id(1)

        @pl.when(j == 0)
        def _():
            o_ref[...] = jnp.zeros_like(o_ref)

        o_ref[...] += _dot_tn(_b(a_ref[...]), _b(b_ref[...]))

        if scale != 1.0:
            @pl.when(j == pl.num_programs(1) - 1)
            def _():
                o_ref[...] = o_ref[...] * scale

    return pl.pallas_call(
        body, name="wgrad_gb",
        grid=(ng, t // tk),
        in_specs=[pl.BlockSpec((tk, m), lambda k, j: (j, 0)), pl.BlockSpec((None, tk, n), lambda k, j: (k, j, 0))],
        out_specs=pl.BlockSpec((None, m, n), lambda k, j: (k, 0, 0)),
        out_shape=jax.ShapeDtypeStruct((ng, m, n), F32),
        compiler_params=_params("arbitrary", "arbitrary"),
    )(a, bs)


def wgrad_grouped_a(as_, b, scale=1.0):
    ng, t, m = as_.shape
    n = b.shape[1]
    tk = _tile(t, 512)

    def body(a_ref, b_ref, o_ref):
        j = pl.program_id(1)

        @pl.when(j == 0)
        def _():
            o_ref[...] = jnp.zeros_like(o_ref)

        o_ref[...] += _dot_tn(_b(a_ref[...]), _b(b_ref[...]))

        if scale != 1.0:
            @pl.when(j == pl.num_programs(1) - 1)
            def _():
                o_ref[...] = o_ref[...] * scale

    return pl.pallas_call(
        body, name="wgrad_ga",
        grid=(ng, t // tk),
        in_specs=[pl.BlockSpec((None, tk, m), lambda k, j: (k, j, 0)), pl.BlockSpec((tk, n), lambda k, j: (j, 0))],
        out_specs=pl.BlockSpec((None, m, n), lambda k, j: (k, 0, 0)),
        out_shape=jax.ShapeDtypeStruct((ng, m, n), F32),
        compiler_params=_params("arbitrary", "arbitrary"),
    )(as_, b)


def wgrad(a, b):
    t, m = a.shape
    n = b.shape[1]
    tk = _tile(t, 512)
    tn = _tile(n, 512)

    def body(a_ref, b_ref, o_ref):
        @pl.when(pl.program_id(1) == 0)
        def _():
            o_ref[...] = jnp.zeros_like(o_ref)

        o_ref[...] += _dot_tn(_b(a_ref[...]), _b(b_ref[...]))

    return pl.pallas_call(
        body, name="wgrad",
        grid=(n // tn, t // tk),
        in_specs=[pl.BlockSpec((tk, m), lambda c, j: (j, 0)), pl.BlockSpec((tk, tn), lambda c, j: (j, c))],
        out_specs=pl.BlockSpec((m, tn), lambda c, j: (0, c)),
        out_shape=jax.ShapeDtypeStruct((m, n), F32),
        compiler_params=_params("arbitrary", "arbitrary"),
    )(a, b)


def norm_mm(h, g, w):
    t, d = h.shape
    n = w.shape[1]
    tm = _tile(t, 512)
    tn = _tile(n, 512)

    def body(h_ref, g_ref, w_ref, o_ref, u_scr):
        @pl.when(pl.program_id(1) == 0)
        def _():
            u_scr[...] = _b(_rms(h_ref[...]) * g_ref[...])

        o_ref[...] = _dot(u_scr[...], w_ref[...])

    return pl.pallas_call(
        body, name="norm_mm",
        grid=(t // tm, n // tn),
        in_specs=[pl.BlockSpec((tm, d), lambda i, j: (i, 0)), pl.BlockSpec((1, d), lambda i, j: (0, 0)),
                  pl.BlockSpec((d, tn), lambda i, j: (0, j))],
        out_specs=pl.BlockSpec((tm, tn), lambda i, j: (i, j)),
        out_shape=jax.ShapeDtypeStruct((t, n), F32),
        scratch_shapes=[pltpu.VMEM((tm, d), BF16)],
        compiler_params=_params("arbitrary", "arbitrary"),
    )(h, g, w)


def norm_mm_bwd(h, g, w, dout, dres, scale=1.0):
    t, d = h.shape
    n = w.shape[1]
    tm = _tile(t, 512)
    tn = _tile(n, 512)
    nj = n // tn

    def body(h_ref, g_ref, w_ref, do_ref, dr_ref, dh_ref, u_ref, dg_ref, du_acc):
        i, j = pl.program_id(0), pl.program_id(1)

        @pl.when(j == 0)
        def _():
            du_acc[...] = jnp.zeros_like(du_acc)

        @pl.when((j == 0) & (i == 0))
        def _():
            dg_ref[...] = jnp.zeros_like(dg_ref)

        du_acc[...] += _dot_nt(_b(do_ref[...]), w_ref[...])

        @pl.when(j == nj - 1)
        def _():
            hh = h_ref[...]
            rstd = lax.rsqrt(jnp.mean(hh * hh, axis=-1, keepdims=True) + EPS)
            xh = hh * rstd
            gg = g_ref[...]
            u_ref[...] = _b(xh * gg)
            du = du_acc[...] * scale
            dg_ref[...] += jnp.sum(du * xh, axis=0, keepdims=True)
            dxh = du * gg
            dh_ref[...] = dr_ref[...] + rstd * (dxh - xh * jnp.mean(dxh * xh, axis=-1, keepdims=True))

    tok = pl.BlockSpec((tm, d), lambda i, j: (i, 0))
    return pl.pallas_call(
        body, name="norm_mm_bwd",
        grid=(t // tm, nj),
        in_specs=[tok, pl.BlockSpec((1, d), lambda i, j: (0, 0)), pl.BlockSpec((d, tn), lambda i, j: (0, j)),
                  pl.BlockSpec((tm, tn), lambda i, j: (i, j)), tok],
        out_specs=[tok, tok, pl.BlockSpec((1, d), lambda i, j: (0, 0))],
        out_shape=[jax.ShapeDtypeStruct((t, d), F32), jax.ShapeDtypeStruct((t, d), BF16),
                   jax.ShapeDtypeStruct((1, d), F32)],
        scratch_shapes=[pltpu.VMEM((tm, d), F32)],
        compiler_params=_params("arbitrary", "arbitrary"),
    )(h, g, w, dout, dres)


CONV_COLS = 512


def _conv_tile(ext, w, b):
    tm = ext.shape[0] - 8
    y = b + w[0:1] * ext[5:5 + tm] + w[1:2] * ext[6:6 + tm] + w[2:3] * ext[7:7 + tm] + w[3:4] * ext[8:8 + tm]
    return _silu(y)


def conv_fwd(zx, cw, cb, col0):
    t = zx.shape[0]
    c = cw.shape[1]
    tm = _tile(t, 512)
    cb0 = col0 // CONV_COLS

    def body(x_ref, w_ref, b_ref, o_ref, halo):
        @pl.when(pl.program_id(1) == 0)
        def _():
            halo[...] = jnp.zeros_like(halo)

        xt = x_ref[...]
        ext = jnp.concatenate([halo[...], xt], axis=0)
        o_ref[...] = _conv_tile(ext, w_ref[...], b_ref[...])
        halo[...] = xt[tm - 8:]

    return pl.pallas_call(
        body, name="conv_fwd",
        grid=(c // CONV_COLS, t // tm),
        in_specs=[pl.BlockSpec((tm, CONV_COLS), lambda j, i: (i, cb0 + j)),
                  pl.BlockSpec((8, CONV_COLS), lambda j, i: (0, j)), pl.BlockSpec((1, CONV_COLS), lambda j, i: (0, j))],
        out_specs=pl.BlockSpec((tm, CONV_COLS), lambda j, i: (i, j)),
        out_shape=jax.ShapeDtypeStruct((t, c), F32),
        scratch_shapes=[pltpu.VMEM((8, CONV_COLS), F32)],
        compiler_params=_params("arbitrary", "arbitrary"),
    )(zx, cw, cb)


def conv_bwd(dzx, zx, dxs, dbm, dcm, cw, cb, col0):
    t = zx.shape[0]
    c = cw.shape[1]
    tm = _tile(t, 512)
    nt = t // tm
    cb0 = col0 // CONV_COLS
    nxs = dxs.shape[1] // CONV_COLS
    hb = tm // 8

    def body(dzx_ref, x_ref, xh_ref, dxs_ref, db_ref, dc_ref, w_ref, b_ref, o_ref, dw_ref, dbias_ref, carry):
        j, i = pl.program_id(0), pl.program_id(1)
        ri = nt - 1 - i

        @pl.when(i == 0)
        def _():
            carry[...] = jnp.zeros_like(carry)
            dw_ref[...] = jnp.zeros_like(dw_ref)
            dbias_ref[...] = jnp.zeros_like(dbias_ref)

        dout = jnp.where(j < nxs, dxs_ref[...], jnp.where(j == nxs, db_ref[...], dc_ref[...]))
        hist = jnp.where(ri > 0, xh_ref[...], 0.0)
        ext = jnp.concatenate([hist, x_ref[...]], axis=0)
        _, vjp = jax.vjp(_conv_tile, ext, w_ref[...], b_ref[...])
        dext, dw, dbias = vjp(dout)
        tail = dext[tm:] + carry[...]
        o_ref[...] = jnp.concatenate([dext[8:tm], tail], axis=0)
        carry[...] = dext[0:8]
        dw_ref[...] += dw
        dbias_ref[...] += dbias

    return pl.pallas_call(
        body, name="conv_bwd",
        grid=(c // CONV_COLS, nt),
        in_specs=[pl.BlockSpec(memory_space=pl.ANY),
                  pl.BlockSpec((tm, CONV_COLS), lambda j, i: (nt - 1 - i, cb0 + j)),
                  pl.BlockSpec((8, CONV_COLS), lambda j, i: (jnp.maximum((nt - 1 - i) * hb - 1, 0), cb0 + j)),
                  pl.BlockSpec((tm, CONV_COLS), lambda j, i: (nt - 1 - i, jnp.minimum(j, nxs - 1))),
                  pl.BlockSpec((tm, CONV_COLS), lambda j, i: (nt - 1 - i, 0)),
                  pl.BlockSpec((tm, CONV_COLS), lambda j, i: (nt - 1 - i, 0)),
                  pl.BlockSpec((8, CONV_COLS), lambda j, i: (0, j)), pl.BlockSpec((1, CONV_COLS), lambda j, i: (0, j))],
        out_specs=[pl.BlockSpec((tm, CONV_COLS), lambda j, i: (nt - 1 - i, cb0 + j)),
                   pl.BlockSpec((8, CONV_COLS), lambda j, i: (0, j)), pl.BlockSpec((1, CONV_COLS), lambda j, i: (0, j))],
        out_shape=[jax.ShapeDtypeStruct(dzx.shape, F32), jax.ShapeDtypeStruct((8, c), F32),
                   jax.ShapeDtypeStruct((1, c), F32)],
        scratch_shapes=[pltpu.VMEM((8, CONV_COLS), F32)],
        input_output_aliases={0: 0},
        compiler_params=_params("arbitrary", "arbitrary"),
    )(dzx, zx, zx, dxs, dbm, dcm, cw, cb)


def _ssd_group(xs, bg, cg, dtraw, s0, bias, alog, dsk):
    L = xs.shape[0]
    causal = _iota((L, L), 0) >= _iota((L, L), 1)
    tril = jnp.where(causal, 1.0, 0.0).astype(BF16)
    dt = _softplus(dtraw + bias)
    a = -jnp.exp(alog)
    acum = _xleft(tril, dt * a)
    acum_t = acum.T
    dt_t = dt.T
    cb = _bmm_nt(cg, bg)
    lo = _iota((L, 128), 1) < 64
    lo_row = _iota((1, 128), 1) < 64
    lo_col = _iota((128, 1), 0) < 64
    alast = acum[L - 1:L, :]
    ys, s1s = [], []
    for q in range(4):
        xp = xs[:, q * 128:(q + 1) * 128]
        sp = s0[q * 128:(q + 1) * 128, :]
        yd, ec, wc, el = [], [], [], []
        for j in range(2):
            r = 2 * q + j
            ac = acum[:, r:r + 1]
            decay = jnp.exp(jnp.where(causal, ac - acum_t[r:r + 1, :], NEG))
            yd.append(_bmm(cb * decay * dt_t[r:r + 1, :], xp))
            ec.append(jnp.exp(ac))
            al = alast[:, r:r + 1]
            wc.append(jnp.exp(al - ac) * dt[:, r:r + 1])
            el.append(jnp.exp(al))
        y_off = _bmm_nt(cg, sp) * jnp.where(lo, ec[0], ec[1])
        dsel = jnp.where(lo_row, dsk[:, 2 * q:2 * q + 1], dsk[:, 2 * q + 1:2 * q + 2])
        ys.append(jnp.where(lo, yd[0], yd[1]) + y_off + dsel * xp)
        xw = xp * jnp.where(lo, wc[0], wc[1])
        s1s.append(sp * jnp.where(lo_col, el[0], el[1]) + _bmm_tn(xw, bg))
    return jnp.concatenate(ys, axis=1), jnp.concatenate(s1s, axis=0)


def ssd_fwd(xc, zx, bias, alog, dsk, dt_col0):
    t = xc.shape[0]
    L = _tile(t, SSM_CHUNK)
    nc = t // L
    g = SSM_GROUPS
    dtb = dt_col0 // 128

    def body(xs_ref, b_ref, c_ref, dt_ref, bias_ref, alog_ref, dsk_ref, y_ref, st_ref, state):
        c, gi = pl.program_id(0), pl.program_id(1)

        @pl.when(c == 0)
        def _():
            state[gi] = jnp.zeros((512, 128), F32)

        s0 = state[gi]
        st_ref[...] = s0
        y, s1 = _ssd_group(xs_ref[...], b_ref[...], c_ref[...], dt_ref[...], s0,
                           bias_ref[...], alog_ref[...], dsk_ref[...])
        y_ref[...] = y
        state[gi] = s1

    vec = pl.BlockSpec((1, 128), lambda c, gi: (0, gi))
    return pl.pallas_call(
        body, name="ssd_fwd",
        grid=(nc, g),
        in_specs=[pl.BlockSpec((L, 512), lambda c, gi: (c, gi)),
                  pl.BlockSpec((L, 128), lambda c, gi: (c, 16 + gi)),
                  pl.BlockSpec((L, 128), lambda c, gi: (c, 20 + gi)),
                  pl.BlockSpec((L, 128), lambda c, gi: (c, dtb + gi)), vec, vec, vec],
        out_specs=[pl.BlockSpec((L, 512), lambda c, gi: (c, gi)),
                   pl.BlockSpec((None, None, 512, 128), lambda c, gi: (c, gi, 0, 0))],
        out_shape=[jax.ShapeDtypeStruct((t, 2048), F32), jax.ShapeDtypeStruct((nc, g, 512, 128), F32)],
        scratch_shapes=[pltpu.VMEM((g, 512, 128), F32)],
        compiler_params=_params("arbitrary", "arbitrary"),
    )(xc, xc, xc, zx, bias, alog, dsk)


def ssd_bwd(dzx, dy, xc, zx, states, bias, alog, dsk, dt_col0):
    t = xc.shape[0]
    L = _tile(t, SSM_CHUNK)
    nc = t // L
    g = SSM_GROUPS
    dtb = dt_col0 // 128

    def body(dzx_ref, dy_ref, xs_ref, b_ref, c_ref, dt_ref, st_ref, bias_ref, alog_ref, dsk_ref,
             ddt_ref, dxs_ref, db_ref, dc_ref, dbias_ref, dalog_ref, ddsk_ref, dstate):
        c, gi = pl.program_id(0), pl.program_id(1)

        @pl.when(c == 0)
        def _():
            dstate[gi] = jnp.zeros((512, 128), F32)

        @pl.when((c == 0) & (gi == 0))
        def _():
            dbias_ref[...] = jnp.zeros_like(dbias_ref)
            dalog_ref[...] = jnp.zeros_like(dalog_ref)
            ddsk_ref[...] = jnp.zeros_like(ddsk_ref)

        _, vjp = jax.vjp(_ssd_group, xs_ref[...], b_ref[...], c_ref[...], dt_ref[...], st_ref[...],
                         bias_ref[...], alog_ref[...], dsk_ref[...])
        dxs, db, dc, ddt, ds0, dbias, dalog, ddsk = vjp((dy_ref[...], dstate[gi]))
        dxs_ref[...] = dxs
        db_ref[...] = db
        dc_ref[...] = dc
        ddt_ref[...] = ddt
        dstate[gi] = ds0
        dbias_ref[gi] += dbias
        dalog_ref[gi] += dalog
        ddsk_ref[gi] += ddsk

    rc = lambda c: nc - 1 - c
    vec = pl.BlockSpec((1, 128), lambda c, gi: (0, gi))
    acc = pl.BlockSpec((g, 1, 128), lambda c, gi: (0, 0, 0))
    return pl.pallas_call(
        body, name="ssd_bwd",
        grid=(nc, g),
        in_specs=[pl.BlockSpec(memory_space=pl.ANY),
                  pl.BlockSpec((L, 512), lambda c, gi: (rc(c), gi)),
                  pl.BlockSpec((L, 512), lambda c, gi: (rc(c), gi)),
                  pl.BlockSpec((L, 128), lambda c, gi: (rc(c), 16 + gi)),
                  pl.BlockSpec((L, 128), lambda c, gi: (rc(c), 20 + gi)),
                  pl.BlockSpec((L, 128), lambda c, gi: (rc(c), dtb + gi)),
                  pl.BlockSpec((None, None, 512, 128), lambda c, gi: (rc(c), gi, 0, 0)), vec, vec, vec],
        out_specs=[pl.BlockSpec((L, 128), lambda c, gi: (rc(c), dtb + gi)),
                   pl.BlockSpec((L, 512), lambda c, gi: (rc(c), gi)),
                   pl.BlockSpec((L, 128), lambda c, gi: (rc(c), gi)),
                   pl.BlockSpec((L, 128), lambda c, gi: (rc(c), gi)), acc, acc, acc],
        out_shape=[jax.ShapeDtypeStruct(dzx.shape, F32), jax.ShapeDtypeStruct((t, 2048), F32),
                   jax.ShapeDtypeStruct((t, 512), F32), jax.ShapeDtypeStruct((t, 512), F32),
                   jax.ShapeDtypeStruct((g, 1, 128), F32), jax.ShapeDtypeStruct((g, 1, 128), F32),
                   jax.ShapeDtypeStruct((g, 1, 128), F32)],
        scratch_shapes=[pltpu.VMEM((g, 512, 128), F32)],
        input_output_aliases={0: 0},
        compiler_params=_params("arbitrary", "arbitrary"),
    )(dzx, dy, xc, xc, xc, zx, states, bias, alog, dsk)


def _gate_tile(y, z, gn):
    gated = y * _silu(z)
    parts = [_rms(gated[:, k * 512:(k + 1) * 512]) for k in range(SSM_GROUPS)]
    return jnp.concatenate(parts, axis=1) * gn


def gate_out_fwd(h, y, zx, gn, w_out):
    t, d = h.shape
    di = y.shape[1]
    tm = _tile(t, 256)

    def body(h_ref, y_ref, z_ref, gn_ref, w_ref, o_ref):
        yn = _gate_tile(y_ref[...], z_ref[...], gn_ref[...])
        o_ref[...] = h_ref[...] + _dot(_b(yn), w_ref[...])

    return pl.pallas_call(
        body, name="gate_out_fwd",
        grid=(t // tm,),
        in_specs=[pl.BlockSpec((tm, d), lambda i: (i, 0)), pl.BlockSpec((tm, di), lambda i: (i, 0)),
                  pl.BlockSpec((tm, di), lambda i: (i, 0)), pl.BlockSpec((1, di), lambda i: (0, 0)),
                  pl.BlockSpec((di, d), lambda i: (0, 0))],
        out_specs=pl.BlockSpec((tm, d), lambda i: (i, 0)),
        out_shape=jax.ShapeDtypeStruct((t, d), F32),
        compiler_params=_params("arbitrary"),
    )(h, y, zx, gn, w_out)


def gate_out_bwd(dy, y, zx, gn, w_out, n_zx):
    t, d = dy.shape
    di = y.shape[1]
    tm = _tile(t, 256)

    def body(dy_ref, y_ref, z_ref, gn_ref, w_ref, dz_ref, dys_ref, yn_ref, dgn_ref):
        @pl.when(pl.program_id(0) == 0)
        def _():
            dgn_ref[...] = jnp.zeros_like(dgn_ref)

        yn, vjp = jax.vjp(_gate_tile, y_ref[...], z_ref[...], gn_ref[...])
        dyn = _dot_nt(_b(dy_ref[...]), w_ref[...])
        dys, dz, dgn = vjp(dyn)
        yn_ref[...] = _b(yn)
        dys_ref[...] = dys
        dz_ref[...] = dz
        dgn_ref[...] += dgn

    return pl.pallas_call(
        body, name="gate_out_bwd",
        grid=(t // tm,),
        in_specs=[pl.BlockSpec((tm, d), lambda i: (i, 0)), pl.BlockSpec((tm, di), lambda i: (i, 0)),
                  pl.BlockSpec((tm, di), lambda i: (i, 0)), pl.BlockSpec((1, di), lambda i: (0, 0)),
                  pl.BlockSpec((di, d), lambda i: (0, 0))],
        out_specs=[pl.BlockSpec((tm, di), lambda i: (i, 0)), pl.BlockSpec((tm, di), lambda i: (i, 0)),
                   pl.BlockSpec((tm, di), lambda i: (i, 0)), pl.BlockSpec((1, di), lambda i: (0, 0))],
        out_shape=[jax.ShapeDtypeStruct((t, n_zx), F32), jax.ShapeDtypeStruct((t, di), F32),
                   jax.ShapeDtypeStruct((t, di), BF16), jax.ShapeDtypeStruct((1, di), F32)],
        compiler_params=_params("arbitrary"),
    )(dy, y, zx, gn, w_out)


def _attn_block(qp, kvp, kvc, biasm, sinks, qg, kg, w_o, first):
    nq = qp.shape[0]
    n_pairs = qp.shape[1] // 128
    heads_per_kv = (2 * n_pairs) // 2
    seg = functools.partial(_xright, m=_blockdiag64(128))
    scale = ATT_HEAD_DIM ** -0.5
    qi = _iota((nq, 2 * nq), 0) + nq
    kj = _iota((nq, 2 * nq), 1)
    dist = qi - kj
    valid = (dist >= 0) & (dist < ATT_WINDOW) & (jnp.logical_not(first) | (kj >= nq))
    lo = _iota((nq, 128), 1) < 64
    kv = jnp.concatenate([kvp, kvc], axis=0)
    kn = [_group64_rms(kv[:, h * 128:(h + 1) * 128], seg) * kg for h in range(2)]
    vv = [kv[:, 256 + h * 128:256 + (h + 1) * 128] for h in range(2)]
    outs = []
    for p in range(n_pairs):
        kvh = (2 * p) // heads_per_kv
        qn = _group64_rms(qp[:, p * 128:(p + 1) * 128], seg) * qg
        o2 = []
        for j in range(2):
            h = 2 * p + j
            qm = jnp.where(lo, qn, 0.0) if j == 0 else jnp.where(lo, 0.0, qn)
            s = _bmm_nt(qm, kn[kvh]) * scale + biasm[h]
            s = jnp.where(valid, s, NEG)
            sink = sinks[:, h:h + 1]
            m = jnp.maximum(jnp.max(s, axis=-1, keepdims=True), sink)
            pexp = jnp.exp(s - m)
            den = jnp.sum(pexp, axis=-1, keepdims=True) + jnp.exp(sink - m)
            o2.append(_bmm(pexp / den, vv[kvh]))
        outs.append(jnp.where(lo, o2[0], o2[1]))
    o = jnp.concatenate(outs, axis=1)
    return _bmm(o, w_o), o


def attn_fwd(h, qp, kvd, biasm, sinks, qg, kg, w_o):
    t, d = h.shape
    nq = ATT_WINDOW
    nb = t // nq
    nh = biasm.shape[0]

    def body(h_ref, q_ref, kp_ref, kc_ref, bias_ref, s_ref, qg_ref, kg_ref, w_ref, o_ref):
        out, _ = _attn_block(q_ref[...], kp_ref[...], kc_ref[...], bias_ref[...], s_ref[...], qg_ref[...],
                             kg_ref[...], w_ref[...], pl.program_id(0) == 0)
        o_ref[...] = h_ref[...] + out

    vec = pl.BlockSpec((1, 128), lambda i: (0, 0))
    return pl.pallas_call(
        body, name="attn_fwd",
        grid=(nb,),
        in_specs=[pl.BlockSpec((nq, d), lambda i: (i, 0)), pl.BlockSpec((nq, nh * 64), lambda i: (i, 0)),
                  pl.BlockSpec((nq, 512), lambda i: (jnp.maximum(i - 1, 0), 0)),
                  pl.BlockSpec((nq, 512), lambda i: (i, 0)),
                  pl.BlockSpec((nh, nq, 2 * nq), lambda i: (0, 0, 0)), vec, vec, vec,
                  pl.BlockSpec((nh * 64, d), lambda i: (0, 0))],
        out_specs=pl.BlockSpec((nq, d), lambda i: (i, 0)),
        out_shape=jax.ShapeDtypeStruct((t, d), F32),
        compiler_params=_params("arbitrary"),
    )(h, qp, kvd, kvd, biasm, sinks, qg, kg, w_o)


def attn_bwd(dy, qp, kvd, biasm, sinks, qg, kg, w_o):
    t, d = dy.shape
    nq = ATT_WINDOW
    nb = t // nq
    nh = biasm.shape[0]

    def body(dy_ref, q_ref, kp_ref, kc_ref, bias_ref, s_ref, qg_ref, kg_ref, w_ref,
             dq_ref, dkv_ref, o_ref, dbias_ref, ds_ref, dqg_ref, dkg_ref, carry):
        i = pl.program_id(0)

        @pl.when(i == 0)
        def _():
            carry[...] = jnp.zeros_like(carry)
            dbias_ref[...] = jnp.zeros_like(dbias_ref)
            ds_ref[...] = jnp.zeros_like(ds_ref)
            dqg_ref[...] = jnp.zeros_like(dqg_ref)
            dkg_ref[...] = jnp.zeros_like(dkg_ref)

        @pl.when(i < nb)
        def _():
            fn = functools.partial(_attn_block, w_o=w_ref[...], first=(i == 0))
            (_, o), vjp = jax.vjp(fn, q_ref[...], kp_ref[...], kc_ref[...], bias_ref[...], s_ref[...],
                                  qg_ref[...], kg_ref[...])
            dq, dkp, dkc, dbias, dsk, dqg, dkg = vjp((dy_ref[...], jnp.zeros((nq, nh * 64), F32)))
            dq_ref[...] = dq
            o_ref[...] = _b(o)
            dkv_ref[...] = _fold64(carry[...] + dkp)
            carry[...] = dkc
            dbias_ref[...] += dbias
            ds_ref[...] += dsk
            dqg_ref[...] += _fold64(dqg)
            dkg_ref[...] += _fold64(dkg)

        @pl.when(i == nb)
        def _():
            dkv_ref[...] = _fold64(carry[...])

    cl = lambda i: jnp.minimum(i, nb - 1)
    vec = pl.BlockSpec((1, 128), lambda i: (0, 0))
    return pl.pallas_call(
        body, name="attn_bwd",
        grid=(nb + 1,),
        in_specs=[pl.BlockSpec((nq, d), lambda i: (cl(i), 0)), pl.BlockSpec((nq, nh * 64), lambda i: (cl(i), 0)),
                  pl.BlockSpec((nq, 512), lambda i: (jnp.maximum(cl(i) - 1, 0), 0)),
                  pl.BlockSpec((nq, 512), lambda i: (cl(i), 0)),
                  pl.BlockSpec((nh, nq, 2 * nq), lambda i: (0, 0, 0)), vec, vec, vec,
                  pl.BlockSpec((nh * 64, d), lambda i: (0, 0))],
        out_specs=[pl.BlockSpec((nq, nh * 64), lambda i: (cl(i), 0)),
                   pl.BlockSpec((nq, 512), lambda i: (jnp.maximum(i - 1, 0), 0)),
                   pl.BlockSpec((nq, nh * 64), lambda i: (cl(i), 0)),
                   pl.BlockSpec((nh, nq, 2 * nq), lambda i: (0, 0, 0)), vec, vec, vec],
        out_shape=[jax.ShapeDtypeStruct((t, nh * 64), F32), jax.ShapeDtypeStruct((t, 512), F32),
                   jax.ShapeDtypeStruct((t, nh * 64), BF16), jax.ShapeDtypeStruct((nh, nq, 2 * nq), F32),
                   jax.ShapeDtypeStruct((1, 128), F32), jax.ShapeDtypeStruct((1, 128), F32),
                   jax.ShapeDtypeStruct((1, 128), F32)],
        scratch_shapes=[pltpu.VMEM((nq, 512), F32)],
        compiler_params=_params("arbitrary"),
    )(dy, qp, kvd, kvd, biasm, sinks, qg, kg, w_o)


def _t5_buckets():
    nq = ATT_WINDOW
    dist = (np.arange(nq)[:, None] + nq) - np.arange(2 * nq)[None, :]
    n = np.maximum(dist, 0)
    max_exact = REL_BUCKETS // 2
    nf = np.maximum(n, 1).astype(np.float32)
    large = max_exact + (np.log(nf / max_exact) / math.log(ATT_WINDOW / max_exact)
                         * (REL_BUCKETS - max_exact)).astype(np.int32)
    large = np.minimum(large, REL_BUCKETS - 1)
    return np.where(n < max_exact, n, large).astype(np.int32)


def rel_bias_bwd(dbias, buckets):
    nh = dbias.shape[0]

    def body(db_ref, bk_ref, o_ref):
        bk = bk_ref[...]
        lane = _iota((1, 128), 1)
        row = _iota((REL_BUCKETS, 128), 0)
        acc = jnp.zeros((REL_BUCKETS, 128), F32)
        for h in range(nh):
            dbh = db_ref[h]
            for b in range(REL_BUCKETS):
                v = jnp.sum(jnp.where(bk == b, dbh, 0.0))
                acc = acc + jnp.where((row == b) & (lane == h), v, 0.0)
        o_ref[...] = acc

    return pl.pallas_call(
        body, name="rel_bias_bwd",
        out_shape=jax.ShapeDtypeStruct((REL_BUCKETS, 128), F32),
        compiler_params=_params(),
    )(dbias, buckets)


def loss_head(y, target):
    t, d = y.shape
    tm = _tile(t, 512)

    def body(y_ref, t_ref, l_ref, dy_ref):
        @pl.when(pl.program_id(0) == 0)
        def _():
            l_ref[...] = jnp.zeros_like(l_ref)

        e = y_ref[...] - t_ref[...]
        l_ref[...] += 0.5 * jnp.sum(jnp.mean(e * e, axis=-1, keepdims=True), axis=0, keepdims=True)
        dy_ref[...] = e * (1.0 / d)

    return pl.pallas_call(
        body, name="loss_head",
        grid=(t // tm,),
        in_specs=[pl.BlockSpec((tm, d), lambda i: (i, 0)), pl.BlockSpec((tm, d), lambda i: (i, 0))],
        out_specs=[pl.BlockSpec((1, 1), lambda i: (0, 0)), pl.BlockSpec((tm, d), lambda i: (i, 0))],
        out_shape=[jax.ShapeDtypeStruct((1, 1), F32), jax.ShapeDtypeStruct((t, d), F32)],
        compiler_params=_params("arbitrary"),
    )(y, target)


def adamw(w, g, m, v):
    r, c = w.shape
    tr = r if r <= 512 else _tile(r, 256)

    def body(w_ref, g_ref, m_ref, v_ref, d_ref, nm_ref, nv_ref):
        gg = g_ref[...]
        nm = ADAM_B1 * m_ref[...] + (1.0 - ADAM_B1) * gg
        nv = ADAM_B2 * v_ref[...] + (1.0 - ADAM_B2) * (gg * gg)
        m_hat = nm / (1.0 - ADAM_B1 ** ADAM_STEP)
        v_hat = nv / (1.0 - ADAM_B2 ** ADAM_STEP)
        d_ref[...] = -ADAM_LR * (m_hat / (jnp.sqrt(v_hat) + ADAM_EPS) + ADAM_WD * w_ref[...])
        nm_ref[...] = nm
        nv_ref[...] = nv

    spec = pl.BlockSpec((tr, c), lambda i: (i, 0))
    shp = jax.ShapeDtypeStruct((r, c), F32)
    return pl.pallas_call(
        body, name="adamw",
        grid=(r // tr,),
        in_specs=[spec] * 4, out_specs=[spec] * 3, out_shape=[shp] * 3,
        compiler_params=_params("arbitrary"),
    )(w, g, m, v)


def _my_pos():
    return lax.axis_index("x"), lax.axis_index("y"), lax.axis_index("c")


def _other_chips(x, y):
    return [(1 - x, y), (x, 1 - y), (1 - x, 1 - y)]


def _chip_id(x, y):
    return 2 * x + y


def gather_weights(shards):
    n = len(shards)

    def body(*refs):
        ins, outs = refs[:n], refs[n:2 * n]
        send_sems, recv_sems, local_sems = refs[2 * n:]
        x, y, c = _my_pos()
        me = _chip_id(x, y)
        sibling = (x, y, 1 - c)
        chips = _other_chips(x, y)

        def copy(p, k, chip, half, to, src=None):
            dst = outs[p].at[_chip_id(*chip), half]
            return pltpu.make_async_remote_copy(
                src_ref=dst if src is None else src, dst_ref=dst,
                send_sem=send_sems.at[p, k], recv_sem=recv_sems.at[p, k], device_id=to, device_id_type=MESH)

        local = [pltpu.make_async_copy(ins[p], outs[p].at[me], local_sems.at[p]) for p in range(n)]
        for cp in local:
            cp.start()
        first = [[copy(p, j, (x, y), c, (*chip, c), src=ins[p].at[c]) for j, chip in enumerate(chips)]
                 for p in range(n)]
        for p in range(n):
            for cp in first[p]:
                cp.start()
        passed = [[copy(p, 3 + j, chip, c, sibling) for j, chip in enumerate(chips)] for p in range(n)]
        for p in range(n):
            for j, chip in enumerate(chips):
                copy(p, j, chip, c, (x, y, c)).wait_recv()
                passed[p][j].start()
        for p in range(n):
            for j, chip in enumerate(chips):
                copy(p, 3 + j, chip, 1 - c, (x, y, c)).wait_recv()
        for p in range(n):
            for cp in first[p] + passed[p]:
                cp.wait_send()
            local[p].wait()

    any_spec = pl.BlockSpec(memory_space=pl.ANY)
    return pl.pallas_call(
        body, name="gather_weights",
        in_specs=[any_spec] * n, out_specs=[any_spec] * n,
        out_shape=[jax.ShapeDtypeStruct((N_CHIPS,) + s.shape, s.dtype) for s in shards],
        scratch_shapes=[pltpu.SemaphoreType.DMA((n, 6)), pltpu.SemaphoreType.DMA((n, 6)),
                        pltpu.SemaphoreType.DMA((n,))],
    )(*shards)


def allreduce_small(v):
    r, c = v.shape

    def body(v_ref, o_ref, buf, send_sems, recv_sems):
        x, y, cc = _my_pos()
        me = 4 * x + 2 * y + cc
        buf[me] = v_ref[...]
        copies = []
        for k in range(1, 8):
            dx, dy, dc = (k >> 2) & 1, (k >> 1) & 1, k & 1
            peer = (x ^ dx, y ^ dy, cc ^ dc)
            cp = pltpu.make_async_remote_copy(
                src_ref=v_ref, dst_ref=buf.at[me], send_sem=send_sems.at[k - 1], recv_sem=recv_sems.at[k - 1],
                device_id=peer, device_id_type=MESH)
            cp.start()
            copies.append(cp)
        for cp in copies:
            cp.wait_recv()
        for cp in copies:
            cp.wait_send()
        acc = buf[0]
        for k in range(1, 8):
            acc = acc + buf[k]
        o_ref[...] = acc

    vm = pl.BlockSpec(memory_space=pltpu.VMEM)
    return pl.pallas_call(
        body, name="allreduce_small",
        in_specs=[vm], out_specs=vm,
        out_shape=jax.ShapeDtypeStruct((r, c), F32),
        scratch_shapes=[pltpu.VMEM((8, r, c), F32), pltpu.SemaphoreType.DMA((7,)), pltpu.SemaphoreType.DMA((7,))],
    )(v)


def exchange_sibling_halves(grads):
    n = len(grads)

    def body(*refs):
        ins, outs = refs[:n], refs[n:2 * n]
        send_sems, recv_sems = refs[2 * n:]
        x, y, c = _my_pos()
        copies = []
        for p in range(n):
            cp = pltpu.make_async_remote_copy(
                src_ref=ins[p].at[:, 1 - c], dst_ref=outs[p], send_sem=send_sems.at[p], recv_sem=recv_sems.at[p],
                device_id=(x, y, 1 - c), device_id_type=MESH)
            cp.start()
            copies.append(cp)
        for cp in copies:
            cp.wait_recv()
        for cp in copies:
            cp.wait_send()

    any_spec = pl.BlockSpec(memory_space=pl.ANY)
    return pl.pallas_call(
        body, name="exchange_sibling_halves",
        in_specs=[any_spec] * n, out_specs=[any_spec] * n,
        out_shape=[jax.ShapeDtypeStruct((g.shape[0],) + g.shape[2:], g.dtype) for g in grads],
        scratch_shapes=[pltpu.SemaphoreType.DMA((n,)), pltpu.SemaphoreType.DMA((n,))],
    )(*grads)


def exchange_chip_partials(parts):
    n = len(parts)

    def body(*refs):
        ins, outs = refs[:n], refs[n:2 * n]
        send_sems, recv_sems, local_sems = refs[2 * n:]
        x, y, c = _my_pos()
        me = _chip_id(x, y)
        chips = _other_chips(x, y)
        local = [pltpu.make_async_copy(ins[p].at[me], outs[p].at[me], local_sems.at[p]) for p in range(n)]
        for cp in local:
            cp.start()
        copies = []
        for p in range(n):
            for j, chip in enumerate(chips):
                cp = pltpu.make_async_remote_copy(
                    src_ref=ins[p].at[_chip_id(*chip)], dst_ref=outs[p].at[me],
                    send_sem=send_sems.at[p, j], recv_sem=recv_sems.at[p, j],
                    device_id=(*chip, c), device_id_type=MESH)
                cp.start()
                copies.append(cp)
        for cp in copies:
            cp.wait_recv()
        for cp in copies:
            cp.wait_send()
        for cp in local:
            cp.wait()

    any_spec = pl.BlockSpec(memory_space=pl.ANY)
    return pl.pallas_call(
        body, name="exchange_chip_partials",
        in_specs=[any_spec] * n, out_specs=[any_spec] * n,
        out_shape=[jax.ShapeDtypeStruct(s.shape, s.dtype) for s in parts],
        scratch_shapes=[pltpu.SemaphoreType.DMA((n, 3)), pltpu.SemaphoreType.DMA((n, 3)),
                        pltpu.SemaphoreType.DMA((n,))],
    )(*parts)


def share_with_sibling(halves):
    n = len(halves)

    def body(*refs):
        ins, outs = refs[:n], refs[n:2 * n]
        send_sems, recv_sems, local_sems = refs[2 * n:]
        x, y, c = _my_pos()
        local = [pltpu.make_async_copy(ins[p], outs[p].at[c], local_sems.at[p]) for p in range(n)]
        for cp in local:
            cp.start()
        copies = []
        for p in range(n):
            cp = pltpu.make_async_remote_copy(
                src_ref=ins[p], dst_ref=outs[p].at[c], send_sem=send_sems.at[p], recv_sem=recv_sems.at[p],
                device_id=(x, y, 1 - c), device_id_type=MESH)
            cp.start()
            copies.append(cp)
        for cp in copies:
            cp.wait_recv()
        for cp in copies:
            cp.wait_send()
        for cp in local:
            cp.wait()

    any_spec = pl.BlockSpec(memory_space=pl.ANY)
    return pl.pallas_call(
        body, name="share_with_sibling",
        in_specs=[any_spec] * n, out_specs=[any_spec] * n,
        out_shape=[jax.ShapeDtypeStruct((2,) + s.shape, s.dtype) for s in halves],
        scratch_shapes=[pltpu.SemaphoreType.DMA((n,)), pltpu.SemaphoreType.DMA((n,)), pltpu.SemaphoreType.DMA((n,))],
    )(*halves)


def add_sibling(g, recv, half):
    _, _, r, c = g.shape
    tr = _tile(r, 256) if r % 256 == 0 else r

    def body(half_ref, g_ref, r_ref, o32_ref, o16_ref):
        s = g_ref[...] + r_ref[...]
        o32_ref[...] = s
        o16_ref[...] = _b(s)

    return pl.pallas_call(
        body, name="add_sibling",
        grid_spec=pltpu.PrefetchScalarGridSpec(
            num_scalar_prefetch=1, grid=(N_CHIPS, r // tr),
            in_specs=[pl.BlockSpec((None, None, tr, c), lambda k, i, hf: (k, hf[0], i, 0)),
                      pl.BlockSpec((None, tr, c), lambda k, i, hf: (k, i, 0))],
            out_specs=[pl.BlockSpec((None, tr, c), lambda k, i, hf: (k, i, 0)),
                       pl.BlockSpec((None, tr, c), lambda k, i, hf: (k, i, 0))]),
        out_shape=[jax.ShapeDtypeStruct((N_CHIPS, r, c), F32), jax.ShapeDtypeStruct((N_CHIPS, r, c), BF16)],
        compiler_params=_params("arbitrary", "arbitrary"),
    )(half, g, recv)


def add_chip_partials(p32, recv, chip):
    _, r, c = p32.shape
    tr = _tile(r, 256) if r % 256 == 0 else r

    def body(chip_ref, p_ref, r_ref, o_ref):
        me = chip_ref[0]
        own = p_ref[...]
        acc = jnp.zeros_like(own)
        for k in range(N_CHIPS):
            acc = acc + jnp.where(me == k, own, r_ref[k].astype(F32))
        o_ref[...] = acc

    return pl.pallas_call(
        body, name="add_chip_partials",
        grid_spec=pltpu.PrefetchScalarGridSpec(
            num_scalar_prefetch=1, grid=(r // tr,),
            in_specs=[pl.BlockSpec((None, tr, c), lambda i, ch: (ch[0], i, 0)),
                      pl.BlockSpec((N_CHIPS, tr, c), lambda i, ch: (0, i, 0))],
            out_specs=pl.BlockSpec((tr, c), lambda i, ch: (i, 0))),
        out_shape=jax.ShapeDtypeStruct((r, c), F32),
        compiler_params=_params("arbitrary"),
    )(chip, p32, recv)


def cast_bf16(w):
    r, c = w.shape
    tr = _tile(r, 256) if r % 256 == 0 else r

    def body(w_ref, o_ref):
        o_ref[...] = _b(w_ref[...])

    return pl.pallas_call(
        body, name="cast_bf16",
        grid=(r // tr,),
        in_specs=[pl.BlockSpec((tr, c), lambda i: (i, 0))], out_specs=pl.BlockSpec((tr, c), lambda i: (i, 0)),
        out_shape=jax.ShapeDtypeStruct((r, c), BF16),
        compiler_params=_params("arbitrary"),
    )(w)


SMALL_ROWS = 256


def kernel(x, ffn_norm, ffn_w1, ffn_w3, ffn_w2, ssm_norm, ssm_w_in, ssm_conv_w, ssm_conv_b, ssm_dt_bias, ssm_a_log, ssm_d, ssm_gate_norm, ssm_w_out, kv_norm, w_kv, k_norm, attn_norm, w_q, q_norm, sinks, w_o, rel_bias, loss_target, m_ffn_norm, m_ffn_w1, m_ffn_w3, m_ffn_w2, m_ssm_norm, m_ssm_w_in, m_ssm_conv_w, m_ssm_conv_b, m_ssm_dt_bias, m_ssm_a_log, m_ssm_d, m_ssm_gate_norm, m_ssm_w_out, m_kv_norm, m_w_kv, m_k_norm, m_attn_norm, m_w_q, m_q_norm, m_sinks, m_w_o, m_rel_bias, v_ffn_norm, v_ffn_w1, v_ffn_w3, v_ffn_w2, v_ssm_norm, v_ssm_w_in, v_ssm_conv_w, v_ssm_conv_b, v_ssm_dt_bias, v_ssm_a_log, v_ssm_d, v_ssm_gate_norm, v_ssm_w_out, v_kv_norm, v_w_kv, v_k_norm, v_attn_norm, v_w_q, v_q_norm, v_sinks, v_w_o, v_rel_bias):
    weights = dict(ffn_norm=ffn_norm, ffn_w1=ffn_w1, ffn_w3=ffn_w3, ffn_w2=ffn_w2, ssm_norm=ssm_norm,
                   ssm_w_in=ssm_w_in, ssm_conv_w=ssm_conv_w, ssm_conv_b=ssm_conv_b, ssm_dt_bias=ssm_dt_bias,
                   ssm_a_log=ssm_a_log, ssm_d=ssm_d, ssm_gate_norm=ssm_gate_norm, ssm_w_out=ssm_w_out,
                   kv_norm=kv_norm, w_kv=w_kv, k_norm=k_norm, attn_norm=attn_norm, w_q=w_q, q_norm=q_norm,
                   sinks=sinks, w_o=w_o, rel_bias=rel_bias)
    m_in = dict(ffn_norm=m_ffn_norm, ffn_w1=m_ffn_w1, ffn_w3=m_ffn_w3, ffn_w2=m_ffn_w2, ssm_norm=m_ssm_norm,
                ssm_w_in=m_ssm_w_in, ssm_conv_w=m_ssm_conv_w, ssm_conv_b=m_ssm_conv_b, ssm_dt_bias=m_ssm_dt_bias,
                ssm_a_log=m_ssm_a_log, ssm_d=m_ssm_d, ssm_gate_norm=m_ssm_gate_norm, ssm_w_out=m_ssm_w_out,
                kv_norm=m_kv_norm, w_kv=m_w_kv, k_norm=m_k_norm, attn_norm=m_attn_norm, w_q=m_w_q, q_norm=m_q_norm,
                sinks=m_sinks, w_o=m_w_o, rel_bias=m_rel_bias)
    v_in = dict(ffn_norm=v_ffn_norm, ffn_w1=v_ffn_w1, ffn_w3=v_ffn_w3, ffn_w2=v_ffn_w2, ssm_norm=v_ssm_norm,
                ssm_w_in=v_ssm_w_in, ssm_conv_w=v_ssm_conv_w, ssm_conv_b=v_ssm_conv_b, ssm_dt_bias=v_ssm_dt_bias,
                ssm_a_log=v_ssm_a_log, ssm_d=v_ssm_d, ssm_gate_norm=v_ssm_gate_norm, ssm_w_out=v_ssm_w_out,
                kv_norm=v_kv_norm, w_kv=v_w_kv, k_norm=v_k_norm, attn_norm=v_attn_norm, w_q=v_w_q, q_norm=v_q_norm,
                sinks=v_sinks, w_o=v_w_o, rel_bias=v_rel_bias)
    return _step(x[0], loss_target[0], weights, m_in, v_in)


BIG = ("ffn_w1", "ffn_w3", "ffn_w2", "ssm_w_in", "ssm_w_out", "w_kv", "w_q", "w_o")
SMALL = (("ffn_norm", True), ("ssm_norm", True), ("ssm_conv_w", True), ("ssm_conv_b", True),
         ("ssm_gate_norm", True), ("ssm_dt_bias", False), ("ssm_a_log", False), ("ssm_d", False),
         ("kv_norm", False), ("k_norm", False), ("attn_norm", False), ("q_norm", False), ("sinks", False),
         ("rel_bias", False))


def _halves_view(a):
    shape = a.shape
    ax = next(i for i, s in enumerate(shape) if s > 1)
    lead = int(np.prod(shape[:ax + 1])) // 2
    c = shape[-1]
    total = int(np.prod(shape))
    assert shape[ax] % 2 == 0
    return a.reshape(2, total // 2 // c, c)


def _small_layout(weights):
    off, table = 0, {}
    for name, sharded in SMALL:
        shape = weights[name].shape
        full = shape[:-1] + (shape[-1] * N_CHIPS,) if sharded else shape
        n = int(np.prod(full))
        table[name] = (off, full, sharded)
        off += n
    assert off <= SMALL_ROWS * 128
    return table


def _place_small(values, table, chip, scale_mask):
    flat = jnp.zeros((SMALL_ROWS * 128,), F32)
    for name, (off, full, sharded) in table.items():
        if not sharded:
            continue
        v = values[name].astype(F32)
        lead = int(np.prod(full[:-1]))
        w = v.shape[-1]
        blk = jnp.zeros((lead, full[-1]), F32)
        blk = lax.dynamic_update_slice(blk, v.reshape(lead, w) * scale_mask, (0, chip * w))
        flat = lax.dynamic_update_slice(flat, blk.reshape(-1), (off,))
    return flat.reshape(SMALL_ROWS, 128)


def _take_small(mat, table, name):
    off, full, _ = table[name]
    n = int(np.prod(full))
    return mat.reshape(-1)[off:off + n].reshape(full)


def _step(x, target, weights, m_in, v_in):
    t, d = x.shape
    xi, yi, ci = lax.axis_index("x"), lax.axis_index("y"), lax.axis_index("c")
    chip = 2 * xi + yi
    chip_arr = jnp.reshape(chip, (1,)).astype(jnp.int32)
    half_arr = jnp.reshape(ci, (1,)).astype(jnp.int32)

    shards16 = [_halves_view(cast_bf16(weights[n].reshape(-1, weights[n].shape[-1]))) for n in BIG]
    gathered = dict(zip(BIG, gather_weights(shards16)))
    table = _small_layout(weights)
    south = (ci == 0).astype(F32)
    small = allreduce_small(_place_small(weights, table, chip, south))
    sp = {n: _take_small(small, table, n) if sh else weights[n] for n, sh in SMALL}

    fs = weights["ffn_w1"].shape[-1]
    w1 = gathered["ffn_w1"].reshape(N_CHIPS, 2, 2, d, fs)
    w3 = gathered["ffn_w3"].reshape(N_CHIPS, 2, 2, d, fs)
    w2 = gathered["ffn_w2"].reshape(N_CHIPS, 2, 2, fs, d)
    n_in = weights["ssm_w_in"].shape[-1] * N_CHIPS
    di = weights["ssm_w_out"].shape[1] * N_CHIPS
    nheads = di // SSM_HEAD_DIM
    conv_dim = n_in - di - nheads
    w_in_full = jnp.moveaxis(gathered["ssm_w_in"].reshape(N_CHIPS, d, n_in // N_CHIPS), 0, 1).reshape(d, n_in)
    hpg = nheads // SSM_GROUPS

    def spread_heads(v):
        lead = v.shape[:-1]
        v = v.reshape(lead + (SSM_GROUPS, hpg))
        v = jnp.pad(v, [(0, 0)] * len(lead) + [(0, 0), (0, 128 - hpg)])
        return v.reshape(lead + (SSM_GROUPS * 128,))

    def gather_heads(v):
        lead = v.shape[:-1]
        return v.reshape(lead + (SSM_GROUPS, 128))[..., :hpg].reshape(lead + (nheads,))

    dt_col0 = di + conv_dim
    n_zx = dt_col0 + SSM_GROUPS * 128
    w_in = jnp.concatenate([w_in_full[:, :dt_col0], spread_heads(w_in_full[:, dt_col0:])], axis=1)
    w_out = gathered["ssm_w_out"].reshape(di, d)
    nkv = weights["w_kv"].shape[1] // (2 * ATT_HEAD_DIM)
    assert nkv == 2
    wkv_full = gathered["w_kv"].reshape(d, 2 * nkv * ATT_HEAD_DIM)
    wkv_heads = wkv_full.reshape(d, 2 * nkv, 1, ATT_HEAD_DIM)
    w_kvd = jnp.broadcast_to(wkv_heads, (d, 2 * nkv, 2, ATT_HEAD_DIM)).reshape(d, 4 * nkv * ATT_HEAD_DIM)
    wq = gathered["w_q"].reshape(d, -1)
    wo = gathered["w_o"].reshape(-1, d)
    nh = wq.shape[1] // ATT_HEAD_DIM

    ffn_g = sp["ffn_norm"]
    ssm_g = sp["ssm_norm"].reshape(1, d)
    cw = jnp.pad(sp["ssm_conv_w"].reshape(SSM_CONV, conv_dim), [(0, 8 - SSM_CONV), (0, 0)])
    cb = sp["ssm_conv_b"].reshape(1, conv_dim)
    gate_g = sp["ssm_gate_norm"].reshape(1, di)
    dt_bias = spread_heads(sp["ssm_dt_bias"].reshape(1, nheads))
    a_log = spread_heads(sp["ssm_a_log"].reshape(1, nheads))
    d_skip = spread_heads(sp["ssm_d"].reshape(1, nheads))
    kv_g = sp["kv_norm"].reshape(1, d)
    k_g = jnp.tile(sp["k_norm"].reshape(1, ATT_HEAD_DIM), (1, 2))
    attn_g = sp["attn_norm"].reshape(1, d)
    q_g = jnp.tile(sp["q_norm"].reshape(1, ATT_HEAD_DIM), (1, 2))
    sink_row = jnp.pad(sp["sinks"].reshape(1, nh), [(0, 0), (0, 128 - nh)])
    buckets = jnp.asarray(_t5_buckets())
    biasm = jnp.transpose(sp["rel_bias"][buckets], (2, 0, 1))

    h0 = x
    h1, a00, b00 = ffn_fwd(h0, ffn_g[0, 0].reshape(1, d), w1, w3, w2, 0, 0)
    zx = norm_mm(h1, ssm_g, w_in)
    xc = conv_fwd(zx, cw, cb, di)
    y_ssd, states = ssd_fwd(xc, zx, dt_bias, a_log, d_skip, dt_col0)
    h2 = gate_out_fwd(h1, y_ssd, zx, gate_g, w_out)
    h3, a01, b01 = ffn_fwd(h2, ffn_g[0, 1].reshape(1, d), w1, w3, w2, 0, 1)
    kvd = norm_mm(h3, kv_g, w_kvd)
    h4, a10, b10 = ffn_fwd(h3, ffn_g[1, 0].reshape(1, d), w1, w3, w2, 1, 0)
    qp = norm_mm(h4, attn_g, wq)
    h5 = attn_fwd(h4, qp, kvd, biasm, sink_row, q_g, k_g, wo)
    h6, a11, b11 = ffn_fwd(h5, ffn_g[1, 1].reshape(1, d), w1, w3, w2, 1, 1)
    loss_part, d6 = loss_head(h6, target)
    loss = lax.psum(loss_part[0, 0], ("x", "y", "c"))

    gw1 = [[None, None], [None, None]]
    gw3 = [[None, None], [None, None]]
    gw2 = [[None, None], [None, None]]
    gfn = [[None, None], [None, None]]

    def ffn_back(h_in, dy, a_s, b_s, layer, idx):
        dh, u, da, db, s, dg = ffn_bwd(h_in, dy, ffn_g[layer, idx].reshape(1, d), a_s, b_s, w1, w3, w2, layer, idx)
        gw1[layer][idx] = wgrad_grouped_b(u, da)
        gw3[layer][idx] = wgrad_grouped_b(u, db)
        gw2[layer][idx] = wgrad_grouped_a(s, dy, 0.5)
        gfn[layer][idx] = dg
        return dh

    d5 = ffn_back(h5, d6, a11, b11, 1, 1)
    dqp, dkvd, o16, dbiasm, dsinks, dqg, dkg = attn_bwd(d5, qp, kvd, biasm, sink_row, q_g, k_g, wo)
    g_wo = wgrad(o16, d5)
    d4, u_q, g_attn_norm = norm_mm_bwd(h4, attn_g, wq, dqp, d5)
    g_wq = wgrad(u_q, dqp)
    d3a = ffn_back(h3, d4, a10, b10, 1, 0)
    d3, u_kv, g_kv_norm = norm_mm_bwd(h3, kv_g, w_kvd, dkvd, d3a, 0.5)
    g_wkvd = wgrad(u_kv, dkvd)
    d2 = ffn_back(h2, d3, a01, b01, 0, 1)
    dzx, dy_ssd, yn16, g_gate = gate_out_bwd(d2, y_ssd, zx, gate_g, w_out, n_zx)
    g_wout = wgrad(yn16, d2)
    dzx, dxs, dbm, dcm, g_dtb, g_alog, g_dsk = ssd_bwd(dzx, dy_ssd, xc, zx, states, dt_bias, a_log, d_skip, dt_col0)
    dzx, g_cw, g_cb = conv_bwd(dzx, zx, dxs, dbm, dcm, cw, cb, di)
    d1, u_in, g_ssm_norm = norm_mm_bwd(h1, ssm_g, w_in, dzx, d2)
    g_win = wgrad(u_in, dzx)
    grad_x = ffn_back(h0, d1, a00, b00, 0, 0)
    g_relb = rel_bias_bwd(dbiasm, buckets)

    def stack_ffn(g):
        return jnp.stack([jnp.stack([g[l][i] for i in range(2)], axis=1) for l in range(2)], axis=1)

    g_win_full = jnp.concatenate([g_win[:, :dt_col0], gather_heads(g_win[:, dt_col0:])], axis=1)
    g_wkv = g_wkvd.reshape(d, 2 * nkv, 2, ATT_HEAD_DIM)[:, :, 0, :].reshape(d, 2 * nkv * ATT_HEAD_DIM)
    big_grads = {
        "ffn_w1": stack_ffn(gw1), "ffn_w3": stack_ffn(gw3), "ffn_w2": stack_ffn(gw2),
        "ssm_w_in": jnp.moveaxis(g_win_full.reshape(d, N_CHIPS, n_in // N_CHIPS), 1, 0),
        "ssm_w_out": g_wout.reshape(N_CHIPS, di // N_CHIPS, d),
        "w_kv": g_wkv.reshape(N_CHIPS, d // N_CHIPS, -1),
        "w_q": g_wq.reshape(N_CHIPS, d // N_CHIPS, -1),
        "w_o": g_wo.reshape(N_CHIPS, -1, d),
    }
    views = []
    for n in BIG:
        shard = weights[n]
        hv = _halves_view(shard)
        views.append(big_grads[n].reshape((N_CHIPS,) + hv.shape))
    recv1 = exchange_sibling_halves(views)
    p32, p16 = zip(*[add_sibling(g, r, half_arr) for g, r in zip(views, recv1)])
    recv2 = exchange_chip_partials(list(p16))
    mine = [add_chip_partials(p, r, chip_arr) for p, r in zip(p32, recv2)]
    full = share_with_sibling(mine)
    grads = {n: f.reshape(weights[n].shape) for n, f in zip(BIG, full)}

    small_grads = {
        "ffn_norm": jnp.stack([jnp.stack([gfn[l][i].reshape(d) for i in range(2)]) for l in range(2)]),
        "ssm_norm": g_ssm_norm.reshape(1, d),
        "ssm_conv_w": g_cw[:SSM_CONV].reshape(1, SSM_CONV, conv_dim),
        "ssm_conv_b": g_cb.reshape(1, conv_dim),
        "ssm_gate_norm": g_gate.reshape(1, di),
        "ssm_dt_bias": gather_heads(g_dtb.reshape(1, -1)), "ssm_a_log": gather_heads(g_alog.reshape(1, -1)),
        "ssm_d": gather_heads(g_dsk.reshape(1, -1)),
        "kv_norm": g_kv_norm.reshape(d), "k_norm": dkg[0, :ATT_HEAD_DIM], "attn_norm": g_attn_norm.reshape(1, d),
        "q_norm": dqg[:, :ATT_HEAD_DIM], "sinks": dsinks[:, :nh], "rel_bias": g_relb[:, :nh],
    }
    flat = jnp.zeros((SMALL_ROWS * 128,), F32)
    for name, (off, fshape, _) in table.items():
        flat = lax.dynamic_update_slice(flat, small_grads[name].astype(F32).reshape(-1), (off,))
    small_sum = allreduce_small(flat.reshape(SMALL_ROWS, 128))
    for name, (off, fshape, sharded) in table.items():
        g = _take_small(small_sum, table, name)
        if sharded:
            w = weights[name].shape[-1]
            lead = int(np.prod(fshape[:-1]))
            g = lax.dynamic_slice(g.reshape(lead, fshape[-1]), (0, chip * w), (lead, w)).reshape(weights[name].shape)
        grads[name] = g.reshape(weights[name].shape)

    names = list(weights)
    deltas, new_m, new_v = {}, {}, {}
    small_names = [n for n, _ in SMALL]
    for n in BIG:
        shp = weights[n].shape
        v2 = lambda a: a.reshape(-1, shp[-1])
        dl, nm, nv = adamw(v2(weights[n]), v2(grads[n]), v2(m_in[n]), v2(v_in[n]))
        deltas[n], new_m[n], new_v[n] = dl.reshape(shp), nm.reshape(shp), nv.reshape(shp)
    sizes = [int(np.prod(weights[n].shape)) for n in small_names]
    tot = sum(sizes)
    rows = -(-tot // 128)
    rows = -(-rows // 8) * 8

    def pack(dct):
        flat = jnp.concatenate([dct[n].reshape(-1) for n in small_names])
        return jnp.pad(flat, (0, rows * 128 - tot), constant_values=1.0).reshape(rows, 128)

    dl, nm, nv = adamw(pack(weights), pack(grads), pack(m_in), pack(v_in))
    off = 0
    for n, sz in zip(small_names, sizes):
        shp = weights[n].shape
        take = lambda a: a.reshape(-1)[off:off + sz].reshape(shp)
        deltas[n], new_m[n], new_v[n] = take(dl), take(nm), take(nv)
        off += sz

    return (loss, grad_x[None], *[grads[n] for n in names], *[deltas[n] for n in names],
            *[new_m[n] for n in names], *[new_v[n] for n in names])
```

```python
import functools
import math

import jax
import jax.numpy as jnp
import numpy as np
from jax import lax
from jax.experimental import pallas as pl
from jax.experimental.pallas import tpu as pltpu

F32 = jnp.float32
BF16 = jnp.bfloat16
EPS = 1e-6
MESH = pl.DeviceIdType.MESH

SSM_HEAD_DIM = 64
SSM_GROUPS = 4
SSM_STATE = 128
SSM_CONV = 4
SSM_CHUNK = 256
ATT_HEAD_DIM = 64
ATT_WINDOW = 128
REL_BUCKETS = 32
N_CHIPS = 4

ADAM_LR = 0.001
ADAM_B1 = 0.9
ADAM_B2 = 0.999
ADAM_EPS = 1e-08
ADAM_WD = 0.01
ADAM_STEP = 10

VMEM_LIMIT_BYTES = 56 * 1024 * 1024
NEG = -1e30


def _params(*sem):
    return pltpu.CompilerParams(dimension_semantics=sem if sem else None, vmem_limit_bytes=VMEM_LIMIT_BYTES)


def _dot(a, b):
    return jnp.dot(a, b, preferred_element_type=F32)


def _dot_nt(a, b):
    return lax.dot_general(a, b, (((1,), (1,)), ((), ())), preferred_element_type=F32)


def _dot_tn(a, b):
    return lax.dot_general(a, b, (((0,), (0,)), ((), ())), preferred_element_type=F32)


def _b(x):
    return x.astype(BF16)


@jax.custom_vjp
def _bmm(a, b):
    return _dot(_b(a), _b(b))


def _bmm_fwd(a, b):
    return _bmm(a, b), (a, b)


def _bmm_bwd(res, g):
    a, b = res
    g16 = _b(g)
    return _dot_nt(g16, _b(b)).astype(a.dtype), _dot_tn(_b(a), g16).astype(b.dtype)


_bmm.defvjp(_bmm_fwd, _bmm_bwd)


@jax.custom_vjp
def _bmm_nt(a, b):
    return _dot_nt(_b(a), _b(b))


def _bmm_nt_fwd(a, b):
    return _bmm_nt(a, b), (a, b)


def _bmm_nt_bwd(res, g):
    a, b = res
    g16 = _b(g)
    return _dot(g16, _b(b)).astype(a.dtype), _dot_tn(g16, _b(a)).astype(b.dtype)


_bmm_nt.defvjp(_bmm_nt_fwd, _bmm_nt_bwd)


@jax.custom_vjp
def _bmm_tn(a, b):
    return _dot_tn(_b(a), _b(b))


def _bmm_tn_fwd(a, b):
    return _bmm_tn(a, b), (a, b)


def _bmm_tn_bwd(res, g):
    a, b = res
    g16 = _b(g)
    return _dot_nt(_b(b), g16).astype(a.dtype), _dot(_b(a), g16).astype(b.dtype)


_bmm_tn.defvjp(_bmm_tn_fwd, _bmm_tn_bwd)


def _split3(x):
    hi = _b(x)
    r = x - hi.astype(F32)
    mid = _b(r)
    lo = _b(r - mid.astype(F32))
    return hi, mid, lo


def _x_left_raw(m, x):
    hi, mid, lo = _split3(x)
    return _dot(m, hi) + _dot(m, mid) + _dot(m, lo)


def _x_left_t_raw(m, x):
    hi, mid, lo = _split3(x)
    return _dot_tn(m, hi) + _dot_tn(m, mid) + _dot_tn(m, lo)


def _x_right_raw(x, m):
    hi, mid, lo = _split3(x)
    return _dot(hi, m) + _dot(mid, m) + _dot(lo, m)


def _x_right_t_raw(x, m):
    hi, mid, lo = _split3(x)
    return _dot_nt(hi, m) + _dot_nt(mid, m) + _dot_nt(lo, m)


@jax.custom_vjp
def _xleft(m, x):
    return _x_left_raw(m, x)


_xleft.defvjp(lambda m, x: (_x_left_raw(m, x), m),
              lambda m, g: (jnp.zeros_like(m), _x_left_t_raw(m, g)))


@jax.custom_vjp
def _xright(x, m):
    return _x_right_raw(x, m)


_xright.defvjp(lambda x, m: (_x_right_raw(x, m), m),
               lambda m, g: (_x_right_t_raw(g, m), jnp.zeros_like(m)))


def _sigmoid(x):
    return 1.0 / (1.0 + jnp.exp(-x))


def _silu(x):
    return x * _sigmoid(x)


def _softplus(x):
    return jnp.maximum(x, 0.0) + jnp.log(1.0 + jnp.exp(-jnp.abs(x)))


def _rms(x):
    return x * lax.rsqrt(jnp.mean(x * x, axis=-1, keepdims=True) + EPS)


def _iota(shape, dim):
    return lax.broadcasted_iota(jnp.int32, shape, dim)


def _blockdiag64(n):
    return jnp.where(_iota((n, n), 0) // 64 == _iota((n, n), 1) // 64, 1.0, 0.0).astype(BF16)


def _group64_rms(x, seg_sum):
    ms = seg_sum(x * x) * (1.0 / 64.0)
    return x * lax.rsqrt(ms + EPS)


def _fold64(x):
    ax = x.ndim - 1
    w = x.shape[ax]
    lo = (_iota(x.shape, ax) % 128) < 64
    return x + jnp.where(lo, pltpu.roll(x, w - 64, ax), pltpu.roll(x, 64, ax))


def _tile(n, want):
    t = min(n, want)
    assert n % t == 0, (n, t)
    return t


def ffn_fwd(h, g, w1, w3, w2, layer, idx):
    t, d = h.shape
    nk, fs = w1.shape[0], w1.shape[-1]
    tm = _tile(t, 512)

    def body(h_ref, g_ref, w1_ref, w3_ref, w2_ref, o_ref, a_ref, b_ref, u_scr, acc):
        k = pl.program_id(1)

        @pl.when(k == 0)
        def _():
            u_scr[...] = _b(_rms(h_ref[...]) * g_ref[...])
            acc[...] = jnp.zeros_like(acc)

        u = u_scr[...]
        a = _dot(u, w1_ref[...])
        b = _dot(u, w3_ref[...])
        a_ref[...] = _b(a)
        b_ref[...] = _b(b)
        acc[...] += _dot(_b(_silu(a) * b), w2_ref[...])

        @pl.when(k == nk - 1)
        def _():
            o_ref[...] = h_ref[...] + 0.5 * acc[...]

    wspec = lambda r, c: pl.BlockSpec((None, None, None, r, c), lambda i, k: (k, layer, idx, 0, 0))
    return pl.pallas_call(
        body, name="ffn_fwd",
        grid=(t // tm, nk),
        in_specs=[pl.BlockSpec((tm, d), lambda i, k: (i, 0)), pl.BlockSpec((1, d), lambda i, k: (0, 0)),
                  wspec(d, fs), wspec(d, fs), wspec(fs, d)],
        out_specs=[pl.BlockSpec((tm, d), lambda i, k: (i, 0)),
                   pl.BlockSpec((None, tm, fs), lambda i, k: (k, i, 0)),
                   pl.BlockSpec((None, tm, fs), lambda i, k: (k, i, 0))],
        out_shape=[jax.ShapeDtypeStruct((t, d), F32), jax.ShapeDtypeStruct((nk, t, fs), BF16),
                   jax.ShapeDtypeStruct((nk, t, fs), BF16)],
        scratch_shapes=[pltpu.VMEM((tm, d), BF16), pltpu.VMEM((tm, d), F32)],
        compiler_params=_params("arbitrary", "arbitrary"),
    )(h, g, w1, w3, w2)


def ffn_bwd(h, dy, g, a_s, b_s, w1, w3, w2, layer, idx):
    t, d = h.shape
    nk, fs = w1.shape[0], w1.shape[-1]
    tm = _tile(t, 512)

    def body(h_ref, dy_ref, g_ref, a_ref, b_ref, w1_ref, w3_ref, w2_ref,
             dh_ref, u_ref, da_ref, db_ref, s_ref, dg_ref, dyh_scr, du_acc):
        i, k = pl.program_id(0), pl.program_id(1)

        @pl.when(k == 0)
        def _():
            dyh_scr[...] = _b(0.5 * dy_ref[...])
            du_acc[...] = jnp.zeros_like(du_acc)

        @pl.when((k == 0) & (i == 0))
        def _():
            dg_ref[...] = jnp.zeros_like(dg_ref)

        ds = _dot_nt(dyh_scr[...], w2_ref[...])
        a = a_ref[...].astype(F32)
        b = b_ref[...].astype(F32)
        sig = _sigmoid(a)
        sl = a * sig
        s_ref[...] = _b(sl * b)
        da = _b(ds * b * (sig * (1.0 + a * (1.0 - sig))))
        db = _b(ds * sl)
        da_ref[...] = da
        db_ref[...] = db
        du_acc[...] += _dot_nt(da, w1_ref[...]) + _dot_nt(db, w3_ref[...])

        @pl.when(k == nk - 1)
        def _():
            hh = h_ref[...]
            rstd = lax.rsqrt(jnp.mean(hh * hh, axis=-1, keepdims=True) + EPS)
            xh = hh * rstd
            gg = g_ref[...]
            u_ref[...] = _b(xh * gg)
            du = du_acc[...]
            dg_ref[...] += jnp.sum(du * xh, axis=0, keepdims=True)
            dxh = du * gg
            dh_ref[...] = dy_ref[...] + rstd * (dxh - xh * jnp.mean(dxh * xh, axis=-1, keepdims=True))

    wspec = lambda r, c: pl.BlockSpec((None, None, None, r, c), lambda i, k: (k, layer, idx, 0, 0))
    tok = pl.BlockSpec((tm, d), lambda i, k: (i, 0))
    hid = pl.BlockSpec((None, tm, fs), lambda i, k: (k, i, 0))
    return pl.pallas_call(
        body, name="ffn_bwd",
        grid=(t // tm, nk),
        in_specs=[tok, tok, pl.BlockSpec((1, d), lambda i, k: (0, 0)), hid, hid, wspec(d, fs), wspec(d, fs), wspec(fs, d)],
        out_specs=[tok, tok, hid, hid, hid, pl.BlockSpec((1, d), lambda i, k: (0, 0))],
        out_shape=[jax.ShapeDtypeStruct((t, d), F32), jax.ShapeDtypeStruct((t, d), BF16),
                   jax.ShapeDtypeStruct((nk, t, fs), BF16), jax.ShapeDtypeStruct((nk, t, fs), BF16),
                   jax.ShapeDtypeStruct((nk, t, fs), BF16), jax.ShapeDtypeStruct((1, d), F32)],
        scratch_shapes=[pltpu.VMEM((tm, d), BF16), pltpu.VMEM((tm, d), F32)],
        compiler_params=_params("arbitrary", "arbitrary"),
    )(h, dy, g, a_s, b_s, w1, w3, w2)


def wgrad_grouped_b(a, bs, scale=1.0):
    t, m = a.shape
    ng, _, n = bs.shape
    tk = _tile(t, 512)

    def body(a_ref, b_ref, o_ref):
        j = pl.program_id(1)

        @pl.when(j == 0)
        def _():
            o_ref[...] = jnp.zeros_like(o_ref)

        o_ref[...] += _dot_tn(_b(a_ref[...]), _b(b_ref[...]))

        if scale != 1.0:
            @pl.when(j == pl.num_programs(1) - 1)
            def _():
                o_ref[...] = o_ref[...] * scale

    return pl.pallas_call(
        body, name="wgrad_gb",
        grid=(ng, t // tk),
        in_specs=[pl.BlockSpec((tk, m), lambda k, j: (j, 0)), pl.BlockSpec((None, tk, n), lambda k, j: (k, j, 0))],
        out_specs=pl.BlockSpec((None, m, n), lambda k, j: (k, 0, 0)),
        out_shape=jax.ShapeDtypeStruct((ng, m, n), F32),
        compiler_params=_params("arbitrary", "arbitrary"),
    )(a, bs)


def wgrad_grouped_a(as_, b, scale=1.0):
    ng, t, m = as_.shape
    n = b.shape[1]
    tk = _tile(t, 512)

    def body(a_ref, b_ref, o_ref):
        j = pl.program_id(1)

        @pl.when(j == 0)
        def _():
            o_ref[...] = jnp.zeros_like(o_ref)

        o_ref[...] += _dot_tn(_b(a_ref[...]), _b(b_ref[...]))

        if scale != 1.0:
            @pl.when(j == pl.num_programs(1) - 1)
            def _():
                o_ref[...] = o_ref[...] * scale

    return pl.pallas_call(
        body, name="wgrad_ga",
        grid=(ng, t // tk),
        in_specs=[pl.BlockSpec((None, tk, m), lambda k, j: (k, j, 0)), pl.BlockSpec((tk, n), lambda k, j: (j, 0))],
        out_specs=pl.BlockSpec((None, m, n), lambda k, j: (k, 0, 0)),
        out_shape=jax.ShapeDtypeStruct((ng, m, n), F32),
        compiler_params=_params("arbitrary", "arbitrary"),
    )(as_, b)


def wgrad(a, b):
    t, m = a.shape
    n = b.shape[1]
    tk = _tile(t, 512)
    tn = _tile(n, 512)

    def body(a_ref, b_ref, o_ref):
        @pl.when(pl.program_id(1) == 0)
        def _():
            o_ref[...] = jnp.zeros_like(o_ref)

        o_ref[...] += _dot_tn(_b(a_ref[...]), _b(b_ref[...]))

    return pl.pallas_call(
        body, name="wgrad",
        grid=(n // tn, t // tk),
        in_specs=[pl.BlockSpec((tk, m), lambda c, j: (j, 0)), pl.BlockSpec((tk, tn), lambda c, j: (j, c))],
        out_specs=pl.BlockSpec((m, tn), lambda c, j: (0, c)),
        out_shape=jax.ShapeDtypeStruct((m, n), F32),
        compiler_params=_params("arbitrary", "arbitrary"),
    )(a, b)


def norm_mm(h, g, w):
    t, d = h.shape
    n = w.shape[1]
    tm = _tile(t, 512)
    tn = _tile(n, 512)

    def body(h_ref, g_ref, w_ref, o_ref, u_scr):
        @pl.when(pl.program_id(1) == 0)
        def _():
            u_scr[...] = _b(_rms(h_ref[...]) * g_ref[...])

        o_ref[...] = _dot(u_scr[...], w_ref[...])

    return pl.pallas_call(
        body, name="norm_mm",
        grid=(t // tm, n // tn),
        in_specs=[pl.BlockSpec((tm, d), lambda i, j: (i, 0)), pl.BlockSpec((1, d), lambda i, j: (0, 0)),
                  pl.BlockSpec((d, tn), lambda i, j: (0, j))],
        out_specs=pl.BlockSpec((tm, tn), lambda i, j: (i, j)),
        out_shape=jax.ShapeDtypeStruct((t, n), F32),
        scratch_shapes=[pltpu.VMEM((tm, d), BF16)],
        compiler_params=_params("arbitrary", "arbitrary"),
    )(h, g, w)


def norm_mm_bwd(h, g, w, dout, dres, scale=1.0):
    t, d = h.shape
    n = w.shape[1]
    tm = _tile(t, 512)
    tn = _tile(n, 512)
    nj = n // tn

    def body(h_ref, g_ref, w_ref, do_ref, dr_ref, dh_ref, u_ref, dg_ref, du_acc):
        i, j = pl.program_id(0), pl.program_id(1)

        @pl.when(j == 0)
        def _():
            du_acc[...] = jnp.zeros_like(du_acc)

        @pl.when((j == 0) & (i == 0))
        def _():
            dg_ref[...] = jnp.zeros_like(dg_ref)

        du_acc[...] += _dot_nt(_b(do_ref[...]), w_ref[...])

        @pl.when(j == nj - 1)
        def _():
            hh = h_ref[...]
            rstd = lax.rsqrt(jnp.mean(hh * hh, axis=-1, keepdims=True) + EPS)
            xh = hh * rstd
            gg = g_ref[...]
            u_ref[...] = _b(xh * gg)
            du = du_acc[...] * scale
            dg_ref[...] += jnp.sum(du * xh, axis=0, keepdims=True)
            dxh = du * gg
            dh_ref[...] = dr_ref[...] + rstd * (dxh - xh * jnp.mean(dxh * xh, axis=-1, keepdims=True))

    tok = pl.BlockSpec((tm, d), lambda i, j: (i, 0))
    return pl.pallas_call(
        body, name="norm_mm_bwd",
        grid=(t // tm, nj),
        in_specs=[tok, pl.BlockSpec((1, d), lambda i, j: (0, 0)), pl.BlockSpec((d, tn), lambda i, j: (0, j)),
                  pl.BlockSpec((tm, tn), lambda i, j: (i, j)), tok],
        out_specs=[tok, tok, pl.BlockSpec((1, d), lambda i, j: (0, 0))],
        out_shape=[jax.ShapeDtypeStruct((t, d), F32), jax.ShapeDtypeStruct((t, d), BF16),
                   jax.ShapeDtypeStruct((1, d), F32)],
        scratch_shapes=[pltpu.VMEM((tm, d), F32)],
        compiler_params=_params("arbitrary", "arbitrary"),
    )(h, g, w, dout, dres)


CONV_COLS = 512


def _conv_tile(ext, w, b):
    tm = ext.shape[0] - 8
    y = b + w[0:1] * ext[5:5 + tm] + w[1:2] * ext[6:6 + tm] + w[2:3] * ext[7:7 + tm] + w[3:4] * ext[8:8 + tm]
    return _silu(y)


def conv_fwd(zx, cw, cb, col0):
    t = zx.shape[0]
    c = cw.shape[1]
    tm = _tile(t, 512)
    cb0 = col0 // CONV_COLS

    def body(x_ref, w_ref, b_ref, o_ref, halo):
        @pl.when(pl.program_id(1) == 0)
        def _():
            halo[...] = jnp.zeros_like(halo)

        xt = x_ref[...]
        ext = jnp.concatenate([halo[...], xt], axis=0)
        o_ref[...] = _conv_tile(ext, w_ref[...], b_ref[...])
        halo[...] = xt[tm - 8:]

    return pl.pallas_call(
        body, name="conv_fwd",
        grid=(c // CONV_COLS, t // tm),
        in_specs=[pl.BlockSpec((tm, CONV_COLS), lambda j, i: (i, cb0 + j)),
                  pl.BlockSpec((8, CONV_COLS), lambda j, i: (0, j)), pl.BlockSpec((1, CONV_COLS), lambda j, i: (0, j))],
        out_specs=pl.BlockSpec((tm, CONV_COLS), lambda j, i: (i, j)),
        out_shape=jax.ShapeDtypeStruct((t, c), F32),
        scratch_shapes=[pltpu.VMEM((8, CONV_COLS), F32)],
        compiler_params=_params("arbitrary", "arbitrary"),
    )(zx, cw, cb)


def conv_bwd(dzx, zx, dxs, dbm, dcm, cw, cb, col0):
    t = zx.shape[0]
    c = cw.shape[1]
    tm = _tile(t, 512)
    nt = t // tm
    cb0 = col0 // CONV_COLS
    nxs = dxs.shape[1] // CONV_COLS
    hb = tm // 8

    def body(dzx_ref, x_ref, xh_ref, dxs_ref, db_ref, dc_ref, w_ref, b_ref, o_ref, dw_ref, dbias_ref, carry):
        j, i = pl.program_id(0), pl.program_id(1)
        ri = nt - 1 - i

        @pl.when(i == 0)
        def _():
            carry[...] = jnp.zeros_like(carry)
            dw_ref[...] = jnp.zeros_like(dw_ref)
            dbias_ref[...] = jnp.zeros_like(dbias_ref)

        dout = jnp.where(j < nxs, dxs_ref[...], jnp.where(j == nxs, db_ref[...], dc_ref[...]))
        hist = jnp.where(ri > 0, xh_ref[...], 0.0)
        ext = jnp.concatenate([hist, x_ref[...]], axis=0)
        _, vjp = jax.vjp(_conv_tile, ext, w_ref[...], b_ref[...])
        dext, dw, dbias = vjp(dout)
        tail = dext[tm:] + carry[...]
        o_ref[...] = jnp.concatenate([dext[8:tm], tail], axis=0)
        carry[...] = dext[0:8]
        dw_ref[...] += dw
        dbias_ref[...] += dbias

    return pl.pallas_call(
        body, name="conv_bwd",
        grid=(c // CONV_COLS, nt),
        in_specs=[pl.BlockSpec(memory_space=pl.ANY),
                  pl.BlockSpec((tm, CONV_COLS), lambda j, i: (nt - 1 - i, cb0 + j)),
                  pl.BlockSpec((8, CONV_COLS), lambda j, i: (jnp.maximum((nt - 1 - i) * hb - 1, 0), cb0 + j)),
                  pl.BlockSpec((tm, CONV_COLS), lambda j, i: (nt - 1 - i, jnp.minimum(j, nxs - 1))),
                  pl.BlockSpec((tm, CONV_COLS), lambda j, i: (nt - 1 - i, 0)),
                  pl.BlockSpec((tm, CONV_COLS), lambda j, i: (nt - 1 - i, 0)),
                  pl.BlockSpec((8, CONV_COLS), lambda j, i: (0, j)), pl.BlockSpec((1, CONV_COLS), lambda j, i: (0, j))],
        out_specs=[pl.BlockSpec((tm, CONV_COLS), lambda j, i: (nt - 1 - i, cb0 + j)),
                   pl.BlockSpec((8, CONV_COLS), lambda j, i: (0, j)), pl.BlockSpec((1, CONV_COLS), lambda j, i: (0, j))],
        out_shape=[jax.ShapeDtypeStruct(dzx.shape, F32), jax.ShapeDtypeStruct((8, c), F32),
                   jax.ShapeDtypeStruct((1, c), F32)],
        scratch_shapes=[pltpu.VMEM((8, CONV_COLS), F32)],
        input_output_aliases={0: 0},
        compiler_params=_params("arbitrary", "arbitrary"),
    )(dzx, zx, zx, dxs, dbm, dcm, cw, cb)


def _ssd_group(xs, bg, cg, dtraw, s0, bias, alog, dsk):
    L = xs.shape[0]
    causal = _iota((L, L), 0) >= _iota((L, L), 1)
    tril = jnp.where(causal, 1.0, 0.0).astype(BF16)
    dt = _softplus(dtraw + bias)
    a = -jnp.exp(alog)
    acum = _xleft(tril, dt * a)
    acum_t = acum.T
    dt_t = dt.T
    cb = _bmm_nt(cg, bg)
    lo = _iota((L, 128), 1) < 64
    lo_row = _iota((1, 128), 1) < 64
    lo_col = _iota((128, 1), 0) < 64
    alast = acum[L - 1:L, :]
    ys, s1s = [], []
    for q in range(4):
        xp = xs[:, q * 128:(q + 1) * 128]
        sp = s0[q * 128:(q + 1) * 128, :]
        yd, ec, wc, el = [], [], [], []
        for j in range(2):
            r = 2 * q + j
            ac = acum[:, r:r + 1]
            decay = jnp.exp(jnp.where(causal, ac - acum_t[r:r + 1, :], NEG))
            yd.append(_bmm(cb * decay * dt_t[r:r + 1, :], xp))
            ec.append(jnp.exp(ac))
            al = alast[:, r:r + 1]
            wc.append(jnp.exp(al - ac) * dt[:, r:r + 1])
            el.append(jnp.exp(al))
        y_off = _bmm_nt(cg, sp) * jnp.where(lo, ec[0], ec[1])
        dsel = jnp.where(lo_row, dsk[:, 2 * q:2 * q + 1], dsk[:, 2 * q + 1:2 * q + 2])
        ys.append(jnp.where(lo, yd[0], yd[1]) + y_off + dsel * xp)
        xw = xp * jnp.where(lo, wc[0], wc[1])
        s1s.append(sp * jnp.where(lo_col, el[0], el[1]) + _bmm_tn(xw, bg))
    return jnp.concatenate(ys, axis=1), jnp.concatenate(s1s, axis=0)


def ssd_fwd(xc, zx, bias, alog, dsk, dt_col0):
    t = xc.shape[0]
    L = _tile(t, SSM_CHUNK)
    nc = t // L
    g = SSM_GROUPS
    dtb = dt_col0 // 128

    def body(xs_ref, b_ref, c_ref, dt_ref, bias_ref, alog_ref, dsk_ref, y_ref, st_ref, state):
        c, gi = pl.program_id(0), pl.program_id(1)

        @pl.when(c == 0)
        def _():
            state[gi] = jnp.zeros((512, 128), F32)

        s0 = state[gi]
        st_ref[...] = s0
        y, s1 = _ssd_group(xs_ref[...], b_ref[...], c_ref[...], dt_ref[...], s0,
                           bias_ref[...], alog_ref[...], dsk_ref[...])
        y_ref[...] = y
        state[gi] = s1

    vec = pl.BlockSpec((1, 128), lambda c, gi: (0, gi))
    return pl.pallas_call(
        body, name="ssd_fwd",
        grid=(nc, g),
        in_specs=[pl.BlockSpec((L, 512), lambda c, gi: (c, gi)),
                  pl.BlockSpec((L, 128), lambda c, gi: (c, 16 + gi)),
                  pl.BlockSpec((L, 128), lambda c, gi: (c, 20 + gi)),
                  pl.BlockSpec((L, 128), lambda c, gi: (c, dtb + gi)), vec, vec, vec],
        out_specs=[pl.BlockSpec((L, 512), lambda c, gi: (c, gi)),
                   pl.BlockSpec((None, None, 512, 128), lambda c, gi: (c, gi, 0, 0))],
        out_shape=[jax.ShapeDtypeStruct((t, 2048), F32), jax.ShapeDtypeStruct((nc, g, 512, 128), F32)],
        scratch_shapes=[pltpu.VMEM((g, 512, 128), F32)],
        compiler_params=_params("arbitrary", "arbitrary"),
    )(xc, xc, xc, zx, bias, alog, dsk)


def ssd_bwd(dzx, dy, xc, zx, states, bias, alog, dsk, dt_col0):
    t = xc.shape[0]
    L = _tile(t, SSM_CHUNK)
    nc = t // L
    g = SSM_GROUPS
    dtb = dt_col0 // 128

    def body(dzx_ref, dy_ref, xs_ref, b_ref, c_ref, dt_ref, st_ref, bias_ref, alog_ref, dsk_ref,
             ddt_ref, dxs_ref, db_ref, dc_ref, dbias_ref, dalog_ref, ddsk_ref, dstate):
        c, gi = pl.program_id(0), pl.program_id(1)

        @pl.when(c == 0)
        def _():
            dstate[gi] = jnp.zeros((512, 128), F32)

        @pl.when((c == 0) & (gi == 0))
        def _():
            dbias_ref[...] = jnp.zeros_like(dbias_ref)
            dalog_ref[...] = jnp.zeros_like(dalog_ref)
            ddsk_ref[...] = jnp.zeros_like(ddsk_ref)

        _, vjp = jax.vjp(_ssd_group, xs_ref[...], b_ref[...], c_ref[...], dt_ref[...], st_ref[...],
                         bias_ref[...], alog_ref[...], dsk_ref[...])
        dxs, db, dc, ddt, ds0, dbias, dalog, ddsk = vjp((dy_ref[...], dstate[gi]))
        dxs_ref[...] = dxs
        db_ref[...] = db
        dc_ref[...] = dc
        ddt_ref[...] = ddt
        dstate[gi] = ds0
        dbias_ref[gi] += dbias
        dalog_ref[gi] += dalog
        ddsk_ref[gi] += ddsk

    rc = lambda c: nc - 1 - c
    vec = pl.BlockSpec((1, 128), lambda c, gi: (0, gi))
    acc = pl.BlockSpec((g, 1, 128), lambda c, gi: (0, 0, 0))
    return pl.pallas_call(
        body, name="ssd_bwd",
        grid=(nc, g),
        in_specs=[pl.BlockSpec(memory_space=pl.ANY),
                  pl.BlockSpec((L, 512), lambda c, gi: (rc(c), gi)),
                  pl.BlockSpec((L, 512), lambda c, gi: (rc(c), gi)),
                  pl.BlockSpec((L, 128), lambda c, gi: (rc(c), 16 + gi)),
                  pl.BlockSpec((L, 128), lambda c, gi: (rc(c), 20 + gi)),
                  pl.BlockSpec((L, 128), lambda c, gi: (rc(c), dtb + gi)),
                  pl.BlockSpec((None, None, 512, 128), lambda c, gi: (rc(c), gi, 0, 0)), vec, vec, vec],
        out_specs=[pl.BlockSpec((L, 128), lambda c, gi: (rc(c), dtb + gi)),
                   pl.BlockSpec((L, 512), lambda c, gi: (rc(c), gi)),
                   pl.BlockSpec((L, 128), lambda c, gi: (rc(c), gi)),
                   pl.BlockSpec((L, 128), lambda c, gi: (rc(c), gi)), acc, acc, acc],
        out_shape=[jax.ShapeDtypeStruct(dzx.shape, F32), jax.ShapeDtypeStruct((t, 2048), F32),
                   jax.ShapeDtypeStruct((t, 512), F32), jax.ShapeDtypeStruct((t, 512), F32),
                   jax.ShapeDtypeStruct((g, 1, 128), F32), jax.ShapeDtypeStruct((g, 1, 128), F32),
                   jax.ShapeDtypeStruct((g, 1, 128), F32)],
        scratch_shapes=[pltpu.VMEM((g, 512, 128), F32)],
        input_output_aliases={0: 0},
        compiler_params=_params("arbitrary", "arbitrary"),
    )(dzx, dy, xc, xc, xc, zx, states, bias, alog, dsk)


def _gate_tile(y, z, gn):
    gated = y * _silu(z)
    parts = [_rms(gated[:, k * 512:(k + 1) * 512]) for k in range(SSM_GROUPS)]
    return jnp.concatenate(parts, axis=1) * gn


def gate_out_fwd(h, y, zx, gn, w_out):
    t, d = h.shape
    di = y.shape[1]
    tm = _tile(t, 256)

    def body(h_ref, y_ref, z_ref, gn_ref, w_ref, o_ref):
        yn = _gate_tile(y_ref[...], z_ref[...], gn_ref[...])
        o_ref[...] = h_ref[...] + _dot(_b(yn), w_ref[...])

    return pl.pallas_call(
        body, name="gate_out_fwd",
        grid=(t // tm,),
        in_specs=[pl.BlockSpec((tm, d), lambda i: (i, 0)), pl.BlockSpec((tm, di), lambda i: (i, 0)),
                  pl.BlockSpec((tm, di), lambda i: (i, 0)), pl.BlockSpec((1, di), lambda i: (0, 0)),
                  pl.BlockSpec((di, d), lambda i: (0, 0))],
        out_specs=pl.BlockSpec((tm, d), lambda i: (i, 0)),
        out_shape=jax.ShapeDtypeStruct((t, d), F32),
        compiler_params=_params("arbitrary"),
    )(h, y, zx, gn, w_out)


def gate_out_bwd(dy, y, zx, gn, w_out, n_zx):
    t, d = dy.shape
    di = y.shape[1]
    tm = _tile(t, 256)

    def body(dy_ref, y_ref, z_ref, gn_ref, w_ref, dz_ref, dys_ref, yn_ref, dgn_ref):
        @pl.when(pl.program_id(0) == 0)
        def _():
            dgn_ref[...] = jnp.zeros_like(dgn_ref)

        yn, vjp = jax.vjp(_gate_tile, y_ref[...], z_ref[...], gn_ref[...])
        dyn = _dot_nt(_b(dy_ref[...]), w_ref[...])
        dys, dz, dgn = vjp(dyn)
        yn_ref[...] = _b(yn)
        dys_ref[...] = dys
        dz_ref[...] = dz
        dgn_ref[...] += dgn

    return pl.pallas_call(
        body, name="gate_out_bwd",
        grid=(t // tm,),
        in_specs=[pl.BlockSpec((tm, d), lambda i: (i, 0)), pl.BlockSpec((tm, di), lambda i: (i, 0)),
                  pl.BlockSpec((tm, di), lambda i: (i, 0)), pl.BlockSpec((1, di), lambda i: (0, 0)),
                  pl.BlockSpec((di, d), lambda i: (0, 0))],
        out_specs=[pl.BlockSpec((tm, di), lambda i: (i, 0)), pl.BlockSpec((tm, di), lambda i: (i, 0)),
                   pl.BlockSpec((tm, di), lambda i: (i, 0)), pl.BlockSpec((1, di), lambda i: (0, 0))],
        out_shape=[jax.ShapeDtypeStruct((t, n_zx), F32), jax.ShapeDtypeStruct((t, di), F32),
                   jax.ShapeDtypeStruct((t, di), BF16), jax.ShapeDtypeStruct((1, di), F32)],
        compiler_params=_params("arbitrary"),
    )(dy, y, zx, gn, w_out)


def _attn_block(qp, kvp, kvc, biasm, sinks, qg, kg, w_o, first):
    nq = qp.shape[0]
    n_pairs = qp.shape[1] // 128
    heads_per_kv = (2 * n_pairs) // 2
    seg = functools.partial(_xright, m=_blockdiag64(128))
    scale = ATT_HEAD_DIM ** -0.5
    qi = _iota((nq, 2 * nq), 0) + nq
    kj = _iota((nq, 2 * nq), 1)
    dist = qi - kj
    valid = (dist >= 0) & (dist < ATT_WINDOW) & (jnp.logical_not(first) | (kj >= nq))
    lo = _iota((nq, 128), 1) < 64
    kv = jnp.concatenate([kvp, kvc], axis=0)
    kn = [_group64_rms(kv[:, h * 128:(h + 1) * 128], seg) * kg for h in range(2)]
    vv = [kv[:, 256 + h * 128:256 + (h + 1) * 128] for h in range(2)]
    outs = []
    for p in range(n_pairs):
        kvh = (2 * p) // heads_per_kv
        qn = _group64_rms(qp[:, p * 128:(p + 1) * 128], seg) * qg
        o2 = []
        for j in range(2):
            h = 2 * p + j
            qm = jnp.where(lo, qn, 0.0) if j == 0 else jnp.where(lo, 0.0, qn)
            s = _bmm_nt(qm, kn[kvh]) * scale + biasm[h]
            s = jnp.where(valid, s, NEG)
            sink = sinks[:, h:h + 1]
            m = jnp.maximum(jnp.max(s, axis=-1, keepdims=True), sink)
            pexp = jnp.exp(s - m)
            den = jnp.sum(pexp, axis=-1, keepdims=True) + jnp.exp(sink - m)
            o2.append(_bmm(pexp / den, vv[kvh]))
        outs.append(jnp.where(lo, o2[0], o2[1]))
    o = jnp.concatenate(outs, axis=1)
    return _bmm(o, w_o), o


def attn_fwd(h, qp, kvd, biasm, sinks, qg, kg, w_o):
    t, d = h.shape
    nq = ATT_WINDOW
    nb = t // nq
    nh = biasm.shape[0]

    def body(h_ref, q_ref, kp_ref, kc_ref, bias_ref, s_ref, qg_ref, kg_ref, w_ref, o_ref):
        out, _ = _attn_block(q_ref[...], kp_ref[...], kc_ref[...], bias_ref[...], s_ref[...], qg_ref[...],
                             kg_ref[...], w_ref[...], pl.program_id(0) == 0)
        o_ref[...] = h_ref[...] + out

    vec = pl.BlockSpec((1, 128), lambda i: (0, 0))
    return pl.pallas_call(
        body, name="attn_fwd",
        grid=(nb,),
        in_specs=[pl.BlockSpec((nq, d), lambda i: (i, 0)), pl.BlockSpec((nq, nh * 64), lambda i: (i, 0)),
                  pl.BlockSpec((nq, 512), lambda i: (jnp.maximum(i - 1, 0), 0)),
                  pl.BlockSpec((nq, 512), lambda i: (i, 0)),
                  pl.BlockSpec((nh, nq, 2 * nq), lambda i: (0, 0, 0)), vec, vec, vec,
                  pl.BlockSpec((nh * 64, d), lambda i: (0, 0))],
        out_specs=pl.BlockSpec((nq, d), lambda i: (i, 0)),
        out_shape=jax.ShapeDtypeStruct((t, d), F32),
        compiler_params=_params("arbitrary"),
    )(h, qp, kvd, kvd, biasm, sinks, qg, kg, w_o)


def attn_bwd(dy, qp, kvd, biasm, sinks, qg, kg, w_o):
    t, d = dy.shape
    nq = ATT_WINDOW
    nb = t // nq
    nh = biasm.shape[0]

    def body(dy_ref, q_ref, kp_ref, kc_ref, bias_ref, s_ref, qg_ref, kg_ref, w_ref,
             dq_ref, dkv_ref, o_ref, dbias_ref, ds_ref, dqg_ref, dkg_ref, carry):
        i = pl.program_id(0)

        @pl.when(i == 0)
        def _():
            carry[...] = jnp.zeros_like(carry)
            dbias_ref[...] = jnp.zeros_like(dbias_ref)
            ds_ref[...] = jnp.zeros_like(ds_ref)
            dqg_ref[...] = jnp.zeros_like(dqg_ref)
            dkg_ref[...] = jnp.zeros_like(dkg_ref)

        @pl.when(i < nb)
        def _():
            fn = functools.partial(_attn_block, w_o=w_ref[...], first=(i == 0))
            (_, o), vjp = jax.vjp(fn, q_ref[...], kp_ref[...], kc_ref[...], bias_ref[...], s_ref[...],
                                  qg_ref[...], kg_ref[...])
            dq, dkp, dkc, dbias, dsk, dqg, dkg = vjp((dy_ref[...], jnp.zeros((nq, nh * 64), F32)))
            dq_ref[...] = dq
            o_ref[...] = _b(o)
            dkv_ref[...] = _fold64(carry[...] + dkp)
            carry[...] = dkc
            dbias_ref[...] += dbias
            ds_ref[...] += dsk
            dqg_ref[...] += _fold64(dqg)
            dkg_ref[...] += _fold64(dkg)

        @pl.when(i == nb)
        def _():
            dkv_ref[...] = _fold64(carry[...])

    cl = lambda i: jnp.minimum(i, nb - 1)
    vec = pl.BlockSpec((1, 128), lambda i: (0, 0))
    return pl.pallas_call(
        body, name="attn_bwd",
        grid=(nb + 1,),
        in_specs=[pl.BlockSpec((nq, d), lambda i: (cl(i), 0)), pl.BlockSpec((nq, nh * 64), lambda i: (cl(i), 0)),
                  pl.BlockSpec((nq, 512), lambda i: (jnp.maximum(cl(i) - 1, 0), 0)),
                  pl.BlockSpec((nq, 512), lambda i: (cl(i), 0)),
                  pl.BlockSpec((nh, nq, 2 * nq), lambda i: (0, 0, 0)), vec, vec, vec,
                  pl.BlockSpec((nh * 64, d), lambda i: (0, 0))],
        out_specs=[pl.BlockSpec((nq, nh * 64), lambda i: (cl(i), 0)),
                   pl.BlockSpec((nq, 512), lambda i: (jnp.maximum(i - 1, 0), 0)),
                   pl.BlockSpec((nq, nh * 64), lambda i: (cl(i), 0)),
                   pl.BlockSpec((nh, nq, 2 * nq), lambda i: (0, 0, 0)), vec, vec, vec],
        out_shape=[jax.ShapeDtypeStruct((t, nh * 64), F32), jax.ShapeDtypeStruct((t, 512), F32),
                   jax.ShapeDtypeStruct((t, nh * 64), BF16), jax.ShapeDtypeStruct((nh, nq, 2 * nq), F32),
                   jax.ShapeDtypeStruct((1, 128), F32), jax.ShapeDtypeStruct((1, 128), F32),
                   jax.ShapeDtypeStruct((1, 128), F32)],
        scratch_shapes=[pltpu.VMEM((nq, 512), F32)],
        compiler_params=_params("arbitrary"),
    )(dy, qp, kvd, kvd, biasm, sinks, qg, kg, w_o)


def _t5_buckets():
    nq = ATT_WINDOW
    dist = (np.arange(nq)[:, None] + nq) - np.arange(2 * nq)[None, :]
    n = np.maximum(dist, 0)
    max_exact = REL_BUCKETS // 2
    nf = np.maximum(n, 1).astype(np.float32)
    large = max_exact + (np.log(nf / max_exact) / math.log(ATT_WINDOW / max_exact)
                         * (REL_BUCKETS - max_exact)).astype(np.int32)
    large = np.minimum(large, REL_BUCKETS - 1)
    return np.where(n < max_exact, n, large).astype(np.int32)


def rel_bias_bwd(dbias, buckets):
    nh = dbias.shape[0]

    def body(db_ref, bk_ref, o_ref):
        bk = bk_ref[...]
        lane = _iota((1, 128), 1)
        row = _iota((REL_BUCKETS, 128), 0)
        acc = jnp.zeros((REL_BUCKETS, 128), F32)
        for h in range(nh):
            dbh = db_ref[h]
            for b in range(REL_BUCKETS):
                v = jnp.sum(jnp.where(bk == b, dbh, 0.0))
                acc = acc + jnp.where((row == b) & (lane == h), v, 0.0)
        o_ref[...] = acc

    return pl.pallas_call(
        body, name="rel_bias_bwd",
        out_shape=jax.ShapeDtypeStruct((REL_BUCKETS, 128), F32),
        compiler_params=_params(),
    )(dbias, buckets)


def loss_head(y, target):
    t, d = y.shape
    tm = _tile(t, 512)

    def body(y_ref, t_ref, l_ref, dy_ref):
        @pl.when(pl.program_id(0) == 0)
        def _():
            l_ref[...] = jnp.zeros_like(l_ref)

        e = y_ref[...] - t_ref[...]
        l_ref[...] += 0.5 * jnp.sum(jnp.mean(e * e, axis=-1, keepdims=True), axis=0, keepdims=True)
        dy_ref[...] = e * (1.0 / d)

    return pl.pallas_call(
        body, name="loss_head",
        grid=(t // tm,),
        in_specs=[pl.BlockSpec((tm, d), lambda i: (i, 0)), pl.BlockSpec((tm, d), lambda i: (i, 0))],
        out_specs=[pl.BlockSpec((1, 1), lambda i: (0, 0)), pl.BlockSpec((tm, d), lambda i: (i, 0))],
        out_shape=[jax.ShapeDtypeStruct((1, 1), F32), jax.ShapeDtypeStruct((t, d), F32)],
        compiler_params=_params("arbitrary"),
    )(y, target)


def adamw(w, g, m, v):
    r, c = w.shape
    tr = r if r <= 512 else _tile(r, 256)

    def body(w_ref, g_ref, m_ref, v_ref, d_ref, nm_ref, nv_ref):
        gg = g_ref[...]
        nm = ADAM_B1 * m_ref[...] + (1.0 - ADAM_B1) * gg
        nv = ADAM_B2 * v_ref[...] + (1.0 - ADAM_B2) * (gg * gg)
        m_hat = nm / (1.0 - ADAM_B1 ** ADAM_STEP)
        v_hat = nv / (1.0 - ADAM_B2 ** ADAM_STEP)
        d_ref[...] = -ADAM_LR * (m_hat / (jnp.sqrt(v_hat) + ADAM_EPS) + ADAM_WD * w_ref[...])
        nm_ref[...] = nm
        nv_ref[...] = nv

    spec = pl.BlockSpec((tr, c), lambda i: (i, 0))
    shp = jax.ShapeDtypeStruct((r, c), F32)
    return pl.pallas_call(
        body, name="adamw",
        grid=(r // tr,),
        in_specs=[spec] * 4, out_specs=[spec] * 3, out_shape=[shp] * 3,
        compiler_params=_params("arbitrary"),
    )(w, g, m, v)


def _my_pos():
    return lax.axis_index("x"), lax.axis_index("y"), lax.axis_index("c")


def _other_chips(x, y):
    return [(1 - x, y), (x, 1 - y), (1 - x, 1 - y)]


def _chip_id(x, y):
    return 2 * x + y


def gather_weights(bufs):
    n = len(bufs)

    def body(*refs):
        outs = refs[n:2 * n]
        send_sems, recv_sems = refs[2 * n:]
        x, y, c = _my_pos()
        sibling = (x, y, 1 - c)
        chips = _other_chips(x, y)

        def copy(p, k, chip, half, to):
            blk = outs[p].at[_chip_id(*chip), half]
            return pltpu.make_async_remote_copy(
                src_ref=blk, dst_ref=blk, send_sem=send_sems.at[p, k], recv_sem=recv_sems.at[p, k],
                device_id=to, device_id_type=MESH)

        first = [[copy(p, j, (x, y), c, (*chip, c)) for j, chip in enumerate(chips)] for p in range(n)]
        for p in range(n):
            for cp in first[p]:
                cp.start()
        passed = [[copy(p, 3 + j, chip, c, sibling) for j, chip in enumerate(chips)] for p in range(n)]
        for p in range(n):
            for j, chip in enumerate(chips):
                copy(p, j, chip, c, (x, y, c)).wait_recv()
                passed[p][j].start()
        for p in range(n):
            for j, chip in enumerate(chips):
                copy(p, 3 + j, chip, 1 - c, (x, y, c)).wait_recv()
        for p in range(n):
            for cp in first[p] + passed[p]:
                cp.wait_send()

    any_spec = pl.BlockSpec(memory_space=pl.ANY)
    return pl.pallas_call(
        body, name="gather_weights",
        in_specs=[any_spec] * n, out_specs=[any_spec] * n,
        out_shape=[jax.ShapeDtypeStruct(b.shape, b.dtype) for b in bufs],
        scratch_shapes=[pltpu.SemaphoreType.DMA((n, 6)), pltpu.SemaphoreType.DMA((n, 6))],
        input_output_aliases={p: p for p in range(n)},
    )(*bufs)


def allreduce_small(v):
    r, c = v.shape

    def body(v_ref, o_ref, buf, send_sems, recv_sems):
        x, y, cc = _my_pos()
        me = 4 * x + 2 * y + cc
        buf[me] = v_ref[...]
        copies = []
        for k in range(1, 8):
            dx, dy, dc = (k >> 2) & 1, (k >> 1) & 1, k & 1
            peer = (x ^ dx, y ^ dy, cc ^ dc)
            cp = pltpu.make_async_remote_copy(
                src_ref=v_ref, dst_ref=buf.at[me], send_sem=send_sems.at[k - 1], recv_sem=recv_sems.at[k - 1],
                device_id=peer, device_id_type=MESH)
            cp.start()
            copies.append(cp)
        for cp in copies:
            cp.wait_recv()
        for cp in copies:
            cp.wait_send()
        acc = buf[0]
        for k in range(1, 8):
            acc = acc + buf[k]
        o_ref[...] = acc

    vm = pl.BlockSpec(memory_space=pltpu.VMEM)
    return pl.pallas_call(
        body, name="allreduce_small",
        in_specs=[vm], out_specs=vm,
        out_shape=jax.ShapeDtypeStruct((r, c), F32),
        scratch_shapes=[pltpu.VMEM((8, r, c), F32), pltpu.SemaphoreType.DMA((7,)), pltpu.SemaphoreType.DMA((7,))],
    )(v)


def exchange_sibling_halves(grads):
    n = len(grads)

    def body(*refs):
        ins, outs = refs[:n], refs[n:2 * n]
        send_sems, recv_sems = refs[2 * n:]
        x, y, c = _my_pos()
        copies = []
        for p in range(n):
            cp = pltpu.make_async_remote_copy(
                src_ref=ins[p].at[:, 1 - c], dst_ref=outs[p], send_sem=send_sems.at[p], recv_sem=recv_sems.at[p],
                device_id=(x, y, 1 - c), device_id_type=MESH)
            cp.start()
            copies.append(cp)
        for cp in copies:
            cp.wait_recv()
        for cp in copies:
            cp.wait_send()

    any_spec = pl.BlockSpec(memory_space=pl.ANY)
    return pl.pallas_call(
        body, name="exchange_sibling_halves",
        in_specs=[any_spec] * n, out_specs=[any_spec] * n,
        out_shape=[jax.ShapeDtypeStruct((g.shape[0],) + g.shape[2:], g.dtype) for g in grads],
        scratch_shapes=[pltpu.SemaphoreType.DMA((n,)), pltpu.SemaphoreType.DMA((n,))],
    )(*grads)


def exchange_chip_partials(parts):
    n = len(parts)

    def body(*refs):
        ins, outs = refs[:n], refs[n:2 * n]
        send_sems, recv_sems = refs[2 * n:]
        x, y, c = _my_pos()
        chips = _other_chips(x, y)
        copies = []
        for p in range(n):
            for j, chip in enumerate(chips):
                cp = pltpu.make_async_remote_copy(
                    src_ref=ins[p].at[_chip_id(*chip)], dst_ref=outs[p].at[j],
                    send_sem=send_sems.at[p, j], recv_sem=recv_sems.at[p, j],
                    device_id=(*chip, c), device_id_type=MESH)
                cp.start()
                copies.append(cp)
        for cp in copies:
            cp.wait_recv()
        for cp in copies:
            cp.wait_send()

    any_spec = pl.BlockSpec(memory_space=pl.ANY)
    return pl.pallas_call(
        body, name="exchange_chip_partials",
        in_specs=[any_spec] * n, out_specs=[any_spec] * n,
        out_shape=[jax.ShapeDtypeStruct((3,) + s.shape[1:], s.dtype) for s in parts],
        scratch_shapes=[pltpu.SemaphoreType.DMA((n, 3)), pltpu.SemaphoreType.DMA((n, 3))],
    )(*parts)


def share_with_sibling(bufs):
    n = len(bufs)

    def body(*refs):
        outs = refs[n:2 * n]
        send_sems, recv_sems = refs[2 * n:]
        x, y, c = _my_pos()
        copies = []
        for p in range(n):
            cp = pltpu.make_async_remote_copy(
                src_ref=outs[p].at[c], dst_ref=outs[p].at[c], send_sem=send_sems.at[p], recv_sem=recv_sems.at[p],
                device_id=(x, y, 1 - c), device_id_type=MESH)
            cp.start()
            copies.append(cp)
        for p in range(n):
            pltpu.make_async_remote_copy(
                src_ref=outs[p].at[1 - c], dst_ref=outs[p].at[1 - c], send_sem=send_sems.at[p],
                recv_sem=recv_sems.at[p], device_id=(x, y, 1 - c), device_id_type=MESH).wait_recv()
        for cp in copies:
            cp.wait_send()

    any_spec = pl.BlockSpec(memory_space=pl.ANY)
    return pl.pallas_call(
        body, name="share_with_sibling",
        in_specs=[any_spec] * n, out_specs=[any_spec] * n,
        out_shape=[jax.ShapeDtypeStruct(b.shape, b.dtype) for b in bufs],
        scratch_shapes=[pltpu.SemaphoreType.DMA((n,)), pltpu.SemaphoreType.DMA((n,))],
        input_output_aliases={p: p for p in range(n)},
    )(*bufs)


def add_sibling(g, recv, half):
    _, _, r, c = g.shape
    tr = _tile(r, 256) if r % 256 == 0 else r

    def body(half_ref, g_ref, r_ref, o32_ref, o16_ref):
        s = g_ref[...] + r_ref[...]
        o32_ref[...] = s
        o16_ref[...] = _b(s)

    return pl.pallas_call(
        body, name="add_sibling",
        grid_spec=pltpu.PrefetchScalarGridSpec(
            num_scalar_prefetch=1, grid=(N_CHIPS, r // tr),
            in_specs=[pl.BlockSpec((None, None, tr, c), lambda k, i, hf: (k, hf[0], i, 0)),
                      pl.BlockSpec((None, tr, c), lambda k, i, hf: (k, i, 0))],
            out_specs=[pl.BlockSpec((None, tr, c), lambda k, i, hf: (k, i, 0)),
                       pl.BlockSpec((None, tr, c), lambda k, i, hf: (k, i, 0))]),
        out_shape=[jax.ShapeDtypeStruct((N_CHIPS, r, c), F32), jax.ShapeDtypeStruct((N_CHIPS, r, c), BF16)],
        compiler_params=_params("arbitrary", "arbitrary"),
    )(half, g, recv)


def add_chip_partials(p32, recv, pos):
    _, r, c = p32.shape
    tr = _tile(r, 256) if r % 256 == 0 else r

    def body(pos_ref, p_ref, r_ref, o_ref):
        acc = p_ref[...]
        for j in range(N_CHIPS - 1):
            acc = acc + r_ref[j].astype(F32)
        o_ref[...] = acc

    return pl.pallas_call(
        body, name="add_chip_partials",
        grid_spec=pltpu.PrefetchScalarGridSpec(
            num_scalar_prefetch=1, grid=(r // tr,),
            in_specs=[pl.BlockSpec((None, tr, c), lambda i, ps: (ps[0], i, 0)),
                      pl.BlockSpec((N_CHIPS - 1, tr, c), lambda i, ps: (0, i, 0))],
            out_specs=pl.BlockSpec((None, tr, c), lambda i, ps: (ps[1], i, 0))),
        out_shape=jax.ShapeDtypeStruct((2, r, c), F32),
        compiler_params=_params("arbitrary"),
    )(pos, p32, recv)


def cast_into_gather(w, pos):
    _, r, c = w.shape
    tr = _tile(r, 256) if r % 256 == 0 else r

    def body(pos_ref, w_ref, o_ref):
        o_ref[...] = _b(w_ref[...])

    return pl.pallas_call(
        body, name="cast_into_gather",
        grid_spec=pltpu.PrefetchScalarGridSpec(
            num_scalar_prefetch=1, grid=(2, r // tr),
            in_specs=[pl.BlockSpec((None, tr, c), lambda hf, i, ps: (hf, i, 0))],
            out_specs=pl.BlockSpec((None, None, tr, c), lambda hf, i, ps: (ps[0], hf, i, 0))),
        out_shape=jax.ShapeDtypeStruct((N_CHIPS, 2, r, c), BF16),
        compiler_params=_params("arbitrary", "arbitrary"),
    )(pos, w)


def build_bias(rel, buckets):
    nb, nh = rel.shape

    def body(rel_ref, bk_ref, o_ref):
        bk = bk_ref[...]
        for h in range(nh):
            acc = jnp.zeros(bk.shape, F32)
            for b in range(nb):
                acc = jnp.where(bk == b, rel_ref[b, h], acc)
            o_ref[h] = acc

    return pl.pallas_call(
        body, name="build_bias",
        in_specs=[pl.BlockSpec(memory_space=pltpu.SMEM), pl.BlockSpec(memory_space=pltpu.VMEM)],
        out_specs=pl.BlockSpec(memory_space=pltpu.VMEM),
        out_shape=jax.ShapeDtypeStruct((nh,) + buckets.shape, F32),
        compiler_params=_params(),
    )(rel, buckets)


SMALL_ROWS = 256


def kernel(x, ffn_norm, ffn_w1, ffn_w3, ffn_w2, ssm_norm, ssm_w_in, ssm_conv_w, ssm_conv_b, ssm_dt_bias, ssm_a_log, ssm_d, ssm_gate_norm, ssm_w_out, kv_norm, w_kv, k_norm, attn_norm, w_q, q_norm, sinks, w_o, rel_bias, loss_target, m_ffn_norm, m_ffn_w1, m_ffn_w3, m_ffn_w2, m_ssm_norm, m_ssm_w_in, m_ssm_conv_w, m_ssm_conv_b, m_ssm_dt_bias, m_ssm_a_log, m_ssm_d, m_ssm_gate_norm, m_ssm_w_out, m_kv_norm, m_w_kv, m_k_norm, m_attn_norm, m_w_q, m_q_norm, m_sinks, m_w_o, m_rel_bias, v_ffn_norm, v_ffn_w1, v_ffn_w3, v_ffn_w2, v_ssm_norm, v_ssm_w_in, v_ssm_conv_w, v_ssm_conv_b, v_ssm_dt_bias, v_ssm_a_log, v_ssm_d, v_ssm_gate_norm, v_ssm_w_out, v_kv_norm, v_w_kv, v_k_norm, v_attn_norm, v_w_q, v_q_norm, v_sinks, v_w_o, v_rel_bias):
    weights = dict(ffn_norm=ffn_norm, ffn_w1=ffn_w1, ffn_w3=ffn_w3, ffn_w2=ffn_w2, ssm_norm=ssm_norm,
                   ssm_w_in=ssm_w_in, ssm_conv_w=ssm_conv_w, ssm_conv_b=ssm_conv_b, ssm_dt_bias=ssm_dt_bias,
                   ssm_a_log=ssm_a_log, ssm_d=ssm_d, ssm_gate_norm=ssm_gate_norm, ssm_w_out=ssm_w_out,
                   kv_norm=kv_norm, w_kv=w_kv, k_norm=k_norm, attn_norm=attn_norm, w_q=w_q, q_norm=q_norm,
                   sinks=sinks, w_o=w_o, rel_bias=rel_bias)
    m_in = dict(ffn_norm=m_ffn_norm, ffn_w1=m_ffn_w1, ffn_w3=m_ffn_w3, ffn_w2=m_ffn_w2, ssm_norm=m_ssm_norm,
                ssm_w_in=m_ssm_w_in, ssm_conv_w=m_ssm_conv_w, ssm_conv_b=m_ssm_conv_b, ssm_dt_bias=m_ssm_dt_bias,
                ssm_a_log=m_ssm_a_log, ssm_d=m_ssm_d, ssm_gate_norm=m_ssm_gate_norm, ssm_w_out=m_ssm_w_out,
                kv_norm=m_kv_norm, w_kv=m_w_kv, k_norm=m_k_norm, attn_norm=m_attn_norm, w_q=m_w_q, q_norm=m_q_norm,
                sinks=m_sinks, w_o=m_w_o, rel_bias=m_rel_bias)
    v_in = dict(ffn_norm=v_ffn_norm, ffn_w1=v_ffn_w1, ffn_w3=v_ffn_w3, ffn_w2=v_ffn_w2, ssm_norm=v_ssm_norm,
                ssm_w_in=v_ssm_w_in, ssm_conv_w=v_ssm_conv_w, ssm_conv_b=v_ssm_conv_b, ssm_dt_bias=v_ssm_dt_bias,
                ssm_a_log=v_ssm_a_log, ssm_d=v_ssm_d, ssm_gate_norm=v_ssm_gate_norm, ssm_w_out=v_ssm_w_out,
                kv_norm=v_kv_norm, w_kv=v_w_kv, k_norm=v_k_norm, attn_norm=v_attn_norm, w_q=v_w_q, q_norm=v_q_norm,
                sinks=v_sinks, w_o=v_w_o, rel_bias=v_rel_bias)
    return _step(x[0], loss_target[0], weights, m_in, v_in)


BIG = ("ffn_w1", "ffn_w3", "ffn_w2", "ssm_w_in", "ssm_w_out", "w_kv", "w_q", "w_o")
SMALL = (("ffn_norm", True), ("ssm_norm", True), ("ssm_conv_w", True), ("ssm_conv_b", True),
         ("ssm_gate_norm", True), ("ssm_dt_bias", False), ("ssm_a_log", False), ("ssm_d", False),
         ("kv_norm", False), ("k_norm", False), ("attn_norm", False), ("q_norm", False), ("sinks", False),
         ("rel_bias", False))


def _halves_view(a):
    shape = a.shape
    ax = next(i for i, s in enumerate(shape) if s > 1)
    lead = int(np.prod(shape[:ax + 1])) // 2
    c = shape[-1]
    total = int(np.prod(shape))
    assert shape[ax] % 2 == 0
    return a.reshape(2, total // 2 // c, c)


def _small_layout(weights):
    off, table = 0, {}
    for name, sharded in SMALL:
        shape = weights[name].shape
        full = shape[:-1] + (shape[-1] * N_CHIPS,) if sharded else shape
        n = int(np.prod(full))
        table[name] = (off, full, sharded)
        off += n
    assert off <= SMALL_ROWS * 128
    return table


def _place_small(values, table, chip, scale_mask):
    flat = jnp.zeros((SMALL_ROWS * 128,), F32)
    for name, (off, full, sharded) in table.items():
        if not sharded:
            continue
        v = values[name].astype(F32)
        lead = int(np.prod(full[:-1]))
        w = v.shape[-1]
        blk = jnp.zeros((lead, full[-1]), F32)
        blk = lax.dynamic_update_slice(blk, v.reshape(lead, w) * scale_mask, (0, chip * w))
        flat = lax.dynamic_update_slice(flat, blk.reshape(-1), (off,))
    return flat.reshape(SMALL_ROWS, 128)


def _take_small(mat, table, name):
    off, full, _ = table[name]
    n = int(np.prod(full))
    return mat.reshape(-1)[off:off + n].reshape(full)


def _step(x, target, weights, m_in, v_in):
    t, d = x.shape
    xi, yi, ci = lax.axis_index("x"), lax.axis_index("y"), lax.axis_index("c")
    chip = 2 * xi + yi
    pos_arr = jnp.stack([chip, ci]).astype(jnp.int32)
    half_arr = jnp.reshape(ci, (1,)).astype(jnp.int32)

    gathered = dict(zip(BIG, gather_weights([cast_into_gather(_halves_view(weights[n]), pos_arr) for n in BIG])))
    table = _small_layout(weights)
    south = (ci == 0).astype(F32)
    small = allreduce_small(_place_small(weights, table, chip, south))
    sp = {n: _take_small(small, table, n) if sh else weights[n] for n, sh in SMALL}

    fs = weights["ffn_w1"].shape[-1]
    w1 = gathered["ffn_w1"].reshape(N_CHIPS, 2, 2, d, fs)
    w3 = gathered["ffn_w3"].reshape(N_CHIPS, 2, 2, d, fs)
    w2 = gathered["ffn_w2"].reshape(N_CHIPS, 2, 2, fs, d)
    n_in = weights["ssm_w_in"].shape[-1] * N_CHIPS
    di = weights["ssm_w_out"].shape[1] * N_CHIPS
    nheads = di // SSM_HEAD_DIM
    conv_dim = n_in - di - nheads
    w_in_full = jnp.moveaxis(gathered["ssm_w_in"].reshape(N_CHIPS, d, n_in // N_CHIPS), 0, 1).reshape(d, n_in)
    hpg = nheads // SSM_GROUPS

    def spread_heads(v):
        lead = v.shape[:-1]
        v = v.reshape(lead + (SSM_GROUPS, hpg))
        v = jnp.pad(v, [(0, 0)] * len(lead) + [(0, 0), (0, 128 - hpg)])
        return v.reshape(lead + (SSM_GROUPS * 128,))

    def gather_heads(v):
        lead = v.shape[:-1]
        return v.reshape(lead + (SSM_GROUPS, 128))[..., :hpg].reshape(lead + (nheads,))

    dt_col0 = di + conv_dim
    n_zx = dt_col0 + SSM_GROUPS * 128
    w_in = jnp.concatenate([w_in_full[:, :dt_col0], spread_heads(w_in_full[:, dt_col0:])], axis=1)
    w_out = gathered["ssm_w_out"].reshape(di, d)
    nkv = weights["w_kv"].shape[1] // (2 * ATT_HEAD_DIM)
    assert nkv == 2
    wkv_full = gathered["w_kv"].reshape(d, 2 * nkv * ATT_HEAD_DIM)
    wkv_heads = wkv_full.reshape(d, 2 * nkv, 1, ATT_HEAD_DIM)
    w_kvd = jnp.broadcast_to(wkv_heads, (d, 2 * nkv, 2, ATT_HEAD_DIM)).reshape(d, 4 * nkv * ATT_HEAD_DIM)
    wq = gathered["w_q"].reshape(d, -1)
    wo = gathered["w_o"].reshape(-1, d)
    nh = wq.shape[1] // ATT_HEAD_DIM

    ffn_g = sp["ffn_norm"]
    ssm_g = sp["ssm_norm"].reshape(1, d)
    cw = jnp.pad(sp["ssm_conv_w"].reshape(SSM_CONV, conv_dim), [(0, 8 - SSM_CONV), (0, 0)])
    cb = sp["ssm_conv_b"].reshape(1, conv_dim)
    gate_g = sp["ssm_gate_norm"].reshape(1, di)
    dt_bias = spread_heads(sp["ssm_dt_bias"].reshape(1, nheads))
    a_log = spread_heads(sp["ssm_a_log"].reshape(1, nheads))
    d_skip = spread_heads(sp["ssm_d"].reshape(1, nheads))
    kv_g = sp["kv_norm"].reshape(1, d)
    k_g = jnp.tile(sp["k_norm"].reshape(1, ATT_HEAD_DIM), (1, 2))
    attn_g = sp["attn_norm"].reshape(1, d)
    q_g = jnp.tile(sp["q_norm"].reshape(1, ATT_HEAD_DIM), (1, 2))
    sink_row = jnp.pad(sp["sinks"].reshape(1, nh), [(0, 0), (0, 128 - nh)])
    buckets = jnp.asarray(_t5_buckets())
    biasm = build_bias(sp["rel_bias"], buckets)

    h0 = x
    h1, a00, b00 = ffn_fwd(h0, ffn_g[0, 0].reshape(1, d), w1, w3, w2, 0, 0)
    zx = norm_mm(h1, ssm_g, w_in)
    xc = conv_fwd(zx, cw, cb, di)
    y_ssd, states = ssd_fwd(xc, zx, dt_bias, a_log, d_skip, dt_col0)
    h2 = gate_out_fwd(h1, y_ssd, zx, gate_g, w_out)
    h3, a01, b01 = ffn_fwd(h2, ffn_g[0, 1].reshape(1, d), w1, w3, w2, 0, 1)
    kvd = norm_mm(h3, kv_g, w_kvd)
    h4, a10, b10 = ffn_fwd(h3, ffn_g[1, 0].reshape(1, d), w1, w3, w2, 1, 0)
    qp = norm_mm(h4, attn_g, wq)
    h5 = attn_fwd(h4, qp, kvd, biasm, sink_row, q_g, k_g, wo)
    h6, a11, b11 = ffn_fwd(h5, ffn_g[1, 1].reshape(1, d), w1, w3, w2, 1, 1)
    loss_part, d6 = loss_head(h6, target)
    loss = lax.psum(loss_part[0, 0], ("x", "y", "c"))

    gw1 = [[None, None], [None, None]]
    gw3 = [[None, None], [None, None]]
    gw2 = [[None, None], [None, None]]
    gfn = [[None, None], [None, None]]

    def ffn_back(h_in, dy, a_s, b_s, layer, idx):
        dh, u, da, db, s, dg = ffn_bwd(h_in, dy, ffn_g[layer, idx].reshape(1, d), a_s, b_s, w1, w3, w2, layer, idx)
        gw1[layer][idx] = wgrad_grouped_b(u, da)
        gw3[layer][idx] = wgrad_grouped_b(u, db)
        gw2[layer][idx] = wgrad_grouped_a(s, dy, 0.5)
        gfn[layer][idx] = dg
        return dh

    d5 = ffn_back(h5, d6, a11, b11, 1, 1)
    dqp, dkvd, o16, dbiasm, dsinks, dqg, dkg = attn_bwd(d5, qp, kvd, biasm, sink_row, q_g, k_g, wo)
    g_wo = wgrad(o16, d5)
    d4, u_q, g_attn_norm = norm_mm_bwd(h4, attn_g, wq, dqp, d5)
    g_wq = wgrad(u_q, dqp)
    d3a = ffn_back(h3, d4, a10, b10, 1, 0)
    d3, u_kv, g_kv_norm = norm_mm_bwd(h3, kv_g, w_kvd, dkvd, d3a, 0.5)
    g_wkvd = wgrad(u_kv, dkvd)
    d2 = ffn_back(h2, d3, a01, b01, 0, 1)
    dzx, dy_ssd, yn16, g_gate = gate_out_bwd(d2, y_ssd, zx, gate_g, w_out, n_zx)
    g_wout = wgrad(yn16, d2)
    dzx, dxs, dbm, dcm, g_dtb, g_alog, g_dsk = ssd_bwd(dzx, dy_ssd, xc, zx, states, dt_bias, a_log, d_skip, dt_col0)
    dzx, g_cw, g_cb = conv_bwd(dzx, zx, dxs, dbm, dcm, cw, cb, di)
    d1, u_in, g_ssm_norm = norm_mm_bwd(h1, ssm_g, w_in, dzx, d2)
    g_win = wgrad(u_in, dzx)
    grad_x = ffn_back(h0, d1, a00, b00, 0, 0)
    g_relb = rel_bias_bwd(dbiasm, buckets)

    def stack_ffn(g):
        return jnp.stack([jnp.stack([g[l][i] for i in range(2)], axis=1) for l in range(2)], axis=1)

    g_win_full = jnp.concatenate([g_win[:, :dt_col0], gather_heads(g_win[:, dt_col0:])], axis=1)
    g_wkv = g_wkvd.reshape(d, 2 * nkv, 2, ATT_HEAD_DIM)[:, :, 0, :].reshape(d, 2 * nkv * ATT_HEAD_DIM)
    big_grads = {
        "ffn_w1": stack_ffn(gw1), "ffn_w3": stack_ffn(gw3), "ffn_w2": stack_ffn(gw2),
        "ssm_w_in": jnp.moveaxis(g_win_full.reshape(d, N_CHIPS, n_in // N_CHIPS), 1, 0),
        "ssm_w_out": g_wout.reshape(N_CHIPS, di // N_CHIPS, d),
        "w_kv": g_wkv.reshape(N_CHIPS, d // N_CHIPS, -1),
        "w_q": g_wq.reshape(N_CHIPS, d // N_CHIPS, -1),
        "w_o": g_wo.reshape(N_CHIPS, -1, d),
    }
    views = []
    for n in BIG:
        shard = weights[n]
        hv = _halves_view(shard)
        views.append(big_grads[n].reshape((N_CHIPS,) + hv.shape))
    recv1 = exchange_sibling_halves(views)
    p32, p16 = zip(*[add_sibling(g, r, half_arr) for g, r in zip(views, recv1)])
    recv2 = exchange_chip_partials(list(p16))
    mine = [add_chip_partials(p, r, pos_arr) for p, r in zip(p32, recv2)]
    full = share_with_sibling(mine)
    grads = {n: f.reshape(weights[n].shape) for n, f in zip(BIG, full)}

    small_grads = {
        "ffn_norm": jnp.stack([jnp.stack([gfn[l][i].reshape(d) for i in range(2)]) for l in range(2)]),
        "ssm_norm": g_ssm_norm.reshape(1, d),
        "ssm_conv_w": g_cw[:SSM_CONV].reshape(1, SSM_CONV, conv_dim),
        "ssm_conv_b": g_cb.reshape(1, conv_dim),
        "ssm_gate_norm": g_gate.reshape(1, di),
        "ssm_dt_bias": gather_heads(g_dtb.reshape(1, -1)), "ssm_a_log": gather_heads(g_alog.reshape(1, -1)),
        "ssm_d": gather_heads(g_dsk.reshape(1, -1)),
        "kv_norm": g_kv_norm.reshape(d), "k_norm": dkg[0, :ATT_HEAD_DIM], "attn_norm": g_attn_norm.reshape(1, d),
        "q_norm": dqg[:, :ATT_HEAD_DIM], "sinks": dsinks[:, :nh], "rel_bias": g_relb[:, :nh],
    }
    flat = jnp.zeros((SMALL_ROWS * 128,), F32)
    for name, (off, fshape, _) in table.items():
        flat = lax.dynamic_update_slice(flat, small_grads[name].astype(F32).reshape(-1), (off,))
    small_sum = allreduce_small(flat.reshape(SMALL_ROWS, 128))
    for name, (off, fshape, sharded) in table.items():
        g = _take_small(small_sum, table, name)
        if sharded:
            w = weights[name].shape[-1]
            lead = int(np.prod(fshape[:-1]))
            g = lax.dynamic_slice(g.reshape(lead, fshape[-1]), (0, chip * w), (lead, w)).reshape(weights[name].shape)
        grads[name] = g.reshape(weights[name].shape)

    names = list(weights)
    deltas, new_m, new_v = {}, {}, {}
    small_names = [n for n, _ in SMALL]
    for n in BIG:
        shp = weights[n].shape
        v2 = lambda a: a.reshape(-1, shp[-1])
        dl, nm, nv = adamw(v2(weights[n]), v2(grads[n]), v2(m_in[n]), v2(v_in[n]))
        deltas[n], new_m[n], new_v[n] = dl.reshape(shp), nm.reshape(shp), nv.reshape(shp)
    sizes = [int(np.prod(weights[n].shape)) for n in small_names]
    tot = sum(sizes)
    rows = -(-tot // 128)
    rows = -(-rows // 8) * 8

    def pack(dct):
        flat = jnp.concatenate([dct[n].reshape(-1) for n in small_names])
        return jnp.pad(flat, (0, rows * 128 - tot), constant_values=1.0).reshape(rows, 128)

    dl, nm, nv = adamw(pack(weights), pack(grads), pack(m_in), pack(v_in))
    off = 0
    for n, sz in zip(small_names, sizes):
        shp = weights[n].shape
        take = lambda a: a.reshape(-1)[off:off + sz].reshape(shp)
        deltas[n], new_m[n], new_v[n] = take(dl), take(nm), take(nv)
        off += sz

    return (loss, grad_x[None], *[grads[n] for n in names], *[deltas[n] for n in names],
            *[new_m[n] for n in names], *[new_v[n] for n in names])
```

```python
import functools
import math

import jax
import jax.numpy as jnp
import numpy as np
from jax import lax
from jax.experimental import pallas as pl
from jax.experimental.pallas import tpu as pltpu

F32 = jnp.float32
BF16 = jnp.bfloat16
EPS = 1e-6
MESH = pl.DeviceIdType.MESH

SSM_HEAD_DIM = 64
SSM_GROUPS = 4
SSM_STATE = 128
SSM_CONV = 4
SSM_CHUNK = 256
ATT_HEAD_DIM = 64
ATT_WINDOW = 128
REL_BUCKETS = 32
N_CHIPS = 4

ADAM_LR = 0.001
ADAM_B1 = 0.9
ADAM_B2 = 0.999
ADAM_EPS = 1e-08
ADAM_WD = 0.01
ADAM_STEP = 10

VMEM_LIMIT_BYTES = 56 * 1024 * 1024
NEG = -1e30


def _params(*sem):
    return pltpu.CompilerParams(dimension_semantics=sem if sem else None, vmem_limit_bytes=VMEM_LIMIT_BYTES)


def _dot(a, b):
    return jnp.dot(a, b, preferred_element_type=F32)


def _dot_nt(a, b):
    return lax.dot_general(a, b, (((1,), (1,)), ((), ())), preferred_element_type=F32)


def _dot_tn(a, b):
    return lax.dot_general(a, b, (((0,), (0,)), ((), ())), preferred_element_type=F32)


def _b(x):
    return x.astype(BF16)


@jax.custom_vjp
def _bmm(a, b):
    return _dot(_b(a), _b(b))


def _bmm_fwd(a, b):
    return _bmm(a, b), (a, b)


def _bmm_bwd(res, g):
    a, b = res
    g16 = _b(g)
    return _dot_nt(g16, _b(b)).astype(a.dtype), _dot_tn(_b(a), g16).astype(b.dtype)


_bmm.defvjp(_bmm_fwd, _bmm_bwd)


@jax.custom_vjp
def _bmm_nt(a, b):
    return _dot_nt(_b(a), _b(b))


def _bmm_nt_fwd(a, b):
    return _bmm_nt(a, b), (a, b)


def _bmm_nt_bwd(res, g):
    a, b = res
    g16 = _b(g)
    return _dot(g16, _b(b)).astype(a.dtype), _dot_tn(g16, _b(a)).astype(b.dtype)


_bmm_nt.defvjp(_bmm_nt_fwd, _bmm_nt_bwd)


@jax.custom_vjp
def _bmm_tn(a, b):
    return _dot_tn(_b(a), _b(b))


def _bmm_tn_fwd(a, b):
    return _bmm_tn(a, b), (a, b)


def _bmm_tn_bwd(res, g):
    a, b = res
    g16 = _b(g)
    return _dot_nt(_b(b), g16).astype(a.dtype), _dot(_b(a), g16).astype(b.dtype)


_bmm_tn.defvjp(_bmm_tn_fwd, _bmm_tn_bwd)


def _split3(x):
    hi = _b(x)
    r = x - hi.astype(F32)
    mid = _b(r)
    lo = _b(r - mid.astype(F32))
    return hi, mid, lo


def _x_left_raw(m, x):
    hi, mid, lo = _split3(x)
    return _dot(m, hi) + _dot(m, mid) + _dot(m, lo)


def _x_left_t_raw(m, x):
    hi, mid, lo = _split3(x)
    return _dot_tn(m, hi) + _dot_tn(m, mid) + _dot_tn(m, lo)


def _x_right_raw(x, m):
    hi, mid, lo = _split3(x)
    return _dot(hi, m) + _dot(mid, m) + _dot(lo, m)


def _x_right_t_raw(x, m):
    hi, mid, lo = _split3(x)
    return _dot_nt(hi, m) + _dot_nt(mid, m) + _dot_nt(lo, m)


@jax.custom_vjp
def _xleft(m, x):
    return _x_left_raw(m, x)


_xleft.defvjp(lambda m, x: (_x_left_raw(m, x), m),
              lambda m, g: (jnp.zeros_like(m), _x_left_t_raw(m, g)))


@jax.custom_vjp
def _xright(x, m):
    return _x_right_raw(x, m)


_xright.defvjp(lambda x, m: (_x_right_raw(x, m), m),
               lambda m, g: (_x_right_t_raw(g, m), jnp.zeros_like(m)))


def _sigmoid(x):
    return 1.0 / (1.0 + jnp.exp(-x))


def _silu(x):
    return x * _sigmoid(x)


def _softplus(x):
    return jnp.maximum(x, 0.0) + jnp.log(1.0 + jnp.exp(-jnp.abs(x)))


def _rms(x):
    return x * lax.rsqrt(jnp.mean(x * x, axis=-1, keepdims=True) + EPS)


def _iota(shape, dim):
    return lax.broadcasted_iota(jnp.int32, shape, dim)


def _blockdiag64(n):
    return jnp.where(_iota((n, n), 0) // 64 == _iota((n, n), 1) // 64, 1.0, 0.0).astype(BF16)


def _group64_rms(x, seg_sum):
    ms = seg_sum(x * x) * (1.0 / 64.0)
    return x * lax.rsqrt(ms + EPS)


def _fold64(x):
    ax = x.ndim - 1
    w = x.shape[ax]
    lo = (_iota(x.shape, ax) % 128) < 64
    return x + jnp.where(lo, pltpu.roll(x, w - 64, ax), pltpu.roll(x, 64, ax))


def _tile(n, want):
    t = min(n, want)
    assert n % t == 0, (n, t)
    return t


def _lane_tile(n, cap=1536):
    if n <= cap:
        return n
    return max(w for w in range(128, cap + 1, 128) if n % w == 0)


def ffn_fwd(h, g, w1, w3, w2, layer, idx):
    t, d = h.shape
    nk, fs = w1.shape[0], w1.shape[-1]
    tm = _tile(t, 512)

    def body(h_ref, g_ref, w1_ref, w3_ref, w2_ref, o_ref, a_ref, b_ref, u_scr, acc):
        k = pl.program_id(1)

        @pl.when(k == 0)
        def _():
            u_scr[...] = _b(_rms(h_ref[...]) * g_ref[...])
            acc[...] = jnp.zeros_like(acc)

        u = u_scr[...]
        a = _dot(u, w1_ref[...])
        b = _dot(u, w3_ref[...])
        a_ref[...] = _b(a)
        b_ref[...] = _b(b)
        acc[...] += _dot(_b(_silu(a) * b), w2_ref[...])

        @pl.when(k == nk - 1)
        def _():
            o_ref[...] = h_ref[...] + 0.5 * acc[...]

    wspec = lambda r, c: pl.BlockSpec((None, None, None, r, c), lambda i, k: (k, layer, idx, 0, 0))
    return pl.pallas_call(
        body, name="ffn_fwd",
        grid=(t // tm, nk),
        in_specs=[pl.BlockSpec((tm, d), lambda i, k: (i, 0)), pl.BlockSpec((1, d), lambda i, k: (0, 0)),
                  wspec(d, fs), wspec(d, fs), wspec(fs, d)],
        out_specs=[pl.BlockSpec((tm, d), lambda i, k: (i, 0)),
                   pl.BlockSpec((None, tm, fs), lambda i, k: (k, i, 0)),
                   pl.BlockSpec((None, tm, fs), lambda i, k: (k, i, 0))],
        out_shape=[jax.ShapeDtypeStruct((t, d), F32), jax.ShapeDtypeStruct((nk, t, fs), BF16),
                   jax.ShapeDtypeStruct((nk, t, fs), BF16)],
        scratch_shapes=[pltpu.VMEM((tm, d), BF16), pltpu.VMEM((tm, d), F32)],
        compiler_params=_params("arbitrary", "arbitrary"),
    )(h, g, w1, w3, w2)


def ffn_bwd(h, dy, g, a_s, b_s, w1, w3, w2, layer, idx):
    t, d = h.shape
    nk, fs = w1.shape[0], w1.shape[-1]
    tm = _tile(t, 512)

    def body(h_ref, dy_ref, g_ref, a_ref, b_ref, w1_ref, w3_ref, w2_ref,
             dh_ref, u_ref, da_ref, db_ref, s_ref, dg_ref, dyh_scr, du_acc):
        i, k = pl.program_id(0), pl.program_id(1)

        @pl.when(k == 0)
        def _():
            dyh_scr[...] = _b(0.5 * dy_ref[...])
            du_acc[...] = jnp.zeros_like(du_acc)

        @pl.when((k == 0) & (i == 0))
        def _():
            dg_ref[...] = jnp.zeros_like(dg_ref)

        ds = _dot_nt(dyh_scr[...], w2_ref[...])
        a = a_ref[...].astype(F32)
        b = b_ref[...].astype(F32)
        sig = _sigmoid(a)
        sl = a * sig
        s_ref[...] = _b(sl * b)
        da = _b(ds * b * (sig * (1.0 + a * (1.0 - sig))))
        db = _b(ds * sl)
        da_ref[...] = da
        db_ref[...] = db
        du_acc[...] += _dot_nt(da, w1_ref[...]) + _dot_nt(db, w3_ref[...])

        @pl.when(k == nk - 1)
        def _():
            hh = h_ref[...]
            rstd = lax.rsqrt(jnp.mean(hh * hh, axis=-1, keepdims=True) + EPS)
            xh = hh * rstd
            gg = g_ref[...]
            u_ref[...] = _b(xh * gg)
            du = du_acc[...]
            dg_ref[...] += jnp.sum(du * xh, axis=0, keepdims=True)
            dxh = du * gg
            dh_ref[...] = dy_ref[...] + rstd * (dxh - xh * jnp.mean(dxh * xh, axis=-1, keepdims=True))

    wspec = lambda r, c: pl.BlockSpec((None, None, None, r, c), lambda i, k: (k, layer, idx, 0, 0))
    tok = pl.BlockSpec((tm, d), lambda i, k: (i, 0))
    hid = pl.BlockSpec((None, tm, fs), lambda i, k: (k, i, 0))
    return pl.pallas_call(
        body, name="ffn_bwd",
        grid=(t // tm, nk),
        in_specs=[tok, tok, pl.BlockSpec((1, d), lambda i, k: (0, 0)), hid, hid, wspec(d, fs), wspec(d, fs), wspec(fs, d)],
        out_specs=[tok, tok, hid, hid, hid, pl.BlockSpec((1, d), lambda i, k: (0, 0))],
        out_shape=[jax.ShapeDtypeStruct((t, d), F32), jax.ShapeDtypeStruct((t, d), BF16),
                   jax.ShapeDtypeStruct((nk, t, fs), BF16), jax.ShapeDtypeStruct((nk, t, fs), BF16),
                   jax.ShapeDtypeStruct((nk, t, fs), BF16), jax.ShapeDtypeStruct((1, d), F32)],
        scratch_shapes=[pltpu.VMEM((tm, d), BF16), pltpu.VMEM((tm, d), F32)],
        compiler_params=_params("arbitrary", "arbitrary"),
    )(h, dy, g, a_s, b_s, w1, w3, w2)


def wgrad_grouped_b(a, bs, scale=1.0):
    t, m = a.shape
    ng, _, n = bs.shape
    tk = _tile(t, 2048)

    def body(a_ref, b_ref, o_ref):
        j = pl.program_id(1)

        @pl.when(j == 0)
        def _():
            o_ref[...] = jnp.zeros_like(o_ref)

        o_ref[...] += _dot_tn(_b(a_ref[...]), _b(b_ref[...]))

        if scale != 1.0:
            @pl.when(j == pl.num_programs(1) - 1)
            def _():
                o_ref[...] = o_ref[...] * scale

    return pl.pallas_call(
        body, name="wgrad_gb",
        grid=(ng, t // tk),
        in_specs=[pl.BlockSpec((tk, m), lambda k, j: (j, 0)), pl.BlockSpec((None, tk, n), lambda k, j: (k, j, 0))],
        out_specs=pl.BlockSpec((None, m, n), lambda k, j: (k, 0, 0)),
        out_shape=jax.ShapeDtypeStruct((ng, m, n), F32),
        compiler_params=_params("arbitrary", "arbitrary"),
    )(a, bs)


def wgrad_grouped_a(as_, b, scale=1.0):
    ng, t, m = as_.shape
    n = b.shape[1]
    tk = _tile(t, 2048)

    def body(a_ref, b_ref, o_ref):
        j = pl.program_id(1)

        @pl.when(j == 0)
        def _():
            o_ref[...] = jnp.zeros_like(o_ref)

        o_ref[...] += _dot_tn(_b(a_ref[...]), _b(b_ref[...]))

        if scale != 1.0:
            @pl.when(j == pl.num_programs(1) - 1)
            def _():
                o_ref[...] = o_ref[...] * scale

    return pl.pallas_call(
        body, name="wgrad_ga",
        grid=(ng, t // tk),
        in_specs=[pl.BlockSpec((None, tk, m), lambda k, j: (k, j, 0)), pl.BlockSpec((tk, n), lambda k, j: (j, 0))],
        out_specs=pl.BlockSpec((None, m, n), lambda k, j: (k, 0, 0)),
        out_shape=jax.ShapeDtypeStruct((ng, m, n), F32),
        compiler_params=_params("arbitrary", "arbitrary"),
    )(as_, b)


def wgrad(a, b):
    t, m = a.shape
    n = b.shape[1]
    tk = _tile(t, 1024)
    tn = _lane_tile(n, 1536 if m <= 1024 else 512)

    def body(a_ref, b_ref, o_ref):
        @pl.when(pl.program_id(1) == 0)
        def _():
            o_ref[...] = jnp.zeros_like(o_ref)

        o_ref[...] += _dot_tn(_b(a_ref[...]), _b(b_ref[...]))

    return pl.pallas_call(
        body, name="wgrad",
        grid=(n // tn, t // tk),
        in_specs=[pl.BlockSpec((tk, m), lambda c, j: (j, 0)), pl.BlockSpec((tk, tn), lambda c, j: (j, c))],
        out_specs=pl.BlockSpec((m, tn), lambda c, j: (0, c)),
        out_shape=jax.ShapeDtypeStruct((m, n), F32),
        compiler_params=_params("arbitrary", "arbitrary"),
    )(a, b)


def norm_mm(h, g, w):
    t, d = h.shape
    n = w.shape[1]
    tm = _tile(t, 512)
    tn = _lane_tile(n)

    def body(h_ref, g_ref, w_ref, o_ref, u_scr):
        @pl.when(pl.program_id(1) == 0)
        def _():
            u_scr[...] = _b(_rms(h_ref[...]) * g_ref[...])

        o_ref[...] = _dot(u_scr[...], w_ref[...])

    return pl.pallas_call(
        body, name="norm_mm",
        grid=(t // tm, n // tn),
        in_specs=[pl.BlockSpec((tm, d), lambda i, j: (i, 0)), pl.BlockSpec((1, d), lambda i, j: (0, 0)),
                  pl.BlockSpec((d, tn), lambda i, j: (0, j))],
        out_specs=pl.BlockSpec((tm, tn), lambda i, j: (i, j)),
        out_shape=jax.ShapeDtypeStruct((t, n), F32),
        scratch_shapes=[pltpu.VMEM((tm, d), BF16)],
        compiler_params=_params("arbitrary", "arbitrary"),
    )(h, g, w)


def norm_mm_bwd(h, g, w, dout, dres, scale=1.0):
    t, d = h.shape
    n = w.shape[1]
    tm = _tile(t, 512)
    tn = _lane_tile(n)
    nj = n // tn

    def body(h_ref, g_ref, w_ref, do_ref, dr_ref, dh_ref, u_ref, dg_ref, du_acc):
        i, j = pl.program_id(0), pl.program_id(1)

        @pl.when(j == 0)
        def _():
            du_acc[...] = jnp.zeros_like(du_acc)

        @pl.when((j == 0) & (i == 0))
        def _():
            dg_ref[...] = jnp.zeros_like(dg_ref)

        du_acc[...] += _dot_nt(_b(do_ref[...]), w_ref[...])

        @pl.when(j == nj - 1)
        def _():
            hh = h_ref[...]
            rstd = lax.rsqrt(jnp.mean(hh * hh, axis=-1, keepdims=True) + EPS)
            xh = hh * rstd
            gg = g_ref[...]
            u_ref[...] = _b(xh * gg)
            du = du_acc[...] * scale
            dg_ref[...] += jnp.sum(du * xh, axis=0, keepdims=True)
            dxh = du * gg
            dh_ref[...] = dr_ref[...] + rstd * (dxh - xh * jnp.mean(dxh * xh, axis=-1, keepdims=True))

    tok = pl.BlockSpec((tm, d), lambda i, j: (i, 0))
    return pl.pallas_call(
        body, name="norm_mm_bwd",
        grid=(t // tm, nj),
        in_specs=[tok, pl.BlockSpec((1, d), lambda i, j: (0, 0)), pl.BlockSpec((d, tn), lambda i, j: (0, j)),
                  pl.BlockSpec((tm, tn), lambda i, j: (i, j)), tok],
        out_specs=[tok, tok, pl.BlockSpec((1, d), lambda i, j: (0, 0))],
        out_shape=[jax.ShapeDtypeStruct((t, d), F32), jax.ShapeDtypeStruct((t, d), BF16),
                   jax.ShapeDtypeStruct((1, d), F32)],
        scratch_shapes=[pltpu.VMEM((tm, d), F32)],
        compiler_params=_params("arbitrary", "arbitrary"),
    )(h, g, w, dout, dres)


CONV_COLS = 512


CONV_ROWS = 64


def _conv_pre(ext, w, b, r0, n):
    return (b + w[0:1] * ext[pl.ds(5 + r0, n), :] + w[1:2] * ext[pl.ds(6 + r0, n), :]
            + w[2:3] * ext[pl.ds(7 + r0, n), :] + w[3:4] * ext[pl.ds(8 + r0, n), :])


def conv_fwd(zx, cw, cb, col0):
    t = zx.shape[0]
    c = cw.shape[1]
    tm = _tile(t, 512)
    cb0 = col0 // CONV_COLS

    rc = _tile(tm, CONV_ROWS)

    def body(x_ref, w_ref, b_ref, o_ref, ext):
        @pl.when(pl.program_id(1) == 0)
        def _():
            ext[0:8, :] = jnp.zeros((8, CONV_COLS), F32)

        ext[8:, :] = x_ref[...]
        w, b = w_ref[...], b_ref[...]
        for r0 in range(0, tm, rc):
            o_ref[r0:r0 + rc, :] = _silu(_conv_pre(ext, w, b, r0, rc))
        ext[0:8, :] = ext[tm:tm + 8, :]

    return pl.pallas_call(
        body, name="conv_fwd",
        grid=(c // CONV_COLS, t // tm),
        in_specs=[pl.BlockSpec((tm, CONV_COLS), lambda j, i: (i, cb0 + j)),
                  pl.BlockSpec((8, CONV_COLS), lambda j, i: (0, j)), pl.BlockSpec((1, CONV_COLS), lambda j, i: (0, j))],
        out_specs=pl.BlockSpec((tm, CONV_COLS), lambda j, i: (i, j)),
        out_shape=jax.ShapeDtypeStruct((t, c), F32),
        scratch_shapes=[pltpu.VMEM((tm + 8, CONV_COLS), F32)],
        compiler_params=_params("arbitrary", "arbitrary"),
    )(zx, cw, cb)


def conv_bwd(dzx, zx, dxs, dbm, dcm, cw, cb, col0):
    t = zx.shape[0]
    c = cw.shape[1]
    tm = _tile(t, 512)
    nt = t // tm
    cb0 = col0 // CONV_COLS
    nxs = dxs.shape[1] // CONV_COLS
    hb = tm // 8

    rc = _tile(tm, CONV_ROWS)

    def body(dzx_ref, x_ref, xh_ref, dxs_ref, db_ref, dc_ref, w_ref, b_ref, o_ref, dw_ref, dbias_ref, ext, gy):
        j, i = pl.program_id(0), pl.program_id(1)
        ri = nt - 1 - i

        @pl.when(i == 0)
        def _():
            gy[tm:tm + 8, :] = jnp.zeros((8, CONV_COLS), F32)
            dw_ref[...] = jnp.zeros_like(dw_ref)
            dbias_ref[...] = jnp.zeros_like(dbias_ref)

        ext[0:8, :] = jnp.where(ri > 0, xh_ref[...], 0.0)
        ext[8:, :] = x_ref[...]
        w, b = w_ref[...], b_ref[...]
        dw = [jnp.zeros((1, CONV_COLS), F32) for _ in range(SSM_CONV)]
        dbias = jnp.zeros((1, CONV_COLS), F32)
        for r0 in range(0, tm, rc):
            rows = pl.ds(r0, rc)
            win = [ext[pl.ds(5 + tap + r0, rc), :] for tap in range(SSM_CONV)]
            y = b + w[0:1] * win[0] + w[1:2] * win[1] + w[2:3] * win[2] + w[3:4] * win[3]
            sig = _sigmoid(y)
            dout = jnp.where(j < nxs, dxs_ref[rows, :], jnp.where(j == nxs, db_ref[rows, :], dc_ref[rows, :]))
            g = dout * (sig * (1.0 + y * (1.0 - sig)))
            gy[rows, :] = g
            dbias = dbias + jnp.sum(g, axis=0, keepdims=True)
            for tap in range(SSM_CONV):
                dw[tap] = dw[tap] + jnp.sum(g * win[tap], axis=0, keepdims=True)
        for r0 in range(0, tm, rc):
            o_ref[r0:r0 + rc, :] = (w[0:1] * gy[pl.ds(r0 + 3, rc), :] + w[1:2] * gy[pl.ds(r0 + 2, rc), :]
                                    + w[2:3] * gy[pl.ds(r0 + 1, rc), :] + w[3:4] * gy[pl.ds(r0, rc), :])
        gy[tm:tm + 8, :] = gy[0:8, :]
        for tap in range(SSM_CONV):
            dw_ref[tap:tap + 1, :] += dw[tap]
        dbias_ref[...] += dbias

    return pl.pallas_call(
        body, name="conv_bwd",
        grid=(c // CONV_COLS, nt),
        in_specs=[pl.BlockSpec(memory_space=pl.ANY),
                  pl.BlockSpec((tm, CONV_COLS), lambda j, i: (nt - 1 - i, cb0 + j)),
                  pl.BlockSpec((8, CONV_COLS), lambda j, i: (jnp.maximum((nt - 1 - i) * hb - 1, 0), cb0 + j)),
                  pl.BlockSpec((tm, CONV_COLS), lambda j, i: (nt - 1 - i, jnp.minimum(j, nxs - 1))),
                  pl.BlockSpec((tm, CONV_COLS), lambda j, i: (nt - 1 - i, 0)),
                  pl.BlockSpec((tm, CONV_COLS), lambda j, i: (nt - 1 - i, 0)),
                  pl.BlockSpec((8, CONV_COLS), lambda j, i: (0, j)), pl.BlockSpec((1, CONV_COLS), lambda j, i: (0, j))],
        out_specs=[pl.BlockSpec((tm, CONV_COLS), lambda j, i: (nt - 1 - i, cb0 + j)),
                   pl.BlockSpec((8, CONV_COLS), lambda j, i: (0, j)), pl.BlockSpec((1, CONV_COLS), lambda j, i: (0, j))],
        out_shape=[jax.ShapeDtypeStruct(dzx.shape, F32), jax.ShapeDtypeStruct((8, c), F32),
                   jax.ShapeDtypeStruct((1, c), F32)],
        scratch_shapes=[pltpu.VMEM((tm + 8, CONV_COLS), F32), pltpu.VMEM((tm + 8, CONV_COLS), F32)],
        input_output_aliases={0: 0},
        compiler_params=_params("arbitrary", "arbitrary"),
    )(dzx, zx, zx, dxs, dbm, dcm, cw, cb)


def _ssd_group(xs, bg, cg, dtraw, s0, bias, alog, dsk):
    L = xs.shape[0]
    causal = _iota((L, L), 0) >= _iota((L, L), 1)
    tril = jnp.where(causal, 1.0, 0.0).astype(BF16)
    dt = _softplus(dtraw + bias)
    a = -jnp.exp(alog)
    acum = _xleft(tril, dt * a)
    acum_t = acum.T
    dt_t = dt.T
    cb = _bmm_nt(cg, bg)
    lo = _iota((L, 128), 1) < 64
    lo_row = _iota((1, 128), 1) < 64
    lo_col = _iota((128, 1), 0) < 64
    alast = acum[L - 1:L, :]
    ys, s1s = [], []
    for q in range(4):
        xp = xs[:, q * 128:(q + 1) * 128]
        sp = s0[q * 128:(q + 1) * 128, :]
        yd, ec, wc, el = [], [], [], []
        for j in range(2):
            r = 2 * q + j
            ac = acum[:, r:r + 1]
            decay = jnp.exp(jnp.where(causal, ac - acum_t[r:r + 1, :], NEG))
            yd.append(_bmm(cb * decay * dt_t[r:r + 1, :], xp))
            ec.append(jnp.exp(ac))
            al = alast[:, r:r + 1]
            wc.append(jnp.exp(al - ac) * dt[:, r:r + 1])
            el.append(jnp.exp(al))
        y_off = _bmm_nt(cg, sp) * jnp.where(lo, ec[0], ec[1])
        dsel = jnp.where(lo_row, dsk[:, 2 * q:2 * q + 1], dsk[:, 2 * q + 1:2 * q + 2])
        ys.append(jnp.where(lo, yd[0], yd[1]) + y_off + dsel * xp)
        xw = xp * jnp.where(lo, wc[0], wc[1])
        s1s.append(sp * jnp.where(lo_col, el[0], el[1]) + _bmm_tn(xw, bg))
    return jnp.concatenate(ys, axis=1), jnp.concatenate(s1s, axis=0)


def ssd_fwd(xc, zx, bias, alog, dsk, dt_col0):
    t = xc.shape[0]
    L = _tile(t, SSM_CHUNK)
    nc = t // L
    g = SSM_GROUPS
    dtb = dt_col0 // 128

    def body(xs_ref, b_ref, c_ref, dt_ref, bias_ref, alog_ref, dsk_ref, y_ref, st_ref, state):
        c, gi = pl.program_id(0), pl.program_id(1)

        @pl.when(c == 0)
        def _():
            state[gi] = jnp.zeros((512, 128), F32)

        s0 = state[gi]
        st_ref[...] = s0
        y, s1 = _ssd_group(xs_ref[...], b_ref[...], c_ref[...], dt_ref[...], s0,
                           bias_ref[...], alog_ref[...], dsk_ref[...])
        y_ref[...] = y
        state[gi] = s1

    vec = pl.BlockSpec((1, 128), lambda c, gi: (0, gi))
    return pl.pallas_call(
        body, name="ssd_fwd",
        grid=(nc, g),
        in_specs=[pl.BlockSpec((L, 512), lambda c, gi: (c, gi)),
                  pl.BlockSpec((L, 128), lambda c, gi: (c, 16 + gi)),
                  pl.BlockSpec((L, 128), lambda c, gi: (c, 20 + gi)),
                  pl.BlockSpec((L, 128), lambda c, gi: (c, dtb + gi)), vec, vec, vec],
        out_specs=[pl.BlockSpec((L, 512), lambda c, gi: (c, gi)),
                   pl.BlockSpec((None, None, 512, 128), lambda c, gi: (c, gi, 0, 0))],
        out_shape=[jax.ShapeDtypeStruct((t, 2048), F32), jax.ShapeDtypeStruct((nc, g, 512, 128), F32)],
        scratch_shapes=[pltpu.VMEM((g, 512, 128), F32)],
        compiler_params=_params("arbitrary", "arbitrary"),
    )(xc, xc, xc, zx, bias, alog, dsk)


def ssd_bwd(dzx, dy, xc, zx, states, bias, alog, dsk, dt_col0):
    t = xc.shape[0]
    L = _tile(t, SSM_CHUNK)
    nc = t // L
    g = SSM_GROUPS
    dtb = dt_col0 // 128

    def body(dzx_ref, dy_ref, xs_ref, b_ref, c_ref, dt_ref, st_ref, bias_ref, alog_ref, dsk_ref,
             ddt_ref, dxs_ref, db_ref, dc_ref, dbias_ref, dalog_ref, ddsk_ref, dstate):
        c, gi = pl.program_id(0), pl.program_id(1)

        @pl.when(c == 0)
        def _():
            dstate[gi] = jnp.zeros((512, 128), F32)

        @pl.when((c == 0) & (gi == 0))
        def _():
            dbias_ref[...] = jnp.zeros_like(dbias_ref)
            dalog_ref[...] = jnp.zeros_like(dalog_ref)
            ddsk_ref[...] = jnp.zeros_like(ddsk_ref)

        _, vjp = jax.vjp(_ssd_group, xs_ref[...], b_ref[...], c_ref[...], dt_ref[...], st_ref[...],
                         bias_ref[...], alog_ref[...], dsk_ref[...])
        dxs, db, dc, ddt, ds0, dbias, dalog, ddsk = vjp((dy_ref[...], dstate[gi]))
        dxs_ref[...] = dxs
        db_ref[...] = db
        dc_ref[...] = dc
        ddt_ref[...] = ddt
        dstate[gi] = ds0
        dbias_ref[gi] += dbias
        dalog_ref[gi] += dalog
        ddsk_ref[gi] += ddsk

    rc = lambda c: nc - 1 - c
    vec = pl.BlockSpec((1, 128), lambda c, gi: (0, gi))
    acc = pl.BlockSpec((g, 1, 128), lambda c, gi: (0, 0, 0))
    return pl.pallas_call(
        body, name="ssd_bwd",
        grid=(nc, g),
        in_specs=[pl.BlockSpec(memory_space=pl.ANY),
                  pl.BlockSpec((L, 512), lambda c, gi: (rc(c), gi)),
                  pl.BlockSpec((L, 512), lambda c, gi: (rc(c), gi)),
                  pl.BlockSpec((L, 128), lambda c, gi: (rc(c), 16 + gi)),
                  pl.BlockSpec((L, 128), lambda c, gi: (rc(c), 20 + gi)),
                  pl.BlockSpec((L, 128), lambda c, gi: (rc(c), dtb + gi)),
                  pl.BlockSpec((None, None, 512, 128), lambda c, gi: (rc(c), gi, 0, 0)), vec, vec, vec],
        out_specs=[pl.BlockSpec((L, 128), lambda c, gi: (rc(c), dtb + gi)),
                   pl.BlockSpec((L, 512), lambda c, gi: (rc(c), gi)),
                   pl.BlockSpec((L, 128), lambda c, gi: (rc(c), gi)),
                   pl.BlockSpec((L, 128), lambda c, gi: (rc(c), gi)), acc, acc, acc],
        out_shape=[jax.ShapeDtypeStruct(dzx.shape, F32), jax.ShapeDtypeStruct((t, 2048), F32),
                   jax.ShapeDtypeStruct((t, 512), F32), jax.ShapeDtypeStruct((t, 512), F32),
                   jax.ShapeDtypeStruct((g, 1, 128), F32), jax.ShapeDtypeStruct((g, 1, 128), F32),
                   jax.ShapeDtypeStruct((g, 1, 128), F32)],
        scratch_shapes=[pltpu.VMEM((g, 512, 128), F32)],
        input_output_aliases={0: 0},
        compiler_params=_params("arbitrary", "arbitrary"),
    )(dzx, dy, xc, xc, xc, zx, states, bias, alog, dsk)


def _gate_tile(y, z, gn):
    gated = y * _silu(z)
    parts = [_rms(gated[:, k * 512:(k + 1) * 512]) for k in range(SSM_GROUPS)]
    return jnp.concatenate(parts, axis=1) * gn


def gate_out_fwd(h, y, zx, gn, w_out):
    t, d = h.shape
    di = y.shape[1]
    tm = _tile(t, 256)

    def body(h_ref, y_ref, z_ref, gn_ref, w_ref, o_ref):
        yn = _gate_tile(y_ref[...], z_ref[...], gn_ref[...])
        o_ref[...] = h_ref[...] + _dot(_b(yn), w_ref[...])

    return pl.pallas_call(
        body, name="gate_out_fwd",
        grid=(t // tm,),
        in_specs=[pl.BlockSpec((tm, d), lambda i: (i, 0)), pl.BlockSpec((tm, di), lambda i: (i, 0)),
                  pl.BlockSpec((tm, di), lambda i: (i, 0)), pl.BlockSpec((1, di), lambda i: (0, 0)),
                  pl.BlockSpec((di, d), lambda i: (0, 0))],
        out_specs=pl.BlockSpec((tm, d), lambda i: (i, 0)),
        out_shape=jax.ShapeDtypeStruct((t, d), F32),
        compiler_params=_params("arbitrary"),
    )(h, y, zx, gn, w_out)


def gate_out_bwd(dy, y, zx, gn, w_out, n_zx):
    t, d = dy.shape
    di = y.shape[1]
    tm = _tile(t, 256)

    def body(dy_ref, y_ref, z_ref, gn_ref, w_ref, dz_ref, dys_ref, yn_ref, dgn_ref):
        @pl.when(pl.program_id(0) == 0)
        def _():
            dgn_ref[...] = jnp.zeros_like(dgn_ref)

        yn, vjp = jax.vjp(_gate_tile, y_ref[...], z_ref[...], gn_ref[...])
        dyn = _dot_nt(_b(dy_ref[...]), w_ref[...])
        dys, dz, dgn = vjp(dyn)
        yn_ref[...] = _b(yn)
        dys_ref[...] = dys
        dz_ref[...] = dz
        dgn_ref[...] += dgn

    return pl.pallas_call(
        body, name="gate_out_bwd",
        grid=(t // tm,),
        in_specs=[pl.BlockSpec((tm, d), lambda i: (i, 0)), pl.BlockSpec((tm, di), lambda i: (i, 0)),
                  pl.BlockSpec((tm, di), lambda i: (i, 0)), pl.BlockSpec((1, di), lambda i: (0, 0)),
                  pl.BlockSpec((di, d), lambda i: (0, 0))],
        out_specs=[pl.BlockSpec((tm, di), lambda i: (i, 0)), pl.BlockSpec((tm, di), lambda i: (i, 0)),
                   pl.BlockSpec((tm, di), lambda i: (i, 0)), pl.BlockSpec((1, di), lambda i: (0, 0))],
        out_shape=[jax.ShapeDtypeStruct((t, n_zx), F32), jax.ShapeDtypeStruct((t, di), F32),
                   jax.ShapeDtypeStruct((t, di), BF16), jax.ShapeDtypeStruct((1, di), F32)],
        compiler_params=_params("arbitrary"),
    )(dy, y, zx, gn, w_out)


def _attn_block(qp, kvp, kvc, biasm, sinks, qg, kg, w_o, first):
    nq = qp.shape[0]
    n_pairs = qp.shape[1] // 128
    hk = n_pairs
    rows = hk * nq
    seg = functools.partial(_xright, m=_blockdiag64(128))
    scale = ATT_HEAD_DIM ** -0.5
    qi = (_iota((rows, 2 * nq), 0) % nq) + nq
    kj = _iota((rows, 2 * nq), 1)
    dist = qi - kj
    valid = (dist >= 0) & (dist < ATT_WINDOW) & (jnp.logical_not(first) | (kj >= nq))
    lo = _iota((nq, 128), 1) < 64
    kv = jnp.concatenate([kvp, kvc], axis=0)
    outs = [None] * n_pairs
    for kvh in range(2):
        kn = _group64_rms(kv[:, kvh * 128:(kvh + 1) * 128], seg) * kg
        vv = kv[:, 256 + kvh * 128:256 + (kvh + 1) * 128]
        pairs = range(kvh * hk // 2, (kvh + 1) * hk // 2)
        qs, sk = [], []
        for p in pairs:
            qn = _group64_rms(qp[:, p * 128:(p + 1) * 128], seg) * qg
            qs += [jnp.where(lo, qn, 0.0), jnp.where(lo, 0.0, qn)]
            sk += [jnp.broadcast_to(sinks[:, h:h + 1], (nq, 1)) for h in (2 * p, 2 * p + 1)]
        sink = jnp.concatenate(sk, axis=0)
        s = _bmm_nt(jnp.concatenate(qs, axis=0), kn) * scale + biasm[kvh * rows:(kvh + 1) * rows]
        s = jnp.where(valid, s, NEG)
        m = lax.stop_gradient(jnp.maximum(jnp.max(s, axis=-1, keepdims=True), sink))
        pexp = jnp.exp(s - m)
        den = jnp.sum(pexp, axis=-1, keepdims=True) + jnp.exp(sink - m)
        o = _bmm(pexp * (1.0 / den), vv)
        for n, p in enumerate(pairs):
            outs[p] = jnp.where(lo, o[2 * n * nq:(2 * n + 1) * nq], o[(2 * n + 1) * nq:(2 * n + 2) * nq])
    o = jnp.concatenate(outs, axis=1)
    return _bmm(o, w_o), o


def attn_fwd(h, qp, kvd, biasm, sinks, qg, kg, w_o):
    t, d = h.shape
    nq = ATT_WINDOW
    nb = t // nq
    nh = qp.shape[1] // ATT_HEAD_DIM

    def body(h_ref, q_ref, kp_ref, kc_ref, bias_ref, s_ref, qg_ref, kg_ref, w_ref, o_ref):
        out, _ = _attn_block(q_ref[...], kp_ref[...], kc_ref[...], bias_ref[...], s_ref[...], qg_ref[...],
                             kg_ref[...], w_ref[...], pl.program_id(0) == 0)
        o_ref[...] = h_ref[...] + out

    vec = pl.BlockSpec((1, 128), lambda i: (0, 0))
    return pl.pallas_call(
        body, name="attn_fwd",
        grid=(nb,),
        in_specs=[pl.BlockSpec((nq, d), lambda i: (i, 0)), pl.BlockSpec((nq, nh * 64), lambda i: (i, 0)),
                  pl.BlockSpec((nq, 512), lambda i: (jnp.maximum(i - 1, 0), 0)),
                  pl.BlockSpec((nq, 512), lambda i: (i, 0)),
                  pl.BlockSpec((nh * nq, 2 * nq), lambda i: (0, 0)), vec, vec, vec,
                  pl.BlockSpec((nh * 64, d), lambda i: (0, 0))],
        out_specs=pl.BlockSpec((nq, d), lambda i: (i, 0)),
        out_shape=jax.ShapeDtypeStruct((t, d), F32),
        compiler_params=_params("arbitrary"),
    )(h, qp, kvd, kvd, biasm, sinks, qg, kg, w_o)


def attn_bwd(dy, qp, kvd, biasm, sinks, qg, kg, w_o):
    t, d = dy.shape
    nq = ATT_WINDOW
    nb = t // nq
    nh = qp.shape[1] // ATT_HEAD_DIM

    def body(dy_ref, q_ref, kp_ref, kc_ref, bias_ref, s_ref, qg_ref, kg_ref, w_ref,
             dq_ref, dkv_ref, o_ref, dbias_ref, ds_ref, dqg_ref, dkg_ref, carry):
        i = pl.program_id(0)

        @pl.when(i == 0)
        def _():
            carry[...] = jnp.zeros_like(carry)
            dbias_ref[...] = jnp.zeros_like(dbias_ref)
            ds_ref[...] = jnp.zeros_like(ds_ref)
            dqg_ref[...] = jnp.zeros_like(dqg_ref)
            dkg_ref[...] = jnp.zeros_like(dkg_ref)

        @pl.when(i < nb)
        def _():
            fn = functools.partial(_attn_block, w_o=w_ref[...], first=(i == 0))
            (_, o), vjp = jax.vjp(fn, q_ref[...], kp_ref[...], kc_ref[...], bias_ref[...], s_ref[...],
                                  qg_ref[...], kg_ref[...])
            dq, dkp, dkc, dbias, dsk, dqg, dkg = vjp((dy_ref[...], jnp.zeros((nq, nh * 64), F32)))
            dq_ref[...] = dq
            o_ref[...] = _b(o)
            dkv_ref[...] = _fold64(carry[...] + dkp)
            carry[...] = dkc
            dbias_ref[...] += dbias
            ds_ref[...] += dsk
            dqg_ref[...] += _fold64(dqg)
            dkg_ref[...] += _fold64(dkg)

        @pl.when(i == nb)
        def _():
            dkv_ref[...] = _fold64(carry[...])

    cl = lambda i: jnp.minimum(i, nb - 1)
    vec = pl.BlockSpec((1, 128), lambda i: (0, 0))
    return pl.pallas_call(
        body, name="attn_bwd",
        grid=(nb + 1,),
        in_specs=[pl.BlockSpec((nq, d), lambda i: (cl(i), 0)), pl.BlockSpec((nq, nh * 64), lambda i: (cl(i), 0)),
                  pl.BlockSpec((nq, 512), lambda i: (jnp.maximum(cl(i) - 1, 0), 0)),
                  pl.BlockSpec((nq, 512), lambda i: (cl(i), 0)),
                  pl.BlockSpec((nh * nq, 2 * nq), lambda i: (0, 0)), vec, vec, vec,
                  pl.BlockSpec((nh * 64, d), lambda i: (0, 0))],
        out_specs=[pl.BlockSpec((nq, nh * 64), lambda i: (cl(i), 0)),
                   pl.BlockSpec((nq, 512), lambda i: (jnp.maximum(i - 1, 0), 0)),
                   pl.BlockSpec((nq, nh * 64), lambda i: (cl(i), 0)),
                   pl.BlockSpec((nh * nq, 2 * nq), lambda i: (0, 0)), vec, vec, vec],
        out_shape=[jax.ShapeDtypeStruct((t, nh * 64), F32), jax.ShapeDtypeStruct((t, 512), F32),
                   jax.ShapeDtypeStruct((t, nh * 64), BF16), jax.ShapeDtypeStruct((nh * nq, 2 * nq), F32),
                   jax.ShapeDtypeStruct((1, 128), F32), jax.ShapeDtypeStruct((1, 128), F32),
                   jax.ShapeDtypeStruct((1, 128), F32)],
        scratch_shapes=[pltpu.VMEM((nq, 512), F32)],
        compiler_params=_params("arbitrary"),
    )(dy, qp, kvd, kvd, biasm, sinks, qg, kg, w_o)


def _t5_buckets():
    nq = ATT_WINDOW
    dist = (np.arange(nq)[:, None] + nq) - np.arange(2 * nq)[None, :]
    n = np.maximum(dist, 0)
    max_exact = REL_BUCKETS // 2
    nf = np.maximum(n, 1).astype(np.float32)
    large = max_exact + (np.log(nf / max_exact) / math.log(ATT_WINDOW / max_exact)
                         * (REL_BUCKETS - max_exact)).astype(np.int32)
    large = np.minimum(large, REL_BUCKETS - 1)
    return np.where(n < max_exact, n, large).astype(np.int32)


def rel_bias_bwd(dbias, buckets):
    nh = dbias.shape[0]

    def body(db_ref, bk_ref, o_ref):
        bk = bk_ref[...]
        lane = _iota((1, 128), 1)
        row = _iota((REL_BUCKETS, 128), 0)
        acc = jnp.zeros((REL_BUCKETS, 128), F32)
        for h in range(nh):
            dbh = db_ref[h]
            for b in range(REL_BUCKETS):
                v = jnp.sum(jnp.where(bk == b, dbh, 0.0))
                acc = acc + jnp.where((row == b) & (lane == h), v, 0.0)
        o_ref[...] = acc

    return pl.pallas_call(
        body, name="rel_bias_bwd",
        out_shape=jax.ShapeDtypeStruct((REL_BUCKETS, 128), F32),
        compiler_params=_params(),
    )(dbias, buckets)


def loss_head(y, target):
    t, d = y.shape
    tm = _tile(t, 512)

    def body(y_ref, t_ref, l_ref, dy_ref):
        @pl.when(pl.program_id(0) == 0)
        def _():
            l_ref[...] = jnp.zeros_like(l_ref)

        e = y_ref[...] - t_ref[...]
        l_ref[...] += 0.5 * jnp.sum(jnp.mean(e * e, axis=-1, keepdims=True), axis=0, keepdims=True)
        dy_ref[...] = e * (1.0 / d)

    return pl.pallas_call(
        body, name="loss_head",
        grid=(t // tm,),
        in_specs=[pl.BlockSpec((tm, d), lambda i: (i, 0)), pl.BlockSpec((tm, d), lambda i: (i, 0))],
        out_specs=[pl.BlockSpec((1, 1), lambda i: (0, 0)), pl.BlockSpec((tm, d), lambda i: (i, 0))],
        out_shape=[jax.ShapeDtypeStruct((1, 1), F32), jax.ShapeDtypeStruct((t, d), F32)],
        compiler_params=_params("arbitrary"),
    )(y, target)


def adamw(w, g, m, v):
    r, c = w.shape
    tr = r if r <= 512 else _tile(r, 256)

    def body(w_ref, g_ref, m_ref, v_ref, d_ref, nm_ref, nv_ref):
        gg = g_ref[...]
        nm = ADAM_B1 * m_ref[...] + (1.0 - ADAM_B1) * gg
        nv = ADAM_B2 * v_ref[...] + (1.0 - ADAM_B2) * (gg * gg)
        m_hat = nm / (1.0 - ADAM_B1 ** ADAM_STEP)
        v_hat = nv / (1.0 - ADAM_B2 ** ADAM_STEP)
        d_ref[...] = -ADAM_LR * (m_hat / (jnp.sqrt(v_hat) + ADAM_EPS) + ADAM_WD * w_ref[...])
        nm_ref[...] = nm
        nv_ref[...] = nv

    spec = pl.BlockSpec((tr, c), lambda i: (i, 0))
    shp = jax.ShapeDtypeStruct((r, c), F32)
    return pl.pallas_call(
        body, name="adamw",
        grid=(r // tr,),
        in_specs=[spec] * 4, out_specs=[spec] * 3, out_shape=[shp] * 3,
        compiler_params=_params("arbitrary"),
    )(w, g, m, v)


def _my_pos():
    return lax.axis_index("x"), lax.axis_index("y"), lax.axis_index("c")


def _other_chips(x, y):
    return [(1 - x, y), (x, 1 - y), (1 - x, 1 - y)]


def _chip_id(x, y):
    return 2 * x + y


def gather_weights(bufs):
    n = len(bufs)

    def body(*refs):
        outs = refs[n:2 * n]
        send_sems, recv_sems = refs[2 * n:]
        x, y, c = _my_pos()
        sibling = (x, y, 1 - c)
        chips = _other_chips(x, y)

        def copy(p, k, chip, half, to):
            blk = outs[p].at[_chip_id(*chip), half]
            return pltpu.make_async_remote_copy(
                src_ref=blk, dst_ref=blk, send_sem=send_sems.at[p, k], recv_sem=recv_sems.at[p, k],
                device_id=to, device_id_type=MESH)

        first = [[copy(p, j, (x, y), c, (*chip, c)) for j, chip in enumerate(chips)] for p in range(n)]
        for p in range(n):
            for cp in first[p]:
                cp.start()
        passed = [[copy(p, 3 + j, chip, c, sibling) for j, chip in enumerate(chips)] for p in range(n)]
        for p in range(n):
            for j, chip in enumerate(chips):
                copy(p, j, chip, c, (x, y, c)).wait_recv()
                passed[p][j].start()
        for p in range(n):
            for j, chip in enumerate(chips):
                copy(p, 3 + j, chip, 1 - c, (x, y, c)).wait_recv()
        for p in range(n):
            for cp in first[p] + passed[p]:
                cp.wait_send()

    any_spec = pl.BlockSpec(memory_space=pl.ANY)
    return pl.pallas_call(
        body, name="gather_weights",
        in_specs=[any_spec] * n, out_specs=[any_spec] * n,
        out_shape=[jax.ShapeDtypeStruct(b.shape, b.dtype) for b in bufs],
        scratch_shapes=[pltpu.SemaphoreType.DMA((n, 6)), pltpu.SemaphoreType.DMA((n, 6))],
        input_output_aliases={p: p for p in range(n)},
    )(*bufs)


def allreduce_small(v):
    r, c = v.shape

    def body(v_ref, o_ref, buf, send_sems, recv_sems):
        x, y, cc = _my_pos()
        me = 4 * x + 2 * y + cc
        buf[me] = v_ref[...]
        copies = []
        for k in range(1, 8):
            dx, dy, dc = (k >> 2) & 1, (k >> 1) & 1, k & 1
            peer = (x ^ dx, y ^ dy, cc ^ dc)
            cp = pltpu.make_async_remote_copy(
                src_ref=v_ref, dst_ref=buf.at[me], send_sem=send_sems.at[k - 1], recv_sem=recv_sems.at[k - 1],
                device_id=peer, device_id_type=MESH)
            cp.start()
            copies.append(cp)
        for cp in copies:
            cp.wait_recv()
        for cp in copies:
            cp.wait_send()
        acc = buf[0]
        for k in range(1, 8):
            acc = acc + buf[k]
        o_ref[...] = acc

    vm = pl.BlockSpec(memory_space=pltpu.VMEM)
    return pl.pallas_call(
        body, name="allreduce_small",
        in_specs=[vm], out_specs=vm,
        out_shape=jax.ShapeDtypeStruct((r, c), F32),
        scratch_shapes=[pltpu.VMEM((8, r, c), F32), pltpu.SemaphoreType.DMA((7,)), pltpu.SemaphoreType.DMA((7,))],
    )(v)


def exchange_sibling_halves(grads):
    n = len(grads)

    def body(*refs):
        ins, outs = refs[:n], refs[n:2 * n]
        send_sems, recv_sems = refs[2 * n:]
        x, y, c = _my_pos()
        copies = []
        for p in range(n):
            cp = pltpu.make_async_remote_copy(
                src_ref=ins[p].at[:, 1 - c], dst_ref=outs[p], send_sem=send_sems.at[p], recv_sem=recv_sems.at[p],
                device_id=(x, y, 1 - c), device_id_type=MESH)
            cp.start()
            copies.append(cp)
        for cp in copies:
            cp.wait_recv()
        for cp in copies:
            cp.wait_send()

    any_spec = pl.BlockSpec(memory_space=pl.ANY)
    return pl.pallas_call(
        body, name="exchange_sibling_halves",
        in_specs=[any_spec] * n, out_specs=[any_spec] * n,
        out_shape=[jax.ShapeDtypeStruct((g.shape[0],) + g.shape[2:], g.dtype) for g in grads],
        scratch_shapes=[pltpu.SemaphoreType.DMA((n,)), pltpu.SemaphoreType.DMA((n,))],
    )(*grads)


def exchange_chip_partials(parts):
    n = len(parts)

    def body(*refs):
        ins, outs = refs[:n], refs[n:2 * n]
        send_sems, recv_sems = refs[2 * n:]
        x, y, c = _my_pos()
        chips = _other_chips(x, y)
        copies = []
        for p in range(n):
            for j, chip in enumerate(chips):
                cp = pltpu.make_async_remote_copy(
                    src_ref=ins[p].at[_chip_id(*chip)], dst_ref=outs[p].at[j],
                    send_sem=send_sems.at[p, j], recv_sem=recv_sems.at[p, j],
                    device_id=(*chip, c), device_id_type=MESH)
                cp.start()
                copies.append(cp)
        for cp in copies:
            cp.wait_recv()
        for cp in copies:
            cp.wait_send()

    any_spec = pl.BlockSpec(memory_space=pl.ANY)
    return pl.pallas_call(
        body, name="exchange_chip_partials",
        in_specs=[any_spec] * n, out_specs=[any_spec] * n,
        out_shape=[jax.ShapeDtypeStruct((3,) + s.shape[1:], s.dtype) for s in parts],
        scratch_shapes=[pltpu.SemaphoreType.DMA((n, 3)), pltpu.SemaphoreType.DMA((n, 3))],
    )(*parts)


def share_with_sibling(bufs):
    n = len(bufs)

    def body(*refs):
        outs = refs[n:2 * n]
        send_sems, recv_sems = refs[2 * n:]
        x, y, c = _my_pos()
        copies = []
        for p in range(n):
            cp = pltpu.make_async_remote_copy(
                src_ref=outs[p].at[c], dst_ref=outs[p].at[c], send_sem=send_sems.at[p], recv_sem=recv_sems.at[p],
                device_id=(x, y, 1 - c), device_id_type=MESH)
            cp.start()
            copies.append(cp)
        for p in range(n):
            pltpu.make_async_remote_copy(
                src_ref=outs[p].at[1 - c], dst_ref=outs[p].at[1 - c], send_sem=send_sems.at[p],
                recv_sem=recv_sems.at[p], device_id=(x, y, 1 - c), device_id_type=MESH).wait_recv()
        for cp in copies:
            cp.wait_send()

    any_spec = pl.BlockSpec(memory_space=pl.ANY)
    return pl.pallas_call(
        body, name="share_with_sibling",
        in_specs=[any_spec] * n, out_specs=[any_spec] * n,
        out_shape=[jax.ShapeDtypeStruct(b.shape, b.dtype) for b in bufs],
        scratch_shapes=[pltpu.SemaphoreType.DMA((n,)), pltpu.SemaphoreType.DMA((n,))],
        input_output_aliases={p: p for p in range(n)},
    )(*bufs)


def add_sibling(g, recv, half):
    _, _, r, c = g.shape
    tr = _tile(r, 256) if r % 256 == 0 else r

    def body(half_ref, g_ref, r_ref, o32_ref, o16_ref):
        s = g_ref[...] + r_ref[...]
        o32_ref[...] = s
        o16_ref[...] = _b(s)

    return pl.pallas_call(
        body, name="add_sibling",
        grid_spec=pltpu.PrefetchScalarGridSpec(
            num_scalar_prefetch=1, grid=(N_CHIPS, r // tr),
            in_specs=[pl.BlockSpec((None, None, tr, c), lambda k, i, hf: (k, hf[0], i, 0)),
                      pl.BlockSpec((None, tr, c), lambda k, i, hf: (k, i, 0))],
            out_specs=[pl.BlockSpec((None, tr, c), lambda k, i, hf: (k, i, 0)),
                       pl.BlockSpec((None, tr, c), lambda k, i, hf: (k, i, 0))]),
        out_shape=[jax.ShapeDtypeStruct((N_CHIPS, r, c), F32), jax.ShapeDtypeStruct((N_CHIPS, r, c), BF16)],
        compiler_params=_params("arbitrary", "arbitrary"),
    )(half, g, recv)


def add_chip_partials(p32, recv, pos):
    _, r, c = p32.shape
    tr = _tile(r, 256) if r % 256 == 0 else r

    def body(pos_ref, p_ref, r_ref, o_ref):
        acc = p_ref[...]
        for j in range(N_CHIPS - 1):
            acc = acc + r_ref[j].astype(F32)
        o_ref[...] = acc

    return pl.pallas_call(
        body, name="add_chip_partials",
        grid_spec=pltpu.PrefetchScalarGridSpec(
            num_scalar_prefetch=1, grid=(r // tr,),
            in_specs=[pl.BlockSpec((None, tr, c), lambda i, ps: (ps[0], i, 0)),
                      pl.BlockSpec((N_CHIPS - 1, tr, c), lambda i, ps: (0, i, 0))],
            out_specs=pl.BlockSpec((None, tr, c), lambda i, ps: (ps[1], i, 0))),
        out_shape=jax.ShapeDtypeStruct((2, r, c), F32),
        compiler_params=_params("arbitrary"),
    )(pos, p32, recv)


def cast_into_gather(w, pos):
    _, r, c = w.shape
    tr = _tile(r, 256) if r % 256 == 0 else r

    def body(pos_ref, w_ref, o_ref):
        o_ref[...] = _b(w_ref[...])

    return pl.pallas_call(
        body, name="cast_into_gather",
        grid_spec=pltpu.PrefetchScalarGridSpec(
            num_scalar_prefetch=1, grid=(2, r // tr),
            in_specs=[pl.BlockSpec((None, tr, c), lambda hf, i, ps: (hf, i, 0))],
            out_specs=pl.BlockSpec((None, None, tr, c), lambda hf, i, ps: (ps[0], hf, i, 0))),
        out_shape=jax.ShapeDtypeStruct((N_CHIPS, 2, r, c), BF16),
        compiler_params=_params("arbitrary", "arbitrary"),
    )(pos, w)


def build_bias(rel, buckets):
    nb, nh = rel.shape

    def body(rel_ref, bk_ref, o_ref):
        bk = bk_ref[...]
        for h in range(nh):
            acc = jnp.zeros(bk.shape, F32)
            for b in range(nb):
                acc = jnp.where(bk == b, rel_ref[b, h], acc)
            o_ref[h] = acc

    return pl.pallas_call(
        body, name="build_bias",
        in_specs=[pl.BlockSpec(memory_space=pltpu.SMEM), pl.BlockSpec(memory_space=pltpu.VMEM)],
        out_specs=pl.BlockSpec(memory_space=pltpu.VMEM),
        out_shape=jax.ShapeDtypeStruct((nh,) + buckets.shape, F32),
        compiler_params=_params(),
    )(rel, buckets)


SMALL_ROWS = 256


def kernel(x, ffn_norm, ffn_w1, ffn_w3, ffn_w2, ssm_norm, ssm_w_in, ssm_conv_w, ssm_conv_b, ssm_dt_bias, ssm_a_log, ssm_d, ssm_gate_norm, ssm_w_out, kv_norm, w_kv, k_norm, attn_norm, w_q, q_norm, sinks, w_o, rel_bias, loss_target, m_ffn_norm, m_ffn_w1, m_ffn_w3, m_ffn_w2, m_ssm_norm, m_ssm_w_in, m_ssm_conv_w, m_ssm_conv_b, m_ssm_dt_bias, m_ssm_a_log, m_ssm_d, m_ssm_gate_norm, m_ssm_w_out, m_kv_norm, m_w_kv, m_k_norm, m_attn_norm, m_w_q, m_q_norm, m_sinks, m_w_o, m_rel_bias, v_ffn_norm, v_ffn_w1, v_ffn_w3, v_ffn_w2, v_ssm_norm, v_ssm_w_in, v_ssm_conv_w, v_ssm_conv_b, v_ssm_dt_bias, v_ssm_a_log, v_ssm_d, v_ssm_gate_norm, v_ssm_w_out, v_kv_norm, v_w_kv, v_k_norm, v_attn_norm, v_w_q, v_q_norm, v_sinks, v_w_o, v_rel_bias):
    weights = dict(ffn_norm=ffn_norm, ffn_w1=ffn_w1, ffn_w3=ffn_w3, ffn_w2=ffn_w2, ssm_norm=ssm_norm,
                   ssm_w_in=ssm_w_in, ssm_conv_w=ssm_conv_w, ssm_conv_b=ssm_conv_b, ssm_dt_bias=ssm_dt_bias,
                   ssm_a_log=ssm_a_log, ssm_d=ssm_d, ssm_gate_norm=ssm_gate_norm, ssm_w_out=ssm_w_out,
                   kv_norm=kv_norm, w_kv=w_kv, k_norm=k_norm, attn_norm=attn_norm, w_q=w_q, q_norm=q_norm,
                   sinks=sinks, w_o=w_o, rel_bias=rel_bias)
    m_in = dict(ffn_norm=m_ffn_norm, ffn_w1=m_ffn_w1, ffn_w3=m_ffn_w3, ffn_w2=m_ffn_w2, ssm_norm=m_ssm_norm,
                ssm_w_in=m_ssm_w_in, ssm_conv_w=m_ssm_conv_w, ssm_conv_b=m_ssm_conv_b, ssm_dt_bias=m_ssm_dt_bias,
                ssm_a_log=m_ssm_a_log, ssm_d=m_ssm_d, ssm_gate_norm=m_ssm_gate_norm, ssm_w_out=m_ssm_w_out,
                kv_norm=m_kv_norm, w_kv=m_w_kv, k_norm=m_k_norm, attn_norm=m_attn_norm, w_q=m_w_q, q_norm=m_q_norm,
                sinks=m_sinks, w_o=m_w_o, rel_bias=m_rel_bias)
    v_in = dict(ffn_norm=v_ffn_norm, ffn_w1=v_ffn_w1, ffn_w3=v_ffn_w3, ffn_w2=v_ffn_w2, ssm_norm=v_ssm_norm,
                ssm_w_in=v_ssm_w_in, ssm_conv_w=v_ssm_conv_w, ssm_conv_b=v_ssm_conv_b, ssm_dt_bias=v_ssm_dt_bias,
                ssm_a_log=v_ssm_a_log, ssm_d=v_ssm_d, ssm_gate_norm=v_ssm_gate_norm, ssm_w_out=v_ssm_w_out,
                kv_norm=v_kv_norm, w_kv=v_w_kv, k_norm=v_k_norm, attn_norm=v_attn_norm, w_q=v_w_q, q_norm=v_q_norm,
                sinks=v_sinks, w_o=v_w_o, rel_bias=v_rel_bias)
    return _step(x[0], loss_target[0], weights, m_in, v_in)


BIG = ("ffn_w1", "ffn_w3", "ffn_w2", "ssm_w_in", "ssm_w_out", "w_kv", "w_q", "w_o")
SMALL = (("ffn_norm", True), ("ssm_norm", True), ("ssm_conv_w", True), ("ssm_conv_b", True),
         ("ssm_gate_norm", True), ("ssm_dt_bias", False), ("ssm_a_log", False), ("ssm_d", False),
         ("kv_norm", False), ("k_norm", False), ("attn_norm", False), ("q_norm", False), ("sinks", False),
         ("rel_bias", False))


def _halves_view(a):
    shape = a.shape
    ax = next(i for i, s in enumerate(shape) if s > 1)
    lead = int(np.prod(shape[:ax + 1])) // 2
    c = shape[-1]
    total = int(np.prod(shape))
    assert shape[ax] % 2 == 0
    return a.reshape(2, total // 2 // c, c)


def _small_layout(weights):
    off, table = 0, {}
    for name, sharded in SMALL:
        shape = weights[name].shape
        full = shape[:-1] + (shape[-1] * N_CHIPS,) if sharded else shape
        n = int(np.prod(full))
        table[name] = (off, full, sharded)
        off += n
    assert off <= SMALL_ROWS * 128
    return table


def _place_small(values, table, chip, scale_mask):
    flat = jnp.zeros((SMALL_ROWS * 128,), F32)
    for name, (off, full, sharded) in table.items():
        if not sharded:
            continue
        v = values[name].astype(F32)
        lead = int(np.prod(full[:-1]))
        w = v.shape[-1]
        blk = jnp.zeros((lead, full[-1]), F32)
        blk = lax.dynamic_update_slice(blk, v.reshape(lead, w) * scale_mask, (0, chip * w))
        flat = lax.dynamic_update_slice(flat, blk.reshape(-1), (off,))
    return flat.reshape(SMALL_ROWS, 128)


def _take_small(mat, table, name):
    off, full, _ = table[name]
    n = int(np.prod(full))
    return mat.reshape(-1)[off:off + n].reshape(full)


def _step(x, target, weights, m_in, v_in):
    t, d = x.shape
    xi, yi, ci = lax.axis_index("x"), lax.axis_index("y"), lax.axis_index("c")
    chip = 2 * xi + yi
    pos_arr = jnp.stack([chip, ci]).astype(jnp.int32)
    half_arr = jnp.reshape(ci, (1,)).astype(jnp.int32)

    gathered = dict(zip(BIG, gather_weights([cast_into_gather(_halves_view(weights[n]), pos_arr) for n in BIG])))
    table = _small_layout(weights)
    south = (ci == 0).astype(F32)
    small = allreduce_small(_place_small(weights, table, chip, south))
    sp = {n: _take_small(small, table, n) if sh else weights[n] for n, sh in SMALL}

    fs = weights["ffn_w1"].shape[-1]
    w1 = gathered["ffn_w1"].reshape(N_CHIPS, 2, 2, d, fs)
    w3 = gathered["ffn_w3"].reshape(N_CHIPS, 2, 2, d, fs)
    w2 = gathered["ffn_w2"].reshape(N_CHIPS, 2, 2, fs, d)
    n_in = weights["ssm_w_in"].shape[-1] * N_CHIPS
    di = weights["ssm_w_out"].shape[1] * N_CHIPS
    nheads = di // SSM_HEAD_DIM
    conv_dim = n_in - di - nheads
    w_in_full = jnp.moveaxis(gathered["ssm_w_in"].reshape(N_CHIPS, d, n_in // N_CHIPS), 0, 1).reshape(d, n_in)
    hpg = nheads // SSM_GROUPS

    def spread_heads(v):
        lead = v.shape[:-1]
        v = v.reshape(lead + (SSM_GROUPS, hpg))
        v = jnp.pad(v, [(0, 0)] * len(lead) + [(0, 0), (0, 128 - hpg)])
        return v.reshape(lead + (SSM_GROUPS * 128,))

    def gather_heads(v):
        lead = v.shape[:-1]
        return v.reshape(lead + (SSM_GROUPS, 128))[..., :hpg].reshape(lead + (nheads,))

    dt_col0 = di + conv_dim
    n_zx = dt_col0 + SSM_GROUPS * 128
    w_in = jnp.concatenate([w_in_full[:, :dt_col0], spread_heads(w_in_full[:, dt_col0:])], axis=1)
    w_out = gathered["ssm_w_out"].reshape(di, d)
    nkv = weights["w_kv"].shape[1] // (2 * ATT_HEAD_DIM)
    assert nkv == 2
    wkv_full = gathered["w_kv"].reshape(d, 2 * nkv * ATT_HEAD_DIM)
    wkv_heads = wkv_full.reshape(d, 2 * nkv, 1, ATT_HEAD_DIM)
    w_kvd = jnp.broadcast_to(wkv_heads, (d, 2 * nkv, 2, ATT_HEAD_DIM)).reshape(d, 4 * nkv * ATT_HEAD_DIM)
    wq = gathered["w_q"].reshape(d, -1)
    wo = gathered["w_o"].reshape(-1, d)
    nh = wq.shape[1] // ATT_HEAD_DIM

    ffn_g = sp["ffn_norm"]
    ssm_g = sp["ssm_norm"].reshape(1, d)
    cw = jnp.pad(sp["ssm_conv_w"].reshape(SSM_CONV, conv_dim), [(0, 8 - SSM_CONV), (0, 0)])
    cb = sp["ssm_conv_b"].reshape(1, conv_dim)
    gate_g = sp["ssm_gate_norm"].reshape(1, di)
    dt_bias = spread_heads(sp["ssm_dt_bias"].reshape(1, nheads))
    a_log = spread_heads(sp["ssm_a_log"].reshape(1, nheads))
    d_skip = spread_heads(sp["ssm_d"].reshape(1, nheads))
    kv_g = sp["kv_norm"].reshape(1, d)
    k_g = jnp.tile(sp["k_norm"].reshape(1, ATT_HEAD_DIM), (1, 2))
    attn_g = sp["attn_norm"].reshape(1, d)
    q_g = jnp.tile(sp["q_norm"].reshape(1, ATT_HEAD_DIM), (1, 2))
    sink_row = jnp.pad(sp["sinks"].reshape(1, nh), [(0, 0), (0, 128 - nh)])
    buckets = jnp.asarray(_t5_buckets())
    biasm = build_bias(sp["rel_bias"], buckets).reshape(nh * ATT_WINDOW, 2 * ATT_WINDOW)

    h0 = x
    h1, a00, b00 = ffn_fwd(h0, ffn_g[0, 0].reshape(1, d), w1, w3, w2, 0, 0)
    zx = norm_mm(h1, ssm_g, w_in)
    xc = conv_fwd(zx, cw, cb, di)
    y_ssd, states = ssd_fwd(xc, zx, dt_bias, a_log, d_skip, dt_col0)
    h2 = gate_out_fwd(h1, y_ssd, zx, gate_g, w_out)
    h3, a01, b01 = ffn_fwd(h2, ffn_g[0, 1].reshape(1, d), w1, w3, w2, 0, 1)
    kvd = norm_mm(h3, kv_g, w_kvd)
    h4, a10, b10 = ffn_fwd(h3, ffn_g[1, 0].reshape(1, d), w1, w3, w2, 1, 0)
    qp = norm_mm(h4, attn_g, wq)
    h5 = attn_fwd(h4, qp, kvd, biasm, sink_row, q_g, k_g, wo)
    h6, a11, b11 = ffn_fwd(h5, ffn_g[1, 1].reshape(1, d), w1, w3, w2, 1, 1)
    loss_part, d6 = loss_head(h6, target)
    loss = lax.psum(loss_part[0, 0], ("x", "y", "c"))

    gw1 = [[None, None], [None, None]]
    gw3 = [[None, None], [None, None]]
    gw2 = [[None, None], [None, None]]
    gfn = [[None, None], [None, None]]

    def ffn_back(h_in, dy, a_s, b_s, layer, idx):
        dh, u, da, db, s, dg = ffn_bwd(h_in, dy, ffn_g[layer, idx].reshape(1, d), a_s, b_s, w1, w3, w2, layer, idx)
        gw1[layer][idx] = wgrad_grouped_b(u, da)
        gw3[layer][idx] = wgrad_grouped_b(u, db)
        gw2[layer][idx] = wgrad_grouped_a(s, dy, 0.5)
        gfn[layer][idx] = dg
        return dh

    d5 = ffn_back(h5, d6, a11, b11, 1, 1)
    dqp, dkvd, o16, dbiasm, dsinks, dqg, dkg = attn_bwd(d5, qp, kvd, biasm, sink_row, q_g, k_g, wo)
    g_wo = wgrad(o16, d5)
    d4, u_q, g_attn_norm = norm_mm_bwd(h4, attn_g, wq, dqp, d5)
    g_wq = wgrad(u_q, dqp)
    d3a = ffn_back(h3, d4, a10, b10, 1, 0)
    d3, u_kv, g_kv_norm = norm_mm_bwd(h3, kv_g, w_kvd, dkvd, d3a, 0.5)
    g_wkvd = wgrad(u_kv, dkvd)
    d2 = ffn_back(h2, d3, a01, b01, 0, 1)
    dzx, dy_ssd, yn16, g_gate = gate_out_bwd(d2, y_ssd, zx, gate_g, w_out, n_zx)
    g_wout = wgrad(yn16, d2)
    dzx, dxs, dbm, dcm, g_dtb, g_alog, g_dsk = ssd_bwd(dzx, dy_ssd, xc, zx, states, dt_bias, a_log, d_skip, dt_col0)
    dzx, g_cw, g_cb = conv_bwd(dzx, zx, dxs, dbm, dcm, cw, cb, di)
    d1, u_in, g_ssm_norm = norm_mm_bwd(h1, ssm_g, w_in, dzx, d2)
    g_win = wgrad(u_in, dzx)
    grad_x = ffn_back(h0, d1, a00, b00, 0, 0)
    g_relb = rel_bias_bwd(dbiasm.reshape(nh, ATT_WINDOW, 2 * ATT_WINDOW), buckets)

    def stack_ffn(g):
        return jnp.stack([jnp.stack([g[l][i] for i in range(2)], axis=1) for l in range(2)], axis=1)

    g_win_full = jnp.concatenate([g_win[:, :dt_col0], gather_heads(g_win[:, dt_col0:])], axis=1)
    g_wkv = g_wkvd.reshape(d, 2 * nkv, 2, ATT_HEAD_DIM)[:, :, 0, :].reshape(d, 2 * nkv * ATT_HEAD_DIM)
    big_grads = {
        "ffn_w1": stack_ffn(gw1), "ffn_w3": stack_ffn(gw3), "ffn_w2": stack_ffn(gw2),
        "ssm_w_in": jnp.moveaxis(g_win_full.reshape(d, N_CHIPS, n_in // N_CHIPS), 1, 0),
        "ssm_w_out": g_wout.reshape(N_CHIPS, di // N_CHIPS, d),
        "w_kv": g_wkv.reshape(N_CHIPS, d // N_CHIPS, -1),
        "w_q": g_wq.reshape(N_CHIPS, d // N_CHIPS, -1),
        "w_o": g_wo.reshape(N_CHIPS, -1, d),
    }
    views = []
    for n in BIG:
        shard = weights[n]
        hv = _halves_view(shard)
        views.append(big_grads[n].reshape((N_CHIPS,) + hv.shape))
    recv1 = exchange_sibling_halves(views)
    p32, p16 = zip(*[add_sibling(g, r, half_arr) for g, r in zip(views, recv1)])
    recv2 = exchange_chip_partials(list(p16))
    mine = [add_chip_partials(p, r, pos_arr) for p, r in zip(p32, recv2)]
    full = share_with_sibling(mine)
    grads = {n: f.reshape(weights[n].shape) for n, f in zip(BIG, full)}

    small_grads = {
        "ffn_norm": jnp.stack([jnp.stack([gfn[l][i].reshape(d) for i in range(2)]) for l in range(2)]),
        "ssm_norm": g_ssm_norm.reshape(1, d),
        "ssm_conv_w": g_cw[:SSM_CONV].reshape(1, SSM_CONV, conv_dim),
        "ssm_conv_b": g_cb.reshape(1, conv_dim),
        "ssm_gate_norm": g_gate.reshape(1, di),
        "ssm_dt_bias": gather_heads(g_dtb.reshape(1, -1)), "ssm_a_log": gather_heads(g_alog.reshape(1, -1)),
        "ssm_d": gather_heads(g_dsk.reshape(1, -1)),
        "kv_norm": g_kv_norm.reshape(d), "k_norm": dkg[0, :ATT_HEAD_DIM], "attn_norm": g_attn_norm.reshape(1, d),
        "q_norm": dqg[:, :ATT_HEAD_DIM], "sinks": dsinks[:, :nh], "rel_bias": g_relb[:, :nh],
    }
    flat = jnp.zeros((SMALL_ROWS * 128,), F32)
    for name, (off, fshape, _) in table.items():
        flat = lax.dynamic_update_slice(flat, small_grads[name].astype(F32).reshape(-1), (off,))
    small_sum = allreduce_small(flat.reshape(SMALL_ROWS, 128))
    for name, (off, fshape, sharded) in table.items():
        g = _take_small(small_sum, table, name)
        if sharded:
            w = weights[name].shape[-1]
            lead = int(np.prod(fshape[:-1]))
            g = lax.dynamic_slice(g.reshape(lead, fshape[-1]), (0, chip * w), (lead, w)).reshape(weights[name].shape)
        grads[name] = g.reshape(weights[name].shape)

    names = list(weights)
    deltas, new_m, new_v = {}, {}, {}
    small_names = [n for n, _ in SMALL]
    for n in BIG:
        shp = weights[n].shape
        v2 = lambda a: a.reshape(-1, shp[-1])
        dl, nm, nv = adamw(v2(weights[n]), v2(grads[n]), v2(m_in[n]), v2(v_in[n]))
        deltas[n], new_m[n], new_v[n] = dl.reshape(shp), nm.reshape(shp), nv.reshape(shp)
    sizes = [int(np.prod(weights[n].shape)) for n in small_names]
    tot = sum(sizes)
    rows = -(-tot // 128)
    rows = -(-rows // 8) * 8

    def pack(dct):
        flat = jnp.concatenate([dct[n].reshape(-1) for n in small_names])
        return jnp.pad(flat, (0, rows * 128 - tot), constant_values=1.0).reshape(rows, 128)

    dl, nm, nv = adamw(pack(weights), pack(grads), pack(m_in), pack(v_in))
    off = 0
    for n, sz in zip(small_names, sizes):
        shp = weights[n].shape
        take = lambda a: a.reshape(-1)[off:off + sz].reshape(shp)
        deltas[n], new_m[n], new_v[n] = take(dl), take(nm), take(nv)
        off += sz

    return (loss, grad_x[None], *[grads[n] for n in names], *[deltas[n] for n in names],
            *[new_m[n] for n in names], *[new_v[n] for n in names])
```

```python
import functools
import math

import jax
import jax.numpy as jnp
import numpy as np
from jax import lax
from jax.experimental import pallas as pl
from jax.experimental.pallas import tpu as pltpu

F32 = jnp.float32
BF16 = jnp.bfloat16
EPS = 1e-6
MESH = pl.DeviceIdType.MESH

SSM_HEAD_DIM = 64
SSM_GROUPS = 4
SSM_STATE = 128
SSM_CONV = 4
SSM_CHUNK = 256
ATT_HEAD_DIM = 64
ATT_WINDOW = 128
REL_BUCKETS = 32
N_CHIPS = 4

ADAM_LR = 0.001
ADAM_B1 = 0.9
ADAM_B2 = 0.999
ADAM_EPS = 1e-08
ADAM_WD = 0.01
ADAM_STEP = 10

VMEM_LIMIT_BYTES = 56 * 1024 * 1024
NEG = -1e30


def _params(*sem):
    return pltpu.CompilerParams(dimension_semantics=sem if sem else None, vmem_limit_bytes=VMEM_LIMIT_BYTES)


def _dot(a, b):
    return jnp.dot(a, b, preferred_element_type=F32)


def _dot_nt(a, b):
    return lax.dot_general(a, b, (((1,), (1,)), ((), ())), preferred_element_type=F32)


def _dot_tn(a, b):
    return lax.dot_general(a, b, (((0,), (0,)), ((), ())), preferred_element_type=F32)


def _b(x):
    return x.astype(BF16)


@jax.custom_vjp
def _bmm(a, b):
    return _dot(_b(a), _b(b))


def _bmm_fwd(a, b):
    return _bmm(a, b), (a, b)


def _bmm_bwd(res, g):
    a, b = res
    g16 = _b(g)
    return _dot_nt(g16, _b(b)).astype(a.dtype), _dot_tn(_b(a), g16).astype(b.dtype)


_bmm.defvjp(_bmm_fwd, _bmm_bwd)


@jax.custom_vjp
def _bmm_nt(a, b):
    return _dot_nt(_b(a), _b(b))


def _bmm_nt_fwd(a, b):
    return _bmm_nt(a, b), (a, b)


def _bmm_nt_bwd(res, g):
    a, b = res
    g16 = _b(g)
    return _dot(g16, _b(b)).astype(a.dtype), _dot_tn(g16, _b(a)).astype(b.dtype)


_bmm_nt.defvjp(_bmm_nt_fwd, _bmm_nt_bwd)


@jax.custom_vjp
def _bmm_tn(a, b):
    return _dot_tn(_b(a), _b(b))


def _bmm_tn_fwd(a, b):
    return _bmm_tn(a, b), (a, b)


def _bmm_tn_bwd(res, g):
    a, b = res
    g16 = _b(g)
    return _dot_nt(_b(b), g16).astype(a.dtype), _dot(_b(a), g16).astype(b.dtype)


_bmm_tn.defvjp(_bmm_tn_fwd, _bmm_tn_bwd)


def _split3(x):
    hi = _b(x)
    r = x - hi.astype(F32)
    mid = _b(r)
    lo = _b(r - mid.astype(F32))
    return hi, mid, lo


def _x_left_raw(m, x):
    hi, mid, lo = _split3(x)
    return _dot(m, hi) + _dot(m, mid) + _dot(m, lo)


def _x_left_t_raw(m, x):
    hi, mid, lo = _split3(x)
    return _dot_tn(m, hi) + _dot_tn(m, mid) + _dot_tn(m, lo)


def _x_right_raw(x, m):
    hi, mid, lo = _split3(x)
    return _dot(hi, m) + _dot(mid, m) + _dot(lo, m)


def _x_right_t_raw(x, m):
    hi, mid, lo = _split3(x)
    return _dot_nt(hi, m) + _dot_nt(mid, m) + _dot_nt(lo, m)


@jax.custom_vjp
def _xleft(m, x):
    return _x_left_raw(m, x)


_xleft.defvjp(lambda m, x: (_x_left_raw(m, x), m),
              lambda m, g: (jnp.zeros_like(m), _x_left_t_raw(m, g)))


@jax.custom_vjp
def _xright(x, m):
    return _x_right_raw(x, m)


_xright.defvjp(lambda x, m: (_x_right_raw(x, m), m),
               lambda m, g: (_x_right_t_raw(g, m), jnp.zeros_like(m)))


def _sigmoid(x):
    return 1.0 / (1.0 + jnp.exp(-x))


def _silu(x):
    return x * _sigmoid(x)


def _softplus(x):
    return jnp.maximum(x, 0.0) + jnp.log(1.0 + jnp.exp(-jnp.abs(x)))


def _rms(x):
    return x * lax.rsqrt(jnp.mean(x * x, axis=-1, keepdims=True) + EPS)


def _iota(shape, dim):
    return lax.broadcasted_iota(jnp.int32, shape, dim)


def _blockdiag64(n):
    return jnp.where(_iota((n, n), 0) // 64 == _iota((n, n), 1) // 64, 1.0, 0.0).astype(BF16)


def _group64_rms(x, seg_sum):
    ms = seg_sum(x * x) * (1.0 / 64.0)
    return x * lax.rsqrt(ms + EPS)


def _fold64(x):
    ax = x.ndim - 1
    w = x.shape[ax]
    lo = (_iota(x.shape, ax) % 128) < 64
    return x + jnp.where(lo, pltpu.roll(x, w - 64, ax), pltpu.roll(x, 64, ax))


def _tile(n, want):
    t = min(n, want)
    assert n % t == 0, (n, t)
    return t


def _lane_tile(n, cap=1536):
    if n <= cap:
        return n
    return max(w for w in range(128, cap + 1, 128) if n % w == 0)


def ffn_fwd(h, g, w1, w3, w2, blk):
    t, d = h.shape
    nk, fs = w1.shape[0], w1.shape[-1]
    tm = _tile(t, 512)

    def body(h_ref, g_ref, w1_ref, w3_ref, w2_ref, o_ref, a_ref, b_ref, u_scr, acc):
        k = pl.program_id(1)

        @pl.when(k == 0)
        def _():
            u_scr[...] = _b(_rms(h_ref[...]) * g_ref[...])
            acc[...] = jnp.zeros_like(acc)

        u = u_scr[...]
        a = _dot(u, w1_ref[...])
        b = _dot(u, w3_ref[...])
        a_ref[...] = _b(a)
        b_ref[...] = _b(b)
        acc[...] += _dot(_b(_silu(a) * b), w2_ref[...])

        @pl.when(k == nk - 1)
        def _():
            o_ref[...] = h_ref[...] + 0.5 * acc[...]

    wspec = lambda r, c: pl.BlockSpec((None, None, r, c), lambda i, k: (k, blk, 0, 0))
    return pl.pallas_call(
        body, name="ffn_fwd",
        grid=(t // tm, nk),
        in_specs=[pl.BlockSpec((tm, d), lambda i, k: (i, 0)), pl.BlockSpec((1, d), lambda i, k: (0, 0)),
                  wspec(d, fs), wspec(d, fs), wspec(fs, d)],
        out_specs=[pl.BlockSpec((tm, d), lambda i, k: (i, 0)),
                   pl.BlockSpec((None, tm, fs), lambda i, k: (k, i, 0)),
                   pl.BlockSpec((None, tm, fs), lambda i, k: (k, i, 0))],
        out_shape=[jax.ShapeDtypeStruct((t, d), F32), jax.ShapeDtypeStruct((nk, t, fs), BF16),
                   jax.ShapeDtypeStruct((nk, t, fs), BF16)],
        scratch_shapes=[pltpu.VMEM((tm, d), BF16), pltpu.VMEM((tm, d), F32)],
        compiler_params=_params("arbitrary", "arbitrary"),
    )(h, g, w1, w3, w2)


def ffn_bwd(h, dy, g, a_s, b_s, w1, w3, w2, blk):
    t, d = h.shape
    nk, fs = w1.shape[0], w1.shape[-1]
    tm = _tile(t, 512)

    def body(h_ref, dy_ref, g_ref, a_ref, b_ref, w1_ref, w3_ref, w2_ref,
             dh_ref, u_ref, da_ref, db_ref, s_ref, dg_ref, dyh_scr, du_acc):
        i, k = pl.program_id(0), pl.program_id(1)

        @pl.when(k == 0)
        def _():
            dyh_scr[...] = _b(0.5 * dy_ref[...])
            du_acc[...] = jnp.zeros_like(du_acc)

        @pl.when((k == 0) & (i == 0))
        def _():
            dg_ref[...] = jnp.zeros_like(dg_ref)

        ds = _dot_nt(dyh_scr[...], w2_ref[...])
        a = a_ref[...].astype(F32)
        b = b_ref[...].astype(F32)
        sig = _sigmoid(a)
        sl = a * sig
        s_ref[...] = _b(sl * b)
        da = _b(ds * b * (sig * (1.0 + a * (1.0 - sig))))
        db = _b(ds * sl)
        da_ref[...] = da
        db_ref[...] = db
        du_acc[...] += _dot_nt(da, w1_ref[...]) + _dot_nt(db, w3_ref[...])

        @pl.when(k == nk - 1)
        def _():
            hh = h_ref[...]
            rstd = lax.rsqrt(jnp.mean(hh * hh, axis=-1, keepdims=True) + EPS)
            xh = hh * rstd
            gg = g_ref[...]
            u_ref[...] = _b(xh * gg)
            du = du_acc[...]
            dg_ref[...] += jnp.sum(du * xh, axis=0, keepdims=True)
            dxh = du * gg
            dh_ref[...] = dy_ref[...] + rstd * (dxh - xh * jnp.mean(dxh * xh, axis=-1, keepdims=True))

    wspec = lambda r, c: pl.BlockSpec((None, None, r, c), lambda i, k: (k, blk, 0, 0))
    tok = pl.BlockSpec((tm, d), lambda i, k: (i, 0))
    hid = pl.BlockSpec((None, tm, fs), lambda i, k: (k, i, 0))
    return pl.pallas_call(
        body, name="ffn_bwd",
        grid=(t // tm, nk),
        in_specs=[tok, tok, pl.BlockSpec((1, d), lambda i, k: (0, 0)), hid, hid, wspec(d, fs), wspec(d, fs), wspec(fs, d)],
        out_specs=[tok, tok, hid, hid, hid, pl.BlockSpec((1, d), lambda i, k: (0, 0))],
        out_shape=[jax.ShapeDtypeStruct((t, d), F32), jax.ShapeDtypeStruct((t, d), BF16),
                   jax.ShapeDtypeStruct((nk, t, fs), BF16), jax.ShapeDtypeStruct((nk, t, fs), BF16),
                   jax.ShapeDtypeStruct((nk, t, fs), BF16), jax.ShapeDtypeStruct((1, d), F32)],
        scratch_shapes=[pltpu.VMEM((tm, d), BF16), pltpu.VMEM((tm, d), F32)],
        compiler_params=_params("arbitrary", "arbitrary"),
    )(h, dy, g, a_s, b_s, w1, w3, w2)


def wgrad_grouped_b(a, bs, scale=1.0):
    t, m = a.shape
    ng, _, n = bs.shape
    tk = _tile(t, 2048)

    def body(a_ref, b_ref, o_ref):
        j = pl.program_id(1)

        @pl.when(j == 0)
        def _():
            o_ref[...] = jnp.zeros_like(o_ref)

        o_ref[...] += _dot_tn(_b(a_ref[...]), _b(b_ref[...]))

        if scale != 1.0:
            @pl.when(j == pl.num_programs(1) - 1)
            def _():
                o_ref[...] = o_ref[...] * scale

    return pl.pallas_call(
        body, name="wgrad_gb",
        grid=(ng, t // tk),
        in_specs=[pl.BlockSpec((tk, m), lambda k, j: (j, 0)), pl.BlockSpec((None, tk, n), lambda k, j: (k, j, 0))],
        out_specs=pl.BlockSpec((None, m, n), lambda k, j: (k, 0, 0)),
        out_shape=jax.ShapeDtypeStruct((ng, m, n), F32),
        compiler_params=_params("arbitrary", "arbitrary"),
    )(a, bs)


def wgrad_grouped_a(as_, b, scale=1.0):
    ng, t, m = as_.shape
    n = b.shape[1]
    tk = _tile(t, 2048)

    def body(a_ref, b_ref, o_ref):
        j = pl.program_id(1)

        @pl.when(j == 0)
        def _():
            o_ref[...] = jnp.zeros_like(o_ref)

        o_ref[...] += _dot_tn(_b(a_ref[...]), _b(b_ref[...]))

        if scale != 1.0:
            @pl.when(j == pl.num_programs(1) - 1)
            def _():
                o_ref[...] = o_ref[...] * scale

    return pl.pallas_call(
        body, name="wgrad_ga",
        grid=(ng, t // tk),
        in_specs=[pl.BlockSpec((None, tk, m), lambda k, j: (k, j, 0)), pl.BlockSpec((tk, n), lambda k, j: (j, 0))],
        out_specs=pl.BlockSpec((None, m, n), lambda k, j: (k, 0, 0)),
        out_shape=jax.ShapeDtypeStruct((ng, m, n), F32),
        compiler_params=_params("arbitrary", "arbitrary"),
    )(as_, b)


def wgrad(a, b):
    t, m = a.shape
    n = b.shape[1]
    tk = _tile(t, 1024)
    tn = _lane_tile(n, 1536 if m <= 1024 else 512)

    def body(a_ref, b_ref, o_ref):
        @pl.when(pl.program_id(1) == 0)
        def _():
            o_ref[...] = jnp.zeros_like(o_ref)

        o_ref[...] += _dot_tn(_b(a_ref[...]), _b(b_ref[...]))

    return pl.pallas_call(
        body, name="wgrad",
        grid=(n // tn, t // tk),
        in_specs=[pl.BlockSpec((tk, m), lambda c, j: (j, 0)), pl.BlockSpec((tk, tn), lambda c, j: (j, c))],
        out_specs=pl.BlockSpec((m, tn), lambda c, j: (0, c)),
        out_shape=jax.ShapeDtypeStruct((m, n), F32),
        compiler_params=_params("arbitrary", "arbitrary"),
    )(a, b)


def norm_mm(h, g, w):
    t, d = h.shape
    n = w.shape[1]
    tm = _tile(t, 512)
    tn = _lane_tile(n)

    def body(h_ref, g_ref, w_ref, o_ref, u_scr):
        @pl.when(pl.program_id(1) == 0)
        def _():
            u_scr[...] = _b(_rms(h_ref[...]) * g_ref[...])

        o_ref[...] = _dot(u_scr[...], w_ref[...])

    return pl.pallas_call(
        body, name="norm_mm",
        grid=(t // tm, n // tn),
        in_specs=[pl.BlockSpec((tm, d), lambda i, j: (i, 0)), pl.BlockSpec((1, d), lambda i, j: (0, 0)),
                  pl.BlockSpec((d, tn), lambda i, j: (0, j))],
        out_specs=pl.BlockSpec((tm, tn), lambda i, j: (i, j)),
        out_shape=jax.ShapeDtypeStruct((t, n), F32),
        scratch_shapes=[pltpu.VMEM((tm, d), BF16)],
        compiler_params=_params("arbitrary", "arbitrary"),
    )(h, g, w)


def norm_mm_bwd(h, g, w, dout, dres, scale=1.0):
    t, d = h.shape
    n = w.shape[1]
    tm = _tile(t, 512)
    tn = _lane_tile(n)
    nj = n // tn

    def body(h_ref, g_ref, w_ref, do_ref, dr_ref, dh_ref, u_ref, dg_ref, du_acc):
        i, j = pl.program_id(0), pl.program_id(1)

        @pl.when(j == 0)
        def _():
            du_acc[...] = jnp.zeros_like(du_acc)

        @pl.when((j == 0) & (i == 0))
        def _():
            dg_ref[...] = jnp.zeros_like(dg_ref)

        du_acc[...] += _dot_nt(_b(do_ref[...]), w_ref[...])

        @pl.when(j == nj - 1)
        def _():
            hh = h_ref[...]
            rstd = lax.rsqrt(jnp.mean(hh * hh, axis=-1, keepdims=True) + EPS)
            xh = hh * rstd
            gg = g_ref[...]
            u_ref[...] = _b(xh * gg)
            du = du_acc[...] * scale
            dg_ref[...] += jnp.sum(du * xh, axis=0, keepdims=True)
            dxh = du * gg
            dh_ref[...] = dr_ref[...] + rstd * (dxh - xh * jnp.mean(dxh * xh, axis=-1, keepdims=True))

    tok = pl.BlockSpec((tm, d), lambda i, j: (i, 0))
    return pl.pallas_call(
        body, name="norm_mm_bwd",
        grid=(t // tm, nj),
        in_specs=[tok, pl.BlockSpec((1, d), lambda i, j: (0, 0)), pl.BlockSpec((d, tn), lambda i, j: (0, j)),
                  pl.BlockSpec((tm, tn), lambda i, j: (i, j)), tok],
        out_specs=[tok, tok, pl.BlockSpec((1, d), lambda i, j: (0, 0))],
        out_shape=[jax.ShapeDtypeStruct((t, d), F32), jax.ShapeDtypeStruct((t, d), BF16),
                   jax.ShapeDtypeStruct((1, d), F32)],
        scratch_shapes=[pltpu.VMEM((tm, d), F32)],
        compiler_params=_params("arbitrary", "arbitrary"),
    )(h, g, w, dout, dres)


CONV_COLS = 512


CONV_ROWS = 64


def _conv_pre(ext, w, b, r0, n):
    return (b + w[0:1] * ext[pl.ds(5 + r0, n), :] + w[1:2] * ext[pl.ds(6 + r0, n), :]
            + w[2:3] * ext[pl.ds(7 + r0, n), :] + w[3:4] * ext[pl.ds(8 + r0, n), :])


def conv_fwd(zx, cw, cb, col0):
    t = zx.shape[0]
    c = cw.shape[1]
    tm = _tile(t, 512)
    cb0 = col0 // CONV_COLS

    rc = _tile(tm, CONV_ROWS)

    def body(x_ref, w_ref, b_ref, o_ref, ext):
        @pl.when(pl.program_id(1) == 0)
        def _():
            ext[0:8, :] = jnp.zeros((8, CONV_COLS), F32)

        ext[8:, :] = x_ref[...]
        w, b = w_ref[...], b_ref[...]
        for r0 in range(0, tm, rc):
            o_ref[r0:r0 + rc, :] = _silu(_conv_pre(ext, w, b, r0, rc))
        ext[0:8, :] = ext[tm:tm + 8, :]

    return pl.pallas_call(
        body, name="conv_fwd",
        grid=(c // CONV_COLS, t // tm),
        in_specs=[pl.BlockSpec((tm, CONV_COLS), lambda j, i: (i, cb0 + j)),
                  pl.BlockSpec((8, CONV_COLS), lambda j, i: (0, j)), pl.BlockSpec((1, CONV_COLS), lambda j, i: (0, j))],
        out_specs=pl.BlockSpec((tm, CONV_COLS), lambda j, i: (i, j)),
        out_shape=jax.ShapeDtypeStruct((t, c), F32),
        scratch_shapes=[pltpu.VMEM((tm + 8, CONV_COLS), F32)],
        compiler_params=_params("arbitrary", "arbitrary"),
    )(zx, cw, cb)


def conv_bwd(dzx, zx, dxs, dbm, dcm, cw, cb, col0):
    t = zx.shape[0]
    c = cw.shape[1]
    tm = _tile(t, 512)
    nt = t // tm
    cb0 = col0 // CONV_COLS
    nxs = dxs.shape[1] // CONV_COLS
    hb = tm // 8

    rc = _tile(tm, CONV_ROWS)

    def body(dzx_ref, x_ref, xh_ref, dxs_ref, db_ref, dc_ref, w_ref, b_ref, o_ref, dw_ref, dbias_ref, ext, gy):
        j, i = pl.program_id(0), pl.program_id(1)
        ri = nt - 1 - i

        @pl.when(i == 0)
        def _():
            gy[tm:tm + 8, :] = jnp.zeros((8, CONV_COLS), F32)
            dw_ref[...] = jnp.zeros_like(dw_ref)
            dbias_ref[...] = jnp.zeros_like(dbias_ref)

        ext[0:8, :] = jnp.where(ri > 0, xh_ref[...], 0.0)
        ext[8:, :] = x_ref[...]
        w, b = w_ref[...], b_ref[...]
        dw = [jnp.zeros((1, CONV_COLS), F32) for _ in range(SSM_CONV)]
        dbias = jnp.zeros((1, CONV_COLS), F32)
        for r0 in range(0, tm, rc):
            rows = pl.ds(r0, rc)
            win = [ext[pl.ds(5 + tap + r0, rc), :] for tap in range(SSM_CONV)]
            y = b + w[0:1] * win[0] + w[1:2] * win[1] + w[2:3] * win[2] + w[3:4] * win[3]
            sig = _sigmoid(y)
            dout = jnp.where(j < nxs, dxs_ref[rows, :], jnp.where(j == nxs, db_ref[rows, :], dc_ref[rows, :]))
            g = dout * (sig * (1.0 + y * (1.0 - sig)))
            gy[rows, :] = g
            dbias = dbias + jnp.sum(g, axis=0, keepdims=True)
            for tap in range(SSM_CONV):
                dw[tap] = dw[tap] + jnp.sum(g * win[tap], axis=0, keepdims=True)
        for r0 in range(0, tm, rc):
            o_ref[r0:r0 + rc, :] = (w[0:1] * gy[pl.ds(r0 + 3, rc), :] + w[1:2] * gy[pl.ds(r0 + 2, rc), :]
                                    + w[2:3] * gy[pl.ds(r0 + 1, rc), :] + w[3:4] * gy[pl.ds(r0, rc), :])
        gy[tm:tm + 8, :] = gy[0:8, :]
        for tap in range(SSM_CONV):
            dw_ref[tap:tap + 1, :] += dw[tap]
        dbias_ref[...] += dbias

    return pl.pallas_call(
        body, name="conv_bwd",
        grid=(c // CONV_COLS, nt),
        in_specs=[pl.BlockSpec(memory_space=pl.ANY),
                  pl.BlockSpec((tm, CONV_COLS), lambda j, i: (nt - 1 - i, cb0 + j)),
                  pl.BlockSpec((8, CONV_COLS), lambda j, i: (jnp.maximum((nt - 1 - i) * hb - 1, 0), cb0 + j)),
                  pl.BlockSpec((tm, CONV_COLS), lambda j, i: (nt - 1 - i, jnp.minimum(j, nxs - 1))),
                  pl.BlockSpec((tm, CONV_COLS), lambda j, i: (nt - 1 - i, 0)),
                  pl.BlockSpec((tm, CONV_COLS), lambda j, i: (nt - 1 - i, 0)),
                  pl.BlockSpec((8, CONV_COLS), lambda j, i: (0, j)), pl.BlockSpec((1, CONV_COLS), lambda j, i: (0, j))],
        out_specs=[pl.BlockSpec((tm, CONV_COLS), lambda j, i: (nt - 1 - i, cb0 + j)),
                   pl.BlockSpec((8, CONV_COLS), lambda j, i: (0, j)), pl.BlockSpec((1, CONV_COLS), lambda j, i: (0, j))],
        out_shape=[jax.ShapeDtypeStruct(dzx.shape, F32), jax.ShapeDtypeStruct((8, c), F32),
                   jax.ShapeDtypeStruct((1, c), F32)],
        scratch_shapes=[pltpu.VMEM((tm + 8, CONV_COLS), F32), pltpu.VMEM((tm + 8, CONV_COLS), F32)],
        input_output_aliases={0: 0},
        compiler_params=_params("arbitrary", "arbitrary"),
    )(dzx, zx, zx, dxs, dbm, dcm, cw, cb)


def _ssd_group(xs, bg, cg, dtraw, s0, bias, alog, dsk):
    L = xs.shape[0]
    causal = _iota((L, L), 0) >= _iota((L, L), 1)
    tril = jnp.where(causal, 1.0, 0.0).astype(BF16)
    dt = _softplus(dtraw + bias)
    a = -jnp.exp(alog)
    acum = _xleft(tril, dt * a)
    acum_t = acum.T
    dt_t = dt.T
    cb = _bmm_nt(cg, bg)
    lo = _iota((L, 128), 1) < 64
    lo_row = _iota((1, 128), 1) < 64
    lo_col = _iota((128, 1), 0) < 64
    alast = acum[L - 1:L, :]
    ys, s1s = [], []
    for q in range(4):
        xp = xs[:, q * 128:(q + 1) * 128]
        sp = s0[q * 128:(q + 1) * 128, :]
        yd, ec, wc, el = [], [], [], []
        for j in range(2):
            r = 2 * q + j
            ac = acum[:, r:r + 1]
            decay = jnp.exp(jnp.where(causal, ac - acum_t[r:r + 1, :], NEG))
            yd.append(_bmm(cb * decay * dt_t[r:r + 1, :], xp))
            ec.append(jnp.exp(ac))
            al = alast[:, r:r + 1]
            wc.append(jnp.exp(al - ac) * dt[:, r:r + 1])
            el.append(jnp.exp(al))
        y_off = _bmm_nt(cg, sp) * jnp.where(lo, ec[0], ec[1])
        dsel = jnp.where(lo_row, dsk[:, 2 * q:2 * q + 1], dsk[:, 2 * q + 1:2 * q + 2])
        ys.append(jnp.where(lo, yd[0], yd[1]) + y_off + dsel * xp)
        xw = xp * jnp.where(lo, wc[0], wc[1])
        s1s.append(sp * jnp.where(lo_col, el[0], el[1]) + _bmm_tn(xw, bg))
    return jnp.concatenate(ys, axis=1), jnp.concatenate(s1s, axis=0)


def ssd_fwd(xc, zx, bias, alog, dsk, dt_col0):
    t = xc.shape[0]
    L = _tile(t, SSM_CHUNK)
    nc = t // L
    g = SSM_GROUPS
    dtb = dt_col0 // 128

    def body(xs_ref, b_ref, c_ref, dt_ref, bias_ref, alog_ref, dsk_ref, y_ref, st_ref, state):
        c, gi = pl.program_id(0), pl.program_id(1)

        @pl.when(c == 0)
        def _():
            state[gi] = jnp.zeros((512, 128), F32)

        s0 = state[gi]
        st_ref[...] = s0
        y, s1 = _ssd_group(xs_ref[...], b_ref[...], c_ref[...], dt_ref[...], s0,
                           bias_ref[...], alog_ref[...], dsk_ref[...])
        y_ref[...] = y
        state[gi] = s1

    vec = pl.BlockSpec((1, 128), lambda c, gi: (0, gi))
    return pl.pallas_call(
        body, name="ssd_fwd",
        grid=(nc, g),
        in_specs=[pl.BlockSpec((L, 512), lambda c, gi: (c, gi)),
                  pl.BlockSpec((L, 128), lambda c, gi: (c, 16 + gi)),
                  pl.BlockSpec((L, 128), lambda c, gi: (c, 20 + gi)),
                  pl.BlockSpec((L, 128), lambda c, gi: (c, dtb + gi)), vec, vec, vec],
        out_specs=[pl.BlockSpec((L, 512), lambda c, gi: (c, gi)),
                   pl.BlockSpec((None, None, 512, 128), lambda c, gi: (c, gi, 0, 0))],
        out_shape=[jax.ShapeDtypeStruct((t, 2048), F32), jax.ShapeDtypeStruct((nc, g, 512, 128), F32)],
        scratch_shapes=[pltpu.VMEM((g, 512, 128), F32)],
        compiler_params=_params("arbitrary", "arbitrary"),
    )(xc, xc, xc, zx, bias, alog, dsk)


def ssd_bwd(dzx, dy, xc, zx, states, bias, alog, dsk, dt_col0):
    t = xc.shape[0]
    L = _tile(t, SSM_CHUNK)
    nc = t // L
    g = SSM_GROUPS
    dtb = dt_col0 // 128

    def body(dzx_ref, dy_ref, xs_ref, b_ref, c_ref, dt_ref, st_ref, bias_ref, alog_ref, dsk_ref,
             ddt_ref, dxs_ref, db_ref, dc_ref, dbias_ref, dalog_ref, ddsk_ref, dstate):
        c, gi = pl.program_id(0), pl.program_id(1)

        @pl.when(c == 0)
        def _():
            dstate[gi] = jnp.zeros((512, 128), F32)

        @pl.when((c == 0) & (gi == 0))
        def _():
            dbias_ref[...] = jnp.zeros_like(dbias_ref)
            dalog_ref[...] = jnp.zeros_like(dalog_ref)
            ddsk_ref[...] = jnp.zeros_like(ddsk_ref)

        _, vjp = jax.vjp(_ssd_group, xs_ref[...], b_ref[...], c_ref[...], dt_ref[...], st_ref[...],
                         bias_ref[...], alog_ref[...], dsk_ref[...])
        dxs, db, dc, ddt, ds0, dbias, dalog, ddsk = vjp((dy_ref[...], dstate[gi]))
        dxs_ref[...] = dxs
        db_ref[...] = db
        dc_ref[...] = dc
        ddt_ref[...] = ddt
        dstate[gi] = ds0
        dbias_ref[gi] += dbias
        dalog_ref[gi] += dalog
        ddsk_ref[gi] += ddsk

    rc = lambda c: nc - 1 - c
    vec = pl.BlockSpec((1, 128), lambda c, gi: (0, gi))
    acc = pl.BlockSpec((g, 1, 128), lambda c, gi: (0, 0, 0))
    return pl.pallas_call(
        body, name="ssd_bwd",
        grid=(nc, g),
        in_specs=[pl.BlockSpec(memory_space=pl.ANY),
                  pl.BlockSpec((L, 512), lambda c, gi: (rc(c), gi)),
                  pl.BlockSpec((L, 512), lambda c, gi: (rc(c), gi)),
                  pl.BlockSpec((L, 128), lambda c, gi: (rc(c), 16 + gi)),
                  pl.BlockSpec((L, 128), lambda c, gi: (rc(c), 20 + gi)),
                  pl.BlockSpec((L, 128), lambda c, gi: (rc(c), dtb + gi)),
                  pl.BlockSpec((None, None, 512, 128), lambda c, gi: (rc(c), gi, 0, 0)), vec, vec, vec],
        out_specs=[pl.BlockSpec((L, 128), lambda c, gi: (rc(c), dtb + gi)),
                   pl.BlockSpec((L, 512), lambda c, gi: (rc(c), gi)),
                   pl.BlockSpec((L, 128), lambda c, gi: (rc(c), gi)),
                   pl.BlockSpec((L, 128), lambda c, gi: (rc(c), gi)), acc, acc, acc],
        out_shape=[jax.ShapeDtypeStruct(dzx.shape, F32), jax.ShapeDtypeStruct((t, 2048), F32),
                   jax.ShapeDtypeStruct((t, 512), F32), jax.ShapeDtypeStruct((t, 512), F32),
                   jax.ShapeDtypeStruct((g, 1, 128), F32), jax.ShapeDtypeStruct((g, 1, 128), F32),
                   jax.ShapeDtypeStruct((g, 1, 128), F32)],
        scratch_shapes=[pltpu.VMEM((g, 512, 128), F32)],
        input_output_aliases={0: 0},
        compiler_params=_params("arbitrary", "arbitrary"),
    )(dzx, dy, xc, xc, xc, zx, states, bias, alog, dsk)


def _gate_tile(y, z, gn):
    gated = y * _silu(z)
    parts = [_rms(gated[:, k * 512:(k + 1) * 512]) for k in range(SSM_GROUPS)]
    return jnp.concatenate(parts, axis=1) * gn


def gate_out_fwd(h, y, zx, gn, w_out):
    t, d = h.shape
    di = y.shape[1]
    tm = _tile(t, 256)

    def body(h_ref, y_ref, z_ref, gn_ref, w_ref, o_ref):
        yn = _gate_tile(y_ref[...], z_ref[...], gn_ref[...])
        o_ref[...] = h_ref[...] + _dot(_b(yn), w_ref[...])

    return pl.pallas_call(
        body, name="gate_out_fwd",
        grid=(t // tm,),
        in_specs=[pl.BlockSpec((tm, d), lambda i: (i, 0)), pl.BlockSpec((tm, di), lambda i: (i, 0)),
                  pl.BlockSpec((tm, di), lambda i: (i, 0)), pl.BlockSpec((1, di), lambda i: (0, 0)),
                  pl.BlockSpec((di, d), lambda i: (0, 0))],
        out_specs=pl.BlockSpec((tm, d), lambda i: (i, 0)),
        out_shape=jax.ShapeDtypeStruct((t, d), F32),
        compiler_params=_params("arbitrary"),
    )(h, y, zx, gn, w_out)


def gate_out_bwd(dy, y, zx, gn, w_out, n_zx):
    t, d = dy.shape
    di = y.shape[1]
    tm = _tile(t, 256)

    def body(dy_ref, y_ref, z_ref, gn_ref, w_ref, dz_ref, dys_ref, yn_ref, dgn_ref):
        @pl.when(pl.program_id(0) == 0)
        def _():
            dgn_ref[...] = jnp.zeros_like(dgn_ref)

        yn, vjp = jax.vjp(_gate_tile, y_ref[...], z_ref[...], gn_ref[...])
        dyn = _dot_nt(_b(dy_ref[...]), w_ref[...])
        dys, dz, dgn = vjp(dyn)
        yn_ref[...] = _b(yn)
        dys_ref[...] = dys
        dz_ref[...] = dz
        dgn_ref[...] += dgn

    return pl.pallas_call(
        body, name="gate_out_bwd",
        grid=(t // tm,),
        in_specs=[pl.BlockSpec((tm, d), lambda i: (i, 0)), pl.BlockSpec((tm, di), lambda i: (i, 0)),
                  pl.BlockSpec((tm, di), lambda i: (i, 0)), pl.BlockSpec((1, di), lambda i: (0, 0)),
                  pl.BlockSpec((di, d), lambda i: (0, 0))],
        out_specs=[pl.BlockSpec((tm, di), lambda i: (i, 0)), pl.BlockSpec((tm, di), lambda i: (i, 0)),
                   pl.BlockSpec((tm, di), lambda i: (i, 0)), pl.BlockSpec((1, di), lambda i: (0, 0))],
        out_shape=[jax.ShapeDtypeStruct((t, n_zx), F32), jax.ShapeDtypeStruct((t, di), F32),
                   jax.ShapeDtypeStruct((t, di), BF16), jax.ShapeDtypeStruct((1, di), F32)],
        compiler_params=_params("arbitrary"),
    )(dy, y, zx, gn, w_out)


def _attn_block(qp, kvp, kvc, biasm, sinks, qg, kg, w_o, first):
    nq = qp.shape[0]
    n_pairs = qp.shape[1] // 128
    hk = n_pairs
    rows = hk * nq
    seg = functools.partial(_xright, m=_blockdiag64(128))
    scale = ATT_HEAD_DIM ** -0.5
    qi = (_iota((rows, 2 * nq), 0) % nq) + nq
    kj = _iota((rows, 2 * nq), 1)
    dist = qi - kj
    valid = (dist >= 0) & (dist < ATT_WINDOW) & (jnp.logical_not(first) | (kj >= nq))
    lo = _iota((nq, 128), 1) < 64
    kv = jnp.concatenate([kvp, kvc], axis=0)
    outs = [None] * n_pairs
    for kvh in range(2):
        kn = _group64_rms(kv[:, kvh * 128:(kvh + 1) * 128], seg) * kg
        vv = kv[:, 256 + kvh * 128:256 + (kvh + 1) * 128]
        pairs = range(kvh * hk // 2, (kvh + 1) * hk // 2)
        qs, sk = [], []
        for p in pairs:
            qn = _group64_rms(qp[:, p * 128:(p + 1) * 128], seg) * qg
            qs += [jnp.where(lo, qn, 0.0), jnp.where(lo, 0.0, qn)]
            sk += [jnp.broadcast_to(sinks[:, h:h + 1], (nq, 1)) for h in (2 * p, 2 * p + 1)]
        sink = jnp.concatenate(sk, axis=0)
        s = _bmm_nt(jnp.concatenate(qs, axis=0), kn) * scale + biasm[kvh * rows:(kvh + 1) * rows]
        s = jnp.where(valid, s, NEG)
        m = lax.stop_gradient(jnp.maximum(jnp.max(s, axis=-1, keepdims=True), sink))
        pexp = jnp.exp(s - m)
        den = jnp.sum(pexp, axis=-1, keepdims=True) + jnp.exp(sink - m)
        o = _bmm(pexp * (1.0 / den), vv)
        for n, p in enumerate(pairs):
            outs[p] = jnp.where(lo, o[2 * n * nq:(2 * n + 1) * nq], o[(2 * n + 1) * nq:(2 * n + 2) * nq])
    o = jnp.concatenate(outs, axis=1)
    return _bmm(o, w_o), o


def attn_fwd(h, qp, kvd, biasm, sinks, qg, kg, w_o):
    t, d = h.shape
    nq = ATT_WINDOW
    nb = t // nq
    nh = qp.shape[1] // ATT_HEAD_DIM

    def body(h_ref, q_ref, kp_ref, kc_ref, bias_ref, s_ref, qg_ref, kg_ref, w_ref, o_ref):
        out, _ = _attn_block(q_ref[...], kp_ref[...], kc_ref[...], bias_ref[...], s_ref[...], qg_ref[...],
                             kg_ref[...], w_ref[...], pl.program_id(0) == 0)
        o_ref[...] = h_ref[...] + out

    vec = pl.BlockSpec((1, 128), lambda i: (0, 0))
    return pl.pallas_call(
        body, name="attn_fwd",
        grid=(nb,),
        in_specs=[pl.BlockSpec((nq, d), lambda i: (i, 0)), pl.BlockSpec((nq, nh * 64), lambda i: (i, 0)),
                  pl.BlockSpec((nq, 512), lambda i: (jnp.maximum(i - 1, 0), 0)),
                  pl.BlockSpec((nq, 512), lambda i: (i, 0)),
                  pl.BlockSpec((nh * nq, 2 * nq), lambda i: (0, 0)), vec, vec, vec,
                  pl.BlockSpec((nh * 64, d), lambda i: (0, 0))],
        out_specs=pl.BlockSpec((nq, d), lambda i: (i, 0)),
        out_shape=jax.ShapeDtypeStruct((t, d), F32),
        compiler_params=_params("arbitrary"),
    )(h, qp, kvd, kvd, biasm, sinks, qg, kg, w_o)


def attn_bwd(dy, qp, kvd, biasm, sinks, qg, kg, w_o):
    t, d = dy.shape
    nq = ATT_WINDOW
    nb = t // nq
    nh = qp.shape[1] // ATT_HEAD_DIM

    def body(dy_ref, q_ref, kp_ref, kc_ref, bias_ref, s_ref, qg_ref, kg_ref, w_ref,
             dq_ref, dkv_ref, o_ref, dbias_ref, ds_ref, dqg_ref, dkg_ref, carry):
        i = pl.program_id(0)

        @pl.when(i == 0)
        def _():
            carry[...] = jnp.zeros_like(carry)
            dbias_ref[...] = jnp.zeros_like(dbias_ref)
            ds_ref[...] = jnp.zeros_like(ds_ref)
            dqg_ref[...] = jnp.zeros_like(dqg_ref)
            dkg_ref[...] = jnp.zeros_like(dkg_ref)

        @pl.when(i < nb)
        def _():
            fn = functools.partial(_attn_block, w_o=w_ref[...], first=(i == 0))
            (_, o), vjp = jax.vjp(fn, q_ref[...], kp_ref[...], kc_ref[...], bias_ref[...], s_ref[...],
                                  qg_ref[...], kg_ref[...])
            dq, dkp, dkc, dbias, dsk, dqg, dkg = vjp((dy_ref[...], jnp.zeros((nq, nh * 64), F32)))
            dq_ref[...] = dq
            o_ref[...] = _b(o)
            dkv_ref[...] = _fold64(carry[...] + dkp)
            carry[...] = dkc
            dbias_ref[...] += dbias
            ds_ref[...] += dsk
            dqg_ref[...] += _fold64(dqg)
            dkg_ref[...] += _fold64(dkg)

        @pl.when(i == nb)
        def _():
            dkv_ref[...] = _fold64(carry[...])

    cl = lambda i: jnp.minimum(i, nb - 1)
    vec = pl.BlockSpec((1, 128), lambda i: (0, 0))
    return pl.pallas_call(
        body, name="attn_bwd",
        grid=(nb + 1,),
        in_specs=[pl.BlockSpec((nq, d), lambda i: (cl(i), 0)), pl.BlockSpec((nq, nh * 64), lambda i: (cl(i), 0)),
                  pl.BlockSpec((nq, 512), lambda i: (jnp.maximum(cl(i) - 1, 0), 0)),
                  pl.BlockSpec((nq, 512), lambda i: (cl(i), 0)),
                  pl.BlockSpec((nh * nq, 2 * nq), lambda i: (0, 0)), vec, vec, vec,
                  pl.BlockSpec((nh * 64, d), lambda i: (0, 0))],
        out_specs=[pl.BlockSpec((nq, nh * 64), lambda i: (cl(i), 0)),
                   pl.BlockSpec((nq, 512), lambda i: (jnp.maximum(i - 1, 0), 0)),
                   pl.BlockSpec((nq, nh * 64), lambda i: (cl(i), 0)),
                   pl.BlockSpec((nh * nq, 2 * nq), lambda i: (0, 0)), vec, vec, vec],
        out_shape=[jax.ShapeDtypeStruct((t, nh * 64), F32), jax.ShapeDtypeStruct((t, 512), F32),
                   jax.ShapeDtypeStruct((t, nh * 64), BF16), jax.ShapeDtypeStruct((nh * nq, 2 * nq), F32),
                   jax.ShapeDtypeStruct((1, 128), F32), jax.ShapeDtypeStruct((1, 128), F32),
                   jax.ShapeDtypeStruct((1, 128), F32)],
        scratch_shapes=[pltpu.VMEM((nq, 512), F32)],
        compiler_params=_params("arbitrary"),
    )(dy, qp, kvd, kvd, biasm, sinks, qg, kg, w_o)


def _t5_buckets():
    nq = ATT_WINDOW
    dist = (np.arange(nq)[:, None] + nq) - np.arange(2 * nq)[None, :]
    n = np.maximum(dist, 0)
    max_exact = REL_BUCKETS // 2
    nf = np.maximum(n, 1).astype(np.float32)
    large = max_exact + (np.log(nf / max_exact) / math.log(ATT_WINDOW / max_exact)
                         * (REL_BUCKETS - max_exact)).astype(np.int32)
    large = np.minimum(large, REL_BUCKETS - 1)
    return np.where(n < max_exact, n, large).astype(np.int32)


def rel_bias_bwd(dbias, buckets):
    nh = dbias.shape[0]

    def body(db_ref, bk_ref, o_ref):
        bk = bk_ref[...]
        lane = _iota((1, 128), 1)
        row = _iota((REL_BUCKETS, 128), 0)
        acc = jnp.zeros((REL_BUCKETS, 128), F32)
        for h in range(nh):
            dbh = db_ref[h]
            for b in range(REL_BUCKETS):
                v = jnp.sum(jnp.where(bk == b, dbh, 0.0))
                acc = acc + jnp.where((row == b) & (lane == h), v, 0.0)
        o_ref[...] = acc

    return pl.pallas_call(
        body, name="rel_bias_bwd",
        out_shape=jax.ShapeDtypeStruct((REL_BUCKETS, 128), F32),
        compiler_params=_params(),
    )(dbias, buckets)


def loss_head(y, target):
    t, d = y.shape
    tm = _tile(t, 512)

    def body(y_ref, t_ref, l_ref, dy_ref):
        @pl.when(pl.program_id(0) == 0)
        def _():
            l_ref[...] = jnp.zeros_like(l_ref)

        e = y_ref[...] - t_ref[...]
        l_ref[...] += 0.5 * jnp.sum(jnp.mean(e * e, axis=-1, keepdims=True), axis=0, keepdims=True)
        dy_ref[...] = e * (1.0 / d)

    return pl.pallas_call(
        body, name="loss_head",
        grid=(t // tm,),
        in_specs=[pl.BlockSpec((tm, d), lambda i: (i, 0)), pl.BlockSpec((tm, d), lambda i: (i, 0))],
        out_specs=[pl.BlockSpec((1, 1), lambda i: (0, 0)), pl.BlockSpec((tm, d), lambda i: (i, 0))],
        out_shape=[jax.ShapeDtypeStruct((1, 1), F32), jax.ShapeDtypeStruct((t, d), F32)],
        compiler_params=_params("arbitrary"),
    )(y, target)


def adamw(w, g, m, v):
    r, c = w.shape
    tr = r if r <= 512 else _tile(r, 256)

    def body(w_ref, g_ref, m_ref, v_ref, d_ref, nm_ref, nv_ref):
        gg = g_ref[...]
        nm = ADAM_B1 * m_ref[...] + (1.0 - ADAM_B1) * gg
        nv = ADAM_B2 * v_ref[...] + (1.0 - ADAM_B2) * (gg * gg)
        m_hat = nm / (1.0 - ADAM_B1 ** ADAM_STEP)
        v_hat = nv / (1.0 - ADAM_B2 ** ADAM_STEP)
        d_ref[...] = -ADAM_LR * (m_hat / (jnp.sqrt(v_hat) + ADAM_EPS) + ADAM_WD * w_ref[...])
        nm_ref[...] = nm
        nv_ref[...] = nv

    spec = pl.BlockSpec((tr, c), lambda i: (i, 0))
    shp = jax.ShapeDtypeStruct((r, c), F32)
    return pl.pallas_call(
        body, name="adamw",
        grid=(r // tr,),
        in_specs=[spec] * 4, out_specs=[spec] * 3, out_shape=[shp] * 3,
        compiler_params=_params("arbitrary"),
    )(w, g, m, v)


def _my_pos():
    return lax.axis_index("x"), lax.axis_index("y"), lax.axis_index("c")


def _other_chips(x, y):
    return [(1 - x, y), (x, 1 - y), (1 - x, 1 - y)]


def _chip_id(x, y):
    return 2 * x + y


def gather_weights(bufs):
    n = len(bufs)

    def body(*refs):
        outs = refs[n:2 * n]
        send_sems, recv_sems = refs[2 * n:]
        x, y, c = _my_pos()
        sibling = (x, y, 1 - c)
        chips = _other_chips(x, y)

        def copy(p, k, chip, half, to):
            blk = outs[p].at[_chip_id(*chip), half]
            return pltpu.make_async_remote_copy(
                src_ref=blk, dst_ref=blk, send_sem=send_sems.at[p, k], recv_sem=recv_sems.at[p, k],
                device_id=to, device_id_type=MESH)

        first = [[copy(p, j, (x, y), c, (*chip, c)) for j, chip in enumerate(chips)] for p in range(n)]
        for p in range(n):
            for cp in first[p]:
                cp.start()
        passed = [[copy(p, 3 + j, chip, c, sibling) for j, chip in enumerate(chips)] for p in range(n)]
        for p in range(n):
            for j, chip in enumerate(chips):
                copy(p, j, chip, c, (x, y, c)).wait_recv()
                passed[p][j].start()
        for p in range(n):
            for j, chip in enumerate(chips):
                copy(p, 3 + j, chip, 1 - c, (x, y, c)).wait_recv()
        for p in range(n):
            for cp in first[p] + passed[p]:
                cp.wait_send()

    any_spec = pl.BlockSpec(memory_space=pl.ANY)
    return pl.pallas_call(
        body, name="gather_weights",
        in_specs=[any_spec] * n, out_specs=[any_spec] * n,
        out_shape=[jax.ShapeDtypeStruct(b.shape, b.dtype) for b in bufs],
        scratch_shapes=[pltpu.SemaphoreType.DMA((n, 6)), pltpu.SemaphoreType.DMA((n, 6))],
        input_output_aliases={p: p for p in range(n)},
    )(*bufs)


HBM_SPEC = pl.BlockSpec(memory_space=pltpu.HBM)
SEM_SPEC = pl.BlockSpec(memory_space=pltpu.SEMAPHORE)
DATAFLOW = pltpu.SideEffectType.DATAFLOW_SIDE_EFFECTING


def _in_hbm(a):
    return pltpu.with_memory_space_constraint(a, pltpu.HBM)


def _ici_gather_copy(buf, p, j, chip, c, to, send_sems, recv_sems):
    blk = buf.at[_chip_id(*chip), c]
    return pltpu.make_async_remote_copy(
        src_ref=blk, dst_ref=blk, send_sem=send_sems.at[3 * p + j], recv_sem=recv_sems.at[3 * p + j],
        device_id=to, device_id_type=MESH)


def gather_start(bufs, tag):
    n = len(bufs)

    def body(*refs):
        ins = refs[:n]
        send_sems, recv_sems = refs[n], refs[n + 1]
        token = refs[2 * n + 2]
        x, y, c = _my_pos()
        for p in range(n):
            for j, chip in enumerate(_other_chips(x, y)):
                _ici_gather_copy(ins[p], p, j, (x, y), c, (*chip, c), send_sems, recv_sems).start()
        token[...] = jnp.zeros_like(token)

    out = pl.pallas_call(
        body, name="gather_start_" + tag,
        in_specs=[HBM_SPEC] * n,
        out_specs=(SEM_SPEC, SEM_SPEC, *([HBM_SPEC] * n), pl.BlockSpec(memory_space=pltpu.VMEM)),
        out_shape=(pltpu.SemaphoreType.DMA((3 * n,)), pltpu.SemaphoreType.DMA((3 * n,)),
                   *[pltpu.HBM(b.shape, b.dtype) for b in bufs], jax.ShapeDtypeStruct((8, 128), F32)),
        input_output_aliases={p: 2 + p for p in range(n)},
        compiler_params=pltpu.CompilerParams(has_side_effects=DATAFLOW),
    )(*[_in_hbm(b) for b in bufs])
    return out[0], out[1], list(out[2:2 + n]), out[2 + n]


def gather_wait(send_sems, recv_sems, bufs, after, tag):
    n = len(bufs)

    def body(*refs):
        ins = refs[:n]
        send_sems, recv_sems = refs[n], refs[n + 1]
        x, y, c = _my_pos()
        for p in range(n):
            for j, chip in enumerate(_other_chips(x, y)):
                _ici_gather_copy(ins[p], p, j, (x, y), c, (*chip, c), send_sems, recv_sems).wait_send()
                _ici_gather_copy(ins[p], p, j, chip, c, (x, y, c), send_sems, recv_sems).wait_recv()

    out = pl.pallas_call(
        body, name="gather_wait_" + tag,
        in_specs=[HBM_SPEC] * n + [SEM_SPEC, SEM_SPEC, pl.BlockSpec(memory_space=pl.ANY)],
        out_specs=[HBM_SPEC] * n,
        out_shape=[pltpu.HBM(b.shape, b.dtype) for b in bufs],
        input_output_aliases={p: p for p in range(n)},
        compiler_params=pltpu.CompilerParams(has_side_effects=DATAFLOW),
    )(*bufs, send_sems, recv_sems, after)
    return list(out)


def forward_to_sibling(bufs):
    n = len(bufs)

    def body(*refs):
        outs = refs[n:2 * n]
        send_sems, recv_sems = refs[2 * n:]
        x, y, c = _my_pos()
        chips = _other_chips(x, y)
        sent = []
        for p in range(n):
            for j, chip in enumerate(chips):
                cp = _ici_gather_copy(outs[p], p, j, chip, c, (x, y, 1 - c), send_sems, recv_sems)
                cp.start()
                sent.append(cp)
        for p in range(n):
            for j, chip in enumerate(chips):
                _ici_gather_copy(outs[p], p, j, chip, 1 - c, (x, y, c), send_sems, recv_sems).wait_recv()
        for cp in sent:
            cp.wait_send()

    any_spec = pl.BlockSpec(memory_space=pl.ANY)
    return pl.pallas_call(
        body, name="forward_to_sibling",
        in_specs=[any_spec] * n, out_specs=[any_spec] * n,
        out_shape=[jax.ShapeDtypeStruct(b.shape, b.dtype) for b in bufs],
        scratch_shapes=[pltpu.SemaphoreType.DMA((3 * n,)), pltpu.SemaphoreType.DMA((3 * n,))],
        input_output_aliases={p: p for p in range(n)},
    )(*bufs)


def allreduce_small(v):
    r, c = v.shape

    def body(v_ref, o_ref, buf, send_sems, recv_sems):
        x, y, cc = _my_pos()
        me = 4 * x + 2 * y + cc
        buf[me] = v_ref[...]
        copies = []
        for k in range(1, 8):
            dx, dy, dc = (k >> 2) & 1, (k >> 1) & 1, k & 1
            peer = (x ^ dx, y ^ dy, cc ^ dc)
            cp = pltpu.make_async_remote_copy(
                src_ref=v_ref, dst_ref=buf.at[me], send_sem=send_sems.at[k - 1], recv_sem=recv_sems.at[k - 1],
                device_id=peer, device_id_type=MESH)
            cp.start()
            copies.append(cp)
        for cp in copies:
            cp.wait_recv()
        for cp in copies:
            cp.wait_send()
        acc = buf[0]
        for k in range(1, 8):
            acc = acc + buf[k]
        o_ref[...] = acc

    vm = pl.BlockSpec(memory_space=pltpu.VMEM)
    return pl.pallas_call(
        body, name="allreduce_small",
        in_specs=[vm], out_specs=vm,
        out_shape=jax.ShapeDtypeStruct((r, c), F32),
        scratch_shapes=[pltpu.VMEM((8, r, c), F32), pltpu.SemaphoreType.DMA((7,)), pltpu.SemaphoreType.DMA((7,))],
    )(v)


def exchange_sibling_halves(grads):
    n = len(grads)

    def body(*refs):
        ins, outs = refs[:n], refs[n:2 * n]
        send_sems, recv_sems = refs[2 * n:]
        x, y, c = _my_pos()
        copies = []
        for p in range(n):
            cp = pltpu.make_async_remote_copy(
                src_ref=ins[p].at[:, 1 - c], dst_ref=outs[p], send_sem=send_sems.at[p], recv_sem=recv_sems.at[p],
                device_id=(x, y, 1 - c), device_id_type=MESH)
            cp.start()
            copies.append(cp)
        for cp in copies:
            cp.wait_recv()
        for cp in copies:
            cp.wait_send()

    any_spec = pl.BlockSpec(memory_space=pl.ANY)
    return pl.pallas_call(
        body, name="exchange_sibling_halves",
        in_specs=[any_spec] * n, out_specs=[any_spec] * n,
        out_shape=[jax.ShapeDtypeStruct((g.shape[0],) + g.shape[2:], g.dtype) for g in grads],
        scratch_shapes=[pltpu.SemaphoreType.DMA((n,)), pltpu.SemaphoreType.DMA((n,))],
    )(*grads)


def _chip_partial_copy(part, land, p, j, chip, c, send_sems, recv_sems):
    return pltpu.make_async_remote_copy(
        src_ref=part.at[_chip_id(*chip)], dst_ref=land.at[j], send_sem=send_sems.at[3 * p + j],
        recv_sem=recv_sems.at[3 * p + j], device_id=(*chip, c), device_id_type=MESH)


def chip_partials_start(parts, tag):
    n = len(parts)
    lands = [lax.empty((N_CHIPS - 1,) + s.shape[1:], s.dtype) for s in parts]

    def body(*refs):
        ins, land = refs[:n], refs[n:2 * n]
        send_sems, recv_sems = refs[2 * n], refs[2 * n + 1]
        token = refs[4 * n + 2]
        x, y, c = _my_pos()
        for p in range(n):
            for j, chip in enumerate(_other_chips(x, y)):
                _chip_partial_copy(ins[p], land[p], p, j, chip, c, send_sems, recv_sems).start()
        token[...] = jnp.zeros_like(token)

    out = pl.pallas_call(
        body, name="chip_partials_start_" + tag,
        in_specs=[HBM_SPEC] * (2 * n),
        out_specs=(SEM_SPEC, SEM_SPEC, *([HBM_SPEC] * (2 * n)), pl.BlockSpec(memory_space=pltpu.VMEM)),
        out_shape=(pltpu.SemaphoreType.DMA((3 * n,)), pltpu.SemaphoreType.DMA((3 * n,)),
                   *[pltpu.HBM(a.shape, a.dtype) for a in parts + lands], jax.ShapeDtypeStruct((8, 128), F32)),
        input_output_aliases={i: 2 + i for i in range(2 * n)},
        compiler_params=pltpu.CompilerParams(has_side_effects=DATAFLOW),
    )(*[_in_hbm(a) for a in parts + lands])
    return out[0], out[1], list(out[2:2 + n]), list(out[2 + n:2 + 2 * n]), out[2 + 2 * n]


def chip_partials_wait(send_sems, recv_sems, parts, lands, after, tag):
    n = len(parts)

    def body(*refs):
        ins, land = refs[:n], refs[n:2 * n]
        send_sems, recv_sems = refs[2 * n], refs[2 * n + 1]
        x, y, c = _my_pos()
        for p in range(n):
            for j, chip in enumerate(_other_chips(x, y)):
                cp = _chip_partial_copy(ins[p], land[p], p, j, chip, c, send_sems, recv_sems)
                cp.wait_send()
                cp.wait_recv()

    out = pl.pallas_call(
        body, name="chip_partials_wait_" + tag,
        in_specs=[HBM_SPEC] * (2 * n) + [SEM_SPEC, SEM_SPEC, pl.BlockSpec(memory_space=pl.ANY)],
        out_specs=[HBM_SPEC] * (2 * n),
        out_shape=[pltpu.HBM(a.shape, a.dtype) for a in parts + lands],
        input_output_aliases={i: i for i in range(2 * n)},
        compiler_params=pltpu.CompilerParams(has_side_effects=DATAFLOW),
    )(*parts, *lands, send_sems, recv_sems, after)
    return list(out[n:])


def share_with_sibling(bufs):
    n = len(bufs)

    def body(*refs):
        outs = refs[n:2 * n]
        send_sems, recv_sems = refs[2 * n:]
        x, y, c = _my_pos()
        copies = []
        for p in range(n):
            cp = pltpu.make_async_remote_copy(
                src_ref=outs[p].at[c], dst_ref=outs[p].at[c], send_sem=send_sems.at[p], recv_sem=recv_sems.at[p],
                device_id=(x, y, 1 - c), device_id_type=MESH)
            cp.start()
            copies.append(cp)
        for p in range(n):
            pltpu.make_async_remote_copy(
                src_ref=outs[p].at[1 - c], dst_ref=outs[p].at[1 - c], send_sem=send_sems.at[p],
                recv_sem=recv_sems.at[p], device_id=(x, y, 1 - c), device_id_type=MESH).wait_recv()
        for cp in copies:
            cp.wait_send()

    any_spec = pl.BlockSpec(memory_space=pl.ANY)
    return pl.pallas_call(
        body, name="share_with_sibling",
        in_specs=[any_spec] * n, out_specs=[any_spec] * n,
        out_shape=[jax.ShapeDtypeStruct(b.shape, b.dtype) for b in bufs],
        scratch_shapes=[pltpu.SemaphoreType.DMA((n,)), pltpu.SemaphoreType.DMA((n,))],
        input_output_aliases={p: p for p in range(n)},
    )(*bufs)


def add_sibling(g, recv, half):
    _, _, r, c = g.shape
    tr = _tile(r, 256) if r % 256 == 0 else r

    def body(half_ref, g_ref, r_ref, o32_ref, o16_ref):
        s = g_ref[...] + r_ref[...]
        o32_ref[...] = s
        o16_ref[...] = _b(s)

    return pl.pallas_call(
        body, name="add_sibling",
        grid_spec=pltpu.PrefetchScalarGridSpec(
            num_scalar_prefetch=1, grid=(N_CHIPS, r // tr),
            in_specs=[pl.BlockSpec((None, None, tr, c), lambda k, i, hf: (k, hf[0], i, 0)),
                      pl.BlockSpec((None, tr, c), lambda k, i, hf: (k, i, 0))],
            out_specs=[pl.BlockSpec((None, tr, c), lambda k, i, hf: (k, i, 0)),
                       pl.BlockSpec((None, tr, c), lambda k, i, hf: (k, i, 0))]),
        out_shape=[jax.ShapeDtypeStruct((N_CHIPS, r, c), F32), jax.ShapeDtypeStruct((N_CHIPS, r, c), BF16)],
        compiler_params=_params("arbitrary", "arbitrary"),
    )(half, g, recv)


def add_chip_partials(p32, recv, pos):
    _, r, c = p32.shape
    tr = _tile(r, 256) if r % 256 == 0 else r

    def body(pos_ref, p_ref, r_ref, o_ref):
        acc = p_ref[...]
        for j in range(N_CHIPS - 1):
            acc = acc + r_ref[j].astype(F32)
        o_ref[...] = acc

    return pl.pallas_call(
        body, name="add_chip_partials",
        grid_spec=pltpu.PrefetchScalarGridSpec(
            num_scalar_prefetch=1, grid=(r // tr,),
            in_specs=[pl.BlockSpec((None, tr, c), lambda i, ps: (ps[0], i, 0)),
                      pl.BlockSpec((N_CHIPS - 1, tr, c), lambda i, ps: (0, i, 0))],
            out_specs=pl.BlockSpec((None, tr, c), lambda i, ps: (ps[1], i, 0))),
        out_shape=jax.ShapeDtypeStruct((2, r, c), F32),
        compiler_params=_params("arbitrary"),
    )(pos, p32, recv)


def cast_into_gather(w, pos, row0=0, nrows=None):
    c = w.shape[1]
    nrows = w.shape[0] if nrows is None else nrows
    r = nrows // 2
    common = math.gcd(r, row0) if row0 else r
    tr = max(w for w in range(16, min(common, 512) + 1, 16) if common % w == 0)
    nt = r // tr

    def body(pos_ref, w_ref, o_ref):
        o_ref[...] = _b(w_ref[...])

    return pl.pallas_call(
        body, name="cast_into_gather",
        grid_spec=pltpu.PrefetchScalarGridSpec(
            num_scalar_prefetch=1, grid=(2, nt),
            in_specs=[pl.BlockSpec((tr, c), lambda hf, i, ps: (row0 // tr + hf * nt + i, 0))],
            out_specs=pl.BlockSpec((None, None, tr, c), lambda hf, i, ps: (ps[0], hf, i, 0))),
        out_shape=jax.ShapeDtypeStruct((N_CHIPS, 2, r, c), BF16),
        compiler_params=_params("arbitrary", "arbitrary"),
    )(pos, w)


def build_bias(rel, buckets):
    nb, nh = rel.shape

    def body(rel_ref, bk_ref, o_ref):
        bk = bk_ref[...]
        for h in range(nh):
            acc = jnp.zeros(bk.shape, F32)
            for b in range(nb):
                acc = jnp.where(bk == b, rel_ref[b, h], acc)
            o_ref[h] = acc

    return pl.pallas_call(
        body, name="build_bias",
        in_specs=[pl.BlockSpec(memory_space=pltpu.SMEM), pl.BlockSpec(memory_space=pltpu.VMEM)],
        out_specs=pl.BlockSpec(memory_space=pltpu.VMEM),
        out_shape=jax.ShapeDtypeStruct((nh,) + buckets.shape, F32),
        compiler_params=_params(),
    )(rel, buckets)


SMALL_ROWS = 256


def kernel(x, ffn_norm, ffn_w1, ffn_w3, ffn_w2, ssm_norm, ssm_w_in, ssm_conv_w, ssm_conv_b, ssm_dt_bias, ssm_a_log, ssm_d, ssm_gate_norm, ssm_w_out, kv_norm, w_kv, k_norm, attn_norm, w_q, q_norm, sinks, w_o, rel_bias, loss_target, m_ffn_norm, m_ffn_w1, m_ffn_w3, m_ffn_w2, m_ssm_norm, m_ssm_w_in, m_ssm_conv_w, m_ssm_conv_b, m_ssm_dt_bias, m_ssm_a_log, m_ssm_d, m_ssm_gate_norm, m_ssm_w_out, m_kv_norm, m_w_kv, m_k_norm, m_attn_norm, m_w_q, m_q_norm, m_sinks, m_w_o, m_rel_bias, v_ffn_norm, v_ffn_w1, v_ffn_w3, v_ffn_w2, v_ssm_norm, v_ssm_w_in, v_ssm_conv_w, v_ssm_conv_b, v_ssm_dt_bias, v_ssm_a_log, v_ssm_d, v_ssm_gate_norm, v_ssm_w_out, v_kv_norm, v_w_kv, v_k_norm, v_attn_norm, v_w_q, v_q_norm, v_sinks, v_w_o, v_rel_bias):
    weights = dict(ffn_norm=ffn_norm, ffn_w1=ffn_w1, ffn_w3=ffn_w3, ffn_w2=ffn_w2, ssm_norm=ssm_norm,
                   ssm_w_in=ssm_w_in, ssm_conv_w=ssm_conv_w, ssm_conv_b=ssm_conv_b, ssm_dt_bias=ssm_dt_bias,
                   ssm_a_log=ssm_a_log, ssm_d=ssm_d, ssm_gate_norm=ssm_gate_norm, ssm_w_out=ssm_w_out,
                   kv_norm=kv_norm, w_kv=w_kv, k_norm=k_norm, attn_norm=attn_norm, w_q=w_q, q_norm=q_norm,
                   sinks=sinks, w_o=w_o, rel_bias=rel_bias)
    m_in = dict(ffn_norm=m_ffn_norm, ffn_w1=m_ffn_w1, ffn_w3=m_ffn_w3, ffn_w2=m_ffn_w2, ssm_norm=m_ssm_norm,
                ssm_w_in=m_ssm_w_in, ssm_conv_w=m_ssm_conv_w, ssm_conv_b=m_ssm_conv_b, ssm_dt_bias=m_ssm_dt_bias,
                ssm_a_log=m_ssm_a_log, ssm_d=m_ssm_d, ssm_gate_norm=m_ssm_gate_norm, ssm_w_out=m_ssm_w_out,
                kv_norm=m_kv_norm, w_kv=m_w_kv, k_norm=m_k_norm, attn_norm=m_attn_norm, w_q=m_w_q, q_norm=m_q_norm,
                sinks=m_sinks, w_o=m_w_o, rel_bias=m_rel_bias)
    v_in = dict(ffn_norm=v_ffn_norm, ffn_w1=v_ffn_w1, ffn_w3=v_ffn_w3, ffn_w2=v_ffn_w2, ssm_norm=v_ssm_norm,
                ssm_w_in=v_ssm_w_in, ssm_conv_w=v_ssm_conv_w, ssm_conv_b=v_ssm_conv_b, ssm_dt_bias=v_ssm_dt_bias,
                ssm_a_log=v_ssm_a_log, ssm_d=v_ssm_d, ssm_gate_norm=v_ssm_gate_norm, ssm_w_out=v_ssm_w_out,
                kv_norm=v_kv_norm, w_kv=v_w_kv, k_norm=v_k_norm, attn_norm=v_attn_norm, w_q=v_w_q, q_norm=v_q_norm,
                sinks=v_sinks, w_o=v_w_o, rel_bias=v_rel_bias)
    return _step(x[0], loss_target[0], weights, m_in, v_in)


BIG = ("ffn_w1", "ffn_w3", "ffn_w2", "ssm_w_in", "ssm_w_out", "w_kv", "w_q", "w_o")
SMALL = (("ffn_norm", True), ("ssm_norm", True), ("ssm_conv_w", True), ("ssm_conv_b", True),
         ("ssm_gate_norm", True), ("ssm_dt_bias", False), ("ssm_a_log", False), ("ssm_d", False),
         ("kv_norm", False), ("k_norm", False), ("attn_norm", False), ("q_norm", False), ("sinks", False),
         ("rel_bias", False))


FFN_W = BIG[:3]


def _after(token, x):
    return lax.optimization_barrier((token, x))[1]


def _small_layout(weights):
    off, table = 0, {}
    for name, sharded in SMALL:
        shape = weights[name].shape
        full = shape[:-1] + (shape[-1] * N_CHIPS,) if sharded else shape
        n = int(np.prod(full))
        table[name] = (off, full, sharded)
        off += n
    assert off <= SMALL_ROWS * 128
    return table


def _place_small(values, table, chip, scale_mask):
    flat = jnp.zeros((SMALL_ROWS * 128,), F32)
    for name, (off, full, sharded) in table.items():
        if not sharded:
            continue
        v = values[name].astype(F32)
        lead = int(np.prod(full[:-1]))
        w = v.shape[-1]
        blk = jnp.zeros((lead, full[-1]), F32)
        blk = lax.dynamic_update_slice(blk, v.reshape(lead, w) * scale_mask, (0, chip * w))
        flat = lax.dynamic_update_slice(flat, blk.reshape(-1), (off,))
    return flat.reshape(SMALL_ROWS, 128)


def _take_small(mat, table, name):
    off, full, _ = table[name]
    n = int(np.prod(full))
    return mat.reshape(-1)[off:off + n].reshape(full)


def _step(x, target, weights, m_in, v_in):
    t, d = x.shape
    xi, yi, ci = lax.axis_index("x"), lax.axis_index("y"), lax.axis_index("c")
    chip = 2 * xi + yi
    pos_arr = jnp.stack([chip, ci]).astype(jnp.int32)
    half_arr = jnp.reshape(ci, (1,)).astype(jnp.int32)

    fs = weights["ffn_w1"].shape[-1]
    ffn_rows = {"ffn_w1": d, "ffn_w3": d, "ffn_w2": fs}
    w2d = {n: weights[n].reshape(-1, weights[n].shape[-1]) for n in BIG}
    first = gather_weights([cast_into_gather(w2d[n], pos_arr, 0, ffn_rows[n]) for n in FFN_W])
    mamba_w = ("ssm_w_in", "ssm_w_out")
    late_w = ("w_kv", "w_q", "w_o")
    ms, mr, mbufs, tok_m = gather_start([cast_into_gather(w2d[n], pos_arr) for n in mamba_w], "mamba")
    ls, lr, lbufs, tok_l = gather_start(
        [cast_into_gather(w2d[n], pos_arr, ffn_rows[n], 3 * ffn_rows[n]) for n in FFN_W]
        + [cast_into_gather(w2d[n], pos_arr) for n in late_w], "late")
    x = _after(tok_l, _after(tok_m, x))
    table = _small_layout(weights)
    south = (ci == 0).astype(F32)
    small = allreduce_small(_place_small(weights, table, chip, south))
    sp = {n: _take_small(small, table, n) if sh else weights[n] for n, sh in SMALL}

    ffn_first = [first[0].reshape(N_CHIPS, 1, d, fs), first[1].reshape(N_CHIPS, 1, d, fs),
                 first[2].reshape(N_CHIPS, 1, fs, d)]
    ffn_g = sp["ffn_norm"]
    h0 = x
    h1, a00, b00 = ffn_fwd(h0, ffn_g[0, 0].reshape(1, d), *ffn_first, 0)
    gathered = dict(zip(mamba_w, forward_to_sibling(gather_wait(ms, mr, mbufs, h1, "mamba"))))
    n_in = weights["ssm_w_in"].shape[-1] * N_CHIPS
    di = weights["ssm_w_out"].shape[1] * N_CHIPS
    nheads = di // SSM_HEAD_DIM
    conv_dim = n_in - di - nheads
    w_in_full = jnp.moveaxis(gathered["ssm_w_in"].reshape(N_CHIPS, d, n_in // N_CHIPS), 0, 1).reshape(d, n_in)
    hpg = nheads // SSM_GROUPS

    def spread_heads(v):
        lead = v.shape[:-1]
        v = v.reshape(lead + (SSM_GROUPS, hpg))
        v = jnp.pad(v, [(0, 0)] * len(lead) + [(0, 0), (0, 128 - hpg)])
        return v.reshape(lead + (SSM_GROUPS * 128,))

    def gather_heads(v):
        lead = v.shape[:-1]
        return v.reshape(lead + (SSM_GROUPS, 128))[..., :hpg].reshape(lead + (nheads,))

    dt_col0 = di + conv_dim
    n_zx = dt_col0 + SSM_GROUPS * 128
    w_in = jnp.concatenate([w_in_full[:, :dt_col0], spread_heads(w_in_full[:, dt_col0:])], axis=1)
    w_out = gathered["ssm_w_out"].reshape(di, d)
    nkv = weights["w_kv"].shape[1] // (2 * ATT_HEAD_DIM)
    assert nkv == 2
    nh = weights["w_q"].shape[-1] // ATT_HEAD_DIM

    ssm_g = sp["ssm_norm"].reshape(1, d)
    cw = jnp.pad(sp["ssm_conv_w"].reshape(SSM_CONV, conv_dim), [(0, 8 - SSM_CONV), (0, 0)])
    cb = sp["ssm_conv_b"].reshape(1, conv_dim)
    gate_g = sp["ssm_gate_norm"].reshape(1, di)
    dt_bias = spread_heads(sp["ssm_dt_bias"].reshape(1, nheads))
    a_log = spread_heads(sp["ssm_a_log"].reshape(1, nheads))
    d_skip = spread_heads(sp["ssm_d"].reshape(1, nheads))
    kv_g = sp["kv_norm"].reshape(1, d)
    k_g = jnp.tile(sp["k_norm"].reshape(1, ATT_HEAD_DIM), (1, 2))
    attn_g = sp["attn_norm"].reshape(1, d)
    q_g = jnp.tile(sp["q_norm"].reshape(1, ATT_HEAD_DIM), (1, 2))
    sink_row = jnp.pad(sp["sinks"].reshape(1, nh), [(0, 0), (0, 128 - nh)])
    buckets = jnp.asarray(_t5_buckets())
    biasm = build_bias(sp["rel_bias"], buckets).reshape(nh * ATT_WINDOW, 2 * ATT_WINDOW)

    zx = norm_mm(h1, ssm_g, w_in)
    xc = conv_fwd(zx, cw, cb, di)
    y_ssd, states = ssd_fwd(xc, zx, dt_bias, a_log, d_skip, dt_col0)
    h2 = gate_out_fwd(h1, y_ssd, zx, gate_g, w_out)

    late = forward_to_sibling(gather_wait(ls, lr, lbufs, h2, "late"))
    ffn_rest = [late[0].reshape(N_CHIPS, 3, d, fs), late[1].reshape(N_CHIPS, 3, d, fs),
                late[2].reshape(N_CHIPS, 3, fs, d)]
    gathered.update(zip(late_w, late[3:]))
    wkv_heads = gathered["w_kv"].reshape(d, 2 * nkv, 1, ATT_HEAD_DIM)
    w_kvd = jnp.broadcast_to(wkv_heads, (d, 2 * nkv, 2, ATT_HEAD_DIM)).reshape(d, 4 * nkv * ATT_HEAD_DIM)
    wq = gathered["w_q"].reshape(d, -1)
    wo = gathered["w_o"].reshape(-1, d)

    def ffn_w(layer, idx):
        blk = 2 * layer + idx
        return (*ffn_first, 0) if blk == 0 else (*ffn_rest, blk - 1)

    h3, a01, b01 = ffn_fwd(h2, ffn_g[0, 1].reshape(1, d), *ffn_w(0, 1))
    kvd = norm_mm(h3, kv_g, w_kvd)
    h4, a10, b10 = ffn_fwd(h3, ffn_g[1, 0].reshape(1, d), *ffn_w(1, 0))
    qp = norm_mm(h4, attn_g, wq)
    h5 = attn_fwd(h4, qp, kvd, biasm, sink_row, q_g, k_g, wo)
    h6, a11, b11 = ffn_fwd(h5, ffn_g[1, 1].reshape(1, d), *ffn_w(1, 1))
    loss_part, d6 = loss_head(h6, target)
    loss = lax.psum(loss_part[0, 0], ("x", "y", "c"))

    gfn = [[None, None], [None, None]]

    pending = []

    def reduce_start(pieces, tag):
        views = [g.reshape(N_CHIPS, 2, g.shape[1] // 2, g.shape[2]) for _, g in pieces]
        recv1 = exchange_sibling_halves(views)
        p32, p16 = zip(*[add_sibling(g, r, half_arr) for g, r in zip(views, recv1)])
        ss, rs, parts, lands, token = chip_partials_start(list(p16), tag)
        pending.append(([k for k, _ in pieces], p32, ss, rs, parts, lands, tag))
        return token

    def ffn_back(h_in, dy, a_s, b_s, layer, idx):
        dh, u, da, db, s, dg = ffn_bwd(h_in, dy, ffn_g[layer, idx].reshape(1, d), a_s, b_s, *ffn_w(layer, idx))
        gfn[layer][idx] = dg
        return dh, [(("ffn_w1", layer, idx), wgrad_grouped_b(u, da)), (("ffn_w3", layer, idx), wgrad_grouped_b(u, db)),
                    (("ffn_w2", layer, idx), wgrad_grouped_a(s, dy, 0.5))]

    d5, pieces = ffn_back(h5, d6, a11, b11, 1, 1)
    d5 = _after(reduce_start(pieces, "ffn11"), d5)
    dqp, dkvd, o16, dbiasm, dsinks, dqg, dkg = attn_bwd(d5, qp, kvd, biasm, sink_row, q_g, k_g, wo)
    g_wo = wgrad(o16, d5)
    d4, u_q, g_attn_norm = norm_mm_bwd(h4, attn_g, wq, dqp, d5)
    g_wq = wgrad(u_q, dqp)
    d3a, pieces = ffn_back(h3, d4, a10, b10, 1, 0)
    pieces += [(("w_o",), g_wo.reshape(N_CHIPS, -1, d)), (("w_q",), g_wq.reshape(N_CHIPS, d // N_CHIPS, -1))]
    d3a = _after(reduce_start(pieces, "ffn10"), d3a)
    d3, u_kv, g_kv_norm = norm_mm_bwd(h3, kv_g, w_kvd, dkvd, d3a, 0.5)
    g_wkvd = wgrad(u_kv, dkvd)
    g_wkv = g_wkvd.reshape(d, 2 * nkv, 2, ATT_HEAD_DIM)[:, :, 0, :].reshape(d, 2 * nkv * ATT_HEAD_DIM)
    d2, pieces = ffn_back(h2, d3, a01, b01, 0, 1)
    pieces += [(("w_kv",), g_wkv.reshape(N_CHIPS, d // N_CHIPS, -1))]
    d2 = _after(reduce_start(pieces, "ffn01"), d2)
    dzx, dy_ssd, yn16, g_gate = gate_out_bwd(d2, y_ssd, zx, gate_g, w_out, n_zx)
    g_wout = wgrad(yn16, d2)
    dzx, dxs, dbm, dcm, g_dtb, g_alog, g_dsk = ssd_bwd(dzx, dy_ssd, xc, zx, states, dt_bias, a_log, d_skip, dt_col0)
    dzx, g_cw, g_cb = conv_bwd(dzx, zx, dxs, dbm, dcm, cw, cb, di)
    d1, u_in, g_ssm_norm = norm_mm_bwd(h1, ssm_g, w_in, dzx, d2)
    g_win = wgrad(u_in, dzx)
    g_win_full = jnp.concatenate([g_win[:, :dt_col0], gather_heads(g_win[:, dt_col0:])], axis=1)
    pieces = [(("ssm_w_in",), jnp.moveaxis(g_win_full.reshape(d, N_CHIPS, n_in // N_CHIPS), 1, 0)),
              (("ssm_w_out",), g_wout.reshape(N_CHIPS, di // N_CHIPS, d))]
    d1 = _after(reduce_start(pieces, "mamba"), d1)
    grad_x, pieces = ffn_back(h0, d1, a00, b00, 0, 0)
    grad_x = _after(reduce_start(pieces, "ffn00"), grad_x)
    g_relb = rel_bias_bwd(dbiasm.reshape(nh, ATT_WINDOW, 2 * ATT_WINDOW), buckets)

    reduced = {}
    for keys, p32, ss, rs, parts, lands, tag in pending:
        lands = chip_partials_wait(ss, rs, parts, lands, grad_x, tag)
        for k, p, r in zip(keys, p32, lands):
            reduced[k] = add_chip_partials(p, r, pos_arr)
    keys = list(reduced)
    shared = dict(zip(keys, share_with_sibling([reduced[k] for k in keys])))
    grads = {}
    for n in FFN_W:
        blocks = [shared[(n, l, i)].reshape(1, ffn_rows[n], -1) for l in range(2) for i in range(2)]
        grads[n] = jnp.concatenate(blocks, axis=0).reshape(weights[n].shape)
    for n in BIG[3:]:
        grads[n] = shared[(n,)].reshape(weights[n].shape)

    small_grads = {
        "ffn_norm": jnp.stack([jnp.stack([gfn[l][i].reshape(d) for i in range(2)]) for l in range(2)]),
        "ssm_norm": g_ssm_norm.reshape(1, d),
        "ssm_conv_w": g_cw[:SSM_CONV].reshape(1, SSM_CONV, conv_dim),
        "ssm_conv_b": g_cb.reshape(1, conv_dim),
        "ssm_gate_norm": g_gate.reshape(1, di),
        "ssm_dt_bias": gather_heads(g_dtb.reshape(1, -1)), "ssm_a_log": gather_heads(g_alog.reshape(1, -1)),
        "ssm_d": gather_heads(g_dsk.reshape(1, -1)),
        "kv_norm": g_kv_norm.reshape(d), "k_norm": dkg[0, :ATT_HEAD_DIM], "attn_norm": g_attn_norm.reshape(1, d),
        "q_norm": dqg[:, :ATT_HEAD_DIM], "sinks": dsinks[:, :nh], "rel_bias": g_relb[:, :nh],
    }
    flat = jnp.zeros((SMALL_ROWS * 128,), F32)
    for name, (off, fshape, _) in table.items():
        flat = lax.dynamic_update_slice(flat, small_grads[name].astype(F32).reshape(-1), (off,))
    small_sum = allreduce_small(flat.reshape(SMALL_ROWS, 128))
    for name, (off, fshape, sharded) in table.items():
        g = _take_small(small_sum, table, name)
        if sharded:
            w = weights[name].shape[-1]
            lead = int(np.prod(fshape[:-1]))
            g = lax.dynamic_slice(g.reshape(lead, fshape[-1]), (0, chip * w), (lead, w)).reshape(weights[name].shape)
        grads[name] = g.reshape(weights[name].shape)

    names = list(weights)
    deltas, new_m, new_v = {}, {}, {}
    small_names = [n for n, _ in SMALL]
    for n in BIG:
        shp = weights[n].shape
        v2 = lambda a: a.reshape(-1, shp[-1])
        dl, nm, nv = adamw(v2(weights[n]), v2(grads[n]), v2(m_in[n]), v2(v_in[n]))
        deltas[n], new_m[n], new_v[n] = dl.reshape(shp), nm.reshape(shp), nv.reshape(shp)
    sizes = [int(np.prod(weights[n].shape)) for n in small_names]
    tot = sum(sizes)
    rows = -(-tot // 128)
    rows = -(-rows // 8) * 8

    def pack(dct):
        flat = jnp.concatenate([dct[n].reshape(-1) for n in small_names])
        return jnp.pad(flat, (0, rows * 128 - tot), constant_values=1.0).reshape(rows, 128)

    dl, nm, nv = adamw(pack(weights), pack(grads), pack(m_in), pack(v_in))
    off = 0
    for n, sz in zip(small_names, sizes):
        shp = weights[n].shape
        take = lambda a: a.reshape(-1)[off:off + sz].reshape(shp)
        deltas[n], new_m[n], new_v[n] = take(dl), take(nm), take(nv)
        off += sz

    return (loss, grad_x[None], *[grads[n] for n in names], *[deltas[n] for n in names],
            *[new_m[n] for n in names], *[new_v[n] for n in names])
```

```python
import functools
import math

import jax
import jax.numpy as jnp
import numpy as np
from jax import lax
from jax.experimental import pallas as pl
from jax.experimental.pallas import tpu as pltpu

F32 = jnp.float32
BF16 = jnp.bfloat16
EPS = 1e-6
MESH = pl.DeviceIdType.MESH

SSM_HEAD_DIM = 64
SSM_GROUPS = 4
SSM_STATE = 128
SSM_CONV = 4
SSM_CHUNK = 256
ATT_HEAD_DIM = 64
ATT_WINDOW = 128
REL_BUCKETS = 32
N_CHIPS = 4

ADAM_LR = 0.001
ADAM_B1 = 0.9
ADAM_B2 = 0.999
ADAM_EPS = 1e-08
ADAM_WD = 0.01
ADAM_STEP = 10

VMEM_LIMIT_BYTES = 56 * 1024 * 1024
NEG = -1e30


DEP_SPEC = pl.BlockSpec(memory_space=pl.ANY)


def _params(*sem):
    return pltpu.CompilerParams(dimension_semantics=sem if sem else None, vmem_limit_bytes=VMEM_LIMIT_BYTES)


def _dot(a, b):
    return jnp.dot(a, b, preferred_element_type=F32)


def _dot_nt(a, b):
    return lax.dot_general(a, b, (((1,), (1,)), ((), ())), preferred_element_type=F32)


def _dot_tn(a, b):
    return lax.dot_general(a, b, (((0,), (0,)), ((), ())), preferred_element_type=F32)


def _b(x):
    return x.astype(BF16)


@jax.custom_vjp
def _bmm(a, b):
    return _dot(_b(a), _b(b))


def _bmm_fwd(a, b):
    return _bmm(a, b), (a, b)


def _bmm_bwd(res, g):
    a, b = res
    g16 = _b(g)
    return _dot_nt(g16, _b(b)).astype(a.dtype), _dot_tn(_b(a), g16).astype(b.dtype)


_bmm.defvjp(_bmm_fwd, _bmm_bwd)


@jax.custom_vjp
def _bmm_nt(a, b):
    return _dot_nt(_b(a), _b(b))


def _bmm_nt_fwd(a, b):
    return _bmm_nt(a, b), (a, b)


def _bmm_nt_bwd(res, g):
    a, b = res
    g16 = _b(g)
    return _dot(g16, _b(b)).astype(a.dtype), _dot_tn(g16, _b(a)).astype(b.dtype)


_bmm_nt.defvjp(_bmm_nt_fwd, _bmm_nt_bwd)


@jax.custom_vjp
def _bmm_tn(a, b):
    return _dot_tn(_b(a), _b(b))


def _bmm_tn_fwd(a, b):
    return _bmm_tn(a, b), (a, b)


def _bmm_tn_bwd(res, g):
    a, b = res
    g16 = _b(g)
    return _dot_nt(_b(b), g16).astype(a.dtype), _dot(_b(a), g16).astype(b.dtype)


_bmm_tn.defvjp(_bmm_tn_fwd, _bmm_tn_bwd)


def _split3(x):
    hi = _b(x)
    r = x - hi.astype(F32)
    mid = _b(r)
    lo = _b(r - mid.astype(F32))
    return hi, mid, lo


def _x_left_raw(m, x):
    hi, mid, lo = _split3(x)
    return _dot(m, hi) + _dot(m, mid) + _dot(m, lo)


def _x_left_t_raw(m, x):
    hi, mid, lo = _split3(x)
    return _dot_tn(m, hi) + _dot_tn(m, mid) + _dot_tn(m, lo)


def _x_right_raw(x, m):
    hi, mid, lo = _split3(x)
    return _dot(hi, m) + _dot(mid, m) + _dot(lo, m)


def _x_right_t_raw(x, m):
    hi, mid, lo = _split3(x)
    return _dot_nt(hi, m) + _dot_nt(mid, m) + _dot_nt(lo, m)


@jax.custom_vjp
def _xleft(m, x):
    return _x_left_raw(m, x)


_xleft.defvjp(lambda m, x: (_x_left_raw(m, x), m),
              lambda m, g: (jnp.zeros_like(m), _x_left_t_raw(m, g)))


@jax.custom_vjp
def _xright(x, m):
    return _x_right_raw(x, m)


_xright.defvjp(lambda x, m: (_x_right_raw(x, m), m),
               lambda m, g: (_x_right_t_raw(g, m), jnp.zeros_like(m)))


def _sigmoid(x):
    return 1.0 / (1.0 + jnp.exp(-x))


def _silu(x):
    return x * _sigmoid(x)


def _softplus(x):
    return jnp.maximum(x, 0.0) + jnp.log(1.0 + jnp.exp(-jnp.abs(x)))


def _rms(x):
    return x * lax.rsqrt(jnp.mean(x * x, axis=-1, keepdims=True) + EPS)


def _iota(shape, dim):
    return lax.broadcasted_iota(jnp.int32, shape, dim)


def _blockdiag64(n):
    return jnp.where(_iota((n, n), 0) // 64 == _iota((n, n), 1) // 64, 1.0, 0.0).astype(BF16)


def _group64_rms(x, seg_sum):
    ms = seg_sum(x * x) * (1.0 / 64.0)
    return x * lax.rsqrt(ms + EPS)


def _fold64(x):
    ax = x.ndim - 1
    w = x.shape[ax]
    lo = (_iota(x.shape, ax) % 128) < 64
    return x + jnp.where(lo, pltpu.roll(x, w - 64, ax), pltpu.roll(x, 64, ax))


def _tile(n, want):
    t = min(n, want)
    assert n % t == 0, (n, t)
    return t


def _lane_tile(n, cap=1536):
    if n <= cap:
        return n
    return max(w for w in range(128, cap + 1, 128) if n % w == 0)


def ffn_fwd(h, g, w1, w3, w2, blk, dep):
    t, d = h.shape
    nk, fs = w1.shape[0], w1.shape[-1]
    tm = _tile(t, 512)

    def body(h_ref, g_ref, w1_ref, w3_ref, w2_ref, dep_ref, o_ref, a_ref, b_ref, u_scr, acc):
        k = pl.program_id(1)

        @pl.when(k == 0)
        def _():
            u_scr[...] = _b(_rms(h_ref[...]) * g_ref[...])
            acc[...] = jnp.zeros_like(acc)

        u = u_scr[...]
        a = _dot(u, w1_ref[...])
        b = _dot(u, w3_ref[...])
        a_ref[...] = _b(a)
        b_ref[...] = _b(b)
        acc[...] += _dot(_b(_silu(a) * b), w2_ref[...])

        @pl.when(k == nk - 1)
        def _():
            o_ref[...] = h_ref[...] + 0.5 * acc[...]

    wspec = lambda r, c: pl.BlockSpec((None, None, r, c), lambda i, k: (k, blk, 0, 0))
    return pl.pallas_call(
        body, name="ffn_fwd",
        grid=(t // tm, nk),
        in_specs=[pl.BlockSpec((tm, d), lambda i, k: (i, 0)), pl.BlockSpec((1, d), lambda i, k: (0, 0)),
                  wspec(d, fs), wspec(d, fs), wspec(fs, d), DEP_SPEC],
        out_specs=[pl.BlockSpec((tm, d), lambda i, k: (i, 0)),
                   pl.BlockSpec((None, tm, fs), lambda i, k: (k, i, 0)),
                   pl.BlockSpec((None, tm, fs), lambda i, k: (k, i, 0))],
        out_shape=[jax.ShapeDtypeStruct((t, d), F32), jax.ShapeDtypeStruct((nk, t, fs), BF16),
                   jax.ShapeDtypeStruct((nk, t, fs), BF16)],
        scratch_shapes=[pltpu.VMEM((tm, d), BF16), pltpu.VMEM((tm, d), F32)],
        compiler_params=_params("arbitrary", "arbitrary"),
    )(h, g, w1, w3, w2, dep)


def ffn_bwd(h, dy, g, a_s, b_s, w1, w3, w2, blk, dep):
    t, d = h.shape
    nk, fs = w1.shape[0], w1.shape[-1]
    tm = _tile(t, 512)

    def body(h_ref, dy_ref, g_ref, a_ref, b_ref, w1_ref, w3_ref, w2_ref, dep_ref,
             dh_ref, u_ref, da_ref, db_ref, s_ref, dg_ref, dyh_scr, du_acc):
        i, k = pl.program_id(0), pl.program_id(1)

        @pl.when(k == 0)
        def _():
            dyh_scr[...] = _b(0.5 * dy_ref[...])
            du_acc[...] = jnp.zeros_like(du_acc)

        @pl.when((k == 0) & (i == 0))
        def _():
            dg_ref[...] = jnp.zeros_like(dg_ref)

        ds = _dot_nt(dyh_scr[...], w2_ref[...])
        a = a_ref[...].astype(F32)
        b = b_ref[...].astype(F32)
        sig = _sigmoid(a)
        sl = a * sig
        s_ref[...] = _b(sl * b)
        da = _b(ds * b * (sig * (1.0 + a * (1.0 - sig))))
        db = _b(ds * sl)
        da_ref[...] = da
        db_ref[...] = db
        du_acc[...] += _dot_nt(da, w1_ref[...]) + _dot_nt(db, w3_ref[...])

        @pl.when(k == nk - 1)
        def _():
            hh = h_ref[...]
            rstd = lax.rsqrt(jnp.mean(hh * hh, axis=-1, keepdims=True) + EPS)
            xh = hh * rstd
            gg = g_ref[...]
            u_ref[...] = _b(xh * gg)
            du = du_acc[...]
            dg_ref[...] += jnp.sum(du * xh, axis=0, keepdims=True)
            dxh = du * gg
            dh_ref[...] = dy_ref[...] + rstd * (dxh - xh * jnp.mean(dxh * xh, axis=-1, keepdims=True))

    wspec = lambda r, c: pl.BlockSpec((None, None, r, c), lambda i, k: (k, blk, 0, 0))
    tok = pl.BlockSpec((tm, d), lambda i, k: (i, 0))
    hid = pl.BlockSpec((None, tm, fs), lambda i, k: (k, i, 0))
    return pl.pallas_call(
        body, name="ffn_bwd",
        grid=(t // tm, nk),
        in_specs=[tok, tok, pl.BlockSpec((1, d), lambda i, k: (0, 0)), hid, hid, wspec(d, fs), wspec(d, fs), wspec(fs, d),
                  DEP_SPEC],
        out_specs=[tok, tok, hid, hid, hid, pl.BlockSpec((1, d), lambda i, k: (0, 0))],
        out_shape=[jax.ShapeDtypeStruct((t, d), F32), jax.ShapeDtypeStruct((t, d), BF16),
                   jax.ShapeDtypeStruct((nk, t, fs), BF16), jax.ShapeDtypeStruct((nk, t, fs), BF16),
                   jax.ShapeDtypeStruct((nk, t, fs), BF16), jax.ShapeDtypeStruct((1, d), F32)],
        scratch_shapes=[pltpu.VMEM((tm, d), BF16), pltpu.VMEM((tm, d), F32)],
        compiler_params=_params("arbitrary", "arbitrary"),
    )(h, dy, g, a_s, b_s, w1, w3, w2, dep)


def wgrad_grouped_b(a, bs, scale=1.0):
    t, m = a.shape
    ng, _, n = bs.shape
    tk = _tile(t, 2048)

    def body(a_ref, b_ref, o_ref):
        j = pl.program_id(1)

        @pl.when(j == 0)
        def _():
            o_ref[...] = jnp.zeros_like(o_ref)

        o_ref[...] += _dot_tn(_b(a_ref[...]), _b(b_ref[...]))

        if scale != 1.0:
            @pl.when(j == pl.num_programs(1) - 1)
            def _():
                o_ref[...] = o_ref[...] * scale

    return pl.pallas_call(
        body, name="wgrad_gb",
        grid=(ng, t // tk),
        in_specs=[pl.BlockSpec((tk, m), lambda k, j: (j, 0)), pl.BlockSpec((None, tk, n), lambda k, j: (k, j, 0))],
        out_specs=pl.BlockSpec((None, m, n), lambda k, j: (k, 0, 0)),
        out_shape=jax.ShapeDtypeStruct((ng, m, n), F32),
        compiler_params=_params("arbitrary", "arbitrary"),
    )(a, bs)


def wgrad_grouped_a(as_, b, scale=1.0):
    ng, t, m = as_.shape
    n = b.shape[1]
    tk = _tile(t, 2048)

    def body(a_ref, b_ref, o_ref):
        j = pl.program_id(1)

        @pl.when(j == 0)
        def _():
            o_ref[...] = jnp.zeros_like(o_ref)

        o_ref[...] += _dot_tn(_b(a_ref[...]), _b(b_ref[...]))

        if scale != 1.0:
            @pl.when(j == pl.num_programs(1) - 1)
            def _():
                o_ref[...] = o_ref[...] * scale

    return pl.pallas_call(
        body, name="wgrad_ga",
        grid=(ng, t // tk),
        in_specs=[pl.BlockSpec((None, tk, m), lambda k, j: (k, j, 0)), pl.BlockSpec((tk, n), lambda k, j: (j, 0))],
        out_specs=pl.BlockSpec((None, m, n), lambda k, j: (k, 0, 0)),
        out_shape=jax.ShapeDtypeStruct((ng, m, n), F32),
        compiler_params=_params("arbitrary", "arbitrary"),
    )(as_, b)


def wgrad(a, b):
    t, m = a.shape
    n = b.shape[1]
    tk = _tile(t, 1024)
    tn = _lane_tile(n, 1536 if m <= 1024 else 512)

    def body(a_ref, b_ref, o_ref):
        @pl.when(pl.program_id(1) == 0)
        def _():
            o_ref[...] = jnp.zeros_like(o_ref)

        o_ref[...] += _dot_tn(_b(a_ref[...]), _b(b_ref[...]))

    return pl.pallas_call(
        body, name="wgrad",
        grid=(n // tn, t // tk),
        in_specs=[pl.BlockSpec((tk, m), lambda c, j: (j, 0)), pl.BlockSpec((tk, tn), lambda c, j: (j, c))],
        out_specs=pl.BlockSpec((m, tn), lambda c, j: (0, c)),
        out_shape=jax.ShapeDtypeStruct((m, n), F32),
        compiler_params=_params("arbitrary", "arbitrary"),
    )(a, b)


def norm_mm(h, g, w):
    t, d = h.shape
    n = w.shape[1]
    tm = _tile(t, 512)
    tn = _lane_tile(n)

    def body(h_ref, g_ref, w_ref, o_ref, u_scr):
        @pl.when(pl.program_id(1) == 0)
        def _():
            u_scr[...] = _b(_rms(h_ref[...]) * g_ref[...])

        o_ref[...] = _dot(u_scr[...], w_ref[...])

    return pl.pallas_call(
        body, name="norm_mm",
        grid=(t // tm, n // tn),
        in_specs=[pl.BlockSpec((tm, d), lambda i, j: (i, 0)), pl.BlockSpec((1, d), lambda i, j: (0, 0)),
                  pl.BlockSpec((d, tn), lambda i, j: (0, j))],
        out_specs=pl.BlockSpec((tm, tn), lambda i, j: (i, j)),
        out_shape=jax.ShapeDtypeStruct((t, n), F32),
        scratch_shapes=[pltpu.VMEM((tm, d), BF16)],
        compiler_params=_params("arbitrary", "arbitrary"),
    )(h, g, w)


def norm_mm_bwd(h, g, w, dout, dres, dep, scale=1.0):
    t, d = h.shape
    n = w.shape[1]
    tm = _tile(t, 512)
    tn = _lane_tile(n)
    nj = n // tn

    def body(h_ref, g_ref, w_ref, do_ref, dr_ref, dep_ref, dh_ref, u_ref, dg_ref, du_acc):
        i, j = pl.program_id(0), pl.program_id(1)

        @pl.when(j == 0)
        def _():
            du_acc[...] = jnp.zeros_like(du_acc)

        @pl.when((j == 0) & (i == 0))
        def _():
            dg_ref[...] = jnp.zeros_like(dg_ref)

        du_acc[...] += _dot_nt(_b(do_ref[...]), w_ref[...])

        @pl.when(j == nj - 1)
        def _():
            hh = h_ref[...]
            rstd = lax.rsqrt(jnp.mean(hh * hh, axis=-1, keepdims=True) + EPS)
            xh = hh * rstd
            gg = g_ref[...]
            u_ref[...] = _b(xh * gg)
            du = du_acc[...] * scale
            dg_ref[...] += jnp.sum(du * xh, axis=0, keepdims=True)
            dxh = du * gg
            dh_ref[...] = dr_ref[...] + rstd * (dxh - xh * jnp.mean(dxh * xh, axis=-1, keepdims=True))

    tok = pl.BlockSpec((tm, d), lambda i, j: (i, 0))
    return pl.pallas_call(
        body, name="norm_mm_bwd",
        grid=(t // tm, nj),
        in_specs=[tok, pl.BlockSpec((1, d), lambda i, j: (0, 0)), pl.BlockSpec((d, tn), lambda i, j: (0, j)),
                  pl.BlockSpec((tm, tn), lambda i, j: (i, j)), tok, DEP_SPEC],
        out_specs=[tok, tok, pl.BlockSpec((1, d), lambda i, j: (0, 0))],
        out_shape=[jax.ShapeDtypeStruct((t, d), F32), jax.ShapeDtypeStruct((t, d), BF16),
                   jax.ShapeDtypeStruct((1, d), F32)],
        scratch_shapes=[pltpu.VMEM((tm, d), F32)],
        compiler_params=_params("arbitrary", "arbitrary"),
    )(h, g, w, dout, dres, dep)


CONV_COLS = 512


CONV_ROWS = 64


def _conv_pre(ext, w, b, r0, n):
    return (b + w[0:1] * ext[pl.ds(5 + r0, n), :] + w[1:2] * ext[pl.ds(6 + r0, n), :]
            + w[2:3] * ext[pl.ds(7 + r0, n), :] + w[3:4] * ext[pl.ds(8 + r0, n), :])


def conv_fwd(zx, cw, cb, col0):
    t = zx.shape[0]
    c = cw.shape[1]
    tm = _tile(t, 512)
    cb0 = col0 // CONV_COLS

    rc = _tile(tm, CONV_ROWS)

    def body(x_ref, w_ref, b_ref, o_ref, ext):
        @pl.when(pl.program_id(1) == 0)
        def _():
            ext[0:8, :] = jnp.zeros((8, CONV_COLS), F32)

        ext[8:, :] = x_ref[...]
        w, b = w_ref[...], b_ref[...]
        for r0 in range(0, tm, rc):
            o_ref[r0:r0 + rc, :] = _silu(_conv_pre(ext, w, b, r0, rc))
        ext[0:8, :] = ext[tm:tm + 8, :]

    return pl.pallas_call(
        body, name="conv_fwd",
        grid=(c // CONV_COLS, t // tm),
        in_specs=[pl.BlockSpec((tm, CONV_COLS), lambda j, i: (i, cb0 + j)),
                  pl.BlockSpec((8, CONV_COLS), lambda j, i: (0, j)), pl.BlockSpec((1, CONV_COLS), lambda j, i: (0, j))],
        out_specs=pl.BlockSpec((tm, CONV_COLS), lambda j, i: (i, j)),
        out_shape=jax.ShapeDtypeStruct((t, c), F32),
        scratch_shapes=[pltpu.VMEM((tm + 8, CONV_COLS), F32)],
        compiler_params=_params("arbitrary", "arbitrary"),
    )(zx, cw, cb)


def conv_bwd(dzx, zx, dxs, dbm, dcm, cw, cb, col0):
    t = zx.shape[0]
    c = cw.shape[1]
    tm = _tile(t, 512)
    nt = t // tm
    cb0 = col0 // CONV_COLS
    nxs = dxs.shape[1] // CONV_COLS
    hb = tm // 8

    rc = _tile(tm, CONV_ROWS)

    def body(dzx_ref, x_ref, xh_ref, dxs_ref, db_ref, dc_ref, w_ref, b_ref, o_ref, dw_ref, dbias_ref, ext, gy):
        j, i = pl.program_id(0), pl.program_id(1)
        ri = nt - 1 - i

        @pl.when(i == 0)
        def _():
            gy[tm:tm + 8, :] = jnp.zeros((8, CONV_COLS), F32)
            dw_ref[...] = jnp.zeros_like(dw_ref)
            dbias_ref[...] = jnp.zeros_like(dbias_ref)

        ext[0:8, :] = jnp.where(ri > 0, xh_ref[...], 0.0)
        ext[8:, :] = x_ref[...]
        w, b = w_ref[...], b_ref[...]
        dw = [jnp.zeros((1, CONV_COLS), F32) for _ in range(SSM_CONV)]
        dbias = jnp.zeros((1, CONV_COLS), F32)
        for r0 in range(0, tm, rc):
            rows = pl.ds(r0, rc)
            win = [ext[pl.ds(5 + tap + r0, rc), :] for tap in range(SSM_CONV)]
            y = b + w[0:1] * win[0] + w[1:2] * win[1] + w[2:3] * win[2] + w[3:4] * win[3]
            sig = _sigmoid(y)
            dout = jnp.where(j < nxs, dxs_ref[rows, :], jnp.where(j == nxs, db_ref[rows, :], dc_ref[rows, :]))
            g = dout * (sig * (1.0 + y * (1.0 - sig)))
            gy[rows, :] = g
            dbias = dbias + jnp.sum(g, axis=0, keepdims=True)
            for tap in range(SSM_CONV):
                dw[tap] = dw[tap] + jnp.sum(g * win[tap], axis=0, keepdims=True)
        for r0 in range(0, tm, rc):
            o_ref[r0:r0 + rc, :] = (w[0:1] * gy[pl.ds(r0 + 3, rc), :] + w[1:2] * gy[pl.ds(r0 + 2, rc), :]
                                    + w[2:3] * gy[pl.ds(r0 + 1, rc), :] + w[3:4] * gy[pl.ds(r0, rc), :])
        gy[tm:tm + 8, :] = gy[0:8, :]
        for tap in range(SSM_CONV):
            dw_ref[tap:tap + 1, :] += dw[tap]
        dbias_ref[...] += dbias

    return pl.pallas_call(
        body, name="conv_bwd",
        grid=(c // CONV_COLS, nt),
        in_specs=[pl.BlockSpec(memory_space=pl.ANY),
                  pl.BlockSpec((tm, CONV_COLS), lambda j, i: (nt - 1 - i, cb0 + j)),
                  pl.BlockSpec((8, CONV_COLS), lambda j, i: (jnp.maximum((nt - 1 - i) * hb - 1, 0), cb0 + j)),
                  pl.BlockSpec((tm, CONV_COLS), lambda j, i: (nt - 1 - i, jnp.minimum(j, nxs - 1))),
                  pl.BlockSpec((tm, CONV_COLS), lambda j, i: (nt - 1 - i, 0)),
                  pl.BlockSpec((tm, CONV_COLS), lambda j, i: (nt - 1 - i, 0)),
                  pl.BlockSpec((8, CONV_COLS), lambda j, i: (0, j)), pl.BlockSpec((1, CONV_COLS), lambda j, i: (0, j))],
        out_specs=[pl.BlockSpec((tm, CONV_COLS), lambda j, i: (nt - 1 - i, cb0 + j)),
                   pl.BlockSpec((8, CONV_COLS), lambda j, i: (0, j)), pl.BlockSpec((1, CONV_COLS), lambda j, i: (0, j))],
        out_shape=[jax.ShapeDtypeStruct(dzx.shape, F32), jax.ShapeDtypeStruct((8, c), F32),
                   jax.ShapeDtypeStruct((1, c), F32)],
        scratch_shapes=[pltpu.VMEM((tm + 8, CONV_COLS), F32), pltpu.VMEM((tm + 8, CONV_COLS), F32)],
        input_output_aliases={0: 0},
        compiler_params=_params("arbitrary", "arbitrary"),
    )(dzx, zx, zx, dxs, dbm, dcm, cw, cb)


def _ssd_group(xs, bg, cg, dtraw, s0, bias, alog, dsk):
    L = xs.shape[0]
    causal = _iota((L, L), 0) >= _iota((L, L), 1)
    tril = jnp.where(causal, 1.0, 0.0).astype(BF16)
    dt = _softplus(dtraw + bias)
    a = -jnp.exp(alog)
    acum = _xleft(tril, dt * a)
    acum_t = acum.T
    dt_t = dt.T
    cb = _bmm_nt(cg, bg)
    lo = _iota((L, 128), 1) < 64
    lo_row = _iota((1, 128), 1) < 64
    lo_col = _iota((128, 1), 0) < 64
    alast = acum[L - 1:L, :]
    ys, s1s = [], []
    for q in range(4):
        xp = xs[:, q * 128:(q + 1) * 128]
        sp = s0[q * 128:(q + 1) * 128, :]
        yd, ec, wc, el = [], [], [], []
        for j in range(2):
            r = 2 * q + j
            ac = acum[:, r:r + 1]
            decay = jnp.exp(jnp.where(causal, ac - acum_t[r:r + 1, :], NEG))
            yd.append(_bmm(cb * decay * dt_t[r:r + 1, :], xp))
            ec.append(jnp.exp(ac))
            al = alast[:, r:r + 1]
            wc.append(jnp.exp(al - ac) * dt[:, r:r + 1])
            el.append(jnp.exp(al))
        y_off = _bmm_nt(cg, sp) * jnp.where(lo, ec[0], ec[1])
        dsel = jnp.where(lo_row, dsk[:, 2 * q:2 * q + 1], dsk[:, 2 * q + 1:2 * q + 2])
        ys.append(jnp.where(lo, yd[0], yd[1]) + y_off + dsel * xp)
        xw = xp * jnp.where(lo, wc[0], wc[1])
        s1s.append(sp * jnp.where(lo_col, el[0], el[1]) + _bmm_tn(xw, bg))
    return jnp.concatenate(ys, axis=1), jnp.concatenate(s1s, axis=0)


def ssd_fwd(xc, zx, bias, alog, dsk, dt_col0):
    t = xc.shape[0]
    L = _tile(t, SSM_CHUNK)
    nc = t // L
    g = SSM_GROUPS
    dtb = dt_col0 // 128

    def body(xs_ref, b_ref, c_ref, dt_ref, bias_ref, alog_ref, dsk_ref, y_ref, st_ref, state):
        c, gi = pl.program_id(0), pl.program_id(1)

        @pl.when(c == 0)
        def _():
            state[gi] = jnp.zeros((512, 128), F32)

        s0 = state[gi]
        st_ref[...] = s0
        y, s1 = _ssd_group(xs_ref[...], b_ref[...], c_ref[...], dt_ref[...], s0,
                           bias_ref[...], alog_ref[...], dsk_ref[...])
        y_ref[...] = y
        state[gi] = s1

    vec = pl.BlockSpec((1, 128), lambda c, gi: (0, gi))
    return pl.pallas_call(
        body, name="ssd_fwd",
        grid=(nc, g),
        in_specs=[pl.BlockSpec((L, 512), lambda c, gi: (c, gi)),
                  pl.BlockSpec((L, 128), lambda c, gi: (c, 16 + gi)),
                  pl.BlockSpec((L, 128), lambda c, gi: (c, 20 + gi)),
                  pl.BlockSpec((L, 128), lambda c, gi: (c, dtb + gi)), vec, vec, vec],
        out_specs=[pl.BlockSpec((L, 512), lambda c, gi: (c, gi)),
                   pl.BlockSpec((None, None, 512, 128), lambda c, gi: (c, gi, 0, 0))],
        out_shape=[jax.ShapeDtypeStruct((t, 2048), F32), jax.ShapeDtypeStruct((nc, g, 512, 128), F32)],
        scratch_shapes=[pltpu.VMEM((g, 512, 128), F32)],
        compiler_params=_params("arbitrary", "arbitrary"),
    )(xc, xc, xc, zx, bias, alog, dsk)


def ssd_bwd(dzx, dy, xc, zx, states, bias, alog, dsk, dt_col0):
    t = xc.shape[0]
    L = _tile(t, SSM_CHUNK)
    nc = t // L
    g = SSM_GROUPS
    dtb = dt_col0 // 128

    def body(dzx_ref, dy_ref, xs_ref, b_ref, c_ref, dt_ref, st_ref, bias_ref, alog_ref, dsk_ref,
             ddt_ref, dxs_ref, db_ref, dc_ref, dbias_ref, dalog_ref, ddsk_ref, dstate):
        c, gi = pl.program_id(0), pl.program_id(1)

        @pl.when(c == 0)
        def _():
            dstate[gi] = jnp.zeros((512, 128), F32)

        @pl.when((c == 0) & (gi == 0))
        def _():
            dbias_ref[...] = jnp.zeros_like(dbias_ref)
            dalog_ref[...] = jnp.zeros_like(dalog_ref)
            ddsk_ref[...] = jnp.zeros_like(ddsk_ref)

        _, vjp = jax.vjp(_ssd_group, xs_ref[...], b_ref[...], c_ref[...], dt_ref[...], st_ref[...],
                         bias_ref[...], alog_ref[...], dsk_ref[...])
        dxs, db, dc, ddt, ds0, dbias, dalog, ddsk = vjp((dy_ref[...], dstate[gi]))
        dxs_ref[...] = dxs
        db_ref[...] = db
        dc_ref[...] = dc
        ddt_ref[...] = ddt
        dstate[gi] = ds0
        dbias_ref[gi] += dbias
        dalog_ref[gi] += dalog
        ddsk_ref[gi] += ddsk

    rc = lambda c: nc - 1 - c
    vec = pl.BlockSpec((1, 128), lambda c, gi: (0, gi))
    acc = pl.BlockSpec((g, 1, 128), lambda c, gi: (0, 0, 0))
    return pl.pallas_call(
        body, name="ssd_bwd",
        grid=(nc, g),
        in_specs=[pl.BlockSpec(memory_space=pl.ANY),
                  pl.BlockSpec((L, 512), lambda c, gi: (rc(c), gi)),
                  pl.BlockSpec((L, 512), lambda c, gi: (rc(c), gi)),
                  pl.BlockSpec((L, 128), lambda c, gi: (rc(c), 16 + gi)),
                  pl.BlockSpec((L, 128), lambda c, gi: (rc(c), 20 + gi)),
                  pl.BlockSpec((L, 128), lambda c, gi: (rc(c), dtb + gi)),
                  pl.BlockSpec((None, None, 512, 128), lambda c, gi: (rc(c), gi, 0, 0)), vec, vec, vec],
        out_specs=[pl.BlockSpec((L, 128), lambda c, gi: (rc(c), dtb + gi)),
                   pl.BlockSpec((L, 512), lambda c, gi: (rc(c), gi)),
                   pl.BlockSpec((L, 128), lambda c, gi: (rc(c), gi)),
                   pl.BlockSpec((L, 128), lambda c, gi: (rc(c), gi)), acc, acc, acc],
        out_shape=[jax.ShapeDtypeStruct(dzx.shape, F32), jax.ShapeDtypeStruct((t, 2048), F32),
                   jax.ShapeDtypeStruct((t, 512), F32), jax.ShapeDtypeStruct((t, 512), F32),
                   jax.ShapeDtypeStruct((g, 1, 128), F32), jax.ShapeDtypeStruct((g, 1, 128), F32),
                   jax.ShapeDtypeStruct((g, 1, 128), F32)],
        scratch_shapes=[pltpu.VMEM((g, 512, 128), F32)],
        input_output_aliases={0: 0},
        compiler_params=_params("arbitrary", "arbitrary"),
    )(dzx, dy, xc, xc, xc, zx, states, bias, alog, dsk)


def _gate_tile(y, z, gn):
    gated = y * _silu(z)
    parts = [_rms(gated[:, k * 512:(k + 1) * 512]) for k in range(SSM_GROUPS)]
    return jnp.concatenate(parts, axis=1) * gn


def gate_out_fwd(h, y, zx, gn, w_out):
    t, d = h.shape
    di = y.shape[1]
    tm = _tile(t, 256)

    def body(h_ref, y_ref, z_ref, gn_ref, w_ref, o_ref):
        yn = _gate_tile(y_ref[...], z_ref[...], gn_ref[...])
        o_ref[...] = h_ref[...] + _dot(_b(yn), w_ref[...])

    return pl.pallas_call(
        body, name="gate_out_fwd",
        grid=(t // tm,),
        in_specs=[pl.BlockSpec((tm, d), lambda i: (i, 0)), pl.BlockSpec((tm, di), lambda i: (i, 0)),
                  pl.BlockSpec((tm, di), lambda i: (i, 0)), pl.BlockSpec((1, di), lambda i: (0, 0)),
                  pl.BlockSpec((di, d), lambda i: (0, 0))],
        out_specs=pl.BlockSpec((tm, d), lambda i: (i, 0)),
        out_shape=jax.ShapeDtypeStruct((t, d), F32),
        compiler_params=_params("arbitrary"),
    )(h, y, zx, gn, w_out)


def gate_out_bwd(dy, y, zx, gn, w_out, n_zx, dep):
    t, d = dy.shape
    di = y.shape[1]
    tm = _tile(t, 256)

    def body(dy_ref, y_ref, z_ref, gn_ref, w_ref, dep_ref, dz_ref, dys_ref, yn_ref, dgn_ref):
        @pl.when(pl.program_id(0) == 0)
        def _():
            dgn_ref[...] = jnp.zeros_like(dgn_ref)

        yn, vjp = jax.vjp(_gate_tile, y_ref[...], z_ref[...], gn_ref[...])
        dyn = _dot_nt(_b(dy_ref[...]), w_ref[...])
        dys, dz, dgn = vjp(dyn)
        yn_ref[...] = _b(yn)
        dys_ref[...] = dys
        dz_ref[...] = dz
        dgn_ref[...] += dgn

    return pl.pallas_call(
        body, name="gate_out_bwd",
        grid=(t // tm,),
        in_specs=[pl.BlockSpec((tm, d), lambda i: (i, 0)), pl.BlockSpec((tm, di), lambda i: (i, 0)),
                  pl.BlockSpec((tm, di), lambda i: (i, 0)), pl.BlockSpec((1, di), lambda i: (0, 0)),
                  pl.BlockSpec((di, d), lambda i: (0, 0)), DEP_SPEC],
        out_specs=[pl.BlockSpec((tm, di), lambda i: (i, 0)), pl.BlockSpec((tm, di), lambda i: (i, 0)),
                   pl.BlockSpec((tm, di), lambda i: (i, 0)), pl.BlockSpec((1, di), lambda i: (0, 0))],
        out_shape=[jax.ShapeDtypeStruct((t, n_zx), F32), jax.ShapeDtypeStruct((t, di), F32),
                   jax.ShapeDtypeStruct((t, di), BF16), jax.ShapeDtypeStruct((1, di), F32)],
        compiler_params=_params("arbitrary"),
    )(dy, y, zx, gn, w_out, dep)


def _attn_block(qp, kvp, kvc, biasm, sinks, qg, kg, w_o, first):
    nq = qp.shape[0]
    n_pairs = qp.shape[1] // 128
    hk = n_pairs
    rows = hk * nq
    seg = functools.partial(_xright, m=_blockdiag64(128))
    scale = ATT_HEAD_DIM ** -0.5
    qi = (_iota((rows, 2 * nq), 0) % nq) + nq
    kj = _iota((rows, 2 * nq), 1)
    dist = qi - kj
    valid = (dist >= 0) & (dist < ATT_WINDOW) & (jnp.logical_not(first) | (kj >= nq))
    lo = _iota((nq, 128), 1) < 64
    kv = jnp.concatenate([kvp, kvc], axis=0)
    outs = [None] * n_pairs
    for kvh in range(2):
        kn = _group64_rms(kv[:, kvh * 128:(kvh + 1) * 128], seg) * kg
        vv = kv[:, 256 + kvh * 128:256 + (kvh + 1) * 128]
        pairs = range(kvh * hk // 2, (kvh + 1) * hk // 2)
        qs, sk = [], []
        for p in pairs:
            qn = _group64_rms(qp[:, p * 128:(p + 1) * 128], seg) * qg
            qs += [jnp.where(lo, qn, 0.0), jnp.where(lo, 0.0, qn)]
            sk += [jnp.broadcast_to(sinks[:, h:h + 1], (nq, 1)) for h in (2 * p, 2 * p + 1)]
        sink = jnp.concatenate(sk, axis=0)
        s = _bmm_nt(jnp.concatenate(qs, axis=0), kn) * scale + biasm[kvh * rows:(kvh + 1) * rows]
        s = jnp.where(valid, s, NEG)
        m = lax.stop_gradient(jnp.maximum(jnp.max(s, axis=-1, keepdims=True), sink))
        pexp = jnp.exp(s - m)
        den = jnp.sum(pexp, axis=-1, keepdims=True) + jnp.exp(sink - m)
        o = _bmm(pexp * (1.0 / den), vv)
        for n, p in enumerate(pairs):
            outs[p] = jnp.where(lo, o[2 * n * nq:(2 * n + 1) * nq], o[(2 * n + 1) * nq:(2 * n + 2) * nq])
    o = jnp.concatenate(outs, axis=1)
    return _bmm(o, w_o), o


def attn_fwd(h, qp, kvd, biasm, sinks, qg, kg, w_o):
    t, d = h.shape
    nq = ATT_WINDOW
    nb = t // nq
    nh = qp.shape[1] // ATT_HEAD_DIM

    def body(h_ref, q_ref, kp_ref, kc_ref, bias_ref, s_ref, qg_ref, kg_ref, w_ref, o_ref):
        out, _ = _attn_block(q_ref[...], kp_ref[...], kc_ref[...], bias_ref[...], s_ref[...], qg_ref[...],
                             kg_ref[...], w_ref[...], pl.program_id(0) == 0)
        o_ref[...] = h_ref[...] + out

    vec = pl.BlockSpec((1, 128), lambda i: (0, 0))
    return pl.pallas_call(
        body, name="attn_fwd",
        grid=(nb,),
        in_specs=[pl.BlockSpec((nq, d), lambda i: (i, 0)), pl.BlockSpec((nq, nh * 64), lambda i: (i, 0)),
                  pl.BlockSpec((nq, 512), lambda i: (jnp.maximum(i - 1, 0), 0)),
                  pl.BlockSpec((nq, 512), lambda i: (i, 0)),
                  pl.BlockSpec((nh * nq, 2 * nq), lambda i: (0, 0)), vec, vec, vec,
                  pl.BlockSpec((nh * 64, d), lambda i: (0, 0))],
        out_specs=pl.BlockSpec((nq, d), lambda i: (i, 0)),
        out_shape=jax.ShapeDtypeStruct((t, d), F32),
        compiler_params=_params("arbitrary"),
    )(h, qp, kvd, kvd, biasm, sinks, qg, kg, w_o)


def attn_bwd(dy, qp, kvd, biasm, sinks, qg, kg, w_o, dep):
    t, d = dy.shape
    nq = ATT_WINDOW
    nb = t // nq
    nh = qp.shape[1] // ATT_HEAD_DIM

    def body(dy_ref, q_ref, kp_ref, kc_ref, bias_ref, s_ref, qg_ref, kg_ref, w_ref, dep_ref,
             dq_ref, dkv_ref, o_ref, dbias_ref, ds_ref, dqg_ref, dkg_ref, carry):
        i = pl.program_id(0)

        @pl.when(i == 0)
        def _():
            carry[...] = jnp.zeros_like(carry)
            dbias_ref[...] = jnp.zeros_like(dbias_ref)
            ds_ref[...] = jnp.zeros_like(ds_ref)
            dqg_ref[...] = jnp.zeros_like(dqg_ref)
            dkg_ref[...] = jnp.zeros_like(dkg_ref)

        @pl.when(i < nb)
        def _():
            fn = functools.partial(_attn_block, w_o=w_ref[...], first=(i == 0))
            (_, o), vjp = jax.vjp(fn, q_ref[...], kp_ref[...], kc_ref[...], bias_ref[...], s_ref[...],
                                  qg_ref[...], kg_ref[...])
            dq, dkp, dkc, dbias, dsk, dqg, dkg = vjp((dy_ref[...], jnp.zeros((nq, nh * 64), F32)))
            dq_ref[...] = dq
            o_ref[...] = _b(o)
            dkv_ref[...] = _fold64(carry[...] + dkp)
            carry[...] = dkc
            dbias_ref[...] += dbias
            ds_ref[...] += dsk
            dqg_ref[...] += _fold64(dqg)
            dkg_ref[...] += _fold64(dkg)

        @pl.when(i == nb)
        def _():
            dkv_ref[...] = _fold64(carry[...])

    cl = lambda i: jnp.minimum(i, nb - 1)
    vec = pl.BlockSpec((1, 128), lambda i: (0, 0))
    return pl.pallas_call(
        body, name="attn_bwd",
        grid=(nb + 1,),
        in_specs=[pl.BlockSpec((nq, d), lambda i: (cl(i), 0)), pl.BlockSpec((nq, nh * 64), lambda i: (cl(i), 0)),
                  pl.BlockSpec((nq, 512), lambda i: (jnp.maximum(cl(i) - 1, 0), 0)),
                  pl.BlockSpec((nq, 512), lambda i: (cl(i), 0)),
                  pl.BlockSpec((nh * nq, 2 * nq), lambda i: (0, 0)), vec, vec, vec,
                  pl.BlockSpec((nh * 64, d), lambda i: (0, 0)), DEP_SPEC],
        out_specs=[pl.BlockSpec((nq, nh * 64), lambda i: (cl(i), 0)),
                   pl.BlockSpec((nq, 512), lambda i: (jnp.maximum(i - 1, 0), 0)),
                   pl.BlockSpec((nq, nh * 64), lambda i: (cl(i), 0)),
                   pl.BlockSpec((nh * nq, 2 * nq), lambda i: (0, 0)), vec, vec, vec],
        out_shape=[jax.ShapeDtypeStruct((t, nh * 64), F32), jax.ShapeDtypeStruct((t, 512), F32),
                   jax.ShapeDtypeStruct((t, nh * 64), BF16), jax.ShapeDtypeStruct((nh * nq, 2 * nq), F32),
                   jax.ShapeDtypeStruct((1, 128), F32), jax.ShapeDtypeStruct((1, 128), F32),
                   jax.ShapeDtypeStruct((1, 128), F32)],
        scratch_shapes=[pltpu.VMEM((nq, 512), F32)],
        compiler_params=_params("arbitrary"),
    )(dy, qp, kvd, kvd, biasm, sinks, qg, kg, w_o, dep)


def _t5_buckets():
    nq = ATT_WINDOW
    dist = (np.arange(nq)[:, None] + nq) - np.arange(2 * nq)[None, :]
    n = np.maximum(dist, 0)
    max_exact = REL_BUCKETS // 2
    nf = np.maximum(n, 1).astype(np.float32)
    large = max_exact + (np.log(nf / max_exact) / math.log(ATT_WINDOW / max_exact)
                         * (REL_BUCKETS - max_exact)).astype(np.int32)
    large = np.minimum(large, REL_BUCKETS - 1)
    return np.where(n < max_exact, n, large).astype(np.int32)


def rel_bias_bwd(dbias, buckets):
    nh = dbias.shape[0]

    def body(db_ref, bk_ref, o_ref):
        bk = bk_ref[...]
        lane = _iota((1, 128), 1)
        row = _iota((REL_BUCKETS, 128), 0)
        acc = jnp.zeros((REL_BUCKETS, 128), F32)
        for h in range(nh):
            dbh = db_ref[h]
            for b in range(REL_BUCKETS):
                v = jnp.sum(jnp.where(bk == b, dbh, 0.0))
                acc = acc + jnp.where((row == b) & (lane == h), v, 0.0)
        o_ref[...] = acc

    return pl.pallas_call(
        body, name="rel_bias_bwd",
        out_shape=jax.ShapeDtypeStruct((REL_BUCKETS, 128), F32),
        compiler_params=_params(),
    )(dbias, buckets)


def loss_head(y, target):
    t, d = y.shape
    tm = _tile(t, 512)

    def body(y_ref, t_ref, l_ref, dy_ref):
        @pl.when(pl.program_id(0) == 0)
        def _():
            l_ref[...] = jnp.zeros_like(l_ref)

        e = y_ref[...] - t_ref[...]
        l_ref[...] += 0.5 * jnp.sum(jnp.mean(e * e, axis=-1, keepdims=True), axis=0, keepdims=True)
        dy_ref[...] = e * (1.0 / d)

    return pl.pallas_call(
        body, name="loss_head",
        grid=(t // tm,),
        in_specs=[pl.BlockSpec((tm, d), lambda i: (i, 0)), pl.BlockSpec((tm, d), lambda i: (i, 0))],
        out_specs=[pl.BlockSpec((1, 1), lambda i: (0, 0)), pl.BlockSpec((tm, d), lambda i: (i, 0))],
        out_shape=[jax.ShapeDtypeStruct((1, 1), F32), jax.ShapeDtypeStruct((t, d), F32)],
        compiler_params=_params("arbitrary"),
    )(y, target)


def adamw(w, g, m, v):
    r, c = w.shape
    tr = r if r <= 512 else _tile(r, 256)

    def body(w_ref, g_ref, m_ref, v_ref, d_ref, nm_ref, nv_ref):
        gg = g_ref[...]
        nm = ADAM_B1 * m_ref[...] + (1.0 - ADAM_B1) * gg
        nv = ADAM_B2 * v_ref[...] + (1.0 - ADAM_B2) * (gg * gg)
        m_hat = nm / (1.0 - ADAM_B1 ** ADAM_STEP)
        v_hat = nv / (1.0 - ADAM_B2 ** ADAM_STEP)
        d_ref[...] = -ADAM_LR * (m_hat / (jnp.sqrt(v_hat) + ADAM_EPS) + ADAM_WD * w_ref[...])
        nm_ref[...] = nm
        nv_ref[...] = nv

    spec = pl.BlockSpec((tr, c), lambda i: (i, 0))
    shp = jax.ShapeDtypeStruct((r, c), F32)
    return pl.pallas_call(
        body, name="adamw",
        grid=(r // tr,),
        in_specs=[spec] * 4, out_specs=[spec] * 3, out_shape=[shp] * 3,
        compiler_params=_params("arbitrary"),
    )(w, g, m, v)


def _my_pos():
    return lax.axis_index("x"), lax.axis_index("y"), lax.axis_index("c")


def _other_chips(x, y):
    return [(1 - x, y), (x, 1 - y), (1 - x, 1 - y)]


def _chip_id(x, y):
    return 2 * x + y


def gather_weights(bufs):
    n = len(bufs)

    def body(*refs):
        outs = refs[n:2 * n]
        send_sems, recv_sems = refs[2 * n:]
        x, y, c = _my_pos()
        sibling = (x, y, 1 - c)
        chips = _other_chips(x, y)

        def copy(p, k, chip, half, to):
            blk = outs[p].at[_chip_id(*chip), half]
            return pltpu.make_async_remote_copy(
                src_ref=blk, dst_ref=blk, send_sem=send_sems.at[p, k], recv_sem=recv_sems.at[p, k],
                device_id=to, device_id_type=MESH)

        first = [[copy(p, j, (x, y), c, (*chip, c)) for j, chip in enumerate(chips)] for p in range(n)]
        for p in range(n):
            for cp in first[p]:
                cp.start()
        passed = [[copy(p, 3 + j, chip, c, sibling) for j, chip in enumerate(chips)] for p in range(n)]
        for p in range(n):
            for j, chip in enumerate(chips):
                copy(p, j, chip, c, (x, y, c)).wait_recv()
                passed[p][j].start()
        for p in range(n):
            for j, chip in enumerate(chips):
                copy(p, 3 + j, chip, 1 - c, (x, y, c)).wait_recv()
        for p in range(n):
            for cp in first[p] + passed[p]:
                cp.wait_send()

    any_spec = pl.BlockSpec(memory_space=pl.ANY)
    return pl.pallas_call(
        body, name="gather_weights",
        in_specs=[any_spec] * n, out_specs=[any_spec] * n,
        out_shape=[jax.ShapeDtypeStruct(b.shape, b.dtype) for b in bufs],
        scratch_shapes=[pltpu.SemaphoreType.DMA((n, 6)), pltpu.SemaphoreType.DMA((n, 6))],
        input_output_aliases={p: p for p in range(n)},
    )(*bufs)


HBM_SPEC = pl.BlockSpec(memory_space=pltpu.HBM)
SEM_SPEC = pl.BlockSpec(memory_space=pltpu.SEMAPHORE)
DATAFLOW = pltpu.SideEffectType.DATAFLOW_SIDE_EFFECTING


def _in_hbm(a):
    return pltpu.with_memory_space_constraint(a, pltpu.HBM)


def _ici_gather_copy(buf, p, j, chip, c, to, send_sems, recv_sems):
    blk = buf.at[_chip_id(*chip), c]
    return pltpu.make_async_remote_copy(
        src_ref=blk, dst_ref=blk, send_sem=send_sems.at[3 * p + j], recv_sem=recv_sems.at[3 * p + j],
        device_id=to, device_id_type=MESH)


def gather_start(bufs, tag):
    n = len(bufs)

    def body(*refs):
        ins = refs[:n]
        send_sems, recv_sems = refs[n], refs[n + 1]
        token = refs[2 * n + 2]
        x, y, c = _my_pos()
        for p in range(n):
            for j, chip in enumerate(_other_chips(x, y)):
                _ici_gather_copy(ins[p], p, j, (x, y), c, (*chip, c), send_sems, recv_sems).start()
        token[...] = jnp.zeros_like(token)

    out = pl.pallas_call(
        body, name="gather_start_" + tag,
        in_specs=[HBM_SPEC] * n,
        out_specs=(SEM_SPEC, SEM_SPEC, *([HBM_SPEC] * n), pl.BlockSpec(memory_space=pltpu.VMEM)),
        out_shape=(pltpu.SemaphoreType.DMA((3 * n,)), pltpu.SemaphoreType.DMA((3 * n,)),
                   *[pltpu.HBM(b.shape, b.dtype) for b in bufs], jax.ShapeDtypeStruct((8, 128), F32)),
        input_output_aliases={p: 2 + p for p in range(n)},
        compiler_params=pltpu.CompilerParams(has_side_effects=DATAFLOW),
    )(*[_in_hbm(b) for b in bufs])
    return out[0], out[1], list(out[2:2 + n]), out[2 + n]


def gather_wait(send_sems, recv_sems, bufs, after, tag):
    n = len(bufs)

    def body(*refs):
        ins = refs[:n]
        send_sems, recv_sems = refs[n], refs[n + 1]
        x, y, c = _my_pos()
        for p in range(n):
            for j, chip in enumerate(_other_chips(x, y)):
                _ici_gather_copy(ins[p], p, j, (x, y), c, (*chip, c), send_sems, recv_sems).wait_send()
                _ici_gather_copy(ins[p], p, j, chip, c, (x, y, c), send_sems, recv_sems).wait_recv()

    out = pl.pallas_call(
        body, name="gather_wait_" + tag,
        in_specs=[HBM_SPEC] * n + [SEM_SPEC, SEM_SPEC, pl.BlockSpec(memory_space=pl.ANY)],
        out_specs=[HBM_SPEC] * n,
        out_shape=[pltpu.HBM(b.shape, b.dtype) for b in bufs],
        input_output_aliases={p: p for p in range(n)},
        compiler_params=pltpu.CompilerParams(has_side_effects=DATAFLOW),
    )(*bufs, send_sems, recv_sems, after)
    return list(out)


def forward_to_sibling(bufs):
    n = len(bufs)

    def body(*refs):
        outs = refs[n:2 * n]
        send_sems, recv_sems = refs[2 * n:]
        x, y, c = _my_pos()
        chips = _other_chips(x, y)
        sent = []
        for p in range(n):
            for j, chip in enumerate(chips):
                cp = _ici_gather_copy(outs[p], p, j, chip, c, (x, y, 1 - c), send_sems, recv_sems)
                cp.start()
                sent.append(cp)
        for p in range(n):
            for j, chip in enumerate(chips):
                _ici_gather_copy(outs[p], p, j, chip, 1 - c, (x, y, c), send_sems, recv_sems).wait_recv()
        for cp in sent:
            cp.wait_send()

    any_spec = pl.BlockSpec(memory_space=pl.ANY)
    return pl.pallas_call(
        body, name="forward_to_sibling",
        in_specs=[any_spec] * n, out_specs=[any_spec] * n,
        out_shape=[jax.ShapeDtypeStruct(b.shape, b.dtype) for b in bufs],
        scratch_shapes=[pltpu.SemaphoreType.DMA((3 * n,)), pltpu.SemaphoreType.DMA((3 * n,))],
        input_output_aliases={p: p for p in range(n)},
    )(*bufs)


def allreduce_small(v):
    r, c = v.shape

    def body(v_ref, o_ref, buf, send_sems, recv_sems):
        x, y, cc = _my_pos()
        me = 4 * x + 2 * y + cc
        buf[me] = v_ref[...]
        copies = []
        for k in range(1, 8):
            dx, dy, dc = (k >> 2) & 1, (k >> 1) & 1, k & 1
            peer = (x ^ dx, y ^ dy, cc ^ dc)
            cp = pltpu.make_async_remote_copy(
                src_ref=v_ref, dst_ref=buf.at[me], send_sem=send_sems.at[k - 1], recv_sem=recv_sems.at[k - 1],
                device_id=peer, device_id_type=MESH)
            cp.start()
            copies.append(cp)
        for cp in copies:
            cp.wait_recv()
        for cp in copies:
            cp.wait_send()
        acc = buf[0]
        for k in range(1, 8):
            acc = acc + buf[k]
        o_ref[...] = acc

    vm = pl.BlockSpec(memory_space=pltpu.VMEM)
    return pl.pallas_call(
        body, name="allreduce_small",
        in_specs=[vm], out_specs=vm,
        out_shape=jax.ShapeDtypeStruct((r, c), F32),
        scratch_shapes=[pltpu.VMEM((8, r, c), F32), pltpu.SemaphoreType.DMA((7,)), pltpu.SemaphoreType.DMA((7,))],
    )(v)


def exchange_sibling_halves(grads):
    n = len(grads)

    def body(*refs):
        ins, outs = refs[:n], refs[n:2 * n]
        send_sems, recv_sems = refs[2 * n:]
        x, y, c = _my_pos()
        copies = []
        for p in range(n):
            cp = pltpu.make_async_remote_copy(
                src_ref=ins[p].at[:, 1 - c], dst_ref=outs[p], send_sem=send_sems.at[p], recv_sem=recv_sems.at[p],
                device_id=(x, y, 1 - c), device_id_type=MESH)
            cp.start()
            copies.append(cp)
        for cp in copies:
            cp.wait_recv()
        for cp in copies:
            cp.wait_send()

    any_spec = pl.BlockSpec(memory_space=pl.ANY)
    return pl.pallas_call(
        body, name="exchange_sibling_halves",
        in_specs=[any_spec] * n, out_specs=[any_spec] * n,
        out_shape=[jax.ShapeDtypeStruct((g.shape[0],) + g.shape[2:], g.dtype) for g in grads],
        scratch_shapes=[pltpu.SemaphoreType.DMA((n,)), pltpu.SemaphoreType.DMA((n,))],
    )(*grads)


def _chip_partial_copy(part, land, p, j, chip, c, send_sems, recv_sems):
    return pltpu.make_async_remote_copy(
        src_ref=part.at[_chip_id(*chip)], dst_ref=land.at[j], send_sem=send_sems.at[3 * p + j],
        recv_sem=recv_sems.at[3 * p + j], device_id=(*chip, c), device_id_type=MESH)


def chip_partials_start(parts, tag):
    n = len(parts)
    lands = [lax.empty((N_CHIPS - 1,) + s.shape[1:], s.dtype) for s in parts]

    def body(*refs):
        ins, land = refs[:n], refs[n:2 * n]
        send_sems, recv_sems = refs[2 * n], refs[2 * n + 1]
        token = refs[4 * n + 2]
        x, y, c = _my_pos()
        for p in range(n):
            for j, chip in enumerate(_other_chips(x, y)):
                _chip_partial_copy(ins[p], land[p], p, j, chip, c, send_sems, recv_sems).start()
        token[...] = jnp.zeros_like(token)

    out = pl.pallas_call(
        body, name="chip_partials_start_" + tag,
        in_specs=[HBM_SPEC] * (2 * n),
        out_specs=(SEM_SPEC, SEM_SPEC, *([HBM_SPEC] * (2 * n)), pl.BlockSpec(memory_space=pltpu.VMEM)),
        out_shape=(pltpu.SemaphoreType.DMA((3 * n,)), pltpu.SemaphoreType.DMA((3 * n,)),
                   *[pltpu.HBM(a.shape, a.dtype) for a in parts + lands], jax.ShapeDtypeStruct((8, 128), F32)),
        input_output_aliases={i: 2 + i for i in range(2 * n)},
        compiler_params=pltpu.CompilerParams(has_side_effects=DATAFLOW),
    )(*[_in_hbm(a) for a in parts + lands])
    return out[0], out[1], list(out[2:2 + n]), list(out[2 + n:2 + 2 * n]), out[2 + 2 * n]


def chip_partials_wait(send_sems, recv_sems, parts, lands, after, tag):
    n = len(parts)

    def body(*refs):
        ins, land = refs[:n], refs[n:2 * n]
        send_sems, recv_sems = refs[2 * n], refs[2 * n + 1]
        x, y, c = _my_pos()
        for p in range(n):
            for j, chip in enumerate(_other_chips(x, y)):
                cp = _chip_partial_copy(ins[p], land[p], p, j, chip, c, send_sems, recv_sems)
                cp.wait_send()
                cp.wait_recv()

    out = pl.pallas_call(
        body, name="chip_partials_wait_" + tag,
        in_specs=[HBM_SPEC] * (2 * n) + [SEM_SPEC, SEM_SPEC, pl.BlockSpec(memory_space=pl.ANY)],
        out_specs=[HBM_SPEC] * (2 * n),
        out_shape=[pltpu.HBM(a.shape, a.dtype) for a in parts + lands],
        input_output_aliases={i: i for i in range(2 * n)},
        compiler_params=pltpu.CompilerParams(has_side_effects=DATAFLOW),
    )(*parts, *lands, send_sems, recv_sems, after)
    return list(out[n:])


def share_with_sibling(bufs):
    n = len(bufs)

    def body(*refs):
        outs = refs[n:2 * n]
        send_sems, recv_sems = refs[2 * n:]
        x, y, c = _my_pos()
        copies = []
        for p in range(n):
            cp = pltpu.make_async_remote_copy(
                src_ref=outs[p].at[c], dst_ref=outs[p].at[c], send_sem=send_sems.at[p], recv_sem=recv_sems.at[p],
                device_id=(x, y, 1 - c), device_id_type=MESH)
            cp.start()
            copies.append(cp)
        for p in range(n):
            pltpu.make_async_remote_copy(
                src_ref=outs[p].at[1 - c], dst_ref=outs[p].at[1 - c], send_sem=send_sems.at[p],
                recv_sem=recv_sems.at[p], device_id=(x, y, 1 - c), device_id_type=MESH).wait_recv()
        for cp in copies:
            cp.wait_send()

    any_spec = pl.BlockSpec(memory_space=pl.ANY)
    return pl.pallas_call(
        body, name="share_with_sibling",
        in_specs=[any_spec] * n, out_specs=[any_spec] * n,
        out_shape=[jax.ShapeDtypeStruct(b.shape, b.dtype) for b in bufs],
        scratch_shapes=[pltpu.SemaphoreType.DMA((n,)), pltpu.SemaphoreType.DMA((n,))],
        input_output_aliases={p: p for p in range(n)},
    )(*bufs)


def add_sibling(g, recv, half):
    _, _, r, c = g.shape
    tr = _tile(r, 256) if r % 256 == 0 else r

    def body(half_ref, g_ref, r_ref, o32_ref, o16_ref):
        s = g_ref[...] + r_ref[...]
        o32_ref[...] = s
        o16_ref[...] = _b(s)

    return pl.pallas_call(
        body, name="add_sibling",
        grid_spec=pltpu.PrefetchScalarGridSpec(
            num_scalar_prefetch=1, grid=(N_CHIPS, r // tr),
            in_specs=[pl.BlockSpec((None, None, tr, c), lambda k, i, hf: (k, hf[0], i, 0)),
                      pl.BlockSpec((None, tr, c), lambda k, i, hf: (k, i, 0))],
            out_specs=[pl.BlockSpec((None, tr, c), lambda k, i, hf: (k, i, 0)),
                       pl.BlockSpec((None, tr, c), lambda k, i, hf: (k, i, 0))]),
        out_shape=[jax.ShapeDtypeStruct((N_CHIPS, r, c), F32), jax.ShapeDtypeStruct((N_CHIPS, r, c), BF16)],
        compiler_params=_params("arbitrary", "arbitrary"),
    )(half, g, recv)


def add_chip_partials(p32, recv, pos):
    _, r, c = p32.shape
    tr = _tile(r, 256) if r % 256 == 0 else r

    def body(pos_ref, p_ref, r_ref, o_ref):
        acc = p_ref[...]
        for j in range(N_CHIPS - 1):
            acc = acc + r_ref[j].astype(F32)
        o_ref[...] = acc

    return pl.pallas_call(
        body, name="add_chip_partials",
        grid_spec=pltpu.PrefetchScalarGridSpec(
            num_scalar_prefetch=1, grid=(r // tr,),
            in_specs=[pl.BlockSpec((None, tr, c), lambda i, ps: (ps[0], i, 0)),
                      pl.BlockSpec((N_CHIPS - 1, tr, c), lambda i, ps: (0, i, 0))],
            out_specs=pl.BlockSpec((None, tr, c), lambda i, ps: (ps[1], i, 0))),
        out_shape=jax.ShapeDtypeStruct((2, r, c), F32),
        compiler_params=_params("arbitrary"),
    )(pos, p32, recv)


def cast_into_gather(w, pos, row0=0, nrows=None):
    c = w.shape[1]
    nrows = w.shape[0] if nrows is None else nrows
    r = nrows // 2
    common = math.gcd(r, row0) if row0 else r
    tr = max(w for w in range(16, min(common, 512) + 1, 16) if common % w == 0)
    nt = r // tr

    def body(pos_ref, w_ref, o_ref):
        o_ref[...] = _b(w_ref[...])

    return pl.pallas_call(
        body, name="cast_into_gather",
        grid_spec=pltpu.PrefetchScalarGridSpec(
            num_scalar_prefetch=1, grid=(2, nt),
            in_specs=[pl.BlockSpec((tr, c), lambda hf, i, ps: (row0 // tr + hf * nt + i, 0))],
            out_specs=pl.BlockSpec((None, None, tr, c), lambda hf, i, ps: (ps[0], hf, i, 0))),
        out_shape=jax.ShapeDtypeStruct((N_CHIPS, 2, r, c), BF16),
        compiler_params=_params("arbitrary", "arbitrary"),
    )(pos, w)


def build_bias(rel, buckets):
    nb, nh = rel.shape

    def body(rel_ref, bk_ref, o_ref):
        bk = bk_ref[...]
        for h in range(nh):
            acc = jnp.zeros(bk.shape, F32)
            for b in range(nb):
                acc = jnp.where(bk == b, rel_ref[b, h], acc)
            o_ref[h] = acc

    return pl.pallas_call(
        body, name="build_bias",
        in_specs=[pl.BlockSpec(memory_space=pltpu.SMEM), pl.BlockSpec(memory_space=pltpu.VMEM)],
        out_specs=pl.BlockSpec(memory_space=pltpu.VMEM),
        out_shape=jax.ShapeDtypeStruct((nh,) + buckets.shape, F32),
        compiler_params=_params(),
    )(rel, buckets)


SMALL_ROWS = 256


def kernel(x, ffn_norm, ffn_w1, ffn_w3, ffn_w2, ssm_norm, ssm_w_in, ssm_conv_w, ssm_conv_b, ssm_dt_bias, ssm_a_log, ssm_d, ssm_gate_norm, ssm_w_out, kv_norm, w_kv, k_norm, attn_norm, w_q, q_norm, sinks, w_o, rel_bias, loss_target, m_ffn_norm, m_ffn_w1, m_ffn_w3, m_ffn_w2, m_ssm_norm, m_ssm_w_in, m_ssm_conv_w, m_ssm_conv_b, m_ssm_dt_bias, m_ssm_a_log, m_ssm_d, m_ssm_gate_norm, m_ssm_w_out, m_kv_norm, m_w_kv, m_k_norm, m_attn_norm, m_w_q, m_q_norm, m_sinks, m_w_o, m_rel_bias, v_ffn_norm, v_ffn_w1, v_ffn_w3, v_ffn_w2, v_ssm_norm, v_ssm_w_in, v_ssm_conv_w, v_ssm_conv_b, v_ssm_dt_bias, v_ssm_a_log, v_ssm_d, v_ssm_gate_norm, v_ssm_w_out, v_kv_norm, v_w_kv, v_k_norm, v_attn_norm, v_w_q, v_q_norm, v_sinks, v_w_o, v_rel_bias):
    weights = dict(ffn_norm=ffn_norm, ffn_w1=ffn_w1, ffn_w3=ffn_w3, ffn_w2=ffn_w2, ssm_norm=ssm_norm,
                   ssm_w_in=ssm_w_in, ssm_conv_w=ssm_conv_w, ssm_conv_b=ssm_conv_b, ssm_dt_bias=ssm_dt_bias,
                   ssm_a_log=ssm_a_log, ssm_d=ssm_d, ssm_gate_norm=ssm_gate_norm, ssm_w_out=ssm_w_out,
                   kv_norm=kv_norm, w_kv=w_kv, k_norm=k_norm, attn_norm=attn_norm, w_q=w_q, q_norm=q_norm,
                   sinks=sinks, w_o=w_o, rel_bias=rel_bias)
    m_in = dict(ffn_norm=m_ffn_norm, ffn_w1=m_ffn_w1, ffn_w3=m_ffn_w3, ffn_w2=m_ffn_w2, ssm_norm=m_ssm_norm,
                ssm_w_in=m_ssm_w_in, ssm_conv_w=m_ssm_conv_w, ssm_conv_b=m_ssm_conv_b, ssm_dt_bias=m_ssm_dt_bias,
                ssm_a_log=m_ssm_a_log, ssm_d=m_ssm_d, ssm_gate_norm=m_ssm_gate_norm, ssm_w_out=m_ssm_w_out,
                kv_norm=m_kv_norm, w_kv=m_w_kv, k_norm=m_k_norm, attn_norm=m_attn_norm, w_q=m_w_q, q_norm=m_q_norm,
                sinks=m_sinks, w_o=m_w_o, rel_bias=m_rel_bias)
    v_in = dict(ffn_norm=v_ffn_norm, ffn_w1=v_ffn_w1, ffn_w3=v_ffn_w3, ffn_w2=v_ffn_w2, ssm_norm=v_ssm_norm,
                ssm_w_in=v_ssm_w_in, ssm_conv_w=v_ssm_conv_w, ssm_conv_b=v_ssm_conv_b, ssm_dt_bias=v_ssm_dt_bias,
                ssm_a_log=v_ssm_a_log, ssm_d=v_ssm_d, ssm_gate_norm=v_ssm_gate_norm, ssm_w_out=v_ssm_w_out,
                kv_norm=v_kv_norm, w_kv=v_w_kv, k_norm=v_k_norm, attn_norm=v_attn_norm, w_q=v_w_q, q_norm=v_q_norm,
                sinks=v_sinks, w_o=v_w_o, rel_bias=v_rel_bias)
    return _step(x[0], loss_target[0], weights, m_in, v_in)


BIG = ("ffn_w1", "ffn_w3", "ffn_w2", "ssm_w_in", "ssm_w_out", "w_kv", "w_q", "w_o")
SMALL = (("ffn_norm", True), ("ssm_norm", True), ("ssm_conv_w", True), ("ssm_conv_b", True),
         ("ssm_gate_norm", True), ("ssm_dt_bias", False), ("ssm_a_log", False), ("ssm_d", False),
         ("kv_norm", False), ("k_norm", False), ("attn_norm", False), ("q_norm", False), ("sinks", False),
         ("rel_bias", False))


FFN_W = BIG[:3]


def _small_layout(weights):
    off, table = 0, {}
    for name, sharded in SMALL:
        shape = weights[name].shape
        full = shape[:-1] + (shape[-1] * N_CHIPS,) if sharded else shape
        n = int(np.prod(full))
        table[name] = (off, full, sharded)
        off += n
    assert off <= SMALL_ROWS * 128
    return table


def _place_small(values, table, chip, scale_mask):
    flat = jnp.zeros((SMALL_ROWS * 128,), F32)
    for name, (off, full, sharded) in table.items():
        if not sharded:
            continue
        v = values[name].astype(F32)
        lead = int(np.prod(full[:-1]))
        w = v.shape[-1]
        blk = jnp.zeros((lead, full[-1]), F32)
        blk = lax.dynamic_update_slice(blk, v.reshape(lead, w) * scale_mask, (0, chip * w))
        flat = lax.dynamic_update_slice(flat, blk.reshape(-1), (off,))
    return flat.reshape(SMALL_ROWS, 128)


def _take_small(mat, table, name):
    off, full, _ = table[name]
    n = int(np.prod(full))
    return mat.reshape(-1)[off:off + n].reshape(full)


def _step(x, target, weights, m_in, v_in):
    t, d = x.shape
    xi, yi, ci = lax.axis_index("x"), lax.axis_index("y"), lax.axis_index("c")
    chip = 2 * xi + yi
    pos_arr = jnp.stack([chip, ci]).astype(jnp.int32)
    half_arr = jnp.reshape(ci, (1,)).astype(jnp.int32)

    fs = weights["ffn_w1"].shape[-1]
    ffn_rows = {"ffn_w1": d, "ffn_w3": d, "ffn_w2": fs}
    w2d = {n: weights[n].reshape(-1, weights[n].shape[-1]) for n in BIG}
    first = gather_weights([cast_into_gather(w2d[n], pos_arr, 0, ffn_rows[n]) for n in FFN_W])
    mamba_w = ("ssm_w_in", "ssm_w_out")
    late_w = ("w_kv", "w_q", "w_o")
    ms, mr, mbufs, tok_m = gather_start([cast_into_gather(w2d[n], pos_arr) for n in mamba_w], "mamba")
    ls, lr, lbufs, tok_l = gather_start(
        [cast_into_gather(w2d[n], pos_arr, ffn_rows[n], 3 * ffn_rows[n]) for n in FFN_W]
        + [cast_into_gather(w2d[n], pos_arr) for n in late_w], "late")
    no_dep = jnp.zeros((8, 128), F32)
    table = _small_layout(weights)
    south = (ci == 0).astype(F32)
    small = allreduce_small(_place_small(weights, table, chip, south))
    sp = {n: _take_small(small, table, n) if sh else weights[n] for n, sh in SMALL}

    ffn_first = [first[0].reshape(N_CHIPS, 1, d, fs), first[1].reshape(N_CHIPS, 1, d, fs),
                 first[2].reshape(N_CHIPS, 1, fs, d)]
    ffn_g = sp["ffn_norm"]
    h0 = x
    h1, a00, b00 = ffn_fwd(h0, ffn_g[0, 0].reshape(1, d), *ffn_first, 0, tok_m + tok_l)
    gathered = dict(zip(mamba_w, forward_to_sibling(gather_wait(ms, mr, mbufs, h1, "mamba"))))
    n_in = weights["ssm_w_in"].shape[-1] * N_CHIPS
    di = weights["ssm_w_out"].shape[1] * N_CHIPS
    nheads = di // SSM_HEAD_DIM
    conv_dim = n_in - di - nheads
    w_in_full = jnp.moveaxis(gathered["ssm_w_in"].reshape(N_CHIPS, d, n_in // N_CHIPS), 0, 1).reshape(d, n_in)
    hpg = nheads // SSM_GROUPS

    def spread_heads(v):
        lead = v.shape[:-1]
        v = v.reshape(lead + (SSM_GROUPS, hpg))
        v = jnp.pad(v, [(0, 0)] * len(lead) + [(0, 0), (0, 128 - hpg)])
        return v.reshape(lead + (SSM_GROUPS * 128,))

    def gather_heads(v):
        lead = v.shape[:-1]
        return v.reshape(lead + (SSM_GROUPS, 128))[..., :hpg].reshape(lead + (nheads,))

    dt_col0 = di + conv_dim
    n_zx = dt_col0 + SSM_GROUPS * 128
    w_in = jnp.concatenate([w_in_full[:, :dt_col0], spread_heads(w_in_full[:, dt_col0:])], axis=1)
    w_out = gathered["ssm_w_out"].reshape(di, d)
    nkv = weights["w_kv"].shape[1] // (2 * ATT_HEAD_DIM)
    assert nkv == 2
    nh = weights["w_q"].shape[-1] // ATT_HEAD_DIM

    ssm_g = sp["ssm_norm"].reshape(1, d)
    cw = jnp.pad(sp["ssm_conv_w"].reshape(SSM_CONV, conv_dim), [(0, 8 - SSM_CONV), (0, 0)])
    cb = sp["ssm_conv_b"].reshape(1, conv_dim)
    gate_g = sp["ssm_gate_norm"].reshape(1, di)
    dt_bias = spread_heads(sp["ssm_dt_bias"].reshape(1, nheads))
    a_log = spread_heads(sp["ssm_a_log"].reshape(1, nheads))
    d_skip = spread_heads(sp["ssm_d"].reshape(1, nheads))
    kv_g = sp["kv_norm"].reshape(1, d)
    k_g = jnp.tile(sp["k_norm"].reshape(1, ATT_HEAD_DIM), (1, 2))
    attn_g = sp["attn_norm"].reshape(1, d)
    q_g = jnp.tile(sp["q_norm"].reshape(1, ATT_HEAD_DIM), (1, 2))
    sink_row = jnp.pad(sp["sinks"].reshape(1, nh), [(0, 0), (0, 128 - nh)])
    buckets = jnp.asarray(_t5_buckets())
    biasm = build_bias(sp["rel_bias"], buckets).reshape(nh * ATT_WINDOW, 2 * ATT_WINDOW)

    zx = norm_mm(h1, ssm_g, w_in)
    xc = conv_fwd(zx, cw, cb, di)
    y_ssd, states = ssd_fwd(xc, zx, dt_bias, a_log, d_skip, dt_col0)
    h2 = gate_out_fwd(h1, y_ssd, zx, gate_g, w_out)

    late = forward_to_sibling(gather_wait(ls, lr, lbufs, h2, "late"))
    ffn_rest = [late[0].reshape(N_CHIPS, 3, d, fs), late[1].reshape(N_CHIPS, 3, d, fs),
                late[2].reshape(N_CHIPS, 3, fs, d)]
    gathered.update(zip(late_w, late[3:]))
    wkv_heads = gathered["w_kv"].reshape(d, 2 * nkv, 1, ATT_HEAD_DIM)
    w_kvd = jnp.broadcast_to(wkv_heads, (d, 2 * nkv, 2, ATT_HEAD_DIM)).reshape(d, 4 * nkv * ATT_HEAD_DIM)
    wq = gathered["w_q"].reshape(d, -1)
    wo = gathered["w_o"].reshape(-1, d)

    def ffn_w(layer, idx):
        blk = 2 * layer + idx
        return (*ffn_first, 0) if blk == 0 else (*ffn_rest, blk - 1)

    h3, a01, b01 = ffn_fwd(h2, ffn_g[0, 1].reshape(1, d), *ffn_w(0, 1), no_dep)
    kvd = norm_mm(h3, kv_g, w_kvd)
    h4, a10, b10 = ffn_fwd(h3, ffn_g[1, 0].reshape(1, d), *ffn_w(1, 0), no_dep)
    qp = norm_mm(h4, attn_g, wq)
    h5 = attn_fwd(h4, qp, kvd, biasm, sink_row, q_g, k_g, wo)
    h6, a11, b11 = ffn_fwd(h5, ffn_g[1, 1].reshape(1, d), *ffn_w(1, 1), no_dep)
    loss_part, d6 = loss_head(h6, target)
    loss = lax.psum(loss_part[0, 0], ("x", "y", "c"))

    gfn = [[None, None], [None, None]]

    pending = []

    def reduce_start(pieces, tag):
        views = [g.reshape(N_CHIPS, 2, g.shape[1] // 2, g.shape[2]) for _, g in pieces]
        recv1 = exchange_sibling_halves(views)
        p32, p16 = zip(*[add_sibling(g, r, half_arr) for g, r in zip(views, recv1)])
        ss, rs, parts, lands, token = chip_partials_start(list(p16), tag)
        pending.append(([k for k, _ in pieces], p32, ss, rs, parts, lands, tag))
        return token

    def ffn_back(h_in, dy, a_s, b_s, layer, idx, dep):
        dh, u, da, db, s, dg = ffn_bwd(h_in, dy, ffn_g[layer, idx].reshape(1, d), a_s, b_s, *ffn_w(layer, idx), dep)
        gfn[layer][idx] = dg
        return dh, [(("ffn_w1", layer, idx), wgrad_grouped_b(u, da)), (("ffn_w3", layer, idx), wgrad_grouped_b(u, db)),
                    (("ffn_w2", layer, idx), wgrad_grouped_a(s, dy, 0.5))]

    d5, pieces = ffn_back(h5, d6, a11, b11, 1, 1, no_dep)
    tok = reduce_start(pieces, "ffn11")
    dqp, dkvd, o16, dbiasm, dsinks, dqg, dkg = attn_bwd(d5, qp, kvd, biasm, sink_row, q_g, k_g, wo, tok)
    g_wo = wgrad(o16, d5)
    d4, u_q, g_attn_norm = norm_mm_bwd(h4, attn_g, wq, dqp, d5, no_dep)
    g_wq = wgrad(u_q, dqp)
    d3a, pieces = ffn_back(h3, d4, a10, b10, 1, 0, no_dep)
    pieces += [(("w_o",), g_wo.reshape(N_CHIPS, -1, d)), (("w_q",), g_wq.reshape(N_CHIPS, d // N_CHIPS, -1))]
    tok = reduce_start(pieces, "ffn10")
    d3, u_kv, g_kv_norm = norm_mm_bwd(h3, kv_g, w_kvd, dkvd, d3a, tok, 0.5)
    g_wkvd = wgrad(u_kv, dkvd)
    g_wkv = g_wkvd.reshape(d, 2 * nkv, 2, ATT_HEAD_DIM)[:, :, 0, :].reshape(d, 2 * nkv * ATT_HEAD_DIM)
    d2, pieces = ffn_back(h2, d3, a01, b01, 0, 1, no_dep)
    pieces += [(("w_kv",), g_wkv.reshape(N_CHIPS, d // N_CHIPS, -1))]
    tok = reduce_start(pieces, "ffn01")
    dzx, dy_ssd, yn16, g_gate = gate_out_bwd(d2, y_ssd, zx, gate_g, w_out, n_zx, tok)
    g_wout = wgrad(yn16, d2)
    dzx, dxs, dbm, dcm, g_dtb, g_alog, g_dsk = ssd_bwd(dzx, dy_ssd, xc, zx, states, dt_bias, a_log, d_skip, dt_col0)
    dzx, g_cw, g_cb = conv_bwd(dzx, zx, dxs, dbm, dcm, cw, cb, di)
    d1, u_in, g_ssm_norm = norm_mm_bwd(h1, ssm_g, w_in, dzx, d2, no_dep)
    g_win = wgrad(u_in, dzx)
    g_win_full = jnp.concatenate([g_win[:, :dt_col0], gather_heads(g_win[:, dt_col0:])], axis=1)
    pieces = [(("ssm_w_in",), jnp.moveaxis(g_win_full.reshape(d, N_CHIPS, n_in // N_CHIPS), 1, 0)),
              (("ssm_w_out",), g_wout.reshape(N_CHIPS, di // N_CHIPS, d))]
    tok = reduce_start(pieces, "mamba")
    grad_x, pieces = ffn_back(h0, d1, a00, b00, 0, 0, tok)
    tok = reduce_start(pieces, "ffn00")
    g_relb = rel_bias_bwd(dbiasm.reshape(nh, ATT_WINDOW, 2 * ATT_WINDOW), buckets)

    reduced = {}
    for keys, p32, ss, rs, parts, lands, tag in pending:
        lands = chip_partials_wait(ss, rs, parts, lands, tok, tag)
        for k, p, r in zip(keys, p32, lands):
            reduced[k] = add_chip_partials(p, r, pos_arr)
    keys = list(reduced)
    shared = dict(zip(keys, share_with_sibling([reduced[k] for k in keys])))
    grads = {}
    for n in FFN_W:
        blocks = [shared[(n, l, i)].reshape(1, ffn_rows[n], -1) for l in range(2) for i in range(2)]
        grads[n] = jnp.concatenate(blocks, axis=0).reshape(weights[n].shape)
    for n in BIG[3:]:
        grads[n] = shared[(n,)].reshape(weights[n].shape)

    small_grads = {
        "ffn_norm": jnp.stack([jnp.stack([gfn[l][i].reshape(d) for i in range(2)]) for l in range(2)]),
        "ssm_norm": g_ssm_norm.reshape(1, d),
        "ssm_conv_w": g_cw[:SSM_CONV].reshape(1, SSM_CONV, conv_dim),
        "ssm_conv_b": g_cb.reshape(1, conv_dim),
        "ssm_gate_norm": g_gate.reshape(1, di),
        "ssm_dt_bias": gather_heads(g_dtb.reshape(1, -1)), "ssm_a_log": gather_heads(g_alog.reshape(1, -1)),
        "ssm_d": gather_heads(g_dsk.reshape(1, -1)),
        "kv_norm": g_kv_norm.reshape(d), "k_norm": dkg[0, :ATT_HEAD_DIM], "attn_norm": g_attn_norm.reshape(1, d),
        "q_norm": dqg[:, :ATT_HEAD_DIM], "sinks": dsinks[:, :nh], "rel_bias": g_relb[:, :nh],
    }
    flat = jnp.zeros((SMALL_ROWS * 128,), F32)
    for name, (off, fshape, _) in table.items():
        flat = lax.dynamic_update_slice(flat, small_grads[name].astype(F32).reshape(-1), (off,))
    small_sum = allreduce_small(flat.reshape(SMALL_ROWS, 128))
    for name, (off, fshape, sharded) in table.items():
        g = _take_small(small_sum, table, name)
        if sharded:
            w = weights[name].shape[-1]
            lead = int(np.prod(fshape[:-1]))
            g = lax.dynamic_slice(g.reshape(lead, fshape[-1]), (0, chip * w), (lead, w)).reshape(weights[name].shape)
        grads[name] = g.reshape(weights[name].shape)

    names = list(weights)
    deltas, new_m, new_v = {}, {}, {}
    small_names = [n for n, _ in SMALL]
    for n in BIG:
        shp = weights[n].shape
        v2 = lambda a: a.reshape(-1, shp[-1])
        dl, nm, nv = adamw(v2(weights[n]), v2(grads[n]), v2(m_in[n]), v2(v_in[n]))
        deltas[n], new_m[n], new_v[n] = dl.reshape(shp), nm.reshape(shp), nv.reshape(shp)
    sizes = [int(np.prod(weights[n].shape)) for n in small_names]
    tot = sum(sizes)
    rows = -(-tot // 128)
    rows = -(-rows // 8) * 8

    def pack(dct):
        flat = jnp.concatenate([dct[n].reshape(-1) for n in small_names])
        return jnp.pad(flat, (0, rows * 128 - tot), constant_values=1.0).reshape(rows, 128)

    dl, nm, nv = adamw(pack(weights), pack(grads), pack(m_in), pack(v_in))
    off = 0
    for n, sz in zip(small_names, sizes):
        shp = weights[n].shape
        take = lambda a: a.reshape(-1)[off:off + sz].reshape(shp)
        deltas[n], new_m[n], new_v[n] = take(dl), take(nm), take(nv)
        off += sz

    return (loss, grad_x[None], *[grads[n] for n in names], *[deltas[n] for n in names],
            *[new_m[n] for n in names], *[new_v[n] for n in names])
```

```python
import functools
import math

import jax
import jax.numpy as jnp
import numpy as np
from jax import lax
from jax.experimental import pallas as pl
from jax.experimental.pallas import tpu as pltpu

F32 = jnp.float32
BF16 = jnp.bfloat16
EPS = 1e-6
MESH = pl.DeviceIdType.MESH

SSM_HEAD_DIM = 64
SSM_GROUPS = 4
SSM_STATE = 128
SSM_CONV = 4
SSM_CHUNK = 256
ATT_HEAD_DIM = 64
ATT_WINDOW = 128
REL_BUCKETS = 32
N_CHIPS = 4

ADAM_LR = 0.001
ADAM_B1 = 0.9
ADAM_B2 = 0.999
ADAM_EPS = 1e-08
ADAM_WD = 0.01
ADAM_STEP = 10

VMEM_LIMIT_BYTES = 56 * 1024 * 1024
NEG = -1e30


DEP_SPEC = pl.BlockSpec(memory_space=pl.ANY)


def _params(*sem):
    return pltpu.CompilerParams(dimension_semantics=sem if sem else None, vmem_limit_bytes=VMEM_LIMIT_BYTES)


def _dot(a, b):
    return jnp.dot(a, b, preferred_element_type=F32)


def _dot_nt(a, b):
    return lax.dot_general(a, b, (((1,), (1,)), ((), ())), preferred_element_type=F32)


def _dot_tn(a, b):
    return lax.dot_general(a, b, (((0,), (0,)), ((), ())), preferred_element_type=F32)


def _b(x):
    return x.astype(BF16)


@jax.custom_vjp
def _bmm(a, b):
    return _dot(_b(a), _b(b))


def _bmm_fwd(a, b):
    return _bmm(a, b), (a, b)


def _bmm_bwd(res, g):
    a, b = res
    g16 = _b(g)
    return _dot_nt(g16, _b(b)).astype(a.dtype), _dot_tn(_b(a), g16).astype(b.dtype)


_bmm.defvjp(_bmm_fwd, _bmm_bwd)


@jax.custom_vjp
def _bmm_nt(a, b):
    return _dot_nt(_b(a), _b(b))


def _bmm_nt_fwd(a, b):
    return _bmm_nt(a, b), (a, b)


def _bmm_nt_bwd(res, g):
    a, b = res
    g16 = _b(g)
    return _dot(g16, _b(b)).astype(a.dtype), _dot_tn(g16, _b(a)).astype(b.dtype)


_bmm_nt.defvjp(_bmm_nt_fwd, _bmm_nt_bwd)


@jax.custom_vjp
def _bmm_tn(a, b):
    return _dot_tn(_b(a), _b(b))


def _bmm_tn_fwd(a, b):
    return _bmm_tn(a, b), (a, b)


def _bmm_tn_bwd(res, g):
    a, b = res
    g16 = _b(g)
    return _dot_nt(_b(b), g16).astype(a.dtype), _dot(_b(a), g16).astype(b.dtype)


_bmm_tn.defvjp(_bmm_tn_fwd, _bmm_tn_bwd)


def _split3(x):
    hi = _b(x)
    r = x - hi.astype(F32)
    mid = _b(r)
    lo = _b(r - mid.astype(F32))
    return hi, mid, lo


def _x_left_raw(m, x):
    hi, mid, lo = _split3(x)
    return _dot(m, hi) + _dot(m, mid) + _dot(m, lo)


def _x_left_t_raw(m, x):
    hi, mid, lo = _split3(x)
    return _dot_tn(m, hi) + _dot_tn(m, mid) + _dot_tn(m, lo)


def _x_right_raw(x, m):
    hi, mid, lo = _split3(x)
    return _dot(hi, m) + _dot(mid, m) + _dot(lo, m)


def _x_right_t_raw(x, m):
    hi, mid, lo = _split3(x)
    return _dot_nt(hi, m) + _dot_nt(mid, m) + _dot_nt(lo, m)


@jax.custom_vjp
def _xleft(m, x):
    return _x_left_raw(m, x)


_xleft.defvjp(lambda m, x: (_x_left_raw(m, x), m),
              lambda m, g: (jnp.zeros_like(m), _x_left_t_raw(m, g)))


@jax.custom_vjp
def _xright(x, m):
    return _x_right_raw(x, m)


_xright.defvjp(lambda x, m: (_x_right_raw(x, m), m),
               lambda m, g: (_x_right_t_raw(g, m), jnp.zeros_like(m)))


def _sigmoid(x):
    return 1.0 / (1.0 + jnp.exp(-x))


def _silu(x):
    return x * _sigmoid(x)


def _softplus(x):
    return jnp.maximum(x, 0.0) + jnp.log(1.0 + jnp.exp(-jnp.abs(x)))


def _rms(x):
    return x * lax.rsqrt(jnp.mean(x * x, axis=-1, keepdims=True) + EPS)


def _iota(shape, dim):
    return lax.broadcasted_iota(jnp.int32, shape, dim)


def _blockdiag64(n):
    return jnp.where(_iota((n, n), 0) // 64 == _iota((n, n), 1) // 64, 1.0, 0.0).astype(BF16)


def _group64_rms(x, seg_sum):
    ms = seg_sum(x * x) * (1.0 / 64.0)
    return x * lax.rsqrt(ms + EPS)


def _fold64(x):
    ax = x.ndim - 1
    w = x.shape[ax]
    lo = (_iota(x.shape, ax) % 128) < 64
    return x + jnp.where(lo, pltpu.roll(x, w - 64, ax), pltpu.roll(x, 64, ax))


def _tile(n, want):
    t = min(n, want)
    assert n % t == 0, (n, t)
    return t


def _lane_tile(n, cap=1536):
    if n <= cap:
        return n
    return max(w for w in range(128, cap + 1, 128) if n % w == 0)


def ffn_fwd(h, g, w1, w3, w2, blk, dep):
    t, d = h.shape
    nk, fs = w1.shape[0], w1.shape[-1]
    tm = _tile(t, 512)

    def body(h_ref, g_ref, w1_ref, w3_ref, w2_ref, dep_ref, o_ref, a_ref, b_ref, u_scr, acc):
        k = pl.program_id(1)

        @pl.when(k == 0)
        def _():
            u_scr[...] = _b(_rms(h_ref[...]) * g_ref[...])
            acc[...] = jnp.zeros_like(acc)

        u = u_scr[...]
        a = _dot(u, w1_ref[...])
        b = _dot(u, w3_ref[...])
        a_ref[...] = _b(a)
        b_ref[...] = _b(b)
        acc[...] += _dot(_b(_silu(a) * b), w2_ref[...])

        @pl.when(k == nk - 1)
        def _():
            o_ref[...] = h_ref[...] + 0.5 * acc[...]

    wspec = lambda r, c: pl.BlockSpec((None, None, r, c), lambda i, k: (k, blk, 0, 0))
    return pl.pallas_call(
        body, name="ffn_fwd",
        grid=(t // tm, nk),
        in_specs=[pl.BlockSpec((tm, d), lambda i, k: (i, 0)), pl.BlockSpec((1, d), lambda i, k: (0, 0)),
                  wspec(d, fs), wspec(d, fs), wspec(fs, d), DEP_SPEC],
        out_specs=[pl.BlockSpec((tm, d), lambda i, k: (i, 0)),
                   pl.BlockSpec((None, tm, fs), lambda i, k: (k, i, 0)),
                   pl.BlockSpec((None, tm, fs), lambda i, k: (k, i, 0))],
        out_shape=[jax.ShapeDtypeStruct((t, d), F32), jax.ShapeDtypeStruct((nk, t, fs), BF16),
                   jax.ShapeDtypeStruct((nk, t, fs), BF16)],
        scratch_shapes=[pltpu.VMEM((tm, d), BF16), pltpu.VMEM((tm, d), F32)],
        compiler_params=_params("arbitrary", "arbitrary"),
    )(h, g, w1, w3, w2, dep)


def ffn_bwd(h, dy, g, a_s, b_s, w1, w3, w2, blk, dep):
    t, d = h.shape
    nk, fs = w1.shape[0], w1.shape[-1]
    tm = _tile(t, 512)

    def body(h_ref, dy_ref, g_ref, a_ref, b_ref, w1_ref, w3_ref, w2_ref, dep_ref,
             dh_ref, u_ref, da_ref, db_ref, s_ref, dg_ref, dyh_scr, du_acc, da0, db0, da1, db1):
        i, k = pl.program_id(0), pl.program_id(1)

        @pl.when(k == 0)
        def _():
            dyh_scr[...] = _b(0.5 * dy_ref[...])
            du_acc[...] = jnp.zeros_like(du_acc)

        @pl.when((k == 0) & (i == 0))
        def _():
            dg_ref[...] = jnp.zeros_like(dg_ref)

        def step(prev, cur):
            if prev is not None:
                du_acc[...] += _dot_nt(prev[0][...], w1_ref[...]) + _dot_nt(prev[1][...], w3_ref[...])
            if cur is not None:
                ds = _dot_nt(dyh_scr[...], w2_ref[...])
                a = a_ref[...].astype(F32)
                b = b_ref[...].astype(F32)
                sig = _sigmoid(a)
                sl = a * sig
                s_ref[...] = _b(sl * b)
                da = _b(ds * b * (sig * (1.0 + a * (1.0 - sig))))
                db = _b(ds * sl)
                da_ref[...] = da
                db_ref[...] = db
                cur[0][...] = da
                cur[1][...] = db

        even, odd = (da0, db0), (da1, db1)

        @pl.when(k == 0)
        def _():
            step(None, even)

        @pl.when((k > 0) & (k < nk) & (k % 2 == 1))
        def _():
            step(even, odd)

        @pl.when((k > 0) & (k < nk) & (k % 2 == 0))
        def _():
            step(odd, even)

        @pl.when(k == nk)
        def _():
            step(odd if nk % 2 == 0 else even, None)
            hh = h_ref[...]
            rstd = lax.rsqrt(jnp.mean(hh * hh, axis=-1, keepdims=True) + EPS)
            xh = hh * rstd
            gg = g_ref[...]
            u_ref[...] = _b(xh * gg)
            du = du_acc[...]
            dg_ref[...] += jnp.sum(du * xh, axis=0, keepdims=True)
            dxh = du * gg
            dh_ref[...] = dy_ref[...] + rstd * (dxh - xh * jnp.mean(dxh * xh, axis=-1, keepdims=True))

    cur = lambda k: jnp.minimum(k, nk - 1)
    prv = lambda k: jnp.maximum(k - 1, 0)
    wcur = lambda r, c: pl.BlockSpec((None, None, r, c), lambda i, k: (cur(k), blk, 0, 0))
    wprv = lambda r, c: pl.BlockSpec((None, None, r, c), lambda i, k: (prv(k), blk, 0, 0))
    tok = pl.BlockSpec((tm, d), lambda i, k: (i, 0))
    hid = pl.BlockSpec((None, tm, fs), lambda i, k: (cur(k), i, 0))
    return pl.pallas_call(
        body, name="ffn_bwd",
        grid=(t // tm, nk + 1),
        in_specs=[tok, tok, pl.BlockSpec((1, d), lambda i, k: (0, 0)), hid, hid, wprv(d, fs), wprv(d, fs), wcur(fs, d),
                  DEP_SPEC],
        out_specs=[tok, tok, hid, hid, hid, pl.BlockSpec((1, d), lambda i, k: (0, 0))],
        out_shape=[jax.ShapeDtypeStruct((t, d), F32), jax.ShapeDtypeStruct((t, d), BF16),
                   jax.ShapeDtypeStruct((nk, t, fs), BF16), jax.ShapeDtypeStruct((nk, t, fs), BF16),
                   jax.ShapeDtypeStruct((nk, t, fs), BF16), jax.ShapeDtypeStruct((1, d), F32)],
        scratch_shapes=[pltpu.VMEM((tm, d), BF16), pltpu.VMEM((tm, d), F32)] + [pltpu.VMEM((tm, fs), BF16)] * 4,
        compiler_params=_params("arbitrary", "arbitrary"),
    )(h, dy, g, a_s, b_s, w1, w3, w2, dep)


def wgrad_grouped_b(a, bs, scale=1.0):
    t, m = a.shape
    ng, _, n = bs.shape
    tk = _tile(t, 2048)

    def body(a_ref, b_ref, o_ref):
        j = pl.program_id(1)

        @pl.when(j == 0)
        def _():
            o_ref[...] = jnp.zeros_like(o_ref)

        o_ref[...] += _dot_tn(_b(a_ref[...]), _b(b_ref[...]))

        if scale != 1.0:
            @pl.when(j == pl.num_programs(1) - 1)
            def _():
                o_ref[...] = o_ref[...] * scale

    return pl.pallas_call(
        body, name="wgrad_gb",
        grid=(ng, t // tk),
        in_specs=[pl.BlockSpec((tk, m), lambda k, j: (j, 0)), pl.BlockSpec((None, tk, n), lambda k, j: (k, j, 0))],
        out_specs=pl.BlockSpec((None, m, n), lambda k, j: (k, 0, 0)),
        out_shape=jax.ShapeDtypeStruct((ng, m, n), F32),
        compiler_params=_params("arbitrary", "arbitrary"),
    )(a, bs)


def wgrad_grouped_a(as_, b, scale=1.0):
    ng, t, m = as_.shape
    n = b.shape[1]
    tk = _tile(t, 2048)

    def body(a_ref, b_ref, o_ref):
        j = pl.program_id(1)

        @pl.when(j == 0)
        def _():
            o_ref[...] = jnp.zeros_like(o_ref)

        o_ref[...] += _dot_tn(_b(a_ref[...]), _b(b_ref[...]))

        if scale != 1.0:
            @pl.when(j == pl.num_programs(1) - 1)
            def _():
                o_ref[...] = o_ref[...] * scale

    return pl.pallas_call(
        body, name="wgrad_ga",
        grid=(ng, t // tk),
        in_specs=[pl.BlockSpec((None, tk, m), lambda k, j: (k, j, 0)), pl.BlockSpec((tk, n), lambda k, j: (j, 0))],
        out_specs=pl.BlockSpec((None, m, n), lambda k, j: (k, 0, 0)),
        out_shape=jax.ShapeDtypeStruct((ng, m, n), F32),
        compiler_params=_params("arbitrary", "arbitrary"),
    )(as_, b)


def wgrad(a, b):
    t, m = a.shape
    n = b.shape[1]
    tk = _tile(t, 1024)
    tn = _lane_tile(n, 1536 if m <= 1024 else 512)

    def body(a_ref, b_ref, o_ref):
        @pl.when(pl.program_id(1) == 0)
        def _():
            o_ref[...] = jnp.zeros_like(o_ref)

        o_ref[...] += _dot_tn(_b(a_ref[...]), _b(b_ref[...]))

    return pl.pallas_call(
        body, name="wgrad",
        grid=(n // tn, t // tk),
        in_specs=[pl.BlockSpec((tk, m), lambda c, j: (j, 0)), pl.BlockSpec((tk, tn), lambda c, j: (j, c))],
        out_specs=pl.BlockSpec((m, tn), lambda c, j: (0, c)),
        out_shape=jax.ShapeDtypeStruct((m, n), F32),
        compiler_params=_params("arbitrary", "arbitrary"),
    )(a, b)


def norm_mm(h, g, w):
    t, d = h.shape
    n = w.shape[1]
    tm = _tile(t, 512)
    tn = _lane_tile(n)

    def body(h_ref, g_ref, w_ref, o_ref, u_scr):
        @pl.when(pl.program_id(1) == 0)
        def _():
            u_scr[...] = _b(_rms(h_ref[...]) * g_ref[...])

        o_ref[...] = _dot(u_scr[...], w_ref[...])

    return pl.pallas_call(
        body, name="norm_mm",
        grid=(t // tm, n // tn),
        in_specs=[pl.BlockSpec((tm, d), lambda i, j: (i, 0)), pl.BlockSpec((1, d), lambda i, j: (0, 0)),
                  pl.BlockSpec((d, tn), lambda i, j: (0, j))],
        out_specs=pl.BlockSpec((tm, tn), lambda i, j: (i, j)),
        out_shape=jax.ShapeDtypeStruct((t, n), F32),
        scratch_shapes=[pltpu.VMEM((tm, d), BF16)],
        compiler_params=_params("arbitrary", "arbitrary"),
    )(h, g, w)


def norm_mm_bwd(h, g, w, dout, dres, dep, scale=1.0):
    t, d = h.shape
    n = w.shape[1]
    tm = _tile(t, 512)
    tn = _lane_tile(n)
    nj = n // tn

    def body(h_ref, g_ref, w_ref, do_ref, dr_ref, dep_ref, dh_ref, u_ref, dg_ref, du_acc):
        i, j = pl.program_id(0), pl.program_id(1)

        @pl.when(j == 0)
        def _():
            du_acc[...] = jnp.zeros_like(du_acc)

        @pl.when((j == 0) & (i == 0))
        def _():
            dg_ref[...] = jnp.zeros_like(dg_ref)

        du_acc[...] += _dot_nt(_b(do_ref[...]), w_ref[...])

        @pl.when(j == nj - 1)
        def _():
            hh = h_ref[...]
            rstd = lax.rsqrt(jnp.mean(hh * hh, axis=-1, keepdims=True) + EPS)
            xh = hh * rstd
            gg = g_ref[...]
            u_ref[...] = _b(xh * gg)
            du = du_acc[...] * scale
            dg_ref[...] += jnp.sum(du * xh, axis=0, keepdims=True)
            dxh = du * gg
            dh_ref[...] = dr_ref[...] + rstd * (dxh - xh * jnp.mean(dxh * xh, axis=-1, keepdims=True))

    tok = pl.BlockSpec((tm, d), lambda i, j: (i, 0))
    return pl.pallas_call(
        body, name="norm_mm_bwd",
        grid=(t // tm, nj),
        in_specs=[tok, pl.BlockSpec((1, d), lambda i, j: (0, 0)), pl.BlockSpec((d, tn), lambda i, j: (0, j)),
                  pl.BlockSpec((tm, tn), lambda i, j: (i, j)), tok, DEP_SPEC],
        out_specs=[tok, tok, pl.BlockSpec((1, d), lambda i, j: (0, 0))],
        out_shape=[jax.ShapeDtypeStruct((t, d), F32), jax.ShapeDtypeStruct((t, d), BF16),
                   jax.ShapeDtypeStruct((1, d), F32)],
        scratch_shapes=[pltpu.VMEM((tm, d), F32)],
        compiler_params=_params("arbitrary", "arbitrary"),
    )(h, g, w, dout, dres, dep)


CONV_COLS = 512


CONV_ROWS = 64


def _conv_pre(ext, w, b, r0, n):
    return (b + w[0:1] * ext[pl.ds(5 + r0, n), :] + w[1:2] * ext[pl.ds(6 + r0, n), :]
            + w[2:3] * ext[pl.ds(7 + r0, n), :] + w[3:4] * ext[pl.ds(8 + r0, n), :])


def conv_fwd(zx, cw, cb, col0):
    t = zx.shape[0]
    c = cw.shape[1]
    tm = _tile(t, 512)
    cb0 = col0 // CONV_COLS

    rc = _tile(tm, CONV_ROWS)

    def body(x_ref, w_ref, b_ref, o_ref, ext):
        @pl.when(pl.program_id(1) == 0)
        def _():
            ext[0:8, :] = jnp.zeros((8, CONV_COLS), F32)

        ext[8:, :] = x_ref[...]
        w, b = w_ref[...], b_ref[...]
        for r0 in range(0, tm, rc):
            o_ref[r0:r0 + rc, :] = _silu(_conv_pre(ext, w, b, r0, rc))
        ext[0:8, :] = ext[tm:tm + 8, :]

    return pl.pallas_call(
        body, name="conv_fwd",
        grid=(c // CONV_COLS, t // tm),
        in_specs=[pl.BlockSpec((tm, CONV_COLS), lambda j, i: (i, cb0 + j)),
                  pl.BlockSpec((8, CONV_COLS), lambda j, i: (0, j)), pl.BlockSpec((1, CONV_COLS), lambda j, i: (0, j))],
        out_specs=pl.BlockSpec((tm, CONV_COLS), lambda j, i: (i, j)),
        out_shape=jax.ShapeDtypeStruct((t, c), F32),
        scratch_shapes=[pltpu.VMEM((tm + 8, CONV_COLS), F32)],
        compiler_params=_params("arbitrary", "arbitrary"),
    )(zx, cw, cb)


def conv_bwd(dzx, zx, dxs, dbm, dcm, cw, cb, col0):
    t = zx.shape[0]
    c = cw.shape[1]
    tm = _tile(t, 512)
    nt = t // tm
    cb0 = col0 // CONV_COLS
    nxs = dxs.shape[1] // CONV_COLS
    hb = tm // 8

    rc = _tile(tm, CONV_ROWS)

    def body(dzx_ref, x_ref, xh_ref, dxs_ref, db_ref, dc_ref, w_ref, b_ref, o_ref, dw_ref, dbias_ref, ext, gy):
        j, i = pl.program_id(0), pl.program_id(1)
        ri = nt - 1 - i

        @pl.when(i == 0)
        def _():
            gy[tm:tm + 8, :] = jnp.zeros((8, CONV_COLS), F32)
            dw_ref[...] = jnp.zeros_like(dw_ref)
            dbias_ref[...] = jnp.zeros_like(dbias_ref)

        ext[0:8, :] = jnp.where(ri > 0, xh_ref[...], 0.0)
        ext[8:, :] = x_ref[...]
        w, b = w_ref[...], b_ref[...]
        dw = [jnp.zeros((1, CONV_COLS), F32) for _ in range(SSM_CONV)]
        dbias = jnp.zeros((1, CONV_COLS), F32)
        for r0 in range(0, tm, rc):
            rows = pl.ds(r0, rc)
            win = [ext[pl.ds(5 + tap + r0, rc), :] for tap in range(SSM_CONV)]
            y = b + w[0:1] * win[0] + w[1:2] * win[1] + w[2:3] * win[2] + w[3:4] * win[3]
            sig = _sigmoid(y)
            dout = jnp.where(j < nxs, dxs_ref[rows, :], jnp.where(j == nxs, db_ref[rows, :], dc_ref[rows, :]))
            g = dout * (sig * (1.0 + y * (1.0 - sig)))
            gy[rows, :] = g
            dbias = dbias + jnp.sum(g, axis=0, keepdims=True)
            for tap in range(SSM_CONV):
                dw[tap] = dw[tap] + jnp.sum(g * win[tap], axis=0, keepdims=True)
        for r0 in range(0, tm, rc):
            o_ref[r0:r0 + rc, :] = (w[0:1] * gy[pl.ds(r0 + 3, rc), :] + w[1:2] * gy[pl.ds(r0 + 2, rc), :]
                                    + w[2:3] * gy[pl.ds(r0 + 1, rc), :] + w[3:4] * gy[pl.ds(r0, rc), :])
        gy[tm:tm + 8, :] = gy[0:8, :]
        for tap in range(SSM_CONV):
            dw_ref[tap:tap + 1, :] += dw[tap]
        dbias_ref[...] += dbias

    return pl.pallas_call(
        body, name="conv_bwd",
        grid=(c // CONV_COLS, nt),
        in_specs=[pl.BlockSpec(memory_space=pl.ANY),
                  pl.BlockSpec((tm, CONV_COLS), lambda j, i: (nt - 1 - i, cb0 + j)),
                  pl.BlockSpec((8, CONV_COLS), lambda j, i: (jnp.maximum((nt - 1 - i) * hb - 1, 0), cb0 + j)),
                  pl.BlockSpec((tm, CONV_COLS), lambda j, i: (nt - 1 - i, jnp.minimum(j, nxs - 1))),
                  pl.BlockSpec((tm, CONV_COLS), lambda j, i: (nt - 1 - i, 0)),
                  pl.BlockSpec((tm, CONV_COLS), lambda j, i: (nt - 1 - i, 0)),
                  pl.BlockSpec((8, CONV_COLS), lambda j, i: (0, j)), pl.BlockSpec((1, CONV_COLS), lambda j, i: (0, j))],
        out_specs=[pl.BlockSpec((tm, CONV_COLS), lambda j, i: (nt - 1 - i, cb0 + j)),
                   pl.BlockSpec((8, CONV_COLS), lambda j, i: (0, j)), pl.BlockSpec((1, CONV_COLS), lambda j, i: (0, j))],
        out_shape=[jax.ShapeDtypeStruct(dzx.shape, F32), jax.ShapeDtypeStruct((8, c), F32),
                   jax.ShapeDtypeStruct((1, c), F32)],
        scratch_shapes=[pltpu.VMEM((tm + 8, CONV_COLS), F32), pltpu.VMEM((tm + 8, CONV_COLS), F32)],
        input_output_aliases={0: 0},
        compiler_params=_params("arbitrary", "arbitrary"),
    )(dzx, zx, zx, dxs, dbm, dcm, cw, cb)


def _ssd_group(xs, bg, cg, dtraw, s0, bias, alog, dsk):
    L = xs.shape[0]
    causal = _iota((L, L), 0) >= _iota((L, L), 1)
    tril = jnp.where(causal, 1.0, 0.0).astype(BF16)
    dt = _softplus(dtraw + bias)
    a = -jnp.exp(alog)
    acum = _xleft(tril, dt * a)
    acum_t = acum.T
    dt_t = dt.T
    cb = _bmm_nt(cg, bg)
    lo = _iota((L, 128), 1) < 64
    lo_row = _iota((1, 128), 1) < 64
    lo_col = _iota((128, 1), 0) < 64
    alast = acum[L - 1:L, :]
    ys, s1s = [], []
    for q in range(4):
        xp = xs[:, q * 128:(q + 1) * 128]
        sp = s0[q * 128:(q + 1) * 128, :]
        yd, ec, wc, el = [], [], [], []
        for j in range(2):
            r = 2 * q + j
            ac = acum[:, r:r + 1]
            decay = jnp.exp(jnp.where(causal, ac - acum_t[r:r + 1, :], NEG))
            yd.append(_bmm(cb * decay * dt_t[r:r + 1, :], xp))
            ec.append(jnp.exp(ac))
            al = alast[:, r:r + 1]
            wc.append(jnp.exp(al - ac) * dt[:, r:r + 1])
            el.append(jnp.exp(al))
        y_off = _bmm_nt(cg, sp) * jnp.where(lo, ec[0], ec[1])
        dsel = jnp.where(lo_row, dsk[:, 2 * q:2 * q + 1], dsk[:, 2 * q + 1:2 * q + 2])
        ys.append(jnp.where(lo, yd[0], yd[1]) + y_off + dsel * xp)
        xw = xp * jnp.where(lo, wc[0], wc[1])
        s1s.append(sp * jnp.where(lo_col, el[0], el[1]) + _bmm_tn(xw, bg))
    return jnp.concatenate(ys, axis=1), jnp.concatenate(s1s, axis=0)


def ssd_fwd(xc, zx, bias, alog, dsk, dt_col0):
    t = xc.shape[0]
    L = _tile(t, SSM_CHUNK)
    nc = t // L
    g = SSM_GROUPS
    dtb = dt_col0 // 128

    def body(xs_ref, b_ref, c_ref, dt_ref, bias_ref, alog_ref, dsk_ref, y_ref, st_ref, state):
        c, gi = pl.program_id(0), pl.program_id(1)

        @pl.when(c == 0)
        def _():
            state[gi] = jnp.zeros((512, 128), F32)

        s0 = state[gi]
        st_ref[...] = s0
        y, s1 = _ssd_group(xs_ref[...], b_ref[...], c_ref[...], dt_ref[...], s0,
                           bias_ref[...], alog_ref[...], dsk_ref[...])
        y_ref[...] = y
        state[gi] = s1

    vec = pl.BlockSpec((1, 128), lambda c, gi: (0, gi))
    return pl.pallas_call(
        body, name="ssd_fwd",
        grid=(nc, g),
        in_specs=[pl.BlockSpec((L, 512), lambda c, gi: (c, gi)),
                  pl.BlockSpec((L, 128), lambda c, gi: (c, 16 + gi)),
                  pl.BlockSpec((L, 128), lambda c, gi: (c, 20 + gi)),
                  pl.BlockSpec((L, 128), lambda c, gi: (c, dtb + gi)), vec, vec, vec],
        out_specs=[pl.BlockSpec((L, 512), lambda c, gi: (c, gi)),
                   pl.BlockSpec((None, None, 512, 128), lambda c, gi: (c, gi, 0, 0))],
        out_shape=[jax.ShapeDtypeStruct((t, 2048), F32), jax.ShapeDtypeStruct((nc, g, 512, 128), F32)],
        scratch_shapes=[pltpu.VMEM((g, 512, 128), F32)],
        compiler_params=_params("arbitrary", "arbitrary"),
    )(xc, xc, xc, zx, bias, alog, dsk)


def ssd_bwd(dzx, dy, xc, zx, states, bias, alog, dsk, dt_col0):
    t = xc.shape[0]
    L = _tile(t, SSM_CHUNK)
    nc = t // L
    g = SSM_GROUPS
    dtb = dt_col0 // 128

    def body(dzx_ref, dy_ref, xs_ref, b_ref, c_ref, dt_ref, st_ref, bias_ref, alog_ref, dsk_ref,
             ddt_ref, dxs_ref, db_ref, dc_ref, dbias_ref, dalog_ref, ddsk_ref, dstate):
        c, gi = pl.program_id(0), pl.program_id(1)

        @pl.when(c == 0)
        def _():
            dstate[gi] = jnp.zeros((512, 128), F32)

        @pl.when((c == 0) & (gi == 0))
        def _():
            dbias_ref[...] = jnp.zeros_like(dbias_ref)
            dalog_ref[...] = jnp.zeros_like(dalog_ref)
            ddsk_ref[...] = jnp.zeros_like(ddsk_ref)

        _, vjp = jax.vjp(_ssd_group, xs_ref[...], b_ref[...], c_ref[...], dt_ref[...], st_ref[...],
                         bias_ref[...], alog_ref[...], dsk_ref[...])
        dxs, db, dc, ddt, ds0, dbias, dalog, ddsk = vjp((dy_ref[...], dstate[gi]))
        dxs_ref[...] = dxs
        db_ref[...] = db
        dc_ref[...] = dc
        ddt_ref[...] = ddt
        dstate[gi] = ds0
        dbias_ref[gi] += dbias
        dalog_ref[gi] += dalog
        ddsk_ref[gi] += ddsk

    rc = lambda c: nc - 1 - c
    vec = pl.BlockSpec((1, 128), lambda c, gi: (0, gi))
    acc = pl.BlockSpec((g, 1, 128), lambda c, gi: (0, 0, 0))
    return pl.pallas_call(
        body, name="ssd_bwd",
        grid=(nc, g),
        in_specs=[pl.BlockSpec(memory_space=pl.ANY),
                  pl.BlockSpec((L, 512), lambda c, gi: (rc(c), gi)),
                  pl.BlockSpec((L, 512), lambda c, gi: (rc(c), gi)),
                  pl.BlockSpec((L, 128), lambda c, gi: (rc(c), 16 + gi)),
                  pl.BlockSpec((L, 128), lambda c, gi: (rc(c), 20 + gi)),
                  pl.BlockSpec((L, 128), lambda c, gi: (rc(c), dtb + gi)),
                  pl.BlockSpec((None, None, 512, 128), lambda c, gi: (rc(c), gi, 0, 0)), vec, vec, vec],
        out_specs=[pl.BlockSpec((L, 128), lambda c, gi: (rc(c), dtb + gi)),
                   pl.BlockSpec((L, 512), lambda c, gi: (rc(c), gi)),
                   pl.BlockSpec((L, 128), lambda c, gi: (rc(c), gi)),
                   pl.BlockSpec((L, 128), lambda c, gi: (rc(c), gi)), acc, acc, acc],
        out_shape=[jax.ShapeDtypeStruct(dzx.shape, F32), jax.ShapeDtypeStruct((t, 2048), F32),
                   jax.ShapeDtypeStruct((t, 512), F32), jax.ShapeDtypeStruct((t, 512), F32),
                   jax.ShapeDtypeStruct((g, 1, 128), F32), jax.ShapeDtypeStruct((g, 1, 128), F32),
                   jax.ShapeDtypeStruct((g, 1, 128), F32)],
        scratch_shapes=[pltpu.VMEM((g, 512, 128), F32)],
        input_output_aliases={0: 0},
        compiler_params=_params("arbitrary", "arbitrary"),
    )(dzx, dy, xc, xc, xc, zx, states, bias, alog, dsk)


def _gate_tile(y, z, gn):
    gated = y * _silu(z)
    parts = [_rms(gated[:, k * 512:(k + 1) * 512]) for k in range(SSM_GROUPS)]
    return jnp.concatenate(parts, axis=1) * gn


def gate_out_fwd(h, y, zx, gn, w_out):
    t, d = h.shape
    di = y.shape[1]
    tm = _tile(t, 256)

    def body(h_ref, y_ref, z_ref, gn_ref, w_ref, o_ref):
        yn = _gate_tile(y_ref[...], z_ref[...], gn_ref[...])
        o_ref[...] = h_ref[...] + _dot(_b(yn), w_ref[...])

    return pl.pallas_call(
        body, name="gate_out_fwd",
        grid=(t // tm,),
        in_specs=[pl.BlockSpec((tm, d), lambda i: (i, 0)), pl.BlockSpec((tm, di), lambda i: (i, 0)),
                  pl.BlockSpec((tm, di), lambda i: (i, 0)), pl.BlockSpec((1, di), lambda i: (0, 0)),
                  pl.BlockSpec((di, d), lambda i: (0, 0))],
        out_specs=pl.BlockSpec((tm, d), lambda i: (i, 0)),
        out_shape=jax.ShapeDtypeStruct((t, d), F32),
        compiler_params=_params("arbitrary"),
    )(h, y, zx, gn, w_out)


def gate_out_bwd(dy, y, zx, gn, w_out, n_zx, dep):
    t, d = dy.shape
    di = y.shape[1]
    tm = _tile(t, 256)

    def body(dy_ref, y_ref, z_ref, gn_ref, w_ref, dep_ref, dz_ref, dys_ref, yn_ref, dgn_ref):
        @pl.when(pl.program_id(0) == 0)
        def _():
            dgn_ref[...] = jnp.zeros_like(dgn_ref)

        yn, vjp = jax.vjp(_gate_tile, y_ref[...], z_ref[...], gn_ref[...])
        dyn = _dot_nt(_b(dy_ref[...]), w_ref[...])
        dys, dz, dgn = vjp(dyn)
        yn_ref[...] = _b(yn)
        dys_ref[...] = dys
        dz_ref[...] = dz
        dgn_ref[...] += dgn

    return pl.pallas_call(
        body, name="gate_out_bwd",
        grid=(t // tm,),
        in_specs=[pl.BlockSpec((tm, d), lambda i: (i, 0)), pl.BlockSpec((tm, di), lambda i: (i, 0)),
                  pl.BlockSpec((tm, di), lambda i: (i, 0)), pl.BlockSpec((1, di), lambda i: (0, 0)),
                  pl.BlockSpec((di, d), lambda i: (0, 0)), DEP_SPEC],
        out_specs=[pl.BlockSpec((tm, di), lambda i: (i, 0)), pl.BlockSpec((tm, di), lambda i: (i, 0)),
                   pl.BlockSpec((tm, di), lambda i: (i, 0)), pl.BlockSpec((1, di), lambda i: (0, 0))],
        out_shape=[jax.ShapeDtypeStruct((t, n_zx), F32), jax.ShapeDtypeStruct((t, di), F32),
                   jax.ShapeDtypeStruct((t, di), BF16), jax.ShapeDtypeStruct((1, di), F32)],
        compiler_params=_params("arbitrary"),
    )(dy, y, zx, gn, w_out, dep)


def _attn_block(qp, kvp, kvc, biasm, sinks, qg, kg, w_o, first):
    nq = qp.shape[0]
    n_pairs = qp.shape[1] // 128
    hk = n_pairs
    rows = hk * nq
    seg = functools.partial(_xright, m=_blockdiag64(128))
    scale = ATT_HEAD_DIM ** -0.5
    qi = (_iota((rows, 2 * nq), 0) % nq) + nq
    kj = _iota((rows, 2 * nq), 1)
    dist = qi - kj
    valid = (dist >= 0) & (dist < ATT_WINDOW) & (jnp.logical_not(first) | (kj >= nq))
    lo = _iota((nq, 128), 1) < 64
    kv = jnp.concatenate([kvp, kvc], axis=0)
    outs = [None] * n_pairs
    for kvh in range(2):
        kn = _group64_rms(kv[:, kvh * 128:(kvh + 1) * 128], seg) * kg
        vv = kv[:, 256 + kvh * 128:256 + (kvh + 1) * 128]
        pairs = range(kvh * hk // 2, (kvh + 1) * hk // 2)
        qs, sk = [], []
        for p in pairs:
            qn = _group64_rms(qp[:, p * 128:(p + 1) * 128], seg) * qg
            qs += [jnp.where(lo, qn, 0.0), jnp.where(lo, 0.0, qn)]
            sk += [jnp.broadcast_to(sinks[:, h:h + 1], (nq, 1)) for h in (2 * p, 2 * p + 1)]
        sink = jnp.concatenate(sk, axis=0)
        s = _bmm_nt(jnp.concatenate(qs, axis=0), kn) * scale + biasm[kvh * rows:(kvh + 1) * rows]
        s = jnp.where(valid, s, NEG)
        m = lax.stop_gradient(jnp.maximum(jnp.max(s, axis=-1, keepdims=True), sink))
        pexp = jnp.exp(s - m)
        den = jnp.sum(pexp, axis=-1, keepdims=True) + jnp.exp(sink - m)
        o = _bmm(pexp * (1.0 / den), vv)
        for n, p in enumerate(pairs):
            outs[p] = jnp.where(lo, o[2 * n * nq:(2 * n + 1) * nq], o[(2 * n + 1) * nq:(2 * n + 2) * nq])
    o = jnp.concatenate(outs, axis=1)
    return _bmm(o, w_o), o


def attn_fwd(h, qp, kvd, biasm, sinks, qg, kg, w_o):
    t, d = h.shape
    nq = ATT_WINDOW
    nb = t // nq
    nh = qp.shape[1] // ATT_HEAD_DIM

    def body(h_ref, q_ref, kp_ref, kc_ref, bias_ref, s_ref, qg_ref, kg_ref, w_ref, o_ref):
        out, _ = _attn_block(q_ref[...], kp_ref[...], kc_ref[...], bias_ref[...], s_ref[...], qg_ref[...],
                             kg_ref[...], w_ref[...], pl.program_id(0) == 0)
        o_ref[...] = h_ref[...] + out

    vec = pl.BlockSpec((1, 128), lambda i: (0, 0))
    return pl.pallas_call(
        body, name="attn_fwd",
        grid=(nb,),
        in_specs=[pl.BlockSpec((nq, d), lambda i: (i, 0)), pl.BlockSpec((nq, nh * 64), lambda i: (i, 0)),
                  pl.BlockSpec((nq, 512), lambda i: (jnp.maximum(i - 1, 0), 0)),
                  pl.BlockSpec((nq, 512), lambda i: (i, 0)),
                  pl.BlockSpec((nh * nq, 2 * nq), lambda i: (0, 0)), vec, vec, vec,
                  pl.BlockSpec((nh * 64, d), lambda i: (0, 0))],
        out_specs=pl.BlockSpec((nq, d), lambda i: (i, 0)),
        out_shape=jax.ShapeDtypeStruct((t, d), F32),
        compiler_params=_params("arbitrary"),
    )(h, qp, kvd, kvd, biasm, sinks, qg, kg, w_o)


def attn_bwd(dy, qp, kvd, biasm, sinks, qg, kg, w_o, dep):
    t, d = dy.shape
    nq = ATT_WINDOW
    nb = t // nq
    nh = qp.shape[1] // ATT_HEAD_DIM

    def body(dy_ref, q_ref, kp_ref, kc_ref, bias_ref, s_ref, qg_ref, kg_ref, w_ref, dep_ref,
             dq_ref, dkv_ref, o_ref, dbias_ref, ds_ref, dqg_ref, dkg_ref, carry):
        i = pl.program_id(0)

        @pl.when(i == 0)
        def _():
            carry[...] = jnp.zeros_like(carry)
            dbias_ref[...] = jnp.zeros_like(dbias_ref)
            ds_ref[...] = jnp.zeros_like(ds_ref)
            dqg_ref[...] = jnp.zeros_like(dqg_ref)
            dkg_ref[...] = jnp.zeros_like(dkg_ref)

        @pl.when(i < nb)
        def _():
            fn = functools.partial(_attn_block, w_o=w_ref[...], first=(i == 0))
            (_, o), vjp = jax.vjp(fn, q_ref[...], kp_ref[...], kc_ref[...], bias_ref[...], s_ref[...],
                                  qg_ref[...], kg_ref[...])
            dq, dkp, dkc, dbias, dsk, dqg, dkg = vjp((dy_ref[...], jnp.zeros((nq, nh * 64), F32)))
            dq_ref[...] = dq
            o_ref[...] = _b(o)
            dkv_ref[...] = _fold64(carry[...] + dkp)
            carry[...] = dkc
            dbias_ref[...] += dbias
            ds_ref[...] += dsk
            dqg_ref[...] += _fold64(dqg)
            dkg_ref[...] += _fold64(dkg)

        @pl.when(i == nb)
        def _():
            dkv_ref[...] = _fold64(carry[...])

    cl = lambda i: jnp.minimum(i, nb - 1)
    vec = pl.BlockSpec((1, 128), lambda i: (0, 0))
    return pl.pallas_call(
        body, name="attn_bwd",
        grid=(nb + 1,),
        in_specs=[pl.BlockSpec((nq, d), lambda i: (cl(i), 0)), pl.BlockSpec((nq, nh * 64), lambda i: (cl(i), 0)),
                  pl.BlockSpec((nq, 512), lambda i: (jnp.maximum(cl(i) - 1, 0), 0)),
                  pl.BlockSpec((nq, 512), lambda i: (cl(i), 0)),
                  pl.BlockSpec((nh * nq, 2 * nq), lambda i: (0, 0)), vec, vec, vec,
                  pl.BlockSpec((nh * 64, d), lambda i: (0, 0)), DEP_SPEC],
        out_specs=[pl.BlockSpec((nq, nh * 64), lambda i: (cl(i), 0)),
                   pl.BlockSpec((nq, 512), lambda i: (jnp.maximum(i - 1, 0), 0)),
                   pl.BlockSpec((nq, nh * 64), lambda i: (cl(i), 0)),
                   pl.BlockSpec((nh * nq, 2 * nq), lambda i: (0, 0)), vec, vec, vec],
        out_shape=[jax.ShapeDtypeStruct((t, nh * 64), F32), jax.ShapeDtypeStruct((t, 512), F32),
                   jax.ShapeDtypeStruct((t, nh * 64), BF16), jax.ShapeDtypeStruct((nh * nq, 2 * nq), F32),
                   jax.ShapeDtypeStruct((1, 128), F32), jax.ShapeDtypeStruct((1, 128), F32),
                   jax.ShapeDtypeStruct((1, 128), F32)],
        scratch_shapes=[pltpu.VMEM((nq, 512), F32)],
        compiler_params=_params("arbitrary"),
    )(dy, qp, kvd, kvd, biasm, sinks, qg, kg, w_o, dep)


def _t5_buckets():
    nq = ATT_WINDOW
    dist = (np.arange(nq)[:, None] + nq) - np.arange(2 * nq)[None, :]
    n = np.maximum(dist, 0)
    max_exact = REL_BUCKETS // 2
    nf = np.maximum(n, 1).astype(np.float32)
    large = max_exact + (np.log(nf / max_exact) / math.log(ATT_WINDOW / max_exact)
                         * (REL_BUCKETS - max_exact)).astype(np.int32)
    large = np.minimum(large, REL_BUCKETS - 1)
    return np.where(n < max_exact, n, large).astype(np.int32)


def rel_bias_bwd(dbias, buckets):
    nh = dbias.shape[0]

    def body(db_ref, bk_ref, o_ref):
        bk = bk_ref[...]
        lane = _iota((1, 128), 1)
        row = _iota((REL_BUCKETS, 128), 0)
        acc = jnp.zeros((REL_BUCKETS, 128), F32)
        for h in range(nh):
            dbh = db_ref[h]
            for b in range(REL_BUCKETS):
                v = jnp.sum(jnp.where(bk == b, dbh, 0.0))
                acc = acc + jnp.where((row == b) & (lane == h), v, 0.0)
        o_ref[...] = acc

    return pl.pallas_call(
        body, name="rel_bias_bwd",
        out_shape=jax.ShapeDtypeStruct((REL_BUCKETS, 128), F32),
        compiler_params=_params(),
    )(dbias, buckets)


def loss_head(y, target):
    t, d = y.shape
    tm = _tile(t, 512)

    def body(y_ref, t_ref, l_ref, dy_ref):
        @pl.when(pl.program_id(0) == 0)
        def _():
            l_ref[...] = jnp.zeros_like(l_ref)

        e = y_ref[...] - t_ref[...]
        l_ref[...] += 0.5 * jnp.sum(jnp.mean(e * e, axis=-1, keepdims=True), axis=0, keepdims=True)
        dy_ref[...] = e * (1.0 / d)

    return pl.pallas_call(
        body, name="loss_head",
        grid=(t // tm,),
        in_specs=[pl.BlockSpec((tm, d), lambda i: (i, 0)), pl.BlockSpec((tm, d), lambda i: (i, 0))],
        out_specs=[pl.BlockSpec((1, 1), lambda i: (0, 0)), pl.BlockSpec((tm, d), lambda i: (i, 0))],
        out_shape=[jax.ShapeDtypeStruct((1, 1), F32), jax.ShapeDtypeStruct((t, d), F32)],
        compiler_params=_params("arbitrary"),
    )(y, target)


def adamw(w, g, m, v):
    r, c = w.shape
    tr = r if r <= 512 else _tile(r, 256)

    def body(w_ref, g_ref, m_ref, v_ref, d_ref, nm_ref, nv_ref):
        gg = g_ref[...]
        nm = ADAM_B1 * m_ref[...] + (1.0 - ADAM_B1) * gg
        nv = ADAM_B2 * v_ref[...] + (1.0 - ADAM_B2) * (gg * gg)
        m_hat = nm / (1.0 - ADAM_B1 ** ADAM_STEP)
        v_hat = nv / (1.0 - ADAM_B2 ** ADAM_STEP)
        d_ref[...] = -ADAM_LR * (m_hat / (jnp.sqrt(v_hat) + ADAM_EPS) + ADAM_WD * w_ref[...])
        nm_ref[...] = nm
        nv_ref[...] = nv

    spec = pl.BlockSpec((tr, c), lambda i: (i, 0))
    shp = jax.ShapeDtypeStruct((r, c), F32)
    return pl.pallas_call(
        body, name="adamw",
        grid=(r // tr,),
        in_specs=[spec] * 4, out_specs=[spec] * 3, out_shape=[shp] * 3,
        compiler_params=_params("arbitrary"),
    )(w, g, m, v)


def _my_pos():
    return lax.axis_index("x"), lax.axis_index("y"), lax.axis_index("c")


def _other_chips(x, y):
    return [(1 - x, y), (x, 1 - y), (1 - x, 1 - y)]


def _chip_id(x, y):
    return 2 * x + y


def gather_weights(bufs):
    n = len(bufs)

    def body(*refs):
        outs = refs[n:2 * n]
        send_sems, recv_sems = refs[2 * n:]
        x, y, c = _my_pos()
        sibling = (x, y, 1 - c)
        chips = _other_chips(x, y)

        def copy(p, k, chip, half, to):
            blk = outs[p].at[_chip_id(*chip), half]
            return pltpu.make_async_remote_copy(
                src_ref=blk, dst_ref=blk, send_sem=send_sems.at[p, k], recv_sem=recv_sems.at[p, k],
                device_id=to, device_id_type=MESH)

        first = [[copy(p, j, (x, y), c, (*chip, c)) for j, chip in enumerate(chips)] for p in range(n)]
        for p in range(n):
            for cp in first[p]:
                cp.start()
        passed = [[copy(p, 3 + j, chip, c, sibling) for j, chip in enumerate(chips)] for p in range(n)]
        for p in range(n):
            for j, chip in enumerate(chips):
                copy(p, j, chip, c, (x, y, c)).wait_recv()
                passed[p][j].start()
        for p in range(n):
            for j, chip in enumerate(chips):
                copy(p, 3 + j, chip, 1 - c, (x, y, c)).wait_recv()
        for p in range(n):
            for cp in first[p] + passed[p]:
                cp.wait_send()

    any_spec = pl.BlockSpec(memory_space=pl.ANY)
    return pl.pallas_call(
        body, name="gather_weights",
        in_specs=[any_spec] * n, out_specs=[any_spec] * n,
        out_shape=[jax.ShapeDtypeStruct(b.shape, b.dtype) for b in bufs],
        scratch_shapes=[pltpu.SemaphoreType.DMA((n, 6)), pltpu.SemaphoreType.DMA((n, 6))],
        input_output_aliases={p: p for p in range(n)},
    )(*bufs)


HBM_SPEC = pl.BlockSpec(memory_space=pltpu.HBM)
SEM_SPEC = pl.BlockSpec(memory_space=pltpu.SEMAPHORE)
DATAFLOW = pltpu.SideEffectType.DATAFLOW_SIDE_EFFECTING


def _in_hbm(a):
    return pltpu.with_memory_space_constraint(a, pltpu.HBM)


def _ici_gather_copy(buf, p, j, chip, c, to, send_sems, recv_sems):
    blk = buf.at[_chip_id(*chip), c]
    return pltpu.make_async_remote_copy(
        src_ref=blk, dst_ref=blk, send_sem=send_sems.at[3 * p + j], recv_sem=recv_sems.at[3 * p + j],
        device_id=to, device_id_type=MESH)


def gather_start(bufs, after, tag):
    n = len(bufs)

    def body(*refs):
        ins = refs[:n]
        send_sems, recv_sems = refs[n + 1], refs[n + 2]
        token = refs[2 * n + 3]
        x, y, c = _my_pos()
        for p in range(n):
            for j, chip in enumerate(_other_chips(x, y)):
                _ici_gather_copy(ins[p], p, j, (x, y), c, (*chip, c), send_sems, recv_sems).start()
        token[...] = jnp.zeros_like(token)

    out = pl.pallas_call(
        body, name="gather_start_" + tag,
        in_specs=[HBM_SPEC] * n + [DEP_SPEC],
        out_specs=(SEM_SPEC, SEM_SPEC, *([HBM_SPEC] * n), pl.BlockSpec(memory_space=pltpu.VMEM)),
        out_shape=(pltpu.SemaphoreType.DMA((3 * n,)), pltpu.SemaphoreType.DMA((3 * n,)),
                   *[pltpu.HBM(b.shape, b.dtype) for b in bufs], jax.ShapeDtypeStruct((8, 128), F32)),
        input_output_aliases={p: 2 + p for p in range(n)},
        compiler_params=pltpu.CompilerParams(has_side_effects=DATAFLOW),
    )(*[_in_hbm(b) for b in bufs], after)
    return out[0], out[1], list(out[2:2 + n]), out[2 + n]


def gather_wait(send_sems, recv_sems, bufs, after, tag):
    n = len(bufs)

    def body(*refs):
        ins = refs[:n]
        send_sems, recv_sems = refs[n], refs[n + 1]
        x, y, c = _my_pos()
        for p in range(n):
            for j, chip in enumerate(_other_chips(x, y)):
                _ici_gather_copy(ins[p], p, j, (x, y), c, (*chip, c), send_sems, recv_sems).wait_send()
                _ici_gather_copy(ins[p], p, j, chip, c, (x, y, c), send_sems, recv_sems).wait_recv()

    out = pl.pallas_call(
        body, name="gather_wait_" + tag,
        in_specs=[HBM_SPEC] * n + [SEM_SPEC, SEM_SPEC, pl.BlockSpec(memory_space=pl.ANY)],
        out_specs=[HBM_SPEC] * n,
        out_shape=[pltpu.HBM(b.shape, b.dtype) for b in bufs],
        input_output_aliases={p: p for p in range(n)},
        compiler_params=pltpu.CompilerParams(has_side_effects=DATAFLOW),
    )(*bufs, send_sems, recv_sems, after)
    return list(out)


def forward_to_sibling(bufs):
    n = len(bufs)

    def body(*refs):
        outs = refs[n:2 * n]
        send_sems, recv_sems = refs[2 * n:]
        x, y, c = _my_pos()
        chips = _other_chips(x, y)
        sent = []
        for p in range(n):
            for j, chip in enumerate(chips):
                cp = _ici_gather_copy(outs[p], p, j, chip, c, (x, y, 1 - c), send_sems, recv_sems)
                cp.start()
                sent.append(cp)
        for p in range(n):
            for j, chip in enumerate(chips):
                _ici_gather_copy(outs[p], p, j, chip, 1 - c, (x, y, c), send_sems, recv_sems).wait_recv()
        for cp in sent:
            cp.wait_send()

    any_spec = pl.BlockSpec(memory_space=pl.ANY)
    return pl.pallas_call(
        body, name="forward_to_sibling",
        in_specs=[any_spec] * n, out_specs=[any_spec] * n,
        out_shape=[jax.ShapeDtypeStruct(b.shape, b.dtype) for b in bufs],
        scratch_shapes=[pltpu.SemaphoreType.DMA((3 * n,)), pltpu.SemaphoreType.DMA((3 * n,))],
        input_output_aliases={p: p for p in range(n)},
    )(*bufs)


def allreduce_small(v):
    r, c = v.shape

    def body(v_ref, o_ref, buf, send_sems, recv_sems):
        x, y, cc = _my_pos()
        me = 4 * x + 2 * y + cc
        buf[me] = v_ref[...]
        copies = []
        for k in range(1, 8):
            dx, dy, dc = (k >> 2) & 1, (k >> 1) & 1, k & 1
            peer = (x ^ dx, y ^ dy, cc ^ dc)
            cp = pltpu.make_async_remote_copy(
                src_ref=v_ref, dst_ref=buf.at[me], send_sem=send_sems.at[k - 1], recv_sem=recv_sems.at[k - 1],
                device_id=peer, device_id_type=MESH)
            cp.start()
            copies.append(cp)
        for cp in copies:
            cp.wait_recv()
        for cp in copies:
            cp.wait_send()
        acc = buf[0]
        for k in range(1, 8):
            acc = acc + buf[k]
        o_ref[...] = acc

    vm = pl.BlockSpec(memory_space=pltpu.VMEM)
    return pl.pallas_call(
        body, name="allreduce_small",
        in_specs=[vm], out_specs=vm,
        out_shape=jax.ShapeDtypeStruct((r, c), F32),
        scratch_shapes=[pltpu.VMEM((8, r, c), F32), pltpu.SemaphoreType.DMA((7,)), pltpu.SemaphoreType.DMA((7,))],
    )(v)


def exchange_sibling_halves(grads):
    n = len(grads)

    def body(*refs):
        ins, outs = refs[:n], refs[n:2 * n]
        send_sems, recv_sems = refs[2 * n:]
        x, y, c = _my_pos()
        copies = []
        for p in range(n):
            cp = pltpu.make_async_remote_copy(
                src_ref=ins[p].at[:, 1 - c], dst_ref=outs[p], send_sem=send_sems.at[p], recv_sem=recv_sems.at[p],
                device_id=(x, y, 1 - c), device_id_type=MESH)
            cp.start()
            copies.append(cp)
        for cp in copies:
            cp.wait_recv()
        for cp in copies:
            cp.wait_send()

    any_spec = pl.BlockSpec(memory_space=pl.ANY)
    return pl.pallas_call(
        body, name="exchange_sibling_halves",
        in_specs=[any_spec] * n, out_specs=[any_spec] * n,
        out_shape=[jax.ShapeDtypeStruct((g.shape[0],) + g.shape[2:], g.dtype) for g in grads],
        scratch_shapes=[pltpu.SemaphoreType.DMA((n,)), pltpu.SemaphoreType.DMA((n,))],
    )(*grads)


def _chip_partial_copy(part, land, p, j, chip, c, send_sems, recv_sems):
    return pltpu.make_async_remote_copy(
        src_ref=part.at[_chip_id(*chip)], dst_ref=land.at[j], send_sem=send_sems.at[3 * p + j],
        recv_sem=recv_sems.at[3 * p + j], device_id=(*chip, c), device_id_type=MESH)


def chip_partials_start(parts, tag):
    n = len(parts)
    lands = [lax.empty((N_CHIPS - 1,) + s.shape[1:], s.dtype) for s in parts]

    def body(*refs):
        ins, land = refs[:n], refs[n:2 * n]
        send_sems, recv_sems = refs[2 * n], refs[2 * n + 1]
        token = refs[4 * n + 2]
        x, y, c = _my_pos()
        for p in range(n):
            for j, chip in enumerate(_other_chips(x, y)):
                _chip_partial_copy(ins[p], land[p], p, j, chip, c, send_sems, recv_sems).start()
        token[...] = jnp.zeros_like(token)

    out = pl.pallas_call(
        body, name="chip_partials_start_" + tag,
        in_specs=[HBM_SPEC] * (2 * n),
        out_specs=(SEM_SPEC, SEM_SPEC, *([HBM_SPEC] * (2 * n)), pl.BlockSpec(memory_space=pltpu.VMEM)),
        out_shape=(pltpu.SemaphoreType.DMA((3 * n,)), pltpu.SemaphoreType.DMA((3 * n,)),
                   *[pltpu.HBM(a.shape, a.dtype) for a in parts + lands], jax.ShapeDtypeStruct((8, 128), F32)),
        input_output_aliases={i: 2 + i for i in range(2 * n)},
        compiler_params=pltpu.CompilerParams(has_side_effects=DATAFLOW),
    )(*[_in_hbm(a) for a in parts + lands])
    return out[0], out[1], list(out[2:2 + n]), list(out[2 + n:2 + 2 * n]), out[2 + 2 * n]


def chip_partials_wait(send_sems, recv_sems, parts, lands, after, tag):
    n = len(parts)

    def body(*refs):
        ins, land = refs[:n], refs[n:2 * n]
        send_sems, recv_sems = refs[2 * n], refs[2 * n + 1]
        x, y, c = _my_pos()
        for p in range(n):
            for j, chip in enumerate(_other_chips(x, y)):
                cp = _chip_partial_copy(ins[p], land[p], p, j, chip, c, send_sems, recv_sems)
                cp.wait_send()
                cp.wait_recv()

    out = pl.pallas_call(
        body, name="chip_partials_wait_" + tag,
        in_specs=[HBM_SPEC] * (2 * n) + [SEM_SPEC, SEM_SPEC, pl.BlockSpec(memory_space=pl.ANY)],
        out_specs=[HBM_SPEC] * (2 * n),
        out_shape=[pltpu.HBM(a.shape, a.dtype) for a in parts + lands],
        input_output_aliases={i: i for i in range(2 * n)},
        compiler_params=pltpu.CompilerParams(has_side_effects=DATAFLOW),
    )(*parts, *lands, send_sems, recv_sems, after)
    return list(out[n:])


def share_with_sibling(bufs):
    n = len(bufs)

    def body(*refs):
        outs = refs[n:2 * n]
        send_sems, recv_sems = refs[2 * n:]
        x, y, c = _my_pos()
        copies = []
        for p in range(n):
            cp = pltpu.make_async_remote_copy(
                src_ref=outs[p].at[c], dst_ref=outs[p].at[c], send_sem=send_sems.at[p], recv_sem=recv_sems.at[p],
                device_id=(x, y, 1 - c), device_id_type=MESH)
            cp.start()
            copies.append(cp)
        for p in range(n):
            pltpu.make_async_remote_copy(
                src_ref=outs[p].at[1 - c], dst_ref=outs[p].at[1 - c], send_sem=send_sems.at[p],
                recv_sem=recv_sems.at[p], device_id=(x, y, 1 - c), device_id_type=MESH).wait_recv()
        for cp in copies:
            cp.wait_send()

    any_spec = pl.BlockSpec(memory_space=pl.ANY)
    return pl.pallas_call(
        body, name="share_with_sibling",
        in_specs=[any_spec] * n, out_specs=[any_spec] * n,
        out_shape=[jax.ShapeDtypeStruct(b.shape, b.dtype) for b in bufs],
        scratch_shapes=[pltpu.SemaphoreType.DMA((n,)), pltpu.SemaphoreType.DMA((n,))],
        input_output_aliases={p: p for p in range(n)},
    )(*bufs)


def add_sibling(g, recv, half):
    _, _, r, c = g.shape
    tr = _tile(r, 256) if r % 256 == 0 else r

    def body(half_ref, g_ref, r_ref, o32_ref, o16_ref):
        s = g_ref[...] + r_ref[...]
        o32_ref[...] = s
        o16_ref[...] = _b(s)

    return pl.pallas_call(
        body, name="add_sibling",
        grid_spec=pltpu.PrefetchScalarGridSpec(
            num_scalar_prefetch=1, grid=(N_CHIPS, r // tr),
            in_specs=[pl.BlockSpec((None, None, tr, c), lambda k, i, hf: (k, hf[0], i, 0)),
                      pl.BlockSpec((None, tr, c), lambda k, i, hf: (k, i, 0))],
            out_specs=[pl.BlockSpec((None, tr, c), lambda k, i, hf: (k, i, 0)),
                       pl.BlockSpec((None, tr, c), lambda k, i, hf: (k, i, 0))]),
        out_shape=[jax.ShapeDtypeStruct((N_CHIPS, r, c), F32), jax.ShapeDtypeStruct((N_CHIPS, r, c), BF16)],
        compiler_params=_params("arbitrary", "arbitrary"),
    )(half, g, recv)


def add_chip_partials(p32, recv, pos):
    _, r, c = p32.shape
    tr = _tile(r, 256) if r % 256 == 0 else r

    def body(pos_ref, p_ref, r_ref, o_ref):
        acc = p_ref[...]
        for j in range(N_CHIPS - 1):
            acc = acc + r_ref[j].astype(F32)
        o_ref[...] = acc

    return pl.pallas_call(
        body, name="add_chip_partials",
        grid_spec=pltpu.PrefetchScalarGridSpec(
            num_scalar_prefetch=1, grid=(r // tr,),
            in_specs=[pl.BlockSpec((None, tr, c), lambda i, ps: (ps[0], i, 0)),
                      pl.BlockSpec((N_CHIPS - 1, tr, c), lambda i, ps: (0, i, 0))],
            out_specs=pl.BlockSpec((None, tr, c), lambda i, ps: (ps[1], i, 0))),
        out_shape=jax.ShapeDtypeStruct((2, r, c), F32),
        compiler_params=_params("arbitrary"),
    )(pos, p32, recv)


def cast_into_gather(w, pos, row0=0, nrows=None):
    c = w.shape[1]
    nrows = w.shape[0] if nrows is None else nrows
    r = nrows // 2
    common = math.gcd(r, row0) if row0 else r
    tr = max(w for w in range(16, min(common, 512) + 1, 16) if common % w == 0)
    nt = r // tr

    def body(pos_ref, w_ref, o_ref):
        o_ref[...] = _b(w_ref[...])

    return pl.pallas_call(
        body, name="cast_into_gather",
        grid_spec=pltpu.PrefetchScalarGridSpec(
            num_scalar_prefetch=1, grid=(2, nt),
            in_specs=[pl.BlockSpec((tr, c), lambda hf, i, ps: (row0 // tr + hf * nt + i, 0))],
            out_specs=pl.BlockSpec((None, None, tr, c), lambda hf, i, ps: (ps[0], hf, i, 0))),
        out_shape=jax.ShapeDtypeStruct((N_CHIPS, 2, r, c), BF16),
        compiler_params=_params("arbitrary", "arbitrary"),
    )(pos, w)


def build_bias(rel, buckets):
    nb, nh = rel.shape

    def body(rel_ref, bk_ref, o_ref):
        bk = bk_ref[...]
        for h in range(nh):
            acc = jnp.zeros(bk.shape, F32)
            for b in range(nb):
                acc = jnp.where(bk == b, rel_ref[b, h], acc)
            o_ref[h] = acc

    return pl.pallas_call(
        body, name="build_bias",
        in_specs=[pl.BlockSpec(memory_space=pltpu.SMEM), pl.BlockSpec(memory_space=pltpu.VMEM)],
        out_specs=pl.BlockSpec(memory_space=pltpu.VMEM),
        out_shape=jax.ShapeDtypeStruct((nh,) + buckets.shape, F32),
        compiler_params=_params(),
    )(rel, buckets)


SMALL_ROWS = 256


def kernel(x, ffn_norm, ffn_w1, ffn_w3, ffn_w2, ssm_norm, ssm_w_in, ssm_conv_w, ssm_conv_b, ssm_dt_bias, ssm_a_log, ssm_d, ssm_gate_norm, ssm_w_out, kv_norm, w_kv, k_norm, attn_norm, w_q, q_norm, sinks, w_o, rel_bias, loss_target, m_ffn_norm, m_ffn_w1, m_ffn_w3, m_ffn_w2, m_ssm_norm, m_ssm_w_in, m_ssm_conv_w, m_ssm_conv_b, m_ssm_dt_bias, m_ssm_a_log, m_ssm_d, m_ssm_gate_norm, m_ssm_w_out, m_kv_norm, m_w_kv, m_k_norm, m_attn_norm, m_w_q, m_q_norm, m_sinks, m_w_o, m_rel_bias, v_ffn_norm, v_ffn_w1, v_ffn_w3, v_ffn_w2, v_ssm_norm, v_ssm_w_in, v_ssm_conv_w, v_ssm_conv_b, v_ssm_dt_bias, v_ssm_a_log, v_ssm_d, v_ssm_gate_norm, v_ssm_w_out, v_kv_norm, v_w_kv, v_k_norm, v_attn_norm, v_w_q, v_q_norm, v_sinks, v_w_o, v_rel_bias):
    weights = dict(ffn_norm=ffn_norm, ffn_w1=ffn_w1, ffn_w3=ffn_w3, ffn_w2=ffn_w2, ssm_norm=ssm_norm,
                   ssm_w_in=ssm_w_in, ssm_conv_w=ssm_conv_w, ssm_conv_b=ssm_conv_b, ssm_dt_bias=ssm_dt_bias,
                   ssm_a_log=ssm_a_log, ssm_d=ssm_d, ssm_gate_norm=ssm_gate_norm, ssm_w_out=ssm_w_out,
                   kv_norm=kv_norm, w_kv=w_kv, k_norm=k_norm, attn_norm=attn_norm, w_q=w_q, q_norm=q_norm,
                   sinks=sinks, w_o=w_o, rel_bias=rel_bias)
    m_in = dict(ffn_norm=m_ffn_norm, ffn_w1=m_ffn_w1, ffn_w3=m_ffn_w3, ffn_w2=m_ffn_w2, ssm_norm=m_ssm_norm,
                ssm_w_in=m_ssm_w_in, ssm_conv_w=m_ssm_conv_w, ssm_conv_b=m_ssm_conv_b, ssm_dt_bias=m_ssm_dt_bias,
                ssm_a_log=m_ssm_a_log, ssm_d=m_ssm_d, ssm_gate_norm=m_ssm_gate_norm, ssm_w_out=m_ssm_w_out,
                kv_norm=m_kv_norm, w_kv=m_w_kv, k_norm=m_k_norm, attn_norm=m_attn_norm, w_q=m_w_q, q_norm=m_q_norm,
                sinks=m_sinks, w_o=m_w_o, rel_bias=m_rel_bias)
    v_in = dict(ffn_norm=v_ffn_norm, ffn_w1=v_ffn_w1, ffn_w3=v_ffn_w3, ffn_w2=v_ffn_w2, ssm_norm=v_ssm_norm,
                ssm_w_in=v_ssm_w_in, ssm_conv_w=v_ssm_conv_w, ssm_conv_b=v_ssm_conv_b, ssm_dt_bias=v_ssm_dt_bias,
                ssm_a_log=v_ssm_a_log, ssm_d=v_ssm_d, ssm_gate_norm=v_ssm_gate_norm, ssm_w_out=v_ssm_w_out,
                kv_norm=v_kv_norm, w_kv=v_w_kv, k_norm=v_k_norm, attn_norm=v_attn_norm, w_q=v_w_q, q_norm=v_q_norm,
                sinks=v_sinks, w_o=v_w_o, rel_bias=v_rel_bias)
    return _step(x[0], loss_target[0], weights, m_in, v_in)


BIG = ("ffn_w1", "ffn_w3", "ffn_w2", "ssm_w_in", "ssm_w_out", "w_kv", "w_q", "w_o")
SMALL = (("ffn_norm", True), ("ssm_norm", True), ("ssm_conv_w", True), ("ssm_conv_b", True),
         ("ssm_gate_norm", True), ("ssm_dt_bias", False), ("ssm_a_log", False), ("ssm_d", False),
         ("kv_norm", False), ("k_norm", False), ("attn_norm", False), ("q_norm", False), ("sinks", False),
         ("rel_bias", False))


FFN_W = BIG[:3]


def _small_layout(weights):
    off, table = 0, {}
    for name, sharded in SMALL:
        shape = weights[name].shape
        full = shape[:-1] + (shape[-1] * N_CHIPS,) if sharded else shape
        n = int(np.prod(full))
        table[name] = (off, full, sharded)
        off += n
    assert off <= SMALL_ROWS * 128
    return table


def _place_small(values, table, chip, scale_mask):
    flat = jnp.zeros((SMALL_ROWS * 128,), F32)
    for name, (off, full, sharded) in table.items():
        if not sharded:
            continue
        v = values[name].astype(F32)
        lead = int(np.prod(full[:-1]))
        w = v.shape[-1]
        blk = jnp.zeros((lead, full[-1]), F32)
        blk = lax.dynamic_update_slice(blk, v.reshape(lead, w) * scale_mask, (0, chip * w))
        flat = lax.dynamic_update_slice(flat, blk.reshape(-1), (off,))
    return flat.reshape(SMALL_ROWS, 128)


def _take_small(mat, table, name):
    off, full, _ = table[name]
    n = int(np.prod(full))
    return mat.reshape(-1)[off:off + n].reshape(full)


def _step(x, target, weights, m_in, v_in):
    t, d = x.shape
    xi, yi, ci = lax.axis_index("x"), lax.axis_index("y"), lax.axis_index("c")
    chip = 2 * xi + yi
    pos_arr = jnp.stack([chip, ci]).astype(jnp.int32)
    half_arr = jnp.reshape(ci, (1,)).astype(jnp.int32)

    fs = weights["ffn_w1"].shape[-1]
    ffn_rows = {"ffn_w1": d, "ffn_w3": d, "ffn_w2": fs}
    w2d = {n: weights[n].reshape(-1, weights[n].shape[-1]) for n in BIG}
    first = gather_weights([cast_into_gather(w2d[n], pos_arr, 0, ffn_rows[n]) for n in FFN_W])
    mamba_w = ("ssm_w_in", "ssm_w_out")
    late_w = ("w_kv", "w_q", "w_o")
    ms, mr, mbufs, tok_m = gather_start([cast_into_gather(w2d[n], pos_arr) for n in mamba_w], first[0], "mamba")
    ls, lr, lbufs, tok_l = gather_start(
        [cast_into_gather(w2d[n], pos_arr, ffn_rows[n], 3 * ffn_rows[n]) for n in FFN_W]
        + [cast_into_gather(w2d[n], pos_arr) for n in late_w], tok_m, "late")
    no_dep = jnp.zeros((8, 128), F32)
    table = _small_layout(weights)
    south = (ci == 0).astype(F32)
    small = allreduce_small(_place_small(weights, table, chip, south))
    sp = {n: _take_small(small, table, n) if sh else weights[n] for n, sh in SMALL}

    ffn_first = [first[0].reshape(N_CHIPS, 1, d, fs), first[1].reshape(N_CHIPS, 1, d, fs),
                 first[2].reshape(N_CHIPS, 1, fs, d)]
    ffn_g = sp["ffn_norm"]
    h0 = x
    h1, a00, b00 = ffn_fwd(h0, ffn_g[0, 0].reshape(1, d), *ffn_first, 0, tok_m + tok_l)
    gathered = dict(zip(mamba_w, forward_to_sibling(gather_wait(ms, mr, mbufs, h1, "mamba"))))
    n_in = weights["ssm_w_in"].shape[-1] * N_CHIPS
    di = weights["ssm_w_out"].shape[1] * N_CHIPS
    nheads = di // SSM_HEAD_DIM
    conv_dim = n_in - di - nheads
    w_in_full = jnp.moveaxis(gathered["ssm_w_in"].reshape(N_CHIPS, d, n_in // N_CHIPS), 0, 1).reshape(d, n_in)
    hpg = nheads // SSM_GROUPS

    def spread_heads(v):
        lead = v.shape[:-1]
        v = v.reshape(lead + (SSM_GROUPS, hpg))
        v = jnp.pad(v, [(0, 0)] * len(lead) + [(0, 0), (0, 128 - hpg)])
        return v.reshape(lead + (SSM_GROUPS * 128,))

    def gather_heads(v):
        lead = v.shape[:-1]
        return v.reshape(lead + (SSM_GROUPS, 128))[..., :hpg].reshape(lead + (nheads,))

    dt_col0 = di + conv_dim
    n_zx = dt_col0 + SSM_GROUPS * 128
    w_in = jnp.concatenate([w_in_full[:, :dt_col0], spread_heads(w_in_full[:, dt_col0:])], axis=1)
    w_out = gathered["ssm_w_out"].reshape(di, d)
    nkv = weights["w_kv"].shape[1] // (2 * ATT_HEAD_DIM)
    assert nkv == 2
    nh = weights["w_q"].shape[-1] // ATT_HEAD_DIM

    ssm_g = sp["ssm_norm"].reshape(1, d)
    cw = jnp.pad(sp["ssm_conv_w"].reshape(SSM_CONV, conv_dim), [(0, 8 - SSM_CONV), (0, 0)])
    cb = sp["ssm_conv_b"].reshape(1, conv_dim)
    gate_g = sp["ssm_gate_norm"].reshape(1, di)
    dt_bias = spread_heads(sp["ssm_dt_bias"].reshape(1, nheads))
    a_log = spread_heads(sp["ssm_a_log"].reshape(1, nheads))
    d_skip = spread_heads(sp["ssm_d"].reshape(1, nheads))
    kv_g = sp["kv_norm"].reshape(1, d)
    k_g = jnp.tile(sp["k_norm"].reshape(1, ATT_HEAD_DIM), (1, 2))
    attn_g = sp["attn_norm"].reshape(1, d)
    q_g = jnp.tile(sp["q_norm"].reshape(1, ATT_HEAD_DIM), (1, 2))
    sink_row = jnp.pad(sp["sinks"].reshape(1, nh), [(0, 0), (0, 128 - nh)])
    buckets = jnp.asarray(_t5_buckets())
    biasm = build_bias(sp["rel_bias"], buckets).reshape(nh * ATT_WINDOW, 2 * ATT_WINDOW)

    zx = norm_mm(h1, ssm_g, w_in)
    xc = conv_fwd(zx, cw, cb, di)
    y_ssd, states = ssd_fwd(xc, zx, dt_bias, a_log, d_skip, dt_col0)
    h2 = gate_out_fwd(h1, y_ssd, zx, gate_g, w_out)

    late = forward_to_sibling(gather_wait(ls, lr, lbufs, h2, "late"))
    ffn_rest = [late[0].reshape(N_CHIPS, 3, d, fs), late[1].reshape(N_CHIPS, 3, d, fs),
                late[2].reshape(N_CHIPS, 3, fs, d)]
    gathered.update(zip(late_w, late[3:]))
    wkv_heads = gathered["w_kv"].reshape(d, 2 * nkv, 1, ATT_HEAD_DIM)
    w_kvd = jnp.broadcast_to(wkv_heads, (d, 2 * nkv, 2, ATT_HEAD_DIM)).reshape(d, 4 * nkv * ATT_HEAD_DIM)
    wq = gathered["w_q"].reshape(d, -1)
    wo = gathered["w_o"].reshape(-1, d)

    def ffn_w(layer, idx):
        blk = 2 * layer + idx
        return (*ffn_first, 0) if blk == 0 else (*ffn_rest, blk - 1)

    h3, a01, b01 = ffn_fwd(h2, ffn_g[0, 1].reshape(1, d), *ffn_w(0, 1), no_dep)
    kvd = norm_mm(h3, kv_g, w_kvd)
    h4, a10, b10 = ffn_fwd(h3, ffn_g[1, 0].reshape(1, d), *ffn_w(1, 0), no_dep)
    qp = norm_mm(h4, attn_g, wq)
    h5 = attn_fwd(h4, qp, kvd, biasm, sink_row, q_g, k_g, wo)
    h6, a11, b11 = ffn_fwd(h5, ffn_g[1, 1].reshape(1, d), *ffn_w(1, 1), no_dep)
    loss_part, d6 = loss_head(h6, target)
    loss = lax.psum(loss_part[0, 0], ("x", "y", "c"))

    gfn = [[None, None], [None, None]]

    pending = []

    def reduce_start(pieces, tag):
        views = [g.reshape(N_CHIPS, 2, g.shape[1] // 2, g.shape[2]) for _, g in pieces]
        recv1 = exchange_sibling_halves(views)
        p32, p16 = zip(*[add_sibling(g, r, half_arr) for g, r in zip(views, recv1)])
        ss, rs, parts, lands, token = chip_partials_start(list(p16), tag)
        pending.append(([k for k, _ in pieces], p32, ss, rs, parts, lands, tag))
        return token

    def ffn_back(h_in, dy, a_s, b_s, layer, idx, dep):
        dh, u, da, db, s, dg = ffn_bwd(h_in, dy, ffn_g[layer, idx].reshape(1, d), a_s, b_s, *ffn_w(layer, idx), dep)
        gfn[layer][idx] = dg
        return dh, [(("ffn_w1", layer, idx), wgrad_grouped_b(u, da)), (("ffn_w3", layer, idx), wgrad_grouped_b(u, db)),
                    (("ffn_w2", layer, idx), wgrad_grouped_a(s, dy, 0.5))]

    d5, pieces = ffn_back(h5, d6, a11, b11, 1, 1, no_dep)
    tok = reduce_start(pieces, "ffn11")
    dqp, dkvd, o16, dbiasm, dsinks, dqg, dkg = attn_bwd(d5, qp, kvd, biasm, sink_row, q_g, k_g, wo, tok)
    g_wo = wgrad(o16, d5)
    d4, u_q, g_attn_norm = norm_mm_bwd(h4, attn_g, wq, dqp, d5, no_dep)
    g_wq = wgrad(u_q, dqp)
    d3a, pieces = ffn_back(h3, d4, a10, b10, 1, 0, no_dep)
    pieces += [(("w_o",), g_wo.reshape(N_CHIPS, -1, d)), (("w_q",), g_wq.reshape(N_CHIPS, d // N_CHIPS, -1))]
    tok = reduce_start(pieces, "ffn10")
    d3, u_kv, g_kv_norm = norm_mm_bwd(h3, kv_g, w_kvd, dkvd, d3a, tok, 0.5)
    g_wkvd = wgrad(u_kv, dkvd)
    g_wkv = g_wkvd.reshape(d, 2 * nkv, 2, ATT_HEAD_DIM)[:, :, 0, :].reshape(d, 2 * nkv * ATT_HEAD_DIM)
    d2, pieces = ffn_back(h2, d3, a01, b01, 0, 1, no_dep)
    pieces += [(("w_kv",), g_wkv.reshape(N_CHIPS, d // N_CHIPS, -1))]
    tok = reduce_start(pieces, "ffn01")
    dzx, dy_ssd, yn16, g_gate = gate_out_bwd(d2, y_ssd, zx, gate_g, w_out, n_zx, tok)
    g_wout = wgrad(yn16, d2)
    dzx, dxs, dbm, dcm, g_dtb, g_alog, g_dsk = ssd_bwd(dzx, dy_ssd, xc, zx, states, dt_bias, a_log, d_skip, dt_col0)
    dzx, g_cw, g_cb = conv_bwd(dzx, zx, dxs, dbm, dcm, cw, cb, di)
    d1, u_in, g_ssm_norm = norm_mm_bwd(h1, ssm_g, w_in, dzx, d2, no_dep)
    g_win = wgrad(u_in, dzx)
    g_win_full = jnp.concatenate([g_win[:, :dt_col0], gather_heads(g_win[:, dt_col0:])], axis=1)
    pieces = [(("ssm_w_in",), jnp.moveaxis(g_win_full.reshape(d, N_CHIPS, n_in // N_CHIPS), 1, 0)),
              (("ssm_w_out",), g_wout.reshape(N_CHIPS, di // N_CHIPS, d))]
    tok = reduce_start(pieces, "mamba")
    grad_x, pieces = ffn_back(h0, d1, a00, b00, 0, 0, tok)
    tok = reduce_start(pieces, "ffn00")
    g_relb = rel_bias_bwd(dbiasm.reshape(nh, ATT_WINDOW, 2 * ATT_WINDOW), buckets)

    reduced = {}
    for keys, p32, ss, rs, parts, lands, tag in pending:
        lands = chip_partials_wait(ss, rs, parts, lands, tok, tag)
        for k, p, r in zip(keys, p32, lands):
            reduced[k] = add_chip_partials(p, r, pos_arr)
    keys = list(reduced)
    shared = dict(zip(keys, share_with_sibling([reduced[k] for k in keys])))
    grads = {}
    for n in FFN_W:
        blocks = [shared[(n, l, i)].reshape(1, ffn_rows[n], -1) for l in range(2) for i in range(2)]
        grads[n] = jnp.concatenate(blocks, axis=0).reshape(weights[n].shape)
    for n in BIG[3:]:
        grads[n] = shared[(n,)].reshape(weights[n].shape)

    small_grads = {
        "ffn_norm": jnp.stack([jnp.stack([gfn[l][i].reshape(d) for i in range(2)]) for l in range(2)]),
        "ssm_norm": g_ssm_norm.reshape(1, d),
        "ssm_conv_w": g_cw[:SSM_CONV].reshape(1, SSM_CONV, conv_dim),
        "ssm_conv_b": g_cb.reshape(1, conv_dim),
        "ssm_gate_norm": g_gate.reshape(1, di),
        "ssm_dt_bias": gather_heads(g_dtb.reshape(1, -1)), "ssm_a_log": gather_heads(g_alog.reshape(1, -1)),
        "ssm_d": gather_heads(g_dsk.reshape(1, -1)),
        "kv_norm": g_kv_norm.reshape(d), "k_norm": dkg[0, :ATT_HEAD_DIM], "attn_norm": g_attn_norm.reshape(1, d),
        "q_norm": dqg[:, :ATT_HEAD_DIM], "sinks": dsinks[:, :nh], "rel_bias": g_relb[:, :nh],
    }
    flat = jnp.zeros((SMALL_ROWS * 128,), F32)
    for name, (off, fshape, _) in table.items():
        flat = lax.dynamic_update_slice(flat, small_grads[name].astype(F32).reshape(-1), (off,))
    small_sum = allreduce_small(flat.reshape(SMALL_ROWS, 128))
    for name, (off, fshape, sharded) in table.items():
        g = _take_small(small_sum, table, name)
        if sharded:
            w = weights[name].shape[-1]
            lead = int(np.prod(fshape[:-1]))
            g = lax.dynamic_slice(g.reshape(lead, fshape[-1]), (0, chip * w), (lead, w)).reshape(weights[name].shape)
        grads[name] = g.reshape(weights[name].shape)

    names = list(weights)
    deltas, new_m, new_v = {}, {}, {}
    small_names = [n for n, _ in SMALL]
    for n in BIG:
        shp = weights[n].shape
        v2 = lambda a: a.reshape(-1, shp[-1])
        dl, nm, nv = adamw(v2(weights[n]), v2(grads[n]), v2(m_in[n]), v2(v_in[n]))
        deltas[n], new_m[n], new_v[n] = dl.reshape(shp), nm.reshape(shp), nv.reshape(shp)
    sizes = [int(np.prod(weights[n].shape)) for n in small_names]
    tot = sum(sizes)
    rows = -(-tot // 128)
    rows = -(-rows // 8) * 8

    def pack(dct):
        flat = jnp.concatenate([dct[n].reshape(-1) for n in small_names])
        return jnp.pad(flat, (0, rows * 128 - tot), constant_values=1.0).reshape(rows, 128)

    dl, nm, nv = adamw(pack(weights), pack(grads), pack(m_in), pack(v_in))
    off = 0
    for n, sz in zip(small_names, sizes):
        shp = weights[n].shape
        take = lambda a: a.reshape(-1)[off:off + sz].reshape(shp)
        deltas[n], new_m[n], new_v[n] = take(dl), take(nm), take(nv)
        off += sz

    return (loss, grad_x[None], *[grads[n] for n in names], *[deltas[n] for n in names],
            *[new_m[n] for n in names], *[new_v[n] for n in names])
```

```python
import functools
import math

import jax
import jax.numpy as jnp
import numpy as np
from jax import lax
from jax.experimental import pallas as pl
from jax.experimental.pallas import tpu as pltpu

F32 = jnp.float32
BF16 = jnp.bfloat16
EPS = 1e-6
MESH = pl.DeviceIdType.MESH

SSM_HEAD_DIM = 64
SSM_GROUPS = 4
SSM_STATE = 128
SSM_CONV = 4
SSM_CHUNK = 256
ATT_HEAD_DIM = 64
ATT_WINDOW = 128
REL_BUCKETS = 32
N_CHIPS = 4

ADAM_LR = 0.001
ADAM_B1 = 0.9
ADAM_B2 = 0.999
ADAM_EPS = 1e-08
ADAM_WD = 0.01
ADAM_STEP = 10

VMEM_LIMIT_BYTES = 56 * 1024 * 1024
NEG = -1e30


DEP_SPEC = pl.BlockSpec(memory_space=pl.ANY)


def _params(*sem):
    return pltpu.CompilerParams(dimension_semantics=sem if sem else None, vmem_limit_bytes=VMEM_LIMIT_BYTES)


def _dot(a, b):
    return jnp.dot(a, b, preferred_element_type=F32)


def _dot_nt(a, b):
    return lax.dot_general(a, b, (((1,), (1,)), ((), ())), preferred_element_type=F32)


def _dot_tn(a, b):
    return lax.dot_general(a, b, (((0,), (0,)), ((), ())), preferred_element_type=F32)


def _b(x):
    return x.astype(BF16)


@jax.custom_vjp
def _bmm(a, b):
    return _dot(_b(a), _b(b))


def _bmm_fwd(a, b):
    return _bmm(a, b), (a, b)


def _bmm_bwd(res, g):
    a, b = res
    g16 = _b(g)
    return _dot_nt(g16, _b(b)).astype(a.dtype), _dot_tn(_b(a), g16).astype(b.dtype)


_bmm.defvjp(_bmm_fwd, _bmm_bwd)


@jax.custom_vjp
def _bmm_nt(a, b):
    return _dot_nt(_b(a), _b(b))


def _bmm_nt_fwd(a, b):
    return _bmm_nt(a, b), (a, b)


def _bmm_nt_bwd(res, g):
    a, b = res
    g16 = _b(g)
    return _dot(g16, _b(b)).astype(a.dtype), _dot_tn(g16, _b(a)).astype(b.dtype)


_bmm_nt.defvjp(_bmm_nt_fwd, _bmm_nt_bwd)


@jax.custom_vjp
def _bmm_tn(a, b):
    return _dot_tn(_b(a), _b(b))


def _bmm_tn_fwd(a, b):
    return _bmm_tn(a, b), (a, b)


def _bmm_tn_bwd(res, g):
    a, b = res
    g16 = _b(g)
    return _dot_nt(_b(b), g16).astype(a.dtype), _dot(_b(a), g16).astype(b.dtype)


_bmm_tn.defvjp(_bmm_tn_fwd, _bmm_tn_bwd)


def _split3(x):
    hi = _b(x)
    r = x - hi.astype(F32)
    mid = _b(r)
    lo = _b(r - mid.astype(F32))
    return hi, mid, lo


def _x_left_raw(m, x):
    hi, mid, lo = _split3(x)
    return _dot(m, hi) + _dot(m, mid) + _dot(m, lo)


def _x_left_t_raw(m, x):
    hi, mid, lo = _split3(x)
    return _dot_tn(m, hi) + _dot_tn(m, mid) + _dot_tn(m, lo)


def _x_right_raw(x, m):
    hi, mid, lo = _split3(x)
    return _dot(hi, m) + _dot(mid, m) + _dot(lo, m)


def _x_right_t_raw(x, m):
    hi, mid, lo = _split3(x)
    return _dot_nt(hi, m) + _dot_nt(mid, m) + _dot_nt(lo, m)


@jax.custom_vjp
def _xleft(m, x):
    return _x_left_raw(m, x)


_xleft.defvjp(lambda m, x: (_x_left_raw(m, x), m),
              lambda m, g: (jnp.zeros_like(m), _x_left_t_raw(m, g)))


@jax.custom_vjp
def _xright(x, m):
    return _x_right_raw(x, m)


_xright.defvjp(lambda x, m: (_x_right_raw(x, m), m),
               lambda m, g: (_x_right_t_raw(g, m), jnp.zeros_like(m)))


def _sigmoid(x):
    return 1.0 / (1.0 + jnp.exp(-x))


def _silu(x):
    return x * _sigmoid(x)


def _softplus(x):
    return jnp.maximum(x, 0.0) + jnp.log(1.0 + jnp.exp(-jnp.abs(x)))


def _rms(x):
    return x * lax.rsqrt(jnp.mean(x * x, axis=-1, keepdims=True) + EPS)


def _iota(shape, dim):
    return lax.broadcasted_iota(jnp.int32, shape, dim)


def _blockdiag64(n):
    return jnp.where(_iota((n, n), 0) // 64 == _iota((n, n), 1) // 64, 1.0, 0.0).astype(BF16)


def _group64_rms(x, seg_sum):
    ms = seg_sum(x * x) * (1.0 / 64.0)
    return x * lax.rsqrt(ms + EPS)


def _fold64(x):
    ax = x.ndim - 1
    w = x.shape[ax]
    lo = (_iota(x.shape, ax) % 128) < 64
    return x + jnp.where(lo, pltpu.roll(x, w - 64, ax), pltpu.roll(x, 64, ax))


def _tile(n, want):
    t = min(n, want)
    assert n % t == 0, (n, t)
    return t


def _lane_tile(n, cap=1536):
    if n <= cap:
        return n
    return max(w for w in range(128, cap + 1, 128) if n % w == 0)


def ffn_fwd(h, g, w1, w3, w2, blk, dep):
    t, d = h.shape
    nk, fs = w1.shape[0], w1.shape[-1]
    tm = _tile(t, 512)

    def body(h_ref, g_ref, w1_ref, w3_ref, w2_ref, dep_ref, o_ref, a_ref, b_ref, u_scr, acc):
        k = pl.program_id(1)

        @pl.when(k == 0)
        def _():
            u_scr[...] = _b(_rms(h_ref[...]) * g_ref[...])
            acc[...] = jnp.zeros_like(acc)

        u = u_scr[...]
        a = _dot(u, w1_ref[...])
        b = _dot(u, w3_ref[...])
        a_ref[...] = _b(a)
        b_ref[...] = _b(b)
        acc[...] += _dot(_b(_silu(a) * b), w2_ref[...])

        @pl.when(k == nk - 1)
        def _():
            o_ref[...] = h_ref[...] + 0.5 * acc[...]

    wspec = lambda r, c: pl.BlockSpec((None, None, r, c), lambda i, k: (k, blk, 0, 0))
    return pl.pallas_call(
        body, name="ffn_fwd",
        grid=(t // tm, nk),
        in_specs=[pl.BlockSpec((tm, d), lambda i, k: (i, 0)), pl.BlockSpec((1, d), lambda i, k: (0, 0)),
                  wspec(d, fs), wspec(d, fs), wspec(fs, d), DEP_SPEC],
        out_specs=[pl.BlockSpec((tm, d), lambda i, k: (i, 0)),
                   pl.BlockSpec((None, tm, fs), lambda i, k: (k, i, 0)),
                   pl.BlockSpec((None, tm, fs), lambda i, k: (k, i, 0))],
        out_shape=[jax.ShapeDtypeStruct((t, d), F32), jax.ShapeDtypeStruct((nk, t, fs), BF16),
                   jax.ShapeDtypeStruct((nk, t, fs), BF16)],
        scratch_shapes=[pltpu.VMEM((tm, d), BF16), pltpu.VMEM((tm, d), F32)],
        compiler_params=_params("arbitrary", "arbitrary"),
    )(h, g, w1, w3, w2, dep)


def ffn_bwd(h, dy, g, a_s, b_s, w1, w3, w2, blk, dep):
    t, d = h.shape
    nk, fs = w1.shape[0], w1.shape[-1]
    tm = _tile(t, 512)

    def body(h_ref, dy_ref, g_ref, a_ref, b_ref, w1_ref, w3_ref, w2_ref, dep_ref,
             dh_ref, u_ref, da_ref, db_ref, s_ref, dg_ref, dyh_scr, du_acc, da0, db0, da1, db1):
        i, k = pl.program_id(0), pl.program_id(1)

        @pl.when(k == 0)
        def _():
            dyh_scr[...] = _b(0.5 * dy_ref[...])
            du_acc[...] = jnp.zeros_like(du_acc)

        @pl.when((k == 0) & (i == 0))
        def _():
            dg_ref[...] = jnp.zeros_like(dg_ref)

        def step(prev, cur):
            if prev is not None:
                du_acc[...] += _dot_nt(prev[0][...], w1_ref[...]) + _dot_nt(prev[1][...], w3_ref[...])
            if cur is not None:
                ds = _dot_nt(dyh_scr[...], w2_ref[...])
                a = a_ref[...].astype(F32)
                b = b_ref[...].astype(F32)
                sig = _sigmoid(a)
                sl = a * sig
                s_ref[...] = _b(sl * b)
                da = _b(ds * b * (sig * (1.0 + a * (1.0 - sig))))
                db = _b(ds * sl)
                da_ref[...] = da
                db_ref[...] = db
                cur[0][...] = da
                cur[1][...] = db

        even, odd = (da0, db0), (da1, db1)

        @pl.when(k == 0)
        def _():
            step(None, even)

        @pl.when((k > 0) & (k < nk) & (k % 2 == 1))
        def _():
            step(even, odd)

        @pl.when((k > 0) & (k < nk) & (k % 2 == 0))
        def _():
            step(odd, even)

        @pl.when(k == nk)
        def _():
            step(odd if nk % 2 == 0 else even, None)
            hh = h_ref[...]
            rstd = lax.rsqrt(jnp.mean(hh * hh, axis=-1, keepdims=True) + EPS)
            xh = hh * rstd
            gg = g_ref[...]
            u_ref[...] = _b(xh * gg)
            du = du_acc[...]
            dg_ref[...] += jnp.sum(du * xh, axis=0, keepdims=True)
            dxh = du * gg
            dh_ref[...] = dy_ref[...] + rstd * (dxh - xh * jnp.mean(dxh * xh, axis=-1, keepdims=True))

    cur = lambda k: jnp.minimum(k, nk - 1)
    prv = lambda k: jnp.maximum(k - 1, 0)
    wcur = lambda r, c: pl.BlockSpec((None, None, r, c), lambda i, k: (cur(k), blk, 0, 0))
    wprv = lambda r, c: pl.BlockSpec((None, None, r, c), lambda i, k: (prv(k), blk, 0, 0))
    tok = pl.BlockSpec((tm, d), lambda i, k: (i, 0))
    hid = pl.BlockSpec((None, tm, fs), lambda i, k: (cur(k), i, 0))
    return pl.pallas_call(
        body, name="ffn_bwd",
        grid=(t // tm, nk + 1),
        in_specs=[tok, tok, pl.BlockSpec((1, d), lambda i, k: (0, 0)), hid, hid, wprv(d, fs), wprv(d, fs), wcur(fs, d),
                  DEP_SPEC],
        out_specs=[tok, tok, hid, hid, hid, pl.BlockSpec((1, d), lambda i, k: (0, 0))],
        out_shape=[jax.ShapeDtypeStruct((t, d), F32), jax.ShapeDtypeStruct((t, d), BF16),
                   jax.ShapeDtypeStruct((nk, t, fs), BF16), jax.ShapeDtypeStruct((nk, t, fs), BF16),
                   jax.ShapeDtypeStruct((nk, t, fs), BF16), jax.ShapeDtypeStruct((1, d), F32)],
        scratch_shapes=[pltpu.VMEM((tm, d), BF16), pltpu.VMEM((tm, d), F32)] + [pltpu.VMEM((tm, fs), BF16)] * 4,
        compiler_params=_params("arbitrary", "arbitrary"),
    )(h, dy, g, a_s, b_s, w1, w3, w2, dep)


def wgrad_grouped_b(a, bs, scale=1.0):
    t, m = a.shape
    ng, _, n = bs.shape
    tk = _tile(t, 2048)

    def body(a_ref, b_ref, o_ref):
        j = pl.program_id(1)

        @pl.when(j == 0)
        def _():
            o_ref[...] = jnp.zeros_like(o_ref)

        o_ref[...] += _dot_tn(_b(a_ref[...]), _b(b_ref[...]))

        if scale != 1.0:
            @pl.when(j == pl.num_programs(1) - 1)
            def _():
                o_ref[...] = o_ref[...] * scale

    return pl.pallas_call(
        body, name="wgrad_gb",
        grid=(ng, t // tk),
        in_specs=[pl.BlockSpec((tk, m), lambda k, j: (j, 0)), pl.BlockSpec((None, tk, n), lambda k, j: (k, j, 0))],
        out_specs=pl.BlockSpec((None, m, n), lambda k, j: (k, 0, 0)),
        out_shape=jax.ShapeDtypeStruct((ng, m, n), F32),
        compiler_params=_params("arbitrary", "arbitrary"),
    )(a, bs)


def wgrad_grouped_a(as_, b, scale=1.0):
    ng, t, m = as_.shape
    n = b.shape[1]
    tk = _tile(t, 2048)

    def body(a_ref, b_ref, o_ref):
        j = pl.program_id(1)

        @pl.when(j == 0)
        def _():
            o_ref[...] = jnp.zeros_like(o_ref)

        o_ref[...] += _dot_tn(_b(a_ref[...]), _b(b_ref[...]))

        if scale != 1.0:
            @pl.when(j == pl.num_programs(1) - 1)
            def _():
                o_ref[...] = o_ref[...] * scale

    return pl.pallas_call(
        body, name="wgrad_ga",
        grid=(ng, t // tk),
        in_specs=[pl.BlockSpec((None, tk, m), lambda k, j: (k, j, 0)), pl.BlockSpec((tk, n), lambda k, j: (j, 0))],
        out_specs=pl.BlockSpec((None, m, n), lambda k, j: (k, 0, 0)),
        out_shape=jax.ShapeDtypeStruct((ng, m, n), F32),
        compiler_params=_params("arbitrary", "arbitrary"),
    )(as_, b)


def wgrad(a, b):
    t, m = a.shape
    n = b.shape[1]
    tk = _tile(t, 1024)
    tn = _lane_tile(n, 1536 if m <= 1024 else 512)

    def body(a_ref, b_ref, o_ref):
        @pl.when(pl.program_id(1) == 0)
        def _():
            o_ref[...] = jnp.zeros_like(o_ref)

        o_ref[...] += _dot_tn(_b(a_ref[...]), _b(b_ref[...]))

    return pl.pallas_call(
        body, name="wgrad",
        grid=(n // tn, t // tk),
        in_specs=[pl.BlockSpec((tk, m), lambda c, j: (j, 0)), pl.BlockSpec((tk, tn), lambda c, j: (j, c))],
        out_specs=pl.BlockSpec((m, tn), lambda c, j: (0, c)),
        out_shape=jax.ShapeDtypeStruct((m, n), F32),
        compiler_params=_params("arbitrary", "arbitrary"),
    )(a, b)


def norm_mm(h, g, w):
    t, d = h.shape
    n = w.shape[1]
    tm = _tile(t, 1024)
    tn = _lane_tile(n)

    def body(h_ref, g_ref, w_ref, o_ref, u_scr):
        @pl.when(pl.program_id(1) == 0)
        def _():
            u_scr[...] = _b(_rms(h_ref[...]) * g_ref[...])

        o_ref[...] = _dot(u_scr[...], w_ref[...])

    return pl.pallas_call(
        body, name="norm_mm",
        grid=(t // tm, n // tn),
        in_specs=[pl.BlockSpec((tm, d), lambda i, j: (i, 0)), pl.BlockSpec((1, d), lambda i, j: (0, 0)),
                  pl.BlockSpec((d, tn), lambda i, j: (0, j))],
        out_specs=pl.BlockSpec((tm, tn), lambda i, j: (i, j)),
        out_shape=jax.ShapeDtypeStruct((t, n), F32),
        scratch_shapes=[pltpu.VMEM((tm, d), BF16)],
        compiler_params=_params("arbitrary", "arbitrary"),
    )(h, g, w)


def norm_mm_bwd(h, g, w, dout, dres, dep, scale=1.0):
    t, d = h.shape
    n = w.shape[1]
    tm = _tile(t, 512)
    tn = _lane_tile(n)
    nj = n // tn

    def body(h_ref, g_ref, w_ref, do_ref, dr_ref, dep_ref, dh_ref, u_ref, dg_ref, du_acc):
        i, j = pl.program_id(0), pl.program_id(1)

        @pl.when(j == 0)
        def _():
            du_acc[...] = jnp.zeros_like(du_acc)

        @pl.when((j == 0) & (i == 0))
        def _():
            dg_ref[...] = jnp.zeros_like(dg_ref)

        du_acc[...] += _dot_nt(_b(do_ref[...]), w_ref[...])

        @pl.when(j == nj - 1)
        def _():
            hh = h_ref[...]
            rstd = lax.rsqrt(jnp.mean(hh * hh, axis=-1, keepdims=True) + EPS)
            xh = hh * rstd
            gg = g_ref[...]
            u_ref[...] = _b(xh * gg)
            du = du_acc[...] * scale
            dg_ref[...] += jnp.sum(du * xh, axis=0, keepdims=True)
            dxh = du * gg
            dh_ref[...] = dr_ref[...] + rstd * (dxh - xh * jnp.mean(dxh * xh, axis=-1, keepdims=True))

    tok = pl.BlockSpec((tm, d), lambda i, j: (i, 0))
    return pl.pallas_call(
        body, name="norm_mm_bwd",
        grid=(t // tm, nj),
        in_specs=[tok, pl.BlockSpec((1, d), lambda i, j: (0, 0)), pl.BlockSpec((d, tn), lambda i, j: (0, j)),
                  pl.BlockSpec((tm, tn), lambda i, j: (i, j)), tok, DEP_SPEC],
        out_specs=[tok, tok, pl.BlockSpec((1, d), lambda i, j: (0, 0))],
        out_shape=[jax.ShapeDtypeStruct((t, d), F32), jax.ShapeDtypeStruct((t, d), BF16),
                   jax.ShapeDtypeStruct((1, d), F32)],
        scratch_shapes=[pltpu.VMEM((tm, d), F32)],
        compiler_params=_params("arbitrary", "arbitrary"),
    )(h, g, w, dout, dres, dep)


CONV_COLS = 512


CONV_ROWS = 64


def _conv_pre(ext, w, b, r0, n):
    return (b + w[0:1] * ext[pl.ds(5 + r0, n), :] + w[1:2] * ext[pl.ds(6 + r0, n), :]
            + w[2:3] * ext[pl.ds(7 + r0, n), :] + w[3:4] * ext[pl.ds(8 + r0, n), :])


def conv_fwd(zx, cw, cb, col0):
    t = zx.shape[0]
    c = cw.shape[1]
    tm = _tile(t, 512)
    cb0 = col0 // CONV_COLS

    rc = _tile(tm, CONV_ROWS)

    def body(x_ref, w_ref, b_ref, o_ref, ext):
        @pl.when(pl.program_id(1) == 0)
        def _():
            ext[0:8, :] = jnp.zeros((8, CONV_COLS), F32)

        ext[8:, :] = x_ref[...]
        w, b = w_ref[...], b_ref[...]
        for r0 in range(0, tm, rc):
            o_ref[r0:r0 + rc, :] = _silu(_conv_pre(ext, w, b, r0, rc))
        ext[0:8, :] = ext[tm:tm + 8, :]

    return pl.pallas_call(
        body, name="conv_fwd",
        grid=(c // CONV_COLS, t // tm),
        in_specs=[pl.BlockSpec((tm, CONV_COLS), lambda j, i: (i, cb0 + j)),
                  pl.BlockSpec((8, CONV_COLS), lambda j, i: (0, j)), pl.BlockSpec((1, CONV_COLS), lambda j, i: (0, j))],
        out_specs=pl.BlockSpec((tm, CONV_COLS), lambda j, i: (i, j)),
        out_shape=jax.ShapeDtypeStruct((t, c), F32),
        scratch_shapes=[pltpu.VMEM((tm + 8, CONV_COLS), F32)],
        compiler_params=_params("arbitrary", "arbitrary"),
    )(zx, cw, cb)


def conv_bwd(dzx, zx, dxs, dbm, dcm, cw, cb, col0):
    t = zx.shape[0]
    c = cw.shape[1]
    tm = _tile(t, 512)
    nt = t // tm
    cb0 = col0 // CONV_COLS
    nxs = dxs.shape[1] // CONV_COLS
    hb = tm // 8

    rc = _tile(tm, CONV_ROWS)

    def body(dzx_ref, x_ref, xh_ref, dxs_ref, db_ref, dc_ref, w_ref, b_ref, o_ref, dw_ref, dbias_ref, ext, gy):
        j, i = pl.program_id(0), pl.program_id(1)
        ri = nt - 1 - i

        @pl.when(i == 0)
        def _():
            gy[tm:tm + 8, :] = jnp.zeros((8, CONV_COLS), F32)
            dw_ref[...] = jnp.zeros_like(dw_ref)
            dbias_ref[...] = jnp.zeros_like(dbias_ref)

        ext[0:8, :] = jnp.where(ri > 0, xh_ref[...], 0.0)
        ext[8:, :] = x_ref[...]
        w, b = w_ref[...], b_ref[...]
        dw = [jnp.zeros((1, CONV_COLS), F32) for _ in range(SSM_CONV)]
        dbias = jnp.zeros((1, CONV_COLS), F32)
        for r0 in range(0, tm, rc):
            rows = pl.ds(r0, rc)
            win = [ext[pl.ds(5 + tap + r0, rc), :] for tap in range(SSM_CONV)]
            y = b + w[0:1] * win[0] + w[1:2] * win[1] + w[2:3] * win[2] + w[3:4] * win[3]
            sig = _sigmoid(y)
            dout = jnp.where(j < nxs, dxs_ref[rows, :], jnp.where(j == nxs, db_ref[rows, :], dc_ref[rows, :]))
            g = dout * (sig * (1.0 + y * (1.0 - sig)))
            gy[rows, :] = g
            dbias = dbias + jnp.sum(g, axis=0, keepdims=True)
            for tap in range(SSM_CONV):
                dw[tap] = dw[tap] + jnp.sum(g * win[tap], axis=0, keepdims=True)
        for r0 in range(0, tm, rc):
            o_ref[r0:r0 + rc, :] = (w[0:1] * gy[pl.ds(r0 + 3, rc), :] + w[1:2] * gy[pl.ds(r0 + 2, rc), :]
                                    + w[2:3] * gy[pl.ds(r0 + 1, rc), :] + w[3:4] * gy[pl.ds(r0, rc), :])
        gy[tm:tm + 8, :] = gy[0:8, :]
        for tap in range(SSM_CONV):
            dw_ref[tap:tap + 1, :] += dw[tap]
        dbias_ref[...] += dbias

    return pl.pallas_call(
        body, name="conv_bwd",
        grid=(c // CONV_COLS, nt),
        in_specs=[pl.BlockSpec(memory_space=pl.ANY),
                  pl.BlockSpec((tm, CONV_COLS), lambda j, i: (nt - 1 - i, cb0 + j)),
                  pl.BlockSpec((8, CONV_COLS), lambda j, i: (jnp.maximum((nt - 1 - i) * hb - 1, 0), cb0 + j)),
                  pl.BlockSpec((tm, CONV_COLS), lambda j, i: (nt - 1 - i, jnp.minimum(j, nxs - 1))),
                  pl.BlockSpec((tm, CONV_COLS), lambda j, i: (nt - 1 - i, 0)),
                  pl.BlockSpec((tm, CONV_COLS), lambda j, i: (nt - 1 - i, 0)),
                  pl.BlockSpec((8, CONV_COLS), lambda j, i: (0, j)), pl.BlockSpec((1, CONV_COLS), lambda j, i: (0, j))],
        out_specs=[pl.BlockSpec((tm, CONV_COLS), lambda j, i: (nt - 1 - i, cb0 + j)),
                   pl.BlockSpec((8, CONV_COLS), lambda j, i: (0, j)), pl.BlockSpec((1, CONV_COLS), lambda j, i: (0, j))],
        out_shape=[jax.ShapeDtypeStruct(dzx.shape, F32), jax.ShapeDtypeStruct((8, c), F32),
                   jax.ShapeDtypeStruct((1, c), F32)],
        scratch_shapes=[pltpu.VMEM((tm + 8, CONV_COLS), F32), pltpu.VMEM((tm + 8, CONV_COLS), F32)],
        input_output_aliases={0: 0},
        compiler_params=_params("arbitrary", "arbitrary"),
    )(dzx, zx, zx, dxs, dbm, dcm, cw, cb)


def _ssd_group(xs, bg, cg, dtraw, s0, bias, alog, dsk):
    L = xs.shape[0]
    causal = _iota((L, L), 0) >= _iota((L, L), 1)
    tril = jnp.where(causal, 1.0, 0.0).astype(BF16)
    dt = _softplus(dtraw + bias)
    a = -jnp.exp(alog)
    acum = _xleft(tril, dt * a)
    acum_t = acum.T
    dt_t = dt.T
    cb = _bmm_nt(cg, bg)
    lo = _iota((L, 128), 1) < 64
    lo_row = _iota((1, 128), 1) < 64
    lo_col = _iota((128, 1), 0) < 64
    alast = acum[L - 1:L, :]
    ys, s1s = [], []
    for q in range(4):
        xp = xs[:, q * 128:(q + 1) * 128]
        sp = s0[q * 128:(q + 1) * 128, :]
        yd, ec, wc, el = [], [], [], []
        for j in range(2):
            r = 2 * q + j
            ac = acum[:, r:r + 1]
            decay = jnp.exp(jnp.where(causal, ac - acum_t[r:r + 1, :], NEG))
            yd.append(_bmm(cb * decay * dt_t[r:r + 1, :], xp))
            ec.append(jnp.exp(ac))
            al = alast[:, r:r + 1]
            wc.append(jnp.exp(al - ac) * dt[:, r:r + 1])
            el.append(jnp.exp(al))
        y_off = _bmm_nt(cg, sp) * jnp.where(lo, ec[0], ec[1])
        dsel = jnp.where(lo_row, dsk[:, 2 * q:2 * q + 1], dsk[:, 2 * q + 1:2 * q + 2])
        ys.append(jnp.where(lo, yd[0], yd[1]) + y_off + dsel * xp)
        xw = xp * jnp.where(lo, wc[0], wc[1])
        s1s.append(sp * jnp.where(lo_col, el[0], el[1]) + _bmm_tn(xw, bg))
    return jnp.concatenate(ys, axis=1), jnp.concatenate(s1s, axis=0)


def ssd_fwd(xc, zx, bias, alog, dsk, dt_col0):
    t = xc.shape[0]
    L = _tile(t, SSM_CHUNK)
    nc = t // L
    g = SSM_GROUPS
    dtb = dt_col0 // 128

    def body(xs_ref, b_ref, c_ref, dt_ref, bias_ref, alog_ref, dsk_ref, y_ref, st_ref, state):
        c, gi = pl.program_id(0), pl.program_id(1)

        @pl.when(c == 0)
        def _():
            state[gi] = jnp.zeros((512, 128), F32)

        s0 = state[gi]
        st_ref[...] = s0
        y, s1 = _ssd_group(xs_ref[...], b_ref[...], c_ref[...], dt_ref[...], s0,
                           bias_ref[...], alog_ref[...], dsk_ref[...])
        y_ref[...] = y
        state[gi] = s1

    vec = pl.BlockSpec((1, 128), lambda c, gi: (0, gi))
    return pl.pallas_call(
        body, name="ssd_fwd",
        grid=(nc, g),
        in_specs=[pl.BlockSpec((L, 512), lambda c, gi: (c, gi)),
                  pl.BlockSpec((L, 128), lambda c, gi: (c, 16 + gi)),
                  pl.BlockSpec((L, 128), lambda c, gi: (c, 20 + gi)),
                  pl.BlockSpec((L, 128), lambda c, gi: (c, dtb + gi)), vec, vec, vec],
        out_specs=[pl.BlockSpec((L, 512), lambda c, gi: (c, gi)),
                   pl.BlockSpec((None, None, 512, 128), lambda c, gi: (c, gi, 0, 0))],
        out_shape=[jax.ShapeDtypeStruct((t, 2048), F32), jax.ShapeDtypeStruct((nc, g, 512, 128), F32)],
        scratch_shapes=[pltpu.VMEM((g, 512, 128), F32)],
        compiler_params=_params("arbitrary", "arbitrary"),
    )(xc, xc, xc, zx, bias, alog, dsk)


def ssd_bwd(dzx, dy, xc, zx, states, bias, alog, dsk, dt_col0):
    t = xc.shape[0]
    L = _tile(t, SSM_CHUNK)
    nc = t // L
    g = SSM_GROUPS
    dtb = dt_col0 // 128

    def body(dzx_ref, dy_ref, xs_ref, b_ref, c_ref, dt_ref, st_ref, bias_ref, alog_ref, dsk_ref,
             ddt_ref, dxs_ref, db_ref, dc_ref, dbias_ref, dalog_ref, ddsk_ref, dstate):
        c, gi = pl.program_id(0), pl.program_id(1)

        @pl.when(c == 0)
        def _():
            dstate[gi] = jnp.zeros((512, 128), F32)

        @pl.when((c == 0) & (gi == 0))
        def _():
            dbias_ref[...] = jnp.zeros_like(dbias_ref)
            dalog_ref[...] = jnp.zeros_like(dalog_ref)
            ddsk_ref[...] = jnp.zeros_like(ddsk_ref)

        _, vjp = jax.vjp(_ssd_group, xs_ref[...], b_ref[...], c_ref[...], dt_ref[...], st_ref[...],
                         bias_ref[...], alog_ref[...], dsk_ref[...])
        dxs, db, dc, ddt, ds0, dbias, dalog, ddsk = vjp((dy_ref[...], dstate[gi]))
        dxs_ref[...] = dxs
        db_ref[...] = db
        dc_ref[...] = dc
        ddt_ref[...] = ddt
        dstate[gi] = ds0
        dbias_ref[gi] += dbias
        dalog_ref[gi] += dalog
        ddsk_ref[gi] += ddsk

    rc = lambda c: nc - 1 - c
    vec = pl.BlockSpec((1, 128), lambda c, gi: (0, gi))
    acc = pl.BlockSpec((g, 1, 128), lambda c, gi: (0, 0, 0))
    return pl.pallas_call(
        body, name="ssd_bwd",
        grid=(nc, g),
        in_specs=[pl.BlockSpec(memory_space=pl.ANY),
                  pl.BlockSpec((L, 512), lambda c, gi: (rc(c), gi)),
                  pl.BlockSpec((L, 512), lambda c, gi: (rc(c), gi)),
                  pl.BlockSpec((L, 128), lambda c, gi: (rc(c), 16 + gi)),
                  pl.BlockSpec((L, 128), lambda c, gi: (rc(c), 20 + gi)),
                  pl.BlockSpec((L, 128), lambda c, gi: (rc(c), dtb + gi)),
                  pl.BlockSpec((None, None, 512, 128), lambda c, gi: (rc(c), gi, 0, 0)), vec, vec, vec],
        out_specs=[pl.BlockSpec((L, 128), lambda c, gi: (rc(c), dtb + gi)),
                   pl.BlockSpec((L, 512), lambda c, gi: (rc(c), gi)),
                   pl.BlockSpec((L, 128), lambda c, gi: (rc(c), gi)),
                   pl.BlockSpec((L, 128), lambda c, gi: (rc(c), gi)), acc, acc, acc],
        out_shape=[jax.ShapeDtypeStruct(dzx.shape, F32), jax.ShapeDtypeStruct((t, 2048), F32),
                   jax.ShapeDtypeStruct((t, 512), F32), jax.ShapeDtypeStruct((t, 512), F32),
                   jax.ShapeDtypeStruct((g, 1, 128), F32), jax.ShapeDtypeStruct((g, 1, 128), F32),
                   jax.ShapeDtypeStruct((g, 1, 128), F32)],
        scratch_shapes=[pltpu.VMEM((g, 512, 128), F32)],
        input_output_aliases={0: 0},
        compiler_params=_params("arbitrary", "arbitrary"),
    )(dzx, dy, xc, xc, xc, zx, states, bias, alog, dsk)


def _gate_tile(y, z, gn):
    gated = y * _silu(z)
    parts = [_rms(gated[:, k * 512:(k + 1) * 512]) for k in range(SSM_GROUPS)]
    return jnp.concatenate(parts, axis=1) * gn


def gate_out_fwd(h, y, zx, gn, w_out):
    t, d = h.shape
    di = y.shape[1]
    tm = _tile(t, 256)

    def body(h_ref, y_ref, z_ref, gn_ref, w_ref, o_ref):
        yn = _gate_tile(y_ref[...], z_ref[...], gn_ref[...])
        o_ref[...] = h_ref[...] + _dot(_b(yn), w_ref[...])

    return pl.pallas_call(
        body, name="gate_out_fwd",
        grid=(t // tm,),
        in_specs=[pl.BlockSpec((tm, d), lambda i: (i, 0)), pl.BlockSpec((tm, di), lambda i: (i, 0)),
                  pl.BlockSpec((tm, di), lambda i: (i, 0)), pl.BlockSpec((1, di), lambda i: (0, 0)),
                  pl.BlockSpec((di, d), lambda i: (0, 0))],
        out_specs=pl.BlockSpec((tm, d), lambda i: (i, 0)),
        out_shape=jax.ShapeDtypeStruct((t, d), F32),
        compiler_params=_params("arbitrary"),
    )(h, y, zx, gn, w_out)


def gate_out_bwd(dy, y, zx, gn, w_out, n_zx, dep):
    t, d = dy.shape
    di = y.shape[1]
    tm = _tile(t, 256)

    def body(dy_ref, y_ref, z_ref, gn_ref, w_ref, dep_ref, dz_ref, dys_ref, yn_ref, dgn_ref):
        @pl.when(pl.program_id(0) == 0)
        def _():
            dgn_ref[...] = jnp.zeros_like(dgn_ref)

        yn, vjp = jax.vjp(_gate_tile, y_ref[...], z_ref[...], gn_ref[...])
        dyn = _dot_nt(_b(dy_ref[...]), w_ref[...])
        dys, dz, dgn = vjp(dyn)
        yn_ref[...] = _b(yn)
        dys_ref[...] = dys
        dz_ref[...] = dz
        dgn_ref[...] += dgn

    return pl.pallas_call(
        body, name="gate_out_bwd",
        grid=(t // tm,),
        in_specs=[pl.BlockSpec((tm, d), lambda i: (i, 0)), pl.BlockSpec((tm, di), lambda i: (i, 0)),
                  pl.BlockSpec((tm, di), lambda i: (i, 0)), pl.BlockSpec((1, di), lambda i: (0, 0)),
                  pl.BlockSpec((di, d), lambda i: (0, 0)), DEP_SPEC],
        out_specs=[pl.BlockSpec((tm, di), lambda i: (i, 0)), pl.BlockSpec((tm, di), lambda i: (i, 0)),
                   pl.BlockSpec((tm, di), lambda i: (i, 0)), pl.BlockSpec((1, di), lambda i: (0, 0))],
        out_shape=[jax.ShapeDtypeStruct((t, n_zx), F32), jax.ShapeDtypeStruct((t, di), F32),
                   jax.ShapeDtypeStruct((t, di), BF16), jax.ShapeDtypeStruct((1, di), F32)],
        compiler_params=_params("arbitrary"),
    )(dy, y, zx, gn, w_out, dep)


def _attn_block(qp, kvp, kvc, biasm, sinks, qg, kg, w_o, first):
    nq = qp.shape[0]
    n_pairs = qp.shape[1] // 128
    hk = n_pairs
    rows = hk * nq
    seg = functools.partial(_xright, m=_blockdiag64(128))
    scale = ATT_HEAD_DIM ** -0.5
    qi = (_iota((rows, 2 * nq), 0) % nq) + nq
    kj = _iota((rows, 2 * nq), 1)
    dist = qi - kj
    valid = (dist >= 0) & (dist < ATT_WINDOW) & (jnp.logical_not(first) | (kj >= nq))
    lo = _iota((nq, 128), 1) < 64
    kv = jnp.concatenate([kvp, kvc], axis=0)
    outs = [None] * n_pairs
    for kvh in range(2):
        kn = _group64_rms(kv[:, kvh * 128:(kvh + 1) * 128], seg) * kg
        vv = kv[:, 256 + kvh * 128:256 + (kvh + 1) * 128]
        pairs = range(kvh * hk // 2, (kvh + 1) * hk // 2)
        qs, sk = [], []
        for p in pairs:
            qn = _group64_rms(qp[:, p * 128:(p + 1) * 128], seg) * qg
            qs += [jnp.where(lo, qn, 0.0), jnp.where(lo, 0.0, qn)]
            sk += [jnp.broadcast_to(sinks[:, h:h + 1], (nq, 1)) for h in (2 * p, 2 * p + 1)]
        sink = jnp.concatenate(sk, axis=0)
        s = _bmm_nt(jnp.concatenate(qs, axis=0), kn) * scale + biasm[kvh * rows:(kvh + 1) * rows]
        s = jnp.where(valid, s, NEG)
        m = lax.stop_gradient(jnp.maximum(jnp.max(s, axis=-1, keepdims=True), sink))
        pexp = jnp.exp(s - m)
        den = jnp.sum(pexp, axis=-1, keepdims=True) + jnp.exp(sink - m)
        o = _bmm(pexp * (1.0 / den), vv)
        for n, p in enumerate(pairs):
            outs[p] = jnp.where(lo, o[2 * n * nq:(2 * n + 1) * nq], o[(2 * n + 1) * nq:(2 * n + 2) * nq])
    o = jnp.concatenate(outs, axis=1)
    return _bmm(o, w_o), o


def attn_fwd(h, qp, kvd, biasm, sinks, qg, kg, w_o):
    t, d = h.shape
    nq = ATT_WINDOW
    nb = t // nq
    nh = qp.shape[1] // ATT_HEAD_DIM

    def body(h_ref, q_ref, kp_ref, kc_ref, bias_ref, s_ref, qg_ref, kg_ref, w_ref, o_ref):
        out, _ = _attn_block(q_ref[...], kp_ref[...], kc_ref[...], bias_ref[...], s_ref[...], qg_ref[...],
                             kg_ref[...], w_ref[...], pl.program_id(0) == 0)
        o_ref[...] = h_ref[...] + out

    vec = pl.BlockSpec((1, 128), lambda i: (0, 0))
    return pl.pallas_call(
        body, name="attn_fwd",
        grid=(nb,),
        in_specs=[pl.BlockSpec((nq, d), lambda i: (i, 0)), pl.BlockSpec((nq, nh * 64), lambda i: (i, 0)),
                  pl.BlockSpec((nq, 512), lambda i: (jnp.maximum(i - 1, 0), 0)),
                  pl.BlockSpec((nq, 512), lambda i: (i, 0)),
                  pl.BlockSpec((nh * nq, 2 * nq), lambda i: (0, 0)), vec, vec, vec,
                  pl.BlockSpec((nh * 64, d), lambda i: (0, 0))],
        out_specs=pl.BlockSpec((nq, d), lambda i: (i, 0)),
        out_shape=jax.ShapeDtypeStruct((t, d), F32),
        compiler_params=_params("arbitrary"),
    )(h, qp, kvd, kvd, biasm, sinks, qg, kg, w_o)


def attn_bwd(dy, qp, kvd, biasm, sinks, qg, kg, w_o, dep):
    t, d = dy.shape
    nq = ATT_WINDOW
    nb = t // nq
    nh = qp.shape[1] // ATT_HEAD_DIM

    def body(dy_ref, q_ref, kp_ref, kc_ref, bias_ref, s_ref, qg_ref, kg_ref, w_ref, dep_ref,
             dq_ref, dkv_ref, o_ref, dbias_ref, ds_ref, dqg_ref, dkg_ref, carry):
        i = pl.program_id(0)

        @pl.when(i == 0)
        def _():
            carry[...] = jnp.zeros_like(carry)
            dbias_ref[...] = jnp.zeros_like(dbias_ref)
            ds_ref[...] = jnp.zeros_like(ds_ref)
            dqg_ref[...] = jnp.zeros_like(dqg_ref)
            dkg_ref[...] = jnp.zeros_like(dkg_ref)

        @pl.when(i < nb)
        def _():
            fn = functools.partial(_attn_block, w_o=w_ref[...], first=(i == 0))
            (_, o), vjp = jax.vjp(fn, q_ref[...], kp_ref[...], kc_ref[...], bias_ref[...], s_ref[...],
                                  qg_ref[...], kg_ref[...])
            dq, dkp, dkc, dbias, dsk, dqg, dkg = vjp((dy_ref[...], jnp.zeros((nq, nh * 64), F32)))
            dq_ref[...] = dq
            o_ref[...] = _b(o)
            dkv_ref[...] = _fold64(carry[...] + dkp)
            carry[...] = dkc
            dbias_ref[...] += dbias
            ds_ref[...] += dsk
            dqg_ref[...] += _fold64(dqg)
            dkg_ref[...] += _fold64(dkg)

        @pl.when(i == nb)
        def _():
            dkv_ref[...] = _fold64(carry[...])

    cl = lambda i: jnp.minimum(i, nb - 1)
    vec = pl.BlockSpec((1, 128), lambda i: (0, 0))
    return pl.pallas_call(
        body, name="attn_bwd",
        grid=(nb + 1,),
        in_specs=[pl.BlockSpec((nq, d), lambda i: (cl(i), 0)), pl.BlockSpec((nq, nh * 64), lambda i: (cl(i), 0)),
                  pl.BlockSpec((nq, 512), lambda i: (jnp.maximum(cl(i) - 1, 0), 0)),
                  pl.BlockSpec((nq, 512), lambda i: (cl(i), 0)),
                  pl.BlockSpec((nh * nq, 2 * nq), lambda i: (0, 0)), vec, vec, vec,
                  pl.BlockSpec((nh * 64, d), lambda i: (0, 0)), DEP_SPEC],
        out_specs=[pl.BlockSpec((nq, nh * 64), lambda i: (cl(i), 0)),
                   pl.BlockSpec((nq, 512), lambda i: (jnp.maximum(i - 1, 0), 0)),
                   pl.BlockSpec((nq, nh * 64), lambda i: (cl(i), 0)),
                   pl.BlockSpec((nh * nq, 2 * nq), lambda i: (0, 0)), vec, vec, vec],
        out_shape=[jax.ShapeDtypeStruct((t, nh * 64), F32), jax.ShapeDtypeStruct((t, 512), F32),
                   jax.ShapeDtypeStruct((t, nh * 64), BF16), jax.ShapeDtypeStruct((nh * nq, 2 * nq), F32),
                   jax.ShapeDtypeStruct((1, 128), F32), jax.ShapeDtypeStruct((1, 128), F32),
                   jax.ShapeDtypeStruct((1, 128), F32)],
        scratch_shapes=[pltpu.VMEM((nq, 512), F32)],
        compiler_params=_params("arbitrary"),
    )(dy, qp, kvd, kvd, biasm, sinks, qg, kg, w_o, dep)


def _t5_buckets():
    nq = ATT_WINDOW
    dist = (np.arange(nq)[:, None] + nq) - np.arange(2 * nq)[None, :]
    n = np.maximum(dist, 0)
    max_exact = REL_BUCKETS // 2
    nf = np.maximum(n, 1).astype(np.float32)
    large = max_exact + (np.log(nf / max_exact) / math.log(ATT_WINDOW / max_exact)
                         * (REL_BUCKETS - max_exact)).astype(np.int32)
    large = np.minimum(large, REL_BUCKETS - 1)
    return np.where(n < max_exact, n, large).astype(np.int32)


def rel_bias_bwd(dbias, buckets):
    nh = dbias.shape[0]

    def body(db_ref, bk_ref, o_ref):
        bk = bk_ref[...]
        lane = _iota((1, 128), 1)
        row = _iota((REL_BUCKETS, 128), 0)
        acc = jnp.zeros((REL_BUCKETS, 128), F32)
        for h in range(nh):
            dbh = db_ref[h]
            for b in range(REL_BUCKETS):
                v = jnp.sum(jnp.where(bk == b, dbh, 0.0))
                acc = acc + jnp.where((row == b) & (lane == h), v, 0.0)
        o_ref[...] = acc

    return pl.pallas_call(
        body, name="rel_bias_bwd",
        out_shape=jax.ShapeDtypeStruct((REL_BUCKETS, 128), F32),
        compiler_params=_params(),
    )(dbias, buckets)


def loss_head(y, target):
    t, d = y.shape
    tm = _tile(t, 512)

    def body(y_ref, t_ref, l_ref, dy_ref):
        @pl.when(pl.program_id(0) == 0)
        def _():
            l_ref[...] = jnp.zeros_like(l_ref)

        e = y_ref[...] - t_ref[...]
        l_ref[...] += 0.5 * jnp.sum(jnp.mean(e * e, axis=-1, keepdims=True), axis=0, keepdims=True)
        dy_ref[...] = e * (1.0 / d)

    return pl.pallas_call(
        body, name="loss_head",
        grid=(t // tm,),
        in_specs=[pl.BlockSpec((tm, d), lambda i: (i, 0)), pl.BlockSpec((tm, d), lambda i: (i, 0))],
        out_specs=[pl.BlockSpec((1, 1), lambda i: (0, 0)), pl.BlockSpec((tm, d), lambda i: (i, 0))],
        out_shape=[jax.ShapeDtypeStruct((1, 1), F32), jax.ShapeDtypeStruct((t, d), F32)],
        compiler_params=_params("arbitrary"),
    )(y, target)


def adamw(w, g, m, v):
    r, c = w.shape
    tr = r if r <= 512 else _tile(r, 256)

    def body(w_ref, g_ref, m_ref, v_ref, d_ref, nm_ref, nv_ref):
        gg = g_ref[...]
        nm = ADAM_B1 * m_ref[...] + (1.0 - ADAM_B1) * gg
        nv = ADAM_B2 * v_ref[...] + (1.0 - ADAM_B2) * (gg * gg)
        m_hat = nm / (1.0 - ADAM_B1 ** ADAM_STEP)
        v_hat = nv / (1.0 - ADAM_B2 ** ADAM_STEP)
        d_ref[...] = -ADAM_LR * (m_hat / (jnp.sqrt(v_hat) + ADAM_EPS) + ADAM_WD * w_ref[...])
        nm_ref[...] = nm
        nv_ref[...] = nv

    spec = pl.BlockSpec((tr, c), lambda i: (i, 0))
    shp = jax.ShapeDtypeStruct((r, c), F32)
    return pl.pallas_call(
        body, name="adamw",
        grid=(r // tr,),
        in_specs=[spec] * 4, out_specs=[spec] * 3, out_shape=[shp] * 3,
        compiler_params=_params("arbitrary"),
    )(w, g, m, v)


def _my_pos():
    return lax.axis_index("x"), lax.axis_index("y"), lax.axis_index("c")


def _other_chips(x, y):
    return [(1 - x, y), (x, 1 - y), (1 - x, 1 - y)]


def _chip_id(x, y):
    return 2 * x + y


HBM_SPEC = pl.BlockSpec(memory_space=pltpu.HBM)
SEM_SPEC = pl.BlockSpec(memory_space=pltpu.SEMAPHORE)
DATAFLOW = pltpu.SideEffectType.DATAFLOW_SIDE_EFFECTING


def _in_hbm(a):
    return pltpu.with_memory_space_constraint(a, pltpu.HBM)


def _ici_gather_copy(buf, p, j, chip, c, to, send_sems, recv_sems):
    blk = buf.at[_chip_id(*chip), c]
    return pltpu.make_async_remote_copy(
        src_ref=blk, dst_ref=blk, send_sem=send_sems.at[3 * p + j], recv_sem=recv_sems.at[3 * p + j],
        device_id=to, device_id_type=MESH)


def gather_start(bufs, after, tag):
    n = len(bufs)

    def body(*refs):
        ins = refs[:n]
        send_sems, recv_sems = refs[n + 1], refs[n + 2]
        token = refs[2 * n + 3]
        x, y, c = _my_pos()
        for p in range(n):
            for j, chip in enumerate(_other_chips(x, y)):
                _ici_gather_copy(ins[p], p, j, (x, y), c, (*chip, c), send_sems, recv_sems).start()
        token[...] = jnp.zeros_like(token)

    out = pl.pallas_call(
        body, name="gather_start_" + tag,
        in_specs=[HBM_SPEC] * n + [DEP_SPEC],
        out_specs=(SEM_SPEC, SEM_SPEC, *([HBM_SPEC] * n), pl.BlockSpec(memory_space=pltpu.VMEM)),
        out_shape=(pltpu.SemaphoreType.DMA((3 * n,)), pltpu.SemaphoreType.DMA((3 * n,)),
                   *[pltpu.HBM(b.shape, b.dtype) for b in bufs], jax.ShapeDtypeStruct((8, 128), F32)),
        input_output_aliases={p: 2 + p for p in range(n)},
        compiler_params=pltpu.CompilerParams(has_side_effects=DATAFLOW),
    )(*[_in_hbm(b) for b in bufs], after)
    return out[0], out[1], list(out[2:2 + n]), out[2 + n]


def gather_wait(send_sems, recv_sems, bufs, after, tag):
    n = len(bufs)

    def body(*refs):
        ins = refs[:n]
        send_sems, recv_sems = refs[n], refs[n + 1]
        x, y, c = _my_pos()
        for p in range(n):
            for j, chip in enumerate(_other_chips(x, y)):
                _ici_gather_copy(ins[p], p, j, (x, y), c, (*chip, c), send_sems, recv_sems).wait_send()
                _ici_gather_copy(ins[p], p, j, chip, c, (x, y, c), send_sems, recv_sems).wait_recv()

    out = pl.pallas_call(
        body, name="gather_wait_" + tag,
        in_specs=[HBM_SPEC] * n + [SEM_SPEC, SEM_SPEC, pl.BlockSpec(memory_space=pl.ANY)],
        out_specs=[HBM_SPEC] * n,
        out_shape=[pltpu.HBM(b.shape, b.dtype) for b in bufs],
        input_output_aliases={p: p for p in range(n)},
        compiler_params=pltpu.CompilerParams(has_side_effects=DATAFLOW),
    )(*bufs, send_sems, recv_sems, after)
    return list(out)


def forward_to_sibling(bufs):
    n = len(bufs)

    def body(*refs):
        outs = refs[n:2 * n]
        send_sems, recv_sems = refs[2 * n:]
        x, y, c = _my_pos()
        chips = _other_chips(x, y)
        sent = []
        for p in range(n):
            for j, chip in enumerate(chips):
                cp = _ici_gather_copy(outs[p], p, j, chip, c, (x, y, 1 - c), send_sems, recv_sems)
                cp.start()
                sent.append(cp)
        for p in range(n):
            for j, chip in enumerate(chips):
                _ici_gather_copy(outs[p], p, j, chip, 1 - c, (x, y, c), send_sems, recv_sems).wait_recv()
        for cp in sent:
            cp.wait_send()

    any_spec = pl.BlockSpec(memory_space=pl.ANY)
    return pl.pallas_call(
        body, name="forward_to_sibling",
        in_specs=[any_spec] * n, out_specs=[any_spec] * n,
        out_shape=[jax.ShapeDtypeStruct(b.shape, b.dtype) for b in bufs],
        scratch_shapes=[pltpu.SemaphoreType.DMA((3 * n,)), pltpu.SemaphoreType.DMA((3 * n,))],
        input_output_aliases={p: p for p in range(n)},
    )(*bufs)


def allreduce_small(v):
    r, c = v.shape

    def body(v_ref, o_ref, buf, send_sems, recv_sems):
        x, y, cc = _my_pos()
        me = 4 * x + 2 * y + cc
        buf[me] = v_ref[...]
        copies = []
        for k in range(1, 8):
            dx, dy, dc = (k >> 2) & 1, (k >> 1) & 1, k & 1
            peer = (x ^ dx, y ^ dy, cc ^ dc)
            cp = pltpu.make_async_remote_copy(
                src_ref=v_ref, dst_ref=buf.at[me], send_sem=send_sems.at[k - 1], recv_sem=recv_sems.at[k - 1],
                device_id=peer, device_id_type=MESH)
            cp.start()
            copies.append(cp)
        for cp in copies:
            cp.wait_recv()
        for cp in copies:
            cp.wait_send()
        acc = buf[0]
        for k in range(1, 8):
            acc = acc + buf[k]
        o_ref[...] = acc

    vm = pl.BlockSpec(memory_space=pltpu.VMEM)
    return pl.pallas_call(
        body, name="allreduce_small",
        in_specs=[vm], out_specs=vm,
        out_shape=jax.ShapeDtypeStruct((r, c), F32),
        scratch_shapes=[pltpu.VMEM((8, r, c), F32), pltpu.SemaphoreType.DMA((7,)), pltpu.SemaphoreType.DMA((7,))],
    )(v)


def exchange_sibling_halves(grads):
    n = len(grads)

    def body(*refs):
        ins, outs = refs[:n], refs[n:2 * n]
        send_sems, recv_sems = refs[2 * n:]
        x, y, c = _my_pos()
        copies = []
        for p in range(n):
            cp = pltpu.make_async_remote_copy(
                src_ref=ins[p].at[:, 1 - c], dst_ref=outs[p], send_sem=send_sems.at[p], recv_sem=recv_sems.at[p],
                device_id=(x, y, 1 - c), device_id_type=MESH)
            cp.start()
            copies.append(cp)
        for cp in copies:
            cp.wait_recv()
        for cp in copies:
            cp.wait_send()

    any_spec = pl.BlockSpec(memory_space=pl.ANY)
    return pl.pallas_call(
        body, name="exchange_sibling_halves",
        in_specs=[any_spec] * n, out_specs=[any_spec] * n,
        out_shape=[jax.ShapeDtypeStruct((g.shape[0],) + g.shape[2:], g.dtype) for g in grads],
        scratch_shapes=[pltpu.SemaphoreType.DMA((n,)), pltpu.SemaphoreType.DMA((n,))],
    )(*grads)


def _chip_partial_copy(part, land, p, j, chip, c, send_sems, recv_sems):
    return pltpu.make_async_remote_copy(
        src_ref=part.at[_chip_id(*chip)], dst_ref=land.at[j], send_sem=send_sems.at[3 * p + j],
        recv_sem=recv_sems.at[3 * p + j], device_id=(*chip, c), device_id_type=MESH)


def chip_partials_start(parts, tag):
    n = len(parts)
    lands = [lax.empty((N_CHIPS - 1,) + s.shape[1:], s.dtype) for s in parts]

    def body(*refs):
        ins, land = refs[:n], refs[n:2 * n]
        send_sems, recv_sems = refs[2 * n], refs[2 * n + 1]
        token = refs[4 * n + 2]
        x, y, c = _my_pos()
        for p in range(n):
            for j, chip in enumerate(_other_chips(x, y)):
                _chip_partial_copy(ins[p], land[p], p, j, chip, c, send_sems, recv_sems).start()
        token[...] = jnp.zeros_like(token)

    out = pl.pallas_call(
        body, name="chip_partials_start_" + tag,
        in_specs=[HBM_SPEC] * (2 * n),
        out_specs=(SEM_SPEC, SEM_SPEC, *([HBM_SPEC] * (2 * n)), pl.BlockSpec(memory_space=pltpu.VMEM)),
        out_shape=(pltpu.SemaphoreType.DMA((3 * n,)), pltpu.SemaphoreType.DMA((3 * n,)),
                   *[pltpu.HBM(a.shape, a.dtype) for a in parts + lands], jax.ShapeDtypeStruct((8, 128), F32)),
        input_output_aliases={i: 2 + i for i in range(2 * n)},
        compiler_params=pltpu.CompilerParams(has_side_effects=DATAFLOW),
    )(*[_in_hbm(a) for a in parts + lands])
    return out[0], out[1], list(out[2:2 + n]), list(out[2 + n:2 + 2 * n]), out[2 + 2 * n]


def chip_partials_wait(send_sems, recv_sems, parts, lands, after, tag):
    n = len(parts)

    def body(*refs):
        ins, land = refs[:n], refs[n:2 * n]
        send_sems, recv_sems = refs[2 * n], refs[2 * n + 1]
        x, y, c = _my_pos()
        for p in range(n):
            for j, chip in enumerate(_other_chips(x, y)):
                cp = _chip_partial_copy(ins[p], land[p], p, j, chip, c, send_sems, recv_sems)
                cp.wait_send()
                cp.wait_recv()

    out = pl.pallas_call(
        body, name="chip_partials_wait_" + tag,
        in_specs=[HBM_SPEC] * (2 * n) + [SEM_SPEC, SEM_SPEC, pl.BlockSpec(memory_space=pl.ANY)],
        out_specs=[HBM_SPEC] * (2 * n),
        out_shape=[pltpu.HBM(a.shape, a.dtype) for a in parts + lands],
        input_output_aliases={i: i for i in range(2 * n)},
        compiler_params=pltpu.CompilerParams(has_side_effects=DATAFLOW),
    )(*parts, *lands, send_sems, recv_sems, after)
    return list(out[n:])


def share_with_sibling(bufs):
    n = len(bufs)

    def body(*refs):
        outs = refs[n:2 * n]
        send_sems, recv_sems = refs[2 * n:]
        x, y, c = _my_pos()
        copies = []
        for p in range(n):
            cp = pltpu.make_async_remote_copy(
                src_ref=outs[p].at[c], dst_ref=outs[p].at[c], send_sem=send_sems.at[p], recv_sem=recv_sems.at[p],
                device_id=(x, y, 1 - c), device_id_type=MESH)
            cp.start()
            copies.append(cp)
        for p in range(n):
            pltpu.make_async_remote_copy(
                src_ref=outs[p].at[1 - c], dst_ref=outs[p].at[1 - c], send_sem=send_sems.at[p],
                recv_sem=recv_sems.at[p], device_id=(x, y, 1 - c), device_id_type=MESH).wait_recv()
        for cp in copies:
            cp.wait_send()

    any_spec = pl.BlockSpec(memory_space=pl.ANY)
    return pl.pallas_call(
        body, name="share_with_sibling",
        in_specs=[any_spec] * n, out_specs=[any_spec] * n,
        out_shape=[jax.ShapeDtypeStruct(b.shape, b.dtype) for b in bufs],
        scratch_shapes=[pltpu.SemaphoreType.DMA((n,)), pltpu.SemaphoreType.DMA((n,))],
        input_output_aliases={p: p for p in range(n)},
    )(*bufs)


def add_sibling(g, recv, half):
    _, _, r, c = g.shape
    tr = _tile(r, 256) if r % 256 == 0 else r

    def body(half_ref, g_ref, r_ref, o32_ref, o16_ref):
        s = g_ref[...] + r_ref[...]
        o32_ref[...] = s
        o16_ref[...] = _b(s)

    return pl.pallas_call(
        body, name="add_sibling",
        grid_spec=pltpu.PrefetchScalarGridSpec(
            num_scalar_prefetch=1, grid=(N_CHIPS, r // tr),
            in_specs=[pl.BlockSpec((None, None, tr, c), lambda k, i, hf: (k, hf[0], i, 0)),
                      pl.BlockSpec((None, tr, c), lambda k, i, hf: (k, i, 0))],
            out_specs=[pl.BlockSpec((None, tr, c), lambda k, i, hf: (k, i, 0)),
                       pl.BlockSpec((None, tr, c), lambda k, i, hf: (k, i, 0))]),
        out_shape=[jax.ShapeDtypeStruct((N_CHIPS, r, c), F32), jax.ShapeDtypeStruct((N_CHIPS, r, c), BF16)],
        compiler_params=_params("arbitrary", "arbitrary"),
    )(half, g, recv)


def add_chip_partials(p32, recv, pos):
    _, r, c = p32.shape
    tr = _tile(r, 256) if r % 256 == 0 else r

    def body(pos_ref, p_ref, r_ref, o_ref):
        acc = p_ref[...]
        for j in range(N_CHIPS - 1):
            acc = acc + r_ref[j].astype(F32)
        o_ref[...] = acc

    return pl.pallas_call(
        body, name="add_chip_partials",
        grid_spec=pltpu.PrefetchScalarGridSpec(
            num_scalar_prefetch=1, grid=(r // tr,),
            in_specs=[pl.BlockSpec((None, tr, c), lambda i, ps: (ps[0], i, 0)),
                      pl.BlockSpec((N_CHIPS - 1, tr, c), lambda i, ps: (0, i, 0))],
            out_specs=pl.BlockSpec((None, tr, c), lambda i, ps: (ps[1], i, 0))),
        out_shape=jax.ShapeDtypeStruct((2, r, c), F32),
        compiler_params=_params("arbitrary"),
    )(pos, p32, recv)


def cast_into_gather(w, pos, row0=0, nrows=None):
    c = w.shape[1]
    nrows = w.shape[0] if nrows is None else nrows
    r = nrows // 2
    common = math.gcd(r, row0) if row0 else r
    tr = max(w for w in range(16, min(common, 512) + 1, 16) if common % w == 0)
    nt = r // tr

    def body(pos_ref, w_ref, o_ref):
        o_ref[...] = _b(w_ref[...])

    return pl.pallas_call(
        body, name="cast_into_gather",
        grid_spec=pltpu.PrefetchScalarGridSpec(
            num_scalar_prefetch=1, grid=(2, nt),
            in_specs=[pl.BlockSpec((tr, c), lambda hf, i, ps: (row0 // tr + hf * nt + i, 0))],
            out_specs=pl.BlockSpec((None, None, tr, c), lambda hf, i, ps: (ps[0], hf, i, 0))),
        out_shape=jax.ShapeDtypeStruct((N_CHIPS, 2, r, c), BF16),
        compiler_params=_params("arbitrary", "arbitrary"),
    )(pos, w)


def build_bias(rel, buckets):
    nb, nh = rel.shape

    def body(rel_ref, bk_ref, o_ref):
        bk = bk_ref[...]
        for h in range(nh):
            acc = jnp.zeros(bk.shape, F32)
            for b in range(nb):
                acc = jnp.where(bk == b, rel_ref[b, h], acc)
            o_ref[h] = acc

    return pl.pallas_call(
        body, name="build_bias",
        in_specs=[pl.BlockSpec(memory_space=pltpu.SMEM), pl.BlockSpec(memory_space=pltpu.VMEM)],
        out_specs=pl.BlockSpec(memory_space=pltpu.VMEM),
        out_shape=jax.ShapeDtypeStruct((nh,) + buckets.shape, F32),
        compiler_params=_params(),
    )(rel, buckets)


SMALL_ROWS = 256


def kernel(x, ffn_norm, ffn_w1, ffn_w3, ffn_w2, ssm_norm, ssm_w_in, ssm_conv_w, ssm_conv_b, ssm_dt_bias, ssm_a_log, ssm_d, ssm_gate_norm, ssm_w_out, kv_norm, w_kv, k_norm, attn_norm, w_q, q_norm, sinks, w_o, rel_bias, loss_target, m_ffn_norm, m_ffn_w1, m_ffn_w3, m_ffn_w2, m_ssm_norm, m_ssm_w_in, m_ssm_conv_w, m_ssm_conv_b, m_ssm_dt_bias, m_ssm_a_log, m_ssm_d, m_ssm_gate_norm, m_ssm_w_out, m_kv_norm, m_w_kv, m_k_norm, m_attn_norm, m_w_q, m_q_norm, m_sinks, m_w_o, m_rel_bias, v_ffn_norm, v_ffn_w1, v_ffn_w3, v_ffn_w2, v_ssm_norm, v_ssm_w_in, v_ssm_conv_w, v_ssm_conv_b, v_ssm_dt_bias, v_ssm_a_log, v_ssm_d, v_ssm_gate_norm, v_ssm_w_out, v_kv_norm, v_w_kv, v_k_norm, v_attn_norm, v_w_q, v_q_norm, v_sinks, v_w_o, v_rel_bias):
    weights = dict(ffn_norm=ffn_norm, ffn_w1=ffn_w1, ffn_w3=ffn_w3, ffn_w2=ffn_w2, ssm_norm=ssm_norm,
                   ssm_w_in=ssm_w_in, ssm_conv_w=ssm_conv_w, ssm_conv_b=ssm_conv_b, ssm_dt_bias=ssm_dt_bias,
                   ssm_a_log=ssm_a_log, ssm_d=ssm_d, ssm_gate_norm=ssm_gate_norm, ssm_w_out=ssm_w_out,
                   kv_norm=kv_norm, w_kv=w_kv, k_norm=k_norm, attn_norm=attn_norm, w_q=w_q, q_norm=q_norm,
                   sinks=sinks, w_o=w_o, rel_bias=rel_bias)
    m_in = dict(ffn_norm=m_ffn_norm, ffn_w1=m_ffn_w1, ffn_w3=m_ffn_w3, ffn_w2=m_ffn_w2, ssm_norm=m_ssm_norm,
                ssm_w_in=m_ssm_w_in, ssm_conv_w=m_ssm_conv_w, ssm_conv_b=m_ssm_conv_b, ssm_dt_bias=m_ssm_dt_bias,
                ssm_a_log=m_ssm_a_log, ssm_d=m_ssm_d, ssm_gate_norm=m_ssm_gate_norm, ssm_w_out=m_ssm_w_out,
                kv_norm=m_kv_norm, w_kv=m_w_kv, k_norm=m_k_norm, attn_norm=m_attn_norm, w_q=m_w_q, q_norm=m_q_norm,
                sinks=m_sinks, w_o=m_w_o, rel_bias=m_rel_bias)
    v_in = dict(ffn_norm=v_ffn_norm, ffn_w1=v_ffn_w1, ffn_w3=v_ffn_w3, ffn_w2=v_ffn_w2, ssm_norm=v_ssm_norm,
                ssm_w_in=v_ssm_w_in, ssm_conv_w=v_ssm_conv_w, ssm_conv_b=v_ssm_conv_b, ssm_dt_bias=v_ssm_dt_bias,
                ssm_a_log=v_ssm_a_log, ssm_d=v_ssm_d, ssm_gate_norm=v_ssm_gate_norm, ssm_w_out=v_ssm_w_out,
                kv_norm=v_kv_norm, w_kv=v_w_kv, k_norm=v_k_norm, attn_norm=v_attn_norm, w_q=v_w_q, q_norm=v_q_norm,
                sinks=v_sinks, w_o=v_w_o, rel_bias=v_rel_bias)
    return _step(x[0], loss_target[0], weights, m_in, v_in)


BIG = ("ffn_w1", "ffn_w3", "ffn_w2", "ssm_w_in", "ssm_w_out", "w_kv", "w_q", "w_o")
SMALL = (("ffn_norm", True), ("ssm_norm", True), ("ssm_conv_w", True), ("ssm_conv_b", True),
         ("ssm_gate_norm", True), ("ssm_dt_bias", False), ("ssm_a_log", False), ("ssm_d", False),
         ("kv_norm", False), ("k_norm", False), ("attn_norm", False), ("q_norm", False), ("sinks", False),
         ("rel_bias", False))


FFN_W = BIG[:3]


def _small_layout(weights):
    off, table = 0, {}
    for name, sharded in SMALL:
        shape = weights[name].shape
        full = shape[:-1] + (shape[-1] * N_CHIPS,) if sharded else shape
        n = int(np.prod(full))
        table[name] = (off, full, sharded)
        off += n
    assert off <= SMALL_ROWS * 128
    return table


def _place_small(values, table, chip, scale_mask):
    flat = jnp.zeros((SMALL_ROWS * 128,), F32)
    for name, (off, full, sharded) in table.items():
        if not sharded:
            continue
        v = values[name].astype(F32)
        lead = int(np.prod(full[:-1]))
        w = v.shape[-1]
        blk = jnp.zeros((lead, full[-1]), F32)
        blk = lax.dynamic_update_slice(blk, v.reshape(lead, w) * scale_mask, (0, chip * w))
        flat = lax.dynamic_update_slice(flat, blk.reshape(-1), (off,))
    return flat.reshape(SMALL_ROWS, 128)


def _take_small(mat, table, name):
    off, full, _ = table[name]
    n = int(np.prod(full))
    return mat.reshape(-1)[off:off + n].reshape(full)


def _step(x, target, weights, m_in, v_in):
    t, d = x.shape
    xi, yi, ci = lax.axis_index("x"), lax.axis_index("y"), lax.axis_index("c")
    chip = 2 * xi + yi
    pos_arr = jnp.stack([chip, ci]).astype(jnp.int32)
    half_arr = jnp.reshape(ci, (1,)).astype(jnp.int32)

    fs = weights["ffn_w1"].shape[-1]
    ffn_rows = {"ffn_w1": d, "ffn_w3": d, "ffn_w2": fs}
    w2d = {n: weights[n].reshape(-1, weights[n].shape[-1]) for n in BIG}
    mamba_w = ("ssm_w_in", "ssm_w_out")
    late_w = ("w_kv", "w_q", "w_o")
    fs_, fr_, fbufs, tok_f = gather_start(
        [cast_into_gather(w2d[n], pos_arr, 0, ffn_rows[n]) for n in FFN_W], pos_arr, "first")
    ms, mr, mbufs, tok_m = gather_start([cast_into_gather(w2d[n], pos_arr) for n in mamba_w], tok_f, "mamba")
    ls, lr, lbufs, tok_l = gather_start(
        [cast_into_gather(w2d[n], pos_arr, ffn_rows[n], 3 * ffn_rows[n]) for n in FFN_W]
        + [cast_into_gather(w2d[n], pos_arr) for n in late_w], tok_m, "late")
    first = forward_to_sibling(gather_wait(fs_, fr_, fbufs, tok_l, "first"))
    no_dep = jnp.zeros((8, 128), F32)
    table = _small_layout(weights)
    south = (ci == 0).astype(F32)
    small = allreduce_small(_place_small(weights, table, chip, south))
    sp = {n: _take_small(small, table, n) if sh else weights[n] for n, sh in SMALL}

    ffn_first = [first[0].reshape(N_CHIPS, 1, d, fs), first[1].reshape(N_CHIPS, 1, d, fs),
                 first[2].reshape(N_CHIPS, 1, fs, d)]
    ffn_g = sp["ffn_norm"]
    h0 = x
    h1, a00, b00 = ffn_fwd(h0, ffn_g[0, 0].reshape(1, d), *ffn_first, 0, no_dep)
    gathered = dict(zip(mamba_w, forward_to_sibling(gather_wait(ms, mr, mbufs, h1, "mamba"))))
    n_in = weights["ssm_w_in"].shape[-1] * N_CHIPS
    di = weights["ssm_w_out"].shape[1] * N_CHIPS
    nheads = di // SSM_HEAD_DIM
    conv_dim = n_in - di - nheads
    w_in_full = jnp.moveaxis(gathered["ssm_w_in"].reshape(N_CHIPS, d, n_in // N_CHIPS), 0, 1).reshape(d, n_in)
    hpg = nheads // SSM_GROUPS

    def spread_heads(v):
        lead = v.shape[:-1]
        v = v.reshape(lead + (SSM_GROUPS, hpg))
        v = jnp.pad(v, [(0, 0)] * len(lead) + [(0, 0), (0, 128 - hpg)])
        return v.reshape(lead + (SSM_GROUPS * 128,))

    def gather_heads(v):
        lead = v.shape[:-1]
        return v.reshape(lead + (SSM_GROUPS, 128))[..., :hpg].reshape(lead + (nheads,))

    dt_col0 = di + conv_dim
    n_zx = dt_col0 + SSM_GROUPS * 128
    w_in = jnp.concatenate([w_in_full[:, :dt_col0], spread_heads(w_in_full[:, dt_col0:])], axis=1)
    w_out = gathered["ssm_w_out"].reshape(di, d)
    nkv = weights["w_kv"].shape[1] // (2 * ATT_HEAD_DIM)
    assert nkv == 2
    nh = weights["w_q"].shape[-1] // ATT_HEAD_DIM

    ssm_g = sp["ssm_norm"].reshape(1, d)
    cw = jnp.pad(sp["ssm_conv_w"].reshape(SSM_CONV, conv_dim), [(0, 8 - SSM_CONV), (0, 0)])
    cb = sp["ssm_conv_b"].reshape(1, conv_dim)
    gate_g = sp["ssm_gate_norm"].reshape(1, di)
    dt_bias = spread_heads(sp["ssm_dt_bias"].reshape(1, nheads))
    a_log = spread_heads(sp["ssm_a_log"].reshape(1, nheads))
    d_skip = spread_heads(sp["ssm_d"].reshape(1, nheads))
    kv_g = sp["kv_norm"].reshape(1, d)
    k_g = jnp.tile(sp["k_norm"].reshape(1, ATT_HEAD_DIM), (1, 2))
    attn_g = sp["attn_norm"].reshape(1, d)
    q_g = jnp.tile(sp["q_norm"].reshape(1, ATT_HEAD_DIM), (1, 2))
    sink_row = jnp.pad(sp["sinks"].reshape(1, nh), [(0, 0), (0, 128 - nh)])
    buckets = jnp.asarray(_t5_buckets())
    biasm = build_bias(sp["rel_bias"], buckets).reshape(nh * ATT_WINDOW, 2 * ATT_WINDOW)

    zx = norm_mm(h1, ssm_g, w_in)
    xc = conv_fwd(zx, cw, cb, di)
    y_ssd, states = ssd_fwd(xc, zx, dt_bias, a_log, d_skip, dt_col0)
    h2 = gate_out_fwd(h1, y_ssd, zx, gate_g, w_out)

    late = forward_to_sibling(gather_wait(ls, lr, lbufs, h2, "late"))
    ffn_rest = [late[0].reshape(N_CHIPS, 3, d, fs), late[1].reshape(N_CHIPS, 3, d, fs),
                late[2].reshape(N_CHIPS, 3, fs, d)]
    gathered.update(zip(late_w, late[3:]))
    wkv_heads = gathered["w_kv"].reshape(d, 2 * nkv, 1, ATT_HEAD_DIM)
    w_kvd = jnp.broadcast_to(wkv_heads, (d, 2 * nkv, 2, ATT_HEAD_DIM)).reshape(d, 4 * nkv * ATT_HEAD_DIM)
    wq = gathered["w_q"].reshape(d, -1)
    wo = gathered["w_o"].reshape(-1, d)

    def ffn_w(layer, idx):
        blk = 2 * layer + idx
        return (*ffn_first, 0) if blk == 0 else (*ffn_rest, blk - 1)

    h3, a01, b01 = ffn_fwd(h2, ffn_g[0, 1].reshape(1, d), *ffn_w(0, 1), no_dep)
    kvd = norm_mm(h3, kv_g, w_kvd)
    h4, a10, b10 = ffn_fwd(h3, ffn_g[1, 0].reshape(1, d), *ffn_w(1, 0), no_dep)
    qp = norm_mm(h4, attn_g, wq)
    h5 = attn_fwd(h4, qp, kvd, biasm, sink_row, q_g, k_g, wo)
    h6, a11, b11 = ffn_fwd(h5, ffn_g[1, 1].reshape(1, d), *ffn_w(1, 1), no_dep)
    loss_part, d6 = loss_head(h6, target)
    loss = lax.psum(loss_part[0, 0], ("x", "y", "c"))

    gfn = [[None, None], [None, None]]

    pending = []

    def reduce_start(pieces, tag):
        views = [g.reshape(N_CHIPS, 2, g.shape[1] // 2, g.shape[2]) for _, g in pieces]
        recv1 = exchange_sibling_halves(views)
        p32, p16 = zip(*[add_sibling(g, r, half_arr) for g, r in zip(views, recv1)])
        ss, rs, parts, lands, token = chip_partials_start(list(p16), tag)
        pending.append(([k for k, _ in pieces], p32, ss, rs, parts, lands, tag))
        return token

    def ffn_back(h_in, dy, a_s, b_s, layer, idx, dep):
        dh, u, da, db, s, dg = ffn_bwd(h_in, dy, ffn_g[layer, idx].reshape(1, d), a_s, b_s, *ffn_w(layer, idx), dep)
        gfn[layer][idx] = dg
        return dh, [(("ffn_w1", layer, idx), wgrad_grouped_b(u, da)), (("ffn_w3", layer, idx), wgrad_grouped_b(u, db)),
                    (("ffn_w2", layer, idx), wgrad_grouped_a(s, dy, 0.5))]

    d5, pieces = ffn_back(h5, d6, a11, b11, 1, 1, no_dep)
    tok = reduce_start(pieces, "ffn11")
    dqp, dkvd, o16, dbiasm, dsinks, dqg, dkg = attn_bwd(d5, qp, kvd, biasm, sink_row, q_g, k_g, wo, tok)
    g_wo = wgrad(o16, d5)
    d4, u_q, g_attn_norm = norm_mm_bwd(h4, attn_g, wq, dqp, d5, no_dep)
    g_wq = wgrad(u_q, dqp)
    d3a, pieces = ffn_back(h3, d4, a10, b10, 1, 0, no_dep)
    pieces += [(("w_o",), g_wo.reshape(N_CHIPS, -1, d)), (("w_q",), g_wq.reshape(N_CHIPS, d // N_CHIPS, -1))]
    tok = reduce_start(pieces, "ffn10")
    d3, u_kv, g_kv_norm = norm_mm_bwd(h3, kv_g, w_kvd, dkvd, d3a, tok, 0.5)
    g_wkvd = wgrad(u_kv, dkvd)
    g_wkv = g_wkvd.reshape(d, 2 * nkv, 2, ATT_HEAD_DIM)[:, :, 0, :].reshape(d, 2 * nkv * ATT_HEAD_DIM)
    d2, pieces = ffn_back(h2, d3, a01, b01, 0, 1, no_dep)
    pieces += [(("w_kv",), g_wkv.reshape(N_CHIPS, d // N_CHIPS, -1))]
    tok = reduce_start(pieces, "ffn01")
    dzx, dy_ssd, yn16, g_gate = gate_out_bwd(d2, y_ssd, zx, gate_g, w_out, n_zx, tok)
    g_wout = wgrad(yn16, d2)
    dzx, dxs, dbm, dcm, g_dtb, g_alog, g_dsk = ssd_bwd(dzx, dy_ssd, xc, zx, states, dt_bias, a_log, d_skip, dt_col0)
    dzx, g_cw, g_cb = conv_bwd(dzx, zx, dxs, dbm, dcm, cw, cb, di)
    d1, u_in, g_ssm_norm = norm_mm_bwd(h1, ssm_g, w_in, dzx, d2, no_dep)
    g_win = wgrad(u_in, dzx)
    g_win_full = jnp.concatenate([g_win[:, :dt_col0], gather_heads(g_win[:, dt_col0:])], axis=1)
    pieces = [(("ssm_w_in",), jnp.moveaxis(g_win_full.reshape(d, N_CHIPS, n_in // N_CHIPS), 1, 0)),
              (("ssm_w_out",), g_wout.reshape(N_CHIPS, di // N_CHIPS, d))]
    tok = reduce_start(pieces, "mamba")
    grad_x, pieces = ffn_back(h0, d1, a00, b00, 0, 0, tok)
    tok = reduce_start(pieces, "ffn00")
    g_relb = rel_bias_bwd(dbiasm.reshape(nh, ATT_WINDOW, 2 * ATT_WINDOW), buckets)

    reduced = {}
    for keys, p32, ss, rs, parts, lands, tag in pending:
        lands = chip_partials_wait(ss, rs, parts, lands, tok, tag)
        for k, p, r in zip(keys, p32, lands):
            reduced[k] = add_chip_partials(p, r, pos_arr)
    keys = list(reduced)
    shared = dict(zip(keys, share_with_sibling([reduced[k] for k in keys])))
    grads = {}
    for n in FFN_W:
        blocks = [shared[(n, l, i)].reshape(1, ffn_rows[n], -1) for l in range(2) for i in range(2)]
        grads[n] = jnp.concatenate(blocks, axis=0).reshape(weights[n].shape)
    for n in BIG[3:]:
        grads[n] = shared[(n,)].reshape(weights[n].shape)

    small_grads = {
        "ffn_norm": jnp.stack([jnp.stack([gfn[l][i].reshape(d) for i in range(2)]) for l in range(2)]),
        "ssm_norm": g_ssm_norm.reshape(1, d),
        "ssm_conv_w": g_cw[:SSM_CONV].reshape(1, SSM_CONV, conv_dim),
        "ssm_conv_b": g_cb.reshape(1, conv_dim),
        "ssm_gate_norm": g_gate.reshape(1, di),
        "ssm_dt_bias": gather_heads(g_dtb.reshape(1, -1)), "ssm_a_log": gather_heads(g_alog.reshape(1, -1)),
        "ssm_d": gather_heads(g_dsk.reshape(1, -1)),
        "kv_norm": g_kv_norm.reshape(d), "k_norm": dkg[0, :ATT_HEAD_DIM], "attn_norm": g_attn_norm.reshape(1, d),
        "q_norm": dqg[:, :ATT_HEAD_DIM], "sinks": dsinks[:, :nh], "rel_bias": g_relb[:, :nh],
    }
    flat = jnp.zeros((SMALL_ROWS * 128,), F32)
    for name, (off, fshape, _) in table.items():
        flat = lax.dynamic_update_slice(flat, small_grads[name].astype(F32).reshape(-1), (off,))
    small_sum = allreduce_small(flat.reshape(SMALL_ROWS, 128))
    for name, (off, fshape, sharded) in table.items():
        g = _take_small(small_sum, table, name)
        if sharded:
            w = weights[name].shape[-1]
            lead = int(np.prod(fshape[:-1]))
            g = lax.dynamic_slice(g.reshape(lead, fshape[-1]), (0, chip * w), (lead, w)).reshape(weights[name].shape)
        grads[name] = g.reshape(weights[name].shape)

    names = list(weights)
    deltas, new_m, new_v = {}, {}, {}
    small_names = [n for n, _ in SMALL]
    for n in BIG:
        shp = weights[n].shape
        v2 = lambda a: a.reshape(-1, shp[-1])
        dl, nm, nv = adamw(v2(weights[n]), v2(grads[n]), v2(m_in[n]), v2(v_in[n]))
        deltas[n], new_m[n], new_v[n] = dl.reshape(shp), nm.reshape(shp), nv.reshape(shp)
    sizes = [int(np.prod(weights[n].shape)) for n in small_names]
    tot = sum(sizes)
    rows = -(-tot // 128)
    rows = -(-rows // 8) * 8

    def pack(dct):
        flat = jnp.concatenate([dct[n].reshape(-1) for n in small_names])
        return jnp.pad(flat, (0, rows * 128 - tot), constant_values=1.0).reshape(rows, 128)

    dl, nm, nv = adamw(pack(weights), pack(grads), pack(m_in), pack(v_in))
    off = 0
    for n, sz in zip(small_names, sizes):
        shp = weights[n].shape
        take = lambda a: a.reshape(-1)[off:off + sz].reshape(shp)
        deltas[n], new_m[n], new_v[n] = take(dl), take(nm), take(nv)
        off += sz

    return (loss, grad_x[None], *[grads[n] for n in names], *[deltas[n] for n in names],
            *[new_m[n] for n in names], *[new_v[n] for n in names])
```

```python
import functools
import math

import jax
import jax.numpy as jnp
import numpy as np
from jax import lax
from jax.experimental import pallas as pl
from jax.experimental.pallas import tpu as pltpu

F32 = jnp.float32
BF16 = jnp.bfloat16
EPS = 1e-6
MESH = pl.DeviceIdType.MESH

SSM_HEAD_DIM = 64
SSM_GROUPS = 4
SSM_STATE = 128
SSM_CONV = 4
SSM_CHUNK = 256
ATT_HEAD_DIM = 64
ATT_WINDOW = 128
REL_BUCKETS = 32
N_CHIPS = 4

ADAM_LR = 0.001
ADAM_B1 = 0.9
ADAM_B2 = 0.999
ADAM_EPS = 1e-08
ADAM_WD = 0.01
ADAM_STEP = 10

VMEM_LIMIT_BYTES = 56 * 1024 * 1024
NEG = -1e30


DEP_SPEC = pl.BlockSpec(memory_space=pl.ANY)


def _params(*sem):
    return pltpu.CompilerParams(dimension_semantics=sem if sem else None, vmem_limit_bytes=VMEM_LIMIT_BYTES)


def _dot(a, b):
    return jnp.dot(a, b, preferred_element_type=F32)


def _dot_nt(a, b):
    return lax.dot_general(a, b, (((1,), (1,)), ((), ())), preferred_element_type=F32)


def _dot_tn(a, b):
    return lax.dot_general(a, b, (((0,), (0,)), ((), ())), preferred_element_type=F32)


def _b(x):
    return x.astype(BF16)


@jax.custom_vjp
def _bmm(a, b):
    return _dot(_b(a), _b(b))


def _bmm_fwd(a, b):
    return _bmm(a, b), (a, b)


def _bmm_bwd(res, g):
    a, b = res
    g16 = _b(g)
    return _dot_nt(g16, _b(b)).astype(a.dtype), _dot_tn(_b(a), g16).astype(b.dtype)


_bmm.defvjp(_bmm_fwd, _bmm_bwd)


@jax.custom_vjp
def _bmm_nt(a, b):
    return _dot_nt(_b(a), _b(b))


def _bmm_nt_fwd(a, b):
    return _bmm_nt(a, b), (a, b)


def _bmm_nt_bwd(res, g):
    a, b = res
    g16 = _b(g)
    return _dot(g16, _b(b)).astype(a.dtype), _dot_tn(g16, _b(a)).astype(b.dtype)


_bmm_nt.defvjp(_bmm_nt_fwd, _bmm_nt_bwd)


@jax.custom_vjp
def _bmm_tn(a, b):
    return _dot_tn(_b(a), _b(b))


def _bmm_tn_fwd(a, b):
    return _bmm_tn(a, b), (a, b)


def _bmm_tn_bwd(res, g):
    a, b = res
    g16 = _b(g)
    return _dot_nt(_b(b), g16).astype(a.dtype), _dot(_b(a), g16).astype(b.dtype)


_bmm_tn.defvjp(_bmm_tn_fwd, _bmm_tn_bwd)


def _split3(x):
    hi = _b(x)
    r = x - hi.astype(F32)
    mid = _b(r)
    lo = _b(r - mid.astype(F32))
    return hi, mid, lo


def _x_left_raw(m, x):
    hi, mid, lo = _split3(x)
    return _dot(m, hi) + _dot(m, mid) + _dot(m, lo)


def _x_left_t_raw(m, x):
    hi, mid, lo = _split3(x)
    return _dot_tn(m, hi) + _dot_tn(m, mid) + _dot_tn(m, lo)


def _x_right_raw(x, m):
    hi, mid, lo = _split3(x)
    return _dot(hi, m) + _dot(mid, m) + _dot(lo, m)


def _x_right_t_raw(x, m):
    hi, mid, lo = _split3(x)
    return _dot_nt(hi, m) + _dot_nt(mid, m) + _dot_nt(lo, m)


@jax.custom_vjp
def _xleft(m, x):
    return _x_left_raw(m, x)


_xleft.defvjp(lambda m, x: (_x_left_raw(m, x), m),
              lambda m, g: (jnp.zeros_like(m), _x_left_t_raw(m, g)))


@jax.custom_vjp
def _xright(x, m):
    return _x_right_raw(x, m)


_xright.defvjp(lambda x, m: (_x_right_raw(x, m), m),
               lambda m, g: (_x_right_t_raw(g, m), jnp.zeros_like(m)))


def _sigmoid(x):
    return 1.0 / (1.0 + jnp.exp(-x))


def _silu(x):
    return x * _sigmoid(x)


def _softplus(x):
    return jnp.maximum(x, 0.0) + jnp.log(1.0 + jnp.exp(-jnp.abs(x)))


def _rms(x):
    return x * lax.rsqrt(jnp.mean(x * x, axis=-1, keepdims=True) + EPS)


def _iota(shape, dim):
    return lax.broadcasted_iota(jnp.int32, shape, dim)


def _blockdiag64(n):
    return jnp.where(_iota((n, n), 0) // 64 == _iota((n, n), 1) // 64, 1.0, 0.0).astype(BF16)


def _group64_rms(x, seg_sum):
    ms = seg_sum(x * x) * (1.0 / 64.0)
    return x * lax.rsqrt(ms + EPS)


def _fold64(x):
    ax = x.ndim - 1
    w = x.shape[ax]
    lo = (_iota(x.shape, ax) % 128) < 64
    return x + jnp.where(lo, pltpu.roll(x, w - 64, ax), pltpu.roll(x, 64, ax))


def _tile(n, want):
    t = min(n, want)
    assert n % t == 0, (n, t)
    return t


def _lane_tile(n, cap=1536):
    if n <= cap:
        return n
    return max(w for w in range(128, cap + 1, 128) if n % w == 0)


def ffn_fwd(h, g, w1, w3, w2, blk, dep):
    t, d = h.shape
    nk, fs = w1.shape[0], w1.shape[-1]
    tm = _tile(t, 512)

    def body(h_ref, g_ref, w1_ref, w3_ref, w2_ref, dep_ref, o_ref, a_ref, b_ref, u_scr, acc):
        k = pl.program_id(1)

        @pl.when(k == 0)
        def _():
            u_scr[...] = _b(_rms(h_ref[...]) * g_ref[...])
            acc[...] = jnp.zeros_like(acc)

        u = u_scr[...]
        a = _dot(u, w1_ref[...])
        b = _dot(u, w3_ref[...])
        a_ref[...] = _b(a)
        b_ref[...] = _b(b)
        acc[...] += _dot(_b(_silu(a) * b), w2_ref[...])

        @pl.when(k == nk - 1)
        def _():
            o_ref[...] = h_ref[...] + 0.5 * acc[...]

    wspec = lambda r, c: pl.BlockSpec((None, None, r, c), lambda i, k: (k, blk, 0, 0))
    return pl.pallas_call(
        body, name="ffn_fwd",
        grid=(t // tm, nk),
        in_specs=[pl.BlockSpec((tm, d), lambda i, k: (i, 0)), pl.BlockSpec((1, d), lambda i, k: (0, 0)),
                  wspec(d, fs), wspec(d, fs), wspec(fs, d), DEP_SPEC],
        out_specs=[pl.BlockSpec((tm, d), lambda i, k: (i, 0)),
                   pl.BlockSpec((None, tm, fs), lambda i, k: (k, i, 0)),
                   pl.BlockSpec((None, tm, fs), lambda i, k: (k, i, 0))],
        out_shape=[jax.ShapeDtypeStruct((t, d), F32), jax.ShapeDtypeStruct((nk, t, fs), BF16),
                   jax.ShapeDtypeStruct((nk, t, fs), BF16)],
        scratch_shapes=[pltpu.VMEM((tm, d), BF16), pltpu.VMEM((tm, d), F32)],
        compiler_params=_params("arbitrary", "arbitrary"),
    )(h, g, w1, w3, w2, dep)


def ffn_bwd(h, dy, g, a_s, b_s, w1, w3, w2, blk, dep):
    t, d = h.shape
    nk, fs = w1.shape[0], w1.shape[-1]
    tm = _tile(t, 512)

    def body(h_ref, dy_ref, g_ref, a_ref, b_ref, w1_ref, w3_ref, w2_ref, dep_ref,
             dh_ref, u_ref, da_ref, db_ref, s_ref, dg_ref, dyh_scr, du_acc, da0, db0, da1, db1):
        i, k = pl.program_id(0), pl.program_id(1)

        @pl.when(k == 0)
        def _():
            dyh_scr[...] = _b(0.5 * dy_ref[...])
            du_acc[...] = jnp.zeros_like(du_acc)

        @pl.when((k == 0) & (i == 0))
        def _():
            dg_ref[...] = jnp.zeros_like(dg_ref)

        def step(prev, cur):
            if prev is not None:
                du_acc[...] += _dot_nt(prev[0][...], w1_ref[...]) + _dot_nt(prev[1][...], w3_ref[...])
            if cur is not None:
                ds = _dot_nt(dyh_scr[...], w2_ref[...])
                a = a_ref[...].astype(F32)
                b = b_ref[...].astype(F32)
                sig = _sigmoid(a)
                sl = a * sig
                s_ref[...] = _b(sl * b)
                da = _b(ds * b * (sig * (1.0 + a * (1.0 - sig))))
                db = _b(ds * sl)
                da_ref[...] = da
                db_ref[...] = db
                cur[0][...] = da
                cur[1][...] = db

        even, odd = (da0, db0), (da1, db1)

        @pl.when(k == 0)
        def _():
            step(None, even)

        @pl.when((k > 0) & (k < nk) & (k % 2 == 1))
        def _():
            step(even, odd)

        @pl.when((k > 0) & (k < nk) & (k % 2 == 0))
        def _():
            step(odd, even)

        @pl.when(k == nk)
        def _():
            step(odd if nk % 2 == 0 else even, None)
            hh = h_ref[...]
            rstd = lax.rsqrt(jnp.mean(hh * hh, axis=-1, keepdims=True) + EPS)
            xh = hh * rstd
            gg = g_ref[...]
            u_ref[...] = _b(xh * gg)
            du = du_acc[...]
            dg_ref[...] += jnp.sum(du * xh, axis=0, keepdims=True)
            dxh = du * gg
            dh_ref[...] = dy_ref[...] + rstd * (dxh - xh * jnp.mean(dxh * xh, axis=-1, keepdims=True))

    cur = lambda k: jnp.minimum(k, nk - 1)
    prv = lambda k: jnp.maximum(k - 1, 0)
    wcur = lambda r, c: pl.BlockSpec((None, None, r, c), lambda i, k: (cur(k), blk, 0, 0))
    wprv = lambda r, c: pl.BlockSpec((None, None, r, c), lambda i, k: (prv(k), blk, 0, 0))
    tok = pl.BlockSpec((tm, d), lambda i, k: (i, 0))
    hid = pl.BlockSpec((None, tm, fs), lambda i, k: (cur(k), i, 0))
    return pl.pallas_call(
        body, name="ffn_bwd",
        grid=(t // tm, nk + 1),
        in_specs=[tok, tok, pl.BlockSpec((1, d), lambda i, k: (0, 0)), hid, hid, wprv(d, fs), wprv(d, fs), wcur(fs, d),
                  DEP_SPEC],
        out_specs=[tok, tok, hid, hid, hid, pl.BlockSpec((1, d), lambda i, k: (0, 0))],
        out_shape=[jax.ShapeDtypeStruct((t, d), F32), jax.ShapeDtypeStruct((t, d), BF16),
                   jax.ShapeDtypeStruct((nk, t, fs), BF16), jax.ShapeDtypeStruct((nk, t, fs), BF16),
                   jax.ShapeDtypeStruct((nk, t, fs), BF16), jax.ShapeDtypeStruct((1, d), F32)],
        scratch_shapes=[pltpu.VMEM((tm, d), BF16), pltpu.VMEM((tm, d), F32)] + [pltpu.VMEM((tm, fs), BF16)] * 4,
        compiler_params=_params("arbitrary", "arbitrary"),
    )(h, dy, g, a_s, b_s, w1, w3, w2, dep)


def wgrad_grouped_b(a, bs, scale=1.0):
    t, m = a.shape
    ng, _, n = bs.shape
    tk = _tile(t, 2048)

    def body(a_ref, b_ref, o_ref):
        j = pl.program_id(1)

        @pl.when(j == 0)
        def _():
            o_ref[...] = jnp.zeros_like(o_ref)

        o_ref[...] += _dot_tn(_b(a_ref[...]), _b(b_ref[...]))

        if scale != 1.0:
            @pl.when(j == pl.num_programs(1) - 1)
            def _():
                o_ref[...] = o_ref[...] * scale

    return pl.pallas_call(
        body, name="wgrad_gb",
        grid=(ng, t // tk),
        in_specs=[pl.BlockSpec((tk, m), lambda k, j: (j, 0)), pl.BlockSpec((None, tk, n), lambda k, j: (k, j, 0))],
        out_specs=pl.BlockSpec((None, m, n), lambda k, j: (k, 0, 0)),
        out_shape=jax.ShapeDtypeStruct((ng, m, n), F32),
        compiler_params=_params("arbitrary", "arbitrary"),
    )(a, bs)


def wgrad_grouped_a(as_, b, scale=1.0):
    ng, t, m = as_.shape
    n = b.shape[1]
    tk = _tile(t, 2048)

    def body(a_ref, b_ref, o_ref):
        j = pl.program_id(1)

        @pl.when(j == 0)
        def _():
            o_ref[...] = jnp.zeros_like(o_ref)

        o_ref[...] += _dot_tn(_b(a_ref[...]), _b(b_ref[...]))

        if scale != 1.0:
            @pl.when(j == pl.num_programs(1) - 1)
            def _():
                o_ref[...] = o_ref[...] * scale

    return pl.pallas_call(
        body, name="wgrad_ga",
        grid=(ng, t // tk),
        in_specs=[pl.BlockSpec((None, tk, m), lambda k, j: (k, j, 0)), pl.BlockSpec((tk, n), lambda k, j: (j, 0))],
        out_specs=pl.BlockSpec((None, m, n), lambda k, j: (k, 0, 0)),
        out_shape=jax.ShapeDtypeStruct((ng, m, n), F32),
        compiler_params=_params("arbitrary", "arbitrary"),
    )(as_, b)


def wgrad(a, b):
    t, m = a.shape
    n = b.shape[1]
    tk = _tile(t, 1024)
    tn = _lane_tile(n, 1536 if m <= 1024 else 512)

    def body(a_ref, b_ref, o_ref):
        @pl.when(pl.program_id(1) == 0)
        def _():
            o_ref[...] = jnp.zeros_like(o_ref)

        o_ref[...] += _dot_tn(_b(a_ref[...]), _b(b_ref[...]))

    return pl.pallas_call(
        body, name="wgrad",
        grid=(n // tn, t // tk),
        in_specs=[pl.BlockSpec((tk, m), lambda c, j: (j, 0)), pl.BlockSpec((tk, tn), lambda c, j: (j, c))],
        out_specs=pl.BlockSpec((m, tn), lambda c, j: (0, c)),
        out_shape=jax.ShapeDtypeStruct((m, n), F32),
        compiler_params=_params("arbitrary", "arbitrary"),
    )(a, b)


def norm_mm(h, g, w):
    t, d = h.shape
    n = w.shape[1]
    tm = _tile(t, 1024)
    tn = _lane_tile(n)

    def body(h_ref, g_ref, w_ref, o_ref, u_scr):
        @pl.when(pl.program_id(1) == 0)
        def _():
            u_scr[...] = _b(_rms(h_ref[...]) * g_ref[...])

        o_ref[...] = _dot(u_scr[...], w_ref[...])

    return pl.pallas_call(
        body, name="norm_mm",
        grid=(t // tm, n // tn),
        in_specs=[pl.BlockSpec((tm, d), lambda i, j: (i, 0)), pl.BlockSpec((1, d), lambda i, j: (0, 0)),
                  pl.BlockSpec((d, tn), lambda i, j: (0, j))],
        out_specs=pl.BlockSpec((tm, tn), lambda i, j: (i, j)),
        out_shape=jax.ShapeDtypeStruct((t, n), F32),
        scratch_shapes=[pltpu.VMEM((tm, d), BF16)],
        compiler_params=_params("arbitrary", "arbitrary"),
    )(h, g, w)


def norm_mm_bwd(h, g, w, dout, dres, dep, scale=1.0):
    t, d = h.shape
    n = w.shape[1]
    tm = _tile(t, 512)
    tn = _lane_tile(n)
    nj = n // tn

    def body(h_ref, g_ref, w_ref, do_ref, dr_ref, dep_ref, dh_ref, u_ref, dg_ref, du_acc):
        i, j = pl.program_id(0), pl.program_id(1)

        @pl.when(j == 0)
        def _():
            du_acc[...] = jnp.zeros_like(du_acc)

        @pl.when((j == 0) & (i == 0))
        def _():
            dg_ref[...] = jnp.zeros_like(dg_ref)

        du_acc[...] += _dot_nt(_b(do_ref[...]), w_ref[...])

        @pl.when(j == nj - 1)
        def _():
            hh = h_ref[...]
            rstd = lax.rsqrt(jnp.mean(hh * hh, axis=-1, keepdims=True) + EPS)
            xh = hh * rstd
            gg = g_ref[...]
            u_ref[...] = _b(xh * gg)
            du = du_acc[...] * scale
            dg_ref[...] += jnp.sum(du * xh, axis=0, keepdims=True)
            dxh = du * gg
            dh_ref[...] = dr_ref[...] + rstd * (dxh - xh * jnp.mean(dxh * xh, axis=-1, keepdims=True))

    tok = pl.BlockSpec((tm, d), lambda i, j: (i, 0))
    return pl.pallas_call(
        body, name="norm_mm_bwd",
        grid=(t // tm, nj),
        in_specs=[tok, pl.BlockSpec((1, d), lambda i, j: (0, 0)), pl.BlockSpec((d, tn), lambda i, j: (0, j)),
                  pl.BlockSpec((tm, tn), lambda i, j: (i, j)), tok, DEP_SPEC],
        out_specs=[tok, tok, pl.BlockSpec((1, d), lambda i, j: (0, 0))],
        out_shape=[jax.ShapeDtypeStruct((t, d), F32), jax.ShapeDtypeStruct((t, d), BF16),
                   jax.ShapeDtypeStruct((1, d), F32)],
        scratch_shapes=[pltpu.VMEM((tm, d), F32)],
        compiler_params=_params("arbitrary", "arbitrary"),
    )(h, g, w, dout, dres, dep)


CONV_COLS = 512


CONV_ROWS = 64


def _conv_pre(ext, w, b, r0, n):
    return (b + w[0:1] * ext[pl.ds(5 + r0, n), :] + w[1:2] * ext[pl.ds(6 + r0, n), :]
            + w[2:3] * ext[pl.ds(7 + r0, n), :] + w[3:4] * ext[pl.ds(8 + r0, n), :])


def conv_fwd(zx, cw, cb, col0):
    t = zx.shape[0]
    c = cw.shape[1]
    tm = _tile(t, 512)
    cb0 = col0 // CONV_COLS

    rc = _tile(tm, CONV_ROWS)

    def body(x_ref, w_ref, b_ref, o_ref, ext):
        @pl.when(pl.program_id(1) == 0)
        def _():
            ext[0:8, :] = jnp.zeros((8, CONV_COLS), F32)

        ext[8:, :] = x_ref[...]
        w, b = w_ref[...], b_ref[...]
        for r0 in range(0, tm, rc):
            o_ref[r0:r0 + rc, :] = _silu(_conv_pre(ext, w, b, r0, rc))
        ext[0:8, :] = ext[tm:tm + 8, :]

    return pl.pallas_call(
        body, name="conv_fwd",
        grid=(c // CONV_COLS, t // tm),
        in_specs=[pl.BlockSpec((tm, CONV_COLS), lambda j, i: (i, cb0 + j)),
                  pl.BlockSpec((8, CONV_COLS), lambda j, i: (0, j)), pl.BlockSpec((1, CONV_COLS), lambda j, i: (0, j))],
        out_specs=pl.BlockSpec((tm, CONV_COLS), lambda j, i: (i, j)),
        out_shape=jax.ShapeDtypeStruct((t, c), F32),
        scratch_shapes=[pltpu.VMEM((tm + 8, CONV_COLS), F32)],
        compiler_params=_params("arbitrary", "arbitrary"),
    )(zx, cw, cb)


def conv_bwd(dzx, zx, dxs, dbm, dcm, cw, cb, col0):
    t = zx.shape[0]
    c = cw.shape[1]
    tm = _tile(t, 512)
    nt = t // tm
    cb0 = col0 // CONV_COLS
    nxs = dxs.shape[1] // CONV_COLS
    hb = tm // 8

    rc = _tile(tm, CONV_ROWS)

    def body(dzx_ref, x_ref, xh_ref, dxs_ref, db_ref, dc_ref, w_ref, b_ref, o_ref, dw_ref, dbias_ref, ext, gy):
        j, i = pl.program_id(0), pl.program_id(1)
        ri = nt - 1 - i

        @pl.when(i == 0)
        def _():
            gy[tm:tm + 8, :] = jnp.zeros((8, CONV_COLS), F32)
            dw_ref[...] = jnp.zeros_like(dw_ref)
            dbias_ref[...] = jnp.zeros_like(dbias_ref)

        ext[0:8, :] = jnp.where(ri > 0, xh_ref[...], 0.0)
        ext[8:, :] = x_ref[...]
        w, b = w_ref[...], b_ref[...]
        dw = [jnp.zeros((1, CONV_COLS), F32) for _ in range(SSM_CONV)]
        dbias = jnp.zeros((1, CONV_COLS), F32)
        for r0 in range(0, tm, rc):
            rows = pl.ds(r0, rc)
            win = [ext[pl.ds(5 + tap + r0, rc), :] for tap in range(SSM_CONV)]
            y = b + w[0:1] * win[0] + w[1:2] * win[1] + w[2:3] * win[2] + w[3:4] * win[3]
            sig = _sigmoid(y)
            dout = jnp.where(j < nxs, dxs_ref[rows, :], jnp.where(j == nxs, db_ref[rows, :], dc_ref[rows, :]))
            g = dout * (sig * (1.0 + y * (1.0 - sig)))
            gy[rows, :] = g
            dbias = dbias + jnp.sum(g, axis=0, keepdims=True)
            for tap in range(SSM_CONV):
                dw[tap] = dw[tap] + jnp.sum(g * win[tap], axis=0, keepdims=True)
        for r0 in range(0, tm, rc):
            o_ref[r0:r0 + rc, :] = (w[0:1] * gy[pl.ds(r0 + 3, rc), :] + w[1:2] * gy[pl.ds(r0 + 2, rc), :]
                                    + w[2:3] * gy[pl.ds(r0 + 1, rc), :] + w[3:4] * gy[pl.ds(r0, rc), :])
        gy[tm:tm + 8, :] = gy[0:8, :]
        for tap in range(SSM_CONV):
            dw_ref[tap:tap + 1, :] += dw[tap]
        dbias_ref[...] += dbias

    return pl.pallas_call(
        body, name="conv_bwd",
        grid=(c // CONV_COLS, nt),
        in_specs=[pl.BlockSpec(memory_space=pl.ANY),
                  pl.BlockSpec((tm, CONV_COLS), lambda j, i: (nt - 1 - i, cb0 + j)),
                  pl.BlockSpec((8, CONV_COLS), lambda j, i: (jnp.maximum((nt - 1 - i) * hb - 1, 0), cb0 + j)),
                  pl.BlockSpec((tm, CONV_COLS), lambda j, i: (nt - 1 - i, jnp.minimum(j, nxs - 1))),
                  pl.BlockSpec((tm, CONV_COLS), lambda j, i: (nt - 1 - i, 0)),
                  pl.BlockSpec((tm, CONV_COLS), lambda j, i: (nt - 1 - i, 0)),
                  pl.BlockSpec((8, CONV_COLS), lambda j, i: (0, j)), pl.BlockSpec((1, CONV_COLS), lambda j, i: (0, j))],
        out_specs=[pl.BlockSpec((tm, CONV_COLS), lambda j, i: (nt - 1 - i, cb0 + j)),
                   pl.BlockSpec((8, CONV_COLS), lambda j, i: (0, j)), pl.BlockSpec((1, CONV_COLS), lambda j, i: (0, j))],
        out_shape=[jax.ShapeDtypeStruct(dzx.shape, F32), jax.ShapeDtypeStruct((8, c), F32),
                   jax.ShapeDtypeStruct((1, c), F32)],
        scratch_shapes=[pltpu.VMEM((tm + 8, CONV_COLS), F32), pltpu.VMEM((tm + 8, CONV_COLS), F32)],
        input_output_aliases={0: 0},
        compiler_params=_params("arbitrary", "arbitrary"),
    )(dzx, zx, zx, dxs, dbm, dcm, cw, cb)


def _ssd_group(xs, bg, cg, dtraw, s0, bias, alog, dsk):
    L = xs.shape[0]
    causal = _iota((L, L), 0) >= _iota((L, L), 1)
    tril = jnp.where(causal, 1.0, 0.0).astype(BF16)
    dt = _softplus(dtraw + bias)
    a = -jnp.exp(alog)
    acum = _xleft(tril, dt * a)
    acum_t = acum.T
    dt_t = dt.T
    cb = _bmm_nt(cg, bg)
    lo = _iota((L, 128), 1) < 64
    lo_row = _iota((1, 128), 1) < 64
    lo_col = _iota((128, 1), 0) < 64
    alast = acum[L - 1:L, :]
    ys, s1s = [], []
    for q in range(4):
        xp = xs[:, q * 128:(q + 1) * 128]
        sp = s0[q * 128:(q + 1) * 128, :]
        yd, ec, wc, el = [], [], [], []
        for j in range(2):
            r = 2 * q + j
            ac = acum[:, r:r + 1]
            decay = jnp.exp(jnp.where(causal, ac - acum_t[r:r + 1, :], NEG))
            yd.append(_bmm(cb * decay * dt_t[r:r + 1, :], xp))
            ec.append(jnp.exp(ac))
            al = alast[:, r:r + 1]
            wc.append(jnp.exp(al - ac) * dt[:, r:r + 1])
            el.append(jnp.exp(al))
        y_off = _bmm_nt(cg, sp) * jnp.where(lo, ec[0], ec[1])
        dsel = jnp.where(lo_row, dsk[:, 2 * q:2 * q + 1], dsk[:, 2 * q + 1:2 * q + 2])
        ys.append(jnp.where(lo, yd[0], yd[1]) + y_off + dsel * xp)
        xw = xp * jnp.where(lo, wc[0], wc[1])
        s1s.append(sp * jnp.where(lo_col, el[0], el[1]) + _bmm_tn(xw, bg))
    return jnp.concatenate(ys, axis=1), jnp.concatenate(s1s, axis=0)


def ssd_fwd(xc, zx, bias, alog, dsk, dt_col0):
    t = xc.shape[0]
    L = _tile(t, SSM_CHUNK)
    nc = t // L
    g = SSM_GROUPS
    dtb = dt_col0 // 512

    def body(xs_ref, b_ref, c_ref, dt_ref, bias_ref, alog_ref, dsk_ref, y_ref, st_ref, state):
        @pl.when(pl.program_id(0) == 0)
        def _():
            state[...] = jnp.zeros_like(state)

        for gi in range(g):
            lane = slice(gi * 128, (gi + 1) * 128)
            wide = slice(gi * 512, (gi + 1) * 512)
            s0 = state[gi]
            st_ref[gi] = s0
            y, s1 = _ssd_group(xs_ref[:, wide], b_ref[:, lane], c_ref[:, lane], dt_ref[:, lane], s0,
                               bias_ref[:, lane], alog_ref[:, lane], dsk_ref[:, lane])
            y_ref[:, wide] = y
            state[gi] = s1

    vec = pl.BlockSpec((1, 512), lambda c: (0, 0))
    return pl.pallas_call(
        body, name="ssd_fwd",
        grid=(nc,),
        in_specs=[pl.BlockSpec((L, 2048), lambda c: (c, 0)), pl.BlockSpec((L, 512), lambda c: (c, 4)),
                  pl.BlockSpec((L, 512), lambda c: (c, 5)), pl.BlockSpec((L, 512), lambda c: (c, dtb)), vec, vec, vec],
        out_specs=[pl.BlockSpec((L, 2048), lambda c: (c, 0)),
                   pl.BlockSpec((None, g, 512, 128), lambda c: (c, 0, 0, 0))],
        out_shape=[jax.ShapeDtypeStruct((t, 2048), F32), jax.ShapeDtypeStruct((nc, g, 512, 128), F32)],
        scratch_shapes=[pltpu.VMEM((g, 512, 128), F32)],
        compiler_params=_params("arbitrary"),
    )(xc, xc, xc, zx, bias, alog, dsk)


def ssd_bwd(dzx, dy, xc, zx, states, bias, alog, dsk, dt_col0):
    t = xc.shape[0]
    L = _tile(t, SSM_CHUNK)
    nc = t // L
    g = SSM_GROUPS
    dtb = dt_col0 // 512

    def body(dzx_ref, dy_ref, xs_ref, b_ref, c_ref, dt_ref, st_ref, bias_ref, alog_ref, dsk_ref,
             ddt_ref, dxs_ref, db_ref, dc_ref, dbias_ref, dalog_ref, ddsk_ref, dstate):
        @pl.when(pl.program_id(0) == 0)
        def _():
            dstate[...] = jnp.zeros_like(dstate)
            dbias_ref[...] = jnp.zeros_like(dbias_ref)
            dalog_ref[...] = jnp.zeros_like(dalog_ref)
            ddsk_ref[...] = jnp.zeros_like(ddsk_ref)

        for gi in range(g):
            lane = slice(gi * 128, (gi + 1) * 128)
            wide = slice(gi * 512, (gi + 1) * 512)
            _, vjp = jax.vjp(_ssd_group, xs_ref[:, wide], b_ref[:, lane], c_ref[:, lane], dt_ref[:, lane], st_ref[gi],
                             bias_ref[:, lane], alog_ref[:, lane], dsk_ref[:, lane])
            dxs, db, dc, ddt, ds0, dbias, dalog, ddsk = vjp((dy_ref[:, wide], dstate[gi]))
            dxs_ref[:, wide] = dxs
            db_ref[:, lane] = db
            dc_ref[:, lane] = dc
            ddt_ref[:, lane] = ddt
            dstate[gi] = ds0
            dbias_ref[:, lane] += dbias
            dalog_ref[:, lane] += dalog
            ddsk_ref[:, lane] += ddsk

    rc = lambda c: nc - 1 - c
    vec = pl.BlockSpec((1, 512), lambda c: (0, 0))
    return pl.pallas_call(
        body, name="ssd_bwd",
        grid=(nc,),
        in_specs=[pl.BlockSpec(memory_space=pl.ANY),
                  pl.BlockSpec((L, 2048), lambda c: (rc(c), 0)), pl.BlockSpec((L, 2048), lambda c: (rc(c), 0)),
                  pl.BlockSpec((L, 512), lambda c: (rc(c), 4)), pl.BlockSpec((L, 512), lambda c: (rc(c), 5)),
                  pl.BlockSpec((L, 512), lambda c: (rc(c), dtb)),
                  pl.BlockSpec((None, g, 512, 128), lambda c: (rc(c), 0, 0, 0)), vec, vec, vec],
        out_specs=[pl.BlockSpec((L, 512), lambda c: (rc(c), dtb)), pl.BlockSpec((L, 2048), lambda c: (rc(c), 0)),
                   pl.BlockSpec((L, 512), lambda c: (rc(c), 0)), pl.BlockSpec((L, 512), lambda c: (rc(c), 0)),
                   vec, vec, vec],
        out_shape=[jax.ShapeDtypeStruct(dzx.shape, F32), jax.ShapeDtypeStruct((t, 2048), F32),
                   jax.ShapeDtypeStruct((t, 512), F32), jax.ShapeDtypeStruct((t, 512), F32),
                   jax.ShapeDtypeStruct((1, 512), F32), jax.ShapeDtypeStruct((1, 512), F32),
                   jax.ShapeDtypeStruct((1, 512), F32)],
        scratch_shapes=[pltpu.VMEM((g, 512, 128), F32)],
        input_output_aliases={0: 0},
        compiler_params=_params("arbitrary"),
    )(dzx, dy, xc, xc, xc, zx, states, bias, alog, dsk)


def _gate_tile(y, z, gn):
    gated = y * _silu(z)
    parts = [_rms(gated[:, k * 512:(k + 1) * 512]) for k in range(SSM_GROUPS)]
    return jnp.concatenate(parts, axis=1) * gn


def gate_out_fwd(h, y, zx, gn, w_out):
    t, d = h.shape
    di = y.shape[1]
    tm = _tile(t, 256)

    def body(h_ref, y_ref, z_ref, gn_ref, w_ref, o_ref):
        yn = _gate_tile(y_ref[...], z_ref[...], gn_ref[...])
        o_ref[...] = h_ref[...] + _dot(_b(yn), w_ref[...])

    return pl.pallas_call(
        body, name="gate_out_fwd",
        grid=(t // tm,),
        in_specs=[pl.BlockSpec((tm, d), lambda i: (i, 0)), pl.BlockSpec((tm, di), lambda i: (i, 0)),
                  pl.BlockSpec((tm, di), lambda i: (i, 0)), pl.BlockSpec((1, di), lambda i: (0, 0)),
                  pl.BlockSpec((di, d), lambda i: (0, 0))],
        out_specs=pl.BlockSpec((tm, d), lambda i: (i, 0)),
        out_shape=jax.ShapeDtypeStruct((t, d), F32),
        compiler_params=_params("arbitrary"),
    )(h, y, zx, gn, w_out)


def gate_out_bwd(dy, y, zx, gn, w_out, n_zx, dep):
    t, d = dy.shape
    di = y.shape[1]
    tm = _tile(t, 256)

    def body(dy_ref, y_ref, z_ref, gn_ref, w_ref, dep_ref, dz_ref, dys_ref, yn_ref, dgn_ref):
        @pl.when(pl.program_id(0) == 0)
        def _():
            dgn_ref[...] = jnp.zeros_like(dgn_ref)

        yn, vjp = jax.vjp(_gate_tile, y_ref[...], z_ref[...], gn_ref[...])
        dyn = _dot_nt(_b(dy_ref[...]), w_ref[...])
        dys, dz, dgn = vjp(dyn)
        yn_ref[...] = _b(yn)
        dys_ref[...] = dys
        dz_ref[...] = dz
        dgn_ref[...] += dgn

    return pl.pallas_call(
        body, name="gate_out_bwd",
        grid=(t // tm,),
        in_specs=[pl.BlockSpec((tm, d), lambda i: (i, 0)), pl.BlockSpec((tm, di), lambda i: (i, 0)),
                  pl.BlockSpec((tm, di), lambda i: (i, 0)), pl.BlockSpec((1, di), lambda i: (0, 0)),
                  pl.BlockSpec((di, d), lambda i: (0, 0)), DEP_SPEC],
        out_specs=[pl.BlockSpec((tm, di), lambda i: (i, 0)), pl.BlockSpec((tm, di), lambda i: (i, 0)),
                   pl.BlockSpec((tm, di), lambda i: (i, 0)), pl.BlockSpec((1, di), lambda i: (0, 0))],
        out_shape=[jax.ShapeDtypeStruct((t, n_zx), F32), jax.ShapeDtypeStruct((t, di), F32),
                   jax.ShapeDtypeStruct((t, di), BF16), jax.ShapeDtypeStruct((1, di), F32)],
        compiler_params=_params("arbitrary"),
    )(dy, y, zx, gn, w_out, dep)


def _attn_block(qp, kvp, kvc, biasm, sinks, qg, kg, w_o, first):
    nq = qp.shape[0]
    n_pairs = qp.shape[1] // 128
    hk = n_pairs
    rows = hk * nq
    seg = functools.partial(_xright, m=_blockdiag64(128))
    scale = ATT_HEAD_DIM ** -0.5
    qi = (_iota((rows, 2 * nq), 0) % nq) + nq
    kj = _iota((rows, 2 * nq), 1)
    dist = qi - kj
    valid = (dist >= 0) & (dist < ATT_WINDOW) & (jnp.logical_not(first) | (kj >= nq))
    lo = _iota((nq, 128), 1) < 64
    kv = jnp.concatenate([kvp, kvc], axis=0)
    outs = [None] * n_pairs
    for kvh in range(2):
        kn = _group64_rms(kv[:, kvh * 128:(kvh + 1) * 128], seg) * kg
        vv = kv[:, 256 + kvh * 128:256 + (kvh + 1) * 128]
        pairs = range(kvh * hk // 2, (kvh + 1) * hk // 2)
        qs, sk = [], []
        for p in pairs:
            qn = _group64_rms(qp[:, p * 128:(p + 1) * 128], seg) * qg
            qs += [jnp.where(lo, qn, 0.0), jnp.where(lo, 0.0, qn)]
            sk += [jnp.broadcast_to(sinks[:, h:h + 1], (nq, 1)) for h in (2 * p, 2 * p + 1)]
        sink = jnp.concatenate(sk, axis=0)
        s = _bmm_nt(jnp.concatenate(qs, axis=0), kn) * scale + biasm[kvh * rows:(kvh + 1) * rows]
        s = jnp.where(valid, s, NEG)
        m = lax.stop_gradient(jnp.maximum(jnp.max(s, axis=-1, keepdims=True), sink))
        pexp = jnp.exp(s - m)
        den = jnp.sum(pexp, axis=-1, keepdims=True) + jnp.exp(sink - m)
        o = _bmm(pexp * (1.0 / den), vv)
        for n, p in enumerate(pairs):
            outs[p] = jnp.where(lo, o[2 * n * nq:(2 * n + 1) * nq], o[(2 * n + 1) * nq:(2 * n + 2) * nq])
    o = jnp.concatenate(outs, axis=1)
    return _bmm(o, w_o), o


def attn_fwd(h, qp, kvd, biasm, sinks, qg, kg, w_o):
    t, d = h.shape
    nq = ATT_WINDOW
    nb = t // nq
    nh = qp.shape[1] // ATT_HEAD_DIM

    def body(h_ref, q_ref, kp_ref, kc_ref, bias_ref, s_ref, qg_ref, kg_ref, w_ref, o_ref):
        out, _ = _attn_block(q_ref[...], kp_ref[...], kc_ref[...], bias_ref[...], s_ref[...], qg_ref[...],
                             kg_ref[...], w_ref[...], pl.program_id(0) == 0)
        o_ref[...] = h_ref[...] + out

    vec = pl.BlockSpec((1, 128), lambda i: (0, 0))
    return pl.pallas_call(
        body, name="attn_fwd",
        grid=(nb,),
        in_specs=[pl.BlockSpec((nq, d), lambda i: (i, 0)), pl.BlockSpec((nq, nh * 64), lambda i: (i, 0)),
                  pl.BlockSpec((nq, 512), lambda i: (jnp.maximum(i - 1, 0), 0)),
                  pl.BlockSpec((nq, 512), lambda i: (i, 0)),
                  pl.BlockSpec((nh * nq, 2 * nq), lambda i: (0, 0)), vec, vec, vec,
                  pl.BlockSpec((nh * 64, d), lambda i: (0, 0))],
        out_specs=pl.BlockSpec((nq, d), lambda i: (i, 0)),
        out_shape=jax.ShapeDtypeStruct((t, d), F32),
        compiler_params=_params("arbitrary"),
    )(h, qp, kvd, kvd, biasm, sinks, qg, kg, w_o)


def attn_bwd(dy, qp, kvd, biasm, sinks, qg, kg, w_o, dep):
    t, d = dy.shape
    nq = ATT_WINDOW
    nb = t // nq
    nh = qp.shape[1] // ATT_HEAD_DIM

    def body(dy_ref, q_ref, kp_ref, kc_ref, bias_ref, s_ref, qg_ref, kg_ref, w_ref, dep_ref,
             dq_ref, dkv_ref, o_ref, dbias_ref, ds_ref, dqg_ref, dkg_ref, carry):
        i = pl.program_id(0)

        @pl.when(i == 0)
        def _():
            carry[...] = jnp.zeros_like(carry)
            dbias_ref[...] = jnp.zeros_like(dbias_ref)
            ds_ref[...] = jnp.zeros_like(ds_ref)
            dqg_ref[...] = jnp.zeros_like(dqg_ref)
            dkg_ref[...] = jnp.zeros_like(dkg_ref)

        @pl.when(i < nb)
        def _():
            fn = functools.partial(_attn_block, w_o=w_ref[...], first=(i == 0))
            (_, o), vjp = jax.vjp(fn, q_ref[...], kp_ref[...], kc_ref[...], bias_ref[...], s_ref[...],
                                  qg_ref[...], kg_ref[...])
            dq, dkp, dkc, dbias, dsk, dqg, dkg = vjp((dy_ref[...], jnp.zeros((nq, nh * 64), F32)))
            dq_ref[...] = dq
            o_ref[...] = _b(o)
            dkv_ref[...] = _fold64(carry[...] + dkp)
            carry[...] = dkc
            dbias_ref[...] += dbias
            ds_ref[...] += dsk
            dqg_ref[...] += _fold64(dqg)
            dkg_ref[...] += _fold64(dkg)

        @pl.when(i == nb)
        def _():
            dkv_ref[...] = _fold64(carry[...])

    cl = lambda i: jnp.minimum(i, nb - 1)
    vec = pl.BlockSpec((1, 128), lambda i: (0, 0))
    return pl.pallas_call(
        body, name="attn_bwd",
        grid=(nb + 1,),
        in_specs=[pl.BlockSpec((nq, d), lambda i: (cl(i), 0)), pl.BlockSpec((nq, nh * 64), lambda i: (cl(i), 0)),
                  pl.BlockSpec((nq, 512), lambda i: (jnp.maximum(cl(i) - 1, 0), 0)),
                  pl.BlockSpec((nq, 512), lambda i: (cl(i), 0)),
                  pl.BlockSpec((nh * nq, 2 * nq), lambda i: (0, 0)), vec, vec, vec,
                  pl.BlockSpec((nh * 64, d), lambda i: (0, 0)), DEP_SPEC],
        out_specs=[pl.BlockSpec((nq, nh * 64), lambda i: (cl(i), 0)),
                   pl.BlockSpec((nq, 512), lambda i: (jnp.maximum(i - 1, 0), 0)),
                   pl.BlockSpec((nq, nh * 64), lambda i: (cl(i), 0)),
                   pl.BlockSpec((nh * nq, 2 * nq), lambda i: (0, 0)), vec, vec, vec],
        out_shape=[jax.ShapeDtypeStruct((t, nh * 64), F32), jax.ShapeDtypeStruct((t, 512), F32),
                   jax.ShapeDtypeStruct((t, nh * 64), BF16), jax.ShapeDtypeStruct((nh * nq, 2 * nq), F32),
                   jax.ShapeDtypeStruct((1, 128), F32), jax.ShapeDtypeStruct((1, 128), F32),
                   jax.ShapeDtypeStruct((1, 128), F32)],
        scratch_shapes=[pltpu.VMEM((nq, 512), F32)],
        compiler_params=_params("arbitrary"),
    )(dy, qp, kvd, kvd, biasm, sinks, qg, kg, w_o, dep)


def _t5_buckets():
    nq = ATT_WINDOW
    dist = (np.arange(nq)[:, None] + nq) - np.arange(2 * nq)[None, :]
    n = np.maximum(dist, 0)
    max_exact = REL_BUCKETS // 2
    nf = np.maximum(n, 1).astype(np.float32)
    large = max_exact + (np.log(nf / max_exact) / math.log(ATT_WINDOW / max_exact)
                         * (REL_BUCKETS - max_exact)).astype(np.int32)
    large = np.minimum(large, REL_BUCKETS - 1)
    return np.where(n < max_exact, n, large).astype(np.int32)


def rel_bias_bwd(dbias, buckets):
    nh = dbias.shape[0]

    def body(db_ref, bk_ref, o_ref):
        bk = bk_ref[...]
        lane = _iota((1, 128), 1)
        row = _iota((REL_BUCKETS, 128), 0)
        acc = jnp.zeros((REL_BUCKETS, 128), F32)
        for h in range(nh):
            dbh = db_ref[h]
            for b in range(REL_BUCKETS):
                v = jnp.sum(jnp.where(bk == b, dbh, 0.0))
                acc = acc + jnp.where((row == b) & (lane == h), v, 0.0)
        o_ref[...] = acc

    return pl.pallas_call(
        body, name="rel_bias_bwd",
        out_shape=jax.ShapeDtypeStruct((REL_BUCKETS, 128), F32),
        compiler_params=_params(),
    )(dbias, buckets)


def loss_head(y, target):
    t, d = y.shape
    tm = _tile(t, 512)

    def body(y_ref, t_ref, l_ref, dy_ref):
        @pl.when(pl.program_id(0) == 0)
        def _():
            l_ref[...] = jnp.zeros_like(l_ref)

        e = y_ref[...] - t_ref[...]
        l_ref[...] += 0.5 * jnp.sum(jnp.mean(e * e, axis=-1, keepdims=True), axis=0, keepdims=True)
        dy_ref[...] = e * (1.0 / d)

    return pl.pallas_call(
        body, name="loss_head",
        grid=(t // tm,),
        in_specs=[pl.BlockSpec((tm, d), lambda i: (i, 0)), pl.BlockSpec((tm, d), lambda i: (i, 0))],
        out_specs=[pl.BlockSpec((1, 1), lambda i: (0, 0)), pl.BlockSpec((tm, d), lambda i: (i, 0))],
        out_shape=[jax.ShapeDtypeStruct((1, 1), F32), jax.ShapeDtypeStruct((t, d), F32)],
        compiler_params=_params("arbitrary"),
    )(y, target)


def adamw(w, g, m, v):
    r, c = w.shape
    tr = r if r <= 512 else _tile(r, 256)

    def body(w_ref, g_ref, m_ref, v_ref, d_ref, nm_ref, nv_ref):
        gg = g_ref[...]
        nm = ADAM_B1 * m_ref[...] + (1.0 - ADAM_B1) * gg
        nv = ADAM_B2 * v_ref[...] + (1.0 - ADAM_B2) * (gg * gg)
        m_hat = nm / (1.0 - ADAM_B1 ** ADAM_STEP)
        v_hat = nv / (1.0 - ADAM_B2 ** ADAM_STEP)
        d_ref[...] = -ADAM_LR * (m_hat / (jnp.sqrt(v_hat) + ADAM_EPS) + ADAM_WD * w_ref[...])
        nm_ref[...] = nm
        nv_ref[...] = nv

    spec = pl.BlockSpec((tr, c), lambda i: (i, 0))
    shp = jax.ShapeDtypeStruct((r, c), F32)
    return pl.pallas_call(
        body, name="adamw",
        grid=(r // tr,),
        in_specs=[spec] * 4, out_specs=[spec] * 3, out_shape=[shp] * 3,
        compiler_params=_params("arbitrary"),
    )(w, g, m, v)


def _my_pos():
    return lax.axis_index("x"), lax.axis_index("y"), lax.axis_index("c")


def _other_chips(x, y):
    return [(1 - x, y), (x, 1 - y), (1 - x, 1 - y)]


def _chip_id(x, y):
    return 2 * x + y


HBM_SPEC = pl.BlockSpec(memory_space=pltpu.HBM)
SEM_SPEC = pl.BlockSpec(memory_space=pltpu.SEMAPHORE)
DATAFLOW = pltpu.SideEffectType.DATAFLOW_SIDE_EFFECTING


def _in_hbm(a):
    return pltpu.with_memory_space_constraint(a, pltpu.HBM)


def _ici_gather_copy(buf, p, j, chip, c, to, send_sems, recv_sems):
    blk = buf.at[_chip_id(*chip), c]
    return pltpu.make_async_remote_copy(
        src_ref=blk, dst_ref=blk, send_sem=send_sems.at[3 * p + j], recv_sem=recv_sems.at[3 * p + j],
        device_id=to, device_id_type=MESH)


def gather_start(bufs, after, tag):
    n = len(bufs)

    def body(*refs):
        ins = refs[:n]
        send_sems, recv_sems = refs[n + 1], refs[n + 2]
        token = refs[2 * n + 3]
        x, y, c = _my_pos()
        for p in range(n):
            for j, chip in enumerate(_other_chips(x, y)):
                _ici_gather_copy(ins[p], p, j, (x, y), c, (*chip, c), send_sems, recv_sems).start()
        token[...] = jnp.zeros_like(token)

    out = pl.pallas_call(
        body, name="gather_start_" + tag,
        in_specs=[HBM_SPEC] * n + [DEP_SPEC],
        out_specs=(SEM_SPEC, SEM_SPEC, *([HBM_SPEC] * n), pl.BlockSpec(memory_space=pltpu.VMEM)),
        out_shape=(pltpu.SemaphoreType.DMA((3 * n,)), pltpu.SemaphoreType.DMA((3 * n,)),
                   *[pltpu.HBM(b.shape, b.dtype) for b in bufs], jax.ShapeDtypeStruct((8, 128), F32)),
        input_output_aliases={p: 2 + p for p in range(n)},
        compiler_params=pltpu.CompilerParams(has_side_effects=DATAFLOW),
    )(*[_in_hbm(b) for b in bufs], after)
    return out[0], out[1], list(out[2:2 + n]), out[2 + n]


def gather_wait(send_sems, recv_sems, bufs, after, tag):
    n = len(bufs)

    def body(*refs):
        ins = refs[:n]
        send_sems, recv_sems = refs[n], refs[n + 1]
        x, y, c = _my_pos()
        for p in range(n):
            for j, chip in enumerate(_other_chips(x, y)):
                _ici_gather_copy(ins[p], p, j, (x, y), c, (*chip, c), send_sems, recv_sems).wait_send()
                _ici_gather_copy(ins[p], p, j, chip, c, (x, y, c), send_sems, recv_sems).wait_recv()

    out = pl.pallas_call(
        body, name="gather_wait_" + tag,
        in_specs=[HBM_SPEC] * n + [SEM_SPEC, SEM_SPEC, pl.BlockSpec(memory_space=pl.ANY)],
        out_specs=[HBM_SPEC] * n,
        out_shape=[pltpu.HBM(b.shape, b.dtype) for b in bufs],
        input_output_aliases={p: p for p in range(n)},
        compiler_params=pltpu.CompilerParams(has_side_effects=DATAFLOW),
    )(*bufs, send_sems, recv_sems, after)
    return list(out)


def forward_to_sibling(bufs):
    n = len(bufs)

    def body(*refs):
        outs = refs[n:2 * n]
        send_sems, recv_sems = refs[2 * n:]
        x, y, c = _my_pos()
        chips = _other_chips(x, y)
        sent = []
        for p in range(n):
            for j, chip in enumerate(chips):
                cp = _ici_gather_copy(outs[p], p, j, chip, c, (x, y, 1 - c), send_sems, recv_sems)
                cp.start()
                sent.append(cp)
        for p in range(n):
            for j, chip in enumerate(chips):
                _ici_gather_copy(outs[p], p, j, chip, 1 - c, (x, y, c), send_sems, recv_sems).wait_recv()
        for cp in sent:
            cp.wait_send()

    any_spec = pl.BlockSpec(memory_space=pl.ANY)
    return pl.pallas_call(
        body, name="forward_to_sibling",
        in_specs=[any_spec] * n, out_specs=[any_spec] * n,
        out_shape=[jax.ShapeDtypeStruct(b.shape, b.dtype) for b in bufs],
        scratch_shapes=[pltpu.SemaphoreType.DMA((3 * n,)), pltpu.SemaphoreType.DMA((3 * n,))],
        input_output_aliases={p: p for p in range(n)},
    )(*bufs)


def allreduce_small(v):
    r, c = v.shape

    def body(v_ref, o_ref, buf, send_sems, recv_sems):
        x, y, cc = _my_pos()
        me = 4 * x + 2 * y + cc
        buf[me] = v_ref[...]
        copies = []
        for k in range(1, 8):
            dx, dy, dc = (k >> 2) & 1, (k >> 1) & 1, k & 1
            peer = (x ^ dx, y ^ dy, cc ^ dc)
            cp = pltpu.make_async_remote_copy(
                src_ref=v_ref, dst_ref=buf.at[me], send_sem=send_sems.at[k - 1], recv_sem=recv_sems.at[k - 1],
                device_id=peer, device_id_type=MESH)
            cp.start()
            copies.append(cp)
        for cp in copies:
            cp.wait_recv()
        for cp in copies:
            cp.wait_send()
        acc = buf[0]
        for k in range(1, 8):
            acc = acc + buf[k]
        o_ref[...] = acc

    vm = pl.BlockSpec(memory_space=pltpu.VMEM)
    return pl.pallas_call(
        body, name="allreduce_small",
        in_specs=[vm], out_specs=vm,
        out_shape=jax.ShapeDtypeStruct((r, c), F32),
        scratch_shapes=[pltpu.VMEM((8, r, c), F32), pltpu.SemaphoreType.DMA((7,)), pltpu.SemaphoreType.DMA((7,))],
    )(v)


def exchange_sibling_halves(grads):
    n = len(grads)

    def body(*refs):
        ins, outs = refs[:n], refs[n:2 * n]
        send_sems, recv_sems = refs[2 * n:]
        x, y, c = _my_pos()
        copies = []
        for p in range(n):
            cp = pltpu.make_async_remote_copy(
                src_ref=ins[p].at[:, 1 - c], dst_ref=outs[p], send_sem=send_sems.at[p], recv_sem=recv_sems.at[p],
                device_id=(x, y, 1 - c), device_id_type=MESH)
            cp.start()
            copies.append(cp)
        for cp in copies:
            cp.wait_recv()
        for cp in copies:
            cp.wait_send()

    any_spec = pl.BlockSpec(memory_space=pl.ANY)
    return pl.pallas_call(
        body, name="exchange_sibling_halves",
        in_specs=[any_spec] * n, out_specs=[any_spec] * n,
        out_shape=[jax.ShapeDtypeStruct((g.shape[0],) + g.shape[2:], g.dtype) for g in grads],
        scratch_shapes=[pltpu.SemaphoreType.DMA((n,)), pltpu.SemaphoreType.DMA((n,))],
    )(*grads)


def _chip_partial_copy(part, land, p, j, chip, c, send_sems, recv_sems):
    return pltpu.make_async_remote_copy(
        src_ref=part.at[_chip_id(*chip)], dst_ref=land.at[j], send_sem=send_sems.at[3 * p + j],
        recv_sem=recv_sems.at[3 * p + j], device_id=(*chip, c), device_id_type=MESH)


def chip_partials_start(parts, tag):
    n = len(parts)
    lands = [lax.empty((N_CHIPS - 1,) + s.shape[1:], s.dtype) for s in parts]

    def body(*refs):
        ins, land = refs[:n], refs[n:2 * n]
        send_sems, recv_sems = refs[2 * n], refs[2 * n + 1]
        token = refs[4 * n + 2]
        x, y, c = _my_pos()
        for p in range(n):
            for j, chip in enumerate(_other_chips(x, y)):
                _chip_partial_copy(ins[p], land[p], p, j, chip, c, send_sems, recv_sems).start()
        token[...] = jnp.zeros_like(token)

    out = pl.pallas_call(
        body, name="chip_partials_start_" + tag,
        in_specs=[HBM_SPEC] * (2 * n),
        out_specs=(SEM_SPEC, SEM_SPEC, *([HBM_SPEC] * (2 * n)), pl.BlockSpec(memory_space=pltpu.VMEM)),
        out_shape=(pltpu.SemaphoreType.DMA((3 * n,)), pltpu.SemaphoreType.DMA((3 * n,)),
                   *[pltpu.HBM(a.shape, a.dtype) for a in parts + lands], jax.ShapeDtypeStruct((8, 128), F32)),
        input_output_aliases={i: 2 + i for i in range(2 * n)},
        compiler_params=pltpu.CompilerParams(has_side_effects=DATAFLOW),
    )(*[_in_hbm(a) for a in parts + lands])
    return out[0], out[1], list(out[2:2 + n]), list(out[2 + n:2 + 2 * n]), out[2 + 2 * n]


def chip_partials_wait(send_sems, recv_sems, parts, lands, after, tag):
    n = len(parts)

    def body(*refs):
        ins, land = refs[:n], refs[n:2 * n]
        send_sems, recv_sems = refs[2 * n], refs[2 * n + 1]
        x, y, c = _my_pos()
        for p in range(n):
            for j, chip in enumerate(_other_chips(x, y)):
                cp = _chip_partial_copy(ins[p], land[p], p, j, chip, c, send_sems, recv_sems)
                cp.wait_send()
                cp.wait_recv()

    out = pl.pallas_call(
        body, name="chip_partials_wait_" + tag,
        in_specs=[HBM_SPEC] * (2 * n) + [SEM_SPEC, SEM_SPEC, pl.BlockSpec(memory_space=pl.ANY)],
        out_specs=[HBM_SPEC] * (2 * n),
        out_shape=[pltpu.HBM(a.shape, a.dtype) for a in parts + lands],
        input_output_aliases={i: i for i in range(2 * n)},
        compiler_params=pltpu.CompilerParams(has_side_effects=DATAFLOW),
    )(*parts, *lands, send_sems, recv_sems, after)
    return list(out[n:])


def share_with_sibling(bufs):
    n = len(bufs)

    def body(*refs):
        outs = refs[n:2 * n]
        send_sems, recv_sems = refs[2 * n:]
        x, y, c = _my_pos()
        copies = []
        for p in range(n):
            cp = pltpu.make_async_remote_copy(
                src_ref=outs[p].at[c], dst_ref=outs[p].at[c], send_sem=send_sems.at[p], recv_sem=recv_sems.at[p],
                device_id=(x, y, 1 - c), device_id_type=MESH)
            cp.start()
            copies.append(cp)
        for p in range(n):
            pltpu.make_async_remote_copy(
                src_ref=outs[p].at[1 - c], dst_ref=outs[p].at[1 - c], send_sem=send_sems.at[p],
                recv_sem=recv_sems.at[p], device_id=(x, y, 1 - c), device_id_type=MESH).wait_recv()
        for cp in copies:
            cp.wait_send()

    any_spec = pl.BlockSpec(memory_space=pl.ANY)
    return pl.pallas_call(
        body, name="share_with_sibling",
        in_specs=[any_spec] * n, out_specs=[any_spec] * n,
        out_shape=[jax.ShapeDtypeStruct(b.shape, b.dtype) for b in bufs],
        scratch_shapes=[pltpu.SemaphoreType.DMA((n,)), pltpu.SemaphoreType.DMA((n,))],
        input_output_aliases={p: p for p in range(n)},
    )(*bufs)


def add_sibling(g, recv, half):
    _, _, r, c = g.shape
    tr = _tile(r, 256) if r % 256 == 0 else r

    def body(half_ref, g_ref, r_ref, o32_ref, o16_ref):
        s = g_ref[...] + r_ref[...]
        o32_ref[...] = s
        o16_ref[...] = _b(s)

    return pl.pallas_call(
        body, name="add_sibling",
        grid_spec=pltpu.PrefetchScalarGridSpec(
            num_scalar_prefetch=1, grid=(N_CHIPS, r // tr),
            in_specs=[pl.BlockSpec((None, None, tr, c), lambda k, i, hf: (k, hf[0], i, 0)),
                      pl.BlockSpec((None, tr, c), lambda k, i, hf: (k, i, 0))],
            out_specs=[pl.BlockSpec((None, tr, c), lambda k, i, hf: (k, i, 0)),
                       pl.BlockSpec((None, tr, c), lambda k, i, hf: (k, i, 0))]),
        out_shape=[jax.ShapeDtypeStruct((N_CHIPS, r, c), F32), jax.ShapeDtypeStruct((N_CHIPS, r, c), BF16)],
        compiler_params=_params("arbitrary", "arbitrary"),
    )(half, g, recv)


def add_chip_partials(p32, recv, pos):
    _, r, c = p32.shape
    tr = _tile(r, 256) if r % 256 == 0 else r

    def body(pos_ref, p_ref, r_ref, o_ref):
        acc = p_ref[...]
        for j in range(N_CHIPS - 1):
            acc = acc + r_ref[j].astype(F32)
        o_ref[...] = acc

    return pl.pallas_call(
        body, name="add_chip_partials",
        grid_spec=pltpu.PrefetchScalarGridSpec(
            num_scalar_prefetch=1, grid=(r // tr,),
            in_specs=[pl.BlockSpec((None, tr, c), lambda i, ps: (ps[0], i, 0)),
                      pl.BlockSpec((N_CHIPS - 1, tr, c), lambda i, ps: (0, i, 0))],
            out_specs=pl.BlockSpec((None, tr, c), lambda i, ps: (ps[1], i, 0))),
        out_shape=jax.ShapeDtypeStruct((2, r, c), F32),
        compiler_params=_params("arbitrary"),
    )(pos, p32, recv)


def cast_into_gather(w, pos, dep, row0=0, nrows=None):
    c = w.shape[1]
    nrows = w.shape[0] if nrows is None else nrows
    r = nrows // 2
    common = math.gcd(r, row0) if row0 else r
    tr = max(w for w in range(16, min(common, 512) + 1, 16) if common % w == 0)
    nt = r // tr

    def body(pos_ref, w_ref, dep_ref, o_ref):
        o_ref[...] = _b(w_ref[...])

    return pl.pallas_call(
        body, name="cast_into_gather",
        grid_spec=pltpu.PrefetchScalarGridSpec(
            num_scalar_prefetch=1, grid=(2, nt),
            in_specs=[pl.BlockSpec((tr, c), lambda hf, i, ps: (row0 // tr + hf * nt + i, 0)), DEP_SPEC],
            out_specs=pl.BlockSpec((None, None, tr, c), lambda hf, i, ps: (ps[0], hf, i, 0))),
        out_shape=jax.ShapeDtypeStruct((N_CHIPS, 2, r, c), BF16),
        compiler_params=_params("arbitrary", "arbitrary"),
    )(pos, w, dep)


def build_bias(rel, buckets):
    nb, nh = rel.shape

    def body(rel_ref, bk_ref, o_ref):
        bk = bk_ref[...]
        for h in range(nh):
            acc = jnp.zeros(bk.shape, F32)
            for b in range(nb):
                acc = jnp.where(bk == b, rel_ref[b, h], acc)
            o_ref[h] = acc

    return pl.pallas_call(
        body, name="build_bias",
        in_specs=[pl.BlockSpec(memory_space=pltpu.SMEM), pl.BlockSpec(memory_space=pltpu.VMEM)],
        out_specs=pl.BlockSpec(memory_space=pltpu.VMEM),
        out_shape=jax.ShapeDtypeStruct((nh,) + buckets.shape, F32),
        compiler_params=_params(),
    )(rel, buckets)


SMALL_ROWS = 256


def kernel(x, ffn_norm, ffn_w1, ffn_w3, ffn_w2, ssm_norm, ssm_w_in, ssm_conv_w, ssm_conv_b, ssm_dt_bias, ssm_a_log, ssm_d, ssm_gate_norm, ssm_w_out, kv_norm, w_kv, k_norm, attn_norm, w_q, q_norm, sinks, w_o, rel_bias, loss_target, m_ffn_norm, m_ffn_w1, m_ffn_w3, m_ffn_w2, m_ssm_norm, m_ssm_w_in, m_ssm_conv_w, m_ssm_conv_b, m_ssm_dt_bias, m_ssm_a_log, m_ssm_d, m_ssm_gate_norm, m_ssm_w_out, m_kv_norm, m_w_kv, m_k_norm, m_attn_norm, m_w_q, m_q_norm, m_sinks, m_w_o, m_rel_bias, v_ffn_norm, v_ffn_w1, v_ffn_w3, v_ffn_w2, v_ssm_norm, v_ssm_w_in, v_ssm_conv_w, v_ssm_conv_b, v_ssm_dt_bias, v_ssm_a_log, v_ssm_d, v_ssm_gate_norm, v_ssm_w_out, v_kv_norm, v_w_kv, v_k_norm, v_attn_norm, v_w_q, v_q_norm, v_sinks, v_w_o, v_rel_bias):
    weights = dict(ffn_norm=ffn_norm, ffn_w1=ffn_w1, ffn_w3=ffn_w3, ffn_w2=ffn_w2, ssm_norm=ssm_norm,
                   ssm_w_in=ssm_w_in, ssm_conv_w=ssm_conv_w, ssm_conv_b=ssm_conv_b, ssm_dt_bias=ssm_dt_bias,
                   ssm_a_log=ssm_a_log, ssm_d=ssm_d, ssm_gate_norm=ssm_gate_norm, ssm_w_out=ssm_w_out,
                   kv_norm=kv_norm, w_kv=w_kv, k_norm=k_norm, attn_norm=attn_norm, w_q=w_q, q_norm=q_norm,
                   sinks=sinks, w_o=w_o, rel_bias=rel_bias)
    m_in = dict(ffn_norm=m_ffn_norm, ffn_w1=m_ffn_w1, ffn_w3=m_ffn_w3, ffn_w2=m_ffn_w2, ssm_norm=m_ssm_norm,
                ssm_w_in=m_ssm_w_in, ssm_conv_w=m_ssm_conv_w, ssm_conv_b=m_ssm_conv_b, ssm_dt_bias=m_ssm_dt_bias,
                ssm_a_log=m_ssm_a_log, ssm_d=m_ssm_d, ssm_gate_norm=m_ssm_gate_norm, ssm_w_out=m_ssm_w_out,
                kv_norm=m_kv_norm, w_kv=m_w_kv, k_norm=m_k_norm, attn_norm=m_attn_norm, w_q=m_w_q, q_norm=m_q_norm,
                sinks=m_sinks, w_o=m_w_o, rel_bias=m_rel_bias)
    v_in = dict(ffn_norm=v_ffn_norm, ffn_w1=v_ffn_w1, ffn_w3=v_ffn_w3, ffn_w2=v_ffn_w2, ssm_norm=v_ssm_norm,
                ssm_w_in=v_ssm_w_in, ssm_conv_w=v_ssm_conv_w, ssm_conv_b=v_ssm_conv_b, ssm_dt_bias=v_ssm_dt_bias,
                ssm_a_log=v_ssm_a_log, ssm_d=v_ssm_d, ssm_gate_norm=v_ssm_gate_norm, ssm_w_out=v_ssm_w_out,
                kv_norm=v_kv_norm, w_kv=v_w_kv, k_norm=v_k_norm, attn_norm=v_attn_norm, w_q=v_w_q, q_norm=v_q_norm,
                sinks=v_sinks, w_o=v_w_o, rel_bias=v_rel_bias)
    return _step(x[0], loss_target[0], weights, m_in, v_in)


BIG = ("ffn_w1", "ffn_w3", "ffn_w2", "ssm_w_in", "ssm_w_out", "w_kv", "w_q", "w_o")
SMALL = (("ffn_norm", True), ("ssm_norm", True), ("ssm_conv_w", True), ("ssm_conv_b", True),
         ("ssm_gate_norm", True), ("ssm_dt_bias", False), ("ssm_a_log", False), ("ssm_d", False),
         ("kv_norm", False), ("k_norm", False), ("attn_norm", False), ("q_norm", False), ("sinks", False),
         ("rel_bias", False))


FFN_W = BIG[:3]


def _small_layout(weights):
    off, table = 0, {}
    for name, sharded in SMALL:
        shape = weights[name].shape
        full = shape[:-1] + (shape[-1] * N_CHIPS,) if sharded else shape
        n = int(np.prod(full))
        table[name] = (off, full, sharded)
        off += n
    assert off <= SMALL_ROWS * 128
    return table


def _place_small(values, table, chip, scale_mask):
    flat = jnp.zeros((SMALL_ROWS * 128,), F32)
    for name, (off, full, sharded) in table.items():
        if not sharded:
            continue
        v = values[name].astype(F32)
        lead = int(np.prod(full[:-1]))
        w = v.shape[-1]
        blk = jnp.zeros((lead, full[-1]), F32)
        blk = lax.dynamic_update_slice(blk, v.reshape(lead, w) * scale_mask, (0, chip * w))
        flat = lax.dynamic_update_slice(flat, blk.reshape(-1), (off,))
    return flat.reshape(SMALL_ROWS, 128)


def _take_small(mat, table, name):
    off, full, _ = table[name]
    n = int(np.prod(full))
    return mat.reshape(-1)[off:off + n].reshape(full)


def _step(x, target, weights, m_in, v_in):
    t, d = x.shape
    xi, yi, ci = lax.axis_index("x"), lax.axis_index("y"), lax.axis_index("c")
    chip = 2 * xi + yi
    pos_arr = jnp.stack([chip, ci]).astype(jnp.int32)
    half_arr = jnp.reshape(ci, (1,)).astype(jnp.int32)

    fs = weights["ffn_w1"].shape[-1]
    ffn_rows = {"ffn_w1": d, "ffn_w3": d, "ffn_w2": fs}
    w2d = {n: weights[n].reshape(-1, weights[n].shape[-1]) for n in BIG}
    mamba_w = ("ssm_w_in", "ssm_w_out")
    late_w = ("w_kv", "w_q", "w_o")
    fs_, fr_, fbufs, tok_f = gather_start(
        [cast_into_gather(w2d[n], pos_arr, pos_arr, 0, ffn_rows[n]) for n in FFN_W], pos_arr, "first")
    ms, mr, mbufs, tok_m = gather_start([cast_into_gather(w2d[n], pos_arr, tok_f) for n in mamba_w], tok_f, "mamba")
    ls, lr, lbufs, tok_l = gather_start(
        [cast_into_gather(w2d[n], pos_arr, tok_f, ffn_rows[n], 3 * ffn_rows[n]) for n in FFN_W]
        + [cast_into_gather(w2d[n], pos_arr, tok_f) for n in late_w], tok_m, "late")
    first = forward_to_sibling(gather_wait(fs_, fr_, fbufs, tok_l, "first"))
    no_dep = jnp.zeros((8, 128), F32)
    table = _small_layout(weights)
    south = (ci == 0).astype(F32)
    small = allreduce_small(_place_small(weights, table, chip, south))
    sp = {n: _take_small(small, table, n) if sh else weights[n] for n, sh in SMALL}

    ffn_first = [first[0].reshape(N_CHIPS, 1, d, fs), first[1].reshape(N_CHIPS, 1, d, fs),
                 first[2].reshape(N_CHIPS, 1, fs, d)]
    ffn_g = sp["ffn_norm"]
    h0 = x
    h1, a00, b00 = ffn_fwd(h0, ffn_g[0, 0].reshape(1, d), *ffn_first, 0, no_dep)
    gathered = dict(zip(mamba_w, forward_to_sibling(gather_wait(ms, mr, mbufs, h1, "mamba"))))
    n_in = weights["ssm_w_in"].shape[-1] * N_CHIPS
    di = weights["ssm_w_out"].shape[1] * N_CHIPS
    nheads = di // SSM_HEAD_DIM
    conv_dim = n_in - di - nheads
    w_in_full = jnp.moveaxis(gathered["ssm_w_in"].reshape(N_CHIPS, d, n_in // N_CHIPS), 0, 1).reshape(d, n_in)
    hpg = nheads // SSM_GROUPS

    def spread_heads(v):
        lead = v.shape[:-1]
        v = v.reshape(lead + (SSM_GROUPS, hpg))
        v = jnp.pad(v, [(0, 0)] * len(lead) + [(0, 0), (0, 128 - hpg)])
        return v.reshape(lead + (SSM_GROUPS * 128,))

    def gather_heads(v):
        lead = v.shape[:-1]
        return v.reshape(lead + (SSM_GROUPS, 128))[..., :hpg].reshape(lead + (nheads,))

    dt_col0 = di + conv_dim
    n_zx = dt_col0 + SSM_GROUPS * 128
    w_in = jnp.concatenate([w_in_full[:, :dt_col0], spread_heads(w_in_full[:, dt_col0:])], axis=1)
    w_out = gathered["ssm_w_out"].reshape(di, d)
    nkv = weights["w_kv"].shape[1] // (2 * ATT_HEAD_DIM)
    assert nkv == 2
    nh = weights["w_q"].shape[-1] // ATT_HEAD_DIM

    ssm_g = sp["ssm_norm"].reshape(1, d)
    cw = jnp.pad(sp["ssm_conv_w"].reshape(SSM_CONV, conv_dim), [(0, 8 - SSM_CONV), (0, 0)])
    cb = sp["ssm_conv_b"].reshape(1, conv_dim)
    gate_g = sp["ssm_gate_norm"].reshape(1, di)
    dt_bias = spread_heads(sp["ssm_dt_bias"].reshape(1, nheads))
    a_log = spread_heads(sp["ssm_a_log"].reshape(1, nheads))
    d_skip = spread_heads(sp["ssm_d"].reshape(1, nheads))
    kv_g = sp["kv_norm"].reshape(1, d)
    k_g = jnp.tile(sp["k_norm"].reshape(1, ATT_HEAD_DIM), (1, 2))
    attn_g = sp["attn_norm"].reshape(1, d)
    q_g = jnp.tile(sp["q_norm"].reshape(1, ATT_HEAD_DIM), (1, 2))
    sink_row = jnp.pad(sp["sinks"].reshape(1, nh), [(0, 0), (0, 128 - nh)])
    buckets = jnp.asarray(_t5_buckets())
    biasm = build_bias(sp["rel_bias"], buckets).reshape(nh * ATT_WINDOW, 2 * ATT_WINDOW)

    zx = norm_mm(h1, ssm_g, w_in)
    xc = conv_fwd(zx, cw, cb, di)
    y_ssd, states = ssd_fwd(xc, zx, dt_bias, a_log, d_skip, dt_col0)
    h2 = gate_out_fwd(h1, y_ssd, zx, gate_g, w_out)

    late = forward_to_sibling(gather_wait(ls, lr, lbufs, h2, "late"))
    ffn_rest = [late[0].reshape(N_CHIPS, 3, d, fs), late[1].reshape(N_CHIPS, 3, d, fs),
                late[2].reshape(N_CHIPS, 3, fs, d)]
    gathered.update(zip(late_w, late[3:]))
    wkv_heads = gathered["w_kv"].reshape(d, 2 * nkv, 1, ATT_HEAD_DIM)
    w_kvd = jnp.broadcast_to(wkv_heads, (d, 2 * nkv, 2, ATT_HEAD_DIM)).reshape(d, 4 * nkv * ATT_HEAD_DIM)
    wq = gathered["w_q"].reshape(d, -1)
    wo = gathered["w_o"].reshape(-1, d)

    def ffn_w(layer, idx):
        blk = 2 * layer + idx
        return (*ffn_first, 0) if blk == 0 else (*ffn_rest, blk - 1)

    h3, a01, b01 = ffn_fwd(h2, ffn_g[0, 1].reshape(1, d), *ffn_w(0, 1), no_dep)
    kvd = norm_mm(h3, kv_g, w_kvd)
    h4, a10, b10 = ffn_fwd(h3, ffn_g[1, 0].reshape(1, d), *ffn_w(1, 0), no_dep)
    qp = norm_mm(h4, attn_g, wq)
    h5 = attn_fwd(h4, qp, kvd, biasm, sink_row, q_g, k_g, wo)
    h6, a11, b11 = ffn_fwd(h5, ffn_g[1, 1].reshape(1, d), *ffn_w(1, 1), no_dep)
    loss_part, d6 = loss_head(h6, target)
    loss = lax.psum(loss_part[0, 0], ("x", "y", "c"))

    gfn = [[None, None], [None, None]]

    pending = []

    def reduce_start(pieces, tag):
        views = [g.reshape(N_CHIPS, 2, g.shape[1] // 2, g.shape[2]) for _, g in pieces]
        recv1 = exchange_sibling_halves(views)
        p32, p16 = zip(*[add_sibling(g, r, half_arr) for g, r in zip(views, recv1)])
        ss, rs, parts, lands, token = chip_partials_start(list(p16), tag)
        pending.append(([k for k, _ in pieces], p32, ss, rs, parts, lands, tag))
        return token

    def ffn_back(h_in, dy, a_s, b_s, layer, idx, dep):
        dh, u, da, db, s, dg = ffn_bwd(h_in, dy, ffn_g[layer, idx].reshape(1, d), a_s, b_s, *ffn_w(layer, idx), dep)
        gfn[layer][idx] = dg
        return dh, [(("ffn_w1", layer, idx), wgrad_grouped_b(u, da)), (("ffn_w3", layer, idx), wgrad_grouped_b(u, db)),
                    (("ffn_w2", layer, idx), wgrad_grouped_a(s, dy, 0.5))]

    d5, pieces = ffn_back(h5, d6, a11, b11, 1, 1, no_dep)
    tok = reduce_start(pieces, "ffn11")
    dqp, dkvd, o16, dbiasm, dsinks, dqg, dkg = attn_bwd(d5, qp, kvd, biasm, sink_row, q_g, k_g, wo, tok)
    g_wo = wgrad(o16, d5)
    d4, u_q, g_attn_norm = norm_mm_bwd(h4, attn_g, wq, dqp, d5, no_dep)
    g_wq = wgrad(u_q, dqp)
    d3a, pieces = ffn_back(h3, d4, a10, b10, 1, 0, no_dep)
    pieces += [(("w_o",), g_wo.reshape(N_CHIPS, -1, d)), (("w_q",), g_wq.reshape(N_CHIPS, d // N_CHIPS, -1))]
    tok = reduce_start(pieces, "ffn10")
    d3, u_kv, g_kv_norm = norm_mm_bwd(h3, kv_g, w_kvd, dkvd, d3a, tok, 0.5)
    g_wkvd = wgrad(u_kv, dkvd)
    g_wkv = g_wkvd.reshape(d, 2 * nkv, 2, ATT_HEAD_DIM)[:, :, 0, :].reshape(d, 2 * nkv * ATT_HEAD_DIM)
    d2, pieces = ffn_back(h2, d3, a01, b01, 0, 1, no_dep)
    pieces += [(("w_kv",), g_wkv.reshape(N_CHIPS, d // N_CHIPS, -1))]
    tok = reduce_start(pieces, "ffn01")
    dzx, dy_ssd, yn16, g_gate = gate_out_bwd(d2, y_ssd, zx, gate_g, w_out, n_zx, tok)
    g_wout = wgrad(yn16, d2)
    dzx, dxs, dbm, dcm, g_dtb, g_alog, g_dsk = ssd_bwd(dzx, dy_ssd, xc, zx, states, dt_bias, a_log, d_skip, dt_col0)
    dzx, g_cw, g_cb = conv_bwd(dzx, zx, dxs, dbm, dcm, cw, cb, di)
    d1, u_in, g_ssm_norm = norm_mm_bwd(h1, ssm_g, w_in, dzx, d2, no_dep)
    g_win = wgrad(u_in, dzx)
    g_win_full = jnp.concatenate([g_win[:, :dt_col0], gather_heads(g_win[:, dt_col0:])], axis=1)
    pieces = [(("ssm_w_in",), jnp.moveaxis(g_win_full.reshape(d, N_CHIPS, n_in // N_CHIPS), 1, 0)),
              (("ssm_w_out",), g_wout.reshape(N_CHIPS, di // N_CHIPS, d))]
    tok = reduce_start(pieces, "mamba")
    grad_x, pieces = ffn_back(h0, d1, a00, b00, 0, 0, tok)
    tok = reduce_start(pieces, "ffn00")
    g_relb = rel_bias_bwd(dbiasm.reshape(nh, ATT_WINDOW, 2 * ATT_WINDOW), buckets)

    reduced = {}
    for keys, p32, ss, rs, parts, lands, tag in pending:
        lands = chip_partials_wait(ss, rs, parts, lands, tok, tag)
        for k, p, r in zip(keys, p32, lands):
            reduced[k] = add_chip_partials(p, r, pos_arr)
    keys = list(reduced)
    shared = dict(zip(keys, share_with_sibling([reduced[k] for k in keys])))
    grads = {}
    for n in FFN_W:
        blocks = [shared[(n, l, i)].reshape(1, ffn_rows[n], -1) for l in range(2) for i in range(2)]
        grads[n] = jnp.concatenate(blocks, axis=0).reshape(weights[n].shape)
    for n in BIG[3:]:
        grads[n] = shared[(n,)].reshape(weights[n].shape)

    small_grads = {
        "ffn_norm": jnp.stack([jnp.stack([gfn[l][i].reshape(d) for i in range(2)]) for l in range(2)]),
        "ssm_norm": g_ssm_norm.reshape(1, d),
        "ssm_conv_w": g_cw[:SSM_CONV].reshape(1, SSM_CONV, conv_dim),
        "ssm_conv_b": g_cb.reshape(1, conv_dim),
        "ssm_gate_norm": g_gate.reshape(1, di),
        "ssm_dt_bias": gather_heads(g_dtb.reshape(1, -1)), "ssm_a_log": gather_heads(g_alog.reshape(1, -1)),
        "ssm_d": gather_heads(g_dsk.reshape(1, -1)),
        "kv_norm": g_kv_norm.reshape(d), "k_norm": dkg[0, :ATT_HEAD_DIM], "attn_norm": g_attn_norm.reshape(1, d),
        "q_norm": dqg[:, :ATT_HEAD_DIM], "sinks": dsinks[:, :nh], "rel_bias": g_relb[:, :nh],
    }
    flat = jnp.zeros((SMALL_ROWS * 128,), F32)
    for name, (off, fshape, _) in table.items():
        flat = lax.dynamic_update_slice(flat, small_grads[name].astype(F32).reshape(-1), (off,))
    small_sum = allreduce_small(flat.reshape(SMALL_ROWS, 128))
    for name, (off, fshape, sharded) in table.items():
        g = _take_small(small_sum, table, name)
        if sharded:
            w = weights[name].shape[-1]
            lead = int(np.prod(fshape[:-1]))
            g = lax.dynamic_slice(g.reshape(lead, fshape[-1]), (0, chip * w), (lead, w)).reshape(weights[name].shape)
        grads[name] = g.reshape(weights[name].shape)

    names = list(weights)
    deltas, new_m, new_v = {}, {}, {}
    small_names = [n for n, _ in SMALL]
    for n in BIG:
        shp = weights[n].shape
        v2 = lambda a: a.reshape(-1, shp[-1])
        dl, nm, nv = adamw(v2(weights[n]), v2(grads[n]), v2(m_in[n]), v2(v_in[n]))
        deltas[n], new_m[n], new_v[n] = dl.reshape(shp), nm.reshape(shp), nv.reshape(shp)
    sizes = [int(np.prod(weights[n].shape)) for n in small_names]
    tot = sum(sizes)
    rows = -(-tot // 128)
    rows = -(-rows // 8) * 8

    def pack(dct):
        flat = jnp.concatenate([dct[n].reshape(-1) for n in small_names])
        return jnp.pad(flat, (0, rows * 128 - tot), constant_values=1.0).reshape(rows, 128)

    dl, nm, nv = adamw(pack(weights), pack(grads), pack(m_in), pack(v_in))
    off = 0
    for n, sz in zip(small_names, sizes):
        shp = weights[n].shape
        take = lambda a: a.reshape(-1)[off:off + sz].reshape(shp)
        deltas[n], new_m[n], new_v[n] = take(dl), take(nm), take(nv)
        off += sz

    return (loss, grad_x[None], *[grads[n] for n in names], *[deltas[n] for n in names],
            *[new_m[n] for n in names], *[new_v[n] for n in names])
```

```python
import functools
import math

import jax
import jax.numpy as jnp
import numpy as np
from jax import lax
from jax.experimental import pallas as pl
from jax.experimental.pallas import tpu as pltpu

F32 = jnp.float32
BF16 = jnp.bfloat16
EPS = 1e-6
MESH = pl.DeviceIdType.MESH

SSM_HEAD_DIM = 64
SSM_GROUPS = 4
SSM_STATE = 128
SSM_CONV = 4
SSM_CHUNK = 256
ATT_HEAD_DIM = 64
ATT_WINDOW = 128
REL_BUCKETS = 32
N_CHIPS = 4

ADAM_LR = 0.001
ADAM_B1 = 0.9
ADAM_B2 = 0.999
ADAM_EPS = 1e-08
ADAM_WD = 0.01
ADAM_STEP = 10

VMEM_LIMIT_BYTES = 56 * 1024 * 1024
NEG = -1e30


DEP_SPEC = pl.BlockSpec(memory_space=pl.ANY)


def _params(*sem):
    return pltpu.CompilerParams(dimension_semantics=sem if sem else None, vmem_limit_bytes=VMEM_LIMIT_BYTES)


def _dot(a, b):
    return jnp.dot(a, b, preferred_element_type=F32)


def _dot_nt(a, b):
    return lax.dot_general(a, b, (((1,), (1,)), ((), ())), preferred_element_type=F32)


def _dot_tn(a, b):
    return lax.dot_general(a, b, (((0,), (0,)), ((), ())), preferred_element_type=F32)


def _b(x):
    return x.astype(BF16)


@jax.custom_vjp
def _bmm(a, b):
    return _dot(_b(a), _b(b))


def _bmm_fwd(a, b):
    return _bmm(a, b), (a, b)


def _bmm_bwd(res, g):
    a, b = res
    g16 = _b(g)
    return _dot_nt(g16, _b(b)).astype(a.dtype), _dot_tn(_b(a), g16).astype(b.dtype)


_bmm.defvjp(_bmm_fwd, _bmm_bwd)


@jax.custom_vjp
def _bmm_nt(a, b):
    return _dot_nt(_b(a), _b(b))


def _bmm_nt_fwd(a, b):
    return _bmm_nt(a, b), (a, b)


def _bmm_nt_bwd(res, g):
    a, b = res
    g16 = _b(g)
    return _dot(g16, _b(b)).astype(a.dtype), _dot_tn(g16, _b(a)).astype(b.dtype)


_bmm_nt.defvjp(_bmm_nt_fwd, _bmm_nt_bwd)


@jax.custom_vjp
def _bmm_tn(a, b):
    return _dot_tn(_b(a), _b(b))


def _bmm_tn_fwd(a, b):
    return _bmm_tn(a, b), (a, b)


def _bmm_tn_bwd(res, g):
    a, b = res
    g16 = _b(g)
    return _dot_nt(_b(b), g16).astype(a.dtype), _dot(_b(a), g16).astype(b.dtype)


_bmm_tn.defvjp(_bmm_tn_fwd, _bmm_tn_bwd)


def _split3(x):
    hi = _b(x)
    r = x - hi.astype(F32)
    mid = _b(r)
    lo = _b(r - mid.astype(F32))
    return hi, mid, lo


def _x_left_raw(m, x):
    hi, mid, lo = _split3(x)
    return _dot(m, hi) + _dot(m, mid) + _dot(m, lo)


def _x_left_t_raw(m, x):
    hi, mid, lo = _split3(x)
    return _dot_tn(m, hi) + _dot_tn(m, mid) + _dot_tn(m, lo)


def _x_right_raw(x, m):
    hi, mid, lo = _split3(x)
    return _dot(hi, m) + _dot(mid, m) + _dot(lo, m)


def _x_right_t_raw(x, m):
    hi, mid, lo = _split3(x)
    return _dot_nt(hi, m) + _dot_nt(mid, m) + _dot_nt(lo, m)


@jax.custom_vjp
def _xleft(m, x):
    return _x_left_raw(m, x)


_xleft.defvjp(lambda m, x: (_x_left_raw(m, x), m),
              lambda m, g: (jnp.zeros_like(m), _x_left_t_raw(m, g)))


@jax.custom_vjp
def _xright(x, m):
    return _x_right_raw(x, m)


_xright.defvjp(lambda x, m: (_x_right_raw(x, m), m),
               lambda m, g: (_x_right_t_raw(g, m), jnp.zeros_like(m)))


def _sigmoid(x):
    return 1.0 / (1.0 + jnp.exp(-x))


def _silu(x):
    return x * _sigmoid(x)


def _softplus(x):
    return jnp.maximum(x, 0.0) + jnp.log(1.0 + jnp.exp(-jnp.abs(x)))


def _rms(x):
    return x * lax.rsqrt(jnp.mean(x * x, axis=-1, keepdims=True) + EPS)


def _iota(shape, dim):
    return lax.broadcasted_iota(jnp.int32, shape, dim)


def _blockdiag64(n):
    return jnp.where(_iota((n, n), 0) // 64 == _iota((n, n), 1) // 64, 1.0, 0.0).astype(BF16)


def _group64_rms(x, seg_sum):
    ms = seg_sum(x * x) * (1.0 / 64.0)
    return x * lax.rsqrt(ms + EPS)


def _fold64(x):
    ax = x.ndim - 1
    w = x.shape[ax]
    lo = (_iota(x.shape, ax) % 128) < 64
    return x + jnp.where(lo, pltpu.roll(x, w - 64, ax), pltpu.roll(x, 64, ax))


def _tile(n, want):
    t = min(n, want)
    assert n % t == 0, (n, t)
    return t


def _lane_tile(n, cap=1536):
    if n <= cap:
        return n
    return max(w for w in range(128, cap + 1, 128) if n % w == 0)


def ffn_fwd(h, g, w1, w3, w2, blk, dep):
    t, d = h.shape
    nk, fs = w1.shape[0], w1.shape[-1]
    tm = _tile(t, 512)

    def body(h_ref, g_ref, w1_ref, w3_ref, w2_ref, dep_ref, o_ref, a_ref, b_ref, u_scr, acc):
        k = pl.program_id(1)

        @pl.when(k == 0)
        def _():
            u_scr[...] = _b(_rms(h_ref[...]) * g_ref[...])
            acc[...] = jnp.zeros_like(acc)

        u = u_scr[...]
        a = _dot(u, w1_ref[...])
        b = _dot(u, w3_ref[...])
        a_ref[...] = _b(a)
        b_ref[...] = _b(b)
        acc[...] += _dot(_b(_silu(a) * b), w2_ref[...])

        @pl.when(k == nk - 1)
        def _():
            o_ref[...] = h_ref[...] + 0.5 * acc[...]

    wspec = lambda r, c: pl.BlockSpec((None, None, r, c), lambda i, k: (k, blk, 0, 0))
    return pl.pallas_call(
        body, name="ffn_fwd",
        grid=(t // tm, nk),
        in_specs=[pl.BlockSpec((tm, d), lambda i, k: (i, 0)), pl.BlockSpec((1, d), lambda i, k: (0, 0)),
                  wspec(d, fs), wspec(d, fs), wspec(fs, d), DEP_SPEC],
        out_specs=[pl.BlockSpec((tm, d), lambda i, k: (i, 0)),
                   pl.BlockSpec((None, tm, fs), lambda i, k: (k, i, 0)),
                   pl.BlockSpec((None, tm, fs), lambda i, k: (k, i, 0))],
        out_shape=[jax.ShapeDtypeStruct((t, d), F32), jax.ShapeDtypeStruct((nk, t, fs), BF16),
                   jax.ShapeDtypeStruct((nk, t, fs), BF16)],
        scratch_shapes=[pltpu.VMEM((tm, d), BF16), pltpu.VMEM((tm, d), F32)],
        compiler_params=_params("arbitrary", "arbitrary"),
    )(h, g, w1, w3, w2, dep)


def ffn_bwd(h, dy, g, a_s, b_s, w1, w3, w2, blk, dep):
    t, d = h.shape
    nk, fs = w1.shape[0], w1.shape[-1]
    tm = _tile(t, 512)

    def body(h_ref, dy_ref, g_ref, a_ref, b_ref, w1_ref, w3_ref, w2_ref, dep_ref,
             dh_ref, u_ref, da_ref, db_ref, s_ref, dg_ref, dyh_scr, du_acc, da0, db0, da1, db1):
        i, k = pl.program_id(0), pl.program_id(1)

        @pl.when(k == 0)
        def _():
            dyh_scr[...] = _b(0.5 * dy_ref[...])
            du_acc[...] = jnp.zeros_like(du_acc)

        @pl.when((k == 0) & (i == 0))
        def _():
            dg_ref[...] = jnp.zeros_like(dg_ref)

        def step(prev, cur):
            if prev is not None:
                du_acc[...] += _dot_nt(prev[0][...], w1_ref[...]) + _dot_nt(prev[1][...], w3_ref[...])
            if cur is not None:
                ds = _dot_nt(dyh_scr[...], w2_ref[...])
                a = a_ref[...].astype(F32)
                b = b_ref[...].astype(F32)
                sig = _sigmoid(a)
                sl = a * sig
                s_ref[...] = _b(sl * b)
                da = _b(ds * b * (sig * (1.0 + a * (1.0 - sig))))
                db = _b(ds * sl)
                da_ref[...] = da
                db_ref[...] = db
                cur[0][...] = da
                cur[1][...] = db

        even, odd = (da0, db0), (da1, db1)

        @pl.when(k == 0)
        def _():
            step(None, even)

        @pl.when((k > 0) & (k < nk) & (k % 2 == 1))
        def _():
            step(even, odd)

        @pl.when((k > 0) & (k < nk) & (k % 2 == 0))
        def _():
            step(odd, even)

        @pl.when(k == nk)
        def _():
            step(odd if nk % 2 == 0 else even, None)
            hh = h_ref[...]
            rstd = lax.rsqrt(jnp.mean(hh * hh, axis=-1, keepdims=True) + EPS)
            xh = hh * rstd
            gg = g_ref[...]
            u_ref[...] = _b(xh * gg)
            du = du_acc[...]
            dg_ref[...] += jnp.sum(du * xh, axis=0, keepdims=True)
            dxh = du * gg
            dh_ref[...] = dy_ref[...] + rstd * (dxh - xh * jnp.mean(dxh * xh, axis=-1, keepdims=True))

    cur = lambda k: jnp.minimum(k, nk - 1)
    prv = lambda k: jnp.maximum(k - 1, 0)
    wcur = lambda r, c: pl.BlockSpec((None, None, r, c), lambda i, k: (cur(k), blk, 0, 0))
    wprv = lambda r, c: pl.BlockSpec((None, None, r, c), lambda i, k: (prv(k), blk, 0, 0))
    tok = pl.BlockSpec((tm, d), lambda i, k: (i, 0))
    hid = pl.BlockSpec((None, tm, fs), lambda i, k: (cur(k), i, 0))
    return pl.pallas_call(
        body, name="ffn_bwd",
        grid=(t // tm, nk + 1),
        in_specs=[tok, tok, pl.BlockSpec((1, d), lambda i, k: (0, 0)), hid, hid, wprv(d, fs), wprv(d, fs), wcur(fs, d),
                  DEP_SPEC],
        out_specs=[tok, tok, hid, hid, hid, pl.BlockSpec((1, d), lambda i, k: (0, 0))],
        out_shape=[jax.ShapeDtypeStruct((t, d), F32), jax.ShapeDtypeStruct((t, d), BF16),
                   jax.ShapeDtypeStruct((nk, t, fs), BF16), jax.ShapeDtypeStruct((nk, t, fs), BF16),
                   jax.ShapeDtypeStruct((nk, t, fs), BF16), jax.ShapeDtypeStruct((1, d), F32)],
        scratch_shapes=[pltpu.VMEM((tm, d), BF16), pltpu.VMEM((tm, d), F32)] + [pltpu.VMEM((tm, fs), BF16)] * 4,
        compiler_params=_params("arbitrary", "arbitrary"),
    )(h, dy, g, a_s, b_s, w1, w3, w2, dep)


def wgrad_grouped_b(a, bs, dep, scale=1.0):
    t, m = a.shape
    ng, _, n = bs.shape
    tk = _tile(t, 2048)

    def body(a_ref, b_ref, dep_ref, o_ref):
        j = pl.program_id(1)

        @pl.when(j == 0)
        def _():
            o_ref[...] = jnp.zeros_like(o_ref)

        o_ref[...] += _dot_tn(_b(a_ref[...]), _b(b_ref[...]))

        if scale != 1.0:
            @pl.when(j == pl.num_programs(1) - 1)
            def _():
                o_ref[...] = o_ref[...] * scale

    return pl.pallas_call(
        body, name="wgrad_gb",
        grid=(ng, t // tk),
        in_specs=[pl.BlockSpec((tk, m), lambda k, j: (j, 0)), pl.BlockSpec((None, tk, n), lambda k, j: (k, j, 0)),
                  DEP_SPEC],
        out_specs=pl.BlockSpec((None, m, n), lambda k, j: (k, 0, 0)),
        out_shape=jax.ShapeDtypeStruct((ng, m, n), F32),
        compiler_params=_params("arbitrary", "arbitrary"),
    )(a, bs, dep)


def wgrad_grouped_a(as_, b, scale=1.0):
    ng, t, m = as_.shape
    n = b.shape[1]
    tk = _tile(t, 2048)

    def body(a_ref, b_ref, o_ref):
        j = pl.program_id(1)

        @pl.when(j == 0)
        def _():
            o_ref[...] = jnp.zeros_like(o_ref)

        o_ref[...] += _dot_tn(_b(a_ref[...]), _b(b_ref[...]))

        if scale != 1.0:
            @pl.when(j == pl.num_programs(1) - 1)
            def _():
                o_ref[...] = o_ref[...] * scale

    return pl.pallas_call(
        body, name="wgrad_ga",
        grid=(ng, t // tk),
        in_specs=[pl.BlockSpec((None, tk, m), lambda k, j: (k, j, 0)), pl.BlockSpec((tk, n), lambda k, j: (j, 0))],
        out_specs=pl.BlockSpec((None, m, n), lambda k, j: (k, 0, 0)),
        out_shape=jax.ShapeDtypeStruct((ng, m, n), F32),
        compiler_params=_params("arbitrary", "arbitrary"),
    )(as_, b)


def wgrad(a, b):
    t, m = a.shape
    n = b.shape[1]
    tk = _tile(t, 1024)
    tn = _lane_tile(n, 1536 if m <= 1024 else 512)

    def body(a_ref, b_ref, o_ref):
        @pl.when(pl.program_id(1) == 0)
        def _():
            o_ref[...] = jnp.zeros_like(o_ref)

        o_ref[...] += _dot_tn(_b(a_ref[...]), _b(b_ref[...]))

    return pl.pallas_call(
        body, name="wgrad",
        grid=(n // tn, t // tk),
        in_specs=[pl.BlockSpec((tk, m), lambda c, j: (j, 0)), pl.BlockSpec((tk, tn), lambda c, j: (j, c))],
        out_specs=pl.BlockSpec((m, tn), lambda c, j: (0, c)),
        out_shape=jax.ShapeDtypeStruct((m, n), F32),
        compiler_params=_params("arbitrary", "arbitrary"),
    )(a, b)


def norm_mm(h, g, w):
    t, d = h.shape
    n = w.shape[1]
    tm = _tile(t, 1024)
    tn = _lane_tile(n)

    def body(h_ref, g_ref, w_ref, o_ref, u_scr):
        @pl.when(pl.program_id(1) == 0)
        def _():
            u_scr[...] = _b(_rms(h_ref[...]) * g_ref[...])

        o_ref[...] = _dot(u_scr[...], w_ref[...])

    return pl.pallas_call(
        body, name="norm_mm",
        grid=(t // tm, n // tn),
        in_specs=[pl.BlockSpec((tm, d), lambda i, j: (i, 0)), pl.BlockSpec((1, d), lambda i, j: (0, 0)),
                  pl.BlockSpec((d, tn), lambda i, j: (0, j))],
        out_specs=pl.BlockSpec((tm, tn), lambda i, j: (i, j)),
        out_shape=jax.ShapeDtypeStruct((t, n), F32),
        scratch_shapes=[pltpu.VMEM((tm, d), BF16)],
        compiler_params=_params("arbitrary", "arbitrary"),
    )(h, g, w)


def norm_mm_bwd(h, g, w, dout, dres, dep, scale=1.0):
    t, d = h.shape
    n = w.shape[1]
    tm = _tile(t, 512)
    tn = _lane_tile(n)
    nj = n // tn

    def body(h_ref, g_ref, w_ref, do_ref, dr_ref, dep_ref, dh_ref, u_ref, dg_ref, du_acc):
        i, j = pl.program_id(0), pl.program_id(1)

        @pl.when(j == 0)
        def _():
            du_acc[...] = jnp.zeros_like(du_acc)

        @pl.when((j == 0) & (i == 0))
        def _():
            dg_ref[...] = jnp.zeros_like(dg_ref)

        du_acc[...] += _dot_nt(_b(do_ref[...]), w_ref[...])

        @pl.when(j == nj - 1)
        def _():
            hh = h_ref[...]
            rstd = lax.rsqrt(jnp.mean(hh * hh, axis=-1, keepdims=True) + EPS)
            xh = hh * rstd
            gg = g_ref[...]
            u_ref[...] = _b(xh * gg)
            du = du_acc[...] * scale
            dg_ref[...] += jnp.sum(du * xh, axis=0, keepdims=True)
            dxh = du * gg
            dh_ref[...] = dr_ref[...] + rstd * (dxh - xh * jnp.mean(dxh * xh, axis=-1, keepdims=True))

    tok = pl.BlockSpec((tm, d), lambda i, j: (i, 0))
    return pl.pallas_call(
        body, name="norm_mm_bwd",
        grid=(t // tm, nj),
        in_specs=[tok, pl.BlockSpec((1, d), lambda i, j: (0, 0)), pl.BlockSpec((d, tn), lambda i, j: (0, j)),
                  pl.BlockSpec((tm, tn), lambda i, j: (i, j)), tok, DEP_SPEC],
        out_specs=[tok, tok, pl.BlockSpec((1, d), lambda i, j: (0, 0))],
        out_shape=[jax.ShapeDtypeStruct((t, d), F32), jax.ShapeDtypeStruct((t, d), BF16),
                   jax.ShapeDtypeStruct((1, d), F32)],
        scratch_shapes=[pltpu.VMEM((tm, d), F32)],
        compiler_params=_params("arbitrary", "arbitrary"),
    )(h, g, w, dout, dres, dep)


CONV_COLS = 512


CONV_ROWS = 64


def _conv_pre(ext, w, b, r0, n):
    return (b + w[0:1] * ext[pl.ds(5 + r0, n), :] + w[1:2] * ext[pl.ds(6 + r0, n), :]
            + w[2:3] * ext[pl.ds(7 + r0, n), :] + w[3:4] * ext[pl.ds(8 + r0, n), :])


def conv_fwd(zx, cw, cb, col0):
    t = zx.shape[0]
    c = cw.shape[1]
    tm = _tile(t, 512)
    cb0 = col0 // CONV_COLS

    rc = _tile(tm, CONV_ROWS)

    def body(x_ref, w_ref, b_ref, o_ref, ext):
        @pl.when(pl.program_id(1) == 0)
        def _():
            ext[0:8, :] = jnp.zeros((8, CONV_COLS), F32)

        ext[8:, :] = x_ref[...]
        w, b = w_ref[...], b_ref[...]
        for r0 in range(0, tm, rc):
            o_ref[r0:r0 + rc, :] = _silu(_conv_pre(ext, w, b, r0, rc))
        ext[0:8, :] = ext[tm:tm + 8, :]

    return pl.pallas_call(
        body, name="conv_fwd",
        grid=(c // CONV_COLS, t // tm),
        in_specs=[pl.BlockSpec((tm, CONV_COLS), lambda j, i: (i, cb0 + j)),
                  pl.BlockSpec((8, CONV_COLS), lambda j, i: (0, j)), pl.BlockSpec((1, CONV_COLS), lambda j, i: (0, j))],
        out_specs=pl.BlockSpec((tm, CONV_COLS), lambda j, i: (i, j)),
        out_shape=jax.ShapeDtypeStruct((t, c), F32),
        scratch_shapes=[pltpu.VMEM((tm + 8, CONV_COLS), F32)],
        compiler_params=_params("arbitrary", "arbitrary"),
    )(zx, cw, cb)


def conv_bwd(dzx, zx, dxs, dbm, dcm, cw, cb, col0):
    t = zx.shape[0]
    c = cw.shape[1]
    tm = _tile(t, 512)
    nt = t // tm
    cb0 = col0 // CONV_COLS
    nxs = dxs.shape[1] // CONV_COLS
    hb = tm // 8

    rc = _tile(tm, CONV_ROWS)

    def body(dzx_ref, x_ref, xh_ref, dxs_ref, db_ref, dc_ref, w_ref, b_ref, o_ref, dw_ref, dbias_ref, ext, gy):
        j, i = pl.program_id(0), pl.program_id(1)
        ri = nt - 1 - i

        @pl.when(i == 0)
        def _():
            gy[tm:tm + 8, :] = jnp.zeros((8, CONV_COLS), F32)
            dw_ref[...] = jnp.zeros_like(dw_ref)
            dbias_ref[...] = jnp.zeros_like(dbias_ref)

        ext[0:8, :] = jnp.where(ri > 0, xh_ref[...], 0.0)
        ext[8:, :] = x_ref[...]
        w, b = w_ref[...], b_ref[...]
        dw = [jnp.zeros((1, CONV_COLS), F32) for _ in range(SSM_CONV)]
        dbias = jnp.zeros((1, CONV_COLS), F32)
        for r0 in range(0, tm, rc):
            rows = pl.ds(r0, rc)
            win = [ext[pl.ds(5 + tap + r0, rc), :] for tap in range(SSM_CONV)]
            y = b + w[0:1] * win[0] + w[1:2] * win[1] + w[2:3] * win[2] + w[3:4] * win[3]
            sig = _sigmoid(y)
            dout = jnp.where(j < nxs, dxs_ref[rows, :], jnp.where(j == nxs, db_ref[rows, :], dc_ref[rows, :]))
            g = dout * (sig * (1.0 + y * (1.0 - sig)))
            gy[rows, :] = g
            dbias = dbias + jnp.sum(g, axis=0, keepdims=True)
            for tap in range(SSM_CONV):
                dw[tap] = dw[tap] + jnp.sum(g * win[tap], axis=0, keepdims=True)
        for r0 in range(0, tm, rc):
            o_ref[r0:r0 + rc, :] = (w[0:1] * gy[pl.ds(r0 + 3, rc), :] + w[1:2] * gy[pl.ds(r0 + 2, rc), :]
                                    + w[2:3] * gy[pl.ds(r0 + 1, rc), :] + w[3:4] * gy[pl.ds(r0, rc), :])
        gy[tm:tm + 8, :] = gy[0:8, :]
        for tap in range(SSM_CONV):
            dw_ref[tap:tap + 1, :] += dw[tap]
        dbias_ref[...] += dbias

    return pl.pallas_call(
        body, name="conv_bwd",
        grid=(c // CONV_COLS, nt),
        in_specs=[pl.BlockSpec(memory_space=pl.ANY),
                  pl.BlockSpec((tm, CONV_COLS), lambda j, i: (nt - 1 - i, cb0 + j)),
                  pl.BlockSpec((8, CONV_COLS), lambda j, i: (jnp.maximum((nt - 1 - i) * hb - 1, 0), cb0 + j)),
                  pl.BlockSpec((tm, CONV_COLS), lambda j, i: (nt - 1 - i, jnp.minimum(j, nxs - 1))),
                  pl.BlockSpec((tm, CONV_COLS), lambda j, i: (nt - 1 - i, 0)),
                  pl.BlockSpec((tm, CONV_COLS), lambda j, i: (nt - 1 - i, 0)),
                  pl.BlockSpec((8, CONV_COLS), lambda j, i: (0, j)), pl.BlockSpec((1, CONV_COLS), lambda j, i: (0, j))],
        out_specs=[pl.BlockSpec((tm, CONV_COLS), lambda j, i: (nt - 1 - i, cb0 + j)),
                   pl.BlockSpec((8, CONV_COLS), lambda j, i: (0, j)), pl.BlockSpec((1, CONV_COLS), lambda j, i: (0, j))],
        out_shape=[jax.ShapeDtypeStruct(dzx.shape, F32), jax.ShapeDtypeStruct((8, c), F32),
                   jax.ShapeDtypeStruct((1, c), F32)],
        scratch_shapes=[pltpu.VMEM((tm + 8, CONV_COLS), F32), pltpu.VMEM((tm + 8, CONV_COLS), F32)],
        input_output_aliases={0: 0},
        compiler_params=_params("arbitrary", "arbitrary"),
    )(dzx, zx, zx, dxs, dbm, dcm, cw, cb)


def _ssd_group(xs, bg, cg, dtraw, s0, bias, alog, dsk):
    L = xs.shape[0]
    causal = _iota((L, L), 0) >= _iota((L, L), 1)
    tril = jnp.where(causal, 1.0, 0.0).astype(BF16)
    dt = _softplus(dtraw + bias)
    a = -jnp.exp(alog)
    acum = _xleft(tril, dt * a)
    acum_t = acum.T
    dt_t = dt.T
    cb = _bmm_nt(cg, bg)
    lo = _iota((L, 128), 1) < 64
    lo_row = _iota((1, 128), 1) < 64
    lo_col = _iota((128, 1), 0) < 64
    alast = acum[L - 1:L, :]
    ys, s1s = [], []
    for q in range(4):
        xp = xs[:, q * 128:(q + 1) * 128]
        sp = s0[q * 128:(q + 1) * 128, :]
        yd, ec, wc, el = [], [], [], []
        for j in range(2):
            r = 2 * q + j
            ac = acum[:, r:r + 1]
            decay = jnp.exp(jnp.where(causal, ac - acum_t[r:r + 1, :], NEG))
            yd.append(_bmm(cb * decay * dt_t[r:r + 1, :], xp))
            ec.append(jnp.exp(ac))
            al = alast[:, r:r + 1]
            wc.append(jnp.exp(al - ac) * dt[:, r:r + 1])
            el.append(jnp.exp(al))
        y_off = _bmm_nt(cg, sp) * jnp.where(lo, ec[0], ec[1])
        dsel = jnp.where(lo_row, dsk[:, 2 * q:2 * q + 1], dsk[:, 2 * q + 1:2 * q + 2])
        ys.append(jnp.where(lo, yd[0], yd[1]) + y_off + dsel * xp)
        xw = xp * jnp.where(lo, wc[0], wc[1])
        s1s.append(sp * jnp.where(lo_col, el[0], el[1]) + _bmm_tn(xw, bg))
    return jnp.concatenate(ys, axis=1), jnp.concatenate(s1s, axis=0)


def ssd_fwd(xc, zx, bias, alog, dsk, dt_col0):
    t = xc.shape[0]
    L = _tile(t, SSM_CHUNK)
    nc = t // L
    g = SSM_GROUPS
    dtb = dt_col0 // 512

    def body(xs_ref, b_ref, c_ref, dt_ref, bias_ref, alog_ref, dsk_ref, y_ref, st_ref, state):
        @pl.when(pl.program_id(0) == 0)
        def _():
            state[...] = jnp.zeros_like(state)

        for gi in range(g):
            lane = slice(gi * 128, (gi + 1) * 128)
            wide = slice(gi * 512, (gi + 1) * 512)
            s0 = state[gi]
            st_ref[gi] = s0
            y, s1 = _ssd_group(xs_ref[:, wide], b_ref[:, lane], c_ref[:, lane], dt_ref[:, lane], s0,
                               bias_ref[:, lane], alog_ref[:, lane], dsk_ref[:, lane])
            y_ref[:, wide] = y
            state[gi] = s1

    vec = pl.BlockSpec((1, 512), lambda c: (0, 0))
    return pl.pallas_call(
        body, name="ssd_fwd",
        grid=(nc,),
        in_specs=[pl.BlockSpec((L, 2048), lambda c: (c, 0)), pl.BlockSpec((L, 512), lambda c: (c, 4)),
                  pl.BlockSpec((L, 512), lambda c: (c, 5)), pl.BlockSpec((L, 512), lambda c: (c, dtb)), vec, vec, vec],
        out_specs=[pl.BlockSpec((L, 2048), lambda c: (c, 0)),
                   pl.BlockSpec((None, g, 512, 128), lambda c: (c, 0, 0, 0))],
        out_shape=[jax.ShapeDtypeStruct((t, 2048), F32), jax.ShapeDtypeStruct((nc, g, 512, 128), F32)],
        scratch_shapes=[pltpu.VMEM((g, 512, 128), F32)],
        compiler_params=_params("arbitrary"),
    )(xc, xc, xc, zx, bias, alog, dsk)


def ssd_bwd(dzx, dy, xc, zx, states, bias, alog, dsk, dt_col0):
    t = xc.shape[0]
    L = _tile(t, SSM_CHUNK)
    nc = t // L
    g = SSM_GROUPS
    dtb = dt_col0 // 512

    def body(dzx_ref, dy_ref, xs_ref, b_ref, c_ref, dt_ref, st_ref, bias_ref, alog_ref, dsk_ref,
             ddt_ref, dxs_ref, db_ref, dc_ref, dbias_ref, dalog_ref, ddsk_ref, dstate):
        @pl.when(pl.program_id(0) == 0)
        def _():
            dstate[...] = jnp.zeros_like(dstate)
            dbias_ref[...] = jnp.zeros_like(dbias_ref)
            dalog_ref[...] = jnp.zeros_like(dalog_ref)
            ddsk_ref[...] = jnp.zeros_like(ddsk_ref)

        for gi in range(g):
            lane = slice(gi * 128, (gi + 1) * 128)
            wide = slice(gi * 512, (gi + 1) * 512)
            _, vjp = jax.vjp(_ssd_group, xs_ref[:, wide], b_ref[:, lane], c_ref[:, lane], dt_ref[:, lane], st_ref[gi],
                             bias_ref[:, lane], alog_ref[:, lane], dsk_ref[:, lane])
            dxs, db, dc, ddt, ds0, dbias, dalog, ddsk = vjp((dy_ref[:, wide], dstate[gi]))
            dxs_ref[:, wide] = dxs
            db_ref[:, lane] = db
            dc_ref[:, lane] = dc
            ddt_ref[:, lane] = ddt
            dstate[gi] = ds0
            dbias_ref[:, lane] += dbias
            dalog_ref[:, lane] += dalog
            ddsk_ref[:, lane] += ddsk

    rc = lambda c: nc - 1 - c
    vec = pl.BlockSpec((1, 512), lambda c: (0, 0))
    return pl.pallas_call(
        body, name="ssd_bwd",
        grid=(nc,),
        in_specs=[pl.BlockSpec(memory_space=pl.ANY),
                  pl.BlockSpec((L, 2048), lambda c: (rc(c), 0)), pl.BlockSpec((L, 2048), lambda c: (rc(c), 0)),
                  pl.BlockSpec((L, 512), lambda c: (rc(c), 4)), pl.BlockSpec((L, 512), lambda c: (rc(c), 5)),
                  pl.BlockSpec((L, 512), lambda c: (rc(c), dtb)),
                  pl.BlockSpec((None, g, 512, 128), lambda c: (rc(c), 0, 0, 0)), vec, vec, vec],
        out_specs=[pl.BlockSpec((L, 512), lambda c: (rc(c), dtb)), pl.BlockSpec((L, 2048), lambda c: (rc(c), 0)),
                   pl.BlockSpec((L, 512), lambda c: (rc(c), 0)), pl.BlockSpec((L, 512), lambda c: (rc(c), 0)),
                   vec, vec, vec],
        out_shape=[jax.ShapeDtypeStruct(dzx.shape, F32), jax.ShapeDtypeStruct((t, 2048), F32),
                   jax.ShapeDtypeStruct((t, 512), F32), jax.ShapeDtypeStruct((t, 512), F32),
                   jax.ShapeDtypeStruct((1, 512), F32), jax.ShapeDtypeStruct((1, 512), F32),
                   jax.ShapeDtypeStruct((1, 512), F32)],
        scratch_shapes=[pltpu.VMEM((g, 512, 128), F32)],
        input_output_aliases={0: 0},
        compiler_params=_params("arbitrary"),
    )(dzx, dy, xc, xc, xc, zx, states, bias, alog, dsk)


def _gate_tile(y, z, gn):
    gated = y * _silu(z)
    parts = [_rms(gated[:, k * 512:(k + 1) * 512]) for k in range(SSM_GROUPS)]
    return jnp.concatenate(parts, axis=1) * gn


def gate_out_fwd(h, y, zx, gn, w_out):
    t, d = h.shape
    di = y.shape[1]
    tm = _tile(t, 256)

    def body(h_ref, y_ref, z_ref, gn_ref, w_ref, o_ref):
        yn = _gate_tile(y_ref[...], z_ref[...], gn_ref[...])
        o_ref[...] = h_ref[...] + _dot(_b(yn), w_ref[...])

    return pl.pallas_call(
        body, name="gate_out_fwd",
        grid=(t // tm,),
        in_specs=[pl.BlockSpec((tm, d), lambda i: (i, 0)), pl.BlockSpec((tm, di), lambda i: (i, 0)),
                  pl.BlockSpec((tm, di), lambda i: (i, 0)), pl.BlockSpec((1, di), lambda i: (0, 0)),
                  pl.BlockSpec((di, d), lambda i: (0, 0))],
        out_specs=pl.BlockSpec((tm, d), lambda i: (i, 0)),
        out_shape=jax.ShapeDtypeStruct((t, d), F32),
        compiler_params=_params("arbitrary"),
    )(h, y, zx, gn, w_out)


def gate_out_bwd(dy, y, zx, gn, w_out, n_zx, dep):
    t, d = dy.shape
    di = y.shape[1]
    tm = _tile(t, 256)

    def body(dy_ref, y_ref, z_ref, gn_ref, w_ref, dep_ref, dz_ref, dys_ref, yn_ref, dgn_ref):
        @pl.when(pl.program_id(0) == 0)
        def _():
            dgn_ref[...] = jnp.zeros_like(dgn_ref)

        yn, vjp = jax.vjp(_gate_tile, y_ref[...], z_ref[...], gn_ref[...])
        dyn = _dot_nt(_b(dy_ref[...]), w_ref[...])
        dys, dz, dgn = vjp(dyn)
        yn_ref[...] = _b(yn)
        dys_ref[...] = dys
        dz_ref[...] = dz
        dgn_ref[...] += dgn

    return pl.pallas_call(
        body, name="gate_out_bwd",
        grid=(t // tm,),
        in_specs=[pl.BlockSpec((tm, d), lambda i: (i, 0)), pl.BlockSpec((tm, di), lambda i: (i, 0)),
                  pl.BlockSpec((tm, di), lambda i: (i, 0)), pl.BlockSpec((1, di), lambda i: (0, 0)),
                  pl.BlockSpec((di, d), lambda i: (0, 0)), DEP_SPEC],
        out_specs=[pl.BlockSpec((tm, di), lambda i: (i, 0)), pl.BlockSpec((tm, di), lambda i: (i, 0)),
                   pl.BlockSpec((tm, di), lambda i: (i, 0)), pl.BlockSpec((1, di), lambda i: (0, 0))],
        out_shape=[jax.ShapeDtypeStruct((t, n_zx), F32), jax.ShapeDtypeStruct((t, di), F32),
                   jax.ShapeDtypeStruct((t, di), BF16), jax.ShapeDtypeStruct((1, di), F32)],
        compiler_params=_params("arbitrary"),
    )(dy, y, zx, gn, w_out, dep)


def _attn_block(qp, kvp, kvc, biasm, sinks, qg, kg, w_o, first):
    nq = qp.shape[0]
    n_pairs = qp.shape[1] // 128
    hk = n_pairs
    rows = hk * nq
    seg = functools.partial(_xright, m=_blockdiag64(128))
    scale = ATT_HEAD_DIM ** -0.5
    qi = (_iota((rows, 2 * nq), 0) % nq) + nq
    kj = _iota((rows, 2 * nq), 1)
    dist = qi - kj
    valid = (dist >= 0) & (dist < ATT_WINDOW) & (jnp.logical_not(first) | (kj >= nq))
    lo = _iota((nq, 128), 1) < 64
    kv = jnp.concatenate([kvp, kvc], axis=0)
    outs = [None] * n_pairs
    for kvh in range(2):
        kn = _group64_rms(kv[:, kvh * 128:(kvh + 1) * 128], seg) * kg
        vv = kv[:, 256 + kvh * 128:256 + (kvh + 1) * 128]
        pairs = range(kvh * hk // 2, (kvh + 1) * hk // 2)
        qs, sk = [], []
        for p in pairs:
            qn = _group64_rms(qp[:, p * 128:(p + 1) * 128], seg) * qg
            qs += [jnp.where(lo, qn, 0.0), jnp.where(lo, 0.0, qn)]
            sk += [jnp.broadcast_to(sinks[:, h:h + 1], (nq, 1)) for h in (2 * p, 2 * p + 1)]
        sink = jnp.concatenate(sk, axis=0)
        s = _bmm_nt(jnp.concatenate(qs, axis=0), kn) * scale + biasm[kvh * rows:(kvh + 1) * rows]
        s = jnp.where(valid, s, NEG)
        m = lax.stop_gradient(jnp.maximum(jnp.max(s, axis=-1, keepdims=True), sink))
        pexp = jnp.exp(s - m)
        den = jnp.sum(pexp, axis=-1, keepdims=True) + jnp.exp(sink - m)
        o = _bmm(pexp * (1.0 / den), vv)
        for n, p in enumerate(pairs):
            outs[p] = jnp.where(lo, o[2 * n * nq:(2 * n + 1) * nq], o[(2 * n + 1) * nq:(2 * n + 2) * nq])
    o = jnp.concatenate(outs, axis=1)
    return _bmm(o, w_o), o


def attn_fwd(h, qp, kvd, biasm, sinks, qg, kg, w_o):
    t, d = h.shape
    nq = ATT_WINDOW
    nb = t // nq
    nh = qp.shape[1] // ATT_HEAD_DIM

    def body(h_ref, q_ref, kp_ref, kc_ref, bias_ref, s_ref, qg_ref, kg_ref, w_ref, o_ref):
        out, _ = _attn_block(q_ref[...], kp_ref[...], kc_ref[...], bias_ref[...], s_ref[...], qg_ref[...],
                             kg_ref[...], w_ref[...], pl.program_id(0) == 0)
        o_ref[...] = h_ref[...] + out

    vec = pl.BlockSpec((1, 128), lambda i: (0, 0))
    return pl.pallas_call(
        body, name="attn_fwd",
        grid=(nb,),
        in_specs=[pl.BlockSpec((nq, d), lambda i: (i, 0)), pl.BlockSpec((nq, nh * 64), lambda i: (i, 0)),
                  pl.BlockSpec((nq, 512), lambda i: (jnp.maximum(i - 1, 0), 0)),
                  pl.BlockSpec((nq, 512), lambda i: (i, 0)),
                  pl.BlockSpec((nh * nq, 2 * nq), lambda i: (0, 0)), vec, vec, vec,
                  pl.BlockSpec((nh * 64, d), lambda i: (0, 0))],
        out_specs=pl.BlockSpec((nq, d), lambda i: (i, 0)),
        out_shape=jax.ShapeDtypeStruct((t, d), F32),
        compiler_params=_params("arbitrary"),
    )(h, qp, kvd, kvd, biasm, sinks, qg, kg, w_o)


def attn_bwd(dy, qp, kvd, biasm, sinks, qg, kg, w_o, dep):
    t, d = dy.shape
    nq = ATT_WINDOW
    nb = t // nq
    nh = qp.shape[1] // ATT_HEAD_DIM

    def body(dy_ref, q_ref, kp_ref, kc_ref, bias_ref, s_ref, qg_ref, kg_ref, w_ref, dep_ref,
             dq_ref, dkv_ref, o_ref, dbias_ref, ds_ref, dqg_ref, dkg_ref, carry):
        i = pl.program_id(0)

        @pl.when(i == 0)
        def _():
            carry[...] = jnp.zeros_like(carry)
            dbias_ref[...] = jnp.zeros_like(dbias_ref)
            ds_ref[...] = jnp.zeros_like(ds_ref)
            dqg_ref[...] = jnp.zeros_like(dqg_ref)
            dkg_ref[...] = jnp.zeros_like(dkg_ref)

        @pl.when(i < nb)
        def _():
            fn = functools.partial(_attn_block, w_o=w_ref[...], first=(i == 0))
            (_, o), vjp = jax.vjp(fn, q_ref[...], kp_ref[...], kc_ref[...], bias_ref[...], s_ref[...],
                                  qg_ref[...], kg_ref[...])
            dq, dkp, dkc, dbias, dsk, dqg, dkg = vjp((dy_ref[...], jnp.zeros((nq, nh * 64), F32)))
            dq_ref[...] = dq
            o_ref[...] = _b(o)
            dkv_ref[...] = _fold64(carry[...] + dkp)
            carry[...] = dkc
            dbias_ref[...] += dbias
            ds_ref[...] += dsk
            dqg_ref[...] += _fold64(dqg)
            dkg_ref[...] += _fold64(dkg)

        @pl.when(i == nb)
        def _():
            dkv_ref[...] = _fold64(carry[...])

    cl = lambda i: jnp.minimum(i, nb - 1)
    vec = pl.BlockSpec((1, 128), lambda i: (0, 0))
    return pl.pallas_call(
        body, name="attn_bwd",
        grid=(nb + 1,),
        in_specs=[pl.BlockSpec((nq, d), lambda i: (cl(i), 0)), pl.BlockSpec((nq, nh * 64), lambda i: (cl(i), 0)),
                  pl.BlockSpec((nq, 512), lambda i: (jnp.maximum(cl(i) - 1, 0), 0)),
                  pl.BlockSpec((nq, 512), lambda i: (cl(i), 0)),
                  pl.BlockSpec((nh * nq, 2 * nq), lambda i: (0, 0)), vec, vec, vec,
                  pl.BlockSpec((nh * 64, d), lambda i: (0, 0)), DEP_SPEC],
        out_specs=[pl.BlockSpec((nq, nh * 64), lambda i: (cl(i), 0)),
                   pl.BlockSpec((nq, 512), lambda i: (jnp.maximum(i - 1, 0), 0)),
                   pl.BlockSpec((nq, nh * 64), lambda i: (cl(i), 0)),
                   pl.BlockSpec((nh * nq, 2 * nq), lambda i: (0, 0)), vec, vec, vec],
        out_shape=[jax.ShapeDtypeStruct((t, nh * 64), F32), jax.ShapeDtypeStruct((t, 512), F32),
                   jax.ShapeDtypeStruct((t, nh * 64), BF16), jax.ShapeDtypeStruct((nh * nq, 2 * nq), F32),
                   jax.ShapeDtypeStruct((1, 128), F32), jax.ShapeDtypeStruct((1, 128), F32),
                   jax.ShapeDtypeStruct((1, 128), F32)],
        scratch_shapes=[pltpu.VMEM((nq, 512), F32)],
        compiler_params=_params("arbitrary"),
    )(dy, qp, kvd, kvd, biasm, sinks, qg, kg, w_o, dep)


def _t5_buckets():
    nq = ATT_WINDOW
    dist = (np.arange(nq)[:, None] + nq) - np.arange(2 * nq)[None, :]
    n = np.maximum(dist, 0)
    max_exact = REL_BUCKETS // 2
    nf = np.maximum(n, 1).astype(np.float32)
    large = max_exact + (np.log(nf / max_exact) / math.log(ATT_WINDOW / max_exact)
                         * (REL_BUCKETS - max_exact)).astype(np.int32)
    large = np.minimum(large, REL_BUCKETS - 1)
    return np.where(n < max_exact, n, large).astype(np.int32)


def rel_bias_bwd(dbias, buckets):
    nh = dbias.shape[0]

    def body(db_ref, bk_ref, o_ref):
        bk = bk_ref[...]
        lane = _iota((1, 128), 1)
        row = _iota((REL_BUCKETS, 128), 0)
        acc = jnp.zeros((REL_BUCKETS, 128), F32)
        for h in range(nh):
            dbh = db_ref[h]
            for b in range(REL_BUCKETS):
                v = jnp.sum(jnp.where(bk == b, dbh, 0.0))
                acc = acc + jnp.where((row == b) & (lane == h), v, 0.0)
        o_ref[...] = acc

    return pl.pallas_call(
        body, name="rel_bias_bwd",
        out_shape=jax.ShapeDtypeStruct((REL_BUCKETS, 128), F32),
        compiler_params=_params(),
    )(dbias, buckets)


def loss_head(y, target):
    t, d = y.shape
    tm = _tile(t, 512)

    def body(y_ref, t_ref, l_ref, dy_ref):
        @pl.when(pl.program_id(0) == 0)
        def _():
            l_ref[...] = jnp.zeros_like(l_ref)

        e = y_ref[...] - t_ref[...]
        l_ref[...] += 0.5 * jnp.sum(jnp.mean(e * e, axis=-1, keepdims=True), axis=0, keepdims=True)
        dy_ref[...] = e * (1.0 / d)

    return pl.pallas_call(
        body, name="loss_head",
        grid=(t // tm,),
        in_specs=[pl.BlockSpec((tm, d), lambda i: (i, 0)), pl.BlockSpec((tm, d), lambda i: (i, 0))],
        out_specs=[pl.BlockSpec((1, 1), lambda i: (0, 0)), pl.BlockSpec((tm, d), lambda i: (i, 0))],
        out_shape=[jax.ShapeDtypeStruct((1, 1), F32), jax.ShapeDtypeStruct((t, d), F32)],
        compiler_params=_params("arbitrary"),
    )(y, target)


def adamw(w, g, m, v):
    r, c = w.shape
    tr = r if r <= 512 else _tile(r, 256)

    def body(w_ref, g_ref, m_ref, v_ref, d_ref, nm_ref, nv_ref):
        gg = g_ref[...]
        nm = ADAM_B1 * m_ref[...] + (1.0 - ADAM_B1) * gg
        nv = ADAM_B2 * v_ref[...] + (1.0 - ADAM_B2) * (gg * gg)
        m_hat = nm / (1.0 - ADAM_B1 ** ADAM_STEP)
        v_hat = nv / (1.0 - ADAM_B2 ** ADAM_STEP)
        d_ref[...] = -ADAM_LR * (m_hat / (jnp.sqrt(v_hat) + ADAM_EPS) + ADAM_WD * w_ref[...])
        nm_ref[...] = nm
        nv_ref[...] = nv

    spec = pl.BlockSpec((tr, c), lambda i: (i, 0))
    shp = jax.ShapeDtypeStruct((r, c), F32)
    return pl.pallas_call(
        body, name="adamw",
        grid=(r // tr,),
        in_specs=[spec] * 4, out_specs=[spec] * 3, out_shape=[shp] * 3,
        compiler_params=_params("arbitrary"),
    )(w, g, m, v)


def _my_pos():
    return lax.axis_index("x"), lax.axis_index("y"), lax.axis_index("c")


def _other_chips(x, y):
    return [(1 - x, y), (x, 1 - y), (1 - x, 1 - y)]


def _chip_id(x, y):
    return 2 * x + y


HBM_SPEC = pl.BlockSpec(memory_space=pltpu.HBM)
SEM_SPEC = pl.BlockSpec(memory_space=pltpu.SEMAPHORE)
DATAFLOW = pltpu.SideEffectType.DATAFLOW_SIDE_EFFECTING


def _in_hbm(a):
    return pltpu.with_memory_space_constraint(a, pltpu.HBM)


def _ici_gather_copy(buf, p, j, chip, c, to, send_sems, recv_sems):
    blk = buf.at[_chip_id(*chip), c]
    return pltpu.make_async_remote_copy(
        src_ref=blk, dst_ref=blk, send_sem=send_sems.at[3 * p + j], recv_sem=recv_sems.at[3 * p + j],
        device_id=to, device_id_type=MESH)


def gather_start(bufs, after, tag):
    n = len(bufs)

    def body(*refs):
        ins = refs[:n]
        send_sems, recv_sems = refs[n + 1], refs[n + 2]
        token = refs[2 * n + 3]
        x, y, c = _my_pos()
        for p in range(n):
            for j, chip in enumerate(_other_chips(x, y)):
                _ici_gather_copy(ins[p], p, j, (x, y), c, (*chip, c), send_sems, recv_sems).start()
        token[...] = jnp.zeros_like(token)

    out = pl.pallas_call(
        body, name="gather_start_" + tag,
        in_specs=[HBM_SPEC] * n + [DEP_SPEC],
        out_specs=(SEM_SPEC, SEM_SPEC, *([HBM_SPEC] * n), pl.BlockSpec(memory_space=pltpu.VMEM)),
        out_shape=(pltpu.SemaphoreType.DMA((3 * n,)), pltpu.SemaphoreType.DMA((3 * n,)),
                   *[pltpu.HBM(b.shape, b.dtype) for b in bufs], jax.ShapeDtypeStruct((8, 128), F32)),
        input_output_aliases={p: 2 + p for p in range(n)},
        compiler_params=pltpu.CompilerParams(has_side_effects=DATAFLOW),
    )(*[_in_hbm(b) for b in bufs], after)
    return out[0], out[1], list(out[2:2 + n]), out[2 + n]


def gather_wait(send_sems, recv_sems, bufs, after, tag):
    n = len(bufs)

    def body(*refs):
        ins = refs[:n]
        send_sems, recv_sems = refs[n], refs[n + 1]
        x, y, c = _my_pos()
        for p in range(n):
            for j, chip in enumerate(_other_chips(x, y)):
                _ici_gather_copy(ins[p], p, j, (x, y), c, (*chip, c), send_sems, recv_sems).wait_send()
                _ici_gather_copy(ins[p], p, j, chip, c, (x, y, c), send_sems, recv_sems).wait_recv()

    out = pl.pallas_call(
        body, name="gather_wait_" + tag,
        in_specs=[HBM_SPEC] * n + [SEM_SPEC, SEM_SPEC, pl.BlockSpec(memory_space=pl.ANY)],
        out_specs=[HBM_SPEC] * n,
        out_shape=[pltpu.HBM(b.shape, b.dtype) for b in bufs],
        input_output_aliases={p: p for p in range(n)},
        compiler_params=pltpu.CompilerParams(has_side_effects=DATAFLOW),
    )(*bufs, send_sems, recv_sems, after)
    return list(out)


def forward_to_sibling(bufs):
    n = len(bufs)

    def body(*refs):
        outs = refs[n:2 * n]
        send_sems, recv_sems = refs[2 * n:]
        x, y, c = _my_pos()
        chips = _other_chips(x, y)
        sent = []
        for p in range(n):
            for j, chip in enumerate(chips):
                cp = _ici_gather_copy(outs[p], p, j, chip, c, (x, y, 1 - c), send_sems, recv_sems)
                cp.start()
                sent.append(cp)
        for p in range(n):
            for j, chip in enumerate(chips):
                _ici_gather_copy(outs[p], p, j, chip, 1 - c, (x, y, c), send_sems, recv_sems).wait_recv()
        for cp in sent:
            cp.wait_send()

    any_spec = pl.BlockSpec(memory_space=pl.ANY)
    return pl.pallas_call(
        body, name="forward_to_sibling",
        in_specs=[any_spec] * n, out_specs=[any_spec] * n,
        out_shape=[jax.ShapeDtypeStruct(b.shape, b.dtype) for b in bufs],
        scratch_shapes=[pltpu.SemaphoreType.DMA((3 * n,)), pltpu.SemaphoreType.DMA((3 * n,))],
        input_output_aliases={p: p for p in range(n)},
    )(*bufs)


def allreduce_small(v):
    r, c = v.shape

    def body(v_ref, o_ref, buf, send_sems, recv_sems):
        x, y, cc = _my_pos()
        me = 4 * x + 2 * y + cc
        buf[me] = v_ref[...]
        copies = []
        for k in range(1, 8):
            dx, dy, dc = (k >> 2) & 1, (k >> 1) & 1, k & 1
            peer = (x ^ dx, y ^ dy, cc ^ dc)
            cp = pltpu.make_async_remote_copy(
                src_ref=v_ref, dst_ref=buf.at[me], send_sem=send_sems.at[k - 1], recv_sem=recv_sems.at[k - 1],
                device_id=peer, device_id_type=MESH)
            cp.start()
            copies.append(cp)
        for cp in copies:
            cp.wait_recv()
        for cp in copies:
            cp.wait_send()
        acc = buf[0]
        for k in range(1, 8):
            acc = acc + buf[k]
        o_ref[...] = acc

    vm = pl.BlockSpec(memory_space=pltpu.VMEM)
    return pl.pallas_call(
        body, name="allreduce_small",
        in_specs=[vm], out_specs=vm,
        out_shape=jax.ShapeDtypeStruct((r, c), F32),
        scratch_shapes=[pltpu.VMEM((8, r, c), F32), pltpu.SemaphoreType.DMA((7,)), pltpu.SemaphoreType.DMA((7,))],
    )(v)


def _sibling_half_copy(grad, land, p, c, sibling, send_sems, recv_sems):
    return pltpu.make_async_remote_copy(
        src_ref=grad.at[:, 1 - c], dst_ref=land, send_sem=send_sems.at[p], recv_sem=recv_sems.at[p],
        device_id=sibling, device_id_type=MESH)


def sibling_halves_start(grads, tag):
    n = len(grads)
    lands = [lax.empty((g.shape[0],) + g.shape[2:], g.dtype) for g in grads]

    def body(*refs):
        ins, land = refs[:n], refs[n:2 * n]
        send_sems, recv_sems = refs[2 * n], refs[2 * n + 1]
        token = refs[4 * n + 2]
        x, y, c = _my_pos()
        for p in range(n):
            _sibling_half_copy(ins[p], land[p], p, c, (x, y, 1 - c), send_sems, recv_sems).start()
        token[...] = jnp.zeros_like(token)

    out = pl.pallas_call(
        body, name="sibling_halves_start_" + tag,
        in_specs=[HBM_SPEC] * (2 * n),
        out_specs=(SEM_SPEC, SEM_SPEC, *([HBM_SPEC] * (2 * n)), pl.BlockSpec(memory_space=pltpu.VMEM)),
        out_shape=(pltpu.SemaphoreType.DMA((n,)), pltpu.SemaphoreType.DMA((n,)),
                   *[pltpu.HBM(a.shape, a.dtype) for a in grads + lands], jax.ShapeDtypeStruct((8, 128), F32)),
        input_output_aliases={i: 2 + i for i in range(2 * n)},
        compiler_params=pltpu.CompilerParams(has_side_effects=DATAFLOW),
    )(*[_in_hbm(a) for a in grads + lands])
    return out[0], out[1], list(out[2:2 + n]), list(out[2 + n:2 + 2 * n]), out[2 + 2 * n]


def sibling_halves_wait(send_sems, recv_sems, grads, lands, after, tag):
    n = len(grads)

    def body(*refs):
        ins, land = refs[:n], refs[n:2 * n]
        send_sems, recv_sems = refs[2 * n], refs[2 * n + 1]
        x, y, c = _my_pos()
        for p in range(n):
            cp = _sibling_half_copy(ins[p], land[p], p, c, (x, y, 1 - c), send_sems, recv_sems)
            cp.wait_send()
            cp.wait_recv()

    out = pl.pallas_call(
        body, name="sibling_halves_wait_" + tag,
        in_specs=[HBM_SPEC] * (2 * n) + [SEM_SPEC, SEM_SPEC, pl.BlockSpec(memory_space=pl.ANY)],
        out_specs=[HBM_SPEC] * (2 * n),
        out_shape=[pltpu.HBM(a.shape, a.dtype) for a in grads + lands],
        input_output_aliases={i: i for i in range(2 * n)},
        compiler_params=pltpu.CompilerParams(has_side_effects=DATAFLOW),
    )(*grads, *lands, send_sems, recv_sems, after)
    return list(out[:n]), list(out[n:])


def _chip_partial_copy(part, land, p, j, chip, c, send_sems, recv_sems):
    return pltpu.make_async_remote_copy(
        src_ref=part.at[_chip_id(*chip)], dst_ref=land.at[j], send_sem=send_sems.at[3 * p + j],
        recv_sem=recv_sems.at[3 * p + j], device_id=(*chip, c), device_id_type=MESH)


def chip_partials_start(parts, tag):
    n = len(parts)
    lands = [lax.empty((N_CHIPS - 1,) + s.shape[1:], s.dtype) for s in parts]

    def body(*refs):
        ins, land = refs[:n], refs[n:2 * n]
        send_sems, recv_sems = refs[2 * n], refs[2 * n + 1]
        token = refs[4 * n + 2]
        x, y, c = _my_pos()
        for p in range(n):
            for j, chip in enumerate(_other_chips(x, y)):
                _chip_partial_copy(ins[p], land[p], p, j, chip, c, send_sems, recv_sems).start()
        token[...] = jnp.zeros_like(token)

    out = pl.pallas_call(
        body, name="chip_partials_start_" + tag,
        in_specs=[HBM_SPEC] * (2 * n),
        out_specs=(SEM_SPEC, SEM_SPEC, *([HBM_SPEC] * (2 * n)), pl.BlockSpec(memory_space=pltpu.VMEM)),
        out_shape=(pltpu.SemaphoreType.DMA((3 * n,)), pltpu.SemaphoreType.DMA((3 * n,)),
                   *[pltpu.HBM(a.shape, a.dtype) for a in parts + lands], jax.ShapeDtypeStruct((8, 128), F32)),
        input_output_aliases={i: 2 + i for i in range(2 * n)},
        compiler_params=pltpu.CompilerParams(has_side_effects=DATAFLOW),
    )(*[_in_hbm(a) for a in parts + lands])
    return out[0], out[1], list(out[2:2 + n]), list(out[2 + n:2 + 2 * n]), out[2 + 2 * n]


def chip_partials_wait(send_sems, recv_sems, parts, lands, after, tag):
    n = len(parts)

    def body(*refs):
        ins, land = refs[:n], refs[n:2 * n]
        send_sems, recv_sems = refs[2 * n], refs[2 * n + 1]
        x, y, c = _my_pos()
        for p in range(n):
            for j, chip in enumerate(_other_chips(x, y)):
                cp = _chip_partial_copy(ins[p], land[p], p, j, chip, c, send_sems, recv_sems)
                cp.wait_send()
                cp.wait_recv()

    out = pl.pallas_call(
        body, name="chip_partials_wait_" + tag,
        in_specs=[HBM_SPEC] * (2 * n) + [SEM_SPEC, SEM_SPEC, pl.BlockSpec(memory_space=pl.ANY)],
        out_specs=[HBM_SPEC] * (2 * n),
        out_shape=[pltpu.HBM(a.shape, a.dtype) for a in parts + lands],
        input_output_aliases={i: i for i in range(2 * n)},
        compiler_params=pltpu.CompilerParams(has_side_effects=DATAFLOW),
    )(*parts, *lands, send_sems, recv_sems, after)
    return list(out[n:])


def share_with_sibling(bufs):
    n = len(bufs)

    def body(*refs):
        outs = refs[n:2 * n]
        send_sems, recv_sems = refs[2 * n:]
        x, y, c = _my_pos()
        copies = []
        for p in range(n):
            cp = pltpu.make_async_remote_copy(
                src_ref=outs[p].at[c], dst_ref=outs[p].at[c], send_sem=send_sems.at[p], recv_sem=recv_sems.at[p],
                device_id=(x, y, 1 - c), device_id_type=MESH)
            cp.start()
            copies.append(cp)
        for p in range(n):
            pltpu.make_async_remote_copy(
                src_ref=outs[p].at[1 - c], dst_ref=outs[p].at[1 - c], send_sem=send_sems.at[p],
                recv_sem=recv_sems.at[p], device_id=(x, y, 1 - c), device_id_type=MESH).wait_recv()
        for cp in copies:
            cp.wait_send()

    any_spec = pl.BlockSpec(memory_space=pl.ANY)
    return pl.pallas_call(
        body, name="share_with_sibling",
        in_specs=[any_spec] * n, out_specs=[any_spec] * n,
        out_shape=[jax.ShapeDtypeStruct(b.shape, b.dtype) for b in bufs],
        scratch_shapes=[pltpu.SemaphoreType.DMA((n,)), pltpu.SemaphoreType.DMA((n,))],
        input_output_aliases={p: p for p in range(n)},
    )(*bufs)


def add_sibling(g, recv, half):
    _, _, r, c = g.shape
    tr = _tile(r, 256) if r % 256 == 0 else r

    def body(half_ref, g_ref, r_ref, o32_ref, o16_ref):
        s = g_ref[...] + r_ref[...]
        o32_ref[...] = s
        o16_ref[...] = _b(s)

    return pl.pallas_call(
        body, name="add_sibling",
        grid_spec=pltpu.PrefetchScalarGridSpec(
            num_scalar_prefetch=1, grid=(N_CHIPS, r // tr),
            in_specs=[pl.BlockSpec((None, None, tr, c), lambda k, i, hf: (k, hf[0], i, 0)),
                      pl.BlockSpec((None, tr, c), lambda k, i, hf: (k, i, 0))],
            out_specs=[pl.BlockSpec((None, tr, c), lambda k, i, hf: (k, i, 0)),
                       pl.BlockSpec((None, tr, c), lambda k, i, hf: (k, i, 0))]),
        out_shape=[jax.ShapeDtypeStruct((N_CHIPS, r, c), F32), jax.ShapeDtypeStruct((N_CHIPS, r, c), BF16)],
        compiler_params=_params("arbitrary", "arbitrary"),
    )(half, g, recv)


def add_chip_partials(p32, recv, pos):
    _, r, c = p32.shape
    tr = _tile(r, 256) if r % 256 == 0 else r

    def body(pos_ref, p_ref, r_ref, o_ref):
        acc = p_ref[...]
        for j in range(N_CHIPS - 1):
            acc = acc + r_ref[j].astype(F32)
        o_ref[...] = acc

    return pl.pallas_call(
        body, name="add_chip_partials",
        grid_spec=pltpu.PrefetchScalarGridSpec(
            num_scalar_prefetch=1, grid=(r // tr,),
            in_specs=[pl.BlockSpec((None, tr, c), lambda i, ps: (ps[0], i, 0)),
                      pl.BlockSpec((N_CHIPS - 1, tr, c), lambda i, ps: (0, i, 0))],
            out_specs=pl.BlockSpec((None, tr, c), lambda i, ps: (ps[1], i, 0))),
        out_shape=jax.ShapeDtypeStruct((2, r, c), F32),
        compiler_params=_params("arbitrary"),
    )(pos, p32, recv)


def cast_into_gather(w, pos, dep, row0=0, nrows=None):
    c = w.shape[1]
    nrows = w.shape[0] if nrows is None else nrows
    r = nrows // 2
    common = math.gcd(r, row0) if row0 else r
    tr = max(w for w in range(16, min(common, 512) + 1, 16) if common % w == 0)
    nt = r // tr

    def body(pos_ref, w_ref, dep_ref, o_ref):
        o_ref[...] = _b(w_ref[...])

    return pl.pallas_call(
        body, name="cast_into_gather",
        grid_spec=pltpu.PrefetchScalarGridSpec(
            num_scalar_prefetch=1, grid=(2, nt),
            in_specs=[pl.BlockSpec((tr, c), lambda hf, i, ps: (row0 // tr + hf * nt + i, 0)), DEP_SPEC],
            out_specs=pl.BlockSpec((None, None, tr, c), lambda hf, i, ps: (ps[0], hf, i, 0))),
        out_shape=jax.ShapeDtypeStruct((N_CHIPS, 2, r, c), BF16),
        compiler_params=_params("arbitrary", "arbitrary"),
    )(pos, w, dep)


def build_bias(rel, buckets):
    nb, nh = rel.shape

    def body(rel_ref, bk_ref, o_ref):
        bk = bk_ref[...]
        for h in range(nh):
            acc = jnp.zeros(bk.shape, F32)
            for b in range(nb):
                acc = jnp.where(bk == b, rel_ref[b, h], acc)
            o_ref[h] = acc

    return pl.pallas_call(
        body, name="build_bias",
        in_specs=[pl.BlockSpec(memory_space=pltpu.SMEM), pl.BlockSpec(memory_space=pltpu.VMEM)],
        out_specs=pl.BlockSpec(memory_space=pltpu.VMEM),
        out_shape=jax.ShapeDtypeStruct((nh,) + buckets.shape, F32),
        compiler_params=_params(),
    )(rel, buckets)


SMALL_ROWS = 256


def kernel(x, ffn_norm, ffn_w1, ffn_w3, ffn_w2, ssm_norm, ssm_w_in, ssm_conv_w, ssm_conv_b, ssm_dt_bias, ssm_a_log, ssm_d, ssm_gate_norm, ssm_w_out, kv_norm, w_kv, k_norm, attn_norm, w_q, q_norm, sinks, w_o, rel_bias, loss_target, m_ffn_norm, m_ffn_w1, m_ffn_w3, m_ffn_w2, m_ssm_norm, m_ssm_w_in, m_ssm_conv_w, m_ssm_conv_b, m_ssm_dt_bias, m_ssm_a_log, m_ssm_d, m_ssm_gate_norm, m_ssm_w_out, m_kv_norm, m_w_kv, m_k_norm, m_attn_norm, m_w_q, m_q_norm, m_sinks, m_w_o, m_rel_bias, v_ffn_norm, v_ffn_w1, v_ffn_w3, v_ffn_w2, v_ssm_norm, v_ssm_w_in, v_ssm_conv_w, v_ssm_conv_b, v_ssm_dt_bias, v_ssm_a_log, v_ssm_d, v_ssm_gate_norm, v_ssm_w_out, v_kv_norm, v_w_kv, v_k_norm, v_attn_norm, v_w_q, v_q_norm, v_sinks, v_w_o, v_rel_bias):
    weights = dict(ffn_norm=ffn_norm, ffn_w1=ffn_w1, ffn_w3=ffn_w3, ffn_w2=ffn_w2, ssm_norm=ssm_norm,
                   ssm_w_in=ssm_w_in, ssm_conv_w=ssm_conv_w, ssm_conv_b=ssm_conv_b, ssm_dt_bias=ssm_dt_bias,
                   ssm_a_log=ssm_a_log, ssm_d=ssm_d, ssm_gate_norm=ssm_gate_norm, ssm_w_out=ssm_w_out,
                   kv_norm=kv_norm, w_kv=w_kv, k_norm=k_norm, attn_norm=attn_norm, w_q=w_q, q_norm=q_norm,
                   sinks=sinks, w_o=w_o, rel_bias=rel_bias)
    m_in = dict(ffn_norm=m_ffn_norm, ffn_w1=m_ffn_w1, ffn_w3=m_ffn_w3, ffn_w2=m_ffn_w2, ssm_norm=m_ssm_norm,
                ssm_w_in=m_ssm_w_in, ssm_conv_w=m_ssm_conv_w, ssm_conv_b=m_ssm_conv_b, ssm_dt_bias=m_ssm_dt_bias,
                ssm_a_log=m_ssm_a_log, ssm_d=m_ssm_d, ssm_gate_norm=m_ssm_gate_norm, ssm_w_out=m_ssm_w_out,
                kv_norm=m_kv_norm, w_kv=m_w_kv, k_norm=m_k_norm, attn_norm=m_attn_norm, w_q=m_w_q, q_norm=m_q_norm,
                sinks=m_sinks, w_o=m_w_o, rel_bias=m_rel_bias)
    v_in = dict(ffn_norm=v_ffn_norm, ffn_w1=v_ffn_w1, ffn_w3=v_ffn_w3, ffn_w2=v_ffn_w2, ssm_norm=v_ssm_norm,
                ssm_w_in=v_ssm_w_in, ssm_conv_w=v_ssm_conv_w, ssm_conv_b=v_ssm_conv_b, ssm_dt_bias=v_ssm_dt_bias,
                ssm_a_log=v_ssm_a_log, ssm_d=v_ssm_d, ssm_gate_norm=v_ssm_gate_norm, ssm_w_out=v_ssm_w_out,
                kv_norm=v_kv_norm, w_kv=v_w_kv, k_norm=v_k_norm, attn_norm=v_attn_norm, w_q=v_w_q, q_norm=v_q_norm,
                sinks=v_sinks, w_o=v_w_o, rel_bias=v_rel_bias)
    return _step(x[0], loss_target[0], weights, m_in, v_in)


BIG = ("ffn_w1", "ffn_w3", "ffn_w2", "ssm_w_in", "ssm_w_out", "w_kv", "w_q", "w_o")
SMALL = (("ffn_norm", True), ("ssm_norm", True), ("ssm_conv_w", True), ("ssm_conv_b", True),
         ("ssm_gate_norm", True), ("ssm_dt_bias", False), ("ssm_a_log", False), ("ssm_d", False),
         ("kv_norm", False), ("k_norm", False), ("attn_norm", False), ("q_norm", False), ("sinks", False),
         ("rel_bias", False))


FFN_W = BIG[:3]


def _small_layout(weights):
    off, table = 0, {}
    for name, sharded in SMALL:
        shape = weights[name].shape
        full = shape[:-1] + (shape[-1] * N_CHIPS,) if sharded else shape
        n = int(np.prod(full))
        table[name] = (off, full, sharded)
        off += n
    assert off <= SMALL_ROWS * 128
    return table


def _place_small(values, table, chip, scale_mask):
    flat = jnp.zeros((SMALL_ROWS * 128,), F32)
    for name, (off, full, sharded) in table.items():
        if not sharded:
            continue
        v = values[name].astype(F32)
        lead = int(np.prod(full[:-1]))
        w = v.shape[-1]
        blk = jnp.zeros((lead, full[-1]), F32)
        blk = lax.dynamic_update_slice(blk, v.reshape(lead, w) * scale_mask, (0, chip * w))
        flat = lax.dynamic_update_slice(flat, blk.reshape(-1), (off,))
    return flat.reshape(SMALL_ROWS, 128)


def _take_small(mat, table, name):
    off, full, _ = table[name]
    n = int(np.prod(full))
    return mat.reshape(-1)[off:off + n].reshape(full)


def _step(x, target, weights, m_in, v_in):
    t, d = x.shape
    xi, yi, ci = lax.axis_index("x"), lax.axis_index("y"), lax.axis_index("c")
    chip = 2 * xi + yi
    pos_arr = jnp.stack([chip, ci]).astype(jnp.int32)
    half_arr = jnp.reshape(ci, (1,)).astype(jnp.int32)

    fs = weights["ffn_w1"].shape[-1]
    ffn_rows = {"ffn_w1": d, "ffn_w3": d, "ffn_w2": fs}
    w2d = {n: weights[n].reshape(-1, weights[n].shape[-1]) for n in BIG}
    mamba_w = ("ssm_w_in", "ssm_w_out")
    late_w = ("w_kv", "w_q", "w_o")
    fs_, fr_, fbufs, tok_f = gather_start(
        [cast_into_gather(w2d[n], pos_arr, pos_arr, 0, ffn_rows[n]) for n in FFN_W], pos_arr, "first")
    ms, mr, mbufs, tok_m = gather_start([cast_into_gather(w2d[n], pos_arr, tok_f) for n in mamba_w], tok_f, "mamba")
    ls, lr, lbufs, tok_l = gather_start(
        [cast_into_gather(w2d[n], pos_arr, tok_f, ffn_rows[n], 3 * ffn_rows[n]) for n in FFN_W]
        + [cast_into_gather(w2d[n], pos_arr, tok_f) for n in late_w], tok_m, "late")
    first = forward_to_sibling(gather_wait(fs_, fr_, fbufs, tok_l, "first"))
    no_dep = jnp.zeros((8, 128), F32)
    table = _small_layout(weights)
    south = (ci == 0).astype(F32)
    small = allreduce_small(_place_small(weights, table, chip, south))
    sp = {n: _take_small(small, table, n) if sh else weights[n] for n, sh in SMALL}

    ffn_first = [first[0].reshape(N_CHIPS, 1, d, fs), first[1].reshape(N_CHIPS, 1, d, fs),
                 first[2].reshape(N_CHIPS, 1, fs, d)]
    ffn_g = sp["ffn_norm"]
    h0 = x
    h1, a00, b00 = ffn_fwd(h0, ffn_g[0, 0].reshape(1, d), *ffn_first, 0, no_dep)
    gathered = dict(zip(mamba_w, forward_to_sibling(gather_wait(ms, mr, mbufs, h1, "mamba"))))
    n_in = weights["ssm_w_in"].shape[-1] * N_CHIPS
    di = weights["ssm_w_out"].shape[1] * N_CHIPS
    nheads = di // SSM_HEAD_DIM
    conv_dim = n_in - di - nheads
    w_in_full = jnp.moveaxis(gathered["ssm_w_in"].reshape(N_CHIPS, d, n_in // N_CHIPS), 0, 1).reshape(d, n_in)
    hpg = nheads // SSM_GROUPS

    def spread_heads(v):
        lead = v.shape[:-1]
        v = v.reshape(lead + (SSM_GROUPS, hpg))
        v = jnp.pad(v, [(0, 0)] * len(lead) + [(0, 0), (0, 128 - hpg)])
        return v.reshape(lead + (SSM_GROUPS * 128,))

    def gather_heads(v):
        lead = v.shape[:-1]
        return v.reshape(lead + (SSM_GROUPS, 128))[..., :hpg].reshape(lead + (nheads,))

    dt_col0 = di + conv_dim
    n_zx = dt_col0 + SSM_GROUPS * 128
    w_in = jnp.concatenate([w_in_full[:, :dt_col0], spread_heads(w_in_full[:, dt_col0:])], axis=1)
    w_out = gathered["ssm_w_out"].reshape(di, d)
    nkv = weights["w_kv"].shape[1] // (2 * ATT_HEAD_DIM)
    assert nkv == 2
    nh = weights["w_q"].shape[-1] // ATT_HEAD_DIM

    ssm_g = sp["ssm_norm"].reshape(1, d)
    cw = jnp.pad(sp["ssm_conv_w"].reshape(SSM_CONV, conv_dim), [(0, 8 - SSM_CONV), (0, 0)])
    cb = sp["ssm_conv_b"].reshape(1, conv_dim)
    gate_g = sp["ssm_gate_norm"].reshape(1, di)
    dt_bias = spread_heads(sp["ssm_dt_bias"].reshape(1, nheads))
    a_log = spread_heads(sp["ssm_a_log"].reshape(1, nheads))
    d_skip = spread_heads(sp["ssm_d"].reshape(1, nheads))
    kv_g = sp["kv_norm"].reshape(1, d)
    k_g = jnp.tile(sp["k_norm"].reshape(1, ATT_HEAD_DIM), (1, 2))
    attn_g = sp["attn_norm"].reshape(1, d)
    q_g = jnp.tile(sp["q_norm"].reshape(1, ATT_HEAD_DIM), (1, 2))
    sink_row = jnp.pad(sp["sinks"].reshape(1, nh), [(0, 0), (0, 128 - nh)])
    buckets = jnp.asarray(_t5_buckets())
    biasm = build_bias(sp["rel_bias"], buckets).reshape(nh * ATT_WINDOW, 2 * ATT_WINDOW)

    zx = norm_mm(h1, ssm_g, w_in)
    xc = conv_fwd(zx, cw, cb, di)
    y_ssd, states = ssd_fwd(xc, zx, dt_bias, a_log, d_skip, dt_col0)
    h2 = gate_out_fwd(h1, y_ssd, zx, gate_g, w_out)

    late = forward_to_sibling(gather_wait(ls, lr, lbufs, h2, "late"))
    ffn_rest = [late[0].reshape(N_CHIPS, 3, d, fs), late[1].reshape(N_CHIPS, 3, d, fs),
                late[2].reshape(N_CHIPS, 3, fs, d)]
    gathered.update(zip(late_w, late[3:]))
    wkv_heads = gathered["w_kv"].reshape(d, 2 * nkv, 1, ATT_HEAD_DIM)
    w_kvd = jnp.broadcast_to(wkv_heads, (d, 2 * nkv, 2, ATT_HEAD_DIM)).reshape(d, 4 * nkv * ATT_HEAD_DIM)
    wq = gathered["w_q"].reshape(d, -1)
    wo = gathered["w_o"].reshape(-1, d)

    def ffn_w(layer, idx):
        blk = 2 * layer + idx
        return (*ffn_first, 0) if blk == 0 else (*ffn_rest, blk - 1)

    h3, a01, b01 = ffn_fwd(h2, ffn_g[0, 1].reshape(1, d), *ffn_w(0, 1), no_dep)
    kvd = norm_mm(h3, kv_g, w_kvd)
    h4, a10, b10 = ffn_fwd(h3, ffn_g[1, 0].reshape(1, d), *ffn_w(1, 0), no_dep)
    qp = norm_mm(h4, attn_g, wq)
    h5 = attn_fwd(h4, qp, kvd, biasm, sink_row, q_g, k_g, wo)
    h6, a11, b11 = ffn_fwd(h5, ffn_g[1, 1].reshape(1, d), *ffn_w(1, 1), no_dep)
    loss_part, d6 = loss_head(h6, target)
    loss = lax.psum(loss_part[0, 0], ("x", "y", "c"))

    gfn = [[None, None], [None, None]]

    pending = {}

    def swap_start(pieces, tag):
        views = [g.reshape(N_CHIPS, 2, g.shape[1] // 2, g.shape[2]) for _, g in pieces]
        ss, rs, views, lands, token = sibling_halves_start(views, tag)
        pending[tag] = dict(keys=[k for k, _ in pieces], swap=(ss, rs, views, lands))
        return token

    def partials_start(tag, after):
        views, recv1 = sibling_halves_wait(*pending[tag]["swap"], after, tag)
        p32, p16 = zip(*[add_sibling(g, r, half_arr) for g, r in zip(views, recv1)])
        ss, rs, parts, lands, token = chip_partials_start(list(p16), tag)
        pending[tag].update(p32=p32, partials=(ss, rs, parts, lands))
        return token

    def ffn_back(h_in, dy, a_s, b_s, layer, idx, dep, wdep):
        dh, u, da, db, s, dg = ffn_bwd(h_in, dy, ffn_g[layer, idx].reshape(1, d), a_s, b_s, *ffn_w(layer, idx), dep)
        gfn[layer][idx] = dg
        return dh, [(("ffn_w1", layer, idx), wgrad_grouped_b(u, da, wdep)),
                    (("ffn_w3", layer, idx), wgrad_grouped_b(u, db, no_dep)),
                    (("ffn_w2", layer, idx), wgrad_grouped_a(s, dy, 0.5))]

    d5, pieces = ffn_back(h5, d6, a11, b11, 1, 1, no_dep, no_dep)
    tok = swap_start(pieces, "ffn11")
    dqp, dkvd, o16, dbiasm, dsinks, dqg, dkg = attn_bwd(d5, qp, kvd, biasm, sink_row, q_g, k_g, wo, tok)
    tok = partials_start("ffn11", dqp)
    g_wo = wgrad(o16, d5)
    d4, u_q, g_attn_norm = norm_mm_bwd(h4, attn_g, wq, dqp, d5, tok)
    g_wq = wgrad(u_q, dqp)
    d3a, pieces = ffn_back(h3, d4, a10, b10, 1, 0, no_dep, no_dep)
    pieces += [(("w_o",), g_wo.reshape(N_CHIPS, -1, d)), (("w_q",), g_wq.reshape(N_CHIPS, d // N_CHIPS, -1))]
    tok = swap_start(pieces, "ffn10")
    d3, u_kv, g_kv_norm = norm_mm_bwd(h3, kv_g, w_kvd, dkvd, d3a, tok, 0.5)
    tok = partials_start("ffn10", d3)
    g_wkvd = wgrad(u_kv, dkvd)
    g_wkv = g_wkvd.reshape(d, 2 * nkv, 2, ATT_HEAD_DIM)[:, :, 0, :].reshape(d, 2 * nkv * ATT_HEAD_DIM)
    d2, pieces = ffn_back(h2, d3, a01, b01, 0, 1, tok, no_dep)
    pieces += [(("w_kv",), g_wkv.reshape(N_CHIPS, d // N_CHIPS, -1))]
    tok = swap_start(pieces, "ffn01")
    dzx, dy_ssd, yn16, g_gate = gate_out_bwd(d2, y_ssd, zx, gate_g, w_out, n_zx, tok)
    tok = partials_start("ffn01", dy_ssd)
    g_wout = wgrad(yn16, d2)
    dzx, dxs, dbm, dcm, g_dtb, g_alog, g_dsk = ssd_bwd(dzx, dy_ssd, xc, zx, states, dt_bias, a_log, d_skip, dt_col0)
    dzx, g_cw, g_cb = conv_bwd(dzx, zx, dxs, dbm, dcm, cw, cb, di)
    d1, u_in, g_ssm_norm = norm_mm_bwd(h1, ssm_g, w_in, dzx, d2, tok)
    g_win = wgrad(u_in, dzx)
    g_win_full = jnp.concatenate([g_win[:, :dt_col0], gather_heads(g_win[:, dt_col0:])], axis=1)
    pieces = [(("ssm_w_in",), jnp.moveaxis(g_win_full.reshape(d, N_CHIPS, n_in // N_CHIPS), 1, 0)),
              (("ssm_w_out",), g_wout.reshape(N_CHIPS, di // N_CHIPS, d))]
    tok = swap_start(pieces, "mamba")
    grad_x, u0, da0, db0, s0, gfn[0][0] = ffn_bwd(h0, d1, ffn_g[0, 0].reshape(1, d), a00, b00, *ffn_w(0, 0), tok)
    tok = partials_start("mamba", grad_x)
    pieces = [(("ffn_w1", 0, 0), wgrad_grouped_b(u0, da0, tok)), (("ffn_w3", 0, 0), wgrad_grouped_b(u0, db0, no_dep)),
              (("ffn_w2", 0, 0), wgrad_grouped_a(s0, d1, 0.5))]
    tok = swap_start(pieces, "ffn00")
    tok = partials_start("ffn00", tok)
    g_relb = rel_bias_bwd(dbiasm.reshape(nh, ATT_WINDOW, 2 * ATT_WINDOW), buckets)

    reduced = {}
    for tag, st in pending.items():
        lands = chip_partials_wait(*st["partials"], tok, tag)
        for k, p, r in zip(st["keys"], st["p32"], lands):
            reduced[k] = add_chip_partials(p, r, pos_arr)
    keys = list(reduced)
    shared = dict(zip(keys, share_with_sibling([reduced[k] for k in keys])))
    grads = {}
    for n in FFN_W:
        blocks = [shared[(n, l, i)].reshape(1, ffn_rows[n], -1) for l in range(2) for i in range(2)]
        grads[n] = jnp.concatenate(blocks, axis=0).reshape(weights[n].shape)
    for n in BIG[3:]:
        grads[n] = shared[(n,)].reshape(weights[n].shape)

    small_grads = {
        "ffn_norm": jnp.stack([jnp.stack([gfn[l][i].reshape(d) for i in range(2)]) for l in range(2)]),
        "ssm_norm": g_ssm_norm.reshape(1, d),
        "ssm_conv_w": g_cw[:SSM_CONV].reshape(1, SSM_CONV, conv_dim),
        "ssm_conv_b": g_cb.reshape(1, conv_dim),
        "ssm_gate_norm": g_gate.reshape(1, di),
        "ssm_dt_bias": gather_heads(g_dtb.reshape(1, -1)), "ssm_a_log": gather_heads(g_alog.reshape(1, -1)),
        "ssm_d": gather_heads(g_dsk.reshape(1, -1)),
        "kv_norm": g_kv_norm.reshape(d), "k_norm": dkg[0, :ATT_HEAD_DIM], "attn_norm": g_attn_norm.reshape(1, d),
        "q_norm": dqg[:, :ATT_HEAD_DIM], "sinks": dsinks[:, :nh], "rel_bias": g_relb[:, :nh],
    }
    flat = jnp.zeros((SMALL_ROWS * 128,), F32)
    for name, (off, fshape, _) in table.items():
        flat = lax.dynamic_update_slice(flat, small_grads[name].astype(F32).reshape(-1), (off,))
    small_sum = allreduce_small(flat.reshape(SMALL_ROWS, 128))
    for name, (off, fshape, sharded) in table.items():
        g = _take_small(small_sum, table, name)
        if sharded:
            w = weights[name].shape[-1]
            lead = int(np.prod(fshape[:-1]))
            g = lax.dynamic_slice(g.reshape(lead, fshape[-1]), (0, chip * w), (lead, w)).reshape(weights[name].shape)
        grads[name] = g.reshape(weights[name].shape)

    names = list(weights)
    deltas, new_m, new_v = {}, {}, {}
    small_names = [n for n, _ in SMALL]
    for n in BIG:
        shp = weights[n].shape
        v2 = lambda a: a.reshape(-1, shp[-1])
        dl, nm, nv = adamw(v2(weights[n]), v2(grads[n]), v2(m_in[n]), v2(v_in[n]))
        deltas[n], new_m[n], new_v[n] = dl.reshape(shp), nm.reshape(shp), nv.reshape(shp)
    sizes = [int(np.prod(weights[n].shape)) for n in small_names]
    tot = sum(sizes)
    rows = -(-tot // 128)
    rows = -(-rows // 8) * 8

    def pack(dct):
        flat = jnp.concatenate([dct[n].reshape(-1) for n in small_names])
        return jnp.pad(flat, (0, rows * 128 - tot), constant_values=1.0).reshape(rows, 128)

    dl, nm, nv = adamw(pack(weights), pack(grads), pack(m_in), pack(v_in))
    off = 0
    for n, sz in zip(small_names, sizes):
        shp = weights[n].shape
        take = lambda a: a.reshape(-1)[off:off + sz].reshape(shp)
        deltas[n], new_m[n], new_v[n] = take(dl), take(nm), take(nv)
        off += sz

    return (loss, grad_x[None], *[grads[n] for n in names], *[deltas[n] for n in names],
            *[new_m[n] for n in names], *[new_v[n] for n in names])
```

```python
import functools
import math

import jax
import jax.numpy as jnp
import numpy as np
from jax import lax
from jax.experimental import pallas as pl
from jax.experimental.pallas import tpu as pltpu

F32 = jnp.float32
BF16 = jnp.bfloat16
EPS = 1e-6
MESH = pl.DeviceIdType.MESH

SSM_HEAD_DIM = 64
SSM_GROUPS = 4
SSM_STATE = 128
SSM_CONV = 4
SSM_CHUNK = 256
ATT_HEAD_DIM = 64
ATT_WINDOW = 128
REL_BUCKETS = 32
N_CHIPS = 4

ADAM_LR = 0.001
ADAM_B1 = 0.9
ADAM_B2 = 0.999
ADAM_EPS = 1e-08
ADAM_WD = 0.01
ADAM_STEP = 10

VMEM_LIMIT_BYTES = 56 * 1024 * 1024
NEG = -1e30


DEP_SPEC = pl.BlockSpec(memory_space=pl.ANY)


def _params(*sem):
    return pltpu.CompilerParams(dimension_semantics=sem if sem else None, vmem_limit_bytes=VMEM_LIMIT_BYTES)


def _dot(a, b):
    return jnp.dot(a, b, preferred_element_type=F32)


def _dot_nt(a, b):
    return lax.dot_general(a, b, (((1,), (1,)), ((), ())), preferred_element_type=F32)


def _dot_tn(a, b):
    return lax.dot_general(a, b, (((0,), (0,)), ((), ())), preferred_element_type=F32)


def _b(x):
    return x.astype(BF16)


@jax.custom_vjp
def _bmm(a, b):
    return _dot(_b(a), _b(b))


def _bmm_fwd(a, b):
    return _bmm(a, b), (a, b)


def _bmm_bwd(res, g):
    a, b = res
    g16 = _b(g)
    return _dot_nt(g16, _b(b)).astype(a.dtype), _dot_tn(_b(a), g16).astype(b.dtype)


_bmm.defvjp(_bmm_fwd, _bmm_bwd)


@jax.custom_vjp
def _bmm_nt(a, b):
    return _dot_nt(_b(a), _b(b))


def _bmm_nt_fwd(a, b):
    return _bmm_nt(a, b), (a, b)


def _bmm_nt_bwd(res, g):
    a, b = res
    g16 = _b(g)
    return _dot(g16, _b(b)).astype(a.dtype), _dot_tn(g16, _b(a)).astype(b.dtype)


_bmm_nt.defvjp(_bmm_nt_fwd, _bmm_nt_bwd)


@jax.custom_vjp
def _bmm_tn(a, b):
    return _dot_tn(_b(a), _b(b))


def _bmm_tn_fwd(a, b):
    return _bmm_tn(a, b), (a, b)


def _bmm_tn_bwd(res, g):
    a, b = res
    g16 = _b(g)
    return _dot_nt(_b(b), g16).astype(a.dtype), _dot(_b(a), g16).astype(b.dtype)


_bmm_tn.defvjp(_bmm_tn_fwd, _bmm_tn_bwd)


def _split3(x):
    hi = _b(x)
    r = x - hi.astype(F32)
    mid = _b(r)
    lo = _b(r - mid.astype(F32))
    return hi, mid, lo


def _x_left_raw(m, x):
    hi, mid, lo = _split3(x)
    return _dot(m, hi) + _dot(m, mid) + _dot(m, lo)


def _x_left_t_raw(m, x):
    hi, mid, lo = _split3(x)
    return _dot_tn(m, hi) + _dot_tn(m, mid) + _dot_tn(m, lo)


def _x_right_raw(x, m):
    hi, mid, lo = _split3(x)
    return _dot(hi, m) + _dot(mid, m) + _dot(lo, m)


def _x_right_t_raw(x, m):
    hi, mid, lo = _split3(x)
    return _dot_nt(hi, m) + _dot_nt(mid, m) + _dot_nt(lo, m)


@jax.custom_vjp
def _xleft(m, x):
    return _x_left_raw(m, x)


_xleft.defvjp(lambda m, x: (_x_left_raw(m, x), m),
              lambda m, g: (jnp.zeros_like(m), _x_left_t_raw(m, g)))


@jax.custom_vjp
def _xright(x, m):
    return _x_right_raw(x, m)


_xright.defvjp(lambda x, m: (_x_right_raw(x, m), m),
               lambda m, g: (_x_right_t_raw(g, m), jnp.zeros_like(m)))


def _sigmoid(x):
    return 1.0 / (1.0 + jnp.exp(-x))


def _silu(x):
    return x * _sigmoid(x)


def _softplus(x):
    return jnp.maximum(x, 0.0) + jnp.log(1.0 + jnp.exp(-jnp.abs(x)))


def _rms(x):
    return x * lax.rsqrt(jnp.mean(x * x, axis=-1, keepdims=True) + EPS)


def _iota(shape, dim):
    return lax.broadcasted_iota(jnp.int32, shape, dim)


def _blockdiag64(n):
    return jnp.where(_iota((n, n), 0) // 64 == _iota((n, n), 1) // 64, 1.0, 0.0).astype(BF16)


def _group64_rms(x, seg_sum):
    ms = seg_sum(x * x) * (1.0 / 64.0)
    return x * lax.rsqrt(ms + EPS)


def _fold64(x):
    ax = x.ndim - 1
    w = x.shape[ax]
    lo = (_iota(x.shape, ax) % 128) < 64
    return x + jnp.where(lo, pltpu.roll(x, w - 64, ax), pltpu.roll(x, 64, ax))


def _tile(n, want):
    t = min(n, want)
    assert n % t == 0, (n, t)
    return t


def _lane_tile(n, cap=1536):
    if n <= cap:
        return n
    return max(w for w in range(128, cap + 1, 128) if n % w == 0)


def ffn_fwd(h, g, w1, w3, w2, blk, dep):
    t, d = h.shape
    nk, fs = w1.shape[0], w1.shape[-1]
    tm = _tile(t, 512)

    def body(h_ref, g_ref, w1_ref, w3_ref, w2_ref, dep_ref, o_ref, a_ref, b_ref, u_scr, acc):
        k = pl.program_id(1)

        @pl.when(k == 0)
        def _():
            u_scr[...] = _b(_rms(h_ref[...]) * g_ref[...])
            acc[...] = jnp.zeros_like(acc)

        u = u_scr[...]
        a = _dot(u, w1_ref[...])
        b = _dot(u, w3_ref[...])
        a_ref[...] = _b(a)
        b_ref[...] = _b(b)
        acc[...] += _dot(_b(_silu(a) * b), w2_ref[...])

        @pl.when(k == nk - 1)
        def _():
            o_ref[...] = h_ref[...] + 0.5 * acc[...]

    wspec = lambda r, c: pl.BlockSpec((None, None, r, c), lambda i, k: (k, blk, 0, 0))
    return pl.pallas_call(
        body, name="ffn_fwd",
        grid=(t // tm, nk),
        in_specs=[pl.BlockSpec((tm, d), lambda i, k: (i, 0)), pl.BlockSpec((1, d), lambda i, k: (0, 0)),
                  wspec(d, fs), wspec(d, fs), wspec(fs, d), DEP_SPEC],
        out_specs=[pl.BlockSpec((tm, d), lambda i, k: (i, 0)),
                   pl.BlockSpec((None, tm, fs), lambda i, k: (k, i, 0)),
                   pl.BlockSpec((None, tm, fs), lambda i, k: (k, i, 0))],
        out_shape=[jax.ShapeDtypeStruct((t, d), F32), jax.ShapeDtypeStruct((nk, t, fs), BF16),
                   jax.ShapeDtypeStruct((nk, t, fs), BF16)],
        scratch_shapes=[pltpu.VMEM((tm, d), BF16), pltpu.VMEM((tm, d), F32)],
        compiler_params=_params("arbitrary", "arbitrary"),
    )(h, g, w1, w3, w2, dep)


def ffn_bwd(h, dy, g, a_s, b_s, w1, w3, w2, blk, dep):
    t, d = h.shape
    nk, fs = w1.shape[0], w1.shape[-1]
    tm = _tile(t, 512)

    def body(h_ref, dy_ref, g_ref, a_ref, b_ref, w1_ref, w3_ref, w2_ref, dep_ref,
             dh_ref, u_ref, da_ref, db_ref, s_ref, dg_ref, dyh_scr, du_acc, da0, db0, da1, db1):
        i, k = pl.program_id(0), pl.program_id(1)

        @pl.when(k == 0)
        def _():
            dyh_scr[...] = _b(0.5 * dy_ref[...])
            du_acc[...] = jnp.zeros_like(du_acc)

        @pl.when((k == 0) & (i == 0))
        def _():
            dg_ref[...] = jnp.zeros_like(dg_ref)

        def step(prev, cur):
            if prev is not None:
                du_acc[...] += _dot_nt(prev[0][...], w1_ref[...]) + _dot_nt(prev[1][...], w3_ref[...])
            if cur is not None:
                ds = _dot_nt(dyh_scr[...], w2_ref[...])
                a = a_ref[...].astype(F32)
                b = b_ref[...].astype(F32)
                sig = _sigmoid(a)
                sl = a * sig
                s_ref[...] = _b(sl * b)
                da = _b(ds * b * (sig * (1.0 + a * (1.0 - sig))))
                db = _b(ds * sl)
                da_ref[...] = da
                db_ref[...] = db
                cur[0][...] = da
                cur[1][...] = db

        even, odd = (da0, db0), (da1, db1)

        @pl.when(k == 0)
        def _():
            step(None, even)

        @pl.when((k > 0) & (k < nk) & (k % 2 == 1))
        def _():
            step(even, odd)

        @pl.when((k > 0) & (k < nk) & (k % 2 == 0))
        def _():
            step(odd, even)

        @pl.when(k == nk)
        def _():
            step(odd if nk % 2 == 0 else even, None)
            hh = h_ref[...]
            rstd = lax.rsqrt(jnp.mean(hh * hh, axis=-1, keepdims=True) + EPS)
            xh = hh * rstd
            gg = g_ref[...]
            u_ref[...] = _b(xh * gg)
            du = du_acc[...]
            dg_ref[...] += jnp.sum(du * xh, axis=0, keepdims=True)
            dxh = du * gg
            dh_ref[...] = dy_ref[...] + rstd * (dxh - xh * jnp.mean(dxh * xh, axis=-1, keepdims=True))

    cur = lambda k: jnp.minimum(k, nk - 1)
    prv = lambda k: jnp.maximum(k - 1, 0)
    wcur = lambda r, c: pl.BlockSpec((None, None, r, c), lambda i, k: (cur(k), blk, 0, 0))
    wprv = lambda r, c: pl.BlockSpec((None, None, r, c), lambda i, k: (prv(k), blk, 0, 0))
    tok = pl.BlockSpec((tm, d), lambda i, k: (i, 0))
    hid = pl.BlockSpec((None, tm, fs), lambda i, k: (cur(k), i, 0))
    return pl.pallas_call(
        body, name="ffn_bwd",
        grid=(t // tm, nk + 1),
        in_specs=[tok, tok, pl.BlockSpec((1, d), lambda i, k: (0, 0)), hid, hid, wprv(d, fs), wprv(d, fs), wcur(fs, d),
                  DEP_SPEC],
        out_specs=[tok, tok, hid, hid, hid, pl.BlockSpec((1, d), lambda i, k: (0, 0))],
        out_shape=[jax.ShapeDtypeStruct((t, d), F32), jax.ShapeDtypeStruct((t, d), BF16),
                   jax.ShapeDtypeStruct((nk, t, fs), BF16), jax.ShapeDtypeStruct((nk, t, fs), BF16),
                   jax.ShapeDtypeStruct((nk, t, fs), BF16), jax.ShapeDtypeStruct((1, d), F32)],
        scratch_shapes=[pltpu.VMEM((tm, d), BF16), pltpu.VMEM((tm, d), F32)] + [pltpu.VMEM((tm, fs), BF16)] * 4,
        compiler_params=_params("arbitrary", "arbitrary"),
    )(h, dy, g, a_s, b_s, w1, w3, w2, dep)


def wgrad_grouped_b(a, bs, dep, scale=1.0):
    t, m = a.shape
    ng, _, n = bs.shape
    tk = _tile(t, 2048)

    def body(a_ref, b_ref, dep_ref, o_ref):
        j = pl.program_id(1)

        @pl.when(j == 0)
        def _():
            o_ref[...] = jnp.zeros_like(o_ref)

        o_ref[...] += _dot_tn(_b(a_ref[...]), _b(b_ref[...]))

        if scale != 1.0:
            @pl.when(j == pl.num_programs(1) - 1)
            def _():
                o_ref[...] = o_ref[...] * scale

    return pl.pallas_call(
        body, name="wgrad_gb",
        grid=(ng, t // tk),
        in_specs=[pl.BlockSpec((tk, m), lambda k, j: (j, 0)), pl.BlockSpec((None, tk, n), lambda k, j: (k, j, 0)),
                  DEP_SPEC],
        out_specs=pl.BlockSpec((None, m, n), lambda k, j: (k, 0, 0)),
        out_shape=jax.ShapeDtypeStruct((ng, m, n), F32),
        compiler_params=_params("arbitrary", "arbitrary"),
    )(a, bs, dep)


def wgrad_grouped_a(as_, b, scale=1.0):
    ng, t, m = as_.shape
    n = b.shape[1]
    tk = _tile(t, 2048)

    def body(a_ref, b_ref, o_ref):
        j = pl.program_id(1)

        @pl.when(j == 0)
        def _():
            o_ref[...] = jnp.zeros_like(o_ref)

        o_ref[...] += _dot_tn(_b(a_ref[...]), _b(b_ref[...]))

        if scale != 1.0:
            @pl.when(j == pl.num_programs(1) - 1)
            def _():
                o_ref[...] = o_ref[...] * scale

    return pl.pallas_call(
        body, name="wgrad_ga",
        grid=(ng, t // tk),
        in_specs=[pl.BlockSpec((None, tk, m), lambda k, j: (k, j, 0)), pl.BlockSpec((tk, n), lambda k, j: (j, 0))],
        out_specs=pl.BlockSpec((None, m, n), lambda k, j: (k, 0, 0)),
        out_shape=jax.ShapeDtypeStruct((ng, m, n), F32),
        compiler_params=_params("arbitrary", "arbitrary"),
    )(as_, b)


def wgrad(a, b):
    t, m = a.shape
    n = b.shape[1]
    tk = _tile(t, 1024)
    tn = _lane_tile(n, 1536 if m <= 1024 else 512)

    def body(a_ref, b_ref, o_ref):
        @pl.when(pl.program_id(1) == 0)
        def _():
            o_ref[...] = jnp.zeros_like(o_ref)

        o_ref[...] += _dot_tn(_b(a_ref[...]), _b(b_ref[...]))

    return pl.pallas_call(
        body, name="wgrad",
        grid=(n // tn, t // tk),
        in_specs=[pl.BlockSpec((tk, m), lambda c, j: (j, 0)), pl.BlockSpec((tk, tn), lambda c, j: (j, c))],
        out_specs=pl.BlockSpec((m, tn), lambda c, j: (0, c)),
        out_shape=jax.ShapeDtypeStruct((m, n), F32),
        compiler_params=_params("arbitrary", "arbitrary"),
    )(a, b)


def norm_mm(h, g, w):
    t, d = h.shape
    n = w.shape[1]
    tm = _tile(t, 1024)
    tn = _lane_tile(n)

    def body(h_ref, g_ref, w_ref, o_ref, u_scr):
        @pl.when(pl.program_id(1) == 0)
        def _():
            u_scr[...] = _b(_rms(h_ref[...]) * g_ref[...])

        o_ref[...] = _dot(u_scr[...], w_ref[...])

    return pl.pallas_call(
        body, name="norm_mm",
        grid=(t // tm, n // tn),
        in_specs=[pl.BlockSpec((tm, d), lambda i, j: (i, 0)), pl.BlockSpec((1, d), lambda i, j: (0, 0)),
                  pl.BlockSpec((d, tn), lambda i, j: (0, j))],
        out_specs=pl.BlockSpec((tm, tn), lambda i, j: (i, j)),
        out_shape=jax.ShapeDtypeStruct((t, n), F32),
        scratch_shapes=[pltpu.VMEM((tm, d), BF16)],
        compiler_params=_params("arbitrary", "arbitrary"),
    )(h, g, w)


def norm_mm_bwd(h, g, w, dout, dres, dep, scale=1.0):
    t, d = h.shape
    n = w.shape[1]
    tm = _tile(t, 512)
    tn = _lane_tile(n)
    nj = n // tn

    def body(h_ref, g_ref, w_ref, do_ref, dr_ref, dep_ref, dh_ref, u_ref, dg_ref, du_acc):
        i, j = pl.program_id(0), pl.program_id(1)

        @pl.when(j == 0)
        def _():
            du_acc[...] = jnp.zeros_like(du_acc)

        @pl.when((j == 0) & (i == 0))
        def _():
            dg_ref[...] = jnp.zeros_like(dg_ref)

        du_acc[...] += _dot_nt(_b(do_ref[...]), w_ref[...])

        @pl.when(j == nj - 1)
        def _():
            hh = h_ref[...]
            rstd = lax.rsqrt(jnp.mean(hh * hh, axis=-1, keepdims=True) + EPS)
            xh = hh * rstd
            gg = g_ref[...]
            u_ref[...] = _b(xh * gg)
            du = du_acc[...] * scale
            dg_ref[...] += jnp.sum(du * xh, axis=0, keepdims=True)
            dxh = du * gg
            dh_ref[...] = dr_ref[...] + rstd * (dxh - xh * jnp.mean(dxh * xh, axis=-1, keepdims=True))

    tok = pl.BlockSpec((tm, d), lambda i, j: (i, 0))
    return pl.pallas_call(
        body, name="norm_mm_bwd",
        grid=(t // tm, nj),
        in_specs=[tok, pl.BlockSpec((1, d), lambda i, j: (0, 0)), pl.BlockSpec((d, tn), lambda i, j: (0, j)),
                  pl.BlockSpec((tm, tn), lambda i, j: (i, j)), tok, DEP_SPEC],
        out_specs=[tok, tok, pl.BlockSpec((1, d), lambda i, j: (0, 0))],
        out_shape=[jax.ShapeDtypeStruct((t, d), F32), jax.ShapeDtypeStruct((t, d), BF16),
                   jax.ShapeDtypeStruct((1, d), F32)],
        scratch_shapes=[pltpu.VMEM((tm, d), F32)],
        compiler_params=_params("arbitrary", "arbitrary"),
    )(h, g, w, dout, dres, dep)


CONV_COLS = 512


CONV_ROWS = 64


def _conv_pre(ext, w, b, r0, n):
    return (b + w[0:1] * ext[pl.ds(5 + r0, n), :] + w[1:2] * ext[pl.ds(6 + r0, n), :]
            + w[2:3] * ext[pl.ds(7 + r0, n), :] + w[3:4] * ext[pl.ds(8 + r0, n), :])


def conv_fwd(zx, cw, cb, col0):
    t = zx.shape[0]
    c = cw.shape[1]
    tm = _tile(t, 512)
    cb0 = col0 // CONV_COLS

    rc = _tile(tm, CONV_ROWS)

    def body(x_ref, w_ref, b_ref, o_ref, ext):
        @pl.when(pl.program_id(1) == 0)
        def _():
            ext[0:8, :] = jnp.zeros((8, CONV_COLS), F32)

        ext[8:, :] = x_ref[...]
        w, b = w_ref[...], b_ref[...]
        for r0 in range(0, tm, rc):
            o_ref[r0:r0 + rc, :] = _silu(_conv_pre(ext, w, b, r0, rc))
        ext[0:8, :] = ext[tm:tm + 8, :]

    return pl.pallas_call(
        body, name="conv_fwd",
        grid=(c // CONV_COLS, t // tm),
        in_specs=[pl.BlockSpec((tm, CONV_COLS), lambda j, i: (i, cb0 + j)),
                  pl.BlockSpec((8, CONV_COLS), lambda j, i: (0, j)), pl.BlockSpec((1, CONV_COLS), lambda j, i: (0, j))],
        out_specs=pl.BlockSpec((tm, CONV_COLS), lambda j, i: (i, j)),
        out_shape=jax.ShapeDtypeStruct((t, c), F32),
        scratch_shapes=[pltpu.VMEM((tm + 8, CONV_COLS), F32)],
        compiler_params=_params("arbitrary", "arbitrary"),
    )(zx, cw, cb)


def conv_bwd(dzx, zx, dxs, dbm, dcm, cw, cb, col0):
    t = zx.shape[0]
    c = cw.shape[1]
    tm = _tile(t, 512)
    nt = t // tm
    cb0 = col0 // CONV_COLS
    nxs = dxs.shape[1] // CONV_COLS
    hb = tm // 8

    rc = _tile(tm, CONV_ROWS)

    def body(dzx_ref, x_ref, xh_ref, dxs_ref, db_ref, dc_ref, w_ref, b_ref, o_ref, dw_ref, dbias_ref, ext, gy):
        j, i = pl.program_id(0), pl.program_id(1)
        ri = nt - 1 - i

        @pl.when(i == 0)
        def _():
            gy[tm:tm + 8, :] = jnp.zeros((8, CONV_COLS), F32)
            dw_ref[...] = jnp.zeros_like(dw_ref)
            dbias_ref[...] = jnp.zeros_like(dbias_ref)

        ext[0:8, :] = jnp.where(ri > 0, xh_ref[...], 0.0)
        ext[8:, :] = x_ref[...]
        w, b = w_ref[...], b_ref[...]
        dw = [jnp.zeros((1, CONV_COLS), F32) for _ in range(SSM_CONV)]
        dbias = jnp.zeros((1, CONV_COLS), F32)
        for r0 in range(0, tm, rc):
            rows = pl.ds(r0, rc)
            win = [ext[pl.ds(5 + tap + r0, rc), :] for tap in range(SSM_CONV)]
            y = b + w[0:1] * win[0] + w[1:2] * win[1] + w[2:3] * win[2] + w[3:4] * win[3]
            sig = _sigmoid(y)
            dout = jnp.where(j < nxs, dxs_ref[rows, :], jnp.where(j == nxs, db_ref[rows, :], dc_ref[rows, :]))
            g = dout * (sig * (1.0 + y * (1.0 - sig)))
            gy[rows, :] = g
            dbias = dbias + jnp.sum(g, axis=0, keepdims=True)
            for tap in range(SSM_CONV):
                dw[tap] = dw[tap] + jnp.sum(g * win[tap], axis=0, keepdims=True)
        for r0 in range(0, tm, rc):
            o_ref[r0:r0 + rc, :] = (w[0:1] * gy[pl.ds(r0 + 3, rc), :] + w[1:2] * gy[pl.ds(r0 + 2, rc), :]
                                    + w[2:3] * gy[pl.ds(r0 + 1, rc), :] + w[3:4] * gy[pl.ds(r0, rc), :])
        gy[tm:tm + 8, :] = gy[0:8, :]
        for tap in range(SSM_CONV):
            dw_ref[tap:tap + 1, :] += dw[tap]
        dbias_ref[...] += dbias

    return pl.pallas_call(
        body, name="conv_bwd",
        grid=(c // CONV_COLS, nt),
        in_specs=[pl.BlockSpec(memory_space=pl.ANY),
                  pl.BlockSpec((tm, CONV_COLS), lambda j, i: (nt - 1 - i, cb0 + j)),
                  pl.BlockSpec((8, CONV_COLS), lambda j, i: (jnp.maximum((nt - 1 - i) * hb - 1, 0), cb0 + j)),
                  pl.BlockSpec((tm, CONV_COLS), lambda j, i: (nt - 1 - i, jnp.minimum(j, nxs - 1))),
                  pl.BlockSpec((tm, CONV_COLS), lambda j, i: (nt - 1 - i, 0)),
                  pl.BlockSpec((tm, CONV_COLS), lambda j, i: (nt - 1 - i, 0)),
                  pl.BlockSpec((8, CONV_COLS), lambda j, i: (0, j)), pl.BlockSpec((1, CONV_COLS), lambda j, i: (0, j))],
        out_specs=[pl.BlockSpec((tm, CONV_COLS), lambda j, i: (nt - 1 - i, cb0 + j)),
                   pl.BlockSpec((8, CONV_COLS), lambda j, i: (0, j)), pl.BlockSpec((1, CONV_COLS), lambda j, i: (0, j))],
        out_shape=[jax.ShapeDtypeStruct(dzx.shape, F32), jax.ShapeDtypeStruct((8, c), F32),
                   jax.ShapeDtypeStruct((1, c), F32)],
        scratch_shapes=[pltpu.VMEM((tm + 8, CONV_COLS), F32), pltpu.VMEM((tm + 8, CONV_COLS), F32)],
        input_output_aliases={0: 0},
        compiler_params=_params("arbitrary", "arbitrary"),
    )(dzx, zx, zx, dxs, dbm, dcm, cw, cb)


def _ssd_group(xs, bg, cg, dtraw, s0, bias, alog, dsk):
    L = xs.shape[0]
    causal = _iota((L, L), 0) >= _iota((L, L), 1)
    tril = jnp.where(causal, 1.0, 0.0).astype(BF16)
    dt = _softplus(dtraw + bias)
    a = -jnp.exp(alog)
    acum = _xleft(tril, dt * a)
    acum_t = acum.T
    dt_t = dt.T
    cb = _bmm_nt(cg, bg)
    lo = _iota((L, 128), 1) < 64
    lo_row = _iota((1, 128), 1) < 64
    lo_col = _iota((128, 1), 0) < 64
    alast = acum[L - 1:L, :]
    ys, s1s = [], []
    for q in range(4):
        xp = xs[:, q * 128:(q + 1) * 128]
        sp = s0[q * 128:(q + 1) * 128, :]
        yd, ec, wc, el = [], [], [], []
        for j in range(2):
            r = 2 * q + j
            ac = acum[:, r:r + 1]
            decay = jnp.exp(jnp.where(causal, ac - acum_t[r:r + 1, :], NEG))
            yd.append(_bmm(cb * decay * dt_t[r:r + 1, :], xp))
            ec.append(jnp.exp(ac))
            al = alast[:, r:r + 1]
            wc.append(jnp.exp(al - ac) * dt[:, r:r + 1])
            el.append(jnp.exp(al))
        y_off = _bmm_nt(cg, sp) * jnp.where(lo, ec[0], ec[1])
        dsel = jnp.where(lo_row, dsk[:, 2 * q:2 * q + 1], dsk[:, 2 * q + 1:2 * q + 2])
        ys.append(jnp.where(lo, yd[0], yd[1]) + y_off + dsel * xp)
        xw = xp * jnp.where(lo, wc[0], wc[1])
        s1s.append(sp * jnp.where(lo_col, el[0], el[1]) + _bmm_tn(xw, bg))
    return jnp.concatenate(ys, axis=1), jnp.concatenate(s1s, axis=0)


def ssd_fwd(xc, zx, bias, alog, dsk, dt_col0):
    t = xc.shape[0]
    L = _tile(t, SSM_CHUNK)
    nc = t // L
    g = SSM_GROUPS
    dtb = dt_col0 // 512

    def body(xs_ref, b_ref, c_ref, dt_ref, bias_ref, alog_ref, dsk_ref, y_ref, st_ref, state):
        @pl.when(pl.program_id(0) == 0)
        def _():
            state[...] = jnp.zeros_like(state)

        for gi in range(g):
            lane = slice(gi * 128, (gi + 1) * 128)
            wide = slice(gi * 512, (gi + 1) * 512)
            s0 = state[gi]
            st_ref[gi] = s0
            y, s1 = _ssd_group(xs_ref[:, wide], b_ref[:, lane], c_ref[:, lane], dt_ref[:, lane], s0,
                               bias_ref[:, lane], alog_ref[:, lane], dsk_ref[:, lane])
            y_ref[:, wide] = y
            state[gi] = s1

    vec = pl.BlockSpec((1, 512), lambda c: (0, 0))
    return pl.pallas_call(
        body, name="ssd_fwd",
        grid=(nc,),
        in_specs=[pl.BlockSpec((L, 2048), lambda c: (c, 0)), pl.BlockSpec((L, 512), lambda c: (c, 4)),
                  pl.BlockSpec((L, 512), lambda c: (c, 5)), pl.BlockSpec((L, 512), lambda c: (c, dtb)), vec, vec, vec],
        out_specs=[pl.BlockSpec((L, 2048), lambda c: (c, 0)),
                   pl.BlockSpec((None, g, 512, 128), lambda c: (c, 0, 0, 0))],
        out_shape=[jax.ShapeDtypeStruct((t, 2048), F32), jax.ShapeDtypeStruct((nc, g, 512, 128), F32)],
        scratch_shapes=[pltpu.VMEM((g, 512, 128), F32)],
        compiler_params=_params("arbitrary"),
    )(xc, xc, xc, zx, bias, alog, dsk)


def ssd_bwd(dzx, dy, xc, zx, states, bias, alog, dsk, dt_col0):
    t = xc.shape[0]
    L = _tile(t, SSM_CHUNK)
    nc = t // L
    g = SSM_GROUPS
    dtb = dt_col0 // 512

    def body(dzx_ref, dy_ref, xs_ref, b_ref, c_ref, dt_ref, st_ref, bias_ref, alog_ref, dsk_ref,
             ddt_ref, dxs_ref, db_ref, dc_ref, dbias_ref, dalog_ref, ddsk_ref, dstate):
        @pl.when(pl.program_id(0) == 0)
        def _():
            dstate[...] = jnp.zeros_like(dstate)
            dbias_ref[...] = jnp.zeros_like(dbias_ref)
            dalog_ref[...] = jnp.zeros_like(dalog_ref)
            ddsk_ref[...] = jnp.zeros_like(ddsk_ref)

        for gi in range(g):
            lane = slice(gi * 128, (gi + 1) * 128)
            wide = slice(gi * 512, (gi + 1) * 512)
            _, vjp = jax.vjp(_ssd_group, xs_ref[:, wide], b_ref[:, lane], c_ref[:, lane], dt_ref[:, lane], st_ref[gi],
                             bias_ref[:, lane], alog_ref[:, lane], dsk_ref[:, lane])
            dxs, db, dc, ddt, ds0, dbias, dalog, ddsk = vjp((dy_ref[:, wide], dstate[gi]))
            dxs_ref[:, wide] = dxs
            db_ref[:, lane] = db
            dc_ref[:, lane] = dc
            ddt_ref[:, lane] = ddt
            dstate[gi] = ds0
            dbias_ref[:, lane] += dbias
            dalog_ref[:, lane] += dalog
            ddsk_ref[:, lane] += ddsk

    rc = lambda c: nc - 1 - c
    vec = pl.BlockSpec((1, 512), lambda c: (0, 0))
    return pl.pallas_call(
        body, name="ssd_bwd",
        grid=(nc,),
        in_specs=[pl.BlockSpec(memory_space=pl.ANY),
                  pl.BlockSpec((L, 2048), lambda c: (rc(c), 0)), pl.BlockSpec((L, 2048), lambda c: (rc(c), 0)),
                  pl.BlockSpec((L, 512), lambda c: (rc(c), 4)), pl.BlockSpec((L, 512), lambda c: (rc(c), 5)),
                  pl.BlockSpec((L, 512), lambda c: (rc(c), dtb)),
                  pl.BlockSpec((None, g, 512, 128), lambda c: (rc(c), 0, 0, 0)), vec, vec, vec],
        out_specs=[pl.BlockSpec((L, 512), lambda c: (rc(c), dtb)), pl.BlockSpec((L, 2048), lambda c: (rc(c), 0)),
                   pl.BlockSpec((L, 512), lambda c: (rc(c), 0)), pl.BlockSpec((L, 512), lambda c: (rc(c), 0)),
                   vec, vec, vec],
        out_shape=[jax.ShapeDtypeStruct(dzx.shape, F32), jax.ShapeDtypeStruct((t, 2048), F32),
                   jax.ShapeDtypeStruct((t, 512), F32), jax.ShapeDtypeStruct((t, 512), F32),
                   jax.ShapeDtypeStruct((1, 512), F32), jax.ShapeDtypeStruct((1, 512), F32),
                   jax.ShapeDtypeStruct((1, 512), F32)],
        scratch_shapes=[pltpu.VMEM((g, 512, 128), F32)],
        input_output_aliases={0: 0},
        compiler_params=_params("arbitrary"),
    )(dzx, dy, xc, xc, xc, zx, states, bias, alog, dsk)


def _gate_tile(y, z, gn):
    gated = y * _silu(z)
    parts = [_rms(gated[:, k * 512:(k + 1) * 512]) for k in range(SSM_GROUPS)]
    return jnp.concatenate(parts, axis=1) * gn


def gate_out_fwd(h, y, zx, gn, w_out):
    t, d = h.shape
    di = y.shape[1]
    tm = _tile(t, 256)

    def body(h_ref, y_ref, z_ref, gn_ref, w_ref, o_ref):
        yn = _gate_tile(y_ref[...], z_ref[...], gn_ref[...])
        o_ref[...] = h_ref[...] + _dot(_b(yn), w_ref[...])

    return pl.pallas_call(
        body, name="gate_out_fwd",
        grid=(t // tm,),
        in_specs=[pl.BlockSpec((tm, d), lambda i: (i, 0)), pl.BlockSpec((tm, di), lambda i: (i, 0)),
                  pl.BlockSpec((tm, di), lambda i: (i, 0)), pl.BlockSpec((1, di), lambda i: (0, 0)),
                  pl.BlockSpec((di, d), lambda i: (0, 0))],
        out_specs=pl.BlockSpec((tm, d), lambda i: (i, 0)),
        out_shape=jax.ShapeDtypeStruct((t, d), F32),
        compiler_params=_params("arbitrary"),
    )(h, y, zx, gn, w_out)


def gate_out_bwd(dy, y, zx, gn, w_out, n_zx, dep):
    t, d = dy.shape
    di = y.shape[1]
    tm = _tile(t, 256)

    def body(dy_ref, y_ref, z_ref, gn_ref, w_ref, dep_ref, dz_ref, dys_ref, yn_ref, dgn_ref):
        @pl.when(pl.program_id(0) == 0)
        def _():
            dgn_ref[...] = jnp.zeros_like(dgn_ref)

        yn, vjp = jax.vjp(_gate_tile, y_ref[...], z_ref[...], gn_ref[...])
        dyn = _dot_nt(_b(dy_ref[...]), w_ref[...])
        dys, dz, dgn = vjp(dyn)
        yn_ref[...] = _b(yn)
        dys_ref[...] = dys
        dz_ref[...] = dz
        dgn_ref[...] += dgn

    return pl.pallas_call(
        body, name="gate_out_bwd",
        grid=(t // tm,),
        in_specs=[pl.BlockSpec((tm, d), lambda i: (i, 0)), pl.BlockSpec((tm, di), lambda i: (i, 0)),
                  pl.BlockSpec((tm, di), lambda i: (i, 0)), pl.BlockSpec((1, di), lambda i: (0, 0)),
                  pl.BlockSpec((di, d), lambda i: (0, 0)), DEP_SPEC],
        out_specs=[pl.BlockSpec((tm, di), lambda i: (i, 0)), pl.BlockSpec((tm, di), lambda i: (i, 0)),
                   pl.BlockSpec((tm, di), lambda i: (i, 0)), pl.BlockSpec((1, di), lambda i: (0, 0))],
        out_shape=[jax.ShapeDtypeStruct((t, n_zx), F32), jax.ShapeDtypeStruct((t, di), F32),
                   jax.ShapeDtypeStruct((t, di), BF16), jax.ShapeDtypeStruct((1, di), F32)],
        compiler_params=_params("arbitrary"),
    )(dy, y, zx, gn, w_out, dep)


def _attn_block(qp, kvp, kvc, biasm, sinks, qg, kg, w_o, first):
    nq = qp.shape[0]
    n_pairs = qp.shape[1] // 128
    hk = n_pairs
    rows = hk * nq
    seg = functools.partial(_xright, m=_blockdiag64(128))
    scale = ATT_HEAD_DIM ** -0.5
    qi = (_iota((rows, 2 * nq), 0) % nq) + nq
    kj = _iota((rows, 2 * nq), 1)
    dist = qi - kj
    valid = (dist >= 0) & (dist < ATT_WINDOW) & (jnp.logical_not(first) | (kj >= nq))
    lo = _iota((nq, 128), 1) < 64
    kv = jnp.concatenate([kvp, kvc], axis=0)
    outs = [None] * n_pairs
    for kvh in range(2):
        kn = _group64_rms(kv[:, kvh * 128:(kvh + 1) * 128], seg) * kg
        vv = kv[:, 256 + kvh * 128:256 + (kvh + 1) * 128]
        pairs = range(kvh * hk // 2, (kvh + 1) * hk // 2)
        qs, sk = [], []
        for p in pairs:
            qn = _group64_rms(qp[:, p * 128:(p + 1) * 128], seg) * qg
            qs += [jnp.where(lo, qn, 0.0), jnp.where(lo, 0.0, qn)]
            sk += [jnp.broadcast_to(sinks[:, h:h + 1], (nq, 1)) for h in (2 * p, 2 * p + 1)]
        sink = jnp.concatenate(sk, axis=0)
        s = _bmm_nt(jnp.concatenate(qs, axis=0), kn) * scale + biasm[kvh * rows:(kvh + 1) * rows]
        s = jnp.where(valid, s, NEG)
        m = lax.stop_gradient(jnp.maximum(jnp.max(s, axis=-1, keepdims=True), sink))
        pexp = jnp.exp(s - m)
        den = jnp.sum(pexp, axis=-1, keepdims=True) + jnp.exp(sink - m)
        o = _bmm(pexp * (1.0 / den), vv)
        for n, p in enumerate(pairs):
            outs[p] = jnp.where(lo, o[2 * n * nq:(2 * n + 1) * nq], o[(2 * n + 1) * nq:(2 * n + 2) * nq])
    o = jnp.concatenate(outs, axis=1)
    return _bmm(o, w_o), o


def attn_fwd(h, qp, kvd, biasm, sinks, qg, kg, w_o):
    t, d = h.shape
    nq = ATT_WINDOW
    nb = t // nq
    nh = qp.shape[1] // ATT_HEAD_DIM

    def body(h_ref, q_ref, kp_ref, kc_ref, bias_ref, s_ref, qg_ref, kg_ref, w_ref, o_ref):
        out, _ = _attn_block(q_ref[...], kp_ref[...], kc_ref[...], bias_ref[...], s_ref[...], qg_ref[...],
                             kg_ref[...], w_ref[...], pl.program_id(0) == 0)
        o_ref[...] = h_ref[...] + out

    vec = pl.BlockSpec((1, 128), lambda i: (0, 0))
    return pl.pallas_call(
        body, name="attn_fwd",
        grid=(nb,),
        in_specs=[pl.BlockSpec((nq, d), lambda i: (i, 0)), pl.BlockSpec((nq, nh * 64), lambda i: (i, 0)),
                  pl.BlockSpec((nq, 512), lambda i: (jnp.maximum(i - 1, 0), 0)),
                  pl.BlockSpec((nq, 512), lambda i: (i, 0)),
                  pl.BlockSpec((nh * nq, 2 * nq), lambda i: (0, 0)), vec, vec, vec,
                  pl.BlockSpec((nh * 64, d), lambda i: (0, 0))],
        out_specs=pl.BlockSpec((nq, d), lambda i: (i, 0)),
        out_shape=jax.ShapeDtypeStruct((t, d), F32),
        compiler_params=_params("arbitrary"),
    )(h, qp, kvd, kvd, biasm, sinks, qg, kg, w_o)


def attn_bwd(dy, qp, kvd, biasm, sinks, qg, kg, w_o, dep):
    t, d = dy.shape
    nq = ATT_WINDOW
    nb = t // nq
    nh = qp.shape[1] // ATT_HEAD_DIM

    def body(dy_ref, q_ref, kp_ref, kc_ref, bias_ref, s_ref, qg_ref, kg_ref, w_ref, dep_ref,
             dq_ref, dkv_ref, o_ref, dbias_ref, ds_ref, dqg_ref, dkg_ref, carry):
        i = pl.program_id(0)

        @pl.when(i == 0)
        def _():
            carry[...] = jnp.zeros_like(carry)
            dbias_ref[...] = jnp.zeros_like(dbias_ref)
            ds_ref[...] = jnp.zeros_like(ds_ref)
            dqg_ref[...] = jnp.zeros_like(dqg_ref)
            dkg_ref[...] = jnp.zeros_like(dkg_ref)

        @pl.when(i < nb)
        def _():
            fn = functools.partial(_attn_block, w_o=w_ref[...], first=(i == 0))
            (_, o), vjp = jax.vjp(fn, q_ref[...], kp_ref[...], kc_ref[...], bias_ref[...], s_ref[...],
                                  qg_ref[...], kg_ref[...])
            dq, dkp, dkc, dbias, dsk, dqg, dkg = vjp((dy_ref[...], jnp.zeros((nq, nh * 64), F32)))
            dq_ref[...] = dq
            o_ref[...] = _b(o)
            dkv_ref[...] = _fold64(carry[...] + dkp)
            carry[...] = dkc
            dbias_ref[...] += dbias
            ds_ref[...] += dsk
            dqg_ref[...] += _fold64(dqg)
            dkg_ref[...] += _fold64(dkg)

        @pl.when(i == nb)
        def _():
            dkv_ref[...] = _fold64(carry[...])

    cl = lambda i: jnp.minimum(i, nb - 1)
    vec = pl.BlockSpec((1, 128), lambda i: (0, 0))
    return pl.pallas_call(
        body, name="attn_bwd",
        grid=(nb + 1,),
        in_specs=[pl.BlockSpec((nq, d), lambda i: (cl(i), 0)), pl.BlockSpec((nq, nh * 64), lambda i: (cl(i), 0)),
                  pl.BlockSpec((nq, 512), lambda i: (jnp.maximum(cl(i) - 1, 0), 0)),
                  pl.BlockSpec((nq, 512), lambda i: (cl(i), 0)),
                  pl.BlockSpec((nh * nq, 2 * nq), lambda i: (0, 0)), vec, vec, vec,
                  pl.BlockSpec((nh * 64, d), lambda i: (0, 0)), DEP_SPEC],
        out_specs=[pl.BlockSpec((nq, nh * 64), lambda i: (cl(i), 0)),
                   pl.BlockSpec((nq, 512), lambda i: (jnp.maximum(i - 1, 0), 0)),
                   pl.BlockSpec((nq, nh * 64), lambda i: (cl(i), 0)),
                   pl.BlockSpec((nh * nq, 2 * nq), lambda i: (0, 0)), vec, vec, vec],
        out_shape=[jax.ShapeDtypeStruct((t, nh * 64), F32), jax.ShapeDtypeStruct((t, 512), F32),
                   jax.ShapeDtypeStruct((t, nh * 64), BF16), jax.ShapeDtypeStruct((nh * nq, 2 * nq), F32),
                   jax.ShapeDtypeStruct((1, 128), F32), jax.ShapeDtypeStruct((1, 128), F32),
                   jax.ShapeDtypeStruct((1, 128), F32)],
        scratch_shapes=[pltpu.VMEM((nq, 512), F32)],
        compiler_params=_params("arbitrary"),
    )(dy, qp, kvd, kvd, biasm, sinks, qg, kg, w_o, dep)


def _t5_buckets():
    nq = ATT_WINDOW
    dist = (np.arange(nq)[:, None] + nq) - np.arange(2 * nq)[None, :]
    n = np.maximum(dist, 0)
    max_exact = REL_BUCKETS // 2
    nf = np.maximum(n, 1).astype(np.float32)
    large = max_exact + (np.log(nf / max_exact) / math.log(ATT_WINDOW / max_exact)
                         * (REL_BUCKETS - max_exact)).astype(np.int32)
    large = np.minimum(large, REL_BUCKETS - 1)
    return np.where(n < max_exact, n, large).astype(np.int32)


def rel_bias_bwd(dbias, buckets):
    nh = dbias.shape[0]

    def body(db_ref, bk_ref, o_ref):
        bk = bk_ref[...]
        lane = _iota((1, 128), 1)
        row = _iota((REL_BUCKETS, 128), 0)
        acc = jnp.zeros((REL_BUCKETS, 128), F32)
        for h in range(nh):
            dbh = db_ref[h]
            for b in range(REL_BUCKETS):
                v = jnp.sum(jnp.where(bk == b, dbh, 0.0))
                acc = acc + jnp.where((row == b) & (lane == h), v, 0.0)
        o_ref[...] = acc

    return pl.pallas_call(
        body, name="rel_bias_bwd",
        out_shape=jax.ShapeDtypeStruct((REL_BUCKETS, 128), F32),
        compiler_params=_params(),
    )(dbias, buckets)


def loss_head(y, target):
    t, d = y.shape
    tm = _tile(t, 512)

    def body(y_ref, t_ref, l_ref, dy_ref):
        @pl.when(pl.program_id(0) == 0)
        def _():
            l_ref[...] = jnp.zeros_like(l_ref)

        e = y_ref[...] - t_ref[...]
        l_ref[...] += 0.5 * jnp.sum(jnp.mean(e * e, axis=-1, keepdims=True), axis=0, keepdims=True)
        dy_ref[...] = e * (1.0 / d)

    return pl.pallas_call(
        body, name="loss_head",
        grid=(t // tm,),
        in_specs=[pl.BlockSpec((tm, d), lambda i: (i, 0)), pl.BlockSpec((tm, d), lambda i: (i, 0))],
        out_specs=[pl.BlockSpec((1, 1), lambda i: (0, 0)), pl.BlockSpec((tm, d), lambda i: (i, 0))],
        out_shape=[jax.ShapeDtypeStruct((1, 1), F32), jax.ShapeDtypeStruct((t, d), F32)],
        compiler_params=_params("arbitrary"),
    )(y, target)


def adamw(w, g, m, v):
    r, c = w.shape
    tr = r if r <= 512 else _tile(r, 256)

    def body(w_ref, g_ref, m_ref, v_ref, d_ref, nm_ref, nv_ref):
        gg = g_ref[...]
        nm = ADAM_B1 * m_ref[...] + (1.0 - ADAM_B1) * gg
        nv = ADAM_B2 * v_ref[...] + (1.0 - ADAM_B2) * (gg * gg)
        m_hat = nm / (1.0 - ADAM_B1 ** ADAM_STEP)
        v_hat = nv / (1.0 - ADAM_B2 ** ADAM_STEP)
        d_ref[...] = -ADAM_LR * (m_hat / (jnp.sqrt(v_hat) + ADAM_EPS) + ADAM_WD * w_ref[...])
        nm_ref[...] = nm
        nv_ref[...] = nv

    spec = pl.BlockSpec((tr, c), lambda i: (i, 0))
    shp = jax.ShapeDtypeStruct((r, c), F32)
    return pl.pallas_call(
        body, name="adamw",
        grid=(r // tr,),
        in_specs=[spec] * 4, out_specs=[spec] * 3, out_shape=[shp] * 3,
        compiler_params=_params("arbitrary"),
    )(w, g, m, v)


def _my_pos():
    return lax.axis_index("x"), lax.axis_index("y"), lax.axis_index("c")


def _other_chips(x, y):
    return [(1 - x, y), (x, 1 - y), (1 - x, 1 - y)]


def _chip_id(x, y):
    return 2 * x + y


HBM_SPEC = pl.BlockSpec(memory_space=pltpu.HBM)
SEM_SPEC = pl.BlockSpec(memory_space=pltpu.SEMAPHORE)
DATAFLOW = pltpu.SideEffectType.DATAFLOW_SIDE_EFFECTING


def _in_hbm(a):
    return pltpu.with_memory_space_constraint(a, pltpu.HBM)


def _ici_gather_copy(buf, p, j, chip, c, to, send_sems, recv_sems):
    blk = buf.at[_chip_id(*chip), c]
    return pltpu.make_async_remote_copy(
        src_ref=blk, dst_ref=blk, send_sem=send_sems.at[3 * p + j], recv_sem=recv_sems.at[3 * p + j],
        device_id=to, device_id_type=MESH)


def gather_start(bufs, after, tag):
    n = len(bufs)

    def body(*refs):
        ins = refs[:n]
        send_sems, recv_sems = refs[n + 1], refs[n + 2]
        token = refs[2 * n + 3]
        x, y, c = _my_pos()
        for p in range(n):
            for j, chip in enumerate(_other_chips(x, y)):
                _ici_gather_copy(ins[p], p, j, (x, y), c, (*chip, c), send_sems, recv_sems).start()
        token[...] = jnp.zeros_like(token)

    out = pl.pallas_call(
        body, name="gather_start_" + tag,
        in_specs=[HBM_SPEC] * n + [DEP_SPEC],
        out_specs=(SEM_SPEC, SEM_SPEC, *([HBM_SPEC] * n), pl.BlockSpec(memory_space=pltpu.VMEM)),
        out_shape=(pltpu.SemaphoreType.DMA((3 * n,)), pltpu.SemaphoreType.DMA((3 * n,)),
                   *[pltpu.HBM(b.shape, b.dtype) for b in bufs], jax.ShapeDtypeStruct((8, 128), F32)),
        input_output_aliases={p: 2 + p for p in range(n)},
        compiler_params=pltpu.CompilerParams(has_side_effects=DATAFLOW),
    )(*[_in_hbm(b) for b in bufs], after)
    return out[0], out[1], list(out[2:2 + n]), out[2 + n]


def gather_wait(send_sems, recv_sems, bufs, after, tag):
    n = len(bufs)

    def body(*refs):
        ins = refs[:n]
        send_sems, recv_sems = refs[n], refs[n + 1]
        x, y, c = _my_pos()
        for p in range(n):
            for j, chip in enumerate(_other_chips(x, y)):
                _ici_gather_copy(ins[p], p, j, (x, y), c, (*chip, c), send_sems, recv_sems).wait_send()
                _ici_gather_copy(ins[p], p, j, chip, c, (x, y, c), send_sems, recv_sems).wait_recv()

    out = pl.pallas_call(
        body, name="gather_wait_" + tag,
        in_specs=[HBM_SPEC] * n + [SEM_SPEC, SEM_SPEC, pl.BlockSpec(memory_space=pl.ANY)],
        out_specs=[HBM_SPEC] * n,
        out_shape=[pltpu.HBM(b.shape, b.dtype) for b in bufs],
        input_output_aliases={p: p for p in range(n)},
        compiler_params=pltpu.CompilerParams(has_side_effects=DATAFLOW),
    )(*bufs, send_sems, recv_sems, after)
    return list(out)


def forward_to_sibling(bufs):
    n = len(bufs)

    def body(*refs):
        outs = refs[n:2 * n]
        send_sems, recv_sems = refs[2 * n:]
        x, y, c = _my_pos()
        chips = _other_chips(x, y)
        sent = []
        for p in range(n):
            for j, chip in enumerate(chips):
                cp = _ici_gather_copy(outs[p], p, j, chip, c, (x, y, 1 - c), send_sems, recv_sems)
                cp.start()
                sent.append(cp)
        for p in range(n):
            for j, chip in enumerate(chips):
                _ici_gather_copy(outs[p], p, j, chip, 1 - c, (x, y, c), send_sems, recv_sems).wait_recv()
        for cp in sent:
            cp.wait_send()

    any_spec = pl.BlockSpec(memory_space=pl.ANY)
    return pl.pallas_call(
        body, name="forward_to_sibling",
        in_specs=[any_spec] * n, out_specs=[any_spec] * n,
        out_shape=[jax.ShapeDtypeStruct(b.shape, b.dtype) for b in bufs],
        scratch_shapes=[pltpu.SemaphoreType.DMA((3 * n,)), pltpu.SemaphoreType.DMA((3 * n,))],
        input_output_aliases={p: p for p in range(n)},
    )(*bufs)


def allreduce_small(v):
    r, c = v.shape

    def body(v_ref, o_ref, buf, send_sems, recv_sems):
        x, y, cc = _my_pos()
        me = 4 * x + 2 * y + cc
        buf[me] = v_ref[...]
        copies = []
        for k in range(1, 8):
            dx, dy, dc = (k >> 2) & 1, (k >> 1) & 1, k & 1
            peer = (x ^ dx, y ^ dy, cc ^ dc)
            cp = pltpu.make_async_remote_copy(
                src_ref=v_ref, dst_ref=buf.at[me], send_sem=send_sems.at[k - 1], recv_sem=recv_sems.at[k - 1],
                device_id=peer, device_id_type=MESH)
            cp.start()
            copies.append(cp)
        for cp in copies:
            cp.wait_recv()
        for cp in copies:
            cp.wait_send()
        acc = buf[0]
        for k in range(1, 8):
            acc = acc + buf[k]
        o_ref[...] = acc

    vm = pl.BlockSpec(memory_space=pltpu.VMEM)
    return pl.pallas_call(
        body, name="allreduce_small",
        in_specs=[vm], out_specs=vm,
        out_shape=jax.ShapeDtypeStruct((r, c), F32),
        scratch_shapes=[pltpu.VMEM((8, r, c), F32), pltpu.SemaphoreType.DMA((7,)), pltpu.SemaphoreType.DMA((7,))],
    )(v)


def _sibling_half_copy(grad, land, p, c, sibling, send_sems, recv_sems):
    return pltpu.make_async_remote_copy(
        src_ref=grad.at[:, 1 - c], dst_ref=land, send_sem=send_sems.at[p], recv_sem=recv_sems.at[p],
        device_id=sibling, device_id_type=MESH)


def sibling_halves_start(grads, tag):
    n = len(grads)
    lands = [lax.empty((g.shape[0],) + g.shape[2:], g.dtype) for g in grads]

    def body(*refs):
        ins, land = refs[:n], refs[n:2 * n]
        send_sems, recv_sems = refs[2 * n], refs[2 * n + 1]
        token = refs[4 * n + 2]
        x, y, c = _my_pos()
        for p in range(n):
            _sibling_half_copy(ins[p], land[p], p, c, (x, y, 1 - c), send_sems, recv_sems).start()
        token[...] = jnp.zeros_like(token)

    out = pl.pallas_call(
        body, name="sibling_halves_start_" + tag,
        in_specs=[HBM_SPEC] * (2 * n),
        out_specs=(SEM_SPEC, SEM_SPEC, *([HBM_SPEC] * (2 * n)), pl.BlockSpec(memory_space=pltpu.VMEM)),
        out_shape=(pltpu.SemaphoreType.DMA((n,)), pltpu.SemaphoreType.DMA((n,)),
                   *[pltpu.HBM(a.shape, a.dtype) for a in grads + lands], jax.ShapeDtypeStruct((8, 128), F32)),
        input_output_aliases={i: 2 + i for i in range(2 * n)},
        compiler_params=pltpu.CompilerParams(has_side_effects=DATAFLOW),
    )(*[_in_hbm(a) for a in grads + lands])
    return out[0], out[1], list(out[2:2 + n]), list(out[2 + n:2 + 2 * n]), out[2 + 2 * n]


def sibling_halves_wait(send_sems, recv_sems, grads, lands, after, tag):
    n = len(grads)

    def body(*refs):
        ins, land = refs[:n], refs[n:2 * n]
        send_sems, recv_sems = refs[2 * n], refs[2 * n + 1]
        x, y, c = _my_pos()
        for p in range(n):
            cp = _sibling_half_copy(ins[p], land[p], p, c, (x, y, 1 - c), send_sems, recv_sems)
            cp.wait_send()
            cp.wait_recv()

    out = pl.pallas_call(
        body, name="sibling_halves_wait_" + tag,
        in_specs=[HBM_SPEC] * (2 * n) + [SEM_SPEC, SEM_SPEC, pl.BlockSpec(memory_space=pl.ANY)],
        out_specs=[HBM_SPEC] * (2 * n),
        out_shape=[pltpu.HBM(a.shape, a.dtype) for a in grads + lands],
        input_output_aliases={i: i for i in range(2 * n)},
        compiler_params=pltpu.CompilerParams(has_side_effects=DATAFLOW),
    )(*grads, *lands, send_sems, recv_sems, after)
    return list(out[:n]), list(out[n:])


def _chip_partial_copy(part, land, p, j, chip, c, send_sems, recv_sems):
    return pltpu.make_async_remote_copy(
        src_ref=part.at[_chip_id(*chip)], dst_ref=land.at[j], send_sem=send_sems.at[3 * p + j],
        recv_sem=recv_sems.at[3 * p + j], device_id=(*chip, c), device_id_type=MESH)


def chip_partials_start(parts, tag):
    n = len(parts)
    lands = [lax.empty((N_CHIPS - 1,) + s.shape[1:], s.dtype) for s in parts]

    def body(*refs):
        ins, land = refs[:n], refs[n:2 * n]
        send_sems, recv_sems = refs[2 * n], refs[2 * n + 1]
        token = refs[4 * n + 2]
        x, y, c = _my_pos()
        for p in range(n):
            for j, chip in enumerate(_other_chips(x, y)):
                _chip_partial_copy(ins[p], land[p], p, j, chip, c, send_sems, recv_sems).start()
        token[...] = jnp.zeros_like(token)

    out = pl.pallas_call(
        body, name="chip_partials_start_" + tag,
        in_specs=[HBM_SPEC] * (2 * n),
        out_specs=(SEM_SPEC, SEM_SPEC, *([HBM_SPEC] * (2 * n)), pl.BlockSpec(memory_space=pltpu.VMEM)),
        out_shape=(pltpu.SemaphoreType.DMA((3 * n,)), pltpu.SemaphoreType.DMA((3 * n,)),
                   *[pltpu.HBM(a.shape, a.dtype) for a in parts + lands], jax.ShapeDtypeStruct((8, 128), F32)),
        input_output_aliases={i: 2 + i for i in range(2 * n)},
        compiler_params=pltpu.CompilerParams(has_side_effects=DATAFLOW),
    )(*[_in_hbm(a) for a in parts + lands])
    return out[0], out[1], list(out[2:2 + n]), list(out[2 + n:2 + 2 * n]), out[2 + 2 * n]


def chip_partials_wait(send_sems, recv_sems, parts, lands, after, tag):
    n = len(parts)

    def body(*refs):
        ins, land = refs[:n], refs[n:2 * n]
        send_sems, recv_sems = refs[2 * n], refs[2 * n + 1]
        x, y, c = _my_pos()
        for p in range(n):
            for j, chip in enumerate(_other_chips(x, y)):
                cp = _chip_partial_copy(ins[p], land[p], p, j, chip, c, send_sems, recv_sems)
                cp.wait_send()
                cp.wait_recv()

    out = pl.pallas_call(
        body, name="chip_partials_wait_" + tag,
        in_specs=[HBM_SPEC] * (2 * n) + [SEM_SPEC, SEM_SPEC, pl.BlockSpec(memory_space=pl.ANY)],
        out_specs=[HBM_SPEC] * (2 * n),
        out_shape=[pltpu.HBM(a.shape, a.dtype) for a in parts + lands],
        input_output_aliases={i: i for i in range(2 * n)},
        compiler_params=pltpu.CompilerParams(has_side_effects=DATAFLOW),
    )(*parts, *lands, send_sems, recv_sems, after)
    return list(out[n:])


def share_with_sibling(bufs):
    n = len(bufs)

    def body(*refs):
        outs = refs[n:2 * n]
        send_sems, recv_sems = refs[2 * n:]
        x, y, c = _my_pos()
        copies = []
        for p in range(n):
            cp = pltpu.make_async_remote_copy(
                src_ref=outs[p].at[c], dst_ref=outs[p].at[c], send_sem=send_sems.at[p], recv_sem=recv_sems.at[p],
                device_id=(x, y, 1 - c), device_id_type=MESH)
            cp.start()
            copies.append(cp)
        for p in range(n):
            pltpu.make_async_remote_copy(
                src_ref=outs[p].at[1 - c], dst_ref=outs[p].at[1 - c], send_sem=send_sems.at[p],
                recv_sem=recv_sems.at[p], device_id=(x, y, 1 - c), device_id_type=MESH).wait_recv()
        for cp in copies:
            cp.wait_send()

    any_spec = pl.BlockSpec(memory_space=pl.ANY)
    return pl.pallas_call(
        body, name="share_with_sibling",
        in_specs=[any_spec] * n, out_specs=[any_spec] * n,
        out_shape=[jax.ShapeDtypeStruct(b.shape, b.dtype) for b in bufs],
        scratch_shapes=[pltpu.SemaphoreType.DMA((n,)), pltpu.SemaphoreType.DMA((n,))],
        input_output_aliases={p: p for p in range(n)},
    )(*bufs)


def add_sibling(g, recv, half):
    _, _, r, c = g.shape
    tr = _tile(r, 256) if r % 256 == 0 else r

    def body(half_ref, g_ref, r_ref, o32_ref, o16_ref):
        s = g_ref[...] + r_ref[...]
        o32_ref[...] = s
        o16_ref[...] = _b(s)

    return pl.pallas_call(
        body, name="add_sibling",
        grid_spec=pltpu.PrefetchScalarGridSpec(
            num_scalar_prefetch=1, grid=(N_CHIPS, r // tr),
            in_specs=[pl.BlockSpec((None, None, tr, c), lambda k, i, hf: (k, hf[0], i, 0)),
                      pl.BlockSpec((None, tr, c), lambda k, i, hf: (k, i, 0))],
            out_specs=[pl.BlockSpec((None, tr, c), lambda k, i, hf: (k, i, 0)),
                       pl.BlockSpec((None, tr, c), lambda k, i, hf: (k, i, 0))]),
        out_shape=[jax.ShapeDtypeStruct((N_CHIPS, r, c), F32), jax.ShapeDtypeStruct((N_CHIPS, r, c), BF16)],
        compiler_params=_params("arbitrary", "arbitrary"),
    )(half, g, recv)


def add_chip_partials(p32, recv, pos):
    _, r, c = p32.shape
    tr = _tile(r, 256) if r % 256 == 0 else r

    def body(pos_ref, p_ref, r_ref, o_ref):
        acc = p_ref[...]
        for j in range(N_CHIPS - 1):
            acc = acc + r_ref[j].astype(F32)
        o_ref[...] = acc

    return pl.pallas_call(
        body, name="add_chip_partials",
        grid_spec=pltpu.PrefetchScalarGridSpec(
            num_scalar_prefetch=1, grid=(r // tr,),
            in_specs=[pl.BlockSpec((None, tr, c), lambda i, ps: (ps[0], i, 0)),
                      pl.BlockSpec((N_CHIPS - 1, tr, c), lambda i, ps: (0, i, 0))],
            out_specs=pl.BlockSpec((None, tr, c), lambda i, ps: (ps[1], i, 0))),
        out_shape=jax.ShapeDtypeStruct((2, r, c), F32),
        compiler_params=_params("arbitrary"),
    )(pos, p32, recv)


def cast_into_gather(w, pos, dep, row0=0, nrows=None):
    c = w.shape[1]
    nrows = w.shape[0] if nrows is None else nrows
    r = nrows // 2
    common = math.gcd(r, row0) if row0 else r
    tr = max(w for w in range(16, min(common, 512) + 1, 16) if common % w == 0)
    nt = r // tr

    def body(pos_ref, w_ref, dep_ref, o_ref):
        o_ref[...] = _b(w_ref[...])

    return pl.pallas_call(
        body, name="cast_into_gather",
        grid_spec=pltpu.PrefetchScalarGridSpec(
            num_scalar_prefetch=1, grid=(2, nt),
            in_specs=[pl.BlockSpec((tr, c), lambda hf, i, ps: (row0 // tr + hf * nt + i, 0)), DEP_SPEC],
            out_specs=pl.BlockSpec((None, None, tr, c), lambda hf, i, ps: (ps[0], hf, i, 0))),
        out_shape=jax.ShapeDtypeStruct((N_CHIPS, 2, r, c), BF16),
        compiler_params=_params("arbitrary", "arbitrary"),
    )(pos, w, dep)


def build_bias(rel, buckets):
    nb, nh = rel.shape

    def body(rel_ref, bk_ref, o_ref):
        bk = bk_ref[...]
        for h in range(nh):
            acc = jnp.zeros(bk.shape, F32)
            for b in range(nb):
                acc = jnp.where(bk == b, rel_ref[b, h], acc)
            o_ref[h] = acc

    return pl.pallas_call(
        body, name="build_bias",
        in_specs=[pl.BlockSpec(memory_space=pltpu.SMEM), pl.BlockSpec(memory_space=pltpu.VMEM)],
        out_specs=pl.BlockSpec(memory_space=pltpu.VMEM),
        out_shape=jax.ShapeDtypeStruct((nh,) + buckets.shape, F32),
        compiler_params=_params(),
    )(rel, buckets)


SMALL_ROWS = 256


def kernel(x, ffn_norm, ffn_w1, ffn_w3, ffn_w2, ssm_norm, ssm_w_in, ssm_conv_w, ssm_conv_b, ssm_dt_bias, ssm_a_log, ssm_d, ssm_gate_norm, ssm_w_out, kv_norm, w_kv, k_norm, attn_norm, w_q, q_norm, sinks, w_o, rel_bias, loss_target, m_ffn_norm, m_ffn_w1, m_ffn_w3, m_ffn_w2, m_ssm_norm, m_ssm_w_in, m_ssm_conv_w, m_ssm_conv_b, m_ssm_dt_bias, m_ssm_a_log, m_ssm_d, m_ssm_gate_norm, m_ssm_w_out, m_kv_norm, m_w_kv, m_k_norm, m_attn_norm, m_w_q, m_q_norm, m_sinks, m_w_o, m_rel_bias, v_ffn_norm, v_ffn_w1, v_ffn_w3, v_ffn_w2, v_ssm_norm, v_ssm_w_in, v_ssm_conv_w, v_ssm_conv_b, v_ssm_dt_bias, v_ssm_a_log, v_ssm_d, v_ssm_gate_norm, v_ssm_w_out, v_kv_norm, v_w_kv, v_k_norm, v_attn_norm, v_w_q, v_q_norm, v_sinks, v_w_o, v_rel_bias):
    weights = dict(ffn_norm=ffn_norm, ffn_w1=ffn_w1, ffn_w3=ffn_w3, ffn_w2=ffn_w2, ssm_norm=ssm_norm,
                   ssm_w_in=ssm_w_in, ssm_conv_w=ssm_conv_w, ssm_conv_b=ssm_conv_b, ssm_dt_bias=ssm_dt_bias,
                   ssm_a_log=ssm_a_log, ssm_d=ssm_d, ssm_gate_norm=ssm_gate_norm, ssm_w_out=ssm_w_out,
                   kv_norm=kv_norm, w_kv=w_kv, k_norm=k_norm, attn_norm=attn_norm, w_q=w_q, q_norm=q_norm,
                   sinks=sinks, w_o=w_o, rel_bias=rel_bias)
    m_in = dict(ffn_norm=m_ffn_norm, ffn_w1=m_ffn_w1, ffn_w3=m_ffn_w3, ffn_w2=m_ffn_w2, ssm_norm=m_ssm_norm,
                ssm_w_in=m_ssm_w_in, ssm_conv_w=m_ssm_conv_w, ssm_conv_b=m_ssm_conv_b, ssm_dt_bias=m_ssm_dt_bias,
                ssm_a_log=m_ssm_a_log, ssm_d=m_ssm_d, ssm_gate_norm=m_ssm_gate_norm, ssm_w_out=m_ssm_w_out,
                kv_norm=m_kv_norm, w_kv=m_w_kv, k_norm=m_k_norm, attn_norm=m_attn_norm, w_q=m_w_q, q_norm=m_q_norm,
                sinks=m_sinks, w_o=m_w_o, rel_bias=m_rel_bias)
    v_in = dict(ffn_norm=v_ffn_norm, ffn_w1=v_ffn_w1, ffn_w3=v_ffn_w3, ffn_w2=v_ffn_w2, ssm_norm=v_ssm_norm,
                ssm_w_in=v_ssm_w_in, ssm_conv_w=v_ssm_conv_w, ssm_conv_b=v_ssm_conv_b, ssm_dt_bias=v_ssm_dt_bias,
                ssm_a_log=v_ssm_a_log, ssm_d=v_ssm_d, ssm_gate_norm=v_ssm_gate_norm, ssm_w_out=v_ssm_w_out,
                kv_norm=v_kv_norm, w_kv=v_w_kv, k_norm=v_k_norm, attn_norm=v_attn_norm, w_q=v_w_q, q_norm=v_q_norm,
                sinks=v_sinks, w_o=v_w_o, rel_bias=v_rel_bias)
    return _step(x[0], loss_target[0], weights, m_in, v_in)


BIG = ("ffn_w1", "ffn_w3", "ffn_w2", "ssm_w_in", "ssm_w_out", "w_kv", "w_q", "w_o")
SMALL = (("ffn_norm", True), ("ssm_norm", True), ("ssm_conv_w", True), ("ssm_conv_b", True),
         ("ssm_gate_norm", True), ("ssm_dt_bias", False), ("ssm_a_log", False), ("ssm_d", False),
         ("kv_norm", False), ("k_norm", False), ("attn_norm", False), ("q_norm", False), ("sinks", False),
         ("rel_bias", False))


FFN_W = BIG[:3]


def _small_layout(weights):
    off, table = 0, {}
    for name, sharded in SMALL:
        shape = weights[name].shape
        full = shape[:-1] + (shape[-1] * N_CHIPS,) if sharded else shape
        n = int(np.prod(full))
        table[name] = (off, full, sharded)
        off += n
    assert off <= SMALL_ROWS * 128
    return table


def _place_small(values, table, chip, scale_mask):
    flat = jnp.zeros((SMALL_ROWS * 128,), F32)
    for name, (off, full, sharded) in table.items():
        if not sharded:
            continue
        v = values[name].astype(F32)
        lead = int(np.prod(full[:-1]))
        w = v.shape[-1]
        blk = jnp.zeros((lead, full[-1]), F32)
        blk = lax.dynamic_update_slice(blk, v.reshape(lead, w) * scale_mask, (0, chip * w))
        flat = lax.dynamic_update_slice(flat, blk.reshape(-1), (off,))
    return flat.reshape(SMALL_ROWS, 128)


def _take_small(mat, table, name):
    off, full, _ = table[name]
    n = int(np.prod(full))
    return mat.reshape(-1)[off:off + n].reshape(full)


def _step(x, target, weights, m_in, v_in):
    t, d = x.shape
    xi, yi, ci = lax.axis_index("x"), lax.axis_index("y"), lax.axis_index("c")
    chip = 2 * xi + yi
    pos_arr = jnp.stack([chip, ci]).astype(jnp.int32)
    half_arr = jnp.reshape(ci, (1,)).astype(jnp.int32)

    fs = weights["ffn_w1"].shape[-1]
    ffn_rows = {"ffn_w1": d, "ffn_w3": d, "ffn_w2": fs}
    w2d = {n: weights[n].reshape(-1, weights[n].shape[-1]) for n in BIG}
    mamba_w = ("ssm_w_in", "ssm_w_out")
    late_w = ("w_kv", "w_q", "w_o")
    fs_, fr_, fbufs, tok_f = gather_start(
        [cast_into_gather(w2d[n], pos_arr, pos_arr, 0, ffn_rows[n]) for n in FFN_W], pos_arr, "first")
    ms, mr, mbufs, tok_m = gather_start([cast_into_gather(w2d[n], pos_arr, tok_f) for n in mamba_w], tok_f, "mamba")
    ls, lr, lbufs, tok_l = gather_start(
        [cast_into_gather(w2d[n], pos_arr, tok_f, ffn_rows[n], 3 * ffn_rows[n]) for n in FFN_W]
        + [cast_into_gather(w2d[n], pos_arr, tok_f) for n in late_w], tok_m, "late")
    first = forward_to_sibling(gather_wait(fs_, fr_, fbufs, tok_l, "first"))
    no_dep = jnp.zeros((8, 128), F32)
    table = _small_layout(weights)
    south = (ci == 0).astype(F32)
    small = allreduce_small(_place_small(weights, table, chip, south))
    sp = {n: _take_small(small, table, n) if sh else weights[n] for n, sh in SMALL}

    ffn_first = [first[0].reshape(N_CHIPS, 1, d, fs), first[1].reshape(N_CHIPS, 1, d, fs),
                 first[2].reshape(N_CHIPS, 1, fs, d)]
    ffn_g = sp["ffn_norm"]
    h0 = x
    h1, a00, b00 = ffn_fwd(h0, ffn_g[0, 0].reshape(1, d), *ffn_first, 0, no_dep)
    gathered = dict(zip(mamba_w, forward_to_sibling(gather_wait(ms, mr, mbufs, h1, "mamba"))))
    n_in = weights["ssm_w_in"].shape[-1] * N_CHIPS
    di = weights["ssm_w_out"].shape[1] * N_CHIPS
    nheads = di // SSM_HEAD_DIM
    conv_dim = n_in - di - nheads
    w_in_full = jnp.moveaxis(gathered["ssm_w_in"].reshape(N_CHIPS, d, n_in // N_CHIPS), 0, 1).reshape(d, n_in)
    hpg = nheads // SSM_GROUPS

    def spread_heads(v):
        lead = v.shape[:-1]
        v = v.reshape(lead + (SSM_GROUPS, hpg))
        v = jnp.pad(v, [(0, 0)] * len(lead) + [(0, 0), (0, 128 - hpg)])
        return v.reshape(lead + (SSM_GROUPS * 128,))

    def gather_heads(v):
        lead = v.shape[:-1]
        return v.reshape(lead + (SSM_GROUPS, 128))[..., :hpg].reshape(lead + (nheads,))

    dt_col0 = di + conv_dim
    n_zx = dt_col0 + SSM_GROUPS * 128
    w_in = jnp.concatenate([w_in_full[:, :dt_col0], spread_heads(w_in_full[:, dt_col0:])], axis=1)
    w_out = gathered["ssm_w_out"].reshape(di, d)
    nkv = weights["w_kv"].shape[1] // (2 * ATT_HEAD_DIM)
    assert nkv == 2
    nh = weights["w_q"].shape[-1] // ATT_HEAD_DIM

    ssm_g = sp["ssm_norm"].reshape(1, d)
    cw = jnp.pad(sp["ssm_conv_w"].reshape(SSM_CONV, conv_dim), [(0, 8 - SSM_CONV), (0, 0)])
    cb = sp["ssm_conv_b"].reshape(1, conv_dim)
    gate_g = sp["ssm_gate_norm"].reshape(1, di)
    dt_bias = spread_heads(sp["ssm_dt_bias"].reshape(1, nheads))
    a_log = spread_heads(sp["ssm_a_log"].reshape(1, nheads))
    d_skip = spread_heads(sp["ssm_d"].reshape(1, nheads))
    kv_g = sp["kv_norm"].reshape(1, d)
    k_g = jnp.tile(sp["k_norm"].reshape(1, ATT_HEAD_DIM), (1, 2))
    attn_g = sp["attn_norm"].reshape(1, d)
    q_g = jnp.tile(sp["q_norm"].reshape(1, ATT_HEAD_DIM), (1, 2))
    sink_row = jnp.pad(sp["sinks"].reshape(1, nh), [(0, 0), (0, 128 - nh)])
    buckets = jnp.asarray(_t5_buckets())
    biasm = build_bias(sp["rel_bias"], buckets).reshape(nh * ATT_WINDOW, 2 * ATT_WINDOW)

    zx = norm_mm(h1, ssm_g, w_in)
    xc = conv_fwd(zx, cw, cb, di)
    y_ssd, states = ssd_fwd(xc, zx, dt_bias, a_log, d_skip, dt_col0)
    h2 = gate_out_fwd(h1, y_ssd, zx, gate_g, w_out)

    late = forward_to_sibling(gather_wait(ls, lr, lbufs, h2, "late"))
    ffn_rest = [late[0].reshape(N_CHIPS, 3, d, fs), late[1].reshape(N_CHIPS, 3, d, fs),
                late[2].reshape(N_CHIPS, 3, fs, d)]
    gathered.update(zip(late_w, late[3:]))
    wkv_heads = gathered["w_kv"].reshape(d, 2 * nkv, 1, ATT_HEAD_DIM)
    w_kvd = jnp.broadcast_to(wkv_heads, (d, 2 * nkv, 2, ATT_HEAD_DIM)).reshape(d, 4 * nkv * ATT_HEAD_DIM)
    wq = gathered["w_q"].reshape(d, -1)
    wo = gathered["w_o"].reshape(-1, d)

    def ffn_w(layer, idx):
        blk = 2 * layer + idx
        return (*ffn_first, 0) if blk == 0 else (*ffn_rest, blk - 1)

    h3, a01, b01 = ffn_fwd(h2, ffn_g[0, 1].reshape(1, d), *ffn_w(0, 1), no_dep)
    kvd = norm_mm(h3, kv_g, w_kvd)
    h4, a10, b10 = ffn_fwd(h3, ffn_g[1, 0].reshape(1, d), *ffn_w(1, 0), no_dep)
    qp = norm_mm(h4, attn_g, wq)
    h5 = attn_fwd(h4, qp, kvd, biasm, sink_row, q_g, k_g, wo)
    h6, a11, b11 = ffn_fwd(h5, ffn_g[1, 1].reshape(1, d), *ffn_w(1, 1), no_dep)
    loss_part, d6 = loss_head(h6, target)
    loss = lax.psum(loss_part[0, 0], ("x", "y", "c"))

    gfn = [[None, None], [None, None]]

    pending = {}

    def swap_start(pieces, tag):
        views = [g.reshape(N_CHIPS, 2, g.shape[1] // 2, g.shape[2]) for _, g in pieces]
        ss, rs, views, lands, token = sibling_halves_start(views, tag)
        pending[tag] = dict(keys=[k for k, _ in pieces], swap=(ss, rs, views, lands))
        return token

    def partials_start(tag, after):
        views, recv1 = sibling_halves_wait(*pending[tag]["swap"], after, tag)
        p32, p16 = zip(*[add_sibling(g, r, half_arr) for g, r in zip(views, recv1)])
        ss, rs, parts, lands, token = chip_partials_start(list(p16), tag)
        pending[tag].update(p32=p32, partials=(ss, rs, parts, lands))
        return token

    def ffn_back(h_in, dy, a_s, b_s, layer, idx, dep, wdep):
        dh, u, da, db, s, dg = ffn_bwd(h_in, dy, ffn_g[layer, idx].reshape(1, d), a_s, b_s, *ffn_w(layer, idx), dep)
        gfn[layer][idx] = dg
        return dh, [(("ffn_w1", layer, idx), wgrad_grouped_b(u, da, wdep)),
                    (("ffn_w3", layer, idx), wgrad_grouped_b(u, db, no_dep)),
                    (("ffn_w2", layer, idx), wgrad_grouped_a(s, dy, 0.5))]

    d5, pieces = ffn_back(h5, d6, a11, b11, 1, 1, no_dep, no_dep)
    tok = swap_start(pieces, "ffn11")
    dqp, dkvd, o16, dbiasm, dsinks, dqg, dkg = attn_bwd(d5, qp, kvd, biasm, sink_row, q_g, k_g, wo, tok)
    tok = partials_start("ffn11", dqp)
    g_wo = wgrad(o16, d5)
    d4, u_q, g_attn_norm = norm_mm_bwd(h4, attn_g, wq, dqp, d5, tok)
    g_wq = wgrad(u_q, dqp)
    d3a, pieces = ffn_back(h3, d4, a10, b10, 1, 0, no_dep, no_dep)
    pieces += [(("w_o",), g_wo.reshape(N_CHIPS, -1, d)), (("w_q",), g_wq.reshape(N_CHIPS, d // N_CHIPS, -1))]
    tok = swap_start(pieces, "ffn10")
    d3, u_kv, g_kv_norm = norm_mm_bwd(h3, kv_g, w_kvd, dkvd, d3a, tok, 0.5)
    tok = partials_start("ffn10", d3)
    g_wkvd = wgrad(u_kv, dkvd)
    g_wkv = g_wkvd.reshape(d, 2 * nkv, 2, ATT_HEAD_DIM)[:, :, 0, :].reshape(d, 2 * nkv * ATT_HEAD_DIM)
    d2, pieces = ffn_back(h2, d3, a01, b01, 0, 1, tok, no_dep)
    pieces += [(("w_kv",), g_wkv.reshape(N_CHIPS, d // N_CHIPS, -1))]
    tok = swap_start(pieces, "ffn01")
    dzx, dy_ssd, yn16, g_gate = gate_out_bwd(d2, y_ssd, zx, gate_g, w_out, n_zx, tok)
    tok = partials_start("ffn01", dy_ssd)
    g_wout = wgrad(yn16, d2)
    dzx, dxs, dbm, dcm, g_dtb, g_alog, g_dsk = ssd_bwd(dzx, dy_ssd, xc, zx, states, dt_bias, a_log, d_skip, dt_col0)
    dzx, g_cw, g_cb = conv_bwd(dzx, zx, dxs, dbm, dcm, cw, cb, di)
    d1, u_in, g_ssm_norm = norm_mm_bwd(h1, ssm_g, w_in, dzx, d2, tok)
    g_win = wgrad(u_in, dzx)
    g_win_full = jnp.concatenate([g_win[:, :dt_col0], gather_heads(g_win[:, dt_col0:])], axis=1)
    pieces = [(("ssm_w_in",), jnp.moveaxis(g_win_full.reshape(d, N_CHIPS, n_in // N_CHIPS), 1, 0)),
              (("ssm_w_out",), g_wout.reshape(N_CHIPS, di // N_CHIPS, d))]
    tok = swap_start(pieces, "mamba")
    grad_x, u0, da0, db0, s0, gfn[0][0] = ffn_bwd(h0, d1, ffn_g[0, 0].reshape(1, d), a00, b00, *ffn_w(0, 0), tok)
    tok = partials_start("mamba", grad_x)
    pieces = [(("ffn_w1", 0, 0), wgrad_grouped_b(u0, da0, tok)), (("ffn_w3", 0, 0), wgrad_grouped_b(u0, db0, no_dep)),
              (("ffn_w2", 0, 0), wgrad_grouped_a(s0, d1, 0.5))]
    tok = swap_start(pieces, "ffn00")
    g_relb = rel_bias_bwd(dbiasm.reshape(nh, ATT_WINDOW, 2 * ATT_WINDOW), buckets)

    reduced = {}

    def finish(tag, after):
        st = pending[tag]
        lands = chip_partials_wait(*st["partials"], after, tag)
        for k, p, r in zip(st["keys"], st["p32"], lands):
            reduced[k] = add_chip_partials(p, r, pos_arr)
        return reduced[st["keys"][-1]]

    last = finish("ffn10", finish("ffn11", tok))
    tok = partials_start("ffn00", last)
    last = finish("mamba", finish("ffn01", tok))
    finish("ffn00", last)
    keys = list(reduced)
    shared = dict(zip(keys, share_with_sibling([reduced[k] for k in keys])))
    grads = {}
    for n in FFN_W:
        blocks = [shared[(n, l, i)].reshape(1, ffn_rows[n], -1) for l in range(2) for i in range(2)]
        grads[n] = jnp.concatenate(blocks, axis=0).reshape(weights[n].shape)
    for n in BIG[3:]:
        grads[n] = shared[(n,)].reshape(weights[n].shape)

    small_grads = {
        "ffn_norm": jnp.stack([jnp.stack([gfn[l][i].reshape(d) for i in range(2)]) for l in range(2)]),
        "ssm_norm": g_ssm_norm.reshape(1, d),
        "ssm_conv_w": g_cw[:SSM_CONV].reshape(1, SSM_CONV, conv_dim),
        "ssm_conv_b": g_cb.reshape(1, conv_dim),
        "ssm_gate_norm": g_gate.reshape(1, di),
        "ssm_dt_bias": gather_heads(g_dtb.reshape(1, -1)), "ssm_a_log": gather_heads(g_alog.reshape(1, -1)),
        "ssm_d": gather_heads(g_dsk.reshape(1, -1)),
        "kv_norm": g_kv_norm.reshape(d), "k_norm": dkg[0, :ATT_HEAD_DIM], "attn_norm": g_attn_norm.reshape(1, d),
        "q_norm": dqg[:, :ATT_HEAD_DIM], "sinks": dsinks[:, :nh], "rel_bias": g_relb[:, :nh],
    }
    flat = jnp.zeros((SMALL_ROWS * 128,), F32)
    for name, (off, fshape, _) in table.items():
        flat = lax.dynamic_update_slice(flat, small_grads[name].astype(F32).reshape(-1), (off,))
    small_sum = allreduce_small(flat.reshape(SMALL_ROWS, 128))
    for name, (off, fshape, sharded) in table.items():
        g = _take_small(small_sum, table, name)
        if sharded:
            w = weights[name].shape[-1]
            lead = int(np.prod(fshape[:-1]))
            g = lax.dynamic_slice(g.reshape(lead, fshape[-1]), (0, chip * w), (lead, w)).reshape(weights[name].shape)
        grads[name] = g.reshape(weights[name].shape)

    names = list(weights)
    deltas, new_m, new_v = {}, {}, {}
    small_names = [n for n, _ in SMALL]
    for n in BIG:
        shp = weights[n].shape
        v2 = lambda a: a.reshape(-1, shp[-1])
        dl, nm, nv = adamw(v2(weights[n]), v2(grads[n]), v2(m_in[n]), v2(v_in[n]))
        deltas[n], new_m[n], new_v[n] = dl.reshape(shp), nm.reshape(shp), nv.reshape(shp)
    sizes = [int(np.prod(weights[n].shape)) for n in small_names]
    tot = sum(sizes)
    rows = -(-tot // 128)
    rows = -(-rows // 8) * 8

    def pack(dct):
        flat = jnp.concatenate([dct[n].reshape(-1) for n in small_names])
        return jnp.pad(flat, (0, rows * 128 - tot), constant_values=1.0).reshape(rows, 128)

    dl, nm, nv = adamw(pack(weights), pack(grads), pack(m_in), pack(v_in))
    off = 0
    for n, sz in zip(small_names, sizes):
        shp = weights[n].shape
        take = lambda a: a.reshape(-1)[off:off + sz].reshape(shp)
        deltas[n], new_m[n], new_v[n] = take(dl), take(nm), take(nv)
        off += sz

    return (loss, grad_x[None], *[grads[n] for n in names], *[deltas[n] for n in names],
            *[new_m[n] for n in names], *[new_v[n] for n in names])
```

```python
import functools
import math

import jax
import jax.numpy as jnp
import numpy as np
from jax import lax
from jax.experimental import pallas as pl
from jax.experimental.pallas import tpu as pltpu

F32 = jnp.float32
BF16 = jnp.bfloat16
EPS = 1e-6
MESH = pl.DeviceIdType.MESH

SSM_HEAD_DIM = 64
SSM_GROUPS = 4
SSM_STATE = 128
SSM_CONV = 4
SSM_CHUNK = 256
ATT_HEAD_DIM = 64
ATT_WINDOW = 128
REL_BUCKETS = 32
N_CHIPS = 4

ADAM_LR = 0.001
ADAM_B1 = 0.9
ADAM_B2 = 0.999
ADAM_EPS = 1e-08
ADAM_WD = 0.01
ADAM_STEP = 10

VMEM_LIMIT_BYTES = 56 * 1024 * 1024
NEG = -1e30


DEP_SPEC = pl.BlockSpec(memory_space=pl.ANY)


def _params(*sem):
    return pltpu.CompilerParams(dimension_semantics=sem if sem else None, vmem_limit_bytes=VMEM_LIMIT_BYTES)


def _dot(a, b):
    return jnp.dot(a, b, preferred_element_type=F32)


def _dot_nt(a, b):
    return lax.dot_general(a, b, (((1,), (1,)), ((), ())), preferred_element_type=F32)


def _dot_tn(a, b):
    return lax.dot_general(a, b, (((0,), (0,)), ((), ())), preferred_element_type=F32)


def _b(x):
    return x.astype(BF16)


@jax.custom_vjp
def _bmm(a, b):
    return _dot(_b(a), _b(b))


def _bmm_fwd(a, b):
    return _bmm(a, b), (a, b)


def _bmm_bwd(res, g):
    a, b = res
    g16 = _b(g)
    return _dot_nt(g16, _b(b)).astype(a.dtype), _dot_tn(_b(a), g16).astype(b.dtype)


_bmm.defvjp(_bmm_fwd, _bmm_bwd)


@jax.custom_vjp
def _bmm_nt(a, b):
    return _dot_nt(_b(a), _b(b))


def _bmm_nt_fwd(a, b):
    return _bmm_nt(a, b), (a, b)


def _bmm_nt_bwd(res, g):
    a, b = res
    g16 = _b(g)
    return _dot(g16, _b(b)).astype(a.dtype), _dot_tn(g16, _b(a)).astype(b.dtype)


_bmm_nt.defvjp(_bmm_nt_fwd, _bmm_nt_bwd)


@jax.custom_vjp
def _bmm_tn(a, b):
    return _dot_tn(_b(a), _b(b))


def _bmm_tn_fwd(a, b):
    return _bmm_tn(a, b), (a, b)


def _bmm_tn_bwd(res, g):
    a, b = res
    g16 = _b(g)
    return _dot_nt(_b(b), g16).astype(a.dtype), _dot(_b(a), g16).astype(b.dtype)


_bmm_tn.defvjp(_bmm_tn_fwd, _bmm_tn_bwd)


def _split3(x):
    hi = _b(x)
    r = x - hi.astype(F32)
    mid = _b(r)
    lo = _b(r - mid.astype(F32))
    return hi, mid, lo


def _x_left_raw(m, x):
    hi, mid, lo = _split3(x)
    return _dot(m, hi) + _dot(m, mid) + _dot(m, lo)


def _x_left_t_raw(m, x):
    hi, mid, lo = _split3(x)
    return _dot_tn(m, hi) + _dot_tn(m, mid) + _dot_tn(m, lo)


def _x_right_raw(x, m):
    hi, mid, lo = _split3(x)
    return _dot(hi, m) + _dot(mid, m) + _dot(lo, m)


def _x_right_t_raw(x, m):
    hi, mid, lo = _split3(x)
    return _dot_nt(hi, m) + _dot_nt(mid, m) + _dot_nt(lo, m)


@jax.custom_vjp
def _xleft(m, x):
    return _x_left_raw(m, x)


_xleft.defvjp(lambda m, x: (_x_left_raw(m, x), m),
              lambda m, g: (jnp.zeros_like(m), _x_left_t_raw(m, g)))


@jax.custom_vjp
def _xright(x, m):
    return _x_right_raw(x, m)


_xright.defvjp(lambda x, m: (_x_right_raw(x, m), m),
               lambda m, g: (_x_right_t_raw(g, m), jnp.zeros_like(m)))


def _sigmoid(x):
    return 1.0 / (1.0 + jnp.exp(-x))


def _silu(x):
    return x * _sigmoid(x)


def _softplus(x):
    return jnp.maximum(x, 0.0) + jnp.log(1.0 + jnp.exp(-jnp.abs(x)))


def _rms(x):
    return x * lax.rsqrt(jnp.mean(x * x, axis=-1, keepdims=True) + EPS)


def _iota(shape, dim):
    return lax.broadcasted_iota(jnp.int32, shape, dim)


def _blockdiag64(n):
    return jnp.where(_iota((n, n), 0) // 64 == _iota((n, n), 1) // 64, 1.0, 0.0).astype(BF16)


def _group64_rms(x, seg_sum):
    ms = seg_sum(x * x) * (1.0 / 64.0)
    return x * lax.rsqrt(ms + EPS)


def _fold64(x):
    ax = x.ndim - 1
    w = x.shape[ax]
    lo = (_iota(x.shape, ax) % 128) < 64
    return x + jnp.where(lo, pltpu.roll(x, w - 64, ax), pltpu.roll(x, 64, ax))


def _tile(n, want):
    t = min(n, want)
    assert n % t == 0, (n, t)
    return t


def _lane_tile(n, cap=1536):
    if n <= cap:
        return n
    return max(w for w in range(128, cap + 1, 128) if n % w == 0)


def ffn_fwd(h, g, w1, w3, w2, blk, dep):
    t, d = h.shape
    nk, fs = w1.shape[0], w1.shape[-1]
    tm = _tile(t, 512)

    def body(h_ref, g_ref, w1_ref, w3_ref, w2_ref, dep_ref, o_ref, a_ref, b_ref, u_scr, acc):
        k = pl.program_id(1)

        @pl.when(k == 0)
        def _():
            u_scr[...] = _b(_rms(h_ref[...]) * g_ref[...])
            acc[...] = jnp.zeros_like(acc)

        u = u_scr[...]
        a = _dot(u, w1_ref[...])
        b = _dot(u, w3_ref[...])
        a_ref[...] = _b(a)
        b_ref[...] = _b(b)
        acc[...] += _dot(_b(_silu(a) * b), w2_ref[...])

        @pl.when(k == nk - 1)
        def _():
            o_ref[...] = h_ref[...] + 0.5 * acc[...]

    wspec = lambda r, c: pl.BlockSpec((None, None, r, c), lambda i, k: (k, blk, 0, 0))
    return pl.pallas_call(
        body, name="ffn_fwd",
        grid=(t // tm, nk),
        in_specs=[pl.BlockSpec((tm, d), lambda i, k: (i, 0)), pl.BlockSpec((1, d), lambda i, k: (0, 0)),
                  wspec(d, fs), wspec(d, fs), wspec(fs, d), DEP_SPEC],
        out_specs=[pl.BlockSpec((tm, d), lambda i, k: (i, 0)),
                   pl.BlockSpec((None, tm, fs), lambda i, k: (k, i, 0)),
                   pl.BlockSpec((None, tm, fs), lambda i, k: (k, i, 0))],
        out_shape=[jax.ShapeDtypeStruct((t, d), F32), jax.ShapeDtypeStruct((nk, t, fs), BF16),
                   jax.ShapeDtypeStruct((nk, t, fs), BF16)],
        scratch_shapes=[pltpu.VMEM((tm, d), BF16), pltpu.VMEM((tm, d), F32)],
        compiler_params=_params("arbitrary", "arbitrary"),
    )(h, g, w1, w3, w2, dep)


def ffn_bwd(h, dy, g, a_s, b_s, w1, w3, w2, blk, dep):
    t, d = h.shape
    nk, fs = w1.shape[0], w1.shape[-1]
    tm = _tile(t, 512)

    def body(h_ref, dy_ref, g_ref, a_ref, b_ref, w1_ref, w3_ref, w2_ref, dep_ref,
             dh_ref, u_ref, da_ref, db_ref, s_ref, dg_ref, dyh_scr, du_acc, da0, db0, da1, db1):
        i, k = pl.program_id(0), pl.program_id(1)

        @pl.when(k == 0)
        def _():
            dyh_scr[...] = _b(0.5 * dy_ref[...])
            du_acc[...] = jnp.zeros_like(du_acc)

        @pl.when((k == 0) & (i == 0))
        def _():
            dg_ref[...] = jnp.zeros_like(dg_ref)

        def step(prev, cur):
            if prev is not None:
                du_acc[...] += _dot_nt(prev[0][...], w1_ref[...]) + _dot_nt(prev[1][...], w3_ref[...])
            if cur is not None:
                ds = _dot_nt(dyh_scr[...], w2_ref[...])
                a = a_ref[...].astype(F32)
                b = b_ref[...].astype(F32)
                sig = _sigmoid(a)
                sl = a * sig
                s_ref[...] = _b(sl * b)
                da = _b(ds * b * (sig * (1.0 + a * (1.0 - sig))))
                db = _b(ds * sl)
                da_ref[...] = da
                db_ref[...] = db
                cur[0][...] = da
                cur[1][...] = db

        even, odd = (da0, db0), (da1, db1)

        @pl.when(k == 0)
        def _():
            step(None, even)

        @pl.when((k > 0) & (k < nk) & (k % 2 == 1))
        def _():
            step(even, odd)

        @pl.when((k > 0) & (k < nk) & (k % 2 == 0))
        def _():
            step(odd, even)

        @pl.when(k == nk)
        def _():
            step(odd if nk % 2 == 0 else even, None)
            hh = h_ref[...]
            rstd = lax.rsqrt(jnp.mean(hh * hh, axis=-1, keepdims=True) + EPS)
            xh = hh * rstd
            gg = g_ref[...]
            u_ref[...] = _b(xh * gg)
            du = du_acc[...]
            dg_ref[...] += jnp.sum(du * xh, axis=0, keepdims=True)
            dxh = du * gg
            dh_ref[...] = dy_ref[...] + rstd * (dxh - xh * jnp.mean(dxh * xh, axis=-1, keepdims=True))

    cur = lambda k: jnp.minimum(k, nk - 1)
    prv = lambda k: jnp.maximum(k - 1, 0)
    wcur = lambda r, c: pl.BlockSpec((None, None, r, c), lambda i, k: (cur(k), blk, 0, 0))
    wprv = lambda r, c: pl.BlockSpec((None, None, r, c), lambda i, k: (prv(k), blk, 0, 0))
    tok = pl.BlockSpec((tm, d), lambda i, k: (i, 0))
    hid = pl.BlockSpec((None, tm, fs), lambda i, k: (cur(k), i, 0))
    return pl.pallas_call(
        body, name="ffn_bwd",
        grid=(t // tm, nk + 1),
        in_specs=[tok, tok, pl.BlockSpec((1, d), lambda i, k: (0, 0)), hid, hid, wprv(d, fs), wprv(d, fs), wcur(fs, d),
                  DEP_SPEC],
        out_specs=[tok, tok, hid, hid, hid, pl.BlockSpec((1, d), lambda i, k: (0, 0))],
        out_shape=[jax.ShapeDtypeStruct((t, d), F32), jax.ShapeDtypeStruct((t, d), BF16),
                   jax.ShapeDtypeStruct((nk, t, fs), BF16), jax.ShapeDtypeStruct((nk, t, fs), BF16),
                   jax.ShapeDtypeStruct((nk, t, fs), BF16), jax.ShapeDtypeStruct((1, d), F32)],
        scratch_shapes=[pltpu.VMEM((tm, d), BF16), pltpu.VMEM((tm, d), F32)] + [pltpu.VMEM((tm, fs), BF16)] * 4,
        compiler_params=_params("arbitrary", "arbitrary"),
    )(h, dy, g, a_s, b_s, w1, w3, w2, dep)


def wgrad_grouped_b(a, bs, dep, scale=1.0):
    t, m = a.shape
    ng, _, n = bs.shape
    tk = _tile(t, 2048)

    def body(a_ref, b_ref, dep_ref, o_ref):
        j = pl.program_id(1)

        @pl.when(j == 0)
        def _():
            o_ref[...] = jnp.zeros_like(o_ref)

        o_ref[...] += _dot_tn(_b(a_ref[...]), _b(b_ref[...]))

        if scale != 1.0:
            @pl.when(j == pl.num_programs(1) - 1)
            def _():
                o_ref[...] = o_ref[...] * scale

    return pl.pallas_call(
        body, name="wgrad_gb",
        grid=(ng, t // tk),
        in_specs=[pl.BlockSpec((tk, m), lambda k, j: (j, 0)), pl.BlockSpec((None, tk, n), lambda k, j: (k, j, 0)),
                  DEP_SPEC],
        out_specs=pl.BlockSpec((None, m, n), lambda k, j: (k, 0, 0)),
        out_shape=jax.ShapeDtypeStruct((ng, m, n), F32),
        compiler_params=_params("arbitrary", "arbitrary"),
    )(a, bs, dep)


def wgrad_grouped_a(as_, b, dep, scale=1.0):
    ng, t, m = as_.shape
    n = b.shape[1]
    tk = _tile(t, 2048)

    def body(a_ref, b_ref, dep_ref, o_ref):
        j = pl.program_id(1)

        @pl.when(j == 0)
        def _():
            o_ref[...] = jnp.zeros_like(o_ref)

        o_ref[...] += _dot_tn(_b(a_ref[...]), _b(b_ref[...]))

        if scale != 1.0:
            @pl.when(j == pl.num_programs(1) - 1)
            def _():
                o_ref[...] = o_ref[...] * scale

    return pl.pallas_call(
        body, name="wgrad_ga",
        grid=(ng, t // tk),
        in_specs=[pl.BlockSpec((None, tk, m), lambda k, j: (k, j, 0)), pl.BlockSpec((tk, n), lambda k, j: (j, 0)),
                  DEP_SPEC],
        out_specs=pl.BlockSpec((None, m, n), lambda k, j: (k, 0, 0)),
        out_shape=jax.ShapeDtypeStruct((ng, m, n), F32),
        compiler_params=_params("arbitrary", "arbitrary"),
    )(as_, b, dep)


def wgrad(a, b):
    t, m = a.shape
    n = b.shape[1]
    tk = _tile(t, 1024)
    tn = _lane_tile(n, 1536 if m <= 1024 else 512)

    def body(a_ref, b_ref, o_ref):
        @pl.when(pl.program_id(1) == 0)
        def _():
            o_ref[...] = jnp.zeros_like(o_ref)

        o_ref[...] += _dot_tn(_b(a_ref[...]), _b(b_ref[...]))

    return pl.pallas_call(
        body, name="wgrad",
        grid=(n // tn, t // tk),
        in_specs=[pl.BlockSpec((tk, m), lambda c, j: (j, 0)), pl.BlockSpec((tk, tn), lambda c, j: (j, c))],
        out_specs=pl.BlockSpec((m, tn), lambda c, j: (0, c)),
        out_shape=jax.ShapeDtypeStruct((m, n), F32),
        compiler_params=_params("arbitrary", "arbitrary"),
    )(a, b)


def norm_mm(h, g, w):
    t, d = h.shape
    n = w.shape[1]
    tm = _tile(t, 1024)
    tn = _lane_tile(n)

    def body(h_ref, g_ref, w_ref, o_ref, u_scr):
        @pl.when(pl.program_id(1) == 0)
        def _():
            u_scr[...] = _b(_rms(h_ref[...]) * g_ref[...])

        o_ref[...] = _dot(u_scr[...], w_ref[...])

    return pl.pallas_call(
        body, name="norm_mm",
        grid=(t // tm, n // tn),
        in_specs=[pl.BlockSpec((tm, d), lambda i, j: (i, 0)), pl.BlockSpec((1, d), lambda i, j: (0, 0)),
                  pl.BlockSpec((d, tn), lambda i, j: (0, j))],
        out_specs=pl.BlockSpec((tm, tn), lambda i, j: (i, j)),
        out_shape=jax.ShapeDtypeStruct((t, n), F32),
        scratch_shapes=[pltpu.VMEM((tm, d), BF16)],
        compiler_params=_params("arbitrary", "arbitrary"),
    )(h, g, w)


def norm_mm_bwd(h, g, w, dout, dres, dep, scale=1.0):
    t, d = h.shape
    n = w.shape[1]
    tm = _tile(t, 512)
    tn = _lane_tile(n)
    nj = n // tn

    def body(h_ref, g_ref, w_ref, do_ref, dr_ref, dep_ref, dh_ref, u_ref, dg_ref, du_acc):
        i, j = pl.program_id(0), pl.program_id(1)

        @pl.when(j == 0)
        def _():
            du_acc[...] = jnp.zeros_like(du_acc)

        @pl.when((j == 0) & (i == 0))
        def _():
            dg_ref[...] = jnp.zeros_like(dg_ref)

        du_acc[...] += _dot_nt(_b(do_ref[...]), w_ref[...])

        @pl.when(j == nj - 1)
        def _():
            hh = h_ref[...]
            rstd = lax.rsqrt(jnp.mean(hh * hh, axis=-1, keepdims=True) + EPS)
            xh = hh * rstd
            gg = g_ref[...]
            u_ref[...] = _b(xh * gg)
            du = du_acc[...] * scale
            dg_ref[...] += jnp.sum(du * xh, axis=0, keepdims=True)
            dxh = du * gg
            dh_ref[...] = dr_ref[...] + rstd * (dxh - xh * jnp.mean(dxh * xh, axis=-1, keepdims=True))

    tok = pl.BlockSpec((tm, d), lambda i, j: (i, 0))
    return pl.pallas_call(
        body, name="norm_mm_bwd",
        grid=(t // tm, nj),
        in_specs=[tok, pl.BlockSpec((1, d), lambda i, j: (0, 0)), pl.BlockSpec((d, tn), lambda i, j: (0, j)),
                  pl.BlockSpec((tm, tn), lambda i, j: (i, j)), tok, DEP_SPEC],
        out_specs=[tok, tok, pl.BlockSpec((1, d), lambda i, j: (0, 0))],
        out_shape=[jax.ShapeDtypeStruct((t, d), F32), jax.ShapeDtypeStruct((t, d), BF16),
                   jax.ShapeDtypeStruct((1, d), F32)],
        scratch_shapes=[pltpu.VMEM((tm, d), F32)],
        compiler_params=_params("arbitrary", "arbitrary"),
    )(h, g, w, dout, dres, dep)


CONV_COLS = 512


CONV_ROWS = 64


def _conv_pre(ext, w, b, r0, n):
    return (b + w[0:1] * ext[pl.ds(5 + r0, n), :] + w[1:2] * ext[pl.ds(6 + r0, n), :]
            + w[2:3] * ext[pl.ds(7 + r0, n), :] + w[3:4] * ext[pl.ds(8 + r0, n), :])


def conv_fwd(zx, cw, cb, col0):
    t = zx.shape[0]
    c = cw.shape[1]
    tm = _tile(t, 512)
    cb0 = col0 // CONV_COLS

    rc = _tile(tm, CONV_ROWS)

    def body(x_ref, w_ref, b_ref, o_ref, ext):
        @pl.when(pl.program_id(1) == 0)
        def _():
            ext[0:8, :] = jnp.zeros((8, CONV_COLS), F32)

        ext[8:, :] = x_ref[...]
        w, b = w_ref[...], b_ref[...]
        for r0 in range(0, tm, rc):
            o_ref[r0:r0 + rc, :] = _silu(_conv_pre(ext, w, b, r0, rc))
        ext[0:8, :] = ext[tm:tm + 8, :]

    return pl.pallas_call(
        body, name="conv_fwd",
        grid=(c // CONV_COLS, t // tm),
        in_specs=[pl.BlockSpec((tm, CONV_COLS), lambda j, i: (i, cb0 + j)),
                  pl.BlockSpec((8, CONV_COLS), lambda j, i: (0, j)), pl.BlockSpec((1, CONV_COLS), lambda j, i: (0, j))],
        out_specs=pl.BlockSpec((tm, CONV_COLS), lambda j, i: (i, j)),
        out_shape=jax.ShapeDtypeStruct((t, c), F32),
        scratch_shapes=[pltpu.VMEM((tm + 8, CONV_COLS), F32)],
        compiler_params=_params("arbitrary", "arbitrary"),
    )(zx, cw, cb)


def conv_bwd(dzx, zx, dxs, dbm, dcm, cw, cb, col0):
    t = zx.shape[0]
    c = cw.shape[1]
    tm = _tile(t, 512)
    nt = t // tm
    cb0 = col0 // CONV_COLS
    nxs = dxs.shape[1] // CONV_COLS
    hb = tm // 8

    rc = _tile(tm, CONV_ROWS)

    def body(dzx_ref, x_ref, xh_ref, dxs_ref, db_ref, dc_ref, w_ref, b_ref, o_ref, dw_ref, dbias_ref, ext, gy):
        j, i = pl.program_id(0), pl.program_id(1)
        ri = nt - 1 - i

        @pl.when(i == 0)
        def _():
            gy[tm:tm + 8, :] = jnp.zeros((8, CONV_COLS), F32)
            dw_ref[...] = jnp.zeros_like(dw_ref)
            dbias_ref[...] = jnp.zeros_like(dbias_ref)

        ext[0:8, :] = jnp.where(ri > 0, xh_ref[...], 0.0)
        ext[8:, :] = x_ref[...]
        w, b = w_ref[...], b_ref[...]
        dw = [jnp.zeros((1, CONV_COLS), F32) for _ in range(SSM_CONV)]
        dbias = jnp.zeros((1, CONV_COLS), F32)
        for r0 in range(0, tm, rc):
            rows = pl.ds(r0, rc)
            win = [ext[pl.ds(5 + tap + r0, rc), :] for tap in range(SSM_CONV)]
            y = b + w[0:1] * win[0] + w[1:2] * win[1] + w[2:3] * win[2] + w[3:4] * win[3]
            sig = _sigmoid(y)
            dout = jnp.where(j < nxs, dxs_ref[rows, :], jnp.where(j == nxs, db_ref[rows, :], dc_ref[rows, :]))
            g = dout * (sig * (1.0 + y * (1.0 - sig)))
            gy[rows, :] = g
            dbias = dbias + jnp.sum(g, axis=0, keepdims=True)
            for tap in range(SSM_CONV):
                dw[tap] = dw[tap] + jnp.sum(g * win[tap], axis=0, keepdims=True)
        for r0 in range(0, tm, rc):
            o_ref[r0:r0 + rc, :] = (w[0:1] * gy[pl.ds(r0 + 3, rc), :] + w[1:2] * gy[pl.ds(r0 + 2, rc), :]
                                    + w[2:3] * gy[pl.ds(r0 + 1, rc), :] + w[3:4] * gy[pl.ds(r0, rc), :])
        gy[tm:tm + 8, :] = gy[0:8, :]
        for tap in range(SSM_CONV):
            dw_ref[tap:tap + 1, :] += dw[tap]
        dbias_ref[...] += dbias

    return pl.pallas_call(
        body, name="conv_bwd",
        grid=(c // CONV_COLS, nt),
        in_specs=[pl.BlockSpec(memory_space=pl.ANY),
                  pl.BlockSpec((tm, CONV_COLS), lambda j, i: (nt - 1 - i, cb0 + j)),
                  pl.BlockSpec((8, CONV_COLS), lambda j, i: (jnp.maximum((nt - 1 - i) * hb - 1, 0), cb0 + j)),
                  pl.BlockSpec((tm, CONV_COLS), lambda j, i: (nt - 1 - i, jnp.minimum(j, nxs - 1))),
                  pl.BlockSpec((tm, CONV_COLS), lambda j, i: (nt - 1 - i, 0)),
                  pl.BlockSpec((tm, CONV_COLS), lambda j, i: (nt - 1 - i, 0)),
                  pl.BlockSpec((8, CONV_COLS), lambda j, i: (0, j)), pl.BlockSpec((1, CONV_COLS), lambda j, i: (0, j))],
        out_specs=[pl.BlockSpec((tm, CONV_COLS), lambda j, i: (nt - 1 - i, cb0 + j)),
                   pl.BlockSpec((8, CONV_COLS), lambda j, i: (0, j)), pl.BlockSpec((1, CONV_COLS), lambda j, i: (0, j))],
        out_shape=[jax.ShapeDtypeStruct(dzx.shape, F32), jax.ShapeDtypeStruct((8, c), F32),
                   jax.ShapeDtypeStruct((1, c), F32)],
        scratch_shapes=[pltpu.VMEM((tm + 8, CONV_COLS), F32), pltpu.VMEM((tm + 8, CONV_COLS), F32)],
        input_output_aliases={0: 0},
        compiler_params=_params("arbitrary", "arbitrary"),
    )(dzx, zx, zx, dxs, dbm, dcm, cw, cb)


def _ssd_group(xs, bg, cg, dtraw, s0, bias, alog, dsk):
    L = xs.shape[0]
    causal = _iota((L, L), 0) >= _iota((L, L), 1)
    tril = jnp.where(causal, 1.0, 0.0).astype(BF16)
    dt = _softplus(dtraw + bias)
    a = -jnp.exp(alog)
    acum = _xleft(tril, dt * a)
    acum_t = acum.T
    dt_t = dt.T
    cb = _bmm_nt(cg, bg)
    lo = _iota((L, 128), 1) < 64
    lo_row = _iota((1, 128), 1) < 64
    lo_col = _iota((128, 1), 0) < 64
    alast = acum[L - 1:L, :]
    ys, s1s = [], []
    for q in range(4):
        xp = xs[:, q * 128:(q + 1) * 128]
        sp = s0[q * 128:(q + 1) * 128, :]
        yd, ec, wc, el = [], [], [], []
        for j in range(2):
            r = 2 * q + j
            ac = acum[:, r:r + 1]
            decay = jnp.exp(jnp.where(causal, ac - acum_t[r:r + 1, :], NEG))
            yd.append(_bmm(cb * decay * dt_t[r:r + 1, :], xp))
            ec.append(jnp.exp(ac))
            al = alast[:, r:r + 1]
            wc.append(jnp.exp(al - ac) * dt[:, r:r + 1])
            el.append(jnp.exp(al))
        y_off = _bmm_nt(cg, sp) * jnp.where(lo, ec[0], ec[1])
        dsel = jnp.where(lo_row, dsk[:, 2 * q:2 * q + 1], dsk[:, 2 * q + 1:2 * q + 2])
        ys.append(jnp.where(lo, yd[0], yd[1]) + y_off + dsel * xp)
        xw = xp * jnp.where(lo, wc[0], wc[1])
        s1s.append(sp * jnp.where(lo_col, el[0], el[1]) + _bmm_tn(xw, bg))
    return jnp.concatenate(ys, axis=1), jnp.concatenate(s1s, axis=0)


def ssd_fwd(xc, zx, bias, alog, dsk, dt_col0):
    t = xc.shape[0]
    L = _tile(t, SSM_CHUNK)
    nc = t // L
    g = SSM_GROUPS
    dtb = dt_col0 // 512

    def body(xs_ref, b_ref, c_ref, dt_ref, bias_ref, alog_ref, dsk_ref, y_ref, st_ref, state):
        @pl.when(pl.program_id(0) == 0)
        def _():
            state[...] = jnp.zeros_like(state)

        for gi in range(g):
            lane = slice(gi * 128, (gi + 1) * 128)
            wide = slice(gi * 512, (gi + 1) * 512)
            s0 = state[gi]
            st_ref[gi] = s0
            y, s1 = _ssd_group(xs_ref[:, wide], b_ref[:, lane], c_ref[:, lane], dt_ref[:, lane], s0,
                               bias_ref[:, lane], alog_ref[:, lane], dsk_ref[:, lane])
            y_ref[:, wide] = y
            state[gi] = s1

    vec = pl.BlockSpec((1, 512), lambda c: (0, 0))
    return pl.pallas_call(
        body, name="ssd_fwd",
        grid=(nc,),
        in_specs=[pl.BlockSpec((L, 2048), lambda c: (c, 0)), pl.BlockSpec((L, 512), lambda c: (c, 4)),
                  pl.BlockSpec((L, 512), lambda c: (c, 5)), pl.BlockSpec((L, 512), lambda c: (c, dtb)), vec, vec, vec],
        out_specs=[pl.BlockSpec((L, 2048), lambda c: (c, 0)),
                   pl.BlockSpec((None, g, 512, 128), lambda c: (c, 0, 0, 0))],
        out_shape=[jax.ShapeDtypeStruct((t, 2048), F32), jax.ShapeDtypeStruct((nc, g, 512, 128), F32)],
        scratch_shapes=[pltpu.VMEM((g, 512, 128), F32)],
        compiler_params=_params("arbitrary"),
    )(xc, xc, xc, zx, bias, alog, dsk)


def ssd_bwd(dzx, dy, xc, zx, states, bias, alog, dsk, dt_col0):
    t = xc.shape[0]
    L = _tile(t, SSM_CHUNK)
    nc = t // L
    g = SSM_GROUPS
    dtb = dt_col0 // 512

    def body(dzx_ref, dy_ref, xs_ref, b_ref, c_ref, dt_ref, st_ref, bias_ref, alog_ref, dsk_ref,
             ddt_ref, dxs_ref, db_ref, dc_ref, dbias_ref, dalog_ref, ddsk_ref, dstate):
        @pl.when(pl.program_id(0) == 0)
        def _():
            dstate[...] = jnp.zeros_like(dstate)
            dbias_ref[...] = jnp.zeros_like(dbias_ref)
            dalog_ref[...] = jnp.zeros_like(dalog_ref)
            ddsk_ref[...] = jnp.zeros_like(ddsk_ref)

        for gi in range(g):
            lane = slice(gi * 128, (gi + 1) * 128)
            wide = slice(gi * 512, (gi + 1) * 512)
            _, vjp = jax.vjp(_ssd_group, xs_ref[:, wide], b_ref[:, lane], c_ref[:, lane], dt_ref[:, lane], st_ref[gi],
                             bias_ref[:, lane], alog_ref[:, lane], dsk_ref[:, lane])
            dxs, db, dc, ddt, ds0, dbias, dalog, ddsk = vjp((dy_ref[:, wide], dstate[gi]))
            dxs_ref[:, wide] = dxs
            db_ref[:, lane] = db
            dc_ref[:, lane] = dc
            ddt_ref[:, lane] = ddt
            dstate[gi] = ds0
            dbias_ref[:, lane] += dbias
            dalog_ref[:, lane] += dalog
            ddsk_ref[:, lane] += ddsk

    rc = lambda c: nc - 1 - c
    vec = pl.BlockSpec((1, 512), lambda c: (0, 0))
    return pl.pallas_call(
        body, name="ssd_bwd",
        grid=(nc,),
        in_specs=[pl.BlockSpec(memory_space=pl.ANY),
                  pl.BlockSpec((L, 2048), lambda c: (rc(c), 0)), pl.BlockSpec((L, 2048), lambda c: (rc(c), 0)),
                  pl.BlockSpec((L, 512), lambda c: (rc(c), 4)), pl.BlockSpec((L, 512), lambda c: (rc(c), 5)),
                  pl.BlockSpec((L, 512), lambda c: (rc(c), dtb)),
                  pl.BlockSpec((None, g, 512, 128), lambda c: (rc(c), 0, 0, 0)), vec, vec, vec],
        out_specs=[pl.BlockSpec((L, 512), lambda c: (rc(c), dtb)), pl.BlockSpec((L, 2048), lambda c: (rc(c), 0)),
                   pl.BlockSpec((L, 512), lambda c: (rc(c), 0)), pl.BlockSpec((L, 512), lambda c: (rc(c), 0)),
                   vec, vec, vec],
        out_shape=[jax.ShapeDtypeStruct(dzx.shape, F32), jax.ShapeDtypeStruct((t, 2048), F32),
                   jax.ShapeDtypeStruct((t, 512), F32), jax.ShapeDtypeStruct((t, 512), F32),
                   jax.ShapeDtypeStruct((1, 512), F32), jax.ShapeDtypeStruct((1, 512), F32),
                   jax.ShapeDtypeStruct((1, 512), F32)],
        scratch_shapes=[pltpu.VMEM((g, 512, 128), F32)],
        input_output_aliases={0: 0},
        compiler_params=_params("arbitrary"),
    )(dzx, dy, xc, xc, xc, zx, states, bias, alog, dsk)


def _gate_tile(y, z, gn):
    gated = y * _silu(z)
    parts = [_rms(gated[:, k * 512:(k + 1) * 512]) for k in range(SSM_GROUPS)]
    return jnp.concatenate(parts, axis=1) * gn


def gate_out_fwd(h, y, zx, gn, w_out):
    t, d = h.shape
    di = y.shape[1]
    tm = _tile(t, 256)

    def body(h_ref, y_ref, z_ref, gn_ref, w_ref, o_ref):
        yn = _gate_tile(y_ref[...], z_ref[...], gn_ref[...])
        o_ref[...] = h_ref[...] + _dot(_b(yn), w_ref[...])

    return pl.pallas_call(
        body, name="gate_out_fwd",
        grid=(t // tm,),
        in_specs=[pl.BlockSpec((tm, d), lambda i: (i, 0)), pl.BlockSpec((tm, di), lambda i: (i, 0)),
                  pl.BlockSpec((tm, di), lambda i: (i, 0)), pl.BlockSpec((1, di), lambda i: (0, 0)),
                  pl.BlockSpec((di, d), lambda i: (0, 0))],
        out_specs=pl.BlockSpec((tm, d), lambda i: (i, 0)),
        out_shape=jax.ShapeDtypeStruct((t, d), F32),
        compiler_params=_params("arbitrary"),
    )(h, y, zx, gn, w_out)


def gate_out_bwd(dy, y, zx, gn, w_out, n_zx, dep):
    t, d = dy.shape
    di = y.shape[1]
    tm = _tile(t, 256)

    def body(dy_ref, y_ref, z_ref, gn_ref, w_ref, dep_ref, dz_ref, dys_ref, yn_ref, dgn_ref):
        @pl.when(pl.program_id(0) == 0)
        def _():
            dgn_ref[...] = jnp.zeros_like(dgn_ref)

        yn, vjp = jax.vjp(_gate_tile, y_ref[...], z_ref[...], gn_ref[...])
        dyn = _dot_nt(_b(dy_ref[...]), w_ref[...])
        dys, dz, dgn = vjp(dyn)
        yn_ref[...] = _b(yn)
        dys_ref[...] = dys
        dz_ref[...] = dz
        dgn_ref[...] += dgn

    return pl.pallas_call(
        body, name="gate_out_bwd",
        grid=(t // tm,),
        in_specs=[pl.BlockSpec((tm, d), lambda i: (i, 0)), pl.BlockSpec((tm, di), lambda i: (i, 0)),
                  pl.BlockSpec((tm, di), lambda i: (i, 0)), pl.BlockSpec((1, di), lambda i: (0, 0)),
                  pl.BlockSpec((di, d), lambda i: (0, 0)), DEP_SPEC],
        out_specs=[pl.BlockSpec((tm, di), lambda i: (i, 0)), pl.BlockSpec((tm, di), lambda i: (i, 0)),
                   pl.BlockSpec((tm, di), lambda i: (i, 0)), pl.BlockSpec((1, di), lambda i: (0, 0))],
        out_shape=[jax.ShapeDtypeStruct((t, n_zx), F32), jax.ShapeDtypeStruct((t, di), F32),
                   jax.ShapeDtypeStruct((t, di), BF16), jax.ShapeDtypeStruct((1, di), F32)],
        compiler_params=_params("arbitrary"),
    )(dy, y, zx, gn, w_out, dep)


def _attn_block(qp, kvp, kvc, biasm, sinks, qg, kg, w_o, first):
    nq = qp.shape[0]
    n_pairs = qp.shape[1] // 128
    hk = n_pairs
    rows = hk * nq
    seg = functools.partial(_xright, m=_blockdiag64(128))
    scale = ATT_HEAD_DIM ** -0.5
    qi = (_iota((rows, 2 * nq), 0) % nq) + nq
    kj = _iota((rows, 2 * nq), 1)
    dist = qi - kj
    valid = (dist >= 0) & (dist < ATT_WINDOW) & (jnp.logical_not(first) | (kj >= nq))
    lo = _iota((nq, 128), 1) < 64
    kv = jnp.concatenate([kvp, kvc], axis=0)
    outs = [None] * n_pairs
    for kvh in range(2):
        kn = _group64_rms(kv[:, kvh * 128:(kvh + 1) * 128], seg) * kg
        vv = kv[:, 256 + kvh * 128:256 + (kvh + 1) * 128]
        pairs = range(kvh * hk // 2, (kvh + 1) * hk // 2)
        qs, sk = [], []
        for p in pairs:
            qn = _group64_rms(qp[:, p * 128:(p + 1) * 128], seg) * qg
            qs += [jnp.where(lo, qn, 0.0), jnp.where(lo, 0.0, qn)]
            sk += [jnp.broadcast_to(sinks[:, h:h + 1], (nq, 1)) for h in (2 * p, 2 * p + 1)]
        sink = jnp.concatenate(sk, axis=0)
        s = _bmm_nt(jnp.concatenate(qs, axis=0), kn) * scale + biasm[kvh * rows:(kvh + 1) * rows]
        s = jnp.where(valid, s, NEG)
        m = lax.stop_gradient(jnp.maximum(jnp.max(s, axis=-1, keepdims=True), sink))
        pexp = jnp.exp(s - m)
        den = jnp.sum(pexp, axis=-1, keepdims=True) + jnp.exp(sink - m)
        o = _bmm(pexp * (1.0 / den), vv)
        for n, p in enumerate(pairs):
            outs[p] = jnp.where(lo, o[2 * n * nq:(2 * n + 1) * nq], o[(2 * n + 1) * nq:(2 * n + 2) * nq])
    o = jnp.concatenate(outs, axis=1)
    return _bmm(o, w_o), o


def attn_fwd(h, qp, kvd, biasm, sinks, qg, kg, w_o):
    t, d = h.shape
    nq = ATT_WINDOW
    nb = t // nq
    nh = qp.shape[1] // ATT_HEAD_DIM

    def body(h_ref, q_ref, kp_ref, kc_ref, bias_ref, s_ref, qg_ref, kg_ref, w_ref, o_ref):
        out, _ = _attn_block(q_ref[...], kp_ref[...], kc_ref[...], bias_ref[...], s_ref[...], qg_ref[...],
                             kg_ref[...], w_ref[...], pl.program_id(0) == 0)
        o_ref[...] = h_ref[...] + out

    vec = pl.BlockSpec((1, 128), lambda i: (0, 0))
    return pl.pallas_call(
        body, name="attn_fwd",
        grid=(nb,),
        in_specs=[pl.BlockSpec((nq, d), lambda i: (i, 0)), pl.BlockSpec((nq, nh * 64), lambda i: (i, 0)),
                  pl.BlockSpec((nq, 512), lambda i: (jnp.maximum(i - 1, 0), 0)),
                  pl.BlockSpec((nq, 512), lambda i: (i, 0)),
                  pl.BlockSpec((nh * nq, 2 * nq), lambda i: (0, 0)), vec, vec, vec,
                  pl.BlockSpec((nh * 64, d), lambda i: (0, 0))],
        out_specs=pl.BlockSpec((nq, d), lambda i: (i, 0)),
        out_shape=jax.ShapeDtypeStruct((t, d), F32),
        compiler_params=_params("arbitrary"),
    )(h, qp, kvd, kvd, biasm, sinks, qg, kg, w_o)


def attn_bwd(dy, qp, kvd, biasm, sinks, qg, kg, w_o, dep):
    t, d = dy.shape
    nq = ATT_WINDOW
    nb = t // nq
    nh = qp.shape[1] // ATT_HEAD_DIM

    def body(dy_ref, q_ref, kp_ref, kc_ref, bias_ref, s_ref, qg_ref, kg_ref, w_ref, dep_ref,
             dq_ref, dkv_ref, o_ref, dbias_ref, ds_ref, dqg_ref, dkg_ref, carry):
        i = pl.program_id(0)

        @pl.when(i == 0)
        def _():
            carry[...] = jnp.zeros_like(carry)
            dbias_ref[...] = jnp.zeros_like(dbias_ref)
            ds_ref[...] = jnp.zeros_like(ds_ref)
            dqg_ref[...] = jnp.zeros_like(dqg_ref)
            dkg_ref[...] = jnp.zeros_like(dkg_ref)

        @pl.when(i < nb)
        def _():
            fn = functools.partial(_attn_block, w_o=w_ref[...], first=(i == 0))
            (_, o), vjp = jax.vjp(fn, q_ref[...], kp_ref[...], kc_ref[...], bias_ref[...], s_ref[...],
                                  qg_ref[...], kg_ref[...])
            dq, dkp, dkc, dbias, dsk, dqg, dkg = vjp((dy_ref[...], jnp.zeros((nq, nh * 64), F32)))
            dq_ref[...] = dq
            o_ref[...] = _b(o)
            dkv_ref[...] = _fold64(carry[...] + dkp)
            carry[...] = dkc
            dbias_ref[...] += dbias
            ds_ref[...] += dsk
            dqg_ref[...] += _fold64(dqg)
            dkg_ref[...] += _fold64(dkg)

        @pl.when(i == nb)
        def _():
            dkv_ref[...] = _fold64(carry[...])

    cl = lambda i: jnp.minimum(i, nb - 1)
    vec = pl.BlockSpec((1, 128), lambda i: (0, 0))
    return pl.pallas_call(
        body, name="attn_bwd",
        grid=(nb + 1,),
        in_specs=[pl.BlockSpec((nq, d), lambda i: (cl(i), 0)), pl.BlockSpec((nq, nh * 64), lambda i: (cl(i), 0)),
                  pl.BlockSpec((nq, 512), lambda i: (jnp.maximum(cl(i) - 1, 0), 0)),
                  pl.BlockSpec((nq, 512), lambda i: (cl(i), 0)),
                  pl.BlockSpec((nh * nq, 2 * nq), lambda i: (0, 0)), vec, vec, vec,
                  pl.BlockSpec((nh * 64, d), lambda i: (0, 0)), DEP_SPEC],
        out_specs=[pl.BlockSpec((nq, nh * 64), lambda i: (cl(i), 0)),
                   pl.BlockSpec((nq, 512), lambda i: (jnp.maximum(i - 1, 0), 0)),
                   pl.BlockSpec((nq, nh * 64), lambda i: (cl(i), 0)),
                   pl.BlockSpec((nh * nq, 2 * nq), lambda i: (0, 0)), vec, vec, vec],
        out_shape=[jax.ShapeDtypeStruct((t, nh * 64), F32), jax.ShapeDtypeStruct((t, 512), F32),
                   jax.ShapeDtypeStruct((t, nh * 64), BF16), jax.ShapeDtypeStruct((nh * nq, 2 * nq), F32),
                   jax.ShapeDtypeStruct((1, 128), F32), jax.ShapeDtypeStruct((1, 128), F32),
                   jax.ShapeDtypeStruct((1, 128), F32)],
        scratch_shapes=[pltpu.VMEM((nq, 512), F32)],
        compiler_params=_params("arbitrary"),
    )(dy, qp, kvd, kvd, biasm, sinks, qg, kg, w_o, dep)


def _t5_buckets():
    nq = ATT_WINDOW
    dist = (np.arange(nq)[:, None] + nq) - np.arange(2 * nq)[None, :]
    n = np.maximum(dist, 0)
    max_exact = REL_BUCKETS // 2
    nf = np.maximum(n, 1).astype(np.float32)
    large = max_exact + (np.log(nf / max_exact) / math.log(ATT_WINDOW / max_exact)
                         * (REL_BUCKETS - max_exact)).astype(np.int32)
    large = np.minimum(large, REL_BUCKETS - 1)
    return np.where(n < max_exact, n, large).astype(np.int32)


def rel_bias_bwd(dbias, buckets):
    nh = dbias.shape[0]

    def body(db_ref, bk_ref, o_ref):
        bk = bk_ref[...]
        lane = _iota((1, 128), 1)
        row = _iota((REL_BUCKETS, 128), 0)
        acc = jnp.zeros((REL_BUCKETS, 128), F32)
        for h in range(nh):
            dbh = db_ref[h]
            for b in range(REL_BUCKETS):
                v = jnp.sum(jnp.where(bk == b, dbh, 0.0))
                acc = acc + jnp.where((row == b) & (lane == h), v, 0.0)
        o_ref[...] = acc

    return pl.pallas_call(
        body, name="rel_bias_bwd",
        out_shape=jax.ShapeDtypeStruct((REL_BUCKETS, 128), F32),
        compiler_params=_params(),
    )(dbias, buckets)


def loss_head(y, target):
    t, d = y.shape
    tm = _tile(t, 512)

    def body(y_ref, t_ref, l_ref, dy_ref):
        @pl.when(pl.program_id(0) == 0)
        def _():
            l_ref[...] = jnp.zeros_like(l_ref)

        e = y_ref[...] - t_ref[...]
        l_ref[...] += 0.5 * jnp.sum(jnp.mean(e * e, axis=-1, keepdims=True), axis=0, keepdims=True)
        dy_ref[...] = e * (1.0 / d)

    return pl.pallas_call(
        body, name="loss_head",
        grid=(t // tm,),
        in_specs=[pl.BlockSpec((tm, d), lambda i: (i, 0)), pl.BlockSpec((tm, d), lambda i: (i, 0))],
        out_specs=[pl.BlockSpec((1, 1), lambda i: (0, 0)), pl.BlockSpec((tm, d), lambda i: (i, 0))],
        out_shape=[jax.ShapeDtypeStruct((1, 1), F32), jax.ShapeDtypeStruct((t, d), F32)],
        compiler_params=_params("arbitrary"),
    )(y, target)


def adamw(w, g, m, v):
    r, c = w.shape
    tr = r if r <= 512 else _tile(r, 256)

    def body(w_ref, g_ref, m_ref, v_ref, d_ref, nm_ref, nv_ref):
        gg = g_ref[...]
        nm = ADAM_B1 * m_ref[...] + (1.0 - ADAM_B1) * gg
        nv = ADAM_B2 * v_ref[...] + (1.0 - ADAM_B2) * (gg * gg)
        m_hat = nm / (1.0 - ADAM_B1 ** ADAM_STEP)
        v_hat = nv / (1.0 - ADAM_B2 ** ADAM_STEP)
        d_ref[...] = -ADAM_LR * (m_hat / (jnp.sqrt(v_hat) + ADAM_EPS) + ADAM_WD * w_ref[...])
        nm_ref[...] = nm
        nv_ref[...] = nv

    spec = pl.BlockSpec((tr, c), lambda i: (i, 0))
    shp = jax.ShapeDtypeStruct((r, c), F32)
    return pl.pallas_call(
        body, name="adamw",
        grid=(r // tr,),
        in_specs=[spec] * 4, out_specs=[spec] * 3, out_shape=[shp] * 3,
        compiler_params=_params("arbitrary"),
    )(w, g, m, v)


def _my_pos():
    return lax.axis_index("x"), lax.axis_index("y"), lax.axis_index("c")


def _other_chips(x, y):
    return [(1 - x, y), (x, 1 - y), (1 - x, 1 - y)]


def _chip_id(x, y):
    return 2 * x + y


HBM_SPEC = pl.BlockSpec(memory_space=pltpu.HBM)
SEM_SPEC = pl.BlockSpec(memory_space=pltpu.SEMAPHORE)
DATAFLOW = pltpu.SideEffectType.DATAFLOW_SIDE_EFFECTING


def _in_hbm(a):
    return pltpu.with_memory_space_constraint(a, pltpu.HBM)


def _ici_gather_copy(buf, p, j, chip, c, to, send_sems, recv_sems):
    blk = buf.at[_chip_id(*chip), c]
    return pltpu.make_async_remote_copy(
        src_ref=blk, dst_ref=blk, send_sem=send_sems.at[3 * p + j], recv_sem=recv_sems.at[3 * p + j],
        device_id=to, device_id_type=MESH)


def gather_start(bufs, after, tag):
    n = len(bufs)

    def body(*refs):
        ins = refs[:n]
        send_sems, recv_sems = refs[n + 1], refs[n + 2]
        token = refs[2 * n + 3]
        x, y, c = _my_pos()
        for p in range(n):
            for j, chip in enumerate(_other_chips(x, y)):
                _ici_gather_copy(ins[p], p, j, (x, y), c, (*chip, c), send_sems, recv_sems).start()
        token[...] = jnp.zeros_like(token)

    out = pl.pallas_call(
        body, name="gather_start_" + tag,
        in_specs=[HBM_SPEC] * n + [DEP_SPEC],
        out_specs=(SEM_SPEC, SEM_SPEC, *([HBM_SPEC] * n), pl.BlockSpec(memory_space=pltpu.VMEM)),
        out_shape=(pltpu.SemaphoreType.DMA((3 * n,)), pltpu.SemaphoreType.DMA((3 * n,)),
                   *[pltpu.HBM(b.shape, b.dtype) for b in bufs], jax.ShapeDtypeStruct((8, 128), F32)),
        input_output_aliases={p: 2 + p for p in range(n)},
        compiler_params=pltpu.CompilerParams(has_side_effects=DATAFLOW),
    )(*[_in_hbm(b) for b in bufs], after)
    return out[0], out[1], list(out[2:2 + n]), out[2 + n]


def gather_wait(send_sems, recv_sems, bufs, after, tag):
    n = len(bufs)

    def body(*refs):
        ins = refs[:n]
        send_sems, recv_sems = refs[n], refs[n + 1]
        x, y, c = _my_pos()
        for p in range(n):
            for j, chip in enumerate(_other_chips(x, y)):
                _ici_gather_copy(ins[p], p, j, (x, y), c, (*chip, c), send_sems, recv_sems).wait_send()
                _ici_gather_copy(ins[p], p, j, chip, c, (x, y, c), send_sems, recv_sems).wait_recv()

    out = pl.pallas_call(
        body, name="gather_wait_" + tag,
        in_specs=[HBM_SPEC] * n + [SEM_SPEC, SEM_SPEC, pl.BlockSpec(memory_space=pl.ANY)],
        out_specs=[HBM_SPEC] * n,
        out_shape=[pltpu.HBM(b.shape, b.dtype) for b in bufs],
        input_output_aliases={p: p for p in range(n)},
        compiler_params=pltpu.CompilerParams(has_side_effects=DATAFLOW),
    )(*bufs, send_sems, recv_sems, after)
    return list(out)


def forward_to_sibling(bufs):
    n = len(bufs)

    def body(*refs):
        outs = refs[n:2 * n]
        send_sems, recv_sems = refs[2 * n:]
        x, y, c = _my_pos()
        chips = _other_chips(x, y)
        sent = []
        for p in range(n):
            for j, chip in enumerate(chips):
                cp = _ici_gather_copy(outs[p], p, j, chip, c, (x, y, 1 - c), send_sems, recv_sems)
                cp.start()
                sent.append(cp)
        for p in range(n):
            for j, chip in enumerate(chips):
                _ici_gather_copy(outs[p], p, j, chip, 1 - c, (x, y, c), send_sems, recv_sems).wait_recv()
        for cp in sent:
            cp.wait_send()

    any_spec = pl.BlockSpec(memory_space=pl.ANY)
    return pl.pallas_call(
        body, name="forward_to_sibling",
        in_specs=[any_spec] * n, out_specs=[any_spec] * n,
        out_shape=[jax.ShapeDtypeStruct(b.shape, b.dtype) for b in bufs],
        scratch_shapes=[pltpu.SemaphoreType.DMA((3 * n,)), pltpu.SemaphoreType.DMA((3 * n,))],
        input_output_aliases={p: p for p in range(n)},
    )(*bufs)


def allreduce_small(v):
    r, c = v.shape

    def body(v_ref, o_ref, buf, send_sems, recv_sems):
        x, y, cc = _my_pos()
        me = 4 * x + 2 * y + cc
        buf[me] = v_ref[...]
        copies = []
        for k in range(1, 8):
            dx, dy, dc = (k >> 2) & 1, (k >> 1) & 1, k & 1
            peer = (x ^ dx, y ^ dy, cc ^ dc)
            cp = pltpu.make_async_remote_copy(
                src_ref=v_ref, dst_ref=buf.at[me], send_sem=send_sems.at[k - 1], recv_sem=recv_sems.at[k - 1],
                device_id=peer, device_id_type=MESH)
            cp.start()
            copies.append(cp)
        for cp in copies:
            cp.wait_recv()
        for cp in copies:
            cp.wait_send()
        acc = buf[0]
        for k in range(1, 8):
            acc = acc + buf[k]
        o_ref[...] = acc

    vm = pl.BlockSpec(memory_space=pltpu.VMEM)
    return pl.pallas_call(
        body, name="allreduce_small",
        in_specs=[vm], out_specs=vm,
        out_shape=jax.ShapeDtypeStruct((r, c), F32),
        scratch_shapes=[pltpu.VMEM((8, r, c), F32), pltpu.SemaphoreType.DMA((7,)), pltpu.SemaphoreType.DMA((7,))],
    )(v)


def _sibling_half_copy(grad, land, p, c, sibling, send_sems, recv_sems):
    return pltpu.make_async_remote_copy(
        src_ref=grad.at[:, 1 - c], dst_ref=land, send_sem=send_sems.at[p], recv_sem=recv_sems.at[p],
        device_id=sibling, device_id_type=MESH)


def sibling_halves_start(grads, tag):
    n = len(grads)
    lands = [lax.empty((g.shape[0],) + g.shape[2:], g.dtype) for g in grads]

    def body(*refs):
        ins, land = refs[:n], refs[n:2 * n]
        send_sems, recv_sems = refs[2 * n], refs[2 * n + 1]
        token = refs[4 * n + 2]
        x, y, c = _my_pos()
        for p in range(n):
            _sibling_half_copy(ins[p], land[p], p, c, (x, y, 1 - c), send_sems, recv_sems).start()
        token[...] = jnp.zeros_like(token)

    out = pl.pallas_call(
        body, name="sibling_halves_start_" + tag,
        in_specs=[HBM_SPEC] * (2 * n),
        out_specs=(SEM_SPEC, SEM_SPEC, *([HBM_SPEC] * (2 * n)), pl.BlockSpec(memory_space=pltpu.VMEM)),
        out_shape=(pltpu.SemaphoreType.DMA((n,)), pltpu.SemaphoreType.DMA((n,)),
                   *[pltpu.HBM(a.shape, a.dtype) for a in grads + lands], jax.ShapeDtypeStruct((8, 128), F32)),
        input_output_aliases={i: 2 + i for i in range(2 * n)},
        compiler_params=pltpu.CompilerParams(has_side_effects=DATAFLOW),
    )(*[_in_hbm(a) for a in grads + lands])
    return out[0], out[1], list(out[2:2 + n]), list(out[2 + n:2 + 2 * n]), out[2 + 2 * n]


def sibling_halves_wait(send_sems, recv_sems, grads, lands, after, tag):
    n = len(grads)

    def body(*refs):
        ins, land = refs[:n], refs[n:2 * n]
        send_sems, recv_sems = refs[2 * n], refs[2 * n + 1]
        x, y, c = _my_pos()
        for p in range(n):
            cp = _sibling_half_copy(ins[p], land[p], p, c, (x, y, 1 - c), send_sems, recv_sems)
            cp.wait_send()
            cp.wait_recv()

    out = pl.pallas_call(
        body, name="sibling_halves_wait_" + tag,
        in_specs=[HBM_SPEC] * (2 * n) + [SEM_SPEC, SEM_SPEC, pl.BlockSpec(memory_space=pl.ANY)],
        out_specs=[HBM_SPEC] * (2 * n),
        out_shape=[pltpu.HBM(a.shape, a.dtype) for a in grads + lands],
        input_output_aliases={i: i for i in range(2 * n)},
        compiler_params=pltpu.CompilerParams(has_side_effects=DATAFLOW),
    )(*grads, *lands, send_sems, recv_sems, after)
    return list(out[:n]), list(out[n:])


def _chip_partial_copy(part, land, p, j, chip, c, send_sems, recv_sems):
    return pltpu.make_async_remote_copy(
        src_ref=part.at[_chip_id(*chip)], dst_ref=land.at[j], send_sem=send_sems.at[3 * p + j],
        recv_sem=recv_sems.at[3 * p + j], device_id=(*chip, c), device_id_type=MESH)


def chip_partials_start(parts, tag):
    n = len(parts)
    lands = [lax.empty((N_CHIPS - 1,) + s.shape[1:], s.dtype) for s in parts]

    def body(*refs):
        ins, land = refs[:n], refs[n:2 * n]
        send_sems, recv_sems = refs[2 * n], refs[2 * n + 1]
        token = refs[4 * n + 2]
        x, y, c = _my_pos()
        for p in range(n):
            for j, chip in enumerate(_other_chips(x, y)):
                _chip_partial_copy(ins[p], land[p], p, j, chip, c, send_sems, recv_sems).start()
        token[...] = jnp.zeros_like(token)

    out = pl.pallas_call(
        body, name="chip_partials_start_" + tag,
        in_specs=[HBM_SPEC] * (2 * n),
        out_specs=(SEM_SPEC, SEM_SPEC, *([HBM_SPEC] * (2 * n)), pl.BlockSpec(memory_space=pltpu.VMEM)),
        out_shape=(pltpu.SemaphoreType.DMA((3 * n,)), pltpu.SemaphoreType.DMA((3 * n,)),
                   *[pltpu.HBM(a.shape, a.dtype) for a in parts + lands], jax.ShapeDtypeStruct((8, 128), F32)),
        input_output_aliases={i: 2 + i for i in range(2 * n)},
        compiler_params=pltpu.CompilerParams(has_side_effects=DATAFLOW),
    )(*[_in_hbm(a) for a in parts + lands])
    return out[0], out[1], list(out[2:2 + n]), list(out[2 + n:2 + 2 * n]), out[2 + 2 * n]


def chip_partials_wait(send_sems, recv_sems, parts, lands, after, tag):
    n = len(parts)

    def body(*refs):
        ins, land = refs[:n], refs[n:2 * n]
        send_sems, recv_sems = refs[2 * n], refs[2 * n + 1]
        x, y, c = _my_pos()
        for p in range(n):
            for j, chip in enumerate(_other_chips(x, y)):
                cp = _chip_partial_copy(ins[p], land[p], p, j, chip, c, send_sems, recv_sems)
                cp.wait_send()
                cp.wait_recv()

    out = pl.pallas_call(
        body, name="chip_partials_wait_" + tag,
        in_specs=[HBM_SPEC] * (2 * n) + [SEM_SPEC, SEM_SPEC, pl.BlockSpec(memory_space=pl.ANY)],
        out_specs=[HBM_SPEC] * (2 * n),
        out_shape=[pltpu.HBM(a.shape, a.dtype) for a in parts + lands],
        input_output_aliases={i: i for i in range(2 * n)},
        compiler_params=pltpu.CompilerParams(has_side_effects=DATAFLOW),
    )(*parts, *lands, send_sems, recv_sems, after)
    return list(out[n:])


def share_with_sibling(bufs):
    n = len(bufs)

    def body(*refs):
        outs = refs[n:2 * n]
        send_sems, recv_sems = refs[2 * n:]
        x, y, c = _my_pos()
        copies = []
        for p in range(n):
            cp = pltpu.make_async_remote_copy(
                src_ref=outs[p].at[c], dst_ref=outs[p].at[c], send_sem=send_sems.at[p], recv_sem=recv_sems.at[p],
                device_id=(x, y, 1 - c), device_id_type=MESH)
            cp.start()
            copies.append(cp)
        for p in range(n):
            pltpu.make_async_remote_copy(
                src_ref=outs[p].at[1 - c], dst_ref=outs[p].at[1 - c], send_sem=send_sems.at[p],
                recv_sem=recv_sems.at[p], device_id=(x, y, 1 - c), device_id_type=MESH).wait_recv()
        for cp in copies:
            cp.wait_send()

    any_spec = pl.BlockSpec(memory_space=pl.ANY)
    return pl.pallas_call(
        body, name="share_with_sibling",
        in_specs=[any_spec] * n, out_specs=[any_spec] * n,
        out_shape=[jax.ShapeDtypeStruct(b.shape, b.dtype) for b in bufs],
        scratch_shapes=[pltpu.SemaphoreType.DMA((n,)), pltpu.SemaphoreType.DMA((n,))],
        input_output_aliases={p: p for p in range(n)},
    )(*bufs)


def add_sibling(g, recv, half):
    _, _, r, c = g.shape
    tr = _tile(r, 256) if r % 256 == 0 else r

    def body(half_ref, g_ref, r_ref, o32_ref, o16_ref):
        s = g_ref[...] + r_ref[...]
        o32_ref[...] = s
        o16_ref[...] = _b(s)

    return pl.pallas_call(
        body, name="add_sibling",
        grid_spec=pltpu.PrefetchScalarGridSpec(
            num_scalar_prefetch=1, grid=(N_CHIPS, r // tr),
            in_specs=[pl.BlockSpec((None, None, tr, c), lambda k, i, hf: (k, hf[0], i, 0)),
                      pl.BlockSpec((None, tr, c), lambda k, i, hf: (k, i, 0))],
            out_specs=[pl.BlockSpec((None, tr, c), lambda k, i, hf: (k, i, 0)),
                       pl.BlockSpec((None, tr, c), lambda k, i, hf: (k, i, 0))]),
        out_shape=[jax.ShapeDtypeStruct((N_CHIPS, r, c), F32), jax.ShapeDtypeStruct((N_CHIPS, r, c), BF16)],
        compiler_params=_params("arbitrary", "arbitrary"),
    )(half, g, recv)


def add_chip_partials(p32, recv, pos):
    _, r, c = p32.shape
    tr = _tile(r, 256) if r % 256 == 0 else r

    def body(pos_ref, p_ref, r_ref, o_ref):
        acc = p_ref[...]
        for j in range(N_CHIPS - 1):
            acc = acc + r_ref[j].astype(F32)
        o_ref[...] = acc

    return pl.pallas_call(
        body, name="add_chip_partials",
        grid_spec=pltpu.PrefetchScalarGridSpec(
            num_scalar_prefetch=1, grid=(r // tr,),
            in_specs=[pl.BlockSpec((None, tr, c), lambda i, ps: (ps[0], i, 0)),
                      pl.BlockSpec((N_CHIPS - 1, tr, c), lambda i, ps: (0, i, 0))],
            out_specs=pl.BlockSpec((None, tr, c), lambda i, ps: (ps[1], i, 0))),
        out_shape=jax.ShapeDtypeStruct((2, r, c), F32),
        compiler_params=_params("arbitrary"),
    )(pos, p32, recv)


def cast_into_gather(w, pos, dep, row0=0, nrows=None):
    c = w.shape[1]
    nrows = w.shape[0] if nrows is None else nrows
    r = nrows // 2
    common = math.gcd(r, row0) if row0 else r
    tr = max(w for w in range(16, min(common, 512) + 1, 16) if common % w == 0)
    nt = r // tr

    def body(pos_ref, w_ref, dep_ref, o_ref):
        o_ref[...] = _b(w_ref[...])

    return pl.pallas_call(
        body, name="cast_into_gather",
        grid_spec=pltpu.PrefetchScalarGridSpec(
            num_scalar_prefetch=1, grid=(2, nt),
            in_specs=[pl.BlockSpec((tr, c), lambda hf, i, ps: (row0 // tr + hf * nt + i, 0)), DEP_SPEC],
            out_specs=pl.BlockSpec((None, None, tr, c), lambda hf, i, ps: (ps[0], hf, i, 0))),
        out_shape=jax.ShapeDtypeStruct((N_CHIPS, 2, r, c), BF16),
        compiler_params=_params("arbitrary", "arbitrary"),
    )(pos, w, dep)


def build_bias(rel, buckets):
    nb, nh = rel.shape

    def body(rel_ref, bk_ref, o_ref):
        bk = bk_ref[...]
        for h in range(nh):
            acc = jnp.zeros(bk.shape, F32)
            for b in range(nb):
                acc = jnp.where(bk == b, rel_ref[b, h], acc)
            o_ref[h] = acc

    return pl.pallas_call(
        body, name="build_bias",
        in_specs=[pl.BlockSpec(memory_space=pltpu.SMEM), pl.BlockSpec(memory_space=pltpu.VMEM)],
        out_specs=pl.BlockSpec(memory_space=pltpu.VMEM),
        out_shape=jax.ShapeDtypeStruct((nh,) + buckets.shape, F32),
        compiler_params=_params(),
    )(rel, buckets)


SMALL_ROWS = 256


def kernel(x, ffn_norm, ffn_w1, ffn_w3, ffn_w2, ssm_norm, ssm_w_in, ssm_conv_w, ssm_conv_b, ssm_dt_bias, ssm_a_log, ssm_d, ssm_gate_norm, ssm_w_out, kv_norm, w_kv, k_norm, attn_norm, w_q, q_norm, sinks, w_o, rel_bias, loss_target, m_ffn_norm, m_ffn_w1, m_ffn_w3, m_ffn_w2, m_ssm_norm, m_ssm_w_in, m_ssm_conv_w, m_ssm_conv_b, m_ssm_dt_bias, m_ssm_a_log, m_ssm_d, m_ssm_gate_norm, m_ssm_w_out, m_kv_norm, m_w_kv, m_k_norm, m_attn_norm, m_w_q, m_q_norm, m_sinks, m_w_o, m_rel_bias, v_ffn_norm, v_ffn_w1, v_ffn_w3, v_ffn_w2, v_ssm_norm, v_ssm_w_in, v_ssm_conv_w, v_ssm_conv_b, v_ssm_dt_bias, v_ssm_a_log, v_ssm_d, v_ssm_gate_norm, v_ssm_w_out, v_kv_norm, v_w_kv, v_k_norm, v_attn_norm, v_w_q, v_q_norm, v_sinks, v_w_o, v_rel_bias):
    weights = dict(ffn_norm=ffn_norm, ffn_w1=ffn_w1, ffn_w3=ffn_w3, ffn_w2=ffn_w2, ssm_norm=ssm_norm,
                   ssm_w_in=ssm_w_in, ssm_conv_w=ssm_conv_w, ssm_conv_b=ssm_conv_b, ssm_dt_bias=ssm_dt_bias,
                   ssm_a_log=ssm_a_log, ssm_d=ssm_d, ssm_gate_norm=ssm_gate_norm, ssm_w_out=ssm_w_out,
                   kv_norm=kv_norm, w_kv=w_kv, k_norm=k_norm, attn_norm=attn_norm, w_q=w_q, q_norm=q_norm,
                   sinks=sinks, w_o=w_o, rel_bias=rel_bias)
    m_in = dict(ffn_norm=m_ffn_norm, ffn_w1=m_ffn_w1, ffn_w3=m_ffn_w3, ffn_w2=m_ffn_w2, ssm_norm=m_ssm_norm,
                ssm_w_in=m_ssm_w_in, ssm_conv_w=m_ssm_conv_w, ssm_conv_b=m_ssm_conv_b, ssm_dt_bias=m_ssm_dt_bias,
                ssm_a_log=m_ssm_a_log, ssm_d=m_ssm_d, ssm_gate_norm=m_ssm_gate_norm, ssm_w_out=m_ssm_w_out,
                kv_norm=m_kv_norm, w_kv=m_w_kv, k_norm=m_k_norm, attn_norm=m_attn_norm, w_q=m_w_q, q_norm=m_q_norm,
                sinks=m_sinks, w_o=m_w_o, rel_bias=m_rel_bias)
    v_in = dict(ffn_norm=v_ffn_norm, ffn_w1=v_ffn_w1, ffn_w3=v_ffn_w3, ffn_w2=v_ffn_w2, ssm_norm=v_ssm_norm,
                ssm_w_in=v_ssm_w_in, ssm_conv_w=v_ssm_conv_w, ssm_conv_b=v_ssm_conv_b, ssm_dt_bias=v_ssm_dt_bias,
                ssm_a_log=v_ssm_a_log, ssm_d=v_ssm_d, ssm_gate_norm=v_ssm_gate_norm, ssm_w_out=v_ssm_w_out,
                kv_norm=v_kv_norm, w_kv=v_w_kv, k_norm=v_k_norm, attn_norm=v_attn_norm, w_q=v_w_q, q_norm=v_q_norm,
                sinks=v_sinks, w_o=v_w_o, rel_bias=v_rel_bias)
    return _step(x[0], loss_target[0], weights, m_in, v_in)


BIG = ("ffn_w1", "ffn_w3", "ffn_w2", "ssm_w_in", "ssm_w_out", "w_kv", "w_q", "w_o")
SMALL = (("ffn_norm", True), ("ssm_norm", True), ("ssm_conv_w", True), ("ssm_conv_b", True),
         ("ssm_gate_norm", True), ("ssm_dt_bias", False), ("ssm_a_log", False), ("ssm_d", False),
         ("kv_norm", False), ("k_norm", False), ("attn_norm", False), ("q_norm", False), ("sinks", False),
         ("rel_bias", False))


FFN_W = BIG[:3]


def _small_layout(weights):
    off, table = 0, {}
    for name, sharded in SMALL:
        shape = weights[name].shape
        full = shape[:-1] + (shape[-1] * N_CHIPS,) if sharded else shape
        n = int(np.prod(full))
        table[name] = (off, full, sharded)
        off += n
    assert off <= SMALL_ROWS * 128
    return table


def _place_small(values, table, chip, scale_mask):
    flat = jnp.zeros((SMALL_ROWS * 128,), F32)
    for name, (off, full, sharded) in table.items():
        if not sharded:
            continue
        v = values[name].astype(F32)
        lead = int(np.prod(full[:-1]))
        w = v.shape[-1]
        blk = jnp.zeros((lead, full[-1]), F32)
        blk = lax.dynamic_update_slice(blk, v.reshape(lead, w) * scale_mask, (0, chip * w))
        flat = lax.dynamic_update_slice(flat, blk.reshape(-1), (off,))
    return flat.reshape(SMALL_ROWS, 128)


def _take_small(mat, table, name):
    off, full, _ = table[name]
    n = int(np.prod(full))
    return mat.reshape(-1)[off:off + n].reshape(full)


def _step(x, target, weights, m_in, v_in):
    t, d = x.shape
    xi, yi, ci = lax.axis_index("x"), lax.axis_index("y"), lax.axis_index("c")
    chip = 2 * xi + yi
    pos_arr = jnp.stack([chip, ci]).astype(jnp.int32)
    half_arr = jnp.reshape(ci, (1,)).astype(jnp.int32)

    fs = weights["ffn_w1"].shape[-1]
    ffn_rows = {"ffn_w1": d, "ffn_w3": d, "ffn_w2": fs}
    w2d = {n: weights[n].reshape(-1, weights[n].shape[-1]) for n in BIG}
    mamba_w = ("ssm_w_in", "ssm_w_out")
    late_w = ("w_kv", "w_q", "w_o")
    fs_, fr_, fbufs, tok_f = gather_start(
        [cast_into_gather(w2d[n], pos_arr, pos_arr, 0, ffn_rows[n]) for n in FFN_W], pos_arr, "first")
    ms, mr, mbufs, tok_m = gather_start([cast_into_gather(w2d[n], pos_arr, tok_f) for n in mamba_w], tok_f, "mamba")
    ls, lr, lbufs, tok_l = gather_start(
        [cast_into_gather(w2d[n], pos_arr, tok_f, ffn_rows[n], 3 * ffn_rows[n]) for n in FFN_W]
        + [cast_into_gather(w2d[n], pos_arr, tok_f) for n in late_w], tok_m, "late")
    first = forward_to_sibling(gather_wait(fs_, fr_, fbufs, tok_l, "first"))
    no_dep = jnp.zeros((8, 128), F32)
    table = _small_layout(weights)
    south = (ci == 0).astype(F32)
    small = allreduce_small(_place_small(weights, table, chip, south))
    sp = {n: _take_small(small, table, n) if sh else weights[n] for n, sh in SMALL}

    ffn_first = [first[0].reshape(N_CHIPS, 1, d, fs), first[1].reshape(N_CHIPS, 1, d, fs),
                 first[2].reshape(N_CHIPS, 1, fs, d)]
    ffn_g = sp["ffn_norm"]
    h0 = x
    h1, a00, b00 = ffn_fwd(h0, ffn_g[0, 0].reshape(1, d), *ffn_first, 0, no_dep)
    gathered = dict(zip(mamba_w, forward_to_sibling(gather_wait(ms, mr, mbufs, h1, "mamba"))))
    n_in = weights["ssm_w_in"].shape[-1] * N_CHIPS
    di = weights["ssm_w_out"].shape[1] * N_CHIPS
    nheads = di // SSM_HEAD_DIM
    conv_dim = n_in - di - nheads
    w_in_full = jnp.moveaxis(gathered["ssm_w_in"].reshape(N_CHIPS, d, n_in // N_CHIPS), 0, 1).reshape(d, n_in)
    hpg = nheads // SSM_GROUPS

    def spread_heads(v):
        lead = v.shape[:-1]
        v = v.reshape(lead + (SSM_GROUPS, hpg))
        v = jnp.pad(v, [(0, 0)] * len(lead) + [(0, 0), (0, 128 - hpg)])
        return v.reshape(lead + (SSM_GROUPS * 128,))

    def gather_heads(v):
        lead = v.shape[:-1]
        return v.reshape(lead + (SSM_GROUPS, 128))[..., :hpg].reshape(lead + (nheads,))

    dt_col0 = di + conv_dim
    n_zx = dt_col0 + SSM_GROUPS * 128
    w_in = jnp.concatenate([w_in_full[:, :dt_col0], spread_heads(w_in_full[:, dt_col0:])], axis=1)
    w_out = gathered["ssm_w_out"].reshape(di, d)
    nkv = weights["w_kv"].shape[1] // (2 * ATT_HEAD_DIM)
    assert nkv == 2
    nh = weights["w_q"].shape[-1] // ATT_HEAD_DIM

    ssm_g = sp["ssm_norm"].reshape(1, d)
    cw = jnp.pad(sp["ssm_conv_w"].reshape(SSM_CONV, conv_dim), [(0, 8 - SSM_CONV), (0, 0)])
    cb = sp["ssm_conv_b"].reshape(1, conv_dim)
    gate_g = sp["ssm_gate_norm"].reshape(1, di)
    dt_bias = spread_heads(sp["ssm_dt_bias"].reshape(1, nheads))
    a_log = spread_heads(sp["ssm_a_log"].reshape(1, nheads))
    d_skip = spread_heads(sp["ssm_d"].reshape(1, nheads))
    kv_g = sp["kv_norm"].reshape(1, d)
    k_g = jnp.tile(sp["k_norm"].reshape(1, ATT_HEAD_DIM), (1, 2))
    attn_g = sp["attn_norm"].reshape(1, d)
    q_g = jnp.tile(sp["q_norm"].reshape(1, ATT_HEAD_DIM), (1, 2))
    sink_row = jnp.pad(sp["sinks"].reshape(1, nh), [(0, 0), (0, 128 - nh)])
    buckets = jnp.asarray(_t5_buckets())
    biasm = build_bias(sp["rel_bias"], buckets).reshape(nh * ATT_WINDOW, 2 * ATT_WINDOW)

    zx = norm_mm(h1, ssm_g, w_in)
    xc = conv_fwd(zx, cw, cb, di)
    y_ssd, states = ssd_fwd(xc, zx, dt_bias, a_log, d_skip, dt_col0)
    h2 = gate_out_fwd(h1, y_ssd, zx, gate_g, w_out)

    late = forward_to_sibling(gather_wait(ls, lr, lbufs, h2, "late"))
    ffn_rest = [late[0].reshape(N_CHIPS, 3, d, fs), late[1].reshape(N_CHIPS, 3, d, fs),
                late[2].reshape(N_CHIPS, 3, fs, d)]
    gathered.update(zip(late_w, late[3:]))
    wkv_heads = gathered["w_kv"].reshape(d, 2 * nkv, 1, ATT_HEAD_DIM)
    w_kvd = jnp.broadcast_to(wkv_heads, (d, 2 * nkv, 2, ATT_HEAD_DIM)).reshape(d, 4 * nkv * ATT_HEAD_DIM)
    wq = gathered["w_q"].reshape(d, -1)
    wo = gathered["w_o"].reshape(-1, d)

    def ffn_w(layer, idx):
        blk = 2 * layer + idx
        return (*ffn_first, 0) if blk == 0 else (*ffn_rest, blk - 1)

    h3, a01, b01 = ffn_fwd(h2, ffn_g[0, 1].reshape(1, d), *ffn_w(0, 1), no_dep)
    kvd = norm_mm(h3, kv_g, w_kvd)
    h4, a10, b10 = ffn_fwd(h3, ffn_g[1, 0].reshape(1, d), *ffn_w(1, 0), no_dep)
    qp = norm_mm(h4, attn_g, wq)
    h5 = attn_fwd(h4, qp, kvd, biasm, sink_row, q_g, k_g, wo)
    h6, a11, b11 = ffn_fwd(h5, ffn_g[1, 1].reshape(1, d), *ffn_w(1, 1), no_dep)
    loss_part, d6 = loss_head(h6, target)
    loss = lax.psum(loss_part[0, 0], ("x", "y", "c"))

    gfn = [[None, None], [None, None]]

    pending = {}

    def swap_start(pieces, tag):
        views = [g.reshape(N_CHIPS, 2, g.shape[1] // 2, g.shape[2]) for _, g in pieces]
        ss, rs, views, lands, token = sibling_halves_start(views, tag)
        pending[tag] = dict(keys=[k for k, _ in pieces], swap=(ss, rs, views, lands))
        return token

    def partials_start(tag, after):
        views, recv1 = sibling_halves_wait(*pending[tag]["swap"], after, tag)
        p32, p16 = zip(*[add_sibling(g, r, half_arr) for g, r in zip(views, recv1)])
        ss, rs, parts, lands, token = chip_partials_start(list(p16), tag)
        pending[tag].update(p32=p32, partials=(ss, rs, parts, lands))
        return token

    def ffn_back(h_in, dy, a_s, b_s, layer, idx, dep, wdep):
        dh, u, da, db, s, dg = ffn_bwd(h_in, dy, ffn_g[layer, idx].reshape(1, d), a_s, b_s, *ffn_w(layer, idx), dep)
        gfn[layer][idx] = dg
        return dh, [(("ffn_w1", layer, idx), wgrad_grouped_b(u, da, wdep)),
                    (("ffn_w3", layer, idx), wgrad_grouped_b(u, db, no_dep)),
                    (("ffn_w2", layer, idx), wgrad_grouped_a(s, dy, no_dep, 0.5))]

    d5, pieces = ffn_back(h5, d6, a11, b11, 1, 1, no_dep, no_dep)
    tok = swap_start(pieces, "ffn11")
    dqp, dkvd, o16, dbiasm, dsinks, dqg, dkg = attn_bwd(d5, qp, kvd, biasm, sink_row, q_g, k_g, wo, tok)
    tok = partials_start("ffn11", dqp)
    g_wo = wgrad(o16, d5)
    d4, u_q, g_attn_norm = norm_mm_bwd(h4, attn_g, wq, dqp, d5, tok)
    g_wq = wgrad(u_q, dqp)
    d3a, pieces = ffn_back(h3, d4, a10, b10, 1, 0, no_dep, no_dep)
    pieces += [(("w_o",), g_wo.reshape(N_CHIPS, -1, d)), (("w_q",), g_wq.reshape(N_CHIPS, d // N_CHIPS, -1))]
    tok = swap_start(pieces, "ffn10")
    d3, u_kv, g_kv_norm = norm_mm_bwd(h3, kv_g, w_kvd, dkvd, d3a, tok, 0.5)
    tok = partials_start("ffn10", d3)
    g_wkvd = wgrad(u_kv, dkvd)
    g_wkv = g_wkvd.reshape(d, 2 * nkv, 2, ATT_HEAD_DIM)[:, :, 0, :].reshape(d, 2 * nkv * ATT_HEAD_DIM)
    d2, pieces = ffn_back(h2, d3, a01, b01, 0, 1, tok, no_dep)
    pieces += [(("w_kv",), g_wkv.reshape(N_CHIPS, d // N_CHIPS, -1))]
    tok = swap_start(pieces, "ffn01")
    dzx, dy_ssd, yn16, g_gate = gate_out_bwd(d2, y_ssd, zx, gate_g, w_out, n_zx, tok)
    tok = partials_start("ffn01", dy_ssd)
    g_wout = wgrad(yn16, d2)
    dzx, dxs, dbm, dcm, g_dtb, g_alog, g_dsk = ssd_bwd(dzx, dy_ssd, xc, zx, states, dt_bias, a_log, d_skip, dt_col0)
    dzx, g_cw, g_cb = conv_bwd(dzx, zx, dxs, dbm, dcm, cw, cb, di)
    d1, u_in, g_ssm_norm = norm_mm_bwd(h1, ssm_g, w_in, dzx, d2, tok)
    g_win = wgrad(u_in, dzx)
    g_win_full = jnp.concatenate([g_win[:, :dt_col0], gather_heads(g_win[:, dt_col0:])], axis=1)
    pieces = [(("ssm_w_in",), jnp.moveaxis(g_win_full.reshape(d, N_CHIPS, n_in // N_CHIPS), 1, 0)),
              (("ssm_w_out",), g_wout.reshape(N_CHIPS, di // N_CHIPS, d))]
    tok = swap_start(pieces, "mamba")
    grad_x, u0, da0, db0, s0, gfn[0][0] = ffn_bwd(h0, d1, ffn_g[0, 0].reshape(1, d), a00, b00, *ffn_w(0, 0), tok)
    tok = partials_start("mamba", grad_x)
    g1 = wgrad_grouped_b(u0, da0, tok)
    tok = swap_start([(("ffn_w1", 0, 0), g1)], "ffn00a")
    g3 = wgrad_grouped_b(u0, db0, tok)
    tok = partials_start("ffn00a", g3) + swap_start([(("ffn_w3", 0, 0), g3)], "ffn00b")
    g2 = wgrad_grouped_a(s0, d1, tok, 0.5)
    tok = partials_start("ffn00b", g2) + swap_start([(("ffn_w2", 0, 0), g2)], "ffn00")
    g_relb = rel_bias_bwd(dbiasm.reshape(nh, ATT_WINDOW, 2 * ATT_WINDOW), buckets)

    reduced = {}

    def finish(tag, after):
        st = pending[tag]
        lands = chip_partials_wait(*st["partials"], after, tag)
        for k, p, r in zip(st["keys"], st["p32"], lands):
            reduced[k] = add_chip_partials(p, r, pos_arr)
        return reduced[st["keys"][-1]]

    last = finish("ffn10", finish("ffn11", tok))
    tok = partials_start("ffn00", last)
    last = finish("ffn00b", finish("ffn00a", finish("mamba", finish("ffn01", tok))))
    finish("ffn00", last)
    keys = list(reduced)
    shared = dict(zip(keys, share_with_sibling([reduced[k] for k in keys])))
    grads = {}
    for n in FFN_W:
        blocks = [shared[(n, l, i)].reshape(1, ffn_rows[n], -1) for l in range(2) for i in range(2)]
        grads[n] = jnp.concatenate(blocks, axis=0).reshape(weights[n].shape)
    for n in BIG[3:]:
        grads[n] = shared[(n,)].reshape(weights[n].shape)

    small_grads = {
        "ffn_norm": jnp.stack([jnp.stack([gfn[l][i].reshape(d) for i in range(2)]) for l in range(2)]),
        "ssm_norm": g_ssm_norm.reshape(1, d),
        "ssm_conv_w": g_cw[:SSM_CONV].reshape(1, SSM_CONV, conv_dim),
        "ssm_conv_b": g_cb.reshape(1, conv_dim),
        "ssm_gate_norm": g_gate.reshape(1, di),
        "ssm_dt_bias": gather_heads(g_dtb.reshape(1, -1)), "ssm_a_log": gather_heads(g_alog.reshape(1, -1)),
        "ssm_d": gather_heads(g_dsk.reshape(1, -1)),
        "kv_norm": g_kv_norm.reshape(d), "k_norm": dkg[0, :ATT_HEAD_DIM], "attn_norm": g_attn_norm.reshape(1, d),
        "q_norm": dqg[:, :ATT_HEAD_DIM], "sinks": dsinks[:, :nh], "rel_bias": g_relb[:, :nh],
    }
    flat = jnp.zeros((SMALL_ROWS * 128,), F32)
    for name, (off, fshape, _) in table.items():
        flat = lax.dynamic_update_slice(flat, small_grads[name].astype(F32).reshape(-1), (off,))
    small_sum = allreduce_small(flat.reshape(SMALL_ROWS, 128))
    for name, (off, fshape, sharded) in table.items():
        g = _take_small(small_sum, table, name)
        if sharded:
            w = weights[name].shape[-1]
            lead = int(np.prod(fshape[:-1]))
            g = lax.dynamic_slice(g.reshape(lead, fshape[-1]), (0, chip * w), (lead, w)).reshape(weights[name].shape)
        grads[name] = g.reshape(weights[name].shape)

    names = list(weights)
    deltas, new_m, new_v = {}, {}, {}
    small_names = [n for n, _ in SMALL]
    for n in BIG:
        shp = weights[n].shape
        v2 = lambda a: a.reshape(-1, shp[-1])
        dl, nm, nv = adamw(v2(weights[n]), v2(grads[n]), v2(m_in[n]), v2(v_in[n]))
        deltas[n], new_m[n], new_v[n] = dl.reshape(shp), nm.reshape(shp), nv.reshape(shp)
    sizes = [int(np.prod(weights[n].shape)) for n in small_names]
    tot = sum(sizes)
    rows = -(-tot // 128)
    rows = -(-rows // 8) * 8

    def pack(dct):
        flat = jnp.concatenate([dct[n].reshape(-1) for n in small_names])
        return jnp.pad(flat, (0, rows * 128 - tot), constant_values=1.0).reshape(rows, 128)

    dl, nm, nv = adamw(pack(weights), pack(grads), pack(m_in), pack(v_in))
    off = 0
    for n, sz in zip(small_names, sizes):
        shp = weights[n].shape
        take = lambda a: a.reshape(-1)[off:off + sz].reshape(shp)
        deltas[n], new_m[n], new_v[n] = take(dl), take(nm), take(nv)
        off += sz

    return (loss, grad_x[None], *[grads[n] for n in names], *[deltas[n] for n in names],
            *[new_m[n] for n in names], *[new_v[n] for n in names])
```

```python
import functools
import math

import jax
import jax.numpy as jnp
import numpy as np
from jax import lax
from jax.experimental import pallas as pl
from jax.experimental.pallas import tpu as pltpu

F32 = jnp.float32
BF16 = jnp.bfloat16
EPS = 1e-6
MESH = pl.DeviceIdType.MESH

SSM_HEAD_DIM = 64
SSM_GROUPS = 4
SSM_STATE = 128
SSM_CONV = 4
SSM_CHUNK = 256
ATT_HEAD_DIM = 64
ATT_WINDOW = 128
REL_BUCKETS = 32
N_CHIPS = 4

ADAM_LR = 0.001
ADAM_B1 = 0.9
ADAM_B2 = 0.999
ADAM_EPS = 1e-08
ADAM_WD = 0.01
ADAM_STEP = 10

VMEM_LIMIT_BYTES = 56 * 1024 * 1024
NEG = -1e30


DEP_SPEC = pl.BlockSpec(memory_space=pl.ANY)


def _params(*sem):
    return pltpu.CompilerParams(dimension_semantics=sem if sem else None, vmem_limit_bytes=VMEM_LIMIT_BYTES)


def _dot(a, b):
    return jnp.dot(a, b, preferred_element_type=F32)


def _dot_nt(a, b):
    return lax.dot_general(a, b, (((1,), (1,)), ((), ())), preferred_element_type=F32)


def _dot_tn(a, b):
    return lax.dot_general(a, b, (((0,), (0,)), ((), ())), preferred_element_type=F32)


def _b(x):
    return x.astype(BF16)


@jax.custom_vjp
def _bmm(a, b):
    return _dot(_b(a), _b(b))


def _bmm_fwd(a, b):
    return _bmm(a, b), (a, b)


def _bmm_bwd(res, g):
    a, b = res
    g16 = _b(g)
    return _dot_nt(g16, _b(b)).astype(a.dtype), _dot_tn(_b(a), g16).astype(b.dtype)


_bmm.defvjp(_bmm_fwd, _bmm_bwd)


@jax.custom_vjp
def _bmm_nt(a, b):
    return _dot_nt(_b(a), _b(b))


def _bmm_nt_fwd(a, b):
    return _bmm_nt(a, b), (a, b)


def _bmm_nt_bwd(res, g):
    a, b = res
    g16 = _b(g)
    return _dot(g16, _b(b)).astype(a.dtype), _dot_tn(g16, _b(a)).astype(b.dtype)


_bmm_nt.defvjp(_bmm_nt_fwd, _bmm_nt_bwd)


@jax.custom_vjp
def _bmm_tn(a, b):
    return _dot_tn(_b(a), _b(b))


def _bmm_tn_fwd(a, b):
    return _bmm_tn(a, b), (a, b)


def _bmm_tn_bwd(res, g):
    a, b = res
    g16 = _b(g)
    return _dot_nt(_b(b), g16).astype(a.dtype), _dot(_b(a), g16).astype(b.dtype)


_bmm_tn.defvjp(_bmm_tn_fwd, _bmm_tn_bwd)


def _split3(x):
    hi = _b(x)
    r = x - hi.astype(F32)
    mid = _b(r)
    lo = _b(r - mid.astype(F32))
    return hi, mid, lo


def _x_left_raw(m, x):
    hi, mid, lo = _split3(x)
    return _dot(m, hi) + _dot(m, mid) + _dot(m, lo)


def _x_left_t_raw(m, x):
    hi, mid, lo = _split3(x)
    return _dot_tn(m, hi) + _dot_tn(m, mid) + _dot_tn(m, lo)


def _x_right_raw(x, m):
    hi, mid, lo = _split3(x)
    return _dot(hi, m) + _dot(mid, m) + _dot(lo, m)


def _x_right_t_raw(x, m):
    hi, mid, lo = _split3(x)
    return _dot_nt(hi, m) + _dot_nt(mid, m) + _dot_nt(lo, m)


@jax.custom_vjp
def _xleft(m, x):
    return _x_left_raw(m, x)


_xleft.defvjp(lambda m, x: (_x_left_raw(m, x), m),
              lambda m, g: (jnp.zeros_like(m), _x_left_t_raw(m, g)))


@jax.custom_vjp
def _xright(x, m):
    return _x_right_raw(x, m)


_xright.defvjp(lambda x, m: (_x_right_raw(x, m), m),
               lambda m, g: (_x_right_t_raw(g, m), jnp.zeros_like(m)))


def _sigmoid(x):
    return 1.0 / (1.0 + jnp.exp(-x))


def _silu(x):
    return x * _sigmoid(x)


def _softplus(x):
    return jnp.maximum(x, 0.0) + jnp.log(1.0 + jnp.exp(-jnp.abs(x)))


def _rms(x):
    return x * lax.rsqrt(jnp.mean(x * x, axis=-1, keepdims=True) + EPS)


def _iota(shape, dim):
    return lax.broadcasted_iota(jnp.int32, shape, dim)


def _blockdiag64(n):
    return jnp.where(_iota((n, n), 0) // 64 == _iota((n, n), 1) // 64, 1.0, 0.0).astype(BF16)


def _group64_rms(x, seg_sum):
    ms = seg_sum(x * x) * (1.0 / 64.0)
    return x * lax.rsqrt(ms + EPS)


def _fold64(x):
    ax = x.ndim - 1
    w = x.shape[ax]
    lo = (_iota(x.shape, ax) % 128) < 64
    return x + jnp.where(lo, pltpu.roll(x, w - 64, ax), pltpu.roll(x, 64, ax))


def _tile(n, want):
    t = min(n, want)
    assert n % t == 0, (n, t)
    return t


def _lane_tile(n, cap=1536):
    if n <= cap:
        return n
    return max(w for w in range(128, cap + 1, 128) if n % w == 0)


def ffn_fwd(h, g, w1, w3, w2, blk, dep):
    t, d = h.shape
    nk, fs = w1.shape[0], w1.shape[-1]
    tm = _tile(t, 512)

    def body(h_ref, g_ref, w1_ref, w3_ref, w2_ref, dep_ref, o_ref, a_ref, b_ref, u_scr, acc):
        k = pl.program_id(1)

        @pl.when(k == 0)
        def _():
            u_scr[...] = _b(_rms(h_ref[...]) * g_ref[...])
            acc[...] = jnp.zeros_like(acc)

        u = u_scr[...]
        a = _dot(u, w1_ref[...])
        b = _dot(u, w3_ref[...])
        a_ref[...] = _b(a)
        b_ref[...] = _b(b)
        acc[...] += _dot(_b(_silu(a) * b), w2_ref[...])

        @pl.when(k == nk - 1)
        def _():
            o_ref[...] = h_ref[...] + 0.5 * acc[...]

    wspec = lambda r, c: pl.BlockSpec((None, None, r, c), lambda i, k: (k, blk, 0, 0))
    return pl.pallas_call(
        body, name="ffn_fwd",
        grid=(t // tm, nk),
        in_specs=[pl.BlockSpec((tm, d), lambda i, k: (i, 0)), pl.BlockSpec((1, d), lambda i, k: (0, 0)),
                  wspec(d, fs), wspec(d, fs), wspec(fs, d), DEP_SPEC],
        out_specs=[pl.BlockSpec((tm, d), lambda i, k: (i, 0)),
                   pl.BlockSpec((None, tm, fs), lambda i, k: (k, i, 0)),
                   pl.BlockSpec((None, tm, fs), lambda i, k: (k, i, 0))],
        out_shape=[jax.ShapeDtypeStruct((t, d), F32), jax.ShapeDtypeStruct((nk, t, fs), BF16),
                   jax.ShapeDtypeStruct((nk, t, fs), BF16)],
        scratch_shapes=[pltpu.VMEM((tm, d), BF16), pltpu.VMEM((tm, d), F32)],
        compiler_params=_params("arbitrary", "arbitrary"),
    )(h, g, w1, w3, w2, dep)


def ffn_bwd(h, dy, g, a_s, b_s, w1, w3, w2, blk, dep):
    t, d = h.shape
    nk, fs = w1.shape[0], w1.shape[-1]
    tm = _tile(t, 512)

    def body(h_ref, dy_ref, g_ref, a_ref, b_ref, w1_ref, w3_ref, w2_ref, dep_ref,
             dh_ref, u_ref, da_ref, db_ref, s_ref, dg_ref, dyh_scr, du_acc, da0, db0, da1, db1):
        i, k = pl.program_id(0), pl.program_id(1)

        @pl.when(k == 0)
        def _():
            dyh_scr[...] = _b(0.5 * dy_ref[...])
            du_acc[...] = jnp.zeros_like(du_acc)

        @pl.when((k == 0) & (i == 0))
        def _():
            dg_ref[...] = jnp.zeros_like(dg_ref)

        def step(prev, cur):
            if prev is not None:
                du_acc[...] += _dot_nt(prev[0][...], w1_ref[...]) + _dot_nt(prev[1][...], w3_ref[...])
            if cur is not None:
                ds = _dot_nt(dyh_scr[...], w2_ref[...])
                a = a_ref[...].astype(F32)
                b = b_ref[...].astype(F32)
                sig = _sigmoid(a)
                sl = a * sig
                s_ref[...] = _b(sl * b)
                da = _b(ds * b * (sig * (1.0 + a * (1.0 - sig))))
                db = _b(ds * sl)
                da_ref[...] = da
                db_ref[...] = db
                cur[0][...] = da
                cur[1][...] = db

        even, odd = (da0, db0), (da1, db1)

        @pl.when(k == 0)
        def _():
            step(None, even)

        @pl.when((k > 0) & (k < nk) & (k % 2 == 1))
        def _():
            step(even, odd)

        @pl.when((k > 0) & (k < nk) & (k % 2 == 0))
        def _():
            step(odd, even)

        @pl.when(k == nk)
        def _():
            step(odd if nk % 2 == 0 else even, None)
            hh = h_ref[...]
            rstd = lax.rsqrt(jnp.mean(hh * hh, axis=-1, keepdims=True) + EPS)
            xh = hh * rstd
            gg = g_ref[...]
            u_ref[...] = _b(xh * gg)
            du = du_acc[...]
            dg_ref[...] += jnp.sum(du * xh, axis=0, keepdims=True)
            dxh = du * gg
            dh_ref[...] = dy_ref[...] + rstd * (dxh - xh * jnp.mean(dxh * xh, axis=-1, keepdims=True))

    cur = lambda k: jnp.minimum(k, nk - 1)
    prv = lambda k: jnp.maximum(k - 1, 0)
    wcur = lambda r, c: pl.BlockSpec((None, None, r, c), lambda i, k: (cur(k), blk, 0, 0))
    wprv = lambda r, c: pl.BlockSpec((None, None, r, c), lambda i, k: (prv(k), blk, 0, 0))
    tok = pl.BlockSpec((tm, d), lambda i, k: (i, 0))
    hid = pl.BlockSpec((None, tm, fs), lambda i, k: (cur(k), i, 0))
    return pl.pallas_call(
        body, name="ffn_bwd",
        grid=(t // tm, nk + 1),
        in_specs=[tok, tok, pl.BlockSpec((1, d), lambda i, k: (0, 0)), hid, hid, wprv(d, fs), wprv(d, fs), wcur(fs, d),
                  DEP_SPEC],
        out_specs=[tok, tok, hid, hid, hid, pl.BlockSpec((1, d), lambda i, k: (0, 0))],
        out_shape=[jax.ShapeDtypeStruct((t, d), F32), jax.ShapeDtypeStruct((t, d), BF16),
                   jax.ShapeDtypeStruct((nk, t, fs), BF16), jax.ShapeDtypeStruct((nk, t, fs), BF16),
                   jax.ShapeDtypeStruct((nk, t, fs), BF16), jax.ShapeDtypeStruct((1, d), F32)],
        scratch_shapes=[pltpu.VMEM((tm, d), BF16), pltpu.VMEM((tm, d), F32)] + [pltpu.VMEM((tm, fs), BF16)] * 4,
        compiler_params=_params("arbitrary", "arbitrary"),
    )(h, dy, g, a_s, b_s, w1, w3, w2, dep)


def wgrad_grouped_b(a, bs, dep, scale=1.0):
    t, m = a.shape
    ng, _, n = bs.shape
    tk = _tile(t, 2048)

    def body(a_ref, b_ref, dep_ref, o_ref):
        j = pl.program_id(1)

        @pl.when(j == 0)
        def _():
            o_ref[...] = jnp.zeros_like(o_ref)

        o_ref[...] += _dot_tn(_b(a_ref[...]), _b(b_ref[...]))

        if scale != 1.0:
            @pl.when(j == pl.num_programs(1) - 1)
            def _():
                o_ref[...] = o_ref[...] * scale

    return pl.pallas_call(
        body, name="wgrad_gb",
        grid=(ng, t // tk),
        in_specs=[pl.BlockSpec((tk, m), lambda k, j: (j, 0)), pl.BlockSpec((None, tk, n), lambda k, j: (k, j, 0)),
                  DEP_SPEC],
        out_specs=pl.BlockSpec((None, m, n), lambda k, j: (k, 0, 0)),
        out_shape=jax.ShapeDtypeStruct((ng, m, n), F32),
        compiler_params=_params("arbitrary", "arbitrary"),
    )(a, bs, dep)


def wgrad_grouped_a(as_, b, dep, scale=1.0):
    ng, t, m = as_.shape
    n = b.shape[1]
    tk = _tile(t, 2048)

    def body(a_ref, b_ref, dep_ref, o_ref):
        j = pl.program_id(1)

        @pl.when(j == 0)
        def _():
            o_ref[...] = jnp.zeros_like(o_ref)

        o_ref[...] += _dot_tn(_b(a_ref[...]), _b(b_ref[...]))

        if scale != 1.0:
            @pl.when(j == pl.num_programs(1) - 1)
            def _():
                o_ref[...] = o_ref[...] * scale

    return pl.pallas_call(
        body, name="wgrad_ga",
        grid=(ng, t // tk),
        in_specs=[pl.BlockSpec((None, tk, m), lambda k, j: (k, j, 0)), pl.BlockSpec((tk, n), lambda k, j: (j, 0)),
                  DEP_SPEC],
        out_specs=pl.BlockSpec((None, m, n), lambda k, j: (k, 0, 0)),
        out_shape=jax.ShapeDtypeStruct((ng, m, n), F32),
        compiler_params=_params("arbitrary", "arbitrary"),
    )(as_, b, dep)


def wgrad(a, b):
    t, m = a.shape
    n = b.shape[1]
    tk = _tile(t, 1024)
    tn = _lane_tile(n, 1536 if m <= 1024 else 512)

    def body(a_ref, b_ref, o_ref):
        @pl.when(pl.program_id(1) == 0)
        def _():
            o_ref[...] = jnp.zeros_like(o_ref)

        o_ref[...] += _dot_tn(_b(a_ref[...]), _b(b_ref[...]))

    return pl.pallas_call(
        body, name="wgrad",
        grid=(n // tn, t // tk),
        in_specs=[pl.BlockSpec((tk, m), lambda c, j: (j, 0)), pl.BlockSpec((tk, tn), lambda c, j: (j, c))],
        out_specs=pl.BlockSpec((m, tn), lambda c, j: (0, c)),
        out_shape=jax.ShapeDtypeStruct((m, n), F32),
        compiler_params=_params("arbitrary", "arbitrary"),
    )(a, b)


def norm_mm(h, g, w):
    t, d = h.shape
    n = w.shape[1]
    tm = _tile(t, 1024)
    tn = _lane_tile(n)

    def body(h_ref, g_ref, w_ref, o_ref, u_scr):
        @pl.when(pl.program_id(1) == 0)
        def _():
            u_scr[...] = _b(_rms(h_ref[...]) * g_ref[...])

        o_ref[...] = _dot(u_scr[...], w_ref[...])

    return pl.pallas_call(
        body, name="norm_mm",
        grid=(t // tm, n // tn),
        in_specs=[pl.BlockSpec((tm, d), lambda i, j: (i, 0)), pl.BlockSpec((1, d), lambda i, j: (0, 0)),
                  pl.BlockSpec((d, tn), lambda i, j: (0, j))],
        out_specs=pl.BlockSpec((tm, tn), lambda i, j: (i, j)),
        out_shape=jax.ShapeDtypeStruct((t, n), F32),
        scratch_shapes=[pltpu.VMEM((tm, d), BF16)],
        compiler_params=_params("arbitrary", "arbitrary"),
    )(h, g, w)


def norm_mm_bwd(h, g, w, dout, dres, dep, scale=1.0):
    t, d = h.shape
    n = w.shape[1]
    tm = _tile(t, 512)
    tn = _lane_tile(n)
    nj = n // tn

    def body(h_ref, g_ref, w_ref, do_ref, dr_ref, dep_ref, dh_ref, u_ref, dg_ref, du_acc):
        i, j = pl.program_id(0), pl.program_id(1)

        @pl.when(j == 0)
        def _():
            du_acc[...] = jnp.zeros_like(du_acc)

        @pl.when((j == 0) & (i == 0))
        def _():
            dg_ref[...] = jnp.zeros_like(dg_ref)

        du_acc[...] += _dot_nt(_b(do_ref[...]), w_ref[...])

        @pl.when(j == nj - 1)
        def _():
            hh = h_ref[...]
            rstd = lax.rsqrt(jnp.mean(hh * hh, axis=-1, keepdims=True) + EPS)
            xh = hh * rstd
            gg = g_ref[...]
            u_ref[...] = _b(xh * gg)
            du = du_acc[...] * scale
            dg_ref[...] += jnp.sum(du * xh, axis=0, keepdims=True)
            dxh = du * gg
            dh_ref[...] = dr_ref[...] + rstd * (dxh - xh * jnp.mean(dxh * xh, axis=-1, keepdims=True))

    tok = pl.BlockSpec((tm, d), lambda i, j: (i, 0))
    return pl.pallas_call(
        body, name="norm_mm_bwd",
        grid=(t // tm, nj),
        in_specs=[tok, pl.BlockSpec((1, d), lambda i, j: (0, 0)), pl.BlockSpec((d, tn), lambda i, j: (0, j)),
                  pl.BlockSpec((tm, tn), lambda i, j: (i, j)), tok, DEP_SPEC],
        out_specs=[tok, tok, pl.BlockSpec((1, d), lambda i, j: (0, 0))],
        out_shape=[jax.ShapeDtypeStruct((t, d), F32), jax.ShapeDtypeStruct((t, d), BF16),
                   jax.ShapeDtypeStruct((1, d), F32)],
        scratch_shapes=[pltpu.VMEM((tm, d), F32)],
        compiler_params=_params("arbitrary", "arbitrary"),
    )(h, g, w, dout, dres, dep)


CONV_COLS = 512


CONV_ROWS = 64


def _conv_pre(ext, w, b, r0, n):
    return (b + w[0:1] * ext[pl.ds(5 + r0, n), :] + w[1:2] * ext[pl.ds(6 + r0, n), :]
            + w[2:3] * ext[pl.ds(7 + r0, n), :] + w[3:4] * ext[pl.ds(8 + r0, n), :])


def conv_fwd(zx, cw, cb, col0):
    t = zx.shape[0]
    c = cw.shape[1]
    tm = _tile(t, 512)
    cb0 = col0 // CONV_COLS

    rc = _tile(tm, CONV_ROWS)

    def body(x_ref, w_ref, b_ref, o_ref, ext):
        @pl.when(pl.program_id(1) == 0)
        def _():
            ext[0:8, :] = jnp.zeros((8, CONV_COLS), F32)

        ext[8:, :] = x_ref[...]
        w, b = w_ref[...], b_ref[...]
        for r0 in range(0, tm, rc):
            o_ref[r0:r0 + rc, :] = _silu(_conv_pre(ext, w, b, r0, rc))
        ext[0:8, :] = ext[tm:tm + 8, :]

    return pl.pallas_call(
        body, name="conv_fwd",
        grid=(c // CONV_COLS, t // tm),
        in_specs=[pl.BlockSpec((tm, CONV_COLS), lambda j, i: (i, cb0 + j)),
                  pl.BlockSpec((8, CONV_COLS), lambda j, i: (0, j)), pl.BlockSpec((1, CONV_COLS), lambda j, i: (0, j))],
        out_specs=pl.BlockSpec((tm, CONV_COLS), lambda j, i: (i, j)),
        out_shape=jax.ShapeDtypeStruct((t, c), F32),
        scratch_shapes=[pltpu.VMEM((tm + 8, CONV_COLS), F32)],
        compiler_params=_params("arbitrary", "arbitrary"),
    )(zx, cw, cb)


def conv_bwd(dzx, zx, dxs, dbm, dcm, cw, cb, col0):
    t = zx.shape[0]
    c = cw.shape[1]
    tm = _tile(t, 512)
    nt = t // tm
    cb0 = col0 // CONV_COLS
    nxs = dxs.shape[1] // CONV_COLS
    hb = tm // 8

    rc = _tile(tm, CONV_ROWS)

    def body(dzx_ref, x_ref, xh_ref, dxs_ref, db_ref, dc_ref, w_ref, b_ref, o_ref, dw_ref, dbias_ref, ext, gy):
        j, i = pl.program_id(0), pl.program_id(1)
        ri = nt - 1 - i

        @pl.when(i == 0)
        def _():
            gy[tm:tm + 8, :] = jnp.zeros((8, CONV_COLS), F32)
            dw_ref[...] = jnp.zeros_like(dw_ref)
            dbias_ref[...] = jnp.zeros_like(dbias_ref)

        ext[0:8, :] = jnp.where(ri > 0, xh_ref[...], 0.0)
        ext[8:, :] = x_ref[...]
        w, b = w_ref[...], b_ref[...]
        dw = [jnp.zeros((1, CONV_COLS), F32) for _ in range(SSM_CONV)]
        dbias = jnp.zeros((1, CONV_COLS), F32)
        for r0 in range(0, tm, rc):
            rows = pl.ds(r0, rc)
            win = [ext[pl.ds(5 + tap + r0, rc), :] for tap in range(SSM_CONV)]
            y = b + w[0:1] * win[0] + w[1:2] * win[1] + w[2:3] * win[2] + w[3:4] * win[3]
            sig = _sigmoid(y)
            dout = jnp.where(j < nxs, dxs_ref[rows, :], jnp.where(j == nxs, db_ref[rows, :], dc_ref[rows, :]))
            g = dout * (sig * (1.0 + y * (1.0 - sig)))
            gy[rows, :] = g
            dbias = dbias + jnp.sum(g, axis=0, keepdims=True)
            for tap in range(SSM_CONV):
                dw[tap] = dw[tap] + jnp.sum(g * win[tap], axis=0, keepdims=True)
        for r0 in range(0, tm, rc):
            o_ref[r0:r0 + rc, :] = _b(w[0:1] * gy[pl.ds(r0 + 3, rc), :] + w[1:2] * gy[pl.ds(r0 + 2, rc), :]
                                      + w[2:3] * gy[pl.ds(r0 + 1, rc), :] + w[3:4] * gy[pl.ds(r0, rc), :])
        gy[tm:tm + 8, :] = gy[0:8, :]
        for tap in range(SSM_CONV):
            dw_ref[tap:tap + 1, :] += dw[tap]
        dbias_ref[...] += dbias

    return pl.pallas_call(
        body, name="conv_bwd",
        grid=(c // CONV_COLS, nt),
        in_specs=[pl.BlockSpec(memory_space=pl.ANY),
                  pl.BlockSpec((tm, CONV_COLS), lambda j, i: (nt - 1 - i, cb0 + j)),
                  pl.BlockSpec((8, CONV_COLS), lambda j, i: (jnp.maximum((nt - 1 - i) * hb - 1, 0), cb0 + j)),
                  pl.BlockSpec((tm, CONV_COLS), lambda j, i: (nt - 1 - i, jnp.minimum(j, nxs - 1))),
                  pl.BlockSpec((tm, CONV_COLS), lambda j, i: (nt - 1 - i, 0)),
                  pl.BlockSpec((tm, CONV_COLS), lambda j, i: (nt - 1 - i, 0)),
                  pl.BlockSpec((8, CONV_COLS), lambda j, i: (0, j)), pl.BlockSpec((1, CONV_COLS), lambda j, i: (0, j))],
        out_specs=[pl.BlockSpec((tm, CONV_COLS), lambda j, i: (nt - 1 - i, cb0 + j)),
                   pl.BlockSpec((8, CONV_COLS), lambda j, i: (0, j)), pl.BlockSpec((1, CONV_COLS), lambda j, i: (0, j))],
        out_shape=[jax.ShapeDtypeStruct(dzx.shape, dzx.dtype), jax.ShapeDtypeStruct((8, c), F32),
                   jax.ShapeDtypeStruct((1, c), F32)],
        scratch_shapes=[pltpu.VMEM((tm + 8, CONV_COLS), F32), pltpu.VMEM((tm + 8, CONV_COLS), F32)],
        input_output_aliases={0: 0},
        compiler_params=_params("arbitrary", "arbitrary"),
    )(dzx, zx, zx, dxs, dbm, dcm, cw, cb)


def _ssd_group(xs, bg, cg, dtraw, s0, bias, alog, dsk):
    L = xs.shape[0]
    causal = _iota((L, L), 0) >= _iota((L, L), 1)
    tril = jnp.where(causal, 1.0, 0.0).astype(BF16)
    dt = _softplus(dtraw + bias)
    a = -jnp.exp(alog)
    acum = _xleft(tril, dt * a)
    acum_t = acum.T
    dt_t = dt.T
    cb = _bmm_nt(cg, bg)
    lo = _iota((L, 128), 1) < 64
    lo_row = _iota((1, 128), 1) < 64
    lo_col = _iota((128, 1), 0) < 64
    alast = acum[L - 1:L, :]
    ys, s1s = [], []
    for q in range(4):
        xp = xs[:, q * 128:(q + 1) * 128]
        sp = s0[q * 128:(q + 1) * 128, :]
        yd, ec, wc, el = [], [], [], []
        for j in range(2):
            r = 2 * q + j
            ac = acum[:, r:r + 1]
            decay = jnp.exp(jnp.where(causal, ac - acum_t[r:r + 1, :], NEG))
            yd.append(_bmm(cb * decay * dt_t[r:r + 1, :], xp))
            ec.append(jnp.exp(ac))
            al = alast[:, r:r + 1]
            wc.append(jnp.exp(al - ac) * dt[:, r:r + 1])
            el.append(jnp.exp(al))
        y_off = _bmm_nt(cg, sp) * jnp.where(lo, ec[0], ec[1])
        dsel = jnp.where(lo_row, dsk[:, 2 * q:2 * q + 1], dsk[:, 2 * q + 1:2 * q + 2])
        ys.append(jnp.where(lo, yd[0], yd[1]) + y_off + dsel * xp)
        xw = xp * jnp.where(lo, wc[0], wc[1])
        s1s.append(sp * jnp.where(lo_col, el[0], el[1]) + _bmm_tn(xw, bg))
    return jnp.concatenate(ys, axis=1), jnp.concatenate(s1s, axis=0)


def ssd_fwd(xc, zx, bias, alog, dsk, dt_col0):
    t = xc.shape[0]
    L = _tile(t, SSM_CHUNK)
    nc = t // L
    g = SSM_GROUPS
    dtb = dt_col0 // 512

    def body(xs_ref, b_ref, c_ref, dt_ref, bias_ref, alog_ref, dsk_ref, y_ref, st_ref, state):
        @pl.when(pl.program_id(0) == 0)
        def _():
            state[...] = jnp.zeros_like(state)

        for gi in range(g):
            lane = slice(gi * 128, (gi + 1) * 128)
            wide = slice(gi * 512, (gi + 1) * 512)
            s0 = state[gi]
            st_ref[gi] = s0
            y, s1 = _ssd_group(xs_ref[:, wide], b_ref[:, lane], c_ref[:, lane], dt_ref[:, lane], s0,
                               bias_ref[:, lane], alog_ref[:, lane], dsk_ref[:, lane])
            y_ref[:, wide] = y
            state[gi] = s1

    vec = pl.BlockSpec((1, 512), lambda c: (0, 0))
    return pl.pallas_call(
        body, name="ssd_fwd",
        grid=(nc,),
        in_specs=[pl.BlockSpec((L, 2048), lambda c: (c, 0)), pl.BlockSpec((L, 512), lambda c: (c, 4)),
                  pl.BlockSpec((L, 512), lambda c: (c, 5)), pl.BlockSpec((L, 512), lambda c: (c, dtb)), vec, vec, vec],
        out_specs=[pl.BlockSpec((L, 2048), lambda c: (c, 0)),
                   pl.BlockSpec((None, g, 512, 128), lambda c: (c, 0, 0, 0))],
        out_shape=[jax.ShapeDtypeStruct((t, 2048), F32), jax.ShapeDtypeStruct((nc, g, 512, 128), F32)],
        scratch_shapes=[pltpu.VMEM((g, 512, 128), F32)],
        compiler_params=_params("arbitrary"),
    )(xc, xc, xc, zx, bias, alog, dsk)


def ssd_bwd(dzx, dy, xc, zx, states, bias, alog, dsk, dt_col0):
    t = xc.shape[0]
    L = _tile(t, SSM_CHUNK)
    nc = t // L
    g = SSM_GROUPS
    dtb = dt_col0 // 512

    def body(dzx_ref, dy_ref, xs_ref, b_ref, c_ref, dt_ref, st_ref, bias_ref, alog_ref, dsk_ref,
             ddt_ref, dxs_ref, db_ref, dc_ref, dbias_ref, dalog_ref, ddsk_ref, dstate):
        @pl.when(pl.program_id(0) == 0)
        def _():
            dstate[...] = jnp.zeros_like(dstate)
            dbias_ref[...] = jnp.zeros_like(dbias_ref)
            dalog_ref[...] = jnp.zeros_like(dalog_ref)
            ddsk_ref[...] = jnp.zeros_like(ddsk_ref)

        for gi in range(g):
            lane = slice(gi * 128, (gi + 1) * 128)
            wide = slice(gi * 512, (gi + 1) * 512)
            _, vjp = jax.vjp(_ssd_group, xs_ref[:, wide], b_ref[:, lane], c_ref[:, lane], dt_ref[:, lane], st_ref[gi],
                             bias_ref[:, lane], alog_ref[:, lane], dsk_ref[:, lane])
            dxs, db, dc, ddt, ds0, dbias, dalog, ddsk = vjp((dy_ref[:, wide], dstate[gi]))
            dxs_ref[:, wide] = dxs
            db_ref[:, lane] = db
            dc_ref[:, lane] = dc
            ddt_ref[:, lane] = _b(ddt)
            dstate[gi] = ds0
            dbias_ref[:, lane] += dbias
            dalog_ref[:, lane] += dalog
            ddsk_ref[:, lane] += ddsk

    rc = lambda c: nc - 1 - c
    vec = pl.BlockSpec((1, 512), lambda c: (0, 0))
    return pl.pallas_call(
        body, name="ssd_bwd",
        grid=(nc,),
        in_specs=[pl.BlockSpec(memory_space=pl.ANY),
                  pl.BlockSpec((L, 2048), lambda c: (rc(c), 0)), pl.BlockSpec((L, 2048), lambda c: (rc(c), 0)),
                  pl.BlockSpec((L, 512), lambda c: (rc(c), 4)), pl.BlockSpec((L, 512), lambda c: (rc(c), 5)),
                  pl.BlockSpec((L, 512), lambda c: (rc(c), dtb)),
                  pl.BlockSpec((None, g, 512, 128), lambda c: (rc(c), 0, 0, 0)), vec, vec, vec],
        out_specs=[pl.BlockSpec((L, 512), lambda c: (rc(c), dtb)), pl.BlockSpec((L, 2048), lambda c: (rc(c), 0)),
                   pl.BlockSpec((L, 512), lambda c: (rc(c), 0)), pl.BlockSpec((L, 512), lambda c: (rc(c), 0)),
                   vec, vec, vec],
        out_shape=[jax.ShapeDtypeStruct(dzx.shape, dzx.dtype), jax.ShapeDtypeStruct((t, 2048), F32),
                   jax.ShapeDtypeStruct((t, 512), F32), jax.ShapeDtypeStruct((t, 512), F32),
                   jax.ShapeDtypeStruct((1, 512), F32), jax.ShapeDtypeStruct((1, 512), F32),
                   jax.ShapeDtypeStruct((1, 512), F32)],
        scratch_shapes=[pltpu.VMEM((g, 512, 128), F32)],
        input_output_aliases={0: 0},
        compiler_params=_params("arbitrary"),
    )(dzx, dy, xc, xc, xc, zx, states, bias, alog, dsk)


def _gate_tile(y, z, gn):
    gated = y * _silu(z)
    parts = [_rms(gated[:, k * 512:(k + 1) * 512]) for k in range(SSM_GROUPS)]
    return jnp.concatenate(parts, axis=1) * gn


def gate_out_fwd(h, y, zx, gn, w_out):
    t, d = h.shape
    di = y.shape[1]
    tm = _tile(t, 256)

    def body(h_ref, y_ref, z_ref, gn_ref, w_ref, o_ref):
        yn = _gate_tile(y_ref[...], z_ref[...], gn_ref[...])
        o_ref[...] = h_ref[...] + _dot(_b(yn), w_ref[...])

    return pl.pallas_call(
        body, name="gate_out_fwd",
        grid=(t // tm,),
        in_specs=[pl.BlockSpec((tm, d), lambda i: (i, 0)), pl.BlockSpec((tm, di), lambda i: (i, 0)),
                  pl.BlockSpec((tm, di), lambda i: (i, 0)), pl.BlockSpec((1, di), lambda i: (0, 0)),
                  pl.BlockSpec((di, d), lambda i: (0, 0))],
        out_specs=pl.BlockSpec((tm, d), lambda i: (i, 0)),
        out_shape=jax.ShapeDtypeStruct((t, d), F32),
        compiler_params=_params("arbitrary"),
    )(h, y, zx, gn, w_out)


def gate_out_bwd(dy, y, zx, gn, w_out, n_zx, dep):
    t, d = dy.shape
    di = y.shape[1]
    tm = _tile(t, 256)

    def body(dy_ref, y_ref, z_ref, gn_ref, w_ref, dep_ref, dz_ref, dys_ref, yn_ref, dgn_ref):
        @pl.when(pl.program_id(0) == 0)
        def _():
            dgn_ref[...] = jnp.zeros_like(dgn_ref)

        yn, vjp = jax.vjp(_gate_tile, y_ref[...], z_ref[...], gn_ref[...])
        dyn = _dot_nt(_b(dy_ref[...]), w_ref[...])
        dys, dz, dgn = vjp(dyn)
        yn_ref[...] = _b(yn)
        dys_ref[...] = dys
        dz_ref[...] = _b(dz)
        dgn_ref[...] += dgn

    return pl.pallas_call(
        body, name="gate_out_bwd",
        grid=(t // tm,),
        in_specs=[pl.BlockSpec((tm, d), lambda i: (i, 0)), pl.BlockSpec((tm, di), lambda i: (i, 0)),
                  pl.BlockSpec((tm, di), lambda i: (i, 0)), pl.BlockSpec((1, di), lambda i: (0, 0)),
                  pl.BlockSpec((di, d), lambda i: (0, 0)), DEP_SPEC],
        out_specs=[pl.BlockSpec((tm, di), lambda i: (i, 0)), pl.BlockSpec((tm, di), lambda i: (i, 0)),
                   pl.BlockSpec((tm, di), lambda i: (i, 0)), pl.BlockSpec((1, di), lambda i: (0, 0))],
        out_shape=[jax.ShapeDtypeStruct((t, n_zx), BF16), jax.ShapeDtypeStruct((t, di), F32),
                   jax.ShapeDtypeStruct((t, di), BF16), jax.ShapeDtypeStruct((1, di), F32)],
        compiler_params=_params("arbitrary"),
    )(dy, y, zx, gn, w_out, dep)


def _attn_block(qp, kvp, kvc, biasm, sinks, qg, kg, w_o, first):
    nq = qp.shape[0]
    n_pairs = qp.shape[1] // 128
    hk = n_pairs
    rows = hk * nq
    seg = functools.partial(_xright, m=_blockdiag64(128))
    scale = ATT_HEAD_DIM ** -0.5
    qi = (_iota((rows, 2 * nq), 0) % nq) + nq
    kj = _iota((rows, 2 * nq), 1)
    dist = qi - kj
    valid = (dist >= 0) & (dist < ATT_WINDOW) & (jnp.logical_not(first) | (kj >= nq))
    lo = _iota((nq, 128), 1) < 64
    kv = jnp.concatenate([kvp, kvc], axis=0)
    outs = [None] * n_pairs
    for kvh in range(2):
        kn = _group64_rms(kv[:, kvh * 128:(kvh + 1) * 128], seg) * kg
        vv = kv[:, 256 + kvh * 128:256 + (kvh + 1) * 128]
        pairs = range(kvh * hk // 2, (kvh + 1) * hk // 2)
        qs, sk = [], []
        for p in pairs:
            qn = _group64_rms(qp[:, p * 128:(p + 1) * 128], seg) * qg
            qs += [jnp.where(lo, qn, 0.0), jnp.where(lo, 0.0, qn)]
            sk += [jnp.broadcast_to(sinks[:, h:h + 1], (nq, 1)) for h in (2 * p, 2 * p + 1)]
        sink = jnp.concatenate(sk, axis=0)
        s = _bmm_nt(jnp.concatenate(qs, axis=0), kn) * scale + biasm[kvh * rows:(kvh + 1) * rows]
        s = jnp.where(valid, s, NEG)
        m = lax.stop_gradient(jnp.maximum(jnp.max(s, axis=-1, keepdims=True), sink))
        pexp = jnp.exp(s - m)
        den = jnp.sum(pexp, axis=-1, keepdims=True) + jnp.exp(sink - m)
        o = _bmm(pexp * (1.0 / den), vv)
        for n, p in enumerate(pairs):
            outs[p] = jnp.where(lo, o[2 * n * nq:(2 * n + 1) * nq], o[(2 * n + 1) * nq:(2 * n + 2) * nq])
    o = jnp.concatenate(outs, axis=1)
    return _bmm(o, w_o), o


def attn_fwd(h, qp, kvd, biasm, sinks, qg, kg, w_o):
    t, d = h.shape
    nq = ATT_WINDOW
    nb = t // nq
    nh = qp.shape[1] // ATT_HEAD_DIM

    def body(h_ref, q_ref, kp_ref, kc_ref, bias_ref, s_ref, qg_ref, kg_ref, w_ref, o_ref):
        out, _ = _attn_block(q_ref[...], kp_ref[...], kc_ref[...], bias_ref[...], s_ref[...], qg_ref[...],
                             kg_ref[...], w_ref[...], pl.program_id(0) == 0)
        o_ref[...] = h_ref[...] + out

    vec = pl.BlockSpec((1, 128), lambda i: (0, 0))
    return pl.pallas_call(
        body, name="attn_fwd",
        grid=(nb,),
        in_specs=[pl.BlockSpec((nq, d), lambda i: (i, 0)), pl.BlockSpec((nq, nh * 64), lambda i: (i, 0)),
                  pl.BlockSpec((nq, 512), lambda i: (jnp.maximum(i - 1, 0), 0)),
                  pl.BlockSpec((nq, 512), lambda i: (i, 0)),
                  pl.BlockSpec((nh * nq, 2 * nq), lambda i: (0, 0)), vec, vec, vec,
                  pl.BlockSpec((nh * 64, d), lambda i: (0, 0))],
        out_specs=pl.BlockSpec((nq, d), lambda i: (i, 0)),
        out_shape=jax.ShapeDtypeStruct((t, d), F32),
        compiler_params=_params("arbitrary"),
    )(h, qp, kvd, kvd, biasm, sinks, qg, kg, w_o)


def attn_bwd(dy, qp, kvd, biasm, sinks, qg, kg, w_o, dep):
    t, d = dy.shape
    nq = ATT_WINDOW
    nb = t // nq
    nh = qp.shape[1] // ATT_HEAD_DIM

    def body(dy_ref, q_ref, kp_ref, kc_ref, bias_ref, s_ref, qg_ref, kg_ref, w_ref, dep_ref,
             dq_ref, dkv_ref, o_ref, dbias_ref, ds_ref, dqg_ref, dkg_ref, carry):
        i = pl.program_id(0)

        @pl.when(i == 0)
        def _():
            carry[...] = jnp.zeros_like(carry)
            dbias_ref[...] = jnp.zeros_like(dbias_ref)
            ds_ref[...] = jnp.zeros_like(ds_ref)
            dqg_ref[...] = jnp.zeros_like(dqg_ref)
            dkg_ref[...] = jnp.zeros_like(dkg_ref)

        @pl.when(i < nb)
        def _():
            fn = functools.partial(_attn_block, w_o=w_ref[...], first=(i == 0))
            (_, o), vjp = jax.vjp(fn, q_ref[...], kp_ref[...], kc_ref[...], bias_ref[...], s_ref[...],
                                  qg_ref[...], kg_ref[...])
            dq, dkp, dkc, dbias, dsk, dqg, dkg = vjp((dy_ref[...], jnp.zeros((nq, nh * 64), F32)))
            dq_ref[...] = _b(dq)
            o_ref[...] = _b(o)
            dkv_ref[...] = _b(_fold64(carry[...] + dkp))
            carry[...] = dkc
            dbias_ref[...] += dbias
            ds_ref[...] += dsk
            dqg_ref[...] += _fold64(dqg)
            dkg_ref[...] += _fold64(dkg)

        @pl.when(i == nb)
        def _():
            dkv_ref[...] = _b(_fold64(carry[...]))

    cl = lambda i: jnp.minimum(i, nb - 1)
    vec = pl.BlockSpec((1, 128), lambda i: (0, 0))
    return pl.pallas_call(
        body, name="attn_bwd",
        grid=(nb + 1,),
        in_specs=[pl.BlockSpec((nq, d), lambda i: (cl(i), 0)), pl.BlockSpec((nq, nh * 64), lambda i: (cl(i), 0)),
                  pl.BlockSpec((nq, 512), lambda i: (jnp.maximum(cl(i) - 1, 0), 0)),
                  pl.BlockSpec((nq, 512), lambda i: (cl(i), 0)),
                  pl.BlockSpec((nh * nq, 2 * nq), lambda i: (0, 0)), vec, vec, vec,
                  pl.BlockSpec((nh * 64, d), lambda i: (0, 0)), DEP_SPEC],
        out_specs=[pl.BlockSpec((nq, nh * 64), lambda i: (cl(i), 0)),
                   pl.BlockSpec((nq, 512), lambda i: (jnp.maximum(i - 1, 0), 0)),
                   pl.BlockSpec((nq, nh * 64), lambda i: (cl(i), 0)),
                   pl.BlockSpec((nh * nq, 2 * nq), lambda i: (0, 0)), vec, vec, vec],
        out_shape=[jax.ShapeDtypeStruct((t, nh * 64), BF16), jax.ShapeDtypeStruct((t, 512), BF16),
                   jax.ShapeDtypeStruct((t, nh * 64), BF16), jax.ShapeDtypeStruct((nh * nq, 2 * nq), F32),
                   jax.ShapeDtypeStruct((1, 128), F32), jax.ShapeDtypeStruct((1, 128), F32),
                   jax.ShapeDtypeStruct((1, 128), F32)],
        scratch_shapes=[pltpu.VMEM((nq, 512), F32)],
        compiler_params=_params("arbitrary"),
    )(dy, qp, kvd, kvd, biasm, sinks, qg, kg, w_o, dep)


def _t5_buckets():
    nq = ATT_WINDOW
    dist = (np.arange(nq)[:, None] + nq) - np.arange(2 * nq)[None, :]
    n = np.maximum(dist, 0)
    max_exact = REL_BUCKETS // 2
    nf = np.maximum(n, 1).astype(np.float32)
    large = max_exact + (np.log(nf / max_exact) / math.log(ATT_WINDOW / max_exact)
                         * (REL_BUCKETS - max_exact)).astype(np.int32)
    large = np.minimum(large, REL_BUCKETS - 1)
    return np.where(n < max_exact, n, large).astype(np.int32)


def rel_bias_bwd(dbias, buckets):
    nh = dbias.shape[0]

    def body(db_ref, bk_ref, o_ref):
        bk = bk_ref[...]
        lane = _iota((1, 128), 1)
        row = _iota((REL_BUCKETS, 128), 0)
        acc = jnp.zeros((REL_BUCKETS, 128), F32)
        for h in range(nh):
            dbh = db_ref[h]
            for b in range(REL_BUCKETS):
                v = jnp.sum(jnp.where(bk == b, dbh, 0.0))
                acc = acc + jnp.where((row == b) & (lane == h), v, 0.0)
        o_ref[...] = acc

    return pl.pallas_call(
        body, name="rel_bias_bwd",
        out_shape=jax.ShapeDtypeStruct((REL_BUCKETS, 128), F32),
        compiler_params=_params(),
    )(dbias, buckets)


def loss_head(y, target):
    t, d = y.shape
    tm = _tile(t, 512)

    def body(y_ref, t_ref, l_ref, dy_ref):
        @pl.when(pl.program_id(0) == 0)
        def _():
            l_ref[...] = jnp.zeros_like(l_ref)

        e = y_ref[...] - t_ref[...]
        l_ref[...] += 0.5 * jnp.sum(jnp.mean(e * e, axis=-1, keepdims=True), axis=0, keepdims=True)
        dy_ref[...] = e * (1.0 / d)

    return pl.pallas_call(
        body, name="loss_head",
        grid=(t // tm,),
        in_specs=[pl.BlockSpec((tm, d), lambda i: (i, 0)), pl.BlockSpec((tm, d), lambda i: (i, 0))],
        out_specs=[pl.BlockSpec((1, 1), lambda i: (0, 0)), pl.BlockSpec((tm, d), lambda i: (i, 0))],
        out_shape=[jax.ShapeDtypeStruct((1, 1), F32), jax.ShapeDtypeStruct((t, d), F32)],
        compiler_params=_params("arbitrary"),
    )(y, target)


def adamw(w, g, m, v):
    r, c = w.shape
    tr = r if r <= 512 else _tile(r, 256)

    def body(w_ref, g_ref, m_ref, v_ref, d_ref, nm_ref, nv_ref):
        gg = g_ref[...]
        nm = ADAM_B1 * m_ref[...] + (1.0 - ADAM_B1) * gg
        nv = ADAM_B2 * v_ref[...] + (1.0 - ADAM_B2) * (gg * gg)
        m_hat = nm / (1.0 - ADAM_B1 ** ADAM_STEP)
        v_hat = nv / (1.0 - ADAM_B2 ** ADAM_STEP)
        d_ref[...] = -ADAM_LR * (m_hat / (jnp.sqrt(v_hat) + ADAM_EPS) + ADAM_WD * w_ref[...])
        nm_ref[...] = nm
        nv_ref[...] = nv

    spec = pl.BlockSpec((tr, c), lambda i: (i, 0))
    shp = jax.ShapeDtypeStruct((r, c), F32)
    return pl.pallas_call(
        body, name="adamw",
        grid=(r // tr,),
        in_specs=[spec] * 4, out_specs=[spec] * 3, out_shape=[shp] * 3,
        compiler_params=_params("arbitrary"),
    )(w, g, m, v)


def _my_pos():
    return lax.axis_index("x"), lax.axis_index("y"), lax.axis_index("c")


def _other_chips(x, y):
    return [(1 - x, y), (x, 1 - y), (1 - x, 1 - y)]


def _chip_id(x, y):
    return 2 * x + y


HBM_SPEC = pl.BlockSpec(memory_space=pltpu.HBM)
SEM_SPEC = pl.BlockSpec(memory_space=pltpu.SEMAPHORE)
DATAFLOW = pltpu.SideEffectType.DATAFLOW_SIDE_EFFECTING


def _in_hbm(a):
    return pltpu.with_memory_space_constraint(a, pltpu.HBM)


def _ici_gather_copy(buf, p, j, chip, c, to, send_sems, recv_sems):
    blk = buf.at[_chip_id(*chip), c]
    return pltpu.make_async_remote_copy(
        src_ref=blk, dst_ref=blk, send_sem=send_sems.at[3 * p + j], recv_sem=recv_sems.at[3 * p + j],
        device_id=to, device_id_type=MESH)


def gather_start(bufs, after, tag):
    n = len(bufs)

    def body(*refs):
        ins = refs[:n]
        send_sems, recv_sems = refs[n + 1], refs[n + 2]
        token = refs[2 * n + 3]
        x, y, c = _my_pos()
        for p in range(n):
            for j, chip in enumerate(_other_chips(x, y)):
                _ici_gather_copy(ins[p], p, j, (x, y), c, (*chip, c), send_sems, recv_sems).start()
        token[...] = jnp.zeros_like(token)

    out = pl.pallas_call(
        body, name="gather_start_" + tag,
        in_specs=[HBM_SPEC] * n + [DEP_SPEC],
        out_specs=(SEM_SPEC, SEM_SPEC, *([HBM_SPEC] * n), pl.BlockSpec(memory_space=pltpu.VMEM)),
        out_shape=(pltpu.SemaphoreType.DMA((3 * n,)), pltpu.SemaphoreType.DMA((3 * n,)),
                   *[pltpu.HBM(b.shape, b.dtype) for b in bufs], jax.ShapeDtypeStruct((8, 128), F32)),
        input_output_aliases={p: 2 + p for p in range(n)},
        compiler_params=pltpu.CompilerParams(has_side_effects=DATAFLOW),
    )(*[_in_hbm(b) for b in bufs], after)
    return out[0], out[1], list(out[2:2 + n]), out[2 + n]


def gather_wait(send_sems, recv_sems, bufs, after, tag):
    n = len(bufs)

    def body(*refs):
        ins = refs[:n]
        send_sems, recv_sems = refs[n], refs[n + 1]
        x, y, c = _my_pos()
        for p in range(n):
            for j, chip in enumerate(_other_chips(x, y)):
                _ici_gather_copy(ins[p], p, j, (x, y), c, (*chip, c), send_sems, recv_sems).wait_send()
                _ici_gather_copy(ins[p], p, j, chip, c, (x, y, c), send_sems, recv_sems).wait_recv()

    out = pl.pallas_call(
        body, name="gather_wait_" + tag,
        in_specs=[HBM_SPEC] * n + [SEM_SPEC, SEM_SPEC, pl.BlockSpec(memory_space=pl.ANY)],
        out_specs=[HBM_SPEC] * n,
        out_shape=[pltpu.HBM(b.shape, b.dtype) for b in bufs],
        input_output_aliases={p: p for p in range(n)},
        compiler_params=pltpu.CompilerParams(has_side_effects=DATAFLOW),
    )(*bufs, send_sems, recv_sems, after)
    return list(out)


def forward_to_sibling(bufs):
    n = len(bufs)

    def body(*refs):
        outs = refs[n:2 * n]
        send_sems, recv_sems = refs[2 * n:]
        x, y, c = _my_pos()
        chips = _other_chips(x, y)
        sent = []
        for p in range(n):
            for j, chip in enumerate(chips):
                cp = _ici_gather_copy(outs[p], p, j, chip, c, (x, y, 1 - c), send_sems, recv_sems)
                cp.start()
                sent.append(cp)
        for p in range(n):
            for j, chip in enumerate(chips):
                _ici_gather_copy(outs[p], p, j, chip, 1 - c, (x, y, c), send_sems, recv_sems).wait_recv()
        for cp in sent:
            cp.wait_send()

    any_spec = pl.BlockSpec(memory_space=pl.ANY)
    return pl.pallas_call(
        body, name="forward_to_sibling",
        in_specs=[any_spec] * n, out_specs=[any_spec] * n,
        out_shape=[jax.ShapeDtypeStruct(b.shape, b.dtype) for b in bufs],
        scratch_shapes=[pltpu.SemaphoreType.DMA((3 * n,)), pltpu.SemaphoreType.DMA((3 * n,))],
        input_output_aliases={p: p for p in range(n)},
    )(*bufs)


def allreduce_small(v):
    r, c = v.shape

    def body(v_ref, o_ref, buf, send_sems, recv_sems):
        x, y, cc = _my_pos()
        me = 4 * x + 2 * y + cc
        buf[me] = v_ref[...]
        copies = []
        for k in range(1, 8):
            dx, dy, dc = (k >> 2) & 1, (k >> 1) & 1, k & 1
            peer = (x ^ dx, y ^ dy, cc ^ dc)
            cp = pltpu.make_async_remote_copy(
                src_ref=v_ref, dst_ref=buf.at[me], send_sem=send_sems.at[k - 1], recv_sem=recv_sems.at[k - 1],
                device_id=peer, device_id_type=MESH)
            cp.start()
            copies.append(cp)
        for cp in copies:
            cp.wait_recv()
        for cp in copies:
            cp.wait_send()
        acc = buf[0]
        for k in range(1, 8):
            acc = acc + buf[k]
        o_ref[...] = acc

    vm = pl.BlockSpec(memory_space=pltpu.VMEM)
    return pl.pallas_call(
        body, name="allreduce_small",
        in_specs=[vm], out_specs=vm,
        out_shape=jax.ShapeDtypeStruct((r, c), F32),
        scratch_shapes=[pltpu.VMEM((8, r, c), F32), pltpu.SemaphoreType.DMA((7,)), pltpu.SemaphoreType.DMA((7,))],
    )(v)


def _sibling_half_copy(grad, land, p, c, sibling, send_sems, recv_sems):
    return pltpu.make_async_remote_copy(
        src_ref=grad.at[:, 1 - c], dst_ref=land, send_sem=send_sems.at[p], recv_sem=recv_sems.at[p],
        device_id=sibling, device_id_type=MESH)


def sibling_halves_start(grads, tag):
    n = len(grads)
    lands = [lax.empty((g.shape[0],) + g.shape[2:], g.dtype) for g in grads]

    def body(*refs):
        ins, land = refs[:n], refs[n:2 * n]
        send_sems, recv_sems = refs[2 * n], refs[2 * n + 1]
        token = refs[4 * n + 2]
        x, y, c = _my_pos()
        for p in range(n):
            _sibling_half_copy(ins[p], land[p], p, c, (x, y, 1 - c), send_sems, recv_sems).start()
        token[...] = jnp.zeros_like(token)

    out = pl.pallas_call(
        body, name="sibling_halves_start_" + tag,
        in_specs=[HBM_SPEC] * (2 * n),
        out_specs=(SEM_SPEC, SEM_SPEC, *([HBM_SPEC] * (2 * n)), pl.BlockSpec(memory_space=pltpu.VMEM)),
        out_shape=(pltpu.SemaphoreType.DMA((n,)), pltpu.SemaphoreType.DMA((n,)),
                   *[pltpu.HBM(a.shape, a.dtype) for a in grads + lands], jax.ShapeDtypeStruct((8, 128), F32)),
        input_output_aliases={i: 2 + i for i in range(2 * n)},
        compiler_params=pltpu.CompilerParams(has_side_effects=DATAFLOW),
    )(*[_in_hbm(a) for a in grads + lands])
    return out[0], out[1], list(out[2:2 + n]), list(out[2 + n:2 + 2 * n]), out[2 + 2 * n]


def sibling_halves_wait(send_sems, recv_sems, grads, lands, after, tag):
    n = len(grads)

    def body(*refs):
        ins, land = refs[:n], refs[n:2 * n]
        send_sems, recv_sems = refs[2 * n], refs[2 * n + 1]
        x, y, c = _my_pos()
        for p in range(n):
            cp = _sibling_half_copy(ins[p], land[p], p, c, (x, y, 1 - c), send_sems, recv_sems)
            cp.wait_send()
            cp.wait_recv()

    out = pl.pallas_call(
        body, name="sibling_halves_wait_" + tag,
        in_specs=[HBM_SPEC] * (2 * n) + [SEM_SPEC, SEM_SPEC, pl.BlockSpec(memory_space=pl.ANY)],
        out_specs=[HBM_SPEC] * (2 * n),
        out_shape=[pltpu.HBM(a.shape, a.dtype) for a in grads + lands],
        input_output_aliases={i: i for i in range(2 * n)},
        compiler_params=pltpu.CompilerParams(has_side_effects=DATAFLOW),
    )(*grads, *lands, send_sems, recv_sems, after)
    return list(out[:n]), list(out[n:])


def _chip_partial_copy(part, land, p, j, chip, c, send_sems, recv_sems):
    return pltpu.make_async_remote_copy(
        src_ref=part.at[_chip_id(*chip)], dst_ref=land.at[j], send_sem=send_sems.at[3 * p + j],
        recv_sem=recv_sems.at[3 * p + j], device_id=(*chip, c), device_id_type=MESH)


def chip_partials_start(parts, tag):
    n = len(parts)
    lands = [lax.empty((N_CHIPS - 1,) + s.shape[1:], s.dtype) for s in parts]

    def body(*refs):
        ins, land = refs[:n], refs[n:2 * n]
        send_sems, recv_sems = refs[2 * n], refs[2 * n + 1]
        token = refs[4 * n + 2]
        x, y, c = _my_pos()
        for p in range(n):
            for j, chip in enumerate(_other_chips(x, y)):
                _chip_partial_copy(ins[p], land[p], p, j, chip, c, send_sems, recv_sems).start()
        token[...] = jnp.zeros_like(token)

    out = pl.pallas_call(
        body, name="chip_partials_start_" + tag,
        in_specs=[HBM_SPEC] * (2 * n),
        out_specs=(SEM_SPEC, SEM_SPEC, *([HBM_SPEC] * (2 * n)), pl.BlockSpec(memory_space=pltpu.VMEM)),
        out_shape=(pltpu.SemaphoreType.DMA((3 * n,)), pltpu.SemaphoreType.DMA((3 * n,)),
                   *[pltpu.HBM(a.shape, a.dtype) for a in parts + lands], jax.ShapeDtypeStruct((8, 128), F32)),
        input_output_aliases={i: 2 + i for i in range(2 * n)},
        compiler_params=pltpu.CompilerParams(has_side_effects=DATAFLOW),
    )(*[_in_hbm(a) for a in parts + lands])
    return out[0], out[1], list(out[2:2 + n]), list(out[2 + n:2 + 2 * n]), out[2 + 2 * n]


def chip_partials_wait(send_sems, recv_sems, parts, lands, after, tag):
    n = len(parts)

    def body(*refs):
        ins, land = refs[:n], refs[n:2 * n]
        send_sems, recv_sems = refs[2 * n], refs[2 * n + 1]
        x, y, c = _my_pos()
        for p in range(n):
            for j, chip in enumerate(_other_chips(x, y)):
                cp = _chip_partial_copy(ins[p], land[p], p, j, chip, c, send_sems, recv_sems)
                cp.wait_send()
                cp.wait_recv()

    out = pl.pallas_call(
        body, name="chip_partials_wait_" + tag,
        in_specs=[HBM_SPEC] * (2 * n) + [SEM_SPEC, SEM_SPEC, pl.BlockSpec(memory_space=pl.ANY)],
        out_specs=[HBM_SPEC] * (2 * n),
        out_shape=[pltpu.HBM(a.shape, a.dtype) for a in parts + lands],
        input_output_aliases={i: i for i in range(2 * n)},
        compiler_params=pltpu.CompilerParams(has_side_effects=DATAFLOW),
    )(*parts, *lands, send_sems, recv_sems, after)
    return list(out[n:])


def share_with_sibling(bufs):
    n = len(bufs)

    def body(*refs):
        outs = refs[n:2 * n]
        send_sems, recv_sems = refs[2 * n:]
        x, y, c = _my_pos()
        copies = []
        for p in range(n):
            cp = pltpu.make_async_remote_copy(
                src_ref=outs[p].at[c], dst_ref=outs[p].at[c], send_sem=send_sems.at[p], recv_sem=recv_sems.at[p],
                device_id=(x, y, 1 - c), device_id_type=MESH)
            cp.start()
            copies.append(cp)
        for p in range(n):
            pltpu.make_async_remote_copy(
                src_ref=outs[p].at[1 - c], dst_ref=outs[p].at[1 - c], send_sem=send_sems.at[p],
                recv_sem=recv_sems.at[p], device_id=(x, y, 1 - c), device_id_type=MESH).wait_recv()
        for cp in copies:
            cp.wait_send()

    any_spec = pl.BlockSpec(memory_space=pl.ANY)
    return pl.pallas_call(
        body, name="share_with_sibling",
        in_specs=[any_spec] * n, out_specs=[any_spec] * n,
        out_shape=[jax.ShapeDtypeStruct(b.shape, b.dtype) for b in bufs],
        scratch_shapes=[pltpu.SemaphoreType.DMA((n,)), pltpu.SemaphoreType.DMA((n,))],
        input_output_aliases={p: p for p in range(n)},
    )(*bufs)


def add_sibling(g, recv, half):
    _, _, r, c = g.shape
    tr = _tile(r, 256) if r % 256 == 0 else r

    def body(half_ref, g_ref, r_ref, o32_ref, o16_ref):
        s = g_ref[...] + r_ref[...]
        o32_ref[...] = s
        o16_ref[...] = _b(s)

    return pl.pallas_call(
        body, name="add_sibling",
        grid_spec=pltpu.PrefetchScalarGridSpec(
            num_scalar_prefetch=1, grid=(N_CHIPS, r // tr),
            in_specs=[pl.BlockSpec((None, None, tr, c), lambda k, i, hf: (k, hf[0], i, 0)),
                      pl.BlockSpec((None, tr, c), lambda k, i, hf: (k, i, 0))],
            out_specs=[pl.BlockSpec((None, tr, c), lambda k, i, hf: (k, i, 0)),
                       pl.BlockSpec((None, tr, c), lambda k, i, hf: (k, i, 0))]),
        out_shape=[jax.ShapeDtypeStruct((N_CHIPS, r, c), F32), jax.ShapeDtypeStruct((N_CHIPS, r, c), BF16)],
        compiler_params=_params("arbitrary", "arbitrary"),
    )(half, g, recv)


def add_chip_partials(p32, recv, pos):
    _, r, c = p32.shape
    tr = _tile(r, 256) if r % 256 == 0 else r

    def body(pos_ref, p_ref, r_ref, o_ref):
        acc = p_ref[...]
        for j in range(N_CHIPS - 1):
            acc = acc + r_ref[j].astype(F32)
        o_ref[...] = acc

    return pl.pallas_call(
        body, name="add_chip_partials",
        grid_spec=pltpu.PrefetchScalarGridSpec(
            num_scalar_prefetch=1, grid=(r // tr,),
            in_specs=[pl.BlockSpec((None, tr, c), lambda i, ps: (ps[0], i, 0)),
                      pl.BlockSpec((N_CHIPS - 1, tr, c), lambda i, ps: (0, i, 0))],
            out_specs=pl.BlockSpec((None, tr, c), lambda i, ps: (ps[1], i, 0))),
        out_shape=jax.ShapeDtypeStruct((2, r, c), F32),
        compiler_params=_params("arbitrary"),
    )(pos, p32, recv)


def cast_into_gather(w, pos, dep, row0=0, nrows=None):
    c = w.shape[1]
    nrows = w.shape[0] if nrows is None else nrows
    r = nrows // 2
    common = math.gcd(r, row0) if row0 else r
    tr = max(w for w in range(16, min(common, 512) + 1, 16) if common % w == 0)
    nt = r // tr

    def body(pos_ref, w_ref, dep_ref, o_ref):
        o_ref[...] = _b(w_ref[...])

    return pl.pallas_call(
        body, name="cast_into_gather",
        grid_spec=pltpu.PrefetchScalarGridSpec(
            num_scalar_prefetch=1, grid=(2, nt),
            in_specs=[pl.BlockSpec((tr, c), lambda hf, i, ps: (row0 // tr + hf * nt + i, 0)), DEP_SPEC],
            out_specs=pl.BlockSpec((None, None, tr, c), lambda hf, i, ps: (ps[0], hf, i, 0))),
        out_shape=jax.ShapeDtypeStruct((N_CHIPS, 2, r, c), BF16),
        compiler_params=_params("arbitrary", "arbitrary"),
    )(pos, w, dep)


def build_bias(rel, buckets):
    nb, nh = rel.shape

    def body(rel_ref, bk_ref, o_ref):
        bk = bk_ref[...]
        for h in range(nh):
            acc = jnp.zeros(bk.shape, F32)
            for b in range(nb):
                acc = jnp.where(bk == b, rel_ref[b, h], acc)
            o_ref[h] = acc

    return pl.pallas_call(
        body, name="build_bias",
        in_specs=[pl.BlockSpec(memory_space=pltpu.SMEM), pl.BlockSpec(memory_space=pltpu.VMEM)],
        out_specs=pl.BlockSpec(memory_space=pltpu.VMEM),
        out_shape=jax.ShapeDtypeStruct((nh,) + buckets.shape, F32),
        compiler_params=_params(),
    )(rel, buckets)


SMALL_ROWS = 256


def kernel(x, ffn_norm, ffn_w1, ffn_w3, ffn_w2, ssm_norm, ssm_w_in, ssm_conv_w, ssm_conv_b, ssm_dt_bias, ssm_a_log, ssm_d, ssm_gate_norm, ssm_w_out, kv_norm, w_kv, k_norm, attn_norm, w_q, q_norm, sinks, w_o, rel_bias, loss_target, m_ffn_norm, m_ffn_w1, m_ffn_w3, m_ffn_w2, m_ssm_norm, m_ssm_w_in, m_ssm_conv_w, m_ssm_conv_b, m_ssm_dt_bias, m_ssm_a_log, m_ssm_d, m_ssm_gate_norm, m_ssm_w_out, m_kv_norm, m_w_kv, m_k_norm, m_attn_norm, m_w_q, m_q_norm, m_sinks, m_w_o, m_rel_bias, v_ffn_norm, v_ffn_w1, v_ffn_w3, v_ffn_w2, v_ssm_norm, v_ssm_w_in, v_ssm_conv_w, v_ssm_conv_b, v_ssm_dt_bias, v_ssm_a_log, v_ssm_d, v_ssm_gate_norm, v_ssm_w_out, v_kv_norm, v_w_kv, v_k_norm, v_attn_norm, v_w_q, v_q_norm, v_sinks, v_w_o, v_rel_bias):
    weights = dict(ffn_norm=ffn_norm, ffn_w1=ffn_w1, ffn_w3=ffn_w3, ffn_w2=ffn_w2, ssm_norm=ssm_norm,
                   ssm_w_in=ssm_w_in, ssm_conv_w=ssm_conv_w, ssm_conv_b=ssm_conv_b, ssm_dt_bias=ssm_dt_bias,
                   ssm_a_log=ssm_a_log, ssm_d=ssm_d, ssm_gate_norm=ssm_gate_norm, ssm_w_out=ssm_w_out,
                   kv_norm=kv_norm, w_kv=w_kv, k_norm=k_norm, attn_norm=attn_norm, w_q=w_q, q_norm=q_norm,
                   sinks=sinks, w_o=w_o, rel_bias=rel_bias)
    m_in = dict(ffn_norm=m_ffn_norm, ffn_w1=m_ffn_w1, ffn_w3=m_ffn_w3, ffn_w2=m_ffn_w2, ssm_norm=m_ssm_norm,
                ssm_w_in=m_ssm_w_in, ssm_conv_w=m_ssm_conv_w, ssm_conv_b=m_ssm_conv_b, ssm_dt_bias=m_ssm_dt_bias,
                ssm_a_log=m_ssm_a_log, ssm_d=m_ssm_d, ssm_gate_norm=m_ssm_gate_norm, ssm_w_out=m_ssm_w_out,
                kv_norm=m_kv_norm, w_kv=m_w_kv, k_norm=m_k_norm, attn_norm=m_attn_norm, w_q=m_w_q, q_norm=m_q_norm,
                sinks=m_sinks, w_o=m_w_o, rel_bias=m_rel_bias)
    v_in = dict(ffn_norm=v_ffn_norm, ffn_w1=v_ffn_w1, ffn_w3=v_ffn_w3, ffn_w2=v_ffn_w2, ssm_norm=v_ssm_norm,
                ssm_w_in=v_ssm_w_in, ssm_conv_w=v_ssm_conv_w, ssm_conv_b=v_ssm_conv_b, ssm_dt_bias=v_ssm_dt_bias,
                ssm_a_log=v_ssm_a_log, ssm_d=v_ssm_d, ssm_gate_norm=v_ssm_gate_norm, ssm_w_out=v_ssm_w_out,
                kv_norm=v_kv_norm, w_kv=v_w_kv, k_norm=v_k_norm, attn_norm=v_attn_norm, w_q=v_w_q, q_norm=v_q_norm,
                sinks=v_sinks, w_o=v_w_o, rel_bias=v_rel_bias)
    return _step(x[0], loss_target[0], weights, m_in, v_in)


BIG = ("ffn_w1", "ffn_w3", "ffn_w2", "ssm_w_in", "ssm_w_out", "w_kv", "w_q", "w_o")
SMALL = (("ffn_norm", True), ("ssm_norm", True), ("ssm_conv_w", True), ("ssm_conv_b", True),
         ("ssm_gate_norm", True), ("ssm_dt_bias", False), ("ssm_a_log", False), ("ssm_d", False),
         ("kv_norm", False), ("k_norm", False), ("attn_norm", False), ("q_norm", False), ("sinks", False),
         ("rel_bias", False))


FFN_W = BIG[:3]


def _small_layout(weights):
    off, table = 0, {}
    for name, sharded in SMALL:
        shape = weights[name].shape
        full = shape[:-1] + (shape[-1] * N_CHIPS,) if sharded else shape
        n = int(np.prod(full))
        table[name] = (off, full, sharded)
        off += n
    assert off <= SMALL_ROWS * 128
    return table


def _place_small(values, table, chip, scale_mask):
    flat = jnp.zeros((SMALL_ROWS * 128,), F32)
    for name, (off, full, sharded) in table.items():
        if not sharded:
            continue
        v = values[name].astype(F32)
        lead = int(np.prod(full[:-1]))
        w = v.shape[-1]
        blk = jnp.zeros((lead, full[-1]), F32)
        blk = lax.dynamic_update_slice(blk, v.reshape(lead, w) * scale_mask, (0, chip * w))
        flat = lax.dynamic_update_slice(flat, blk.reshape(-1), (off,))
    return flat.reshape(SMALL_ROWS, 128)


def _take_small(mat, table, name):
    off, full, _ = table[name]
    n = int(np.prod(full))
    return mat.reshape(-1)[off:off + n].reshape(full)


def _step(x, target, weights, m_in, v_in):
    t, d = x.shape
    xi, yi, ci = lax.axis_index("x"), lax.axis_index("y"), lax.axis_index("c")
    chip = 2 * xi + yi
    pos_arr = jnp.stack([chip, ci]).astype(jnp.int32)
    half_arr = jnp.reshape(ci, (1,)).astype(jnp.int32)

    fs = weights["ffn_w1"].shape[-1]
    ffn_rows = {"ffn_w1": d, "ffn_w3": d, "ffn_w2": fs}
    w2d = {n: weights[n].reshape(-1, weights[n].shape[-1]) for n in BIG}
    mamba_w = ("ssm_w_in", "ssm_w_out")
    late_w = ("w_kv", "w_q", "w_o")
    fs_, fr_, fbufs, tok_f = gather_start(
        [cast_into_gather(w2d[n], pos_arr, pos_arr, 0, ffn_rows[n]) for n in FFN_W], pos_arr, "first")
    ms, mr, mbufs, tok_m = gather_start([cast_into_gather(w2d[n], pos_arr, tok_f) for n in mamba_w], tok_f, "mamba")
    ls, lr, lbufs, tok_l = gather_start(
        [cast_into_gather(w2d[n], pos_arr, tok_f, ffn_rows[n], 3 * ffn_rows[n]) for n in FFN_W]
        + [cast_into_gather(w2d[n], pos_arr, tok_f) for n in late_w], tok_m, "late")
    first = forward_to_sibling(gather_wait(fs_, fr_, fbufs, tok_l, "first"))
    no_dep = jnp.zeros((8, 128), F32)
    table = _small_layout(weights)
    south = (ci == 0).astype(F32)
    small = allreduce_small(_place_small(weights, table, chip, south))
    sp = {n: _take_small(small, table, n) if sh else weights[n] for n, sh in SMALL}

    ffn_first = [first[0].reshape(N_CHIPS, 1, d, fs), first[1].reshape(N_CHIPS, 1, d, fs),
                 first[2].reshape(N_CHIPS, 1, fs, d)]
    ffn_g = sp["ffn_norm"]
    h0 = x
    h1, a00, b00 = ffn_fwd(h0, ffn_g[0, 0].reshape(1, d), *ffn_first, 0, no_dep)
    gathered = dict(zip(mamba_w, forward_to_sibling(gather_wait(ms, mr, mbufs, h1, "mamba"))))
    n_in = weights["ssm_w_in"].shape[-1] * N_CHIPS
    di = weights["ssm_w_out"].shape[1] * N_CHIPS
    nheads = di // SSM_HEAD_DIM
    conv_dim = n_in - di - nheads
    w_in_full = jnp.moveaxis(gathered["ssm_w_in"].reshape(N_CHIPS, d, n_in // N_CHIPS), 0, 1).reshape(d, n_in)
    hpg = nheads // SSM_GROUPS

    def spread_heads(v):
        lead = v.shape[:-1]
        v = v.reshape(lead + (SSM_GROUPS, hpg))
        v = jnp.pad(v, [(0, 0)] * len(lead) + [(0, 0), (0, 128 - hpg)])
        return v.reshape(lead + (SSM_GROUPS * 128,))

    def gather_heads(v):
        lead = v.shape[:-1]
        return v.reshape(lead + (SSM_GROUPS, 128))[..., :hpg].reshape(lead + (nheads,))

    dt_col0 = di + conv_dim
    n_zx = dt_col0 + SSM_GROUPS * 128
    w_in = jnp.concatenate([w_in_full[:, :dt_col0], spread_heads(w_in_full[:, dt_col0:])], axis=1)
    w_out = gathered["ssm_w_out"].reshape(di, d)
    nkv = weights["w_kv"].shape[1] // (2 * ATT_HEAD_DIM)
    assert nkv == 2
    nh = weights["w_q"].shape[-1] // ATT_HEAD_DIM

    ssm_g = sp["ssm_norm"].reshape(1, d)
    cw = jnp.pad(sp["ssm_conv_w"].reshape(SSM_CONV, conv_dim), [(0, 8 - SSM_CONV), (0, 0)])
    cb = sp["ssm_conv_b"].reshape(1, conv_dim)
    gate_g = sp["ssm_gate_norm"].reshape(1, di)
    dt_bias = spread_heads(sp["ssm_dt_bias"].reshape(1, nheads))
    a_log = spread_heads(sp["ssm_a_log"].reshape(1, nheads))
    d_skip = spread_heads(sp["ssm_d"].reshape(1, nheads))
    kv_g = sp["kv_norm"].reshape(1, d)
    k_g = jnp.tile(sp["k_norm"].reshape(1, ATT_HEAD_DIM), (1, 2))
    attn_g = sp["attn_norm"].reshape(1, d)
    q_g = jnp.tile(sp["q_norm"].reshape(1, ATT_HEAD_DIM), (1, 2))
    sink_row = jnp.pad(sp["sinks"].reshape(1, nh), [(0, 0), (0, 128 - nh)])
    buckets = jnp.asarray(_t5_buckets())
    biasm = build_bias(sp["rel_bias"], buckets).reshape(nh * ATT_WINDOW, 2 * ATT_WINDOW)

    zx = norm_mm(h1, ssm_g, w_in)
    xc = conv_fwd(zx, cw, cb, di)
    y_ssd, states = ssd_fwd(xc, zx, dt_bias, a_log, d_skip, dt_col0)
    h2 = gate_out_fwd(h1, y_ssd, zx, gate_g, w_out)

    late = forward_to_sibling(gather_wait(ls, lr, lbufs, h2, "late"))
    ffn_rest = [late[0].reshape(N_CHIPS, 3, d, fs), late[1].reshape(N_CHIPS, 3, d, fs),
                late[2].reshape(N_CHIPS, 3, fs, d)]
    gathered.update(zip(late_w, late[3:]))
    wkv_heads = gathered["w_kv"].reshape(d, 2 * nkv, 1, ATT_HEAD_DIM)
    w_kvd = jnp.broadcast_to(wkv_heads, (d, 2 * nkv, 2, ATT_HEAD_DIM)).reshape(d, 4 * nkv * ATT_HEAD_DIM)
    wq = gathered["w_q"].reshape(d, -1)
    wo = gathered["w_o"].reshape(-1, d)

    def ffn_w(layer, idx):
        blk = 2 * layer + idx
        return (*ffn_first, 0) if blk == 0 else (*ffn_rest, blk - 1)

    h3, a01, b01 = ffn_fwd(h2, ffn_g[0, 1].reshape(1, d), *ffn_w(0, 1), no_dep)
    kvd = norm_mm(h3, kv_g, w_kvd)
    h4, a10, b10 = ffn_fwd(h3, ffn_g[1, 0].reshape(1, d), *ffn_w(1, 0), no_dep)
    qp = norm_mm(h4, attn_g, wq)
    h5 = attn_fwd(h4, qp, kvd, biasm, sink_row, q_g, k_g, wo)
    h6, a11, b11 = ffn_fwd(h5, ffn_g[1, 1].reshape(1, d), *ffn_w(1, 1), no_dep)
    loss_part, d6 = loss_head(h6, target)
    loss = lax.psum(loss_part[0, 0], ("x", "y", "c"))

    gfn = [[None, None], [None, None]]

    pending = {}

    def swap_start(pieces, tag):
        views = [g.reshape(N_CHIPS, 2, g.shape[1] // 2, g.shape[2]) for _, g in pieces]
        ss, rs, views, lands, token = sibling_halves_start(views, tag)
        pending[tag] = dict(keys=[k for k, _ in pieces], swap=(ss, rs, views, lands))
        return token

    def partials_start(tag, after):
        views, recv1 = sibling_halves_wait(*pending[tag]["swap"], after, tag)
        p32, p16 = zip(*[add_sibling(g, r, half_arr) for g, r in zip(views, recv1)])
        ss, rs, parts, lands, token = chip_partials_start(list(p16), tag)
        pending[tag].update(p32=p32, partials=(ss, rs, parts, lands))
        return token

    def ffn_back(h_in, dy, a_s, b_s, layer, idx, dep, wdep):
        dh, u, da, db, s, dg = ffn_bwd(h_in, dy, ffn_g[layer, idx].reshape(1, d), a_s, b_s, *ffn_w(layer, idx), dep)
        gfn[layer][idx] = dg
        return dh, [(("ffn_w1", layer, idx), wgrad_grouped_b(u, da, wdep)),
                    (("ffn_w3", layer, idx), wgrad_grouped_b(u, db, no_dep)),
                    (("ffn_w2", layer, idx), wgrad_grouped_a(s, dy, no_dep, 0.5))]

    d5, pieces = ffn_back(h5, d6, a11, b11, 1, 1, no_dep, no_dep)
    tok = swap_start(pieces, "ffn11")
    dqp, dkvd, o16, dbiasm, dsinks, dqg, dkg = attn_bwd(d5, qp, kvd, biasm, sink_row, q_g, k_g, wo, tok)
    tok = partials_start("ffn11", dqp)
    g_wo = wgrad(o16, d5)
    d4, u_q, g_attn_norm = norm_mm_bwd(h4, attn_g, wq, dqp, d5, tok)
    g_wq = wgrad(u_q, dqp)
    d3a, pieces = ffn_back(h3, d4, a10, b10, 1, 0, no_dep, no_dep)
    pieces += [(("w_o",), g_wo.reshape(N_CHIPS, -1, d)), (("w_q",), g_wq.reshape(N_CHIPS, d // N_CHIPS, -1))]
    tok = swap_start(pieces, "ffn10")
    d3, u_kv, g_kv_norm = norm_mm_bwd(h3, kv_g, w_kvd, dkvd, d3a, tok, 0.5)
    tok = partials_start("ffn10", d3)
    g_wkvd = wgrad(u_kv, dkvd)
    g_wkv = g_wkvd.reshape(d, 2 * nkv, 2, ATT_HEAD_DIM)[:, :, 0, :].reshape(d, 2 * nkv * ATT_HEAD_DIM)
    d2, pieces = ffn_back(h2, d3, a01, b01, 0, 1, tok, no_dep)
    pieces += [(("w_kv",), g_wkv.reshape(N_CHIPS, d // N_CHIPS, -1))]
    tok = swap_start(pieces, "ffn01")
    dzx, dy_ssd, yn16, g_gate = gate_out_bwd(d2, y_ssd, zx, gate_g, w_out, n_zx, tok)
    tok = partials_start("ffn01", dy_ssd)
    g_wout = wgrad(yn16, d2)
    dzx, dxs, dbm, dcm, g_dtb, g_alog, g_dsk = ssd_bwd(dzx, dy_ssd, xc, zx, states, dt_bias, a_log, d_skip, dt_col0)
    dzx, g_cw, g_cb = conv_bwd(dzx, zx, dxs, dbm, dcm, cw, cb, di)
    d1, u_in, g_ssm_norm = norm_mm_bwd(h1, ssm_g, w_in, dzx, d2, tok)
    g_win = wgrad(u_in, dzx)
    g_win_full = jnp.concatenate([g_win[:, :dt_col0], gather_heads(g_win[:, dt_col0:])], axis=1)
    pieces = [(("ssm_w_in",), jnp.moveaxis(g_win_full.reshape(d, N_CHIPS, n_in // N_CHIPS), 1, 0)),
              (("ssm_w_out",), g_wout.reshape(N_CHIPS, di // N_CHIPS, d))]
    tok = swap_start(pieces, "mamba")
    grad_x, u0, da0, db0, s0, gfn[0][0] = ffn_bwd(h0, d1, ffn_g[0, 0].reshape(1, d), a00, b00, *ffn_w(0, 0), tok)
    tok = partials_start("mamba", grad_x)
    g1 = wgrad_grouped_b(u0, da0, tok)
    tok = swap_start([(("ffn_w1", 0, 0), g1)], "ffn00a")
    g3 = wgrad_grouped_b(u0, db0, tok)
    tok = partials_start("ffn00a", g3) + swap_start([(("ffn_w3", 0, 0), g3)], "ffn00b")
    g2 = wgrad_grouped_a(s0, d1, tok, 0.5)
    tok = partials_start("ffn00b", g2) + swap_start([(("ffn_w2", 0, 0), g2)], "ffn00")
    g_relb = rel_bias_bwd(dbiasm.reshape(nh, ATT_WINDOW, 2 * ATT_WINDOW), buckets)

    reduced = {}

    def finish(tag, after):
        st = pending[tag]
        lands = chip_partials_wait(*st["partials"], after, tag)
        for k, p, r in zip(st["keys"], st["p32"], lands):
            reduced[k] = add_chip_partials(p, r, pos_arr)
        return reduced[st["keys"][-1]]

    last = finish("ffn10", finish("ffn11", tok))
    tok = partials_start("ffn00", last)
    last = finish("ffn00b", finish("ffn00a", finish("mamba", finish("ffn01", tok))))
    finish("ffn00", last)
    keys = list(reduced)
    shared = dict(zip(keys, share_with_sibling([reduced[k] for k in keys])))
    grads = {}
    for n in FFN_W:
        blocks = [shared[(n, l, i)].reshape(1, ffn_rows[n], -1) for l in range(2) for i in range(2)]
        grads[n] = jnp.concatenate(blocks, axis=0).reshape(weights[n].shape)
    for n in BIG[3:]:
        grads[n] = shared[(n,)].reshape(weights[n].shape)

    small_grads = {
        "ffn_norm": jnp.stack([jnp.stack([gfn[l][i].reshape(d) for i in range(2)]) for l in range(2)]),
        "ssm_norm": g_ssm_norm.reshape(1, d),
        "ssm_conv_w": g_cw[:SSM_CONV].reshape(1, SSM_CONV, conv_dim),
        "ssm_conv_b": g_cb.reshape(1, conv_dim),
        "ssm_gate_norm": g_gate.reshape(1, di),
        "ssm_dt_bias": gather_heads(g_dtb.reshape(1, -1)), "ssm_a_log": gather_heads(g_alog.reshape(1, -1)),
        "ssm_d": gather_heads(g_dsk.reshape(1, -1)),
        "kv_norm": g_kv_norm.reshape(d), "k_norm": dkg[0, :ATT_HEAD_DIM], "attn_norm": g_attn_norm.reshape(1, d),
        "q_norm": dqg[:, :ATT_HEAD_DIM], "sinks": dsinks[:, :nh], "rel_bias": g_relb[:, :nh],
    }
    flat = jnp.zeros((SMALL_ROWS * 128,), F32)
    for name, (off, fshape, _) in table.items():
        flat = lax.dynamic_update_slice(flat, small_grads[name].astype(F32).reshape(-1), (off,))
    small_sum = allreduce_small(flat.reshape(SMALL_ROWS, 128))
    for name, (off, fshape, sharded) in table.items():
        g = _take_small(small_sum, table, name)
        if sharded:
            w = weights[name].shape[-1]
            lead = int(np.prod(fshape[:-1]))
            g = lax.dynamic_slice(g.reshape(lead, fshape[-1]), (0, chip * w), (lead, w)).reshape(weights[name].shape)
        grads[name] = g.reshape(weights[name].shape)

    names = list(weights)
    deltas, new_m, new_v = {}, {}, {}
    small_names = [n for n, _ in SMALL]
    for n in BIG:
        shp = weights[n].shape
        v2 = lambda a: a.reshape(-1, shp[-1])
        dl, nm, nv = adamw(v2(weights[n]), v2(grads[n]), v2(m_in[n]), v2(v_in[n]))
        deltas[n], new_m[n], new_v[n] = dl.reshape(shp), nm.reshape(shp), nv.reshape(shp)
    sizes = [int(np.prod(weights[n].shape)) for n in small_names]
    tot = sum(sizes)
    rows = -(-tot // 128)
    rows = -(-rows // 8) * 8

    def pack(dct):
        flat = jnp.concatenate([dct[n].reshape(-1) for n in small_names])
        return jnp.pad(flat, (0, rows * 128 - tot), constant_values=1.0).reshape(rows, 128)

    dl, nm, nv = adamw(pack(weights), pack(grads), pack(m_in), pack(v_in))
    off = 0
    for n, sz in zip(small_names, sizes):
        shp = weights[n].shape
        take = lambda a: a.reshape(-1)[off:off + sz].reshape(shp)
        deltas[n], new_m[n], new_v[n] = take(dl), take(nm), take(nv)
        off += sz

    return (loss, grad_x[None], *[grads[n] for n in names], *[deltas[n] for n in names],
            *[new_m[n] for n in names], *[new_v[n] for n in names])
```

```python
import functools
import math

import jax
import jax.numpy as jnp
import numpy as np
from jax import lax
from jax.experimental import pallas as pl
from jax.experimental.pallas import tpu as pltpu

F32 = jnp.float32
BF16 = jnp.bfloat16
EPS = 1e-6
MESH = pl.DeviceIdType.MESH

SSM_HEAD_DIM = 64
SSM_GROUPS = 4
SSM_STATE = 128
SSM_CONV = 4
SSM_CHUNK = 256
ATT_HEAD_DIM = 64
ATT_WINDOW = 128
REL_BUCKETS = 32
N_CHIPS = 4

ADAM_LR = 0.001
ADAM_B1 = 0.9
ADAM_B2 = 0.999
ADAM_EPS = 1e-08
ADAM_WD = 0.01
ADAM_STEP = 10

VMEM_LIMIT_BYTES = 56 * 1024 * 1024
NEG = -1e30


DEP_SPEC = pl.BlockSpec(memory_space=pl.ANY)


def _params(*sem):
    return pltpu.CompilerParams(dimension_semantics=sem if sem else None, vmem_limit_bytes=VMEM_LIMIT_BYTES)


def _dot(a, b):
    return jnp.dot(a, b, preferred_element_type=F32)


def _dot_nt(a, b):
    return lax.dot_general(a, b, (((1,), (1,)), ((), ())), preferred_element_type=F32)


def _dot_tn(a, b):
    return lax.dot_general(a, b, (((0,), (0,)), ((), ())), preferred_element_type=F32)


def _b(x):
    return x.astype(BF16)


@jax.custom_vjp
def _bmm(a, b):
    return _dot(_b(a), _b(b))


def _bmm_fwd(a, b):
    return _bmm(a, b), (a, b)


def _bmm_bwd(res, g):
    a, b = res
    g16 = _b(g)
    return _dot_nt(g16, _b(b)).astype(a.dtype), _dot_tn(_b(a), g16).astype(b.dtype)


_bmm.defvjp(_bmm_fwd, _bmm_bwd)


@jax.custom_vjp
def _bmm_nt(a, b):
    return _dot_nt(_b(a), _b(b))


def _bmm_nt_fwd(a, b):
    return _bmm_nt(a, b), (a, b)


def _bmm_nt_bwd(res, g):
    a, b = res
    g16 = _b(g)
    return _dot(g16, _b(b)).astype(a.dtype), _dot_tn(g16, _b(a)).astype(b.dtype)


_bmm_nt.defvjp(_bmm_nt_fwd, _bmm_nt_bwd)


@jax.custom_vjp
def _bmm_tn(a, b):
    return _dot_tn(_b(a), _b(b))


def _bmm_tn_fwd(a, b):
    return _bmm_tn(a, b), (a, b)


def _bmm_tn_bwd(res, g):
    a, b = res
    g16 = _b(g)
    return _dot_nt(_b(b), g16).astype(a.dtype), _dot(_b(a), g16).astype(b.dtype)


_bmm_tn.defvjp(_bmm_tn_fwd, _bmm_tn_bwd)


def _split3(x):
    hi = _b(x)
    r = x - hi.astype(F32)
    mid = _b(r)
    lo = _b(r - mid.astype(F32))
    return hi, mid, lo


def _x_left_raw(m, x):
    hi, mid, lo = _split3(x)
    return _dot(m, hi) + _dot(m, mid) + _dot(m, lo)


def _x_left_t_raw(m, x):
    hi, mid, lo = _split3(x)
    return _dot_tn(m, hi) + _dot_tn(m, mid) + _dot_tn(m, lo)


def _x_right_raw(x, m):
    hi, mid, lo = _split3(x)
    return _dot(hi, m) + _dot(mid, m) + _dot(lo, m)


def _x_right_t_raw(x, m):
    hi, mid, lo = _split3(x)
    return _dot_nt(hi, m) + _dot_nt(mid, m) + _dot_nt(lo, m)


@jax.custom_vjp
def _xleft(m, x):
    return _x_left_raw(m, x)


_xleft.defvjp(lambda m, x: (_x_left_raw(m, x), m),
              lambda m, g: (jnp.zeros_like(m), _x_left_t_raw(m, g)))


@jax.custom_vjp
def _xright(x, m):
    return _x_right_raw(x, m)


_xright.defvjp(lambda x, m: (_x_right_raw(x, m), m),
               lambda m, g: (_x_right_t_raw(g, m), jnp.zeros_like(m)))


def _sigmoid(x):
    return 1.0 / (1.0 + jnp.exp(-x))


def _silu(x):
    return x * _sigmoid(x)


def _softplus(x):
    return jnp.maximum(x, 0.0) + jnp.log(1.0 + jnp.exp(-jnp.abs(x)))


def _rms(x):
    return x * lax.rsqrt(jnp.mean(x * x, axis=-1, keepdims=True) + EPS)


def _iota(shape, dim):
    return lax.broadcasted_iota(jnp.int32, shape, dim)


def _blockdiag64(n):
    return jnp.where(_iota((n, n), 0) // 64 == _iota((n, n), 1) // 64, 1.0, 0.0).astype(BF16)


def _group64_rms(x, seg_sum):
    ms = seg_sum(x * x) * (1.0 / 64.0)
    return x * lax.rsqrt(ms + EPS)


def _fold64(x):
    ax = x.ndim - 1
    w = x.shape[ax]
    lo = (_iota(x.shape, ax) % 128) < 64
    return x + jnp.where(lo, pltpu.roll(x, w - 64, ax), pltpu.roll(x, 64, ax))


def _tile(n, want):
    t = min(n, want)
    assert n % t == 0, (n, t)
    return t


def _lane_tile(n, cap=1536):
    if n <= cap:
        return n
    return max(w for w in range(128, cap + 1, 128) if n % w == 0)


def ffn_fwd(h, g, w1, w3, w2, blk, dep):
    t, d = h.shape
    nk, fs = w1.shape[0], w1.shape[-1]
    tm = _tile(t, 512)

    def body(h_ref, g_ref, w1_ref, w3_ref, w2_ref, dep_ref, o_ref, a_ref, b_ref, u_scr, acc):
        k = pl.program_id(1)

        @pl.when(k == 0)
        def _():
            u_scr[...] = _b(_rms(h_ref[...]) * g_ref[...])
            acc[...] = jnp.zeros_like(acc)

        u = u_scr[...]
        a = _dot(u, w1_ref[...])
        b = _dot(u, w3_ref[...])
        a_ref[...] = _b(a)
        b_ref[...] = _b(b)
        acc[...] += _dot(_b(_silu(a) * b), w2_ref[...])

        @pl.when(k == nk - 1)
        def _():
            o_ref[...] = h_ref[...] + 0.5 * acc[...]

    wspec = lambda r, c: pl.BlockSpec((None, None, r, c), lambda i, k: (k, blk, 0, 0))
    return pl.pallas_call(
        body, name="ffn_fwd",
        grid=(t // tm, nk),
        in_specs=[pl.BlockSpec((tm, d), lambda i, k: (i, 0)), pl.BlockSpec((1, d), lambda i, k: (0, 0)),
                  wspec(d, fs), wspec(d, fs), wspec(fs, d), DEP_SPEC],
        out_specs=[pl.BlockSpec((tm, d), lambda i, k: (i, 0)),
                   pl.BlockSpec((None, tm, fs), lambda i, k: (k, i, 0)),
                   pl.BlockSpec((None, tm, fs), lambda i, k: (k, i, 0))],
        out_shape=[jax.ShapeDtypeStruct((t, d), F32), jax.ShapeDtypeStruct((nk, t, fs), BF16),
                   jax.ShapeDtypeStruct((nk, t, fs), BF16)],
        scratch_shapes=[pltpu.VMEM((tm, d), BF16), pltpu.VMEM((tm, d), F32)],
        compiler_params=_params("arbitrary", "arbitrary"),
    )(h, g, w1, w3, w2, dep)


def ffn_bwd(h, dy, g, a_s, b_s, w1, w3, w2, blk, dep):
    t, d = h.shape
    nk, fs = w1.shape[0], w1.shape[-1]
    tm = _tile(t, 512)

    def body(h_ref, dy_ref, g_ref, a_ref, b_ref, w1_ref, w3_ref, w2_ref, dep_ref,
             dh_ref, u_ref, da_ref, db_ref, s_ref, dg_ref, dyh_scr, du_acc, da0, db0, da1, db1):
        i, k = pl.program_id(0), pl.program_id(1)

        @pl.when(k == 0)
        def _():
            dyh_scr[...] = _b(0.5 * dy_ref[...])
            du_acc[...] = jnp.zeros_like(du_acc)

        @pl.when((k == 0) & (i == 0))
        def _():
            dg_ref[...] = jnp.zeros_like(dg_ref)

        def step(prev, cur):
            if prev is not None:
                du_acc[...] += _dot_nt(prev[0][...], w1_ref[...]) + _dot_nt(prev[1][...], w3_ref[...])
            if cur is not None:
                ds = _dot_nt(dyh_scr[...], w2_ref[...])
                a = a_ref[...].astype(F32)
                b = b_ref[...].astype(F32)
                sig = _sigmoid(a)
                sl = a * sig
                s_ref[...] = _b(sl * b)
                da = _b(ds * b * (sig * (1.0 + a * (1.0 - sig))))
                db = _b(ds * sl)
                da_ref[...] = da
                db_ref[...] = db
                cur[0][...] = da
                cur[1][...] = db

        even, odd = (da0, db0), (da1, db1)

        @pl.when(k == 0)
        def _():
            step(None, even)

        @pl.when((k > 0) & (k < nk) & (k % 2 == 1))
        def _():
            step(even, odd)

        @pl.when((k > 0) & (k < nk) & (k % 2 == 0))
        def _():
            step(odd, even)

        @pl.when(k == nk)
        def _():
            step(odd if nk % 2 == 0 else even, None)
            hh = h_ref[...]
            rstd = lax.rsqrt(jnp.mean(hh * hh, axis=-1, keepdims=True) + EPS)
            xh = hh * rstd
            gg = g_ref[...]
            u_ref[...] = _b(xh * gg)
            du = du_acc[...]
            dg_ref[...] += jnp.sum(du * xh, axis=0, keepdims=True)
            dxh = du * gg
            dh_ref[...] = dy_ref[...] + rstd * (dxh - xh * jnp.mean(dxh * xh, axis=-1, keepdims=True))

    cur = lambda k: jnp.minimum(k, nk - 1)
    prv = lambda k: jnp.maximum(k - 1, 0)
    wcur = lambda r, c: pl.BlockSpec((None, None, r, c), lambda i, k: (cur(k), blk, 0, 0))
    wprv = lambda r, c: pl.BlockSpec((None, None, r, c), lambda i, k: (prv(k), blk, 0, 0))
    tok = pl.BlockSpec((tm, d), lambda i, k: (i, 0))
    hid = pl.BlockSpec((None, tm, fs), lambda i, k: (cur(k), i, 0))
    return pl.pallas_call(
        body, name="ffn_bwd",
        grid=(t // tm, nk + 1),
        in_specs=[tok, tok, pl.BlockSpec((1, d), lambda i, k: (0, 0)), hid, hid, wprv(d, fs), wprv(d, fs), wcur(fs, d),
                  DEP_SPEC],
        out_specs=[tok, tok, hid, hid, hid, pl.BlockSpec((1, d), lambda i, k: (0, 0))],
        out_shape=[jax.ShapeDtypeStruct((t, d), F32), jax.ShapeDtypeStruct((t, d), BF16),
                   jax.ShapeDtypeStruct((nk, t, fs), BF16), jax.ShapeDtypeStruct((nk, t, fs), BF16),
                   jax.ShapeDtypeStruct((nk, t, fs), BF16), jax.ShapeDtypeStruct((1, d), F32)],
        scratch_shapes=[pltpu.VMEM((tm, d), BF16), pltpu.VMEM((tm, d), F32)] + [pltpu.VMEM((tm, fs), BF16)] * 4,
        compiler_params=_params("arbitrary", "arbitrary"),
    )(h, dy, g, a_s, b_s, w1, w3, w2, dep)


def wgrad_grouped_b(a, bs, dep, scale=1.0):
    t, m = a.shape
    ng, _, n = bs.shape
    tk = _tile(t, 2048)

    def body(a_ref, b_ref, dep_ref, o_ref):
        j = pl.program_id(1)

        @pl.when(j == 0)
        def _():
            o_ref[...] = jnp.zeros_like(o_ref)

        o_ref[...] += _dot_tn(_b(a_ref[...]), _b(b_ref[...]))

        if scale != 1.0:
            @pl.when(j == pl.num_programs(1) - 1)
            def _():
                o_ref[...] = o_ref[...] * scale

    return pl.pallas_call(
        body, name="wgrad_gb",
        grid=(ng, t // tk),
        in_specs=[pl.BlockSpec((tk, m), lambda k, j: (j, 0)), pl.BlockSpec((None, tk, n), lambda k, j: (k, j, 0)),
                  DEP_SPEC],
        out_specs=pl.BlockSpec((None, m, n), lambda k, j: (k, 0, 0)),
        out_shape=jax.ShapeDtypeStruct((ng, m, n), F32),
        compiler_params=_params("arbitrary", "arbitrary"),
    )(a, bs, dep)


def wgrad_grouped_a(as_, b, dep, scale=1.0):
    ng, t, m = as_.shape
    n = b.shape[1]
    tk = _tile(t, 2048)

    def body(a_ref, b_ref, dep_ref, o_ref):
        j = pl.program_id(1)

        @pl.when(j == 0)
        def _():
            o_ref[...] = jnp.zeros_like(o_ref)

        o_ref[...] += _dot_tn(_b(a_ref[...]), _b(b_ref[...]))

        if scale != 1.0:
            @pl.when(j == pl.num_programs(1) - 1)
            def _():
                o_ref[...] = o_ref[...] * scale

    return pl.pallas_call(
        body, name="wgrad_ga",
        grid=(ng, t // tk),
        in_specs=[pl.BlockSpec((None, tk, m), lambda k, j: (k, j, 0)), pl.BlockSpec((tk, n), lambda k, j: (j, 0)),
                  DEP_SPEC],
        out_specs=pl.BlockSpec((None, m, n), lambda k, j: (k, 0, 0)),
        out_shape=jax.ShapeDtypeStruct((ng, m, n), F32),
        compiler_params=_params("arbitrary", "arbitrary"),
    )(as_, b, dep)


def wgrad(a, b):
    t, m = a.shape
    n = b.shape[1]
    tk = _tile(t, 2048 if m <= 1024 else 1024)
    tn = _lane_tile(n, 1536 if m <= 1024 else 512)

    def body(a_ref, b_ref, o_ref):
        @pl.when(pl.program_id(1) == 0)
        def _():
            o_ref[...] = jnp.zeros_like(o_ref)

        o_ref[...] += _dot_tn(_b(a_ref[...]), _b(b_ref[...]))

    return pl.pallas_call(
        body, name="wgrad",
        grid=(n // tn, t // tk),
        in_specs=[pl.BlockSpec((tk, m), lambda c, j: (j, 0)), pl.BlockSpec((tk, tn), lambda c, j: (j, c))],
        out_specs=pl.BlockSpec((m, tn), lambda c, j: (0, c)),
        out_shape=jax.ShapeDtypeStruct((m, n), F32),
        compiler_params=_params("arbitrary", "arbitrary"),
    )(a, b)


def norm_mm(h, g, w):
    t, d = h.shape
    n = w.shape[1]
    tm = _tile(t, 1024)
    tn = _lane_tile(n)

    def body(h_ref, g_ref, w_ref, o_ref, u_scr):
        @pl.when(pl.program_id(1) == 0)
        def _():
            u_scr[...] = _b(_rms(h_ref[...]) * g_ref[...])

        o_ref[...] = _dot(u_scr[...], w_ref[...])

    return pl.pallas_call(
        body, name="norm_mm",
        grid=(t // tm, n // tn),
        in_specs=[pl.BlockSpec((tm, d), lambda i, j: (i, 0)), pl.BlockSpec((1, d), lambda i, j: (0, 0)),
                  pl.BlockSpec((d, tn), lambda i, j: (0, j))],
        out_specs=pl.BlockSpec((tm, tn), lambda i, j: (i, j)),
        out_shape=jax.ShapeDtypeStruct((t, n), F32),
        scratch_shapes=[pltpu.VMEM((tm, d), BF16)],
        compiler_params=_params("arbitrary", "arbitrary"),
    )(h, g, w)


def norm_mm_bwd(h, g, w, dout, dres, dep, scale=1.0):
    t, d = h.shape
    n = w.shape[1]
    tm = _tile(t, 512)
    tn = _lane_tile(n)
    nj = n // tn

    def body(h_ref, g_ref, w_ref, do_ref, dr_ref, dep_ref, dh_ref, u_ref, dg_ref, du_acc):
        i, j = pl.program_id(0), pl.program_id(1)

        @pl.when(j == 0)
        def _():
            du_acc[...] = jnp.zeros_like(du_acc)

        @pl.when((j == 0) & (i == 0))
        def _():
            dg_ref[...] = jnp.zeros_like(dg_ref)

        du_acc[...] += _dot_nt(_b(do_ref[...]), w_ref[...])

        @pl.when(j == nj - 1)
        def _():
            hh = h_ref[...]
            rstd = lax.rsqrt(jnp.mean(hh * hh, axis=-1, keepdims=True) + EPS)
            xh = hh * rstd
            gg = g_ref[...]
            u_ref[...] = _b(xh * gg)
            du = du_acc[...] * scale
            dg_ref[...] += jnp.sum(du * xh, axis=0, keepdims=True)
            dxh = du * gg
            dh_ref[...] = dr_ref[...] + rstd * (dxh - xh * jnp.mean(dxh * xh, axis=-1, keepdims=True))

    tok = pl.BlockSpec((tm, d), lambda i, j: (i, 0))
    return pl.pallas_call(
        body, name="norm_mm_bwd",
        grid=(t // tm, nj),
        in_specs=[tok, pl.BlockSpec((1, d), lambda i, j: (0, 0)), pl.BlockSpec((d, tn), lambda i, j: (0, j)),
                  pl.BlockSpec((tm, tn), lambda i, j: (i, j)), tok, DEP_SPEC],
        out_specs=[tok, tok, pl.BlockSpec((1, d), lambda i, j: (0, 0))],
        out_shape=[jax.ShapeDtypeStruct((t, d), F32), jax.ShapeDtypeStruct((t, d), BF16),
                   jax.ShapeDtypeStruct((1, d), F32)],
        scratch_shapes=[pltpu.VMEM((tm, d), F32)],
        compiler_params=_params("arbitrary", "arbitrary"),
    )(h, g, w, dout, dres, dep)


CONV_COLS = 512


CONV_ROWS = 64


def _conv_pre(ext, w, b, r0, n):
    return (b + w[0:1] * ext[pl.ds(5 + r0, n), :] + w[1:2] * ext[pl.ds(6 + r0, n), :]
            + w[2:3] * ext[pl.ds(7 + r0, n), :] + w[3:4] * ext[pl.ds(8 + r0, n), :])


def conv_fwd(zx, cw, cb, col0):
    t = zx.shape[0]
    c = cw.shape[1]
    tm = _tile(t, 512)
    cb0 = col0 // CONV_COLS

    rc = _tile(tm, CONV_ROWS)

    def body(x_ref, w_ref, b_ref, o_ref, ext):
        @pl.when(pl.program_id(1) == 0)
        def _():
            ext[0:8, :] = jnp.zeros((8, CONV_COLS), F32)

        ext[8:, :] = x_ref[...]
        w, b = w_ref[...], b_ref[...]
        for r0 in range(0, tm, rc):
            o_ref[r0:r0 + rc, :] = _silu(_conv_pre(ext, w, b, r0, rc))
        ext[0:8, :] = ext[tm:tm + 8, :]

    return pl.pallas_call(
        body, name="conv_fwd",
        grid=(c // CONV_COLS, t // tm),
        in_specs=[pl.BlockSpec((tm, CONV_COLS), lambda j, i: (i, cb0 + j)),
                  pl.BlockSpec((8, CONV_COLS), lambda j, i: (0, j)), pl.BlockSpec((1, CONV_COLS), lambda j, i: (0, j))],
        out_specs=pl.BlockSpec((tm, CONV_COLS), lambda j, i: (i, j)),
        out_shape=jax.ShapeDtypeStruct((t, c), F32),
        scratch_shapes=[pltpu.VMEM((tm + 8, CONV_COLS), F32)],
        compiler_params=_params("arbitrary", "arbitrary"),
    )(zx, cw, cb)


def conv_bwd(dzx, zx, dxs, dbm, dcm, cw, cb, col0):
    t = zx.shape[0]
    c = cw.shape[1]
    tm = _tile(t, 512)
    nt = t // tm
    cb0 = col0 // CONV_COLS
    nxs = dxs.shape[1] // CONV_COLS
    hb = tm // 8

    rc = _tile(tm, CONV_ROWS)

    def body(dzx_ref, x_ref, xh_ref, dxs_ref, db_ref, dc_ref, w_ref, b_ref, o_ref, dw_ref, dbias_ref, ext, gy):
        j, i = pl.program_id(0), pl.program_id(1)
        ri = nt - 1 - i

        @pl.when(i == 0)
        def _():
            gy[tm:tm + 8, :] = jnp.zeros((8, CONV_COLS), F32)
            dw_ref[...] = jnp.zeros_like(dw_ref)
            dbias_ref[...] = jnp.zeros_like(dbias_ref)

        ext[0:8, :] = jnp.where(ri > 0, xh_ref[...], 0.0)
        ext[8:, :] = x_ref[...]
        w, b = w_ref[...], b_ref[...]
        dw = [jnp.zeros((1, CONV_COLS), F32) for _ in range(SSM_CONV)]
        dbias = jnp.zeros((1, CONV_COLS), F32)
        for r0 in range(0, tm, rc):
            rows = pl.ds(r0, rc)
            win = [ext[pl.ds(5 + tap + r0, rc), :] for tap in range(SSM_CONV)]
            y = b + w[0:1] * win[0] + w[1:2] * win[1] + w[2:3] * win[2] + w[3:4] * win[3]
            sig = _sigmoid(y)
            dout = jnp.where(j < nxs, dxs_ref[rows, :], jnp.where(j == nxs, db_ref[rows, :], dc_ref[rows, :]))
            g = dout * (sig * (1.0 + y * (1.0 - sig)))
            gy[rows, :] = g
            dbias = dbias + jnp.sum(g, axis=0, keepdims=True)
            for tap in range(SSM_CONV):
                dw[tap] = dw[tap] + jnp.sum(g * win[tap], axis=0, keepdims=True)
        for r0 in range(0, tm, rc):
            o_ref[r0:r0 + rc, :] = _b(w[0:1] * gy[pl.ds(r0 + 3, rc), :] + w[1:2] * gy[pl.ds(r0 + 2, rc), :]
                                      + w[2:3] * gy[pl.ds(r0 + 1, rc), :] + w[3:4] * gy[pl.ds(r0, rc), :])
        gy[tm:tm + 8, :] = gy[0:8, :]
        for tap in range(SSM_CONV):
            dw_ref[tap:tap + 1, :] += dw[tap]
        dbias_ref[...] += dbias

    return pl.pallas_call(
        body, name="conv_bwd",
        grid=(c // CONV_COLS, nt),
        in_specs=[pl.BlockSpec(memory_space=pl.ANY),
                  pl.BlockSpec((tm, CONV_COLS), lambda j, i: (nt - 1 - i, cb0 + j)),
                  pl.BlockSpec((8, CONV_COLS), lambda j, i: (jnp.maximum((nt - 1 - i) * hb - 1, 0), cb0 + j)),
                  pl.BlockSpec((tm, CONV_COLS), lambda j, i: (nt - 1 - i, jnp.minimum(j, nxs - 1))),
                  pl.BlockSpec((tm, CONV_COLS), lambda j, i: (nt - 1 - i, 0)),
                  pl.BlockSpec((tm, CONV_COLS), lambda j, i: (nt - 1 - i, 0)),
                  pl.BlockSpec((8, CONV_COLS), lambda j, i: (0, j)), pl.BlockSpec((1, CONV_COLS), lambda j, i: (0, j))],
        out_specs=[pl.BlockSpec((tm, CONV_COLS), lambda j, i: (nt - 1 - i, cb0 + j)),
                   pl.BlockSpec((8, CONV_COLS), lambda j, i: (0, j)), pl.BlockSpec((1, CONV_COLS), lambda j, i: (0, j))],
        out_shape=[jax.ShapeDtypeStruct(dzx.shape, dzx.dtype), jax.ShapeDtypeStruct((8, c), F32),
                   jax.ShapeDtypeStruct((1, c), F32)],
        scratch_shapes=[pltpu.VMEM((tm + 8, CONV_COLS), F32), pltpu.VMEM((tm + 8, CONV_COLS), F32)],
        input_output_aliases={0: 0},
        compiler_params=_params("arbitrary", "arbitrary"),
    )(dzx, zx, zx, dxs, dbm, dcm, cw, cb)


def _ssd_group(xs, bg, cg, dtraw, s0, bias, alog, dsk):
    L = xs.shape[0]
    causal = _iota((L, L), 0) >= _iota((L, L), 1)
    tril = jnp.where(causal, 1.0, 0.0).astype(BF16)
    dt = _softplus(dtraw + bias)
    a = -jnp.exp(alog)
    acum = _xleft(tril, dt * a)
    acum_t = acum.T
    dt_t = dt.T
    cb = _bmm_nt(cg, bg)
    lo = _iota((L, 128), 1) < 64
    lo_row = _iota((1, 128), 1) < 64
    lo_col = _iota((128, 1), 0) < 64
    alast = acum[L - 1:L, :]
    ys, s1s = [], []
    for q in range(4):
        xp = xs[:, q * 128:(q + 1) * 128]
        sp = s0[q * 128:(q + 1) * 128, :]
        yd, ec, wc, el = [], [], [], []
        for j in range(2):
            r = 2 * q + j
            ac = acum[:, r:r + 1]
            decay = jnp.exp(jnp.where(causal, ac - acum_t[r:r + 1, :], NEG))
            yd.append(_bmm(cb * decay * dt_t[r:r + 1, :], xp))
            ec.append(jnp.exp(ac))
            al = alast[:, r:r + 1]
            wc.append(jnp.exp(al - ac) * dt[:, r:r + 1])
            el.append(jnp.exp(al))
        y_off = _bmm_nt(cg, sp) * jnp.where(lo, ec[0], ec[1])
        dsel = jnp.where(lo_row, dsk[:, 2 * q:2 * q + 1], dsk[:, 2 * q + 1:2 * q + 2])
        ys.append(jnp.where(lo, yd[0], yd[1]) + y_off + dsel * xp)
        xw = xp * jnp.where(lo, wc[0], wc[1])
        s1s.append(sp * jnp.where(lo_col, el[0], el[1]) + _bmm_tn(xw, bg))
    return jnp.concatenate(ys, axis=1), jnp.concatenate(s1s, axis=0)


def ssd_fwd(xc, zx, bias, alog, dsk, dt_col0):
    t = xc.shape[0]
    L = _tile(t, SSM_CHUNK)
    nc = t // L
    g = SSM_GROUPS
    dtb = dt_col0 // 512

    def body(xs_ref, b_ref, c_ref, dt_ref, bias_ref, alog_ref, dsk_ref, y_ref, st_ref, state):
        @pl.when(pl.program_id(0) == 0)
        def _():
            state[...] = jnp.zeros_like(state)

        for gi in range(g):
            lane = slice(gi * 128, (gi + 1) * 128)
            wide = slice(gi * 512, (gi + 1) * 512)
            s0 = state[gi]
            st_ref[gi] = s0
            y, s1 = _ssd_group(xs_ref[:, wide], b_ref[:, lane], c_ref[:, lane], dt_ref[:, lane], s0,
                               bias_ref[:, lane], alog_ref[:, lane], dsk_ref[:, lane])
            y_ref[:, wide] = y
            state[gi] = s1

    vec = pl.BlockSpec((1, 512), lambda c: (0, 0))
    return pl.pallas_call(
        body, name="ssd_fwd",
        grid=(nc,),
        in_specs=[pl.BlockSpec((L, 2048), lambda c: (c, 0)), pl.BlockSpec((L, 512), lambda c: (c, 4)),
                  pl.BlockSpec((L, 512), lambda c: (c, 5)), pl.BlockSpec((L, 512), lambda c: (c, dtb)), vec, vec, vec],
        out_specs=[pl.BlockSpec((L, 2048), lambda c: (c, 0)),
                   pl.BlockSpec((None, g, 512, 128), lambda c: (c, 0, 0, 0))],
        out_shape=[jax.ShapeDtypeStruct((t, 2048), F32), jax.ShapeDtypeStruct((nc, g, 512, 128), F32)],
        scratch_shapes=[pltpu.VMEM((g, 512, 128), F32)],
        compiler_params=_params("arbitrary"),
    )(xc, xc, xc, zx, bias, alog, dsk)


def ssd_bwd(dzx, dy, xc, zx, states, bias, alog, dsk, dt_col0):
    t = xc.shape[0]
    L = _tile(t, SSM_CHUNK)
    nc = t // L
    g = SSM_GROUPS
    dtb = dt_col0 // 512

    def body(dzx_ref, dy_ref, xs_ref, b_ref, c_ref, dt_ref, st_ref, bias_ref, alog_ref, dsk_ref,
             ddt_ref, dxs_ref, db_ref, dc_ref, dbias_ref, dalog_ref, ddsk_ref, dstate):
        @pl.when(pl.program_id(0) == 0)
        def _():
            dstate[...] = jnp.zeros_like(dstate)
            dbias_ref[...] = jnp.zeros_like(dbias_ref)
            dalog_ref[...] = jnp.zeros_like(dalog_ref)
            ddsk_ref[...] = jnp.zeros_like(ddsk_ref)

        for gi in range(g):
            lane = slice(gi * 128, (gi + 1) * 128)
            wide = slice(gi * 512, (gi + 1) * 512)
            _, vjp = jax.vjp(_ssd_group, xs_ref[:, wide], b_ref[:, lane], c_ref[:, lane], dt_ref[:, lane], st_ref[gi],
                             bias_ref[:, lane], alog_ref[:, lane], dsk_ref[:, lane])
            dxs, db, dc, ddt, ds0, dbias, dalog, ddsk = vjp((dy_ref[:, wide], dstate[gi]))
            dxs_ref[:, wide] = dxs
            db_ref[:, lane] = db
            dc_ref[:, lane] = dc
            ddt_ref[:, lane] = _b(ddt)
            dstate[gi] = ds0
            dbias_ref[:, lane] += dbias
            dalog_ref[:, lane] += dalog
            ddsk_ref[:, lane] += ddsk

    rc = lambda c: nc - 1 - c
    vec = pl.BlockSpec((1, 512), lambda c: (0, 0))
    return pl.pallas_call(
        body, name="ssd_bwd",
        grid=(nc,),
        in_specs=[pl.BlockSpec(memory_space=pl.ANY),
                  pl.BlockSpec((L, 2048), lambda c: (rc(c), 0)), pl.BlockSpec((L, 2048), lambda c: (rc(c), 0)),
                  pl.BlockSpec((L, 512), lambda c: (rc(c), 4)), pl.BlockSpec((L, 512), lambda c: (rc(c), 5)),
                  pl.BlockSpec((L, 512), lambda c: (rc(c), dtb)),
                  pl.BlockSpec((None, g, 512, 128), lambda c: (rc(c), 0, 0, 0)), vec, vec, vec],
        out_specs=[pl.BlockSpec((L, 512), lambda c: (rc(c), dtb)), pl.BlockSpec((L, 2048), lambda c: (rc(c), 0)),
                   pl.BlockSpec((L, 512), lambda c: (rc(c), 0)), pl.BlockSpec((L, 512), lambda c: (rc(c), 0)),
                   vec, vec, vec],
        out_shape=[jax.ShapeDtypeStruct(dzx.shape, dzx.dtype), jax.ShapeDtypeStruct((t, 2048), F32),
                   jax.ShapeDtypeStruct((t, 512), F32), jax.ShapeDtypeStruct((t, 512), F32),
                   jax.ShapeDtypeStruct((1, 512), F32), jax.ShapeDtypeStruct((1, 512), F32),
                   jax.ShapeDtypeStruct((1, 512), F32)],
        scratch_shapes=[pltpu.VMEM((g, 512, 128), F32)],
        input_output_aliases={0: 0},
        compiler_params=_params("arbitrary"),
    )(dzx, dy, xc, xc, xc, zx, states, bias, alog, dsk)


def _gate_tile(y, z, gn):
    gated = y * _silu(z)
    parts = [_rms(gated[:, k * 512:(k + 1) * 512]) for k in range(SSM_GROUPS)]
    return jnp.concatenate(parts, axis=1) * gn


def gate_out_fwd(h, y, zx, gn, w_out):
    t, d = h.shape
    di = y.shape[1]
    tm = _tile(t, 256)

    def body(h_ref, y_ref, z_ref, gn_ref, w_ref, o_ref):
        yn = _gate_tile(y_ref[...], z_ref[...], gn_ref[...])
        o_ref[...] = h_ref[...] + _dot(_b(yn), w_ref[...])

    return pl.pallas_call(
        body, name="gate_out_fwd",
        grid=(t // tm,),
        in_specs=[pl.BlockSpec((tm, d), lambda i: (i, 0)), pl.BlockSpec((tm, di), lambda i: (i, 0)),
                  pl.BlockSpec((tm, di), lambda i: (i, 0)), pl.BlockSpec((1, di), lambda i: (0, 0)),
                  pl.BlockSpec((di, d), lambda i: (0, 0))],
        out_specs=pl.BlockSpec((tm, d), lambda i: (i, 0)),
        out_shape=jax.ShapeDtypeStruct((t, d), F32),
        compiler_params=_params("arbitrary"),
    )(h, y, zx, gn, w_out)


def gate_out_bwd(dy, y, zx, gn, w_out, n_zx, dep):
    t, d = dy.shape
    di = y.shape[1]
    tm = _tile(t, 256)

    def body(dy_ref, y_ref, z_ref, gn_ref, w_ref, dep_ref, dz_ref, dys_ref, yn_ref, dgn_ref):
        @pl.when(pl.program_id(0) == 0)
        def _():
            dgn_ref[...] = jnp.zeros_like(dgn_ref)

        yn, vjp = jax.vjp(_gate_tile, y_ref[...], z_ref[...], gn_ref[...])
        dyn = _dot_nt(_b(dy_ref[...]), w_ref[...])
        dys, dz, dgn = vjp(dyn)
        yn_ref[...] = _b(yn)
        dys_ref[...] = dys
        dz_ref[...] = _b(dz)
        dgn_ref[...] += dgn

    return pl.pallas_call(
        body, name="gate_out_bwd",
        grid=(t // tm,),
        in_specs=[pl.BlockSpec((tm, d), lambda i: (i, 0)), pl.BlockSpec((tm, di), lambda i: (i, 0)),
                  pl.BlockSpec((tm, di), lambda i: (i, 0)), pl.BlockSpec((1, di), lambda i: (0, 0)),
                  pl.BlockSpec((di, d), lambda i: (0, 0)), DEP_SPEC],
        out_specs=[pl.BlockSpec((tm, di), lambda i: (i, 0)), pl.BlockSpec((tm, di), lambda i: (i, 0)),
                   pl.BlockSpec((tm, di), lambda i: (i, 0)), pl.BlockSpec((1, di), lambda i: (0, 0))],
        out_shape=[jax.ShapeDtypeStruct((t, n_zx), BF16), jax.ShapeDtypeStruct((t, di), F32),
                   jax.ShapeDtypeStruct((t, di), BF16), jax.ShapeDtypeStruct((1, di), F32)],
        compiler_params=_params("arbitrary"),
    )(dy, y, zx, gn, w_out, dep)


def _attn_block(qp, kvp, kvc, biasm, sinks, qg, kg, w_o, first):
    nq = qp.shape[0]
    n_pairs = qp.shape[1] // 128
    hk = n_pairs
    rows = hk * nq
    seg = functools.partial(_xright, m=_blockdiag64(128))
    scale = ATT_HEAD_DIM ** -0.5
    qi = (_iota((rows, 2 * nq), 0) % nq) + nq
    kj = _iota((rows, 2 * nq), 1)
    dist = qi - kj
    valid = (dist >= 0) & (dist < ATT_WINDOW) & (jnp.logical_not(first) | (kj >= nq))
    lo = _iota((nq, 128), 1) < 64
    kv = jnp.concatenate([kvp, kvc], axis=0)
    outs = [None] * n_pairs
    for kvh in range(2):
        kn = _group64_rms(kv[:, kvh * 128:(kvh + 1) * 128], seg) * kg
        vv = kv[:, 256 + kvh * 128:256 + (kvh + 1) * 128]
        pairs = range(kvh * hk // 2, (kvh + 1) * hk // 2)
        qs, sk = [], []
        for p in pairs:
            qn = _group64_rms(qp[:, p * 128:(p + 1) * 128], seg) * qg
            qs += [jnp.where(lo, qn, 0.0), jnp.where(lo, 0.0, qn)]
            sk += [jnp.broadcast_to(sinks[:, h:h + 1], (nq, 1)) for h in (2 * p, 2 * p + 1)]
        sink = jnp.concatenate(sk, axis=0)
        s = _bmm_nt(jnp.concatenate(qs, axis=0), kn) * scale + biasm[kvh * rows:(kvh + 1) * rows]
        s = jnp.where(valid, s, NEG)
        m = lax.stop_gradient(jnp.maximum(jnp.max(s, axis=-1, keepdims=True), sink))
        pexp = jnp.exp(s - m)
        den = jnp.sum(pexp, axis=-1, keepdims=True) + jnp.exp(sink - m)
        o = _bmm(pexp * (1.0 / den), vv)
        for n, p in enumerate(pairs):
            outs[p] = jnp.where(lo, o[2 * n * nq:(2 * n + 1) * nq], o[(2 * n + 1) * nq:(2 * n + 2) * nq])
    o = jnp.concatenate(outs, axis=1)
    return _bmm(o, w_o), o


def attn_fwd(h, qp, kvd, biasm, sinks, qg, kg, w_o):
    t, d = h.shape
    nq = ATT_WINDOW
    nb = t // nq
    nh = qp.shape[1] // ATT_HEAD_DIM

    def body(h_ref, q_ref, kp_ref, kc_ref, bias_ref, s_ref, qg_ref, kg_ref, w_ref, o_ref):
        out, _ = _attn_block(q_ref[...], kp_ref[...], kc_ref[...], bias_ref[...], s_ref[...], qg_ref[...],
                             kg_ref[...], w_ref[...], pl.program_id(0) == 0)
        o_ref[...] = h_ref[...] + out

    vec = pl.BlockSpec((1, 128), lambda i: (0, 0))
    return pl.pallas_call(
        body, name="attn_fwd",
        grid=(nb,),
        in_specs=[pl.BlockSpec((nq, d), lambda i: (i, 0)), pl.BlockSpec((nq, nh * 64), lambda i: (i, 0)),
                  pl.BlockSpec((nq, 512), lambda i: (jnp.maximum(i - 1, 0), 0)),
                  pl.BlockSpec((nq, 512), lambda i: (i, 0)),
                  pl.BlockSpec((nh * nq, 2 * nq), lambda i: (0, 0)), vec, vec, vec,
                  pl.BlockSpec((nh * 64, d), lambda i: (0, 0))],
        out_specs=pl.BlockSpec((nq, d), lambda i: (i, 0)),
        out_shape=jax.ShapeDtypeStruct((t, d), F32),
        compiler_params=_params("arbitrary"),
    )(h, qp, kvd, kvd, biasm, sinks, qg, kg, w_o)


def attn_bwd(dy, qp, kvd, biasm, sinks, qg, kg, w_o, dep):
    t, d = dy.shape
    nq = ATT_WINDOW
    nb = t // nq
    nh = qp.shape[1] // ATT_HEAD_DIM

    def body(dy_ref, q_ref, kp_ref, kc_ref, bias_ref, s_ref, qg_ref, kg_ref, w_ref, dep_ref,
             dq_ref, dkv_ref, o_ref, dbias_ref, ds_ref, dqg_ref, dkg_ref, carry):
        i = pl.program_id(0)

        @pl.when(i == 0)
        def _():
            carry[...] = jnp.zeros_like(carry)
            dbias_ref[...] = jnp.zeros_like(dbias_ref)
            ds_ref[...] = jnp.zeros_like(ds_ref)
            dqg_ref[...] = jnp.zeros_like(dqg_ref)
            dkg_ref[...] = jnp.zeros_like(dkg_ref)

        @pl.when(i < nb)
        def _():
            fn = functools.partial(_attn_block, w_o=w_ref[...], first=(i == 0))
            (_, o), vjp = jax.vjp(fn, q_ref[...], kp_ref[...], kc_ref[...], bias_ref[...], s_ref[...],
                                  qg_ref[...], kg_ref[...])
            dq, dkp, dkc, dbias, dsk, dqg, dkg = vjp((dy_ref[...], jnp.zeros((nq, nh * 64), F32)))
            dq_ref[...] = _b(dq)
            o_ref[...] = _b(o)
            dkv_ref[...] = _b(_fold64(carry[...] + dkp))
            carry[...] = dkc
            dbias_ref[...] += dbias
            ds_ref[...] += dsk
            dqg_ref[...] += _fold64(dqg)
            dkg_ref[...] += _fold64(dkg)

        @pl.when(i == nb)
        def _():
            dkv_ref[...] = _b(_fold64(carry[...]))

    cl = lambda i: jnp.minimum(i, nb - 1)
    vec = pl.BlockSpec((1, 128), lambda i: (0, 0))
    return pl.pallas_call(
        body, name="attn_bwd",
        grid=(nb + 1,),
        in_specs=[pl.BlockSpec((nq, d), lambda i: (cl(i), 0)), pl.BlockSpec((nq, nh * 64), lambda i: (cl(i), 0)),
                  pl.BlockSpec((nq, 512), lambda i: (jnp.maximum(cl(i) - 1, 0), 0)),
                  pl.BlockSpec((nq, 512), lambda i: (cl(i), 0)),
                  pl.BlockSpec((nh * nq, 2 * nq), lambda i: (0, 0)), vec, vec, vec,
                  pl.BlockSpec((nh * 64, d), lambda i: (0, 0)), DEP_SPEC],
        out_specs=[pl.BlockSpec((nq, nh * 64), lambda i: (cl(i), 0)),
                   pl.BlockSpec((nq, 512), lambda i: (jnp.maximum(i - 1, 0), 0)),
                   pl.BlockSpec((nq, nh * 64), lambda i: (cl(i), 0)),
                   pl.BlockSpec((nh * nq, 2 * nq), lambda i: (0, 0)), vec, vec, vec],
        out_shape=[jax.ShapeDtypeStruct((t, nh * 64), BF16), jax.ShapeDtypeStruct((t, 512), BF16),
                   jax.ShapeDtypeStruct((t, nh * 64), BF16), jax.ShapeDtypeStruct((nh * nq, 2 * nq), F32),
                   jax.ShapeDtypeStruct((1, 128), F32), jax.ShapeDtypeStruct((1, 128), F32),
                   jax.ShapeDtypeStruct((1, 128), F32)],
        scratch_shapes=[pltpu.VMEM((nq, 512), F32)],
        compiler_params=_params("arbitrary"),
    )(dy, qp, kvd, kvd, biasm, sinks, qg, kg, w_o, dep)


def _t5_buckets():
    nq = ATT_WINDOW
    dist = (np.arange(nq)[:, None] + nq) - np.arange(2 * nq)[None, :]
    n = np.maximum(dist, 0)
    max_exact = REL_BUCKETS // 2
    nf = np.maximum(n, 1).astype(np.float32)
    large = max_exact + (np.log(nf / max_exact) / math.log(ATT_WINDOW / max_exact)
                         * (REL_BUCKETS - max_exact)).astype(np.int32)
    large = np.minimum(large, REL_BUCKETS - 1)
    return np.where(n < max_exact, n, large).astype(np.int32)


def rel_bias_bwd(dbias, buckets):
    nh = dbias.shape[0]

    def body(db_ref, bk_ref, o_ref):
        bk = bk_ref[...]
        lane = _iota((1, 128), 1)
        row = _iota((REL_BUCKETS, 128), 0)
        acc = jnp.zeros((REL_BUCKETS, 128), F32)
        for h in range(nh):
            dbh = db_ref[h]
            for b in range(REL_BUCKETS):
                v = jnp.sum(jnp.where(bk == b, dbh, 0.0))
                acc = acc + jnp.where((row == b) & (lane == h), v, 0.0)
        o_ref[...] = acc

    return pl.pallas_call(
        body, name="rel_bias_bwd",
        out_shape=jax.ShapeDtypeStruct((REL_BUCKETS, 128), F32),
        compiler_params=_params(),
    )(dbias, buckets)


def loss_head(y, target):
    t, d = y.shape
    tm = _tile(t, 512)

    def body(y_ref, t_ref, l_ref, dy_ref):
        @pl.when(pl.program_id(0) == 0)
        def _():
            l_ref[...] = jnp.zeros_like(l_ref)

        e = y_ref[...] - t_ref[...]
        l_ref[...] += 0.5 * jnp.sum(jnp.mean(e * e, axis=-1, keepdims=True), axis=0, keepdims=True)
        dy_ref[...] = e * (1.0 / d)

    return pl.pallas_call(
        body, name="loss_head",
        grid=(t // tm,),
        in_specs=[pl.BlockSpec((tm, d), lambda i: (i, 0)), pl.BlockSpec((tm, d), lambda i: (i, 0))],
        out_specs=[pl.BlockSpec((1, 1), lambda i: (0, 0)), pl.BlockSpec((tm, d), lambda i: (i, 0))],
        out_shape=[jax.ShapeDtypeStruct((1, 1), F32), jax.ShapeDtypeStruct((t, d), F32)],
        compiler_params=_params("arbitrary"),
    )(y, target)


def adamw(w, g, m, v):
    r, c = w.shape
    tr = r if r <= 512 else _tile(r, 256)

    def body(w_ref, g_ref, m_ref, v_ref, d_ref, nm_ref, nv_ref):
        gg = g_ref[...]
        nm = ADAM_B1 * m_ref[...] + (1.0 - ADAM_B1) * gg
        nv = ADAM_B2 * v_ref[...] + (1.0 - ADAM_B2) * (gg * gg)
        m_hat = nm / (1.0 - ADAM_B1 ** ADAM_STEP)
        v_hat = nv / (1.0 - ADAM_B2 ** ADAM_STEP)
        d_ref[...] = -ADAM_LR * (m_hat / (jnp.sqrt(v_hat) + ADAM_EPS) + ADAM_WD * w_ref[...])
        nm_ref[...] = nm
        nv_ref[...] = nv

    spec = pl.BlockSpec((tr, c), lambda i: (i, 0))
    shp = jax.ShapeDtypeStruct((r, c), F32)
    return pl.pallas_call(
        body, name="adamw",
        grid=(r // tr,),
        in_specs=[spec] * 4, out_specs=[spec] * 3, out_shape=[shp] * 3,
        compiler_params=_params("arbitrary"),
    )(w, g, m, v)


def _my_pos():
    return lax.axis_index("x"), lax.axis_index("y"), lax.axis_index("c")


def _other_chips(x, y):
    return [(1 - x, y), (x, 1 - y), (1 - x, 1 - y)]


def _chip_id(x, y):
    return 2 * x + y


HBM_SPEC = pl.BlockSpec(memory_space=pltpu.HBM)
SEM_SPEC = pl.BlockSpec(memory_space=pltpu.SEMAPHORE)
DATAFLOW = pltpu.SideEffectType.DATAFLOW_SIDE_EFFECTING


def _in_hbm(a):
    return pltpu.with_memory_space_constraint(a, pltpu.HBM)


def _ici_gather_copy(buf, p, j, chip, c, to, send_sems, recv_sems):
    blk = buf.at[_chip_id(*chip), c]
    return pltpu.make_async_remote_copy(
        src_ref=blk, dst_ref=blk, send_sem=send_sems.at[3 * p + j], recv_sem=recv_sems.at[3 * p + j],
        device_id=to, device_id_type=MESH)


def gather_start(bufs, after, tag):
    n = len(bufs)

    def body(*refs):
        ins = refs[:n]
        send_sems, recv_sems = refs[n + 1], refs[n + 2]
        token = refs[2 * n + 3]
        x, y, c = _my_pos()
        for p in range(n):
            for j, chip in enumerate(_other_chips(x, y)):
                _ici_gather_copy(ins[p], p, j, (x, y), c, (*chip, c), send_sems, recv_sems).start()
        token[...] = jnp.zeros_like(token)

    out = pl.pallas_call(
        body, name="gather_start_" + tag,
        in_specs=[HBM_SPEC] * n + [DEP_SPEC],
        out_specs=(SEM_SPEC, SEM_SPEC, *([HBM_SPEC] * n), pl.BlockSpec(memory_space=pltpu.VMEM)),
        out_shape=(pltpu.SemaphoreType.DMA((3 * n,)), pltpu.SemaphoreType.DMA((3 * n,)),
                   *[pltpu.HBM(b.shape, b.dtype) for b in bufs], jax.ShapeDtypeStruct((8, 128), F32)),
        input_output_aliases={p: 2 + p for p in range(n)},
        compiler_params=pltpu.CompilerParams(has_side_effects=DATAFLOW),
    )(*[_in_hbm(b) for b in bufs], after)
    return out[0], out[1], list(out[2:2 + n]), out[2 + n]


def gather_wait(send_sems, recv_sems, bufs, after, tag):
    n = len(bufs)

    def body(*refs):
        ins = refs[:n]
        send_sems, recv_sems = refs[n], refs[n + 1]
        x, y, c = _my_pos()
        for p in range(n):
            for j, chip in enumerate(_other_chips(x, y)):
                _ici_gather_copy(ins[p], p, j, (x, y), c, (*chip, c), send_sems, recv_sems).wait_send()
                _ici_gather_copy(ins[p], p, j, chip, c, (x, y, c), send_sems, recv_sems).wait_recv()

    out = pl.pallas_call(
        body, name="gather_wait_" + tag,
        in_specs=[HBM_SPEC] * n + [SEM_SPEC, SEM_SPEC, pl.BlockSpec(memory_space=pl.ANY)],
        out_specs=[HBM_SPEC] * n,
        out_shape=[pltpu.HBM(b.shape, b.dtype) for b in bufs],
        input_output_aliases={p: p for p in range(n)},
        compiler_params=pltpu.CompilerParams(has_side_effects=DATAFLOW),
    )(*bufs, send_sems, recv_sems, after)
    return list(out)


def forward_to_sibling(bufs):
    n = len(bufs)

    def body(*refs):
        outs = refs[n:2 * n]
        send_sems, recv_sems = refs[2 * n:]
        x, y, c = _my_pos()
        chips = _other_chips(x, y)
        sent = []
        for p in range(n):
            for j, chip in enumerate(chips):
                cp = _ici_gather_copy(outs[p], p, j, chip, c, (x, y, 1 - c), send_sems, recv_sems)
                cp.start()
                sent.append(cp)
        for p in range(n):
            for j, chip in enumerate(chips):
                _ici_gather_copy(outs[p], p, j, chip, 1 - c, (x, y, c), send_sems, recv_sems).wait_recv()
        for cp in sent:
            cp.wait_send()

    any_spec = pl.BlockSpec(memory_space=pl.ANY)
    return pl.pallas_call(
        body, name="forward_to_sibling",
        in_specs=[any_spec] * n, out_specs=[any_spec] * n,
        out_shape=[jax.ShapeDtypeStruct(b.shape, b.dtype) for b in bufs],
        scratch_shapes=[pltpu.SemaphoreType.DMA((3 * n,)), pltpu.SemaphoreType.DMA((3 * n,))],
        input_output_aliases={p: p for p in range(n)},
    )(*bufs)


def allreduce_small(v):
    r, c = v.shape

    def body(v_ref, o_ref, buf, send_sems, recv_sems):
        x, y, cc = _my_pos()
        me = 4 * x + 2 * y + cc
        buf[me] = v_ref[...]
        copies = []
        for k in range(1, 8):
            dx, dy, dc = (k >> 2) & 1, (k >> 1) & 1, k & 1
            peer = (x ^ dx, y ^ dy, cc ^ dc)
            cp = pltpu.make_async_remote_copy(
                src_ref=v_ref, dst_ref=buf.at[me], send_sem=send_sems.at[k - 1], recv_sem=recv_sems.at[k - 1],
                device_id=peer, device_id_type=MESH)
            cp.start()
            copies.append(cp)
        for cp in copies:
            cp.wait_recv()
        for cp in copies:
            cp.wait_send()
        acc = buf[0]
        for k in range(1, 8):
            acc = acc + buf[k]
        o_ref[...] = acc

    vm = pl.BlockSpec(memory_space=pltpu.VMEM)
    return pl.pallas_call(
        body, name="allreduce_small",
        in_specs=[vm], out_specs=vm,
        out_shape=jax.ShapeDtypeStruct((r, c), F32),
        scratch_shapes=[pltpu.VMEM((8, r, c), F32), pltpu.SemaphoreType.DMA((7,)), pltpu.SemaphoreType.DMA((7,))],
    )(v)


def _sibling_half_copy(grad, land, p, c, sibling, send_sems, recv_sems):
    return pltpu.make_async_remote_copy(
        src_ref=grad.at[:, 1 - c], dst_ref=land, send_sem=send_sems.at[p], recv_sem=recv_sems.at[p],
        device_id=sibling, device_id_type=MESH)


def sibling_halves_start(grads, tag):
    n = len(grads)
    lands = [lax.empty((g.shape[0],) + g.shape[2:], g.dtype) for g in grads]

    def body(*refs):
        ins, land = refs[:n], refs[n:2 * n]
        send_sems, recv_sems = refs[2 * n], refs[2 * n + 1]
        token = refs[4 * n + 2]
        x, y, c = _my_pos()
        for p in range(n):
            _sibling_half_copy(ins[p], land[p], p, c, (x, y, 1 - c), send_sems, recv_sems).start()
        token[...] = jnp.zeros_like(token)

    out = pl.pallas_call(
        body, name="sibling_halves_start_" + tag,
        in_specs=[HBM_SPEC] * (2 * n),
        out_specs=(SEM_SPEC, SEM_SPEC, *([HBM_SPEC] * (2 * n)), pl.BlockSpec(memory_space=pltpu.VMEM)),
        out_shape=(pltpu.SemaphoreType.DMA((n,)), pltpu.SemaphoreType.DMA((n,)),
                   *[pltpu.HBM(a.shape, a.dtype) for a in grads + lands], jax.ShapeDtypeStruct((8, 128), F32)),
        input_output_aliases={i: 2 + i for i in range(2 * n)},
        compiler_params=pltpu.CompilerParams(has_side_effects=DATAFLOW),
    )(*[_in_hbm(a) for a in grads + lands])
    return out[0], out[1], list(out[2:2 + n]), list(out[2 + n:2 + 2 * n]), out[2 + 2 * n]


def sibling_halves_wait(send_sems, recv_sems, grads, lands, after, tag):
    n = len(grads)

    def body(*refs):
        ins, land = refs[:n], refs[n:2 * n]
        send_sems, recv_sems = refs[2 * n], refs[2 * n + 1]
        x, y, c = _my_pos()
        for p in range(n):
            cp = _sibling_half_copy(ins[p], land[p], p, c, (x, y, 1 - c), send_sems, recv_sems)
            cp.wait_send()
            cp.wait_recv()

    out = pl.pallas_call(
        body, name="sibling_halves_wait_" + tag,
        in_specs=[HBM_SPEC] * (2 * n) + [SEM_SPEC, SEM_SPEC, pl.BlockSpec(memory_space=pl.ANY)],
        out_specs=[HBM_SPEC] * (2 * n),
        out_shape=[pltpu.HBM(a.shape, a.dtype) for a in grads + lands],
        input_output_aliases={i: i for i in range(2 * n)},
        compiler_params=pltpu.CompilerParams(has_side_effects=DATAFLOW),
    )(*grads, *lands, send_sems, recv_sems, after)
    return list(out[:n]), list(out[n:])


def _chip_partial_copy(part, land, p, j, chip, c, send_sems, recv_sems):
    return pltpu.make_async_remote_copy(
        src_ref=part.at[_chip_id(*chip)], dst_ref=land.at[j], send_sem=send_sems.at[3 * p + j],
        recv_sem=recv_sems.at[3 * p + j], device_id=(*chip, c), device_id_type=MESH)


def chip_partials_start(parts, tag):
    n = len(parts)
    lands = [lax.empty((N_CHIPS - 1,) + s.shape[1:], s.dtype) for s in parts]

    def body(*refs):
        ins, land = refs[:n], refs[n:2 * n]
        send_sems, recv_sems = refs[2 * n], refs[2 * n + 1]
        token = refs[4 * n + 2]
        x, y, c = _my_pos()
        for p in range(n):
            for j, chip in enumerate(_other_chips(x, y)):
                _chip_partial_copy(ins[p], land[p], p, j, chip, c, send_sems, recv_sems).start()
        token[...] = jnp.zeros_like(token)

    out = pl.pallas_call(
        body, name="chip_partials_start_" + tag,
        in_specs=[HBM_SPEC] * (2 * n),
        out_specs=(SEM_SPEC, SEM_SPEC, *([HBM_SPEC] * (2 * n)), pl.BlockSpec(memory_space=pltpu.VMEM)),
        out_shape=(pltpu.SemaphoreType.DMA((3 * n,)), pltpu.SemaphoreType.DMA((3 * n,)),
                   *[pltpu.HBM(a.shape, a.dtype) for a in parts + lands], jax.ShapeDtypeStruct((8, 128), F32)),
        input_output_aliases={i: 2 + i for i in range(2 * n)},
        compiler_params=pltpu.CompilerParams(has_side_effects=DATAFLOW),
    )(*[_in_hbm(a) for a in parts + lands])
    return out[0], out[1], list(out[2:2 + n]), list(out[2 + n:2 + 2 * n]), out[2 + 2 * n]


def chip_partials_wait(send_sems, recv_sems, parts, lands, after, tag):
    n = len(parts)

    def body(*refs):
        ins, land = refs[:n], refs[n:2 * n]
        send_sems, recv_sems = refs[2 * n], refs[2 * n + 1]
        x, y, c = _my_pos()
        for p in range(n):
            for j, chip in enumerate(_other_chips(x, y)):
                cp = _chip_partial_copy(ins[p], land[p], p, j, chip, c, send_sems, recv_sems)
                cp.wait_send()
                cp.wait_recv()

    out = pl.pallas_call(
        body, name="chip_partials_wait_" + tag,
        in_specs=[HBM_SPEC] * (2 * n) + [SEM_SPEC, SEM_SPEC, pl.BlockSpec(memory_space=pl.ANY)],
        out_specs=[HBM_SPEC] * (2 * n),
        out_shape=[pltpu.HBM(a.shape, a.dtype) for a in parts + lands],
        input_output_aliases={i: i for i in range(2 * n)},
        compiler_params=pltpu.CompilerParams(has_side_effects=DATAFLOW),
    )(*parts, *lands, send_sems, recv_sems, after)
    return list(out[n:])


def share_with_sibling(bufs):
    n = len(bufs)

    def body(*refs):
        outs = refs[n:2 * n]
        send_sems, recv_sems = refs[2 * n:]
        x, y, c = _my_pos()
        copies = []
        for p in range(n):
            cp = pltpu.make_async_remote_copy(
                src_ref=outs[p].at[c], dst_ref=outs[p].at[c], send_sem=send_sems.at[p], recv_sem=recv_sems.at[p],
                device_id=(x, y, 1 - c), device_id_type=MESH)
            cp.start()
            copies.append(cp)
        for p in range(n):
            pltpu.make_async_remote_copy(
                src_ref=outs[p].at[1 - c], dst_ref=outs[p].at[1 - c], send_sem=send_sems.at[p],
                recv_sem=recv_sems.at[p], device_id=(x, y, 1 - c), device_id_type=MESH).wait_recv()
        for cp in copies:
            cp.wait_send()

    any_spec = pl.BlockSpec(memory_space=pl.ANY)
    return pl.pallas_call(
        body, name="share_with_sibling",
        in_specs=[any_spec] * n, out_specs=[any_spec] * n,
        out_shape=[jax.ShapeDtypeStruct(b.shape, b.dtype) for b in bufs],
        scratch_shapes=[pltpu.SemaphoreType.DMA((n,)), pltpu.SemaphoreType.DMA((n,))],
        input_output_aliases={p: p for p in range(n)},
    )(*bufs)


def add_sibling(g, recv, half):
    _, _, r, c = g.shape
    tr = _tile(r, 256) if r % 256 == 0 else r

    def body(half_ref, g_ref, r_ref, o32_ref, o16_ref):
        s = g_ref[...] + r_ref[...]
        o32_ref[...] = s
        o16_ref[...] = _b(s)

    return pl.pallas_call(
        body, name="add_sibling",
        grid_spec=pltpu.PrefetchScalarGridSpec(
            num_scalar_prefetch=1, grid=(N_CHIPS, r // tr),
            in_specs=[pl.BlockSpec((None, None, tr, c), lambda k, i, hf: (k, hf[0], i, 0)),
                      pl.BlockSpec((None, tr, c), lambda k, i, hf: (k, i, 0))],
            out_specs=[pl.BlockSpec((None, tr, c), lambda k, i, hf: (k, i, 0)),
                       pl.BlockSpec((None, tr, c), lambda k, i, hf: (k, i, 0))]),
        out_shape=[jax.ShapeDtypeStruct((N_CHIPS, r, c), F32), jax.ShapeDtypeStruct((N_CHIPS, r, c), BF16)],
        compiler_params=_params("arbitrary", "arbitrary"),
    )(half, g, recv)


def add_chip_partials(p32, recv, pos):
    _, r, c = p32.shape
    tr = _tile(r, 256) if r % 256 == 0 else r

    def body(pos_ref, p_ref, r_ref, o_ref):
        acc = p_ref[...]
        for j in range(N_CHIPS - 1):
            acc = acc + r_ref[j].astype(F32)
        o_ref[...] = acc

    return pl.pallas_call(
        body, name="add_chip_partials",
        grid_spec=pltpu.PrefetchScalarGridSpec(
            num_scalar_prefetch=1, grid=(r // tr,),
            in_specs=[pl.BlockSpec((None, tr, c), lambda i, ps: (ps[0], i, 0)),
                      pl.BlockSpec((N_CHIPS - 1, tr, c), lambda i, ps: (0, i, 0))],
            out_specs=pl.BlockSpec((None, tr, c), lambda i, ps: (ps[1], i, 0))),
        out_shape=jax.ShapeDtypeStruct((2, r, c), F32),
        compiler_params=_params("arbitrary"),
    )(pos, p32, recv)


def cast_into_gather(w, pos, dep, row0=0, nrows=None):
    c = w.shape[1]
    nrows = w.shape[0] if nrows is None else nrows
    r = nrows // 2
    common = math.gcd(r, row0) if row0 else r
    tr = max(w for w in range(16, min(common, 512) + 1, 16) if common % w == 0)
    nt = r // tr

    def body(pos_ref, w_ref, dep_ref, o_ref):
        o_ref[...] = _b(w_ref[...])

    return pl.pallas_call(
        body, name="cast_into_gather",
        grid_spec=pltpu.PrefetchScalarGridSpec(
            num_scalar_prefetch=1, grid=(2, nt),
            in_specs=[pl.BlockSpec((tr, c), lambda hf, i, ps: (row0 // tr + hf * nt + i, 0)), DEP_SPEC],
            out_specs=pl.BlockSpec((None, None, tr, c), lambda hf, i, ps: (ps[0], hf, i, 0))),
        out_shape=jax.ShapeDtypeStruct((N_CHIPS, 2, r, c), BF16),
        compiler_params=_params("arbitrary", "arbitrary"),
    )(pos, w, dep)


def build_bias(rel, buckets):
    nb, nh = rel.shape

    def body(rel_ref, bk_ref, o_ref):
        bk = bk_ref[...]
        for h in range(nh):
            acc = jnp.zeros(bk.shape, F32)
            for b in range(nb):
                acc = jnp.where(bk == b, rel_ref[b, h], acc)
            o_ref[h] = acc

    return pl.pallas_call(
        body, name="build_bias",
        in_specs=[pl.BlockSpec(memory_space=pltpu.SMEM), pl.BlockSpec(memory_space=pltpu.VMEM)],
        out_specs=pl.BlockSpec(memory_space=pltpu.VMEM),
        out_shape=jax.ShapeDtypeStruct((nh,) + buckets.shape, F32),
        compiler_params=_params(),
    )(rel, buckets)


SMALL_ROWS = 256


def kernel(x, ffn_norm, ffn_w1, ffn_w3, ffn_w2, ssm_norm, ssm_w_in, ssm_conv_w, ssm_conv_b, ssm_dt_bias, ssm_a_log, ssm_d, ssm_gate_norm, ssm_w_out, kv_norm, w_kv, k_norm, attn_norm, w_q, q_norm, sinks, w_o, rel_bias, loss_target, m_ffn_norm, m_ffn_w1, m_ffn_w3, m_ffn_w2, m_ssm_norm, m_ssm_w_in, m_ssm_conv_w, m_ssm_conv_b, m_ssm_dt_bias, m_ssm_a_log, m_ssm_d, m_ssm_gate_norm, m_ssm_w_out, m_kv_norm, m_w_kv, m_k_norm, m_attn_norm, m_w_q, m_q_norm, m_sinks, m_w_o, m_rel_bias, v_ffn_norm, v_ffn_w1, v_ffn_w3, v_ffn_w2, v_ssm_norm, v_ssm_w_in, v_ssm_conv_w, v_ssm_conv_b, v_ssm_dt_bias, v_ssm_a_log, v_ssm_d, v_ssm_gate_norm, v_ssm_w_out, v_kv_norm, v_w_kv, v_k_norm, v_attn_norm, v_w_q, v_q_norm, v_sinks, v_w_o, v_rel_bias):
    weights = dict(ffn_norm=ffn_norm, ffn_w1=ffn_w1, ffn_w3=ffn_w3, ffn_w2=ffn_w2, ssm_norm=ssm_norm,
                   ssm_w_in=ssm_w_in, ssm_conv_w=ssm_conv_w, ssm_conv_b=ssm_conv_b, ssm_dt_bias=ssm_dt_bias,
                   ssm_a_log=ssm_a_log, ssm_d=ssm_d, ssm_gate_norm=ssm_gate_norm, ssm_w_out=ssm_w_out,
                   kv_norm=kv_norm, w_kv=w_kv, k_norm=k_norm, attn_norm=attn_norm, w_q=w_q, q_norm=q_norm,
                   sinks=sinks, w_o=w_o, rel_bias=rel_bias)
    m_in = dict(ffn_norm=m_ffn_norm, ffn_w1=m_ffn_w1, ffn_w3=m_ffn_w3, ffn_w2=m_ffn_w2, ssm_norm=m_ssm_norm,
                ssm_w_in=m_ssm_w_in, ssm_conv_w=m_ssm_conv_w, ssm_conv_b=m_ssm_conv_b, ssm_dt_bias=m_ssm_dt_bias,
                ssm_a_log=m_ssm_a_log, ssm_d=m_ssm_d, ssm_gate_norm=m_ssm_gate_norm, ssm_w_out=m_ssm_w_out,
                kv_norm=m_kv_norm, w_kv=m_w_kv, k_norm=m_k_norm, attn_norm=m_attn_norm, w_q=m_w_q, q_norm=m_q_norm,
                sinks=m_sinks, w_o=m_w_o, rel_bias=m_rel_bias)
    v_in = dict(ffn_norm=v_ffn_norm, ffn_w1=v_ffn_w1, ffn_w3=v_ffn_w3, ffn_w2=v_ffn_w2, ssm_norm=v_ssm_norm,
                ssm_w_in=v_ssm_w_in, ssm_conv_w=v_ssm_conv_w, ssm_conv_b=v_ssm_conv_b, ssm_dt_bias=v_ssm_dt_bias,
                ssm_a_log=v_ssm_a_log, ssm_d=v_ssm_d, ssm_gate_norm=v_ssm_gate_norm, ssm_w_out=v_ssm_w_out,
                kv_norm=v_kv_norm, w_kv=v_w_kv, k_norm=v_k_norm, attn_norm=v_attn_norm, w_q=v_w_q, q_norm=v_q_norm,
                sinks=v_sinks, w_o=v_w_o, rel_bias=v_rel_bias)
    return _step(x[0], loss_target[0], weights, m_in, v_in)


BIG = ("ffn_w1", "ffn_w3", "ffn_w2", "ssm_w_in", "ssm_w_out", "w_kv", "w_q", "w_o")
SMALL = (("ffn_norm", True), ("ssm_norm", True), ("ssm_conv_w", True), ("ssm_conv_b", True),
         ("ssm_gate_norm", True), ("ssm_dt_bias", False), ("ssm_a_log", False), ("ssm_d", False),
         ("kv_norm", False), ("k_norm", False), ("attn_norm", False), ("q_norm", False), ("sinks", False),
         ("rel_bias", False))


FFN_W = BIG[:3]


def _small_layout(weights):
    off, table = 0, {}
    for name, sharded in SMALL:
        shape = weights[name].shape
        full = shape[:-1] + (shape[-1] * N_CHIPS,) if sharded else shape
        n = int(np.prod(full))
        table[name] = (off, full, sharded)
        off += n
    assert off <= SMALL_ROWS * 128
    return table


def _place_small(values, table, chip, scale_mask):
    flat = jnp.zeros((SMALL_ROWS * 128,), F32)
    for name, (off, full, sharded) in table.items():
        if not sharded:
            continue
        v = values[name].astype(F32)
        lead = int(np.prod(full[:-1]))
        w = v.shape[-1]
        blk = jnp.zeros((lead, full[-1]), F32)
        blk = lax.dynamic_update_slice(blk, v.reshape(lead, w) * scale_mask, (0, chip * w))
        flat = lax.dynamic_update_slice(flat, blk.reshape(-1), (off,))
    return flat.reshape(SMALL_ROWS, 128)


def _take_small(mat, table, name):
    off, full, _ = table[name]
    n = int(np.prod(full))
    return mat.reshape(-1)[off:off + n].reshape(full)


def _step(x, target, weights, m_in, v_in):
    t, d = x.shape
    xi, yi, ci = lax.axis_index("x"), lax.axis_index("y"), lax.axis_index("c")
    chip = 2 * xi + yi
    pos_arr = jnp.stack([chip, ci]).astype(jnp.int32)
    half_arr = jnp.reshape(ci, (1,)).astype(jnp.int32)

    fs = weights["ffn_w1"].shape[-1]
    ffn_rows = {"ffn_w1": d, "ffn_w3": d, "ffn_w2": fs}
    w2d = {n: weights[n].reshape(-1, weights[n].shape[-1]) for n in BIG}
    mamba_w = ("ssm_w_in", "ssm_w_out")
    late_w = ("w_kv", "w_q", "w_o")
    fs_, fr_, fbufs, tok_f = gather_start(
        [cast_into_gather(w2d[n], pos_arr, pos_arr, 0, ffn_rows[n]) for n in FFN_W], pos_arr, "first")
    ms, mr, mbufs, tok_m = gather_start([cast_into_gather(w2d[n], pos_arr, tok_f) for n in mamba_w], tok_f, "mamba")
    ls, lr, lbufs, tok_l = gather_start(
        [cast_into_gather(w2d[n], pos_arr, tok_f, ffn_rows[n], 3 * ffn_rows[n]) for n in FFN_W]
        + [cast_into_gather(w2d[n], pos_arr, tok_f) for n in late_w], tok_m, "late")
    first = forward_to_sibling(gather_wait(fs_, fr_, fbufs, tok_l, "first"))
    no_dep = jnp.zeros((8, 128), F32)
    table = _small_layout(weights)
    south = (ci == 0).astype(F32)
    small = allreduce_small(_place_small(weights, table, chip, south))
    sp = {n: _take_small(small, table, n) if sh else weights[n] for n, sh in SMALL}

    ffn_first = [first[0].reshape(N_CHIPS, 1, d, fs), first[1].reshape(N_CHIPS, 1, d, fs),
                 first[2].reshape(N_CHIPS, 1, fs, d)]
    ffn_g = sp["ffn_norm"]
    h0 = x
    h1, a00, b00 = ffn_fwd(h0, ffn_g[0, 0].reshape(1, d), *ffn_first, 0, no_dep)
    gathered = dict(zip(mamba_w, forward_to_sibling(gather_wait(ms, mr, mbufs, h1, "mamba"))))
    n_in = weights["ssm_w_in"].shape[-1] * N_CHIPS
    di = weights["ssm_w_out"].shape[1] * N_CHIPS
    nheads = di // SSM_HEAD_DIM
    conv_dim = n_in - di - nheads
    w_in_full = jnp.moveaxis(gathered["ssm_w_in"].reshape(N_CHIPS, d, n_in // N_CHIPS), 0, 1).reshape(d, n_in)
    hpg = nheads // SSM_GROUPS

    def spread_heads(v):
        lead = v.shape[:-1]
        v = v.reshape(lead + (SSM_GROUPS, hpg))
        v = jnp.pad(v, [(0, 0)] * len(lead) + [(0, 0), (0, 128 - hpg)])
        return v.reshape(lead + (SSM_GROUPS * 128,))

    def gather_heads(v):
        lead = v.shape[:-1]
        return v.reshape(lead + (SSM_GROUPS, 128))[..., :hpg].reshape(lead + (nheads,))

    dt_col0 = di + conv_dim
    n_zx = dt_col0 + SSM_GROUPS * 128
    w_in = jnp.concatenate([w_in_full[:, :dt_col0], spread_heads(w_in_full[:, dt_col0:])], axis=1)
    w_out = gathered["ssm_w_out"].reshape(di, d)
    nkv = weights["w_kv"].shape[1] // (2 * ATT_HEAD_DIM)
    assert nkv == 2
    nh = weights["w_q"].shape[-1] // ATT_HEAD_DIM

    ssm_g = sp["ssm_norm"].reshape(1, d)
    cw = jnp.pad(sp["ssm_conv_w"].reshape(SSM_CONV, conv_dim), [(0, 8 - SSM_CONV), (0, 0)])
    cb = sp["ssm_conv_b"].reshape(1, conv_dim)
    gate_g = sp["ssm_gate_norm"].reshape(1, di)
    dt_bias = spread_heads(sp["ssm_dt_bias"].reshape(1, nheads))
    a_log = spread_heads(sp["ssm_a_log"].reshape(1, nheads))
    d_skip = spread_heads(sp["ssm_d"].reshape(1, nheads))
    kv_g = sp["kv_norm"].reshape(1, d)
    k_g = jnp.tile(sp["k_norm"].reshape(1, ATT_HEAD_DIM), (1, 2))
    attn_g = sp["attn_norm"].reshape(1, d)
    q_g = jnp.tile(sp["q_norm"].reshape(1, ATT_HEAD_DIM), (1, 2))
    sink_row = jnp.pad(sp["sinks"].reshape(1, nh), [(0, 0), (0, 128 - nh)])
    buckets = jnp.asarray(_t5_buckets())
    biasm = build_bias(sp["rel_bias"], buckets).reshape(nh * ATT_WINDOW, 2 * ATT_WINDOW)

    zx = norm_mm(h1, ssm_g, w_in)
    xc = conv_fwd(zx, cw, cb, di)
    y_ssd, states = ssd_fwd(xc, zx, dt_bias, a_log, d_skip, dt_col0)
    h2 = gate_out_fwd(h1, y_ssd, zx, gate_g, w_out)

    late = forward_to_sibling(gather_wait(ls, lr, lbufs, h2, "late"))
    ffn_rest = [late[0].reshape(N_CHIPS, 3, d, fs), late[1].reshape(N_CHIPS, 3, d, fs),
                late[2].reshape(N_CHIPS, 3, fs, d)]
    gathered.update(zip(late_w, late[3:]))
    wkv_heads = gathered["w_kv"].reshape(d, 2 * nkv, 1, ATT_HEAD_DIM)
    w_kvd = jnp.broadcast_to(wkv_heads, (d, 2 * nkv, 2, ATT_HEAD_DIM)).reshape(d, 4 * nkv * ATT_HEAD_DIM)
    wq = gathered["w_q"].reshape(d, -1)
    wo = gathered["w_o"].reshape(-1, d)

    def ffn_w(layer, idx):
        blk = 2 * layer + idx
        return (*ffn_first, 0) if blk == 0 else (*ffn_rest, blk - 1)

    h3, a01, b01 = ffn_fwd(h2, ffn_g[0, 1].reshape(1, d), *ffn_w(0, 1), no_dep)
    kvd = norm_mm(h3, kv_g, w_kvd)
    h4, a10, b10 = ffn_fwd(h3, ffn_g[1, 0].reshape(1, d), *ffn_w(1, 0), no_dep)
    qp = norm_mm(h4, attn_g, wq)
    h5 = attn_fwd(h4, qp, kvd, biasm, sink_row, q_g, k_g, wo)
    h6, a11, b11 = ffn_fwd(h5, ffn_g[1, 1].reshape(1, d), *ffn_w(1, 1), no_dep)
    loss_part, d6 = loss_head(h6, target)
    loss = lax.psum(loss_part[0, 0], ("x", "y", "c"))

    gfn = [[None, None], [None, None]]

    pending = {}

    def swap_start(pieces, tag):
        views = [g.reshape(N_CHIPS, 2, g.shape[1] // 2, g.shape[2]) for _, g in pieces]
        ss, rs, views, lands, token = sibling_halves_start(views, tag)
        pending[tag] = dict(keys=[k for k, _ in pieces], swap=(ss, rs, views, lands))
        return token

    def partials_start(tag, after):
        views, recv1 = sibling_halves_wait(*pending[tag]["swap"], after, tag)
        p32, p16 = zip(*[add_sibling(g, r, half_arr) for g, r in zip(views, recv1)])
        ss, rs, parts, lands, token = chip_partials_start(list(p16), tag)
        pending[tag].update(p32=p32, partials=(ss, rs, parts, lands))
        return token

    def ffn_back(h_in, dy, a_s, b_s, layer, idx, dep, wdep):
        dh, u, da, db, s, dg = ffn_bwd(h_in, dy, ffn_g[layer, idx].reshape(1, d), a_s, b_s, *ffn_w(layer, idx), dep)
        gfn[layer][idx] = dg
        return dh, [(("ffn_w1", layer, idx), wgrad_grouped_b(u, da, wdep)),
                    (("ffn_w3", layer, idx), wgrad_grouped_b(u, db, no_dep)),
                    (("ffn_w2", layer, idx), wgrad_grouped_a(s, dy, no_dep, 0.5))]

    d5, pieces = ffn_back(h5, d6, a11, b11, 1, 1, no_dep, no_dep)
    tok = swap_start(pieces, "ffn11")
    dqp, dkvd, o16, dbiasm, dsinks, dqg, dkg = attn_bwd(d5, qp, kvd, biasm, sink_row, q_g, k_g, wo, tok)
    tok = partials_start("ffn11", dqp)
    g_wo = wgrad(o16, d5)
    d4, u_q, g_attn_norm = norm_mm_bwd(h4, attn_g, wq, dqp, d5, tok)
    g_wq = wgrad(u_q, dqp)
    d3a, pieces = ffn_back(h3, d4, a10, b10, 1, 0, no_dep, no_dep)
    pieces += [(("w_o",), g_wo.reshape(N_CHIPS, -1, d)), (("w_q",), g_wq.reshape(N_CHIPS, d // N_CHIPS, -1))]
    tok = swap_start(pieces, "ffn10")
    d3, u_kv, g_kv_norm = norm_mm_bwd(h3, kv_g, w_kvd, dkvd, d3a, tok, 0.5)
    tok = partials_start("ffn10", d3)
    g_wkvd = wgrad(u_kv, dkvd)
    g_wkv = g_wkvd.reshape(d, 2 * nkv, 2, ATT_HEAD_DIM)[:, :, 0, :].reshape(d, 2 * nkv * ATT_HEAD_DIM)
    d2, pieces = ffn_back(h2, d3, a01, b01, 0, 1, tok, no_dep)
    pieces += [(("w_kv",), g_wkv.reshape(N_CHIPS, d // N_CHIPS, -1))]
    tok = swap_start(pieces, "ffn01")
    dzx, dy_ssd, yn16, g_gate = gate_out_bwd(d2, y_ssd, zx, gate_g, w_out, n_zx, tok)
    tok = partials_start("ffn01", dy_ssd)
    g_wout = wgrad(yn16, d2)
    dzx, dxs, dbm, dcm, g_dtb, g_alog, g_dsk = ssd_bwd(dzx, dy_ssd, xc, zx, states, dt_bias, a_log, d_skip, dt_col0)
    dzx, g_cw, g_cb = conv_bwd(dzx, zx, dxs, dbm, dcm, cw, cb, di)
    d1, u_in, g_ssm_norm = norm_mm_bwd(h1, ssm_g, w_in, dzx, d2, tok)
    g_win = wgrad(u_in, dzx)
    g_win_full = jnp.concatenate([g_win[:, :dt_col0], gather_heads(g_win[:, dt_col0:])], axis=1)
    pieces = [(("ssm_w_in",), jnp.moveaxis(g_win_full.reshape(d, N_CHIPS, n_in // N_CHIPS), 1, 0)),
              (("ssm_w_out",), g_wout.reshape(N_CHIPS, di // N_CHIPS, d))]
    tok = swap_start(pieces, "mamba")
    grad_x, u0, da0, db0, s0, gfn[0][0] = ffn_bwd(h0, d1, ffn_g[0, 0].reshape(1, d), a00, b00, *ffn_w(0, 0), tok)
    tok = partials_start("mamba", grad_x)
    g1 = wgrad_grouped_b(u0, da0, tok)
    tok = swap_start([(("ffn_w1", 0, 0), g1)], "ffn00a")
    g3 = wgrad_grouped_b(u0, db0, tok)
    tok = partials_start("ffn00a", g3) + swap_start([(("ffn_w3", 0, 0), g3)], "ffn00b")
    g2 = wgrad_grouped_a(s0, d1, tok, 0.5)
    tok = partials_start("ffn00b", g2) + swap_start([(("ffn_w2", 0, 0), g2)], "ffn00")
    g_relb = rel_bias_bwd(dbiasm.reshape(nh, ATT_WINDOW, 2 * ATT_WINDOW), buckets)

    reduced = {}

    def finish(tag, after):
        st = pending[tag]
        lands = chip_partials_wait(*st["partials"], after, tag)
        for k, p, r in zip(st["keys"], st["p32"], lands):
            reduced[k] = add_chip_partials(p, r, pos_arr)
        return reduced[st["keys"][-1]]

    last = finish("ffn10", finish("ffn11", tok))
    tok = partials_start("ffn00", last)
    last = finish("ffn00b", finish("ffn00a", finish("mamba", finish("ffn01", tok))))
    finish("ffn00", last)
    keys = list(reduced)
    shared = dict(zip(keys, share_with_sibling([reduced[k] for k in keys])))
    grads = {}
    for n in FFN_W:
        blocks = [shared[(n, l, i)].reshape(1, ffn_rows[n], -1) for l in range(2) for i in range(2)]
        grads[n] = jnp.concatenate(blocks, axis=0).reshape(weights[n].shape)
    for n in BIG[3:]:
        grads[n] = shared[(n,)].reshape(weights[n].shape)

    small_grads = {
        "ffn_norm": jnp.stack([jnp.stack([gfn[l][i].reshape(d) for i in range(2)]) for l in range(2)]),
        "ssm_norm": g_ssm_norm.reshape(1, d),
        "ssm_conv_w": g_cw[:SSM_CONV].reshape(1, SSM_CONV, conv_dim),
        "ssm_conv_b": g_cb.reshape(1, conv_dim),
        "ssm_gate_norm": g_gate.reshape(1, di),
        "ssm_dt_bias": gather_heads(g_dtb.reshape(1, -1)), "ssm_a_log": gather_heads(g_alog.reshape(1, -1)),
        "ssm_d": gather_heads(g_dsk.reshape(1, -1)),
        "kv_norm": g_kv_norm.reshape(d), "k_norm": dkg[0, :ATT_HEAD_DIM], "attn_norm": g_attn_norm.reshape(1, d),
        "q_norm": dqg[:, :ATT_HEAD_DIM], "sinks": dsinks[:, :nh], "rel_bias": g_relb[:, :nh],
    }
    flat = jnp.zeros((SMALL_ROWS * 128,), F32)
    for name, (off, fshape, _) in table.items():
        flat = lax.dynamic_update_slice(flat, small_grads[name].astype(F32).reshape(-1), (off,))
    small_sum = allreduce_small(flat.reshape(SMALL_ROWS, 128))
    for name, (off, fshape, sharded) in table.items():
        g = _take_small(small_sum, table, name)
        if sharded:
            w = weights[name].shape[-1]
            lead = int(np.prod(fshape[:-1]))
            g = lax.dynamic_slice(g.reshape(lead, fshape[-1]), (0, chip * w), (lead, w)).reshape(weights[name].shape)
        grads[name] = g.reshape(weights[name].shape)

    names = list(weights)
    deltas, new_m, new_v = {}, {}, {}
    small_names = [n for n, _ in SMALL]
    for n in BIG:
        shp = weights[n].shape
        v2 = lambda a: a.reshape(-1, shp[-1])
        dl, nm, nv = adamw(v2(weights[n]), v2(grads[n]), v2(m_in[n]), v2(v_in[n]))
        deltas[n], new_m[n], new_v[n] = dl.reshape(shp), nm.reshape(shp), nv.reshape(shp)
    sizes = [int(np.prod(weights[n].shape)) for n in small_names]
    tot = sum(sizes)
    rows = -(-tot // 128)
    rows = -(-rows // 8) * 8

    def pack(dct):
        flat = jnp.concatenate([dct[n].reshape(-1) for n in small_names])
        return jnp.pad(flat, (0, rows * 128 - tot), constant_values=1.0).reshape(rows, 128)

    dl, nm, nv = adamw(pack(weights), pack(grads), pack(m_in), pack(v_in))
    off = 0
    for n, sz in zip(small_names, sizes):
        shp = weights[n].shape
        take = lambda a: a.reshape(-1)[off:off + sz].reshape(shp)
        deltas[n], new_m[n], new_v[n] = take(dl), take(nm), take(nv)
        off += sz

    return (loss, grad_x[None], *[grads[n] for n in names], *[deltas[n] for n in names],
            *[new_m[n] for n in names], *[new_v[n] for n in names])
```

```python
import functools
import math

import jax
import jax.numpy as jnp
import numpy as np
from jax import lax
from jax.experimental import pallas as pl
from jax.experimental.pallas import tpu as pltpu

F32 = jnp.float32
BF16 = jnp.bfloat16
EPS = 1e-6
MESH = pl.DeviceIdType.MESH

SSM_HEAD_DIM = 64
SSM_GROUPS = 4
SSM_STATE = 128
SSM_CONV = 4
SSM_CHUNK = 256
ATT_HEAD_DIM = 64
ATT_WINDOW = 128
REL_BUCKETS = 32
N_CHIPS = 4

ADAM_LR = 0.001
ADAM_B1 = 0.9
ADAM_B2 = 0.999
ADAM_EPS = 1e-08
ADAM_WD = 0.01
ADAM_STEP = 10

VMEM_LIMIT_BYTES = 56 * 1024 * 1024
NEG = -1e30


DEP_SPEC = pl.BlockSpec(memory_space=pl.ANY)


def _params(*sem):
    return pltpu.CompilerParams(dimension_semantics=sem if sem else None, vmem_limit_bytes=VMEM_LIMIT_BYTES)


def _dot(a, b):
    return jnp.dot(a, b, preferred_element_type=F32)


def _dot_nt(a, b):
    return lax.dot_general(a, b, (((1,), (1,)), ((), ())), preferred_element_type=F32)


def _dot_tn(a, b):
    return lax.dot_general(a, b, (((0,), (0,)), ((), ())), preferred_element_type=F32)


def _b(x):
    return x.astype(BF16)


@jax.custom_vjp
def _bmm(a, b):
    return _dot(_b(a), _b(b))


def _bmm_fwd(a, b):
    return _bmm(a, b), (a, b)


def _bmm_bwd(res, g):
    a, b = res
    g16 = _b(g)
    return _dot_nt(g16, _b(b)).astype(a.dtype), _dot_tn(_b(a), g16).astype(b.dtype)


_bmm.defvjp(_bmm_fwd, _bmm_bwd)


@jax.custom_vjp
def _bmm_nt(a, b):
    return _dot_nt(_b(a), _b(b))


def _bmm_nt_fwd(a, b):
    return _bmm_nt(a, b), (a, b)


def _bmm_nt_bwd(res, g):
    a, b = res
    g16 = _b(g)
    return _dot(g16, _b(b)).astype(a.dtype), _dot_tn(g16, _b(a)).astype(b.dtype)


_bmm_nt.defvjp(_bmm_nt_fwd, _bmm_nt_bwd)


@jax.custom_vjp
def _bmm_tn(a, b):
    return _dot_tn(_b(a), _b(b))


def _bmm_tn_fwd(a, b):
    return _bmm_tn(a, b), (a, b)


def _bmm_tn_bwd(res, g):
    a, b = res
    g16 = _b(g)
    return _dot_nt(_b(b), g16).astype(a.dtype), _dot(_b(a), g16).astype(b.dtype)


_bmm_tn.defvjp(_bmm_tn_fwd, _bmm_tn_bwd)


def _split3(x):
    hi = _b(x)
    r = x - hi.astype(F32)
    mid = _b(r)
    lo = _b(r - mid.astype(F32))
    return hi, mid, lo


def _x_left_raw(m, x):
    hi, mid, lo = _split3(x)
    return _dot(m, hi) + _dot(m, mid) + _dot(m, lo)


def _x_left_t_raw(m, x):
    hi, mid, lo = _split3(x)
    return _dot_tn(m, hi) + _dot_tn(m, mid) + _dot_tn(m, lo)


def _x_right_raw(x, m):
    hi, mid, lo = _split3(x)
    return _dot(hi, m) + _dot(mid, m) + _dot(lo, m)


def _x_right_t_raw(x, m):
    hi, mid, lo = _split3(x)
    return _dot_nt(hi, m) + _dot_nt(mid, m) + _dot_nt(lo, m)


@jax.custom_vjp
def _xleft(m, x):
    return _x_left_raw(m, x)


_xleft.defvjp(lambda m, x: (_x_left_raw(m, x), m),
              lambda m, g: (jnp.zeros_like(m), _x_left_t_raw(m, g)))


@jax.custom_vjp
def _xright(x, m):
    return _x_right_raw(x, m)


_xright.defvjp(lambda x, m: (_x_right_raw(x, m), m),
               lambda m, g: (_x_right_t_raw(g, m), jnp.zeros_like(m)))


def _sigmoid(x):
    return 1.0 / (1.0 + jnp.exp(-x))


def _silu(x):
    return x * _sigmoid(x)


def _softplus(x):
    return jnp.maximum(x, 0.0) + jnp.log(1.0 + jnp.exp(-jnp.abs(x)))


def _rms(x):
    return x * lax.rsqrt(jnp.mean(x * x, axis=-1, keepdims=True) + EPS)


def _iota(shape, dim):
    return lax.broadcasted_iota(jnp.int32, shape, dim)


def _blockdiag64(n):
    return jnp.where(_iota((n, n), 0) // 64 == _iota((n, n), 1) // 64, 1.0, 0.0).astype(BF16)


def _group64_rms(x, seg_sum):
    ms = seg_sum(x * x) * (1.0 / 64.0)
    return x * lax.rsqrt(ms + EPS)


def _fold64(x):
    ax = x.ndim - 1
    w = x.shape[ax]
    lo = (_iota(x.shape, ax) % 128) < 64
    return x + jnp.where(lo, pltpu.roll(x, w - 64, ax), pltpu.roll(x, 64, ax))


def _tile(n, want):
    t = min(n, want)
    assert n % t == 0, (n, t)
    return t


def _lane_tile(n, cap=1536):
    if n <= cap:
        return n
    return max(w for w in range(128, cap + 1, 128) if n % w == 0)


def ffn_fwd(h, g, w1, w3, w2, blk, dep):
    t, d = h.shape
    nk, fs = w1.shape[0], w1.shape[-1]
    tm = _tile(t, 1024)

    def body(h_ref, g_ref, w1_ref, w3_ref, w2_ref, dep_ref, o_ref, a_ref, b_ref, u_scr, acc):
        k = pl.program_id(1)

        @pl.when(k == 0)
        def _():
            u_scr[...] = _b(_rms(h_ref[...]) * g_ref[...])
            acc[...] = jnp.zeros_like(acc)

        u = u_scr[...]
        a = _dot(u, w1_ref[...])
        b = _dot(u, w3_ref[...])
        a_ref[...] = _b(a)
        b_ref[...] = _b(b)
        acc[...] += _dot(_b(_silu(a) * b), w2_ref[...])

        @pl.when(k == nk - 1)
        def _():
            o_ref[...] = h_ref[...] + 0.5 * acc[...]

    wspec = lambda r, c: pl.BlockSpec((None, None, r, c), lambda i, k: (k, blk, 0, 0))
    return pl.pallas_call(
        body, name="ffn_fwd",
        grid=(t // tm, nk),
        in_specs=[pl.BlockSpec((tm, d), lambda i, k: (i, 0)), pl.BlockSpec((1, d), lambda i, k: (0, 0)),
                  wspec(d, fs), wspec(d, fs), wspec(fs, d), DEP_SPEC],
        out_specs=[pl.BlockSpec((tm, d), lambda i, k: (i, 0)),
                   pl.BlockSpec((None, tm, fs), lambda i, k: (k, i, 0)),
                   pl.BlockSpec((None, tm, fs), lambda i, k: (k, i, 0))],
        out_shape=[jax.ShapeDtypeStruct((t, d), F32), jax.ShapeDtypeStruct((nk, t, fs), BF16),
                   jax.ShapeDtypeStruct((nk, t, fs), BF16)],
        scratch_shapes=[pltpu.VMEM((tm, d), BF16), pltpu.VMEM((tm, d), F32)],
        compiler_params=_params("arbitrary", "arbitrary"),
    )(h, g, w1, w3, w2, dep)


def ffn_bwd(h, dy, g, a_s, b_s, w1, w3, w2, blk, dep):
    t, d = h.shape
    nk, fs = w1.shape[0], w1.shape[-1]
    tm = _tile(t, 512)

    def body(h_ref, dy_ref, g_ref, a_ref, b_ref, w1_ref, w3_ref, w2_ref, dep_ref,
             dh_ref, u_ref, da_ref, db_ref, s_ref, dg_ref, dyh_scr, du_acc, da0, db0, da1, db1):
        i, k = pl.program_id(0), pl.program_id(1)

        @pl.when(k == 0)
        def _():
            dyh_scr[...] = _b(0.5 * dy_ref[...])
            du_acc[...] = jnp.zeros_like(du_acc)

        @pl.when((k == 0) & (i == 0))
        def _():
            dg_ref[...] = jnp.zeros_like(dg_ref)

        def step(prev, cur):
            if prev is not None:
                du_acc[...] += _dot_nt(prev[0][...], w1_ref[...]) + _dot_nt(prev[1][...], w3_ref[...])
            if cur is not None:
                ds = _dot_nt(dyh_scr[...], w2_ref[...])
                a = a_ref[...].astype(F32)
                b = b_ref[...].astype(F32)
                sig = _sigmoid(a)
                sl = a * sig
                s_ref[...] = _b(sl * b)
                da = _b(ds * b * (sig * (1.0 + a * (1.0 - sig))))
                db = _b(ds * sl)
                da_ref[...] = da
                db_ref[...] = db
                cur[0][...] = da
                cur[1][...] = db

        even, odd = (da0, db0), (da1, db1)

        @pl.when(k == 0)
        def _():
            step(None, even)

        @pl.when((k > 0) & (k < nk) & (k % 2 == 1))
        def _():
            step(even, odd)

        @pl.when((k > 0) & (k < nk) & (k % 2 == 0))
        def _():
            step(odd, even)

        @pl.when(k == nk)
        def _():
            step(odd if nk % 2 == 0 else even, None)
            hh = h_ref[...]
            rstd = lax.rsqrt(jnp.mean(hh * hh, axis=-1, keepdims=True) + EPS)
            xh = hh * rstd
            gg = g_ref[...]
            u_ref[...] = _b(xh * gg)
            du = du_acc[...]
            dg_ref[...] += jnp.sum(du * xh, axis=0, keepdims=True)
            dxh = du * gg
            dh_ref[...] = dy_ref[...] + rstd * (dxh - xh * jnp.mean(dxh * xh, axis=-1, keepdims=True))

    cur = lambda k: jnp.minimum(k, nk - 1)
    prv = lambda k: jnp.maximum(k - 1, 0)
    wcur = lambda r, c: pl.BlockSpec((None, None, r, c), lambda i, k: (cur(k), blk, 0, 0))
    wprv = lambda r, c: pl.BlockSpec((None, None, r, c), lambda i, k: (prv(k), blk, 0, 0))
    tok = pl.BlockSpec((tm, d), lambda i, k: (i, 0))
    hid = pl.BlockSpec((None, tm, fs), lambda i, k: (cur(k), i, 0))
    return pl.pallas_call(
        body, name="ffn_bwd",
        grid=(t // tm, nk + 1),
        in_specs=[tok, tok, pl.BlockSpec((1, d), lambda i, k: (0, 0)), hid, hid, wprv(d, fs), wprv(d, fs), wcur(fs, d),
                  DEP_SPEC],
        out_specs=[tok, tok, hid, hid, hid, pl.BlockSpec((1, d), lambda i, k: (0, 0))],
        out_shape=[jax.ShapeDtypeStruct((t, d), F32), jax.ShapeDtypeStruct((t, d), BF16),
                   jax.ShapeDtypeStruct((nk, t, fs), BF16), jax.ShapeDtypeStruct((nk, t, fs), BF16),
                   jax.ShapeDtypeStruct((nk, t, fs), BF16), jax.ShapeDtypeStruct((1, d), F32)],
        scratch_shapes=[pltpu.VMEM((tm, d), BF16), pltpu.VMEM((tm, d), F32)] + [pltpu.VMEM((tm, fs), BF16)] * 4,
        compiler_params=_params("arbitrary", "arbitrary"),
    )(h, dy, g, a_s, b_s, w1, w3, w2, dep)


def wgrad_grouped_b(a, bs, dep, scale=1.0):
    t, m = a.shape
    ng, _, n = bs.shape
    tk = _tile(t, 2048)

    def body(a_ref, b_ref, dep_ref, o_ref):
        j = pl.program_id(1)

        @pl.when(j == 0)
        def _():
            o_ref[...] = jnp.zeros_like(o_ref)

        o_ref[...] += _dot_tn(_b(a_ref[...]), _b(b_ref[...]))

        if scale != 1.0:
            @pl.when(j == pl.num_programs(1) - 1)
            def _():
                o_ref[...] = o_ref[...] * scale

    return pl.pallas_call(
        body, name="wgrad_gb",
        grid=(ng, t // tk),
        in_specs=[pl.BlockSpec((tk, m), lambda k, j: (j, 0)), pl.BlockSpec((None, tk, n), lambda k, j: (k, j, 0)),
                  DEP_SPEC],
        out_specs=pl.BlockSpec((None, m, n), lambda k, j: (k, 0, 0)),
        out_shape=jax.ShapeDtypeStruct((ng, m, n), F32),
        compiler_params=_params("arbitrary", "arbitrary"),
    )(a, bs, dep)


def wgrad_grouped_a(as_, b, dep, scale=1.0):
    ng, t, m = as_.shape
    n = b.shape[1]
    tk = _tile(t, 2048)

    def body(a_ref, b_ref, dep_ref, o_ref):
        j = pl.program_id(1)

        @pl.when(j == 0)
        def _():
            o_ref[...] = jnp.zeros_like(o_ref)

        o_ref[...] += _dot_tn(_b(a_ref[...]), _b(b_ref[...]))

        if scale != 1.0:
            @pl.when(j == pl.num_programs(1) - 1)
            def _():
                o_ref[...] = o_ref[...] * scale

    return pl.pallas_call(
        body, name="wgrad_ga",
        grid=(ng, t // tk),
        in_specs=[pl.BlockSpec((None, tk, m), lambda k, j: (k, j, 0)), pl.BlockSpec((tk, n), lambda k, j: (j, 0)),
                  DEP_SPEC],
        out_specs=pl.BlockSpec((None, m, n), lambda k, j: (k, 0, 0)),
        out_shape=jax.ShapeDtypeStruct((ng, m, n), F32),
        compiler_params=_params("arbitrary", "arbitrary"),
    )(as_, b, dep)


def wgrad(a, b):
    t, m = a.shape
    n = b.shape[1]
    tk = _tile(t, 2048 if m <= 1024 else 1024)
    tn = _lane_tile(n, 1536 if m <= 1024 else 512)

    def body(a_ref, b_ref, o_ref):
        @pl.when(pl.program_id(1) == 0)
        def _():
            o_ref[...] = jnp.zeros_like(o_ref)

        o_ref[...] += _dot_tn(_b(a_ref[...]), _b(b_ref[...]))

    return pl.pallas_call(
        body, name="wgrad",
        grid=(n // tn, t // tk),
        in_specs=[pl.BlockSpec((tk, m), lambda c, j: (j, 0)), pl.BlockSpec((tk, tn), lambda c, j: (j, c))],
        out_specs=pl.BlockSpec((m, tn), lambda c, j: (0, c)),
        out_shape=jax.ShapeDtypeStruct((m, n), F32),
        compiler_params=_params("arbitrary", "arbitrary"),
    )(a, b)


def norm_mm(h, g, w):
    t, d = h.shape
    n = w.shape[1]
    tm = _tile(t, 1024)
    tn = _lane_tile(n)

    def body(h_ref, g_ref, w_ref, o_ref, u_scr):
        @pl.when(pl.program_id(1) == 0)
        def _():
            u_scr[...] = _b(_rms(h_ref[...]) * g_ref[...])

        o_ref[...] = _dot(u_scr[...], w_ref[...])

    return pl.pallas_call(
        body, name="norm_mm",
        grid=(t // tm, n // tn),
        in_specs=[pl.BlockSpec((tm, d), lambda i, j: (i, 0)), pl.BlockSpec((1, d), lambda i, j: (0, 0)),
                  pl.BlockSpec((d, tn), lambda i, j: (0, j))],
        out_specs=pl.BlockSpec((tm, tn), lambda i, j: (i, j)),
        out_shape=jax.ShapeDtypeStruct((t, n), F32),
        scratch_shapes=[pltpu.VMEM((tm, d), BF16)],
        compiler_params=_params("arbitrary", "arbitrary"),
    )(h, g, w)


def norm_mm_bwd(h, g, w, dout, dres, dep, scale=1.0):
    t, d = h.shape
    n = w.shape[1]
    tm = _tile(t, 512)
    tn = _lane_tile(n)
    nj = n // tn

    def body(h_ref, g_ref, w_ref, do_ref, dr_ref, dep_ref, dh_ref, u_ref, dg_ref, du_acc):
        i, j = pl.program_id(0), pl.program_id(1)

        @pl.when(j == 0)
        def _():
            du_acc[...] = jnp.zeros_like(du_acc)

        @pl.when((j == 0) & (i == 0))
        def _():
            dg_ref[...] = jnp.zeros_like(dg_ref)

        du_acc[...] += _dot_nt(_b(do_ref[...]), w_ref[...])

        @pl.when(j == nj - 1)
        def _():
            hh = h_ref[...]
            rstd = lax.rsqrt(jnp.mean(hh * hh, axis=-1, keepdims=True) + EPS)
            xh = hh * rstd
            gg = g_ref[...]
            u_ref[...] = _b(xh * gg)
            du = du_acc[...] * scale
            dg_ref[...] += jnp.sum(du * xh, axis=0, keepdims=True)
            dxh = du * gg
            dh_ref[...] = dr_ref[...] + rstd * (dxh - xh * jnp.mean(dxh * xh, axis=-1, keepdims=True))

    tok = pl.BlockSpec((tm, d), lambda i, j: (i, 0))
    return pl.pallas_call(
        body, name="norm_mm_bwd",
        grid=(t // tm, nj),
        in_specs=[tok, pl.BlockSpec((1, d), lambda i, j: (0, 0)), pl.BlockSpec((d, tn), lambda i, j: (0, j)),
                  pl.BlockSpec((tm, tn), lambda i, j: (i, j)), tok, DEP_SPEC],
        out_specs=[tok, tok, pl.BlockSpec((1, d), lambda i, j: (0, 0))],
        out_shape=[jax.ShapeDtypeStruct((t, d), F32), jax.ShapeDtypeStruct((t, d), BF16),
                   jax.ShapeDtypeStruct((1, d), F32)],
        scratch_shapes=[pltpu.VMEM((tm, d), F32)],
        compiler_params=_params("arbitrary", "arbitrary"),
    )(h, g, w, dout, dres, dep)


CONV_COLS = 512


CONV_ROWS = 64


def _conv_pre(ext, w, b, r0, n):
    return (b + w[0:1] * ext[pl.ds(5 + r0, n), :] + w[1:2] * ext[pl.ds(6 + r0, n), :]
            + w[2:3] * ext[pl.ds(7 + r0, n), :] + w[3:4] * ext[pl.ds(8 + r0, n), :])


def conv_fwd(zx, cw, cb, col0):
    t = zx.shape[0]
    c = cw.shape[1]
    tm = _tile(t, 512)
    cb0 = col0 // CONV_COLS

    rc = _tile(tm, CONV_ROWS)

    def body(x_ref, w_ref, b_ref, o_ref, ext):
        @pl.when(pl.program_id(1) == 0)
        def _():
            ext[0:8, :] = jnp.zeros((8, CONV_COLS), F32)

        ext[8:, :] = x_ref[...]
        w, b = w_ref[...], b_ref[...]
        for r0 in range(0, tm, rc):
            o_ref[r0:r0 + rc, :] = _silu(_conv_pre(ext, w, b, r0, rc))
        ext[0:8, :] = ext[tm:tm + 8, :]

    return pl.pallas_call(
        body, name="conv_fwd",
        grid=(c // CONV_COLS, t // tm),
        in_specs=[pl.BlockSpec((tm, CONV_COLS), lambda j, i: (i, cb0 + j)),
                  pl.BlockSpec((8, CONV_COLS), lambda j, i: (0, j)), pl.BlockSpec((1, CONV_COLS), lambda j, i: (0, j))],
        out_specs=pl.BlockSpec((tm, CONV_COLS), lambda j, i: (i, j)),
        out_shape=jax.ShapeDtypeStruct((t, c), F32),
        scratch_shapes=[pltpu.VMEM((tm + 8, CONV_COLS), F32)],
        compiler_params=_params("arbitrary", "arbitrary"),
    )(zx, cw, cb)


def conv_bwd(dzx, zx, dxs, dbm, dcm, cw, cb, col0):
    t = zx.shape[0]
    c = cw.shape[1]
    tm = _tile(t, 512)
    nt = t // tm
    cb0 = col0 // CONV_COLS
    nxs = dxs.shape[1] // CONV_COLS
    hb = tm // 8

    rc = _tile(tm, CONV_ROWS)

    def body(dzx_ref, x_ref, xh_ref, dxs_ref, db_ref, dc_ref, w_ref, b_ref, o_ref, dw_ref, dbias_ref, ext, gy):
        j, i = pl.program_id(0), pl.program_id(1)
        ri = nt - 1 - i

        @pl.when(i == 0)
        def _():
            gy[tm:tm + 8, :] = jnp.zeros((8, CONV_COLS), F32)
            dw_ref[...] = jnp.zeros_like(dw_ref)
            dbias_ref[...] = jnp.zeros_like(dbias_ref)

        ext[0:8, :] = jnp.where(ri > 0, xh_ref[...], 0.0)
        ext[8:, :] = x_ref[...]
        w, b = w_ref[...], b_ref[...]
        dw = [jnp.zeros((1, CONV_COLS), F32) for _ in range(SSM_CONV)]
        dbias = jnp.zeros((1, CONV_COLS), F32)
        for r0 in range(0, tm, rc):
            rows = pl.ds(r0, rc)
            win = [ext[pl.ds(5 + tap + r0, rc), :] for tap in range(SSM_CONV)]
            y = b + w[0:1] * win[0] + w[1:2] * win[1] + w[2:3] * win[2] + w[3:4] * win[3]
            sig = _sigmoid(y)
            dout = jnp.where(j < nxs, dxs_ref[rows, :], jnp.where(j == nxs, db_ref[rows, :], dc_ref[rows, :]))
            g = dout * (sig * (1.0 + y * (1.0 - sig)))
            gy[rows, :] = g
            dbias = dbias + jnp.sum(g, axis=0, keepdims=True)
            for tap in range(SSM_CONV):
                dw[tap] = dw[tap] + jnp.sum(g * win[tap], axis=0, keepdims=True)
        for r0 in range(0, tm, rc):
            o_ref[r0:r0 + rc, :] = _b(w[0:1] * gy[pl.ds(r0 + 3, rc), :] + w[1:2] * gy[pl.ds(r0 + 2, rc), :]
                                      + w[2:3] * gy[pl.ds(r0 + 1, rc), :] + w[3:4] * gy[pl.ds(r0, rc), :])
        gy[tm:tm + 8, :] = gy[0:8, :]
        for tap in range(SSM_CONV):
            dw_ref[tap:tap + 1, :] += dw[tap]
        dbias_ref[...] += dbias

    return pl.pallas_call(
        body, name="conv_bwd",
        grid=(c // CONV_COLS, nt),
        in_specs=[pl.BlockSpec(memory_space=pl.ANY),
                  pl.BlockSpec((tm, CONV_COLS), lambda j, i: (nt - 1 - i, cb0 + j)),
                  pl.BlockSpec((8, CONV_COLS), lambda j, i: (jnp.maximum((nt - 1 - i) * hb - 1, 0), cb0 + j)),
                  pl.BlockSpec((tm, CONV_COLS), lambda j, i: (nt - 1 - i, jnp.minimum(j, nxs - 1))),
                  pl.BlockSpec((tm, CONV_COLS), lambda j, i: (nt - 1 - i, 0)),
                  pl.BlockSpec((tm, CONV_COLS), lambda j, i: (nt - 1 - i, 0)),
                  pl.BlockSpec((8, CONV_COLS), lambda j, i: (0, j)), pl.BlockSpec((1, CONV_COLS), lambda j, i: (0, j))],
        out_specs=[pl.BlockSpec((tm, CONV_COLS), lambda j, i: (nt - 1 - i, cb0 + j)),
                   pl.BlockSpec((8, CONV_COLS), lambda j, i: (0, j)), pl.BlockSpec((1, CONV_COLS), lambda j, i: (0, j))],
        out_shape=[jax.ShapeDtypeStruct(dzx.shape, dzx.dtype), jax.ShapeDtypeStruct((8, c), F32),
                   jax.ShapeDtypeStruct((1, c), F32)],
        scratch_shapes=[pltpu.VMEM((tm + 8, CONV_COLS), F32), pltpu.VMEM((tm + 8, CONV_COLS), F32)],
        input_output_aliases={0: 0},
        compiler_params=_params("arbitrary", "arbitrary"),
    )(dzx, zx, zx, dxs, dbm, dcm, cw, cb)


def _ssd_group(xs, bg, cg, dtraw, s0, bias, alog, dsk):
    L = xs.shape[0]
    causal = _iota((L, L), 0) >= _iota((L, L), 1)
    tril = jnp.where(causal, 1.0, 0.0).astype(BF16)
    dt = _softplus(dtraw + bias)
    a = -jnp.exp(alog)
    acum = _xleft(tril, dt * a)
    acum_t = acum.T
    dt_t = dt.T
    cb = _bmm_nt(cg, bg)
    lo = _iota((L, 128), 1) < 64
    lo_row = _iota((1, 128), 1) < 64
    lo_col = _iota((128, 1), 0) < 64
    alast = acum[L - 1:L, :]
    ys, s1s = [], []
    for q in range(4):
        xp = xs[:, q * 128:(q + 1) * 128]
        sp = s0[q * 128:(q + 1) * 128, :]
        yd, ec, wc, el = [], [], [], []
        for j in range(2):
            r = 2 * q + j
            ac = acum[:, r:r + 1]
            decay = jnp.exp(jnp.where(causal, ac - acum_t[r:r + 1, :], NEG))
            yd.append(_bmm(cb * decay * dt_t[r:r + 1, :], xp))
            ec.append(jnp.exp(ac))
            al = alast[:, r:r + 1]
            wc.append(jnp.exp(al - ac) * dt[:, r:r + 1])
            el.append(jnp.exp(al))
        y_off = _bmm_nt(cg, sp) * jnp.where(lo, ec[0], ec[1])
        dsel = jnp.where(lo_row, dsk[:, 2 * q:2 * q + 1], dsk[:, 2 * q + 1:2 * q + 2])
        ys.append(jnp.where(lo, yd[0], yd[1]) + y_off + dsel * xp)
        xw = xp * jnp.where(lo, wc[0], wc[1])
        s1s.append(sp * jnp.where(lo_col, el[0], el[1]) + _bmm_tn(xw, bg))
    return jnp.concatenate(ys, axis=1), jnp.concatenate(s1s, axis=0)


def ssd_fwd(xc, zx, bias, alog, dsk, dt_col0):
    t = xc.shape[0]
    L = _tile(t, SSM_CHUNK)
    nc = t // L
    g = SSM_GROUPS
    dtb = dt_col0 // 512

    def body(xs_ref, b_ref, c_ref, dt_ref, bias_ref, alog_ref, dsk_ref, y_ref, st_ref, state):
        @pl.when(pl.program_id(0) == 0)
        def _():
            state[...] = jnp.zeros_like(state)

        for gi in range(g):
            lane = slice(gi * 128, (gi + 1) * 128)
            wide = slice(gi * 512, (gi + 1) * 512)
            s0 = state[gi]
            st_ref[gi] = s0
            y, s1 = _ssd_group(xs_ref[:, wide], b_ref[:, lane], c_ref[:, lane], dt_ref[:, lane], s0,
                               bias_ref[:, lane], alog_ref[:, lane], dsk_ref[:, lane])
            y_ref[:, wide] = y
            state[gi] = s1

    vec = pl.BlockSpec((1, 512), lambda c: (0, 0))
    return pl.pallas_call(
        body, name="ssd_fwd",
        grid=(nc,),
        in_specs=[pl.BlockSpec((L, 2048), lambda c: (c, 0)), pl.BlockSpec((L, 512), lambda c: (c, 4)),
                  pl.BlockSpec((L, 512), lambda c: (c, 5)), pl.BlockSpec((L, 512), lambda c: (c, dtb)), vec, vec, vec],
        out_specs=[pl.BlockSpec((L, 2048), lambda c: (c, 0)),
                   pl.BlockSpec((None, g, 512, 128), lambda c: (c, 0, 0, 0))],
        out_shape=[jax.ShapeDtypeStruct((t, 2048), F32), jax.ShapeDtypeStruct((nc, g, 512, 128), F32)],
        scratch_shapes=[pltpu.VMEM((g, 512, 128), F32)],
        compiler_params=_params("arbitrary"),
    )(xc, xc, xc, zx, bias, alog, dsk)


def ssd_bwd(dzx, dy, xc, zx, states, bias, alog, dsk, dt_col0):
    t = xc.shape[0]
    L = _tile(t, SSM_CHUNK)
    nc = t // L
    g = SSM_GROUPS
    dtb = dt_col0 // 512

    def body(dzx_ref, dy_ref, xs_ref, b_ref, c_ref, dt_ref, st_ref, bias_ref, alog_ref, dsk_ref,
             ddt_ref, dxs_ref, db_ref, dc_ref, dbias_ref, dalog_ref, ddsk_ref, dstate):
        @pl.when(pl.program_id(0) == 0)
        def _():
            dstate[...] = jnp.zeros_like(dstate)
            dbias_ref[...] = jnp.zeros_like(dbias_ref)
            dalog_ref[...] = jnp.zeros_like(dalog_ref)
            ddsk_ref[...] = jnp.zeros_like(ddsk_ref)

        for gi in range(g):
            lane = slice(gi * 128, (gi + 1) * 128)
            wide = slice(gi * 512, (gi + 1) * 512)
            _, vjp = jax.vjp(_ssd_group, xs_ref[:, wide], b_ref[:, lane], c_ref[:, lane], dt_ref[:, lane], st_ref[gi],
                             bias_ref[:, lane], alog_ref[:, lane], dsk_ref[:, lane])
            dxs, db, dc, ddt, ds0, dbias, dalog, ddsk = vjp((dy_ref[:, wide], dstate[gi]))
            dxs_ref[:, wide] = dxs
            db_ref[:, lane] = db
            dc_ref[:, lane] = dc
            ddt_ref[:, lane] = _b(ddt)
            dstate[gi] = ds0
            dbias_ref[:, lane] += dbias
            dalog_ref[:, lane] += dalog
            ddsk_ref[:, lane] += ddsk

    rc = lambda c: nc - 1 - c
    vec = pl.BlockSpec((1, 512), lambda c: (0, 0))
    return pl.pallas_call(
        body, name="ssd_bwd",
        grid=(nc,),
        in_specs=[pl.BlockSpec(memory_space=pl.ANY),
                  pl.BlockSpec((L, 2048), lambda c: (rc(c), 0)), pl.BlockSpec((L, 2048), lambda c: (rc(c), 0)),
                  pl.BlockSpec((L, 512), lambda c: (rc(c), 4)), pl.BlockSpec((L, 512), lambda c: (rc(c), 5)),
                  pl.BlockSpec((L, 512), lambda c: (rc(c), dtb)),
                  pl.BlockSpec((None, g, 512, 128), lambda c: (rc(c), 0, 0, 0)), vec, vec, vec],
        out_specs=[pl.BlockSpec((L, 512), lambda c: (rc(c), dtb)), pl.BlockSpec((L, 2048), lambda c: (rc(c), 0)),
                   pl.BlockSpec((L, 512), lambda c: (rc(c), 0)), pl.BlockSpec((L, 512), lambda c: (rc(c), 0)),
                   vec, vec, vec],
        out_shape=[jax.ShapeDtypeStruct(dzx.shape, dzx.dtype), jax.ShapeDtypeStruct((t, 2048), F32),
                   jax.ShapeDtypeStruct((t, 512), F32), jax.ShapeDtypeStruct((t, 512), F32),
                   jax.ShapeDtypeStruct((1, 512), F32), jax.ShapeDtypeStruct((1, 512), F32),
                   jax.ShapeDtypeStruct((1, 512), F32)],
        scratch_shapes=[pltpu.VMEM((g, 512, 128), F32)],
        input_output_aliases={0: 0},
        compiler_params=_params("arbitrary"),
    )(dzx, dy, xc, xc, xc, zx, states, bias, alog, dsk)


def _gate_tile(y, z, gn):
    gated = y * _silu(z)
    parts = [_rms(gated[:, k * 512:(k + 1) * 512]) for k in range(SSM_GROUPS)]
    return jnp.concatenate(parts, axis=1) * gn


def gate_out_fwd(h, y, zx, gn, w_out):
    t, d = h.shape
    di = y.shape[1]
    tm = _tile(t, 256)

    def body(h_ref, y_ref, z_ref, gn_ref, w_ref, o_ref):
        yn = _gate_tile(y_ref[...], z_ref[...], gn_ref[...])
        o_ref[...] = h_ref[...] + _dot(_b(yn), w_ref[...])

    return pl.pallas_call(
        body, name="gate_out_fwd",
        grid=(t // tm,),
        in_specs=[pl.BlockSpec((tm, d), lambda i: (i, 0)), pl.BlockSpec((tm, di), lambda i: (i, 0)),
                  pl.BlockSpec((tm, di), lambda i: (i, 0)), pl.BlockSpec((1, di), lambda i: (0, 0)),
                  pl.BlockSpec((di, d), lambda i: (0, 0))],
        out_specs=pl.BlockSpec((tm, d), lambda i: (i, 0)),
        out_shape=jax.ShapeDtypeStruct((t, d), F32),
        compiler_params=_params("arbitrary"),
    )(h, y, zx, gn, w_out)


def gate_out_bwd(dy, y, zx, gn, w_out, n_zx, dep):
    t, d = dy.shape
    di = y.shape[1]
    tm = _tile(t, 256)

    def body(dy_ref, y_ref, z_ref, gn_ref, w_ref, dep_ref, dz_ref, dys_ref, yn_ref, dgn_ref):
        @pl.when(pl.program_id(0) == 0)
        def _():
            dgn_ref[...] = jnp.zeros_like(dgn_ref)

        yn, vjp = jax.vjp(_gate_tile, y_ref[...], z_ref[...], gn_ref[...])
        dyn = _dot_nt(_b(dy_ref[...]), w_ref[...])
        dys, dz, dgn = vjp(dyn)
        yn_ref[...] = _b(yn)
        dys_ref[...] = dys
        dz_ref[...] = _b(dz)
        dgn_ref[...] += dgn

    return pl.pallas_call(
        body, name="gate_out_bwd",
        grid=(t // tm,),
        in_specs=[pl.BlockSpec((tm, d), lambda i: (i, 0)), pl.BlockSpec((tm, di), lambda i: (i, 0)),
                  pl.BlockSpec((tm, di), lambda i: (i, 0)), pl.BlockSpec((1, di), lambda i: (0, 0)),
                  pl.BlockSpec((di, d), lambda i: (0, 0)), DEP_SPEC],
        out_specs=[pl.BlockSpec((tm, di), lambda i: (i, 0)), pl.BlockSpec((tm, di), lambda i: (i, 0)),
                   pl.BlockSpec((tm, di), lambda i: (i, 0)), pl.BlockSpec((1, di), lambda i: (0, 0))],
        out_shape=[jax.ShapeDtypeStruct((t, n_zx), BF16), jax.ShapeDtypeStruct((t, di), F32),
                   jax.ShapeDtypeStruct((t, di), BF16), jax.ShapeDtypeStruct((1, di), F32)],
        compiler_params=_params("arbitrary"),
    )(dy, y, zx, gn, w_out, dep)


def _attn_block(qp, kvp, kvc, biasm, sinks, qg, kg, w_o, first):
    nq = qp.shape[0]
    n_pairs = qp.shape[1] // 128
    hk = n_pairs
    rows = hk * nq
    seg = functools.partial(_xright, m=_blockdiag64(128))
    scale = ATT_HEAD_DIM ** -0.5
    qi = (_iota((rows, 2 * nq), 0) % nq) + nq
    kj = _iota((rows, 2 * nq), 1)
    dist = qi - kj
    valid = (dist >= 0) & (dist < ATT_WINDOW) & (jnp.logical_not(first) | (kj >= nq))
    lo = _iota((nq, 128), 1) < 64
    kv = jnp.concatenate([kvp, kvc], axis=0)
    outs = [None] * n_pairs
    for kvh in range(2):
        kn = _group64_rms(kv[:, kvh * 128:(kvh + 1) * 128], seg) * kg
        vv = kv[:, 256 + kvh * 128:256 + (kvh + 1) * 128]
        pairs = range(kvh * hk // 2, (kvh + 1) * hk // 2)
        qs, sk = [], []
        for p in pairs:
            qn = _group64_rms(qp[:, p * 128:(p + 1) * 128], seg) * qg
            qs += [jnp.where(lo, qn, 0.0), jnp.where(lo, 0.0, qn)]
            sk += [jnp.broadcast_to(sinks[:, h:h + 1], (nq, 1)) for h in (2 * p, 2 * p + 1)]
        sink = jnp.concatenate(sk, axis=0)
        s = _bmm_nt(jnp.concatenate(qs, axis=0), kn) * scale + biasm[kvh * rows:(kvh + 1) * rows]
        s = jnp.where(valid, s, NEG)
        m = lax.stop_gradient(jnp.maximum(jnp.max(s, axis=-1, keepdims=True), sink))
        pexp = jnp.exp(s - m)
        den = jnp.sum(pexp, axis=-1, keepdims=True) + jnp.exp(sink - m)
        o = _bmm(pexp * (1.0 / den), vv)
        for n, p in enumerate(pairs):
            outs[p] = jnp.where(lo, o[2 * n * nq:(2 * n + 1) * nq], o[(2 * n + 1) * nq:(2 * n + 2) * nq])
    o = jnp.concatenate(outs, axis=1)
    return _bmm(o, w_o), o


def attn_fwd(h, qp, kvd, biasm, sinks, qg, kg, w_o):
    t, d = h.shape
    nq = ATT_WINDOW
    nb = t // nq
    nh = qp.shape[1] // ATT_HEAD_DIM

    def body(h_ref, q_ref, kp_ref, kc_ref, bias_ref, s_ref, qg_ref, kg_ref, w_ref, o_ref):
        out, _ = _attn_block(q_ref[...], kp_ref[...], kc_ref[...], bias_ref[...], s_ref[...], qg_ref[...],
                             kg_ref[...], w_ref[...], pl.program_id(0) == 0)
        o_ref[...] = h_ref[...] + out

    vec = pl.BlockSpec((1, 128), lambda i: (0, 0))
    return pl.pallas_call(
        body, name="attn_fwd",
        grid=(nb,),
        in_specs=[pl.BlockSpec((nq, d), lambda i: (i, 0)), pl.BlockSpec((nq, nh * 64), lambda i: (i, 0)),
                  pl.BlockSpec((nq, 512), lambda i: (jnp.maximum(i - 1, 0), 0)),
                  pl.BlockSpec((nq, 512), lambda i: (i, 0)),
                  pl.BlockSpec((nh * nq, 2 * nq), lambda i: (0, 0)), vec, vec, vec,
                  pl.BlockSpec((nh * 64, d), lambda i: (0, 0))],
        out_specs=pl.BlockSpec((nq, d), lambda i: (i, 0)),
        out_shape=jax.ShapeDtypeStruct((t, d), F32),
        compiler_params=_params("arbitrary"),
    )(h, qp, kvd, kvd, biasm, sinks, qg, kg, w_o)


def attn_bwd(dy, qp, kvd, biasm, sinks, qg, kg, w_o, dep):
    t, d = dy.shape
    nq = ATT_WINDOW
    nb = t // nq
    nh = qp.shape[1] // ATT_HEAD_DIM

    def body(dy_ref, q_ref, kp_ref, kc_ref, bias_ref, s_ref, qg_ref, kg_ref, w_ref, dep_ref,
             dq_ref, dkv_ref, o_ref, dbias_ref, ds_ref, dqg_ref, dkg_ref, carry):
        i = pl.program_id(0)

        @pl.when(i == 0)
        def _():
            carry[...] = jnp.zeros_like(carry)
            dbias_ref[...] = jnp.zeros_like(dbias_ref)
            ds_ref[...] = jnp.zeros_like(ds_ref)
            dqg_ref[...] = jnp.zeros_like(dqg_ref)
            dkg_ref[...] = jnp.zeros_like(dkg_ref)

        @pl.when(i < nb)
        def _():
            fn = functools.partial(_attn_block, w_o=w_ref[...], first=(i == 0))
            (_, o), vjp = jax.vjp(fn, q_ref[...], kp_ref[...], kc_ref[...], bias_ref[...], s_ref[...],
                                  qg_ref[...], kg_ref[...])
            dq, dkp, dkc, dbias, dsk, dqg, dkg = vjp((dy_ref[...], jnp.zeros((nq, nh * 64), F32)))
            dq_ref[...] = _b(dq)
            o_ref[...] = _b(o)
            dkv_ref[...] = _b(_fold64(carry[...] + dkp))
            carry[...] = dkc
            dbias_ref[...] += dbias
            ds_ref[...] += dsk
            dqg_ref[...] += _fold64(dqg)
            dkg_ref[...] += _fold64(dkg)

        @pl.when(i == nb)
        def _():
            dkv_ref[...] = _b(_fold64(carry[...]))

    cl = lambda i: jnp.minimum(i, nb - 1)
    vec = pl.BlockSpec((1, 128), lambda i: (0, 0))
    return pl.pallas_call(
        body, name="attn_bwd",
        grid=(nb + 1,),
        in_specs=[pl.BlockSpec((nq, d), lambda i: (cl(i), 0)), pl.BlockSpec((nq, nh * 64), lambda i: (cl(i), 0)),
                  pl.BlockSpec((nq, 512), lambda i: (jnp.maximum(cl(i) - 1, 0), 0)),
                  pl.BlockSpec((nq, 512), lambda i: (cl(i), 0)),
                  pl.BlockSpec((nh * nq, 2 * nq), lambda i: (0, 0)), vec, vec, vec,
                  pl.BlockSpec((nh * 64, d), lambda i: (0, 0)), DEP_SPEC],
        out_specs=[pl.BlockSpec((nq, nh * 64), lambda i: (cl(i), 0)),
                   pl.BlockSpec((nq, 512), lambda i: (jnp.maximum(i - 1, 0), 0)),
                   pl.BlockSpec((nq, nh * 64), lambda i: (cl(i), 0)),
                   pl.BlockSpec((nh * nq, 2 * nq), lambda i: (0, 0)), vec, vec, vec],
        out_shape=[jax.ShapeDtypeStruct((t, nh * 64), BF16), jax.ShapeDtypeStruct((t, 512), BF16),
                   jax.ShapeDtypeStruct((t, nh * 64), BF16), jax.ShapeDtypeStruct((nh * nq, 2 * nq), F32),
                   jax.ShapeDtypeStruct((1, 128), F32), jax.ShapeDtypeStruct((1, 128), F32),
                   jax.ShapeDtypeStruct((1, 128), F32)],
        scratch_shapes=[pltpu.VMEM((nq, 512), F32)],
        compiler_params=_params("arbitrary"),
    )(dy, qp, kvd, kvd, biasm, sinks, qg, kg, w_o, dep)


def _t5_buckets():
    nq = ATT_WINDOW
    dist = (np.arange(nq)[:, None] + nq) - np.arange(2 * nq)[None, :]
    n = np.maximum(dist, 0)
    max_exact = REL_BUCKETS // 2
    nf = np.maximum(n, 1).astype(np.float32)
    large = max_exact + (np.log(nf / max_exact) / math.log(ATT_WINDOW / max_exact)
                         * (REL_BUCKETS - max_exact)).astype(np.int32)
    large = np.minimum(large, REL_BUCKETS - 1)
    return np.where(n < max_exact, n, large).astype(np.int32)


def rel_bias_bwd(dbias, buckets):
    nh = dbias.shape[0]

    def body(db_ref, bk_ref, o_ref):
        bk = bk_ref[...]
        lane = _iota((1, 128), 1)
        row = _iota((REL_BUCKETS, 128), 0)
        acc = jnp.zeros((REL_BUCKETS, 128), F32)
        for h in range(nh):
            dbh = db_ref[h]
            for b in range(REL_BUCKETS):
                v = jnp.sum(jnp.where(bk == b, dbh, 0.0))
                acc = acc + jnp.where((row == b) & (lane == h), v, 0.0)
        o_ref[...] = acc

    return pl.pallas_call(
        body, name="rel_bias_bwd",
        out_shape=jax.ShapeDtypeStruct((REL_BUCKETS, 128), F32),
        compiler_params=_params(),
    )(dbias, buckets)


def loss_head(y, target):
    t, d = y.shape
    tm = _tile(t, 512)

    def body(y_ref, t_ref, l_ref, dy_ref):
        @pl.when(pl.program_id(0) == 0)
        def _():
            l_ref[...] = jnp.zeros_like(l_ref)

        e = y_ref[...] - t_ref[...]
        l_ref[...] += 0.5 * jnp.sum(jnp.mean(e * e, axis=-1, keepdims=True), axis=0, keepdims=True)
        dy_ref[...] = e * (1.0 / d)

    return pl.pallas_call(
        body, name="loss_head",
        grid=(t // tm,),
        in_specs=[pl.BlockSpec((tm, d), lambda i: (i, 0)), pl.BlockSpec((tm, d), lambda i: (i, 0))],
        out_specs=[pl.BlockSpec((1, 1), lambda i: (0, 0)), pl.BlockSpec((tm, d), lambda i: (i, 0))],
        out_shape=[jax.ShapeDtypeStruct((1, 1), F32), jax.ShapeDtypeStruct((t, d), F32)],
        compiler_params=_params("arbitrary"),
    )(y, target)


def adamw(w, g, m, v):
    r, c = w.shape
    tr = r if r <= 512 else _tile(r, 256)

    def body(w_ref, g_ref, m_ref, v_ref, d_ref, nm_ref, nv_ref):
        gg = g_ref[...]
        nm = ADAM_B1 * m_ref[...] + (1.0 - ADAM_B1) * gg
        nv = ADAM_B2 * v_ref[...] + (1.0 - ADAM_B2) * (gg * gg)
        m_hat = nm / (1.0 - ADAM_B1 ** ADAM_STEP)
        v_hat = nv / (1.0 - ADAM_B2 ** ADAM_STEP)
        d_ref[...] = -ADAM_LR * (m_hat / (jnp.sqrt(v_hat) + ADAM_EPS) + ADAM_WD * w_ref[...])
        nm_ref[...] = nm
        nv_ref[...] = nv

    spec = pl.BlockSpec((tr, c), lambda i: (i, 0))
    shp = jax.ShapeDtypeStruct((r, c), F32)
    return pl.pallas_call(
        body, name="adamw",
        grid=(r // tr,),
        in_specs=[spec] * 4, out_specs=[spec] * 3, out_shape=[shp] * 3,
        compiler_params=_params("arbitrary"),
    )(w, g, m, v)


def _my_pos():
    return lax.axis_index("x"), lax.axis_index("y"), lax.axis_index("c")


def _other_chips(x, y):
    return [(1 - x, y), (x, 1 - y), (1 - x, 1 - y)]


def _chip_id(x, y):
    return 2 * x + y


HBM_SPEC = pl.BlockSpec(memory_space=pltpu.HBM)
SEM_SPEC = pl.BlockSpec(memory_space=pltpu.SEMAPHORE)
DATAFLOW = pltpu.SideEffectType.DATAFLOW_SIDE_EFFECTING


def _in_hbm(a):
    return pltpu.with_memory_space_constraint(a, pltpu.HBM)


def _ici_gather_copy(buf, p, j, chip, c, to, send_sems, recv_sems):
    blk = buf.at[_chip_id(*chip), c]
    return pltpu.make_async_remote_copy(
        src_ref=blk, dst_ref=blk, send_sem=send_sems.at[3 * p + j], recv_sem=recv_sems.at[3 * p + j],
        device_id=to, device_id_type=MESH)


def gather_start(bufs, after, tag):
    n = len(bufs)

    def body(*refs):
        ins = refs[:n]
        send_sems, recv_sems = refs[n + 1], refs[n + 2]
        token = refs[2 * n + 3]
        x, y, c = _my_pos()
        for p in range(n):
            for j, chip in enumerate(_other_chips(x, y)):
                _ici_gather_copy(ins[p], p, j, (x, y), c, (*chip, c), send_sems, recv_sems).start()
        token[...] = jnp.zeros_like(token)

    out = pl.pallas_call(
        body, name="gather_start_" + tag,
        in_specs=[HBM_SPEC] * n + [DEP_SPEC],
        out_specs=(SEM_SPEC, SEM_SPEC, *([HBM_SPEC] * n), pl.BlockSpec(memory_space=pltpu.VMEM)),
        out_shape=(pltpu.SemaphoreType.DMA((3 * n,)), pltpu.SemaphoreType.DMA((3 * n,)),
                   *[pltpu.HBM(b.shape, b.dtype) for b in bufs], jax.ShapeDtypeStruct((8, 128), F32)),
        input_output_aliases={p: 2 + p for p in range(n)},
        compiler_params=pltpu.CompilerParams(has_side_effects=DATAFLOW),
    )(*[_in_hbm(b) for b in bufs], after)
    return out[0], out[1], list(out[2:2 + n]), out[2 + n]


def gather_wait(send_sems, recv_sems, bufs, after, tag):
    n = len(bufs)

    def body(*refs):
        ins = refs[:n]
        send_sems, recv_sems = refs[n], refs[n + 1]
        x, y, c = _my_pos()
        for p in range(n):
            for j, chip in enumerate(_other_chips(x, y)):
                _ici_gather_copy(ins[p], p, j, (x, y), c, (*chip, c), send_sems, recv_sems).wait_send()
                _ici_gather_copy(ins[p], p, j, chip, c, (x, y, c), send_sems, recv_sems).wait_recv()

    out = pl.pallas_call(
        body, name="gather_wait_" + tag,
        in_specs=[HBM_SPEC] * n + [SEM_SPEC, SEM_SPEC, pl.BlockSpec(memory_space=pl.ANY)],
        out_specs=[HBM_SPEC] * n,
        out_shape=[pltpu.HBM(b.shape, b.dtype) for b in bufs],
        input_output_aliases={p: p for p in range(n)},
        compiler_params=pltpu.CompilerParams(has_side_effects=DATAFLOW),
    )(*bufs, send_sems, recv_sems, after)
    return list(out)


def forward_to_sibling(bufs):
    n = len(bufs)

    def body(*refs):
        outs = refs[n:2 * n]
        send_sems, recv_sems = refs[2 * n:]
        x, y, c = _my_pos()
        chips = _other_chips(x, y)
        sent = []
        for p in range(n):
            for j, chip in enumerate(chips):
                cp = _ici_gather_copy(outs[p], p, j, chip, c, (x, y, 1 - c), send_sems, recv_sems)
                cp.start()
                sent.append(cp)
        for p in range(n):
            for j, chip in enumerate(chips):
                _ici_gather_copy(outs[p], p, j, chip, 1 - c, (x, y, c), send_sems, recv_sems).wait_recv()
        for cp in sent:
            cp.wait_send()

    any_spec = pl.BlockSpec(memory_space=pl.ANY)
    return pl.pallas_call(
        body, name="forward_to_sibling",
        in_specs=[any_spec] * n, out_specs=[any_spec] * n,
        out_shape=[jax.ShapeDtypeStruct(b.shape, b.dtype) for b in bufs],
        scratch_shapes=[pltpu.SemaphoreType.DMA((3 * n,)), pltpu.SemaphoreType.DMA((3 * n,))],
        input_output_aliases={p: p for p in range(n)},
    )(*bufs)


def allreduce_small(v):
    r, c = v.shape

    def body(v_ref, o_ref, buf, send_sems, recv_sems):
        x, y, cc = _my_pos()
        me = 4 * x + 2 * y + cc
        buf[me] = v_ref[...]
        copies = []
        for k in range(1, 8):
            dx, dy, dc = (k >> 2) & 1, (k >> 1) & 1, k & 1
            peer = (x ^ dx, y ^ dy, cc ^ dc)
            cp = pltpu.make_async_remote_copy(
                src_ref=v_ref, dst_ref=buf.at[me], send_sem=send_sems.at[k - 1], recv_sem=recv_sems.at[k - 1],
                device_id=peer, device_id_type=MESH)
            cp.start()
            copies.append(cp)
        for cp in copies:
            cp.wait_recv()
        for cp in copies:
            cp.wait_send()
        acc = buf[0]
        for k in range(1, 8):
            acc = acc + buf[k]
        o_ref[...] = acc

    vm = pl.BlockSpec(memory_space=pltpu.VMEM)
    return pl.pallas_call(
        body, name="allreduce_small",
        in_specs=[vm], out_specs=vm,
        out_shape=jax.ShapeDtypeStruct((r, c), F32),
        scratch_shapes=[pltpu.VMEM((8, r, c), F32), pltpu.SemaphoreType.DMA((7,)), pltpu.SemaphoreType.DMA((7,))],
    )(v)


def _sibling_half_copy(grad, land, p, c, sibling, send_sems, recv_sems):
    return pltpu.make_async_remote_copy(
        src_ref=grad.at[:, 1 - c], dst_ref=land, send_sem=send_sems.at[p], recv_sem=recv_sems.at[p],
        device_id=sibling, device_id_type=MESH)


def sibling_halves_start(grads, tag):
    n = len(grads)
    lands = [lax.empty((g.shape[0],) + g.shape[2:], g.dtype) for g in grads]

    def body(*refs):
        ins, land = refs[:n], refs[n:2 * n]
        send_sems, recv_sems = refs[2 * n], refs[2 * n + 1]
        token = refs[4 * n + 2]
        x, y, c = _my_pos()
        for p in range(n):
            _sibling_half_copy(ins[p], land[p], p, c, (x, y, 1 - c), send_sems, recv_sems).start()
        token[...] = jnp.zeros_like(token)

    out = pl.pallas_call(
        body, name="sibling_halves_start_" + tag,
        in_specs=[HBM_SPEC] * (2 * n),
        out_specs=(SEM_SPEC, SEM_SPEC, *([HBM_SPEC] * (2 * n)), pl.BlockSpec(memory_space=pltpu.VMEM)),
        out_shape=(pltpu.SemaphoreType.DMA((n,)), pltpu.SemaphoreType.DMA((n,)),
                   *[pltpu.HBM(a.shape, a.dtype) for a in grads + lands], jax.ShapeDtypeStruct((8, 128), F32)),
        input_output_aliases={i: 2 + i for i in range(2 * n)},
        compiler_params=pltpu.CompilerParams(has_side_effects=DATAFLOW),
    )(*[_in_hbm(a) for a in grads + lands])
    return out[0], out[1], list(out[2:2 + n]), list(out[2 + n:2 + 2 * n]), out[2 + 2 * n]


def sibling_halves_wait(send_sems, recv_sems, grads, lands, after, tag):
    n = len(grads)

    def body(*refs):
        ins, land = refs[:n], refs[n:2 * n]
        send_sems, recv_sems = refs[2 * n], refs[2 * n + 1]
        x, y, c = _my_pos()
        for p in range(n):
            cp = _sibling_half_copy(ins[p], land[p], p, c, (x, y, 1 - c), send_sems, recv_sems)
            cp.wait_send()
            cp.wait_recv()

    out = pl.pallas_call(
        body, name="sibling_halves_wait_" + tag,
        in_specs=[HBM_SPEC] * (2 * n) + [SEM_SPEC, SEM_SPEC, pl.BlockSpec(memory_space=pl.ANY)],
        out_specs=[HBM_SPEC] * (2 * n),
        out_shape=[pltpu.HBM(a.shape, a.dtype) for a in grads + lands],
        input_output_aliases={i: i for i in range(2 * n)},
        compiler_params=pltpu.CompilerParams(has_side_effects=DATAFLOW),
    )(*grads, *lands, send_sems, recv_sems, after)
    return list(out[:n]), list(out[n:])


def _chip_partial_copy(part, land, p, j, chip, c, send_sems, recv_sems):
    return pltpu.make_async_remote_copy(
        src_ref=part.at[_chip_id(*chip)], dst_ref=land.at[j], send_sem=send_sems.at[3 * p + j],
        recv_sem=recv_sems.at[3 * p + j], device_id=(*chip, c), device_id_type=MESH)


def chip_partials_start(parts, tag):
    n = len(parts)
    lands = [lax.empty((N_CHIPS - 1,) + s.shape[1:], s.dtype) for s in parts]

    def body(*refs):
        ins, land = refs[:n], refs[n:2 * n]
        send_sems, recv_sems = refs[2 * n], refs[2 * n + 1]
        token = refs[4 * n + 2]
        x, y, c = _my_pos()
        for p in range(n):
            for j, chip in enumerate(_other_chips(x, y)):
                _chip_partial_copy(ins[p], land[p], p, j, chip, c, send_sems, recv_sems).start()
        token[...] = jnp.zeros_like(token)

    out = pl.pallas_call(
        body, name="chip_partials_start_" + tag,
        in_specs=[HBM_SPEC] * (2 * n),
        out_specs=(SEM_SPEC, SEM_SPEC, *([HBM_SPEC] * (2 * n)), pl.BlockSpec(memory_space=pltpu.VMEM)),
        out_shape=(pltpu.SemaphoreType.DMA((3 * n,)), pltpu.SemaphoreType.DMA((3 * n,)),
                   *[pltpu.HBM(a.shape, a.dtype) for a in parts + lands], jax.ShapeDtypeStruct((8, 128), F32)),
        input_output_aliases={i: 2 + i for i in range(2 * n)},
        compiler_params=pltpu.CompilerParams(has_side_effects=DATAFLOW),
    )(*[_in_hbm(a) for a in parts + lands])
    return out[0], out[1], list(out[2:2 + n]), list(out[2 + n:2 + 2 * n]), out[2 + 2 * n]


def chip_partials_wait(send_sems, recv_sems, parts, lands, after, tag):
    n = len(parts)

    def body(*refs):
        ins, land = refs[:n], refs[n:2 * n]
        send_sems, recv_sems = refs[2 * n], refs[2 * n + 1]
        x, y, c = _my_pos()
        for p in range(n):
            for j, chip in enumerate(_other_chips(x, y)):
                cp = _chip_partial_copy(ins[p], land[p], p, j, chip, c, send_sems, recv_sems)
                cp.wait_send()
                cp.wait_recv()

    out = pl.pallas_call(
        body, name="chip_partials_wait_" + tag,
        in_specs=[HBM_SPEC] * (2 * n) + [SEM_SPEC, SEM_SPEC, pl.BlockSpec(memory_space=pl.ANY)],
        out_specs=[HBM_SPEC] * (2 * n),
        out_shape=[pltpu.HBM(a.shape, a.dtype) for a in parts + lands],
        input_output_aliases={i: i for i in range(2 * n)},
        compiler_params=pltpu.CompilerParams(has_side_effects=DATAFLOW),
    )(*parts, *lands, send_sems, recv_sems, after)
    return list(out[n:])


def share_with_sibling(bufs):
    n = len(bufs)

    def body(*refs):
        outs = refs[n:2 * n]
        send_sems, recv_sems = refs[2 * n:]
        x, y, c = _my_pos()
        copies = []
        for p in range(n):
            cp = pltpu.make_async_remote_copy(
                src_ref=outs[p].at[c], dst_ref=outs[p].at[c], send_sem=send_sems.at[p], recv_sem=recv_sems.at[p],
                device_id=(x, y, 1 - c), device_id_type=MESH)
            cp.start()
            copies.append(cp)
        for p in range(n):
            pltpu.make_async_remote_copy(
                src_ref=outs[p].at[1 - c], dst_ref=outs[p].at[1 - c], send_sem=send_sems.at[p],
                recv_sem=recv_sems.at[p], device_id=(x, y, 1 - c), device_id_type=MESH).wait_recv()
        for cp in copies:
            cp.wait_send()

    any_spec = pl.BlockSpec(memory_space=pl.ANY)
    return pl.pallas_call(
        body, name="share_with_sibling",
        in_specs=[any_spec] * n, out_specs=[any_spec] * n,
        out_shape=[jax.ShapeDtypeStruct(b.shape, b.dtype) for b in bufs],
        scratch_shapes=[pltpu.SemaphoreType.DMA((n,)), pltpu.SemaphoreType.DMA((n,))],
        input_output_aliases={p: p for p in range(n)},
    )(*bufs)


def add_sibling(g, recv, half):
    _, _, r, c = g.shape
    tr = _tile(r, 256) if r % 256 == 0 else r

    def body(half_ref, g_ref, r_ref, o32_ref, o16_ref):
        s = g_ref[...] + r_ref[...]
        o32_ref[...] = s
        o16_ref[...] = _b(s)

    return pl.pallas_call(
        body, name="add_sibling",
        grid_spec=pltpu.PrefetchScalarGridSpec(
            num_scalar_prefetch=1, grid=(N_CHIPS, r // tr),
            in_specs=[pl.BlockSpec((None, None, tr, c), lambda k, i, hf: (k, hf[0], i, 0)),
                      pl.BlockSpec((None, tr, c), lambda k, i, hf: (k, i, 0))],
            out_specs=[pl.BlockSpec((None, tr, c), lambda k, i, hf: (k, i, 0)),
                       pl.BlockSpec((None, tr, c), lambda k, i, hf: (k, i, 0))]),
        out_shape=[jax.ShapeDtypeStruct((N_CHIPS, r, c), F32), jax.ShapeDtypeStruct((N_CHIPS, r, c), BF16)],
        compiler_params=_params("arbitrary", "arbitrary"),
    )(half, g, recv)


def add_chip_partials(p32, recv, pos):
    _, r, c = p32.shape
    tr = _tile(r, 256) if r % 256 == 0 else r

    def body(pos_ref, p_ref, r_ref, o_ref):
        acc = p_ref[...]
        for j in range(N_CHIPS - 1):
            acc = acc + r_ref[j].astype(F32)
        o_ref[...] = acc

    return pl.pallas_call(
        body, name="add_chip_partials",
        grid_spec=pltpu.PrefetchScalarGridSpec(
            num_scalar_prefetch=1, grid=(r // tr,),
            in_specs=[pl.BlockSpec((None, tr, c), lambda i, ps: (ps[0], i, 0)),
                      pl.BlockSpec((N_CHIPS - 1, tr, c), lambda i, ps: (0, i, 0))],
            out_specs=pl.BlockSpec((None, tr, c), lambda i, ps: (ps[1], i, 0))),
        out_shape=jax.ShapeDtypeStruct((2, r, c), F32),
        compiler_params=_params("arbitrary"),
    )(pos, p32, recv)


def cast_into_gather(w, pos, dep, row0=0, nrows=None):
    c = w.shape[1]
    nrows = w.shape[0] if nrows is None else nrows
    r = nrows // 2
    common = math.gcd(r, row0) if row0 else r
    tr = max(w for w in range(16, min(common, 512) + 1, 16) if common % w == 0)
    nt = r // tr

    def body(pos_ref, w_ref, dep_ref, o_ref):
        o_ref[...] = _b(w_ref[...])

    return pl.pallas_call(
        body, name="cast_into_gather",
        grid_spec=pltpu.PrefetchScalarGridSpec(
            num_scalar_prefetch=1, grid=(2, nt),
            in_specs=[pl.BlockSpec((tr, c), lambda hf, i, ps: (row0 // tr + hf * nt + i, 0)), DEP_SPEC],
            out_specs=pl.BlockSpec((None, None, tr, c), lambda hf, i, ps: (ps[0], hf, i, 0))),
        out_shape=jax.ShapeDtypeStruct((N_CHIPS, 2, r, c), BF16),
        compiler_params=_params("arbitrary", "arbitrary"),
    )(pos, w, dep)


def build_bias(rel, buckets):
    nb, nh = rel.shape

    def body(rel_ref, bk_ref, o_ref):
        bk = bk_ref[...]
        for h in range(nh):
            acc = jnp.zeros(bk.shape, F32)
            for b in range(nb):
                acc = jnp.where(bk == b, rel_ref[b, h], acc)
            o_ref[h] = acc

    return pl.pallas_call(
        body, name="build_bias",
        in_specs=[pl.BlockSpec(memory_space=pltpu.SMEM), pl.BlockSpec(memory_space=pltpu.VMEM)],
        out_specs=pl.BlockSpec(memory_space=pltpu.VMEM),
        out_shape=jax.ShapeDtypeStruct((nh,) + buckets.shape, F32),
        compiler_params=_params(),
    )(rel, buckets)


SMALL_ROWS = 256


def kernel(x, ffn_norm, ffn_w1, ffn_w3, ffn_w2, ssm_norm, ssm_w_in, ssm_conv_w, ssm_conv_b, ssm_dt_bias, ssm_a_log, ssm_d, ssm_gate_norm, ssm_w_out, kv_norm, w_kv, k_norm, attn_norm, w_q, q_norm, sinks, w_o, rel_bias, loss_target, m_ffn_norm, m_ffn_w1, m_ffn_w3, m_ffn_w2, m_ssm_norm, m_ssm_w_in, m_ssm_conv_w, m_ssm_conv_b, m_ssm_dt_bias, m_ssm_a_log, m_ssm_d, m_ssm_gate_norm, m_ssm_w_out, m_kv_norm, m_w_kv, m_k_norm, m_attn_norm, m_w_q, m_q_norm, m_sinks, m_w_o, m_rel_bias, v_ffn_norm, v_ffn_w1, v_ffn_w3, v_ffn_w2, v_ssm_norm, v_ssm_w_in, v_ssm_conv_w, v_ssm_conv_b, v_ssm_dt_bias, v_ssm_a_log, v_ssm_d, v_ssm_gate_norm, v_ssm_w_out, v_kv_norm, v_w_kv, v_k_norm, v_attn_norm, v_w_q, v_q_norm, v_sinks, v_w_o, v_rel_bias):
    weights = dict(ffn_norm=ffn_norm, ffn_w1=ffn_w1, ffn_w3=ffn_w3, ffn_w2=ffn_w2, ssm_norm=ssm_norm,
                   ssm_w_in=ssm_w_in, ssm_conv_w=ssm_conv_w, ssm_conv_b=ssm_conv_b, ssm_dt_bias=ssm_dt_bias,
                   ssm_a_log=ssm_a_log, ssm_d=ssm_d, ssm_gate_norm=ssm_gate_norm, ssm_w_out=ssm_w_out,
                   kv_norm=kv_norm, w_kv=w_kv, k_norm=k_norm, attn_norm=attn_norm, w_q=w_q, q_norm=q_norm,
                   sinks=sinks, w_o=w_o, rel_bias=rel_bias)
    m_in = dict(ffn_norm=m_ffn_norm, ffn_w1=m_ffn_w1, ffn_w3=m_ffn_w3, ffn_w2=m_ffn_w2, ssm_norm=m_ssm_norm,
                ssm_w_in=m_ssm_w_in, ssm_conv_w=m_ssm_conv_w, ssm_conv_b=m_ssm_conv_b, ssm_dt_bias=m_ssm_dt_bias,
                ssm_a_log=m_ssm_a_log, ssm_d=m_ssm_d, ssm_gate_norm=m_ssm_gate_norm, ssm_w_out=m_ssm_w_out,
                kv_norm=m_kv_norm, w_kv=m_w_kv, k_norm=m_k_norm, attn_norm=m_attn_norm, w_q=m_w_q, q_norm=m_q_norm,
                sinks=m_sinks, w_o=m_w_o, rel_bias=m_rel_bias)
    v_in = dict(ffn_norm=v_ffn_norm, ffn_w1=v_ffn_w1, ffn_w3=v_ffn_w3, ffn_w2=v_ffn_w2, ssm_norm=v_ssm_norm,
                ssm_w_in=v_ssm_w_in, ssm_conv_w=v_ssm_conv_w, ssm_conv_b=v_ssm_conv_b, ssm_dt_bias=v_ssm_dt_bias,
                ssm_a_log=v_ssm_a_log, ssm_d=v_ssm_d, ssm_gate_norm=v_ssm_gate_norm, ssm_w_out=v_ssm_w_out,
                kv_norm=v_kv_norm, w_kv=v_w_kv, k_norm=v_k_norm, attn_norm=v_attn_norm, w_q=v_w_q, q_norm=v_q_norm,
                sinks=v_sinks, w_o=v_w_o, rel_bias=v_rel_bias)
    return _step(x[0], loss_target[0], weights, m_in, v_in)


BIG = ("ffn_w1", "ffn_w3", "ffn_w2", "ssm_w_in", "ssm_w_out", "w_kv", "w_q", "w_o")
SMALL = (("ffn_norm", True), ("ssm_norm", True), ("ssm_conv_w", True), ("ssm_conv_b", True),
         ("ssm_gate_norm", True), ("ssm_dt_bias", False), ("ssm_a_log", False), ("ssm_d", False),
         ("kv_norm", False), ("k_norm", False), ("attn_norm", False), ("q_norm", False), ("sinks", False),
         ("rel_bias", False))


FFN_W = BIG[:3]


def _small_layout(weights):
    off, table = 0, {}
    for name, sharded in SMALL:
        shape = weights[name].shape
        full = shape[:-1] + (shape[-1] * N_CHIPS,) if sharded else shape
        n = int(np.prod(full))
        table[name] = (off, full, sharded)
        off += n
    assert off <= SMALL_ROWS * 128
    return table


def _place_small(values, table, chip, scale_mask):
    flat = jnp.zeros((SMALL_ROWS * 128,), F32)
    for name, (off, full, sharded) in table.items():
        if not sharded:
            continue
        v = values[name].astype(F32)
        lead = int(np.prod(full[:-1]))
        w = v.shape[-1]
        blk = jnp.zeros((lead, full[-1]), F32)
        blk = lax.dynamic_update_slice(blk, v.reshape(lead, w) * scale_mask, (0, chip * w))
        flat = lax.dynamic_update_slice(flat, blk.reshape(-1), (off,))
    return flat.reshape(SMALL_ROWS, 128)


def _take_small(mat, table, name):
    off, full, _ = table[name]
    n = int(np.prod(full))
    return mat.reshape(-1)[off:off + n].reshape(full)


def _step(x, target, weights, m_in, v_in):
    t, d = x.shape
    xi, yi, ci = lax.axis_index("x"), lax.axis_index("y"), lax.axis_index("c")
    chip = 2 * xi + yi
    pos_arr = jnp.stack([chip, ci]).astype(jnp.int32)
    half_arr = jnp.reshape(ci, (1,)).astype(jnp.int32)

    fs = weights["ffn_w1"].shape[-1]
    ffn_rows = {"ffn_w1": d, "ffn_w3": d, "ffn_w2": fs}
    w2d = {n: weights[n].reshape(-1, weights[n].shape[-1]) for n in BIG}
    mamba_w = ("ssm_w_in", "ssm_w_out")
    late_w = ("w_kv", "w_q", "w_o")
    fs_, fr_, fbufs, tok_f = gather_start(
        [cast_into_gather(w2d[n], pos_arr, pos_arr, 0, ffn_rows[n]) for n in FFN_W], pos_arr, "first")
    ms, mr, mbufs, tok_m = gather_start([cast_into_gather(w2d[n], pos_arr, tok_f) for n in mamba_w], tok_f, "mamba")
    ls, lr, lbufs, tok_l = gather_start(
        [cast_into_gather(w2d[n], pos_arr, tok_f, ffn_rows[n], 3 * ffn_rows[n]) for n in FFN_W]
        + [cast_into_gather(w2d[n], pos_arr, tok_f) for n in late_w], tok_m, "late")
    first = forward_to_sibling(gather_wait(fs_, fr_, fbufs, tok_l, "first"))
    no_dep = jnp.zeros((8, 128), F32)
    table = _small_layout(weights)
    south = (ci == 0).astype(F32)
    small = allreduce_small(_place_small(weights, table, chip, south))
    sp = {n: _take_small(small, table, n) if sh else weights[n] for n, sh in SMALL}

    ffn_first = [first[0].reshape(N_CHIPS, 1, d, fs), first[1].reshape(N_CHIPS, 1, d, fs),
                 first[2].reshape(N_CHIPS, 1, fs, d)]
    ffn_g = sp["ffn_norm"]
    h0 = x
    h1, a00, b00 = ffn_fwd(h0, ffn_g[0, 0].reshape(1, d), *ffn_first, 0, no_dep)
    gathered = dict(zip(mamba_w, forward_to_sibling(gather_wait(ms, mr, mbufs, h1, "mamba"))))
    n_in = weights["ssm_w_in"].shape[-1] * N_CHIPS
    di = weights["ssm_w_out"].shape[1] * N_CHIPS
    nheads = di // SSM_HEAD_DIM
    conv_dim = n_in - di - nheads
    w_in_full = jnp.moveaxis(gathered["ssm_w_in"].reshape(N_CHIPS, d, n_in // N_CHIPS), 0, 1).reshape(d, n_in)
    hpg = nheads // SSM_GROUPS

    def spread_heads(v):
        lead = v.shape[:-1]
        v = v.reshape(lead + (SSM_GROUPS, hpg))
        v = jnp.pad(v, [(0, 0)] * len(lead) + [(0, 0), (0, 128 - hpg)])
        return v.reshape(lead + (SSM_GROUPS * 128,))

    def gather_heads(v):
        lead = v.shape[:-1]
        return v.reshape(lead + (SSM_GROUPS, 128))[..., :hpg].reshape(lead + (nheads,))

    dt_col0 = di + conv_dim
    n_zx = dt_col0 + SSM_GROUPS * 128
    w_in = jnp.concatenate([w_in_full[:, :dt_col0], spread_heads(w_in_full[:, dt_col0:])], axis=1)
    w_out = gathered["ssm_w_out"].reshape(di, d)
    nkv = weights["w_kv"].shape[1] // (2 * ATT_HEAD_DIM)
    assert nkv == 2
    nh = weights["w_q"].shape[-1] // ATT_HEAD_DIM

    ssm_g = sp["ssm_norm"].reshape(1, d)
    cw = jnp.pad(sp["ssm_conv_w"].reshape(SSM_CONV, conv_dim), [(0, 8 - SSM_CONV), (0, 0)])
    cb = sp["ssm_conv_b"].reshape(1, conv_dim)
    gate_g = sp["ssm_gate_norm"].reshape(1, di)
    dt_bias = spread_heads(sp["ssm_dt_bias"].reshape(1, nheads))
    a_log = spread_heads(sp["ssm_a_log"].reshape(1, nheads))
    d_skip = spread_heads(sp["ssm_d"].reshape(1, nheads))
    kv_g = sp["kv_norm"].reshape(1, d)
    k_g = jnp.tile(sp["k_norm"].reshape(1, ATT_HEAD_DIM), (1, 2))
    attn_g = sp["attn_norm"].reshape(1, d)
    q_g = jnp.tile(sp["q_norm"].reshape(1, ATT_HEAD_DIM), (1, 2))
    sink_row = jnp.pad(sp["sinks"].reshape(1, nh), [(0, 0), (0, 128 - nh)])
    buckets = jnp.asarray(_t5_buckets())
    biasm = build_bias(sp["rel_bias"], buckets).reshape(nh * ATT_WINDOW, 2 * ATT_WINDOW)

    zx = norm_mm(h1, ssm_g, w_in)
    xc = conv_fwd(zx, cw, cb, di)
    y_ssd, states = ssd_fwd(xc, zx, dt_bias, a_log, d_skip, dt_col0)
    h2 = gate_out_fwd(h1, y_ssd, zx, gate_g, w_out)

    late = forward_to_sibling(gather_wait(ls, lr, lbufs, h2, "late"))
    ffn_rest = [late[0].reshape(N_CHIPS, 3, d, fs), late[1].reshape(N_CHIPS, 3, d, fs),
                late[2].reshape(N_CHIPS, 3, fs, d)]
    gathered.update(zip(late_w, late[3:]))
    wkv_heads = gathered["w_kv"].reshape(d, 2 * nkv, 1, ATT_HEAD_DIM)
    w_kvd = jnp.broadcast_to(wkv_heads, (d, 2 * nkv, 2, ATT_HEAD_DIM)).reshape(d, 4 * nkv * ATT_HEAD_DIM)
    wq = gathered["w_q"].reshape(d, -1)
    wo = gathered["w_o"].reshape(-1, d)

    def ffn_w(layer, idx):
        blk = 2 * layer + idx
        return (*ffn_first, 0) if blk == 0 else (*ffn_rest, blk - 1)

    h3, a01, b01 = ffn_fwd(h2, ffn_g[0, 1].reshape(1, d), *ffn_w(0, 1), no_dep)
    kvd = norm_mm(h3, kv_g, w_kvd)
    h4, a10, b10 = ffn_fwd(h3, ffn_g[1, 0].reshape(1, d), *ffn_w(1, 0), no_dep)
    qp = norm_mm(h4, attn_g, wq)
    h5 = attn_fwd(h4, qp, kvd, biasm, sink_row, q_g, k_g, wo)
    h6, a11, b11 = ffn_fwd(h5, ffn_g[1, 1].reshape(1, d), *ffn_w(1, 1), no_dep)
    loss_part, d6 = loss_head(h6, target)
    loss = lax.psum(loss_part[0, 0], ("x", "y", "c"))

    gfn = [[None, None], [None, None]]

    pending = {}

    def swap_start(pieces, tag):
        views = [g.reshape(N_CHIPS, 2, g.shape[1] // 2, g.shape[2]) for _, g in pieces]
        ss, rs, views, lands, token = sibling_halves_start(views, tag)
        pending[tag] = dict(keys=[k for k, _ in pieces], swap=(ss, rs, views, lands))
        return token

    def partials_start(tag, after):
        views, recv1 = sibling_halves_wait(*pending[tag]["swap"], after, tag)
        p32, p16 = zip(*[add_sibling(g, r, half_arr) for g, r in zip(views, recv1)])
        ss, rs, parts, lands, token = chip_partials_start(list(p16), tag)
        pending[tag].update(p32=p32, partials=(ss, rs, parts, lands))
        return token

    def ffn_back(h_in, dy, a_s, b_s, layer, idx, dep, wdep):
        dh, u, da, db, s, dg = ffn_bwd(h_in, dy, ffn_g[layer, idx].reshape(1, d), a_s, b_s, *ffn_w(layer, idx), dep)
        gfn[layer][idx] = dg
        return dh, [(("ffn_w1", layer, idx), wgrad_grouped_b(u, da, wdep)),
                    (("ffn_w3", layer, idx), wgrad_grouped_b(u, db, no_dep)),
                    (("ffn_w2", layer, idx), wgrad_grouped_a(s, dy, no_dep, 0.5))]

    d5, pieces = ffn_back(h5, d6, a11, b11, 1, 1, no_dep, no_dep)
    tok = swap_start(pieces, "ffn11")
    dqp, dkvd, o16, dbiasm, dsinks, dqg, dkg = attn_bwd(d5, qp, kvd, biasm, sink_row, q_g, k_g, wo, tok)
    tok = partials_start("ffn11", dqp)
    g_wo = wgrad(o16, d5)
    d4, u_q, g_attn_norm = norm_mm_bwd(h4, attn_g, wq, dqp, d5, tok)
    g_wq = wgrad(u_q, dqp)
    d3a, pieces = ffn_back(h3, d4, a10, b10, 1, 0, no_dep, no_dep)
    pieces += [(("w_o",), g_wo.reshape(N_CHIPS, -1, d)), (("w_q",), g_wq.reshape(N_CHIPS, d // N_CHIPS, -1))]
    tok = swap_start(pieces, "ffn10")
    d3, u_kv, g_kv_norm = norm_mm_bwd(h3, kv_g, w_kvd, dkvd, d3a, tok, 0.5)
    tok = partials_start("ffn10", d3)
    g_wkvd = wgrad(u_kv, dkvd)
    g_wkv = g_wkvd.reshape(d, 2 * nkv, 2, ATT_HEAD_DIM)[:, :, 0, :].reshape(d, 2 * nkv * ATT_HEAD_DIM)
    d2, pieces = ffn_back(h2, d3, a01, b01, 0, 1, tok, no_dep)
    pieces += [(("w_kv",), g_wkv.reshape(N_CHIPS, d // N_CHIPS, -1))]
    tok = swap_start(pieces, "ffn01")
    dzx, dy_ssd, yn16, g_gate = gate_out_bwd(d2, y_ssd, zx, gate_g, w_out, n_zx, tok)
    tok = partials_start("ffn01", dy_ssd)
    g_wout = wgrad(yn16, d2)
    dzx, dxs, dbm, dcm, g_dtb, g_alog, g_dsk = ssd_bwd(dzx, dy_ssd, xc, zx, states, dt_bias, a_log, d_skip, dt_col0)
    dzx, g_cw, g_cb = conv_bwd(dzx, zx, dxs, dbm, dcm, cw, cb, di)
    d1, u_in, g_ssm_norm = norm_mm_bwd(h1, ssm_g, w_in, dzx, d2, tok)
    g_win = wgrad(u_in, dzx)
    g_win_full = jnp.concatenate([g_win[:, :dt_col0], gather_heads(g_win[:, dt_col0:])], axis=1)
    pieces = [(("ssm_w_in",), jnp.moveaxis(g_win_full.reshape(d, N_CHIPS, n_in // N_CHIPS), 1, 0)),
              (("ssm_w_out",), g_wout.reshape(N_CHIPS, di // N_CHIPS, d))]
    tok = swap_start(pieces, "mamba")
    grad_x, u0, da0, db0, s0, gfn[0][0] = ffn_bwd(h0, d1, ffn_g[0, 0].reshape(1, d), a00, b00, *ffn_w(0, 0), tok)
    tok = partials_start("mamba", grad_x)
    g1 = wgrad_grouped_b(u0, da0, tok)
    tok = swap_start([(("ffn_w1", 0, 0), g1)], "ffn00a")
    g3 = wgrad_grouped_b(u0, db0, tok)
    tok = partials_start("ffn00a", g3) + swap_start([(("ffn_w3", 0, 0), g3)], "ffn00b")
    g2 = wgrad_grouped_a(s0, d1, tok, 0.5)
    tok = partials_start("ffn00b", g2) + swap_start([(("ffn_w2", 0, 0), g2)], "ffn00")
    g_relb = rel_bias_bwd(dbiasm.reshape(nh, ATT_WINDOW, 2 * ATT_WINDOW), buckets)

    reduced = {}

    def finish(tag, after):
        st = pending[tag]
        lands = chip_partials_wait(*st["partials"], after, tag)
        for k, p, r in zip(st["keys"], st["p32"], lands):
            reduced[k] = add_chip_partials(p, r, pos_arr)
        return reduced[st["keys"][-1]]

    last = finish("ffn10", finish("ffn11", tok))
    tok = partials_start("ffn00", last)
    last = finish("ffn00b", finish("ffn00a", finish("mamba", finish("ffn01", tok))))
    finish("ffn00", last)
    keys = list(reduced)
    shared = dict(zip(keys, share_with_sibling([reduced[k] for k in keys])))
    grads = {}
    for n in FFN_W:
        blocks = [shared[(n, l, i)].reshape(1, ffn_rows[n], -1) for l in range(2) for i in range(2)]
        grads[n] = jnp.concatenate(blocks, axis=0).reshape(weights[n].shape)
    for n in BIG[3:]:
        grads[n] = shared[(n,)].reshape(weights[n].shape)

    small_grads = {
        "ffn_norm": jnp.stack([jnp.stack([gfn[l][i].reshape(d) for i in range(2)]) for l in range(2)]),
        "ssm_norm": g_ssm_norm.reshape(1, d),
        "ssm_conv_w": g_cw[:SSM_CONV].reshape(1, SSM_CONV, conv_dim),
        "ssm_conv_b": g_cb.reshape(1, conv_dim),
        "ssm_gate_norm": g_gate.reshape(1, di),
        "ssm_dt_bias": gather_heads(g_dtb.reshape(1, -1)), "ssm_a_log": gather_heads(g_alog.reshape(1, -1)),
        "ssm_d": gather_heads(g_dsk.reshape(1, -1)),
        "kv_norm": g_kv_norm.reshape(d), "k_norm": dkg[0, :ATT_HEAD_DIM], "attn_norm": g_attn_norm.reshape(1, d),
        "q_norm": dqg[:, :ATT_HEAD_DIM], "sinks": dsinks[:, :nh], "rel_bias": g_relb[:, :nh],
    }
    flat = jnp.zeros((SMALL_ROWS * 128,), F32)
    for name, (off, fshape, _) in table.items():
        flat = lax.dynamic_update_slice(flat, small_grads[name].astype(F32).reshape(-1), (off,))
    small_sum = allreduce_small(flat.reshape(SMALL_ROWS, 128))
    for name, (off, fshape, sharded) in table.items():
        g = _take_small(small_sum, table, name)
        if sharded:
            w = weights[name].shape[-1]
            lead = int(np.prod(fshape[:-1]))
            g = lax.dynamic_slice(g.reshape(lead, fshape[-1]), (0, chip * w), (lead, w)).reshape(weights[name].shape)
        grads[name] = g.reshape(weights[name].shape)

    names = list(weights)
    deltas, new_m, new_v = {}, {}, {}
    small_names = [n for n, _ in SMALL]
    for n in BIG:
        shp = weights[n].shape
        v2 = lambda a: a.reshape(-1, shp[-1])
        dl, nm, nv = adamw(v2(weights[n]), v2(grads[n]), v2(m_in[n]), v2(v_in[n]))
        deltas[n], new_m[n], new_v[n] = dl.reshape(shp), nm.reshape(shp), nv.reshape(shp)
    sizes = [int(np.prod(weights[n].shape)) for n in small_names]
    tot = sum(sizes)
    rows = -(-tot // 128)
    rows = -(-rows // 8) * 8

    def pack(dct):
        flat = jnp.concatenate([dct[n].reshape(-1) for n in small_names])
        return jnp.pad(flat, (0, rows * 128 - tot), constant_values=1.0).reshape(rows, 128)

    dl, nm, nv = adamw(pack(weights), pack(grads), pack(m_in), pack(v_in))
    off = 0
    for n, sz in zip(small_names, sizes):
        shp = weights[n].shape
        take = lambda a: a.reshape(-1)[off:off + sz].reshape(shp)
        deltas[n], new_m[n], new_v[n] = take(dl), take(nm), take(nv)
        off += sz

    return (loss, grad_x[None], *[grads[n] for n in names], *[deltas[n] for n in names],
            *[new_m[n] for n in names], *[new_v[n] for n in names])
```

```python
import functools
import math

import jax
import jax.numpy as jnp
import numpy as np
from jax import lax
from jax.experimental import pallas as pl
from jax.experimental.pallas import tpu as pltpu

F32 = jnp.float32
BF16 = jnp.bfloat16
EPS = 1e-6
MESH = pl.DeviceIdType.MESH

SSM_HEAD_DIM = 64
SSM_GROUPS = 4
SSM_STATE = 128
SSM_CONV = 4
SSM_CHUNK = 256
ATT_HEAD_DIM = 64
ATT_WINDOW = 128
REL_BUCKETS = 32
N_CHIPS = 4

ADAM_LR = 0.001
ADAM_B1 = 0.9
ADAM_B2 = 0.999
ADAM_EPS = 1e-08
ADAM_WD = 0.01
ADAM_STEP = 10

VMEM_LIMIT_BYTES = 56 * 1024 * 1024
NEG = -1e30


DEP_SPEC = pl.BlockSpec(memory_space=pl.ANY)


def _params(*sem):
    return pltpu.CompilerParams(dimension_semantics=sem if sem else None, vmem_limit_bytes=VMEM_LIMIT_BYTES)


def _dot(a, b):
    return jnp.dot(a, b, preferred_element_type=F32)


def _dot_nt(a, b):
    return lax.dot_general(a, b, (((1,), (1,)), ((), ())), preferred_element_type=F32)


def _dot_tn(a, b):
    return lax.dot_general(a, b, (((0,), (0,)), ((), ())), preferred_element_type=F32)


def _b(x):
    return x.astype(BF16)


@jax.custom_vjp
def _bmm(a, b):
    return _dot(_b(a), _b(b))


def _bmm_fwd(a, b):
    return _bmm(a, b), (a, b)


def _bmm_bwd(res, g):
    a, b = res
    g16 = _b(g)
    return _dot_nt(g16, _b(b)).astype(a.dtype), _dot_tn(_b(a), g16).astype(b.dtype)


_bmm.defvjp(_bmm_fwd, _bmm_bwd)


@jax.custom_vjp
def _bmm_nt(a, b):
    return _dot_nt(_b(a), _b(b))


def _bmm_nt_fwd(a, b):
    return _bmm_nt(a, b), (a, b)


def _bmm_nt_bwd(res, g):
    a, b = res
    g16 = _b(g)
    return _dot(g16, _b(b)).astype(a.dtype), _dot_tn(g16, _b(a)).astype(b.dtype)


_bmm_nt.defvjp(_bmm_nt_fwd, _bmm_nt_bwd)


@jax.custom_vjp
def _bmm_tn(a, b):
    return _dot_tn(_b(a), _b(b))


def _bmm_tn_fwd(a, b):
    return _bmm_tn(a, b), (a, b)


def _bmm_tn_bwd(res, g):
    a, b = res
    g16 = _b(g)
    return _dot_nt(_b(b), g16).astype(a.dtype), _dot(_b(a), g16).astype(b.dtype)


_bmm_tn.defvjp(_bmm_tn_fwd, _bmm_tn_bwd)


def _split3(x):
    hi = _b(x)
    r = x - hi.astype(F32)
    mid = _b(r)
    lo = _b(r - mid.astype(F32))
    return hi, mid, lo


def _x_left_raw(m, x):
    hi, mid, lo = _split3(x)
    return _dot(m, hi) + _dot(m, mid) + _dot(m, lo)


def _x_left_t_raw(m, x):
    hi, mid, lo = _split3(x)
    return _dot_tn(m, hi) + _dot_tn(m, mid) + _dot_tn(m, lo)


def _x_right_raw(x, m):
    hi, mid, lo = _split3(x)
    return _dot(hi, m) + _dot(mid, m) + _dot(lo, m)


def _x_right_t_raw(x, m):
    hi, mid, lo = _split3(x)
    return _dot_nt(hi, m) + _dot_nt(mid, m) + _dot_nt(lo, m)


@jax.custom_vjp
def _xleft(m, x):
    return _x_left_raw(m, x)


_xleft.defvjp(lambda m, x: (_x_left_raw(m, x), m),
              lambda m, g: (jnp.zeros_like(m), _x_left_t_raw(m, g)))


@jax.custom_vjp
def _xright(x, m):
    return _x_right_raw(x, m)


_xright.defvjp(lambda x, m: (_x_right_raw(x, m), m),
               lambda m, g: (_x_right_t_raw(g, m), jnp.zeros_like(m)))


def _sigmoid(x):
    return 1.0 / (1.0 + jnp.exp(-x))


def _silu(x):
    return x * _sigmoid(x)


def _softplus(x):
    return jnp.maximum(x, 0.0) + jnp.log(1.0 + jnp.exp(-jnp.abs(x)))


def _rms(x):
    return x * lax.rsqrt(jnp.mean(x * x, axis=-1, keepdims=True) + EPS)


def _iota(shape, dim):
    return lax.broadcasted_iota(jnp.int32, shape, dim)


def _blockdiag64(n):
    return jnp.where(_iota((n, n), 0) // 64 == _iota((n, n), 1) // 64, 1.0, 0.0).astype(BF16)


def _group64_rms(x, seg_sum):
    ms = seg_sum(x * x) * (1.0 / 64.0)
    return x * lax.rsqrt(ms + EPS)


def _fold64(x):
    ax = x.ndim - 1
    w = x.shape[ax]
    lo = (_iota(x.shape, ax) % 128) < 64
    return x + jnp.where(lo, pltpu.roll(x, w - 64, ax), pltpu.roll(x, 64, ax))


def _tile(n, want):
    t = min(n, want)
    assert n % t == 0, (n, t)
    return t


def _lane_tile(n, cap=1536):
    if n <= cap:
        return n
    return max(w for w in range(128, cap + 1, 128) if n % w == 0)


def ffn_fwd(h, g, w1, w3, w2, blk, dep):
    t, d = h.shape
    nk, fs = w1.shape[0], w1.shape[-1]
    tm = _tile(t, 1024)

    def body(h_ref, g_ref, w1_ref, w3_ref, w2_ref, dep_ref, o_ref, a_ref, b_ref, u_scr, acc):
        k = pl.program_id(1)

        @pl.when(k == 0)
        def _():
            u_scr[...] = _b(_rms(h_ref[...]) * g_ref[...])
            acc[...] = jnp.zeros_like(acc)

        u = u_scr[...]
        a = _dot(u, w1_ref[...])
        b = _dot(u, w3_ref[...])
        a_ref[...] = _b(a)
        b_ref[...] = _b(b)
        acc[...] += _dot(_b(_silu(a) * b), w2_ref[...])

        @pl.when(k == nk - 1)
        def _():
            o_ref[...] = h_ref[...] + 0.5 * acc[...]

    wspec = lambda r, c: pl.BlockSpec((None, None, r, c), lambda i, k: (k, blk, 0, 0))
    return pl.pallas_call(
        body, name="ffn_fwd",
        grid=(t // tm, nk),
        in_specs=[pl.BlockSpec((tm, d), lambda i, k: (i, 0)), pl.BlockSpec((1, d), lambda i, k: (0, 0)),
                  wspec(d, fs), wspec(d, fs), wspec(fs, d), DEP_SPEC],
        out_specs=[pl.BlockSpec((tm, d), lambda i, k: (i, 0)),
                   pl.BlockSpec((None, tm, fs), lambda i, k: (k, i, 0)),
                   pl.BlockSpec((None, tm, fs), lambda i, k: (k, i, 0))],
        out_shape=[jax.ShapeDtypeStruct((t, d), F32), jax.ShapeDtypeStruct((nk, t, fs), BF16),
                   jax.ShapeDtypeStruct((nk, t, fs), BF16)],
        scratch_shapes=[pltpu.VMEM((tm, d), BF16), pltpu.VMEM((tm, d), F32)],
        compiler_params=_params("arbitrary", "arbitrary"),
    )(h, g, w1, w3, w2, dep)


def ffn_bwd(h, dy, g, a_s, b_s, w1, w3, w2, blk, dep):
    t, d = h.shape
    nk, fs = w1.shape[0], w1.shape[-1]
    tm = _tile(t, 512)

    def body(h_ref, dy_ref, g_ref, a_ref, b_ref, w1_ref, w3_ref, w2_ref, dep_ref,
             dh_ref, u_ref, da_ref, db_ref, s_ref, dg_ref, dyh_scr, du_acc, da0, db0, da1, db1):
        i, k = pl.program_id(0), pl.program_id(1)

        @pl.when(k == 0)
        def _():
            dyh_scr[...] = _b(0.5 * dy_ref[...])
            du_acc[...] = jnp.zeros_like(du_acc)

        @pl.when((k == 0) & (i == 0))
        def _():
            dg_ref[...] = jnp.zeros_like(dg_ref)

        def step(prev, cur):
            if prev is not None:
                du_acc[...] += _dot_nt(prev[0][...], w1_ref[...]) + _dot_nt(prev[1][...], w3_ref[...])
            if cur is not None:
                ds = _dot_nt(dyh_scr[...], w2_ref[...])
                a = a_ref[...].astype(F32)
                b = b_ref[...].astype(F32)
                sig = _sigmoid(a)
                sl = a * sig
                s_ref[...] = _b(sl * b)
                da = _b(ds * b * (sig * (1.0 + a * (1.0 - sig))))
                db = _b(ds * sl)
                da_ref[...] = da
                db_ref[...] = db
                cur[0][...] = da
                cur[1][...] = db

        even, odd = (da0, db0), (da1, db1)

        @pl.when(k == 0)
        def _():
            step(None, even)

        @pl.when((k > 0) & (k < nk) & (k % 2 == 1))
        def _():
            step(even, odd)

        @pl.when((k > 0) & (k < nk) & (k % 2 == 0))
        def _():
            step(odd, even)

        @pl.when(k == nk)
        def _():
            step(odd if nk % 2 == 0 else even, None)
            hh = h_ref[...]
            rstd = lax.rsqrt(jnp.mean(hh * hh, axis=-1, keepdims=True) + EPS)
            xh = hh * rstd
            gg = g_ref[...]
            u_ref[...] = _b(xh * gg)
            du = du_acc[...]
            dg_ref[...] += jnp.sum(du * xh, axis=0, keepdims=True)
            dxh = du * gg
            dh_ref[...] = dy_ref[...] + rstd * (dxh - xh * jnp.mean(dxh * xh, axis=-1, keepdims=True))

    cur = lambda k: jnp.minimum(k, nk - 1)
    prv = lambda k: jnp.maximum(k - 1, 0)
    wcur = lambda r, c: pl.BlockSpec((None, None, r, c), lambda i, k: (cur(k), blk, 0, 0))
    wprv = lambda r, c: pl.BlockSpec((None, None, r, c), lambda i, k: (prv(k), blk, 0, 0))
    tok = pl.BlockSpec((tm, d), lambda i, k: (i, 0))
    hid = pl.BlockSpec((None, tm, fs), lambda i, k: (cur(k), i, 0))
    return pl.pallas_call(
        body, name="ffn_bwd",
        grid=(t // tm, nk + 1),
        in_specs=[tok, tok, pl.BlockSpec((1, d), lambda i, k: (0, 0)), hid, hid, wprv(d, fs), wprv(d, fs), wcur(fs, d),
                  DEP_SPEC],
        out_specs=[tok, tok, hid, hid, hid, pl.BlockSpec((1, d), lambda i, k: (0, 0))],
        out_shape=[jax.ShapeDtypeStruct((t, d), F32), jax.ShapeDtypeStruct((t, d), BF16),
                   jax.ShapeDtypeStruct((nk, t, fs), BF16), jax.ShapeDtypeStruct((nk, t, fs), BF16),
                   jax.ShapeDtypeStruct((nk, t, fs), BF16), jax.ShapeDtypeStruct((1, d), F32)],
        scratch_shapes=[pltpu.VMEM((tm, d), BF16), pltpu.VMEM((tm, d), F32)] + [pltpu.VMEM((tm, fs), BF16)] * 4,
        compiler_params=_params("arbitrary", "arbitrary"),
    )(h, dy, g, a_s, b_s, w1, w3, w2, dep)


def wgrad_grouped_b(a, bs, dep, scale=1.0):
    t, m = a.shape
    ng, _, n = bs.shape
    tk = _tile(t, 2048)

    def body(a_ref, b_ref, dep_ref, o_ref):
        j = pl.program_id(1)

        @pl.when(j == 0)
        def _():
            o_ref[...] = jnp.zeros_like(o_ref)

        o_ref[...] += _dot_tn(_b(a_ref[...]), _b(b_ref[...]))

        if scale != 1.0:
            @pl.when(j == pl.num_programs(1) - 1)
            def _():
                o_ref[...] = o_ref[...] * scale

    return pl.pallas_call(
        body, name="wgrad_gb",
        grid=(ng, t // tk),
        in_specs=[pl.BlockSpec((tk, m), lambda k, j: (j, 0)), pl.BlockSpec((None, tk, n), lambda k, j: (k, j, 0)),
                  DEP_SPEC],
        out_specs=pl.BlockSpec((None, m, n), lambda k, j: (k, 0, 0)),
        out_shape=jax.ShapeDtypeStruct((ng, m, n), F32),
        compiler_params=_params("arbitrary", "arbitrary"),
    )(a, bs, dep)


def wgrad_grouped_a(as_, b, dep, scale=1.0):
    ng, t, m = as_.shape
    n = b.shape[1]
    tk = _tile(t, 2048)

    def body(a_ref, b_ref, dep_ref, o_ref):
        j = pl.program_id(1)

        @pl.when(j == 0)
        def _():
            o_ref[...] = jnp.zeros_like(o_ref)

        o_ref[...] += _dot_tn(_b(a_ref[...]), _b(b_ref[...]))

        if scale != 1.0:
            @pl.when(j == pl.num_programs(1) - 1)
            def _():
                o_ref[...] = o_ref[...] * scale

    return pl.pallas_call(
        body, name="wgrad_ga",
        grid=(ng, t // tk),
        in_specs=[pl.BlockSpec((None, tk, m), lambda k, j: (k, j, 0)), pl.BlockSpec((tk, n), lambda k, j: (j, 0)),
                  DEP_SPEC],
        out_specs=pl.BlockSpec((None, m, n), lambda k, j: (k, 0, 0)),
        out_shape=jax.ShapeDtypeStruct((ng, m, n), F32),
        compiler_params=_params("arbitrary", "arbitrary"),
    )(as_, b, dep)


def wgrad(a, b):
    t, m = a.shape
    n = b.shape[1]
    tk = _tile(t, 2048 if m <= 1024 else 1024)
    tn = _lane_tile(n, 1536 if m <= 1024 else 512)

    def body(a_ref, b_ref, o_ref):
        @pl.when(pl.program_id(1) == 0)
        def _():
            o_ref[...] = jnp.zeros_like(o_ref)

        o_ref[...] += _dot_tn(_b(a_ref[...]), _b(b_ref[...]))

    return pl.pallas_call(
        body, name="wgrad",
        grid=(n // tn, t // tk),
        in_specs=[pl.BlockSpec((tk, m), lambda c, j: (j, 0)), pl.BlockSpec((tk, tn), lambda c, j: (j, c))],
        out_specs=pl.BlockSpec((m, tn), lambda c, j: (0, c)),
        out_shape=jax.ShapeDtypeStruct((m, n), F32),
        compiler_params=_params("arbitrary", "arbitrary"),
    )(a, b)


def norm_mm(h, g, w):
    t, d = h.shape
    n = w.shape[1]
    tm = _tile(t, 1024)
    tn = _lane_tile(n)

    def body(h_ref, g_ref, w_ref, o_ref, u_scr):
        @pl.when(pl.program_id(1) == 0)
        def _():
            u_scr[...] = _b(_rms(h_ref[...]) * g_ref[...])

        o_ref[...] = _dot(u_scr[...], w_ref[...])

    return pl.pallas_call(
        body, name="norm_mm",
        grid=(t // tm, n // tn),
        in_specs=[pl.BlockSpec((tm, d), lambda i, j: (i, 0)), pl.BlockSpec((1, d), lambda i, j: (0, 0)),
                  pl.BlockSpec((d, tn), lambda i, j: (0, j))],
        out_specs=pl.BlockSpec((tm, tn), lambda i, j: (i, j)),
        out_shape=jax.ShapeDtypeStruct((t, n), F32),
        scratch_shapes=[pltpu.VMEM((tm, d), BF16)],
        compiler_params=_params("arbitrary", "arbitrary"),
    )(h, g, w)


def norm_mm_bwd(h, g, w, dout, dres, dep, scale=1.0):
    t, d = h.shape
    n = w.shape[1]
    tm = _tile(t, 512)
    tn = _lane_tile(n)
    nj = n // tn

    def body(h_ref, g_ref, w_ref, do_ref, dr_ref, dep_ref, dh_ref, u_ref, dg_ref, du_acc):
        i, j = pl.program_id(0), pl.program_id(1)

        @pl.when(j == 0)
        def _():
            du_acc[...] = jnp.zeros_like(du_acc)

        @pl.when((j == 0) & (i == 0))
        def _():
            dg_ref[...] = jnp.zeros_like(dg_ref)

        du_acc[...] += _dot_nt(_b(do_ref[...]), w_ref[...])

        @pl.when(j == nj - 1)
        def _():
            hh = h_ref[...]
            rstd = lax.rsqrt(jnp.mean(hh * hh, axis=-1, keepdims=True) + EPS)
            xh = hh * rstd
            gg = g_ref[...]
            u_ref[...] = _b(xh * gg)
            du = du_acc[...] * scale
            dg_ref[...] += jnp.sum(du * xh, axis=0, keepdims=True)
            dxh = du * gg
            dh_ref[...] = dr_ref[...] + rstd * (dxh - xh * jnp.mean(dxh * xh, axis=-1, keepdims=True))

    tok = pl.BlockSpec((tm, d), lambda i, j: (i, 0))
    return pl.pallas_call(
        body, name="norm_mm_bwd",
        grid=(t // tm, nj),
        in_specs=[tok, pl.BlockSpec((1, d), lambda i, j: (0, 0)), pl.BlockSpec((d, tn), lambda i, j: (0, j)),
                  pl.BlockSpec((tm, tn), lambda i, j: (i, j)), tok, DEP_SPEC],
        out_specs=[tok, tok, pl.BlockSpec((1, d), lambda i, j: (0, 0))],
        out_shape=[jax.ShapeDtypeStruct((t, d), F32), jax.ShapeDtypeStruct((t, d), BF16),
                   jax.ShapeDtypeStruct((1, d), F32)],
        scratch_shapes=[pltpu.VMEM((tm, d), F32)],
        compiler_params=_params("arbitrary", "arbitrary"),
    )(h, g, w, dout, dres, dep)


CONV_COLS = 512


CONV_ROWS = 64


def _conv_pre(ext, w, b, r0, n):
    return (b + w[0:1] * ext[pl.ds(5 + r0, n), :] + w[1:2] * ext[pl.ds(6 + r0, n), :]
            + w[2:3] * ext[pl.ds(7 + r0, n), :] + w[3:4] * ext[pl.ds(8 + r0, n), :])


def conv_fwd(zx, cw, cb, col0):
    t = zx.shape[0]
    c = cw.shape[1]
    tm = _tile(t, 1024)
    cb0 = col0 // CONV_COLS

    rc = _tile(tm, CONV_ROWS)

    def body(x_ref, w_ref, b_ref, o_ref, ext):
        @pl.when(pl.program_id(1) == 0)
        def _():
            ext[0:8, :] = jnp.zeros((8, CONV_COLS), F32)

        ext[8:, :] = x_ref[...]
        w, b = w_ref[...], b_ref[...]
        for r0 in range(0, tm, rc):
            o_ref[r0:r0 + rc, :] = _silu(_conv_pre(ext, w, b, r0, rc))
        ext[0:8, :] = ext[tm:tm + 8, :]

    return pl.pallas_call(
        body, name="conv_fwd",
        grid=(c // CONV_COLS, t // tm),
        in_specs=[pl.BlockSpec((tm, CONV_COLS), lambda j, i: (i, cb0 + j)),
                  pl.BlockSpec((8, CONV_COLS), lambda j, i: (0, j)), pl.BlockSpec((1, CONV_COLS), lambda j, i: (0, j))],
        out_specs=pl.BlockSpec((tm, CONV_COLS), lambda j, i: (i, j)),
        out_shape=jax.ShapeDtypeStruct((t, c), F32),
        scratch_shapes=[pltpu.VMEM((tm + 8, CONV_COLS), F32)],
        compiler_params=_params("arbitrary", "arbitrary"),
    )(zx, cw, cb)


def conv_bwd(dzx, zx, dxs, dbm, dcm, cw, cb, col0):
    t = zx.shape[0]
    c = cw.shape[1]
    tm = _tile(t, 1024)
    nt = t // tm
    cb0 = col0 // CONV_COLS
    nxs = dxs.shape[1] // CONV_COLS
    hb = tm // 8

    rc = _tile(tm, CONV_ROWS)

    def body(dzx_ref, x_ref, xh_ref, dxs_ref, db_ref, dc_ref, w_ref, b_ref, o_ref, dw_ref, dbias_ref, ext, gy):
        j, i = pl.program_id(0), pl.program_id(1)
        ri = nt - 1 - i

        @pl.when(i == 0)
        def _():
            gy[tm:tm + 8, :] = jnp.zeros((8, CONV_COLS), F32)
            dw_ref[...] = jnp.zeros_like(dw_ref)
            dbias_ref[...] = jnp.zeros_like(dbias_ref)

        ext[0:8, :] = jnp.where(ri > 0, xh_ref[...], 0.0)
        ext[8:, :] = x_ref[...]
        w, b = w_ref[...], b_ref[...]
        dw = [jnp.zeros((1, CONV_COLS), F32) for _ in range(SSM_CONV)]
        dbias = jnp.zeros((1, CONV_COLS), F32)
        for r0 in range(0, tm, rc):
            rows = pl.ds(r0, rc)
            win = [ext[pl.ds(5 + tap + r0, rc), :] for tap in range(SSM_CONV)]
            y = b + w[0:1] * win[0] + w[1:2] * win[1] + w[2:3] * win[2] + w[3:4] * win[3]
            sig = _sigmoid(y)
            dout = jnp.where(j < nxs, dxs_ref[rows, :], jnp.where(j == nxs, db_ref[rows, :], dc_ref[rows, :]))
            g = dout * (sig * (1.0 + y * (1.0 - sig)))
            gy[rows, :] = g
            dbias = dbias + jnp.sum(g, axis=0, keepdims=True)
            for tap in range(SSM_CONV):
                dw[tap] = dw[tap] + jnp.sum(g * win[tap], axis=0, keepdims=True)
        for r0 in range(0, tm, rc):
            o_ref[r0:r0 + rc, :] = _b(w[0:1] * gy[pl.ds(r0 + 3, rc), :] + w[1:2] * gy[pl.ds(r0 + 2, rc), :]
                                      + w[2:3] * gy[pl.ds(r0 + 1, rc), :] + w[3:4] * gy[pl.ds(r0, rc), :])
        gy[tm:tm + 8, :] = gy[0:8, :]
        for tap in range(SSM_CONV):
            dw_ref[tap:tap + 1, :] += dw[tap]
        dbias_ref[...] += dbias

    return pl.pallas_call(
        body, name="conv_bwd",
        grid=(c // CONV_COLS, nt),
        in_specs=[pl.BlockSpec(memory_space=pl.ANY),
                  pl.BlockSpec((tm, CONV_COLS), lambda j, i: (nt - 1 - i, cb0 + j)),
                  pl.BlockSpec((8, CONV_COLS), lambda j, i: (jnp.maximum((nt - 1 - i) * hb - 1, 0), cb0 + j)),
                  pl.BlockSpec((tm, CONV_COLS), lambda j, i: (nt - 1 - i, jnp.minimum(j, nxs - 1))),
                  pl.BlockSpec((tm, CONV_COLS), lambda j, i: (nt - 1 - i, 0)),
                  pl.BlockSpec((tm, CONV_COLS), lambda j, i: (nt - 1 - i, 0)),
                  pl.BlockSpec((8, CONV_COLS), lambda j, i: (0, j)), pl.BlockSpec((1, CONV_COLS), lambda j, i: (0, j))],
        out_specs=[pl.BlockSpec((tm, CONV_COLS), lambda j, i: (nt - 1 - i, cb0 + j)),
                   pl.BlockSpec((8, CONV_COLS), lambda j, i: (0, j)), pl.BlockSpec((1, CONV_COLS), lambda j, i: (0, j))],
        out_shape=[jax.ShapeDtypeStruct(dzx.shape, dzx.dtype), jax.ShapeDtypeStruct((8, c), F32),
                   jax.ShapeDtypeStruct((1, c), F32)],
        scratch_shapes=[pltpu.VMEM((tm + 8, CONV_COLS), F32), pltpu.VMEM((tm + 8, CONV_COLS), F32)],
        input_output_aliases={0: 0},
        compiler_params=_params("arbitrary", "arbitrary"),
    )(dzx, zx, zx, dxs, dbm, dcm, cw, cb)


def _ssd_group(xs, bg, cg, dtraw, s0, bias, alog, dsk):
    L = xs.shape[0]
    causal = _iota((L, L), 0) >= _iota((L, L), 1)
    tril = jnp.where(causal, 1.0, 0.0).astype(BF16)
    dt = _softplus(dtraw + bias)
    a = -jnp.exp(alog)
    acum = _xleft(tril, dt * a)
    acum_t = acum.T
    dt_t = dt.T
    cb = _bmm_nt(cg, bg)
    lo = _iota((L, 128), 1) < 64
    lo_row = _iota((1, 128), 1) < 64
    lo_col = _iota((128, 1), 0) < 64
    alast = acum[L - 1:L, :]
    ys, s1s = [], []
    for q in range(4):
        xp = xs[:, q * 128:(q + 1) * 128]
        sp = s0[q * 128:(q + 1) * 128, :]
        yd, ec, wc, el = [], [], [], []
        for j in range(2):
            r = 2 * q + j
            ac = acum[:, r:r + 1]
            decay = jnp.exp(jnp.where(causal, ac - acum_t[r:r + 1, :], NEG))
            yd.append(_bmm(cb * decay * dt_t[r:r + 1, :], xp))
            ec.append(jnp.exp(ac))
            al = alast[:, r:r + 1]
            wc.append(jnp.exp(al - ac) * dt[:, r:r + 1])
            el.append(jnp.exp(al))
        y_off = _bmm_nt(cg, sp) * jnp.where(lo, ec[0], ec[1])
        dsel = jnp.where(lo_row, dsk[:, 2 * q:2 * q + 1], dsk[:, 2 * q + 1:2 * q + 2])
        ys.append(jnp.where(lo, yd[0], yd[1]) + y_off + dsel * xp)
        xw = xp * jnp.where(lo, wc[0], wc[1])
        s1s.append(sp * jnp.where(lo_col, el[0], el[1]) + _bmm_tn(xw, bg))
    return jnp.concatenate(ys, axis=1), jnp.concatenate(s1s, axis=0)


def ssd_fwd(xc, zx, bias, alog, dsk, dt_col0):
    t = xc.shape[0]
    L = _tile(t, SSM_CHUNK)
    nc = t // L
    g = SSM_GROUPS
    dtb = dt_col0 // 512

    def body(xs_ref, b_ref, c_ref, dt_ref, bias_ref, alog_ref, dsk_ref, y_ref, st_ref, state):
        @pl.when(pl.program_id(0) == 0)
        def _():
            state[...] = jnp.zeros_like(state)

        for gi in range(g):
            lane = slice(gi * 128, (gi + 1) * 128)
            wide = slice(gi * 512, (gi + 1) * 512)
            s0 = state[gi]
            st_ref[gi] = s0
            y, s1 = _ssd_group(xs_ref[:, wide], b_ref[:, lane], c_ref[:, lane], dt_ref[:, lane], s0,
                               bias_ref[:, lane], alog_ref[:, lane], dsk_ref[:, lane])
            y_ref[:, wide] = y
            state[gi] = s1

    vec = pl.BlockSpec((1, 512), lambda c: (0, 0))
    return pl.pallas_call(
        body, name="ssd_fwd",
        grid=(nc,),
        in_specs=[pl.BlockSpec((L, 2048), lambda c: (c, 0)), pl.BlockSpec((L, 512), lambda c: (c, 4)),
                  pl.BlockSpec((L, 512), lambda c: (c, 5)), pl.BlockSpec((L, 512), lambda c: (c, dtb)), vec, vec, vec],
        out_specs=[pl.BlockSpec((L, 2048), lambda c: (c, 0)),
                   pl.BlockSpec((None, g, 512, 128), lambda c: (c, 0, 0, 0))],
        out_shape=[jax.ShapeDtypeStruct((t, 2048), F32), jax.ShapeDtypeStruct((nc, g, 512, 128), F32)],
        scratch_shapes=[pltpu.VMEM((g, 512, 128), F32)],
        compiler_params=_params("arbitrary"),
    )(xc, xc, xc, zx, bias, alog, dsk)


def ssd_bwd(dzx, dy, xc, zx, states, bias, alog, dsk, dt_col0):
    t = xc.shape[0]
    L = _tile(t, SSM_CHUNK)
    nc = t // L
    g = SSM_GROUPS
    dtb = dt_col0 // 512

    def body(dzx_ref, dy_ref, xs_ref, b_ref, c_ref, dt_ref, st_ref, bias_ref, alog_ref, dsk_ref,
             ddt_ref, dxs_ref, db_ref, dc_ref, dbias_ref, dalog_ref, ddsk_ref, dstate):
        @pl.when(pl.program_id(0) == 0)
        def _():
            dstate[...] = jnp.zeros_like(dstate)
            dbias_ref[...] = jnp.zeros_like(dbias_ref)
            dalog_ref[...] = jnp.zeros_like(dalog_ref)
            ddsk_ref[...] = jnp.zeros_like(ddsk_ref)

        for gi in range(g):
            lane = slice(gi * 128, (gi + 1) * 128)
            wide = slice(gi * 512, (gi + 1) * 512)
            _, vjp = jax.vjp(_ssd_group, xs_ref[:, wide], b_ref[:, lane], c_ref[:, lane], dt_ref[:, lane], st_ref[gi],
                             bias_ref[:, lane], alog_ref[:, lane], dsk_ref[:, lane])
            dxs, db, dc, ddt, ds0, dbias, dalog, ddsk = vjp((dy_ref[:, wide], dstate[gi]))
            dxs_ref[:, wide] = dxs
            db_ref[:, lane] = db
            dc_ref[:, lane] = dc
            ddt_ref[:, lane] = _b(ddt)
            dstate[gi] = ds0
            dbias_ref[:, lane] += dbias
            dalog_ref[:, lane] += dalog
            ddsk_ref[:, lane] += ddsk

    rc = lambda c: nc - 1 - c
    vec = pl.BlockSpec((1, 512), lambda c: (0, 0))
    return pl.pallas_call(
        body, name="ssd_bwd",
        grid=(nc,),
        in_specs=[pl.BlockSpec(memory_space=pl.ANY),
                  pl.BlockSpec((L, 2048), lambda c: (rc(c), 0)), pl.BlockSpec((L, 2048), lambda c: (rc(c), 0)),
                  pl.BlockSpec((L, 512), lambda c: (rc(c), 4)), pl.BlockSpec((L, 512), lambda c: (rc(c), 5)),
                  pl.BlockSpec((L, 512), lambda c: (rc(c), dtb)),
                  pl.BlockSpec((None, g, 512, 128), lambda c: (rc(c), 0, 0, 0)), vec, vec, vec],
        out_specs=[pl.BlockSpec((L, 512), lambda c: (rc(c), dtb)), pl.BlockSpec((L, 2048), lambda c: (rc(c), 0)),
                   pl.BlockSpec((L, 512), lambda c: (rc(c), 0)), pl.BlockSpec((L, 512), lambda c: (rc(c), 0)),
                   vec, vec, vec],
        out_shape=[jax.ShapeDtypeStruct(dzx.shape, dzx.dtype), jax.ShapeDtypeStruct((t, 2048), F32),
                   jax.ShapeDtypeStruct((t, 512), F32), jax.ShapeDtypeStruct((t, 512), F32),
                   jax.ShapeDtypeStruct((1, 512), F32), jax.ShapeDtypeStruct((1, 512), F32),
                   jax.ShapeDtypeStruct((1, 512), F32)],
        scratch_shapes=[pltpu.VMEM((g, 512, 128), F32)],
        input_output_aliases={0: 0},
        compiler_params=_params("arbitrary"),
    )(dzx, dy, xc, xc, xc, zx, states, bias, alog, dsk)


def _gate_tile(y, z, gn):
    gated = y * _silu(z)
    parts = [_rms(gated[:, k * 512:(k + 1) * 512]) for k in range(SSM_GROUPS)]
    return jnp.concatenate(parts, axis=1) * gn


def gate_out_fwd(h, y, zx, gn, w_out):
    t, d = h.shape
    di = y.shape[1]
    tm = _tile(t, 256)

    def body(h_ref, y_ref, z_ref, gn_ref, w_ref, o_ref):
        yn = _gate_tile(y_ref[...], z_ref[...], gn_ref[...])
        o_ref[...] = h_ref[...] + _dot(_b(yn), w_ref[...])

    return pl.pallas_call(
        body, name="gate_out_fwd",
        grid=(t // tm,),
        in_specs=[pl.BlockSpec((tm, d), lambda i: (i, 0)), pl.BlockSpec((tm, di), lambda i: (i, 0)),
                  pl.BlockSpec((tm, di), lambda i: (i, 0)), pl.BlockSpec((1, di), lambda i: (0, 0)),
                  pl.BlockSpec((di, d), lambda i: (0, 0))],
        out_specs=pl.BlockSpec((tm, d), lambda i: (i, 0)),
        out_shape=jax.ShapeDtypeStruct((t, d), F32),
        compiler_params=_params("arbitrary"),
    )(h, y, zx, gn, w_out)


def gate_out_bwd(dy, y, zx, gn, w_out, n_zx, dep):
    t, d = dy.shape
    di = y.shape[1]
    tm = _tile(t, 256)

    def body(dy_ref, y_ref, z_ref, gn_ref, w_ref, dep_ref, dz_ref, dys_ref, yn_ref, dgn_ref):
        @pl.when(pl.program_id(0) == 0)
        def _():
            dgn_ref[...] = jnp.zeros_like(dgn_ref)

        yn, vjp = jax.vjp(_gate_tile, y_ref[...], z_ref[...], gn_ref[...])
        dyn = _dot_nt(_b(dy_ref[...]), w_ref[...])
        dys, dz, dgn = vjp(dyn)
        yn_ref[...] = _b(yn)
        dys_ref[...] = dys
        dz_ref[...] = _b(dz)
        dgn_ref[...] += dgn

    return pl.pallas_call(
        body, name="gate_out_bwd",
        grid=(t // tm,),
        in_specs=[pl.BlockSpec((tm, d), lambda i: (i, 0)), pl.BlockSpec((tm, di), lambda i: (i, 0)),
                  pl.BlockSpec((tm, di), lambda i: (i, 0)), pl.BlockSpec((1, di), lambda i: (0, 0)),
                  pl.BlockSpec((di, d), lambda i: (0, 0)), DEP_SPEC],
        out_specs=[pl.BlockSpec((tm, di), lambda i: (i, 0)), pl.BlockSpec((tm, di), lambda i: (i, 0)),
                   pl.BlockSpec((tm, di), lambda i: (i, 0)), pl.BlockSpec((1, di), lambda i: (0, 0))],
        out_shape=[jax.ShapeDtypeStruct((t, n_zx), BF16), jax.ShapeDtypeStruct((t, di), F32),
                   jax.ShapeDtypeStruct((t, di), BF16), jax.ShapeDtypeStruct((1, di), F32)],
        compiler_params=_params("arbitrary"),
    )(dy, y, zx, gn, w_out, dep)


def _attn_block(qp, kvp, kvc, biasm, sinks, qg, kg, w_o, first):
    nq = qp.shape[0]
    n_pairs = qp.shape[1] // 128
    hk = n_pairs
    rows = hk * nq
    seg = functools.partial(_xright, m=_blockdiag64(128))
    scale = ATT_HEAD_DIM ** -0.5
    qi = (_iota((rows, 2 * nq), 0) % nq) + nq
    kj = _iota((rows, 2 * nq), 1)
    dist = qi - kj
    valid = (dist >= 0) & (dist < ATT_WINDOW) & (jnp.logical_not(first) | (kj >= nq))
    lo = _iota((nq, 128), 1) < 64
    kv = jnp.concatenate([kvp, kvc], axis=0)
    outs = [None] * n_pairs
    for kvh in range(2):
        kn = _group64_rms(kv[:, kvh * 128:(kvh + 1) * 128], seg) * kg
        vv = kv[:, 256 + kvh * 128:256 + (kvh + 1) * 128]
        pairs = range(kvh * hk // 2, (kvh + 1) * hk // 2)
        qs, sk = [], []
        for p in pairs:
            qn = _group64_rms(qp[:, p * 128:(p + 1) * 128], seg) * qg
            qs += [jnp.where(lo, qn, 0.0), jnp.where(lo, 0.0, qn)]
            sk += [jnp.broadcast_to(sinks[:, h:h + 1], (nq, 1)) for h in (2 * p, 2 * p + 1)]
        sink = jnp.concatenate(sk, axis=0)
        s = _bmm_nt(jnp.concatenate(qs, axis=0), kn) * scale + biasm[kvh * rows:(kvh + 1) * rows]
        s = jnp.where(valid, s, NEG)
        m = lax.stop_gradient(jnp.maximum(jnp.max(s, axis=-1, keepdims=True), sink))
        pexp = jnp.exp(s - m)
        den = jnp.sum(pexp, axis=-1, keepdims=True) + jnp.exp(sink - m)
        o = _bmm(pexp * (1.0 / den), vv)
        for n, p in enumerate(pairs):
            outs[p] = jnp.where(lo, o[2 * n * nq:(2 * n + 1) * nq], o[(2 * n + 1) * nq:(2 * n + 2) * nq])
    o = jnp.concatenate(outs, axis=1)
    return _bmm(o, w_o), o


def attn_fwd(h, qp, kvd, biasm, sinks, qg, kg, w_o):
    t, d = h.shape
    nq = ATT_WINDOW
    nb = t // nq
    nh = qp.shape[1] // ATT_HEAD_DIM

    def body(h_ref, q_ref, kp_ref, kc_ref, bias_ref, s_ref, qg_ref, kg_ref, w_ref, o_ref):
        out, _ = _attn_block(q_ref[...], kp_ref[...], kc_ref[...], bias_ref[...], s_ref[...], qg_ref[...],
                             kg_ref[...], w_ref[...], pl.program_id(0) == 0)
        o_ref[...] = h_ref[...] + out

    vec = pl.BlockSpec((1, 128), lambda i: (0, 0))
    return pl.pallas_call(
        body, name="attn_fwd",
        grid=(nb,),
        in_specs=[pl.BlockSpec((nq, d), lambda i: (i, 0)), pl.BlockSpec((nq, nh * 64), lambda i: (i, 0)),
                  pl.BlockSpec((nq, 512), lambda i: (jnp.maximum(i - 1, 0), 0)),
                  pl.BlockSpec((nq, 512), lambda i: (i, 0)),
                  pl.BlockSpec((nh * nq, 2 * nq), lambda i: (0, 0)), vec, vec, vec,
                  pl.BlockSpec((nh * 64, d), lambda i: (0, 0))],
        out_specs=pl.BlockSpec((nq, d), lambda i: (i, 0)),
        out_shape=jax.ShapeDtypeStruct((t, d), F32),
        compiler_params=_params("arbitrary"),
    )(h, qp, kvd, kvd, biasm, sinks, qg, kg, w_o)


def attn_bwd(dy, qp, kvd, biasm, sinks, qg, kg, w_o, dep):
    t, d = dy.shape
    nq = ATT_WINDOW
    nb = t // nq
    nh = qp.shape[1] // ATT_HEAD_DIM

    def body(dy_ref, q_ref, kp_ref, kc_ref, bias_ref, s_ref, qg_ref, kg_ref, w_ref, dep_ref,
             dq_ref, dkv_ref, o_ref, dbias_ref, ds_ref, dqg_ref, dkg_ref, carry):
        i = pl.program_id(0)

        @pl.when(i == 0)
        def _():
            carry[...] = jnp.zeros_like(carry)
            dbias_ref[...] = jnp.zeros_like(dbias_ref)
            ds_ref[...] = jnp.zeros_like(ds_ref)
            dqg_ref[...] = jnp.zeros_like(dqg_ref)
            dkg_ref[...] = jnp.zeros_like(dkg_ref)

        @pl.when(i < nb)
        def _():
            fn = functools.partial(_attn_block, w_o=w_ref[...], first=(i == 0))
            (_, o), vjp = jax.vjp(fn, q_ref[...], kp_ref[...], kc_ref[...], bias_ref[...], s_ref[...],
                                  qg_ref[...], kg_ref[...])
            dq, dkp, dkc, dbias, dsk, dqg, dkg = vjp((dy_ref[...], jnp.zeros((nq, nh * 64), F32)))
            dq_ref[...] = _b(dq)
            o_ref[...] = _b(o)
            dkv_ref[...] = _b(_fold64(carry[...] + dkp))
            carry[...] = dkc
            dbias_ref[...] += dbias
            ds_ref[...] += dsk
            dqg_ref[...] += _fold64(dqg)
            dkg_ref[...] += _fold64(dkg)

        @pl.when(i == nb)
        def _():
            dkv_ref[...] = _b(_fold64(carry[...]))

    cl = lambda i: jnp.minimum(i, nb - 1)
    vec = pl.BlockSpec((1, 128), lambda i: (0, 0))
    return pl.pallas_call(
        body, name="attn_bwd",
        grid=(nb + 1,),
        in_specs=[pl.BlockSpec((nq, d), lambda i: (cl(i), 0)), pl.BlockSpec((nq, nh * 64), lambda i: (cl(i), 0)),
                  pl.BlockSpec((nq, 512), lambda i: (jnp.maximum(cl(i) - 1, 0), 0)),
                  pl.BlockSpec((nq, 512), lambda i: (cl(i), 0)),
                  pl.BlockSpec((nh * nq, 2 * nq), lambda i: (0, 0)), vec, vec, vec,
                  pl.BlockSpec((nh * 64, d), lambda i: (0, 0)), DEP_SPEC],
        out_specs=[pl.BlockSpec((nq, nh * 64), lambda i: (cl(i), 0)),
                   pl.BlockSpec((nq, 512), lambda i: (jnp.maximum(i - 1, 0), 0)),
                   pl.BlockSpec((nq, nh * 64), lambda i: (cl(i), 0)),
                   pl.BlockSpec((nh * nq, 2 * nq), lambda i: (0, 0)), vec, vec, vec],
        out_shape=[jax.ShapeDtypeStruct((t, nh * 64), BF16), jax.ShapeDtypeStruct((t, 512), BF16),
                   jax.ShapeDtypeStruct((t, nh * 64), BF16), jax.ShapeDtypeStruct((nh * nq, 2 * nq), F32),
                   jax.ShapeDtypeStruct((1, 128), F32), jax.ShapeDtypeStruct((1, 128), F32),
                   jax.ShapeDtypeStruct((1, 128), F32)],
        scratch_shapes=[pltpu.VMEM((nq, 512), F32)],
        compiler_params=_params("arbitrary"),
    )(dy, qp, kvd, kvd, biasm, sinks, qg, kg, w_o, dep)


def _t5_buckets():
    nq = ATT_WINDOW
    dist = (np.arange(nq)[:, None] + nq) - np.arange(2 * nq)[None, :]
    n = np.maximum(dist, 0)
    max_exact = REL_BUCKETS // 2
    nf = np.maximum(n, 1).astype(np.float32)
    large = max_exact + (np.log(nf / max_exact) / math.log(ATT_WINDOW / max_exact)
                         * (REL_BUCKETS - max_exact)).astype(np.int32)
    large = np.minimum(large, REL_BUCKETS - 1)
    return np.where(n < max_exact, n, large).astype(np.int32)


def rel_bias_bwd(dbias, buckets):
    nh = dbias.shape[0]

    def body(db_ref, bk_ref, o_ref):
        bk = bk_ref[...]
        lane = _iota((1, 128), 1)
        row = _iota((REL_BUCKETS, 128), 0)
        acc = jnp.zeros((REL_BUCKETS, 128), F32)
        for h in range(nh):
            dbh = db_ref[h]
            for b in range(REL_BUCKETS):
                v = jnp.sum(jnp.where(bk == b, dbh, 0.0))
                acc = acc + jnp.where((row == b) & (lane == h), v, 0.0)
        o_ref[...] = acc

    return pl.pallas_call(
        body, name="rel_bias_bwd",
        out_shape=jax.ShapeDtypeStruct((REL_BUCKETS, 128), F32),
        compiler_params=_params(),
    )(dbias, buckets)


def loss_head(y, target):
    t, d = y.shape
    tm = _tile(t, 512)

    def body(y_ref, t_ref, l_ref, dy_ref):
        @pl.when(pl.program_id(0) == 0)
        def _():
            l_ref[...] = jnp.zeros_like(l_ref)

        e = y_ref[...] - t_ref[...]
        l_ref[...] += 0.5 * jnp.sum(jnp.mean(e * e, axis=-1, keepdims=True), axis=0, keepdims=True)
        dy_ref[...] = e * (1.0 / d)

    return pl.pallas_call(
        body, name="loss_head",
        grid=(t // tm,),
        in_specs=[pl.BlockSpec((tm, d), lambda i: (i, 0)), pl.BlockSpec((tm, d), lambda i: (i, 0))],
        out_specs=[pl.BlockSpec((1, 1), lambda i: (0, 0)), pl.BlockSpec((tm, d), lambda i: (i, 0))],
        out_shape=[jax.ShapeDtypeStruct((1, 1), F32), jax.ShapeDtypeStruct((t, d), F32)],
        compiler_params=_params("arbitrary"),
    )(y, target)


def adamw(w, g, m, v):
    r, c = w.shape
    tr = r if r <= 512 else _tile(r, 256)

    def body(w_ref, g_ref, m_ref, v_ref, d_ref, nm_ref, nv_ref):
        gg = g_ref[...]
        nm = ADAM_B1 * m_ref[...] + (1.0 - ADAM_B1) * gg
        nv = ADAM_B2 * v_ref[...] + (1.0 - ADAM_B2) * (gg * gg)
        m_hat = nm / (1.0 - ADAM_B1 ** ADAM_STEP)
        v_hat = nv / (1.0 - ADAM_B2 ** ADAM_STEP)
        d_ref[...] = -ADAM_LR * (m_hat / (jnp.sqrt(v_hat) + ADAM_EPS) + ADAM_WD * w_ref[...])
        nm_ref[...] = nm
        nv_ref[...] = nv

    spec = pl.BlockSpec((tr, c), lambda i: (i, 0))
    shp = jax.ShapeDtypeStruct((r, c), F32)
    return pl.pallas_call(
        body, name="adamw",
        grid=(r // tr,),
        in_specs=[spec] * 4, out_specs=[spec] * 3, out_shape=[shp] * 3,
        compiler_params=_params("arbitrary"),
    )(w, g, m, v)


def _my_pos():
    return lax.axis_index("x"), lax.axis_index("y"), lax.axis_index("c")


def _other_chips(x, y):
    return [(1 - x, y), (x, 1 - y), (1 - x, 1 - y)]


def _chip_id(x, y):
    return 2 * x + y


HBM_SPEC = pl.BlockSpec(memory_space=pltpu.HBM)
SEM_SPEC = pl.BlockSpec(memory_space=pltpu.SEMAPHORE)
DATAFLOW = pltpu.SideEffectType.DATAFLOW_SIDE_EFFECTING


def _in_hbm(a):
    return pltpu.with_memory_space_constraint(a, pltpu.HBM)


def _ici_gather_copy(buf, p, j, chip, c, to, send_sems, recv_sems):
    blk = buf.at[_chip_id(*chip), c]
    return pltpu.make_async_remote_copy(
        src_ref=blk, dst_ref=blk, send_sem=send_sems.at[3 * p + j], recv_sem=recv_sems.at[3 * p + j],
        device_id=to, device_id_type=MESH)


def gather_start(bufs, after, tag):
    n = len(bufs)

    def body(*refs):
        ins = refs[:n]
        send_sems, recv_sems = refs[n + 1], refs[n + 2]
        token = refs[2 * n + 3]
        x, y, c = _my_pos()
        for p in range(n):
            for j, chip in enumerate(_other_chips(x, y)):
                _ici_gather_copy(ins[p], p, j, (x, y), c, (*chip, c), send_sems, recv_sems).start()
        token[...] = jnp.zeros_like(token)

    out = pl.pallas_call(
        body, name="gather_start_" + tag,
        in_specs=[HBM_SPEC] * n + [DEP_SPEC],
        out_specs=(SEM_SPEC, SEM_SPEC, *([HBM_SPEC] * n), pl.BlockSpec(memory_space=pltpu.VMEM)),
        out_shape=(pltpu.SemaphoreType.DMA((3 * n,)), pltpu.SemaphoreType.DMA((3 * n,)),
                   *[pltpu.HBM(b.shape, b.dtype) for b in bufs], jax.ShapeDtypeStruct((8, 128), F32)),
        input_output_aliases={p: 2 + p for p in range(n)},
        compiler_params=pltpu.CompilerParams(has_side_effects=DATAFLOW),
    )(*[_in_hbm(b) for b in bufs], after)
    return out[0], out[1], list(out[2:2 + n]), out[2 + n]


def gather_wait(send_sems, recv_sems, bufs, after, tag):
    n = len(bufs)

    def body(*refs):
        ins = refs[:n]
        send_sems, recv_sems = refs[n], refs[n + 1]
        x, y, c = _my_pos()
        for p in range(n):
            for j, chip in enumerate(_other_chips(x, y)):
                _ici_gather_copy(ins[p], p, j, (x, y), c, (*chip, c), send_sems, recv_sems).wait_send()
                _ici_gather_copy(ins[p], p, j, chip, c, (x, y, c), send_sems, recv_sems).wait_recv()

    out = pl.pallas_call(
        body, name="gather_wait_" + tag,
        in_specs=[HBM_SPEC] * n + [SEM_SPEC, SEM_SPEC, pl.BlockSpec(memory_space=pl.ANY)],
        out_specs=[HBM_SPEC] * n,
        out_shape=[pltpu.HBM(b.shape, b.dtype) for b in bufs],
        input_output_aliases={p: p for p in range(n)},
        compiler_params=pltpu.CompilerParams(has_side_effects=DATAFLOW),
    )(*bufs, send_sems, recv_sems, after)
    return list(out)


def forward_to_sibling(bufs):
    n = len(bufs)

    def body(*refs):
        outs = refs[n:2 * n]
        send_sems, recv_sems = refs[2 * n:]
        x, y, c = _my_pos()
        chips = _other_chips(x, y)
        sent = []
        for p in range(n):
            for j, chip in enumerate(chips):
                cp = _ici_gather_copy(outs[p], p, j, chip, c, (x, y, 1 - c), send_sems, recv_sems)
                cp.start()
                sent.append(cp)
        for p in range(n):
            for j, chip in enumerate(chips):
                _ici_gather_copy(outs[p], p, j, chip, 1 - c, (x, y, c), send_sems, recv_sems).wait_recv()
        for cp in sent:
            cp.wait_send()

    any_spec = pl.BlockSpec(memory_space=pl.ANY)
    return pl.pallas_call(
        body, name="forward_to_sibling",
        in_specs=[any_spec] * n, out_specs=[any_spec] * n,
        out_shape=[jax.ShapeDtypeStruct(b.shape, b.dtype) for b in bufs],
        scratch_shapes=[pltpu.SemaphoreType.DMA((3 * n,)), pltpu.SemaphoreType.DMA((3 * n,))],
        input_output_aliases={p: p for p in range(n)},
    )(*bufs)


def allreduce_small(v):
    r, c = v.shape

    def body(v_ref, o_ref, buf, send_sems, recv_sems):
        x, y, cc = _my_pos()
        me = 4 * x + 2 * y + cc
        buf[me] = v_ref[...]
        copies = []
        for k in range(1, 8):
            dx, dy, dc = (k >> 2) & 1, (k >> 1) & 1, k & 1
            peer = (x ^ dx, y ^ dy, cc ^ dc)
            cp = pltpu.make_async_remote_copy(
                src_ref=v_ref, dst_ref=buf.at[me], send_sem=send_sems.at[k - 1], recv_sem=recv_sems.at[k - 1],
                device_id=peer, device_id_type=MESH)
            cp.start()
            copies.append(cp)
        for cp in copies:
            cp.wait_recv()
        for cp in copies:
            cp.wait_send()
        acc = buf[0]
        for k in range(1, 8):
            acc = acc + buf[k]
        o_ref[...] = acc

    vm = pl.BlockSpec(memory_space=pltpu.VMEM)
    return pl.pallas_call(
        body, name="allreduce_small",
        in_specs=[vm], out_specs=vm,
        out_shape=jax.ShapeDtypeStruct((r, c), F32),
        scratch_shapes=[pltpu.VMEM((8, r, c), F32), pltpu.SemaphoreType.DMA((7,)), pltpu.SemaphoreType.DMA((7,))],
    )(v)


def _sibling_half_copy(grad, land, p, c, sibling, send_sems, recv_sems):
    return pltpu.make_async_remote_copy(
        src_ref=grad.at[:, 1 - c], dst_ref=land, send_sem=send_sems.at[p], recv_sem=recv_sems.at[p],
        device_id=sibling, device_id_type=MESH)


def sibling_halves_start(grads, tag):
    n = len(grads)
    lands = [lax.empty((g.shape[0],) + g.shape[2:], g.dtype) for g in grads]

    def body(*refs):
        ins, land = refs[:n], refs[n:2 * n]
        send_sems, recv_sems = refs[2 * n], refs[2 * n + 1]
        token = refs[4 * n + 2]
        x, y, c = _my_pos()
        for p in range(n):
            _sibling_half_copy(ins[p], land[p], p, c, (x, y, 1 - c), send_sems, recv_sems).start()
        token[...] = jnp.zeros_like(token)

    out = pl.pallas_call(
        body, name="sibling_halves_start_" + tag,
        in_specs=[HBM_SPEC] * (2 * n),
        out_specs=(SEM_SPEC, SEM_SPEC, *([HBM_SPEC] * (2 * n)), pl.BlockSpec(memory_space=pltpu.VMEM)),
        out_shape=(pltpu.SemaphoreType.DMA((n,)), pltpu.SemaphoreType.DMA((n,)),
                   *[pltpu.HBM(a.shape, a.dtype) for a in grads + lands], jax.ShapeDtypeStruct((8, 128), F32)),
        input_output_aliases={i: 2 + i for i in range(2 * n)},
        compiler_params=pltpu.CompilerParams(has_side_effects=DATAFLOW),
    )(*[_in_hbm(a) for a in grads + lands])
    return out[0], out[1], list(out[2:2 + n]), list(out[2 + n:2 + 2 * n]), out[2 + 2 * n]


def sibling_halves_wait(send_sems, recv_sems, grads, lands, after, tag):
    n = len(grads)

    def body(*refs):
        ins, land = refs[:n], refs[n:2 * n]
        send_sems, recv_sems = refs[2 * n], refs[2 * n + 1]
        x, y, c = _my_pos()
        for p in range(n):
            cp = _sibling_half_copy(ins[p], land[p], p, c, (x, y, 1 - c), send_sems, recv_sems)
            cp.wait_send()
            cp.wait_recv()

    out = pl.pallas_call(
        body, name="sibling_halves_wait_" + tag,
        in_specs=[HBM_SPEC] * (2 * n) + [SEM_SPEC, SEM_SPEC, pl.BlockSpec(memory_space=pl.ANY)],
        out_specs=[HBM_SPEC] * (2 * n),
        out_shape=[pltpu.HBM(a.shape, a.dtype) for a in grads + lands],
        input_output_aliases={i: i for i in range(2 * n)},
        compiler_params=pltpu.CompilerParams(has_side_effects=DATAFLOW),
    )(*grads, *lands, send_sems, recv_sems, after)
    return list(out[:n]), list(out[n:])


def _chip_partial_copy(part, land, p, j, chip, c, send_sems, recv_sems):
    return pltpu.make_async_remote_copy(
        src_ref=part.at[_chip_id(*chip)], dst_ref=land.at[j], send_sem=send_sems.at[3 * p + j],
        recv_sem=recv_sems.at[3 * p + j], device_id=(*chip, c), device_id_type=MESH)


def chip_partials_start(parts, tag):
    n = len(parts)
    lands = [lax.empty((N_CHIPS - 1,) + s.shape[1:], s.dtype) for s in parts]

    def body(*refs):
        ins, land = refs[:n], refs[n:2 * n]
        send_sems, recv_sems = refs[2 * n], refs[2 * n + 1]
        token = refs[4 * n + 2]
        x, y, c = _my_pos()
        for p in range(n):
            for j, chip in enumerate(_other_chips(x, y)):
                _chip_partial_copy(ins[p], land[p], p, j, chip, c, send_sems, recv_sems).start()
        token[...] = jnp.zeros_like(token)

    out = pl.pallas_call(
        body, name="chip_partials_start_" + tag,
        in_specs=[HBM_SPEC] * (2 * n),
        out_specs=(SEM_SPEC, SEM_SPEC, *([HBM_SPEC] * (2 * n)), pl.BlockSpec(memory_space=pltpu.VMEM)),
        out_shape=(pltpu.SemaphoreType.DMA((3 * n,)), pltpu.SemaphoreType.DMA((3 * n,)),
                   *[pltpu.HBM(a.shape, a.dtype) for a in parts + lands], jax.ShapeDtypeStruct((8, 128), F32)),
        input_output_aliases={i: 2 + i for i in range(2 * n)},
        compiler_params=pltpu.CompilerParams(has_side_effects=DATAFLOW),
    )(*[_in_hbm(a) for a in parts + lands])
    return out[0], out[1], list(out[2:2 + n]), list(out[2 + n:2 + 2 * n]), out[2 + 2 * n]


def chip_partials_wait(send_sems, recv_sems, parts, lands, after, tag):
    n = len(parts)

    def body(*refs):
        ins, land = refs[:n], refs[n:2 * n]
        send_sems, recv_sems = refs[2 * n], refs[2 * n + 1]
        x, y, c = _my_pos()
        for p in range(n):
            for j, chip in enumerate(_other_chips(x, y)):
                cp = _chip_partial_copy(ins[p], land[p], p, j, chip, c, send_sems, recv_sems)
                cp.wait_send()
                cp.wait_recv()

    out = pl.pallas_call(
        body, name="chip_partials_wait_" + tag,
        in_specs=[HBM_SPEC] * (2 * n) + [SEM_SPEC, SEM_SPEC, pl.BlockSpec(memory_space=pl.ANY)],
        out_specs=[HBM_SPEC] * (2 * n),
        out_shape=[pltpu.HBM(a.shape, a.dtype) for a in parts + lands],
        input_output_aliases={i: i for i in range(2 * n)},
        compiler_params=pltpu.CompilerParams(has_side_effects=DATAFLOW),
    )(*parts, *lands, send_sems, recv_sems, after)
    return list(out[n:])


def share_with_sibling(bufs):
    n = len(bufs)

    def body(*refs):
        outs = refs[n:2 * n]
        send_sems, recv_sems = refs[2 * n:]
        x, y, c = _my_pos()
        copies = []
        for p in range(n):
            cp = pltpu.make_async_remote_copy(
                src_ref=outs[p].at[c], dst_ref=outs[p].at[c], send_sem=send_sems.at[p], recv_sem=recv_sems.at[p],
                device_id=(x, y, 1 - c), device_id_type=MESH)
            cp.start()
            copies.append(cp)
        for p in range(n):
            pltpu.make_async_remote_copy(
                src_ref=outs[p].at[1 - c], dst_ref=outs[p].at[1 - c], send_sem=send_sems.at[p],
                recv_sem=recv_sems.at[p], device_id=(x, y, 1 - c), device_id_type=MESH).wait_recv()
        for cp in copies:
            cp.wait_send()

    any_spec = pl.BlockSpec(memory_space=pl.ANY)
    return pl.pallas_call(
        body, name="share_with_sibling",
        in_specs=[any_spec] * n, out_specs=[any_spec] * n,
        out_shape=[jax.ShapeDtypeStruct(b.shape, b.dtype) for b in bufs],
        scratch_shapes=[pltpu.SemaphoreType.DMA((n,)), pltpu.SemaphoreType.DMA((n,))],
        input_output_aliases={p: p for p in range(n)},
    )(*bufs)


def add_sibling(g, recv, half):
    _, _, r, c = g.shape
    tr = _tile(r, 256) if r % 256 == 0 else r

    def body(half_ref, g_ref, r_ref, o32_ref, o16_ref):
        s = g_ref[...] + r_ref[...]
        o32_ref[...] = s
        o16_ref[...] = _b(s)

    return pl.pallas_call(
        body, name="add_sibling",
        grid_spec=pltpu.PrefetchScalarGridSpec(
            num_scalar_prefetch=1, grid=(N_CHIPS, r // tr),
            in_specs=[pl.BlockSpec((None, None, tr, c), lambda k, i, hf: (k, hf[0], i, 0)),
                      pl.BlockSpec((None, tr, c), lambda k, i, hf: (k, i, 0))],
            out_specs=[pl.BlockSpec((None, tr, c), lambda k, i, hf: (k, i, 0)),
                       pl.BlockSpec((None, tr, c), lambda k, i, hf: (k, i, 0))]),
        out_shape=[jax.ShapeDtypeStruct((N_CHIPS, r, c), F32), jax.ShapeDtypeStruct((N_CHIPS, r, c), BF16)],
        compiler_params=_params("arbitrary", "arbitrary"),
    )(half, g, recv)


def add_chip_partials(p32, recv, pos):
    _, r, c = p32.shape
    tr = _tile(r, 256) if r % 256 == 0 else r

    def body(pos_ref, p_ref, r_ref, o_ref):
        acc = p_ref[...]
        for j in range(N_CHIPS - 1):
            acc = acc + r_ref[j].astype(F32)
        o_ref[...] = acc

    return pl.pallas_call(
        body, name="add_chip_partials",
        grid_spec=pltpu.PrefetchScalarGridSpec(
            num_scalar_prefetch=1, grid=(r // tr,),
            in_specs=[pl.BlockSpec((None, tr, c), lambda i, ps: (ps[0], i, 0)),
                      pl.BlockSpec((N_CHIPS - 1, tr, c), lambda i, ps: (0, i, 0))],
            out_specs=pl.BlockSpec((None, tr, c), lambda i, ps: (ps[1], i, 0))),
        out_shape=jax.ShapeDtypeStruct((2, r, c), F32),
        compiler_params=_params("arbitrary"),
    )(pos, p32, recv)


def cast_into_gather(w, pos, dep, row0=0, nrows=None):
    c = w.shape[1]
    nrows = w.shape[0] if nrows is None else nrows
    r = nrows // 2
    common = math.gcd(r, row0) if row0 else r
    tr = max(w for w in range(16, min(common, 512) + 1, 16) if common % w == 0)
    nt = r // tr

    def body(pos_ref, w_ref, dep_ref, o_ref):
        o_ref[...] = _b(w_ref[...])

    return pl.pallas_call(
        body, name="cast_into_gather",
        grid_spec=pltpu.PrefetchScalarGridSpec(
            num_scalar_prefetch=1, grid=(2, nt),
            in_specs=[pl.BlockSpec((tr, c), lambda hf, i, ps: (row0 // tr + hf * nt + i, 0)), DEP_SPEC],
            out_specs=pl.BlockSpec((None, None, tr, c), lambda hf, i, ps: (ps[0], hf, i, 0))),
        out_shape=jax.ShapeDtypeStruct((N_CHIPS, 2, r, c), BF16),
        compiler_params=_params("arbitrary", "arbitrary"),
    )(pos, w, dep)


def build_bias(rel, buckets):
    nb, nh = rel.shape

    def body(rel_ref, bk_ref, o_ref):
        bk = bk_ref[...]
        for h in range(nh):
            acc = jnp.zeros(bk.shape, F32)
            for b in range(nb):
                acc = jnp.where(bk == b, rel_ref[b, h], acc)
            o_ref[h] = acc

    return pl.pallas_call(
        body, name="build_bias",
        in_specs=[pl.BlockSpec(memory_space=pltpu.SMEM), pl.BlockSpec(memory_space=pltpu.VMEM)],
        out_specs=pl.BlockSpec(memory_space=pltpu.VMEM),
        out_shape=jax.ShapeDtypeStruct((nh,) + buckets.shape, F32),
        compiler_params=_params(),
    )(rel, buckets)


SMALL_ROWS = 256


def kernel(x, ffn_norm, ffn_w1, ffn_w3, ffn_w2, ssm_norm, ssm_w_in, ssm_conv_w, ssm_conv_b, ssm_dt_bias, ssm_a_log, ssm_d, ssm_gate_norm, ssm_w_out, kv_norm, w_kv, k_norm, attn_norm, w_q, q_norm, sinks, w_o, rel_bias, loss_target, m_ffn_norm, m_ffn_w1, m_ffn_w3, m_ffn_w2, m_ssm_norm, m_ssm_w_in, m_ssm_conv_w, m_ssm_conv_b, m_ssm_dt_bias, m_ssm_a_log, m_ssm_d, m_ssm_gate_norm, m_ssm_w_out, m_kv_norm, m_w_kv, m_k_norm, m_attn_norm, m_w_q, m_q_norm, m_sinks, m_w_o, m_rel_bias, v_ffn_norm, v_ffn_w1, v_ffn_w3, v_ffn_w2, v_ssm_norm, v_ssm_w_in, v_ssm_conv_w, v_ssm_conv_b, v_ssm_dt_bias, v_ssm_a_log, v_ssm_d, v_ssm_gate_norm, v_ssm_w_out, v_kv_norm, v_w_kv, v_k_norm, v_attn_norm, v_w_q, v_q_norm, v_sinks, v_w_o, v_rel_bias):
    weights = dict(ffn_norm=ffn_norm, ffn_w1=ffn_w1, ffn_w3=ffn_w3, ffn_w2=ffn_w2, ssm_norm=ssm_norm,
                   ssm_w_in=ssm_w_in, ssm_conv_w=ssm_conv_w, ssm_conv_b=ssm_conv_b, ssm_dt_bias=ssm_dt_bias,
                   ssm_a_log=ssm_a_log, ssm_d=ssm_d, ssm_gate_norm=ssm_gate_norm, ssm_w_out=ssm_w_out,
                   kv_norm=kv_norm, w_kv=w_kv, k_norm=k_norm, attn_norm=attn_norm, w_q=w_q, q_norm=q_norm,
                   sinks=sinks, w_o=w_o, rel_bias=rel_bias)
    m_in = dict(ffn_norm=m_ffn_norm, ffn_w1=m_ffn_w1, ffn_w3=m_ffn_w3, ffn_w2=m_ffn_w2, ssm_norm=m_ssm_norm,
                ssm_w_in=m_ssm_w_in, ssm_conv_w=m_ssm_conv_w, ssm_conv_b=m_ssm_conv_b, ssm_dt_bias=m_ssm_dt_bias,
                ssm_a_log=m_ssm_a_log, ssm_d=m_ssm_d, ssm_gate_norm=m_ssm_gate_norm, ssm_w_out=m_ssm_w_out,
                kv_norm=m_kv_norm, w_kv=m_w_kv, k_norm=m_k_norm, attn_norm=m_attn_norm, w_q=m_w_q, q_norm=m_q_norm,
                sinks=m_sinks, w_o=m_w_o, rel_bias=m_rel_bias)
    v_in = dict(ffn_norm=v_ffn_norm, ffn_w1=v_ffn_w1, ffn_w3=v_ffn_w3, ffn_w2=v_ffn_w2, ssm_norm=v_ssm_norm,
                ssm_w_in=v_ssm_w_in, ssm_conv_w=v_ssm_conv_w, ssm_conv_b=v_ssm_conv_b, ssm_dt_bias=v_ssm_dt_bias,
                ssm_a_log=v_ssm_a_log, ssm_d=v_ssm_d, ssm_gate_norm=v_ssm_gate_norm, ssm_w_out=v_ssm_w_out,
                kv_norm=v_kv_norm, w_kv=v_w_kv, k_norm=v_k_norm, attn_norm=v_attn_norm, w_q=v_w_q, q_norm=v_q_norm,
                sinks=v_sinks, w_o=v_w_o, rel_bias=v_rel_bias)
    return _step(x[0], loss_target[0], weights, m_in, v_in)


BIG = ("ffn_w1", "ffn_w3", "ffn_w2", "ssm_w_in", "ssm_w_out", "w_kv", "w_q", "w_o")
SMALL = (("ffn_norm", True), ("ssm_norm", True), ("ssm_conv_w", True), ("ssm_conv_b", True),
         ("ssm_gate_norm", True), ("ssm_dt_bias", False), ("ssm_a_log", False), ("ssm_d", False),
         ("kv_norm", False), ("k_norm", False), ("attn_norm", False), ("q_norm", False), ("sinks", False),
         ("rel_bias", False))


FFN_W = BIG[:3]


def _small_layout(weights):
    off, table = 0, {}
    for name, sharded in SMALL:
        shape = weights[name].shape
        full = shape[:-1] + (shape[-1] * N_CHIPS,) if sharded else shape
        n = int(np.prod(full))
        table[name] = (off, full, sharded)
        off += n
    assert off <= SMALL_ROWS * 128
    return table


def _place_small(values, table, chip, scale_mask):
    flat = jnp.zeros((SMALL_ROWS * 128,), F32)
    for name, (off, full, sharded) in table.items():
        if not sharded:
            continue
        v = values[name].astype(F32)
        lead = int(np.prod(full[:-1]))
        w = v.shape[-1]
        blk = jnp.zeros((lead, full[-1]), F32)
        blk = lax.dynamic_update_slice(blk, v.reshape(lead, w) * scale_mask, (0, chip * w))
        flat = lax.dynamic_update_slice(flat, blk.reshape(-1), (off,))
    return flat.reshape(SMALL_ROWS, 128)


def _take_small(mat, table, name):
    off, full, _ = table[name]
    n = int(np.prod(full))
    return mat.reshape(-1)[off:off + n].reshape(full)


def _step(x, target, weights, m_in, v_in):
    t, d = x.shape
    xi, yi, ci = lax.axis_index("x"), lax.axis_index("y"), lax.axis_index("c")
    chip = 2 * xi + yi
    pos_arr = jnp.stack([chip, ci]).astype(jnp.int32)
    half_arr = jnp.reshape(ci, (1,)).astype(jnp.int32)

    fs = weights["ffn_w1"].shape[-1]
    ffn_rows = {"ffn_w1": d, "ffn_w3": d, "ffn_w2": fs}
    w2d = {n: weights[n].reshape(-1, weights[n].shape[-1]) for n in BIG}
    mamba_w = ("ssm_w_in", "ssm_w_out")
    late_w = ("w_kv", "w_q", "w_o")
    fs_, fr_, fbufs, tok_f = gather_start(
        [cast_into_gather(w2d[n], pos_arr, pos_arr, 0, ffn_rows[n]) for n in FFN_W], pos_arr, "first")
    ms, mr, mbufs, tok_m = gather_start([cast_into_gather(w2d[n], pos_arr, tok_f) for n in mamba_w], tok_f, "mamba")
    ls, lr, lbufs, tok_l = gather_start(
        [cast_into_gather(w2d[n], pos_arr, tok_f, ffn_rows[n], 3 * ffn_rows[n]) for n in FFN_W]
        + [cast_into_gather(w2d[n], pos_arr, tok_f) for n in late_w], tok_m, "late")
    first = forward_to_sibling(gather_wait(fs_, fr_, fbufs, tok_l, "first"))
    no_dep = jnp.zeros((8, 128), F32)
    table = _small_layout(weights)
    south = (ci == 0).astype(F32)
    small = allreduce_small(_place_small(weights, table, chip, south))
    sp = {n: _take_small(small, table, n) if sh else weights[n] for n, sh in SMALL}

    ffn_first = [first[0].reshape(N_CHIPS, 1, d, fs), first[1].reshape(N_CHIPS, 1, d, fs),
                 first[2].reshape(N_CHIPS, 1, fs, d)]
    ffn_g = sp["ffn_norm"]
    h0 = x
    h1, a00, b00 = ffn_fwd(h0, ffn_g[0, 0].reshape(1, d), *ffn_first, 0, no_dep)
    gathered = dict(zip(mamba_w, forward_to_sibling(gather_wait(ms, mr, mbufs, h1, "mamba"))))
    n_in = weights["ssm_w_in"].shape[-1] * N_CHIPS
    di = weights["ssm_w_out"].shape[1] * N_CHIPS
    nheads = di // SSM_HEAD_DIM
    conv_dim = n_in - di - nheads
    w_in_full = jnp.moveaxis(gathered["ssm_w_in"].reshape(N_CHIPS, d, n_in // N_CHIPS), 0, 1).reshape(d, n_in)
    hpg = nheads // SSM_GROUPS

    def spread_heads(v):
        lead = v.shape[:-1]
        v = v.reshape(lead + (SSM_GROUPS, hpg))
        v = jnp.pad(v, [(0, 0)] * len(lead) + [(0, 0), (0, 128 - hpg)])
        return v.reshape(lead + (SSM_GROUPS * 128,))

    def gather_heads(v):
        lead = v.shape[:-1]
        return v.reshape(lead + (SSM_GROUPS, 128))[..., :hpg].reshape(lead + (nheads,))

    dt_col0 = di + conv_dim
    n_zx = dt_col0 + SSM_GROUPS * 128
    w_in = jnp.concatenate([w_in_full[:, :dt_col0], spread_heads(w_in_full[:, dt_col0:])], axis=1)
    w_out = gathered["ssm_w_out"].reshape(di, d)
    nkv = weights["w_kv"].shape[1] // (2 * ATT_HEAD_DIM)
    assert nkv == 2
    nh = weights["w_q"].shape[-1] // ATT_HEAD_DIM

    ssm_g = sp["ssm_norm"].reshape(1, d)
    cw = jnp.pad(sp["ssm_conv_w"].reshape(SSM_CONV, conv_dim), [(0, 8 - SSM_CONV), (0, 0)])
    cb = sp["ssm_conv_b"].reshape(1, conv_dim)
    gate_g = sp["ssm_gate_norm"].reshape(1, di)
    dt_bias = spread_heads(sp["ssm_dt_bias"].reshape(1, nheads))
    a_log = spread_heads(sp["ssm_a_log"].reshape(1, nheads))
    d_skip = spread_heads(sp["ssm_d"].reshape(1, nheads))
    kv_g = sp["kv_norm"].reshape(1, d)
    k_g = jnp.tile(sp["k_norm"].reshape(1, ATT_HEAD_DIM), (1, 2))
    attn_g = sp["attn_norm"].reshape(1, d)
    q_g = jnp.tile(sp["q_norm"].reshape(1, ATT_HEAD_DIM), (1, 2))
    sink_row = jnp.pad(sp["sinks"].reshape(1, nh), [(0, 0), (0, 128 - nh)])
    buckets = jnp.asarray(_t5_buckets())
    biasm = build_bias(sp["rel_bias"], buckets).reshape(nh * ATT_WINDOW, 2 * ATT_WINDOW)

    zx = norm_mm(h1, ssm_g, w_in)
    xc = conv_fwd(zx, cw, cb, di)
    y_ssd, states = ssd_fwd(xc, zx, dt_bias, a_log, d_skip, dt_col0)
    h2 = gate_out_fwd(h1, y_ssd, zx, gate_g, w_out)

    late = forward_to_sibling(gather_wait(ls, lr, lbufs, h2, "late"))
    ffn_rest = [late[0].reshape(N_CHIPS, 3, d, fs), late[1].reshape(N_CHIPS, 3, d, fs),
                late[2].reshape(N_CHIPS, 3, fs, d)]
    gathered.update(zip(late_w, late[3:]))
    wkv_heads = gathered["w_kv"].reshape(d, 2 * nkv, 1, ATT_HEAD_DIM)
    w_kvd = jnp.broadcast_to(wkv_heads, (d, 2 * nkv, 2, ATT_HEAD_DIM)).reshape(d, 4 * nkv * ATT_HEAD_DIM)
    wq = gathered["w_q"].reshape(d, -1)
    wo = gathered["w_o"].reshape(-1, d)

    def ffn_w(layer, idx):
        blk = 2 * layer + idx
        return (*ffn_first, 0) if blk == 0 else (*ffn_rest, blk - 1)

    h3, a01, b01 = ffn_fwd(h2, ffn_g[0, 1].reshape(1, d), *ffn_w(0, 1), no_dep)
    kvd = norm_mm(h3, kv_g, w_kvd)
    h4, a10, b10 = ffn_fwd(h3, ffn_g[1, 0].reshape(1, d), *ffn_w(1, 0), no_dep)
    qp = norm_mm(h4, attn_g, wq)
    h5 = attn_fwd(h4, qp, kvd, biasm, sink_row, q_g, k_g, wo)
    h6, a11, b11 = ffn_fwd(h5, ffn_g[1, 1].reshape(1, d), *ffn_w(1, 1), no_dep)
    loss_part, d6 = loss_head(h6, target)
    loss = lax.psum(loss_part[0, 0], ("x", "y", "c"))

    gfn = [[None, None], [None, None]]

    pending = {}

    def swap_start(pieces, tag):
        views = [g.reshape(N_CHIPS, 2, g.shape[1] // 2, g.shape[2]) for _, g in pieces]
        ss, rs, views, lands, token = sibling_halves_start(views, tag)
        pending[tag] = dict(keys=[k for k, _ in pieces], swap=(ss, rs, views, lands))
        return token

    def partials_start(tag, after):
        views, recv1 = sibling_halves_wait(*pending[tag]["swap"], after, tag)
        p32, p16 = zip(*[add_sibling(g, r, half_arr) for g, r in zip(views, recv1)])
        ss, rs, parts, lands, token = chip_partials_start(list(p16), tag)
        pending[tag].update(p32=p32, partials=(ss, rs, parts, lands))
        return token

    def ffn_back(h_in, dy, a_s, b_s, layer, idx, dep, wdep):
        dh, u, da, db, s, dg = ffn_bwd(h_in, dy, ffn_g[layer, idx].reshape(1, d), a_s, b_s, *ffn_w(layer, idx), dep)
        gfn[layer][idx] = dg
        return dh, [(("ffn_w1", layer, idx), wgrad_grouped_b(u, da, wdep)),
                    (("ffn_w3", layer, idx), wgrad_grouped_b(u, db, no_dep)),
                    (("ffn_w2", layer, idx), wgrad_grouped_a(s, dy, no_dep, 0.5))]

    d5, pieces = ffn_back(h5, d6, a11, b11, 1, 1, no_dep, no_dep)
    tok = swap_start(pieces, "ffn11")
    dqp, dkvd, o16, dbiasm, dsinks, dqg, dkg = attn_bwd(d5, qp, kvd, biasm, sink_row, q_g, k_g, wo, tok)
    tok = partials_start("ffn11", dqp)
    g_wo = wgrad(o16, d5)
    d4, u_q, g_attn_norm = norm_mm_bwd(h4, attn_g, wq, dqp, d5, tok)
    g_wq = wgrad(u_q, dqp)
    d3a, pieces = ffn_back(h3, d4, a10, b10, 1, 0, no_dep, no_dep)
    pieces += [(("w_o",), g_wo.reshape(N_CHIPS, -1, d)), (("w_q",), g_wq.reshape(N_CHIPS, d // N_CHIPS, -1))]
    tok = swap_start(pieces, "ffn10")
    d3, u_kv, g_kv_norm = norm_mm_bwd(h3, kv_g, w_kvd, dkvd, d3a, tok, 0.5)
    tok = partials_start("ffn10", d3)
    g_wkvd = wgrad(u_kv, dkvd)
    g_wkv = g_wkvd.reshape(d, 2 * nkv, 2, ATT_HEAD_DIM)[:, :, 0, :].reshape(d, 2 * nkv * ATT_HEAD_DIM)
    d2, pieces = ffn_back(h2, d3, a01, b01, 0, 1, tok, no_dep)
    pieces += [(("w_kv",), g_wkv.reshape(N_CHIPS, d // N_CHIPS, -1))]
    tok = swap_start(pieces, "ffn01")
    dzx, dy_ssd, yn16, g_gate = gate_out_bwd(d2, y_ssd, zx, gate_g, w_out, n_zx, tok)
    tok = partials_start("ffn01", dy_ssd)
    g_wout = wgrad(yn16, d2)
    dzx, dxs, dbm, dcm, g_dtb, g_alog, g_dsk = ssd_bwd(dzx, dy_ssd, xc, zx, states, dt_bias, a_log, d_skip, dt_col0)
    dzx, g_cw, g_cb = conv_bwd(dzx, zx, dxs, dbm, dcm, cw, cb, di)
    d1, u_in, g_ssm_norm = norm_mm_bwd(h1, ssm_g, w_in, dzx, d2, tok)
    g_win = wgrad(u_in, dzx)
    g_win_full = jnp.concatenate([g_win[:, :dt_col0], gather_heads(g_win[:, dt_col0:])], axis=1)
    pieces = [(("ssm_w_in",), jnp.moveaxis(g_win_full.reshape(d, N_CHIPS, n_in // N_CHIPS), 1, 0)),
              (("ssm_w_out",), g_wout.reshape(N_CHIPS, di // N_CHIPS, d))]
    tok = swap_start(pieces, "mamba")
    grad_x, u0, da0, db0, s0, gfn[0][0] = ffn_bwd(h0, d1, ffn_g[0, 0].reshape(1, d), a00, b00, *ffn_w(0, 0), tok)
    tok = partials_start("mamba", grad_x)
    g1 = wgrad_grouped_b(u0, da0, tok)
    tok = swap_start([(("ffn_w1", 0, 0), g1)], "ffn00a")
    g3 = wgrad_grouped_b(u0, db0, tok)
    tok = partials_start("ffn00a", g3) + swap_start([(("ffn_w3", 0, 0), g3)], "ffn00b")
    g2 = wgrad_grouped_a(s0, d1, tok, 0.5)
    tok = partials_start("ffn00b", g2) + swap_start([(("ffn_w2", 0, 0), g2)], "ffn00")
    g_relb = rel_bias_bwd(dbiasm.reshape(nh, ATT_WINDOW, 2 * ATT_WINDOW), buckets)

    reduced = {}

    def finish(tag, after):
        st = pending[tag]
        lands = chip_partials_wait(*st["partials"], after, tag)
        for k, p, r in zip(st["keys"], st["p32"], lands):
            reduced[k] = add_chip_partials(p, r, pos_arr)
        return reduced[st["keys"][-1]]

    last = finish("ffn10", finish("ffn11", tok))
    tok = partials_start("ffn00", last)
    last = finish("ffn00b", finish("ffn00a", finish("mamba", finish("ffn01", tok))))
    finish("ffn00", last)
    keys = list(reduced)
    shared = dict(zip(keys, share_with_sibling([reduced[k] for k in keys])))
    grads = {}
    for n in FFN_W:
        blocks = [shared[(n, l, i)].reshape(1, ffn_rows[n], -1) for l in range(2) for i in range(2)]
        grads[n] = jnp.concatenate(blocks, axis=0).reshape(weights[n].shape)
    for n in BIG[3:]:
        grads[n] = shared[(n,)].reshape(weights[n].shape)

    small_grads = {
        "ffn_norm": jnp.stack([jnp.stack([gfn[l][i].reshape(d) for i in range(2)]) for l in range(2)]),
        "ssm_norm": g_ssm_norm.reshape(1, d),
        "ssm_conv_w": g_cw[:SSM_CONV].reshape(1, SSM_CONV, conv_dim),
        "ssm_conv_b": g_cb.reshape(1, conv_dim),
        "ssm_gate_norm": g_gate.reshape(1, di),
        "ssm_dt_bias": gather_heads(g_dtb.reshape(1, -1)), "ssm_a_log": gather_heads(g_alog.reshape(1, -1)),
        "ssm_d": gather_heads(g_dsk.reshape(1, -1)),
        "kv_norm": g_kv_norm.reshape(d), "k_norm": dkg[0, :ATT_HEAD_DIM], "attn_norm": g_attn_norm.reshape(1, d),
        "q_norm": dqg[:, :ATT_HEAD_DIM], "sinks": dsinks[:, :nh], "rel_bias": g_relb[:, :nh],
    }
    flat = jnp.zeros((SMALL_ROWS * 128,), F32)
    for name, (off, fshape, _) in table.items():
        flat = lax.dynamic_update_slice(flat, small_grads[name].astype(F32).reshape(-1), (off,))
    small_sum = allreduce_small(flat.reshape(SMALL_ROWS, 128))
    for name, (off, fshape, sharded) in table.items():
        g = _take_small(small_sum, table, name)
        if sharded:
            w = weights[name].shape[-1]
            lead = int(np.prod(fshape[:-1]))
            g = lax.dynamic_slice(g.reshape(lead, fshape[-1]), (0, chip * w), (lead, w)).reshape(weights[name].shape)
        grads[name] = g.reshape(weights[name].shape)

    names = list(weights)
    deltas, new_m, new_v = {}, {}, {}
    small_names = [n for n, _ in SMALL]
    for n in BIG:
        shp = weights[n].shape
        v2 = lambda a: a.reshape(-1, shp[-1])
        dl, nm, nv = adamw(v2(weights[n]), v2(grads[n]), v2(m_in[n]), v2(v_in[n]))
        deltas[n], new_m[n], new_v[n] = dl.reshape(shp), nm.reshape(shp), nv.reshape(shp)
    sizes = [int(np.prod(weights[n].shape)) for n in small_names]
    tot = sum(sizes)
    rows = -(-tot // 128)
    rows = -(-rows // 8) * 8

    def pack(dct):
        flat = jnp.concatenate([dct[n].reshape(-1) for n in small_names])
        return jnp.pad(flat, (0, rows * 128 - tot), constant_values=1.0).reshape(rows, 128)

    dl, nm, nv = adamw(pack(weights), pack(grads), pack(m_in), pack(v_in))
    off = 0
    for n, sz in zip(small_names, sizes):
        shp = weights[n].shape
        take = lambda a: a.reshape(-1)[off:off + sz].reshape(shp)
        deltas[n], new_m[n], new_v[n] = take(dl), take(nm), take(nv)
        off += sz

    return (loss, grad_x[None], *[grads[n] for n in names], *[deltas[n] for n in names],
            *[new_m[n] for n in names], *[new_v[n] for n in names])
```
